```python
import jax, jax.numpy as jnp
from jax import lax
import numpy as np

D_MODEL = 2048
BATCH = 8
SEQ = 2048
DEPTH = 1

CHUNK = 64
HG_HEADS = 8
HG_DK = 128
HG_DV = 128
HG_WIDTH = HG_HEADS * HG_DV
AT_HEADS = 16
AT_DH = 64
AT_WIDTH = AT_HEADS * AT_DH
LEFT_CHUNKS = 8
BAND = (LEFT_CHUNKS + 1) * CHUNK
REL_CLIP = 256
N_REL = 2 * REL_CLIP + 1
D_FF = 4 * D_MODEL
N_BRANCH = 2
EPS = 1e-6
D_IN = 4 * HG_WIDTH + 3 * AT_WIDTH + N_BRANCH * D_MODEL
SPLIT_POINTS = (HG_WIDTH, 2 * HG_WIDTH, 3 * HG_WIDTH, 4 * HG_WIDTH,
                4 * HG_WIDTH + AT_WIDTH, 4 * HG_WIDTH + 2 * AT_WIDTH, 4 * HG_WIDTH + 3 * AT_WIDTH)

kernel_name = "hybrid_hgrn2_chunkattn_gated_block"


def rms_norm(x, w):
    xf = x.astype(jnp.float32)
    y = xf * lax.rsqrt(jnp.mean(xf * xf, axis=-1, keepdims=True) + EPS)
    return (y * w.astype(jnp.float32)).astype(x.dtype)


def hgrn2_scan(q, k, v, log_g):
    B, H, T, DK = q.shape
    DV = v.shape[-1]
    nc = T // CHUNK

    def to_chunks(a):
        return a.reshape(B, H, nc, CHUNK, a.shape[-1]).transpose(2, 0, 1, 3, 4)

    causal = jnp.tril(jnp.ones((CHUNK, CHUNK), dtype=bool))

    def step(S, inp):
        qc, kc, vc, gc = inp
        b = jnp.cumsum(gc, axis=2)
        o_inter = jnp.einsum('bhtk,bhkv->bhtv', qc * jnp.exp(b), S)
        diff = b[:, :, :, None, :] - b[:, :, None, :, :]
        decay = jnp.exp(jnp.where(causal[:, :, None], diff, -jnp.inf))
        scores = jnp.einsum('bhtk,bhtsk,bhsk->bhts', qc, decay, kc)
        o_intra = jnp.einsum('bhts,bhsv->bhtv', scores, vc)
        b_last = b[:, :, -1:, :]
        S_new = jnp.exp(b_last[:, :, 0, :])[..., None] * S + jnp.einsum(
            'bhsk,bhsv->bhkv', kc * jnp.exp(b_last - b), vc)
        return S_new, o_inter + o_intra

    S0 = jnp.zeros((B, H, DK, DV), jnp.float32)
    _, o = lax.scan(step, S0, (to_chunks(q), to_chunks(k), to_chunks(v), to_chunks(log_g)))
    return o.transpose(1, 2, 0, 3, 4).reshape(B, H, T, DV)


def chunk_band_attention(q, k, v, rel_bias):
    B, H, T, Dh = q.shape
    nc = T // CHUNK
    pad = LEFT_CHUNKS * CHUNK
    kp = jnp.pad(k, ((0, 0), (0, 0), (pad, 0), (0, 0)))
    vp = jnp.pad(v, ((0, 0), (0, 0), (pad, 0), (0, 0)))
    idx = (jnp.arange(nc) * CHUNK)[:, None] + jnp.arange(BAND)[None, :]
    kb = kp[:, :, idx, :]
    vb = vp[:, :, idx, :]
    qc = q.reshape(B, H, nc, CHUNK, Dh)
    valid = idx >= pad
    t = jnp.arange(CHUNK)
    j = jnp.arange(BAND)
    rel = t[:, None] + pad - j[None, :]
    rel_idx = jnp.clip(rel, -REL_CLIP, REL_CLIP) + REL_CLIP
    bias = rel_bias[:, rel_idx].astype(jnp.float32)
    s = jnp.einsum('bhnqd,bhnkd->bhnqk', qc, kb).astype(jnp.float32) * (Dh ** -0.5) + bias[:, None]
    s = jnp.where(valid[:, None, :], s, -jnp.inf)
    p = jax.nn.softmax(s, axis=-1).astype(v.dtype)
    o = jnp.einsum('bhnqk,bhnkd->bhnqd', p, vb)
    return o.reshape(B, H, T, Dh)


def mixer_block(u, w_in, lb, hg_norm_w, rel_bias, w_branch_a, w_branch_b, w_out):
    B, T, _ = u.shape
    z = u @ w_in
    hq, hf, hi, hg, aq, ak, av, gates = jnp.split(z, SPLIT_POINTS, axis=-1)

    def to_heads(a, h):
        return a.reshape(B, T, h, -1).transpose(0, 2, 1, 3)

    f = jax.nn.sigmoid(hf.astype(jnp.float32))
    g = lb + (1.0 - lb) * f
    log_g = jnp.log(g)
    kk = 1.0 - g
    q = jax.nn.silu(hq.astype(jnp.float32)) * (HG_DK ** -0.5)
    o = hgrn2_scan(to_heads(q, HG_HEADS), to_heads(kk, HG_HEADS),
                   to_heads(hi.astype(jnp.float32), HG_HEADS), to_heads(log_g, HG_HEADS))
    o = o.transpose(0, 2, 1, 3)
    o = rms_norm(o, hg_norm_w) * jax.nn.silu(hg.reshape(B, T, HG_HEADS, HG_DV).astype(jnp.float32))
    y_a = o.reshape(B, T, HG_WIDTH).astype(u.dtype)

    ya = chunk_band_attention(to_heads(aq, AT_HEADS), to_heads(ak, AT_HEADS),
                              to_heads(av, AT_HEADS), rel_bias)
    y_b = ya.transpose(0, 2, 1, 3).reshape(B, T, AT_WIDTH)

    gate_a, gate_b = jnp.split(jax.nn.sigmoid(gates), 2, axis=-1)
    merged = gate_a * (y_a @ w_branch_a) + gate_b * (y_b @ w_branch_b)
    return merged @ w_out


def _fwd_setup_inputs(seed: int = 0) -> dict:
    key = jax.random.key(seed)
    ks = jax.random.split(key, 13)
    f32 = jnp.float32
    x = jax.random.normal(ks[0], (BATCH, SEQ, D_MODEL), f32)
    w_in = jax.random.normal(ks[1], (DEPTH, D_MODEL, D_IN), f32) * D_MODEL ** -0.5
    lb_logits = jax.random.normal(ks[2], (DEPTH + 1, HG_WIDTH), f32)
    hg_norm_w = 1.0 + 0.02 * jax.random.normal(ks[3], (DEPTH, HG_DV), f32)
    rel_bias = 0.1 * jax.random.normal(ks[4], (DEPTH, AT_HEADS, N_REL), f32)
    w_branch_a = jax.random.normal(ks[5], (DEPTH, HG_WIDTH, D_MODEL), f32) * HG_WIDTH ** -0.5
    w_branch_b = jax.random.normal(ks[6], (DEPTH, AT_WIDTH, D_MODEL), f32) * AT_WIDTH ** -0.5
    w_out = jax.random.normal(ks[7], (DEPTH, D_MODEL, D_MODEL), f32) * D_MODEL ** -0.5
    norm_mix_w = 1.0 + 0.02 * jax.random.normal(ks[8], (DEPTH, D_MODEL), f32)
    norm_mlp_w = 1.0 + 0.02 * jax.random.normal(ks[9], (DEPTH, D_MODEL), f32)
    w_up = jax.random.normal(ks[10], (DEPTH, D_MODEL, D_FF), f32) * D_MODEL ** -0.5
    w_down = jax.random.normal(ks[11], (DEPTH, D_FF, D_MODEL), f32) * D_FF ** -0.5
    norm_final_w = 1.0 + 0.02 * jax.random.normal(ks[12], (D_MODEL,), f32)
    return {"x": x, "w_in": w_in, "lb_logits": lb_logits, "hg_norm_w": hg_norm_w,
            "rel_bias": rel_bias, "w_branch_a": w_branch_a, "w_branch_b": w_branch_b,
            "w_out": w_out, "norm_mix_w": norm_mix_w, "norm_mlp_w": norm_mlp_w,
            "w_up": w_up, "w_down": w_down, "norm_final_w": norm_final_w}


def _fwd_reference(x, w_in, lb_logits, hg_norm_w, rel_bias, w_branch_a, w_branch_b, w_out,
              norm_mix_w, norm_mlp_w, w_up, w_down, norm_final_w):
    lb_all = jnp.cumsum(jax.nn.softmax(lb_logits.astype(jnp.float32), axis=0), axis=0)
    h = x
    for l in range(DEPTH):
        h = h + mixer_block(rms_norm(h, norm_mix_w[l]), w_in[l], lb_all[l], hg_norm_w[l],
                            rel_bias[l], w_branch_a[l], w_branch_b[l], w_out[l])
        u = rms_norm(h, norm_mlp_w[l])
        h = h + jnp.square(jax.nn.relu(u @ w_up[l])) @ w_down[l]
    return rms_norm(h, norm_final_w)


import jax as _jax
import jax.numpy as _jnp

TWIN_FORMAT = 'train_step'
FWD_PARAMS = ['x', 'w_in', 'lb_logits', 'hg_norm_w', 'rel_bias', 'w_branch_a', 'w_branch_b', 'w_out', 'norm_mix_w', 'norm_mlp_w', 'w_up', 'w_down', 'norm_final_w']
TWIN_WEIGHTS = ['w_in', 'lb_logits', 'hg_norm_w', 'rel_bias', 'w_branch_a', 'w_branch_b', 'w_out', 'norm_mix_w', 'norm_mlp_w', 'w_up', 'w_down', 'norm_final_w']
TWIN_DIFF_INPUT = 'x'
TWIN_INPUTS = ['x', 'w_in', 'lb_logits', 'hg_norm_w', 'rel_bias', 'w_branch_a', 'w_branch_b', 'w_out', 'norm_mix_w', 'norm_mlp_w', 'w_up', 'w_down', 'norm_final_w', 'loss_target', 'm_w_in', 'm_lb_logits', 'm_hg_norm_w', 'm_rel_bias', 'm_w_branch_a', 'm_w_branch_b', 'm_w_out', 'm_norm_mix_w', 'm_norm_mlp_w', 'm_w_up', 'm_w_down', 'm_norm_final_w', 'v_w_in', 'v_lb_logits', 'v_hg_norm_w', 'v_rel_bias', 'v_w_branch_a', 'v_w_branch_b', 'v_w_out', 'v_norm_mix_w', 'v_norm_mlp_w', 'v_w_up', 'v_w_down', 'v_norm_final_w']
TWIN_OUTPUTS = ['loss', 'grad_x', 'grad_w_in', 'grad_lb_logits', 'grad_hg_norm_w', 'grad_rel_bias', 'grad_w_branch_a', 'grad_w_branch_b', 'grad_w_out', 'grad_norm_mix_w', 'grad_norm_mlp_w', 'grad_w_up', 'grad_w_down', 'grad_norm_final_w', 'delta_w_in', 'delta_lb_logits', 'delta_hg_norm_w', 'delta_rel_bias', 'delta_w_branch_a', 'delta_w_branch_b', 'delta_w_out', 'delta_norm_mix_w', 'delta_norm_mlp_w', 'delta_w_up', 'delta_w_down', 'delta_norm_final_w', 'new_m_w_in', 'new_m_lb_logits', 'new_m_hg_norm_w', 'new_m_rel_bias', 'new_m_w_branch_a', 'new_m_w_branch_b', 'new_m_w_out', 'new_m_norm_mix_w', 'new_m_norm_mlp_w', 'new_m_w_up', 'new_m_w_down', 'new_m_norm_final_w', 'new_v_w_in', 'new_v_lb_logits', 'new_v_hg_norm_w', 'new_v_rel_bias', 'new_v_w_branch_a', 'new_v_w_branch_b', 'new_v_w_out', 'new_v_norm_mix_w', 'new_v_norm_mlp_w', 'new_v_w_up', 'new_v_w_down', 'new_v_norm_final_w']
TWIN_LEAF_KINDS = {'loss': 'loss', 'grad_x': 'grad_x', 'grad_w_in': 'grad_w', 'grad_lb_logits': 'grad_w', 'grad_hg_norm_w': 'grad_w', 'grad_rel_bias': 'grad_w', 'grad_w_branch_a': 'grad_w', 'grad_w_branch_b': 'grad_w', 'grad_w_out': 'grad_w', 'grad_norm_mix_w': 'grad_w', 'grad_norm_mlp_w': 'grad_w', 'grad_w_up': 'grad_w', 'grad_w_down': 'grad_w', 'grad_norm_final_w': 'grad_w', 'delta_w_in': 'delta_w', 'delta_lb_logits': 'delta_w', 'delta_hg_norm_w': 'delta_w', 'delta_rel_bias': 'delta_w', 'delta_w_branch_a': 'delta_w', 'delta_w_branch_b': 'delta_w', 'delta_w_out': 'delta_w', 'delta_norm_mix_w': 'delta_w', 'delta_norm_mlp_w': 'delta_w', 'delta_w_up': 'delta_w', 'delta_w_down': 'delta_w', 'delta_norm_final_w': 'delta_w', 'new_m_w_in': 'new_m', 'new_m_lb_logits': 'new_m', 'new_m_hg_norm_w': 'new_m', 'new_m_rel_bias': 'new_m', 'new_m_w_branch_a': 'new_m', 'new_m_w_branch_b': 'new_m', 'new_m_w_out': 'new_m', 'new_m_norm_mix_w': 'new_m', 'new_m_norm_mlp_w': 'new_m', 'new_m_w_up': 'new_m', 'new_m_w_down': 'new_m', 'new_m_norm_final_w': 'new_m', 'new_v_w_in': 'new_v', 'new_v_lb_logits': 'new_v', 'new_v_hg_norm_w': 'new_v', 'new_v_rel_bias': 'new_v', 'new_v_w_branch_a': 'new_v', 'new_v_w_branch_b': 'new_v', 'new_v_w_out': 'new_v', 'new_v_norm_mix_w': 'new_v', 'new_v_norm_mlp_w': 'new_v', 'new_v_w_up': 'new_v', 'new_v_w_down': 'new_v', 'new_v_norm_final_w': 'new_v'}


def _forward(args):
    return _fwd_reference(*[args[k] for k in FWD_PARAMS])


def _output_shape():
    out = _jax.eval_shape(lambda: _forward(_fwd_setup_inputs(0)))
    return out.shape, out.dtype

N_MICROBATCH = 1
ADAM_LR = 0.001
ADAM_B1 = 0.9
ADAM_B2 = 0.999
ADAM_EPS = 1e-08
ADAM_WD = 0.01
ADAM_STEP = 10
PER_EXAMPLE_BATCH_AXIS = {'x': 0, 'loss_target': 0}
SHARED_INPUTS = []
_WEIGHT_DTYPES = {'w_in': _jnp.float32, 'lb_logits': _jnp.float32, 'hg_norm_w': _jnp.float32, 'rel_bias': _jnp.float32, 'w_branch_a': _jnp.float32, 'w_branch_b': _jnp.float32, 'w_out': _jnp.float32, 'norm_mix_w': _jnp.float32, 'norm_mlp_w': _jnp.float32, 'w_up': _jnp.float32, 'w_down': _jnp.float32, 'norm_final_w': _jnp.float32}
MOMENT_SCALE = {'w_in': 1.399531e-02, 'lb_logits': 2.265415e-03, 'hg_norm_w': 9.367881e-02, 'rel_bias': 2.482857e-03, 'w_branch_a': 2.092729e-02, 'w_branch_b': 5.187483e-03, 'w_out': 2.151684e-02, 'norm_mix_w': 3.261116e-02, 'norm_mlp_w': 5.474841e-02, 'w_up': 2.741104e-02, 'w_down': 5.176197e-02, 'norm_final_w': 8.075568e+00}


def _to_microbatches(a, axis):
    t = _jnp.moveaxis(a, axis, 0)
    t = t.reshape((N_MICROBATCH, t.shape[0] // N_MICROBATCH) + t.shape[1:])
    return _jnp.moveaxis(t, 1, axis + 1)


def setup_inputs(seed: int = 0) -> dict:
    inp = _fwd_setup_inputs(seed)
    key = _jax.random.fold_in(_jax.random.key(seed), 7919)
    shape, _ = _output_shape()
    out = dict(inp)
    out["loss_target"] = _jax.random.normal(_jax.random.fold_in(key, 0), shape, _jnp.float32)
    for i, name in enumerate(TWIN_WEIGHTS):
        w = inp[name].astype(_jnp.float32)
        if MOMENT_SCALE is None:
            s = _jnp.sqrt(_jnp.mean(_jnp.square(w)) + 1e-30)
        else:
            s = MOMENT_SCALE[name]
        km, kv = _jax.random.split(_jax.random.fold_in(key, i + 1))
        out[name] = w
        out["m_" + name] = s * _jax.random.normal(km, w.shape, _jnp.float32)
        out["v_" + name] = (s * s) * _jax.random.uniform(kv, w.shape, _jnp.float32, 0.5, 1.5)
    if N_MICROBATCH > 1:
        for name, axis in PER_EXAMPLE_BATCH_AXIS.items():
            out[name] = _to_microbatches(out[name], axis)
    return {'x': out['x'], 'w_in': out['w_in'], 'lb_logits': out['lb_logits'], 'hg_norm_w': out['hg_norm_w'], 'rel_bias': out['rel_bias'], 'w_branch_a': out['w_branch_a'], 'w_branch_b': out['w_branch_b'], 'w_out': out['w_out'], 'norm_mix_w': out['norm_mix_w'], 'norm_mlp_w': out['norm_mlp_w'], 'w_up': out['w_up'], 'w_down': out['w_down'], 'norm_final_w': out['norm_final_w'], 'loss_target': out['loss_target'], 'm_w_in': out['m_w_in'], 'm_lb_logits': out['m_lb_logits'], 'm_hg_norm_w': out['m_hg_norm_w'], 'm_rel_bias': out['m_rel_bias'], 'm_w_branch_a': out['m_w_branch_a'], 'm_w_branch_b': out['m_w_branch_b'], 'm_w_out': out['m_w_out'], 'm_norm_mix_w': out['m_norm_mix_w'], 'm_norm_mlp_w': out['m_norm_mlp_w'], 'm_w_up': out['m_w_up'], 'm_w_down': out['m_w_down'], 'm_norm_final_w': out['m_norm_final_w'], 'v_w_in': out['v_w_in'], 'v_lb_logits': out['v_lb_logits'], 'v_hg_norm_w': out['v_hg_norm_w'], 'v_rel_bias': out['v_rel_bias'], 'v_w_branch_a': out['v_w_branch_a'], 'v_w_branch_b': out['v_w_branch_b'], 'v_w_out': out['v_w_out'], 'v_norm_mix_w': out['v_norm_mix_w'], 'v_norm_mlp_w': out['v_norm_mlp_w'], 'v_w_up': out['v_w_up'], 'v_w_down': out['v_w_down'], 'v_norm_final_w': out['v_norm_final_w']}


def _loss(weights, diff, rest, loss_target):
    with _jax.named_scope("forward"):
        args = {**rest, TWIN_DIFF_INPUT: diff, **{k: w.astype(_WEIGHT_DTYPES[k]) for k, w in weights.items()}}
        y = _forward(args)
    with _jax.named_scope("loss_head"):
        err = _jnp.square(y.astype(_jnp.float32) - loss_target)
        return 0.5 * _jnp.sum(_jnp.mean(err, axis=-1)) if err.ndim else 0.5 * err


def _adamw(w, g, m, v):
    m = ADAM_B1 * m + (1.0 - ADAM_B1) * g
    v = ADAM_B2 * v + (1.0 - ADAM_B2) * _jnp.square(g)
    m_hat = m / (1.0 - ADAM_B1 ** ADAM_STEP)
    v_hat = v / (1.0 - ADAM_B2 ** ADAM_STEP)
    delta = -ADAM_LR * (m_hat / (_jnp.sqrt(v_hat) + ADAM_EPS) + ADAM_WD * w)
    return delta, m, v


def reference(x, w_in, lb_logits, hg_norm_w, rel_bias, w_branch_a, w_branch_b, w_out, norm_mix_w, norm_mlp_w, w_up, w_down, norm_final_w, loss_target, m_w_in, m_lb_logits, m_hg_norm_w, m_rel_bias, m_w_branch_a, m_w_branch_b, m_w_out, m_norm_mix_w, m_norm_mlp_w, m_w_up, m_w_down, m_norm_final_w, v_w_in, v_lb_logits, v_hg_norm_w, v_rel_bias, v_w_branch_a, v_w_branch_b, v_w_out, v_norm_mix_w, v_norm_mlp_w, v_w_up, v_w_down, v_norm_final_w):
    given = dict(x=x, w_in=w_in, lb_logits=lb_logits, hg_norm_w=hg_norm_w, rel_bias=rel_bias, w_branch_a=w_branch_a, w_branch_b=w_branch_b, w_out=w_out, norm_mix_w=norm_mix_w, norm_mlp_w=norm_mlp_w, w_up=w_up, w_down=w_down, norm_final_w=norm_final_w, loss_target=loss_target, m_w_in=m_w_in, m_lb_logits=m_lb_logits, m_hg_norm_w=m_hg_norm_w, m_rel_bias=m_rel_bias, m_w_branch_a=m_w_branch_a, m_w_branch_b=m_w_branch_b, m_w_out=m_w_out, m_norm_mix_w=m_norm_mix_w, m_norm_mlp_w=m_norm_mlp_w, m_w_up=m_w_up, m_w_down=m_w_down, m_norm_final_w=m_norm_final_w, v_w_in=v_w_in, v_lb_logits=v_lb_logits, v_hg_norm_w=v_hg_norm_w, v_rel_bias=v_rel_bias, v_w_branch_a=v_w_branch_a, v_w_branch_b=v_w_branch_b, v_w_out=v_w_out, v_norm_mix_w=v_norm_mix_w, v_norm_mlp_w=v_norm_mlp_w, v_w_up=v_w_up, v_w_down=v_w_down, v_norm_final_w=v_norm_final_w)
    weights = {n: given[n] for n in TWIN_WEIGHTS}
    shared = {n: given[n] for n in SHARED_INPUTS}
    per_example = {n: given[n] for n in ['x']}
    grad_fn = _jax.value_and_grad(_loss, argnums=(0, 1))

    def one_microbatch(ex, loss_target):
        ex = dict(ex)
        diff = ex.pop(TWIN_DIFF_INPUT)
        return grad_fn(weights, diff, {**shared, **ex}, loss_target)

    if N_MICROBATCH == 1:
        loss, (grad_w, grad_x) = one_microbatch(per_example, given["loss_target"])
    else:
        def body(carry, xs):
            loss_sum, grad_sum = carry
            l_k, (gw_k, gx_k) = one_microbatch(xs[0], xs[1])
            with _jax.named_scope("update"):
                return (loss_sum + l_k, _jax.tree.map(_jnp.add, grad_sum, gw_k)), gx_k

        init = (_jnp.zeros((), _jnp.float32), _jax.tree.map(_jnp.zeros_like, weights))
        (loss, grad_w), grad_x = _jax.lax.scan(body, init, (per_example, given["loss_target"]))
    with _jax.named_scope("update"):
        delta_w, new_m, new_v = {}, {}, {}
        for n in TWIN_WEIGHTS:
            delta_w[n], new_m[n], new_v[n] = _adamw(weights[n], grad_w[n], given["m_" + n], given["v_" + n])
    return (loss, grad_x, *[grad_w[n] for n in TWIN_WEIGHTS], *[delta_w[n] for n in TWIN_WEIGHTS],
            *[new_m[n] for n in TWIN_WEIGHTS], *[new_v[n] for n in TWIN_WEIGHTS])
```

```python
import functools

import jax
import jax.numpy as jnp
from jax import lax
from jax.experimental import pallas as pl
from jax.experimental.pallas import tpu as pltpu

F32 = jnp.float32
BF16 = jnp.bfloat16
HIGHEST = lax.Precision.HIGHEST

D_MODEL = 2048
SEQ = 2048
CHUNK = 64
HG_HEADS = 8
HG_D = 128
AT_HEADS = 16
AT_DH = 64
LEFT = 8
REL_CLIP = 256
D_FF = 8192
EPS = 1e-6
ADAM_LR = 0.001
ADAM_B1 = 0.9
ADAM_B2 = 0.999
ADAM_EPS = 1e-08
ADAM_WD = 0.01
ADAM_STEP = 10

LANE = 128
NEG = -1e30
EXP_CLAMP = 80.0
VMEM_LIMIT = 48 * 1024 * 1024
MM_TM, MM_TN, MM_TK = 1024, 1024, 512
ROW_TILE = 256
QB = 2 * CHUNK


def _hgw():
    return HG_HEADS * HG_D


def _atw():
    return AT_HEADS * AT_DH


def _cparams(sem):
    return pltpu.CompilerParams(dimension_semantics=sem, vmem_limit_bytes=VMEM_LIMIT)


def _sigmoid(x):
    return jax.nn.sigmoid(x)


def _dot(a, b, dims, precision=None):
    return lax.dot_general(a, b, (dims, ((), ())), preferred_element_type=F32, precision=precision)


def _nn(a, b, precision=None):
    return _dot(a, b, ((1,), (0,)), precision)


def _nt(a, b, precision=None):
    return _dot(a, b, ((1,), (1,)), precision)


def _tn(a, b, precision=None):
    return _dot(a, b, ((0,), (0,)), precision)


def _mm(name, a, b, mode, out_dtypes, extras=(), epilogue=None):
    if mode == "nn":
        (M, K), (K2, N) = a.shape, b.shape
    elif mode == "nt":
        (M, K), (N, K2) = a.shape, b.shape
    else:
        (K, M), (K2, N) = a.shape, b.shape
    assert K == K2, (name, a.shape, b.shape)
    tm, tn, tk = min(MM_TM, M), min(MM_TN, N), min(MM_TK, K)
    assert M % tm == 0 and N % tn == 0 and K % tk == 0, (name, M, N, K)
    nk = K // tk
    ne, no = len(extras), len(out_dtypes)
    if epilogue is None:
        epilogue = lambda acc: (acc,)

    def body(a_ref, b_ref, *rest):
        extra_refs, out_refs, acc_ref = rest[:ne], rest[ne:ne + no], rest[ne + no]
        k = pl.program_id(2)

        @pl.when(k == 0)
        def _():
            acc_ref[...] = jnp.zeros_like(acc_ref)

        av, bv = a_ref[...].astype(BF16), b_ref[...].astype(BF16)
        if mode == "nn":
            acc_ref[...] += _nn(av, bv)
        elif mode == "nt":
            acc_ref[...] += _nt(av, bv)
        else:
            acc_ref[...] += _tn(av, bv)

        @pl.when(k == nk - 1)
        def _():
            res = epilogue(acc_ref[...], *[e[...] for e in extra_refs])
            for o_ref, r in zip(out_refs, res):
                o_ref[...] = r.astype(o_ref.dtype)

    if mode == "nn":
        a_spec = pl.BlockSpec((tm, tk), lambda i, j, k: (i, k))
        b_spec = pl.BlockSpec((tk, tn), lambda i, j, k: (k, j))
    elif mode == "nt":
        a_spec = pl.BlockSpec((tm, tk), lambda i, j, k: (i, k))
        b_spec = pl.BlockSpec((tn, tk), lambda i, j, k: (j, k))
    else:
        a_spec = pl.BlockSpec((tk, tm), lambda i, j, k: (k, i))
        b_spec = pl.BlockSpec((tk, tn), lambda i, j, k: (k, j))
    o_spec = pl.BlockSpec((tm, tn), lambda i, j, k: (i, j))
    outs = pl.pallas_call(
        body, name=name,
        grid=(M // tm, N // tn, nk),
        in_specs=[a_spec, b_spec] + [o_spec] * ne,
        out_specs=[o_spec] * no,
        out_shape=[jax.ShapeDtypeStruct((M, N), dt) for dt in out_dtypes],
        scratch_shapes=[pltpu.VMEM((tm, tn), F32)],
        compiler_params=_cparams(("parallel", "parallel", "arbitrary")),
    )(a, b, *extras)
    return outs[0] if no == 1 else outs


def _row_spec(tr, d):
    return pl.BlockSpec((tr, d), lambda i: (i, 0))


def _vec_spec(d):
    return pl.BlockSpec((1, d), lambda i: (0, 0))


def _rms_fwd(name, x, w):
    T, D = x.shape
    tr = min(ROW_TILE, T)

    def body(x_ref, w_ref, o_ref):
        xf = x_ref[...]
        r = lax.rsqrt(jnp.mean(xf * xf, axis=-1, keepdims=True) + EPS)
        o_ref[...] = (xf * r * w_ref[...]).astype(BF16)

    return pl.pallas_call(
        body, name=name, grid=(T // tr,),
        in_specs=[_row_spec(tr, D), _vec_spec(D)], out_specs=_row_spec(tr, D),
        out_shape=jax.ShapeDtypeStruct((T, D), BF16),
        compiler_params=_cparams(("parallel",)),
    )(x, w)


def _rms_bwd(name, dy, h, w, dres):
    T, D = h.shape
    tr = min(ROW_TILE, T)

    def body(dy_ref, h_ref, w_ref, dres_ref, dh_ref, dhb_ref, dw_ref):
        @pl.when(pl.program_id(0) == 0)
        def _():
            dw_ref[...] = jnp.zeros_like(dw_ref)

        hf, dyv = h_ref[...], dy_ref[...]
        r = lax.rsqrt(jnp.mean(hf * hf, axis=-1, keepdims=True) + EPS)
        xhat = hf * r
        dw_ref[...] += jnp.sum(dyv * xhat, axis=0, keepdims=True)
        dxh = dyv * w_ref[...]
        dh = dres_ref[...] + r * (dxh - xhat * jnp.mean(dxh * xhat, axis=-1, keepdims=True))
        dh_ref[...] = dh
        dhb_ref[...] = dh.astype(BF16)

    return pl.pallas_call(
        body, name=name, grid=(T // tr,),
        in_specs=[_row_spec(tr, D), _row_spec(tr, D), _vec_spec(D), _row_spec(tr, D)],
        out_specs=[_row_spec(tr, D), _row_spec(tr, D), _vec_spec(D)],
        out_shape=[jax.ShapeDtypeStruct((T, D), F32), jax.ShapeDtypeStruct((T, D), BF16),
                   jax.ShapeDtypeStruct((1, D), F32)],
        compiler_params=_cparams(("arbitrary",)),
    )(dy, h, w, dres)


def _loss_head(h2, target, w):
    T, D = h2.shape
    tr = min(ROW_TILE, T)

    def body(h_ref, t_ref, w_ref, loss_ref, dh_ref, dhb_ref, dw_ref):
        @pl.when(pl.program_id(0) == 0)
        def _():
            dw_ref[...] = jnp.zeros_like(dw_ref)
            loss_ref[...] = jnp.zeros_like(loss_ref)

        hf, wv = h_ref[...], w_ref[...]
        r = lax.rsqrt(jnp.mean(hf * hf, axis=-1, keepdims=True) + EPS)
        xhat = hf * r
        diff = xhat * wv - t_ref[...]
        loss_ref[...] += 0.5 * jnp.sum(jnp.mean(diff * diff, axis=-1, keepdims=True))
        dyv = diff * (1.0 / D)
        dw_ref[...] += jnp.sum(dyv * xhat, axis=0, keepdims=True)
        dxh = dyv * wv
        dh = r * (dxh - xhat * jnp.mean(dxh * xhat, axis=-1, keepdims=True))
        dh_ref[...] = dh
        dhb_ref[...] = dh.astype(BF16)

    return pl.pallas_call(
        body, name="loss_head", grid=(T // tr,),
        in_specs=[_row_spec(tr, D), _row_spec(tr, D), _vec_spec(D)],
        out_specs=[_vec_spec(LANE), _row_spec(tr, D), _row_spec(tr, D), _vec_spec(D)],
        out_shape=[jax.ShapeDtypeStruct((1, LANE), F32), jax.ShapeDtypeStruct((T, D), F32),
                   jax.ShapeDtypeStruct((T, D), BF16), jax.ShapeDtypeStruct((1, D), F32)],
        compiler_params=_cparams(("arbitrary",)),
    )(h2, target, w)


def _gate_tiles(T, D):
    goff = 4 * _hgw() + 3 * _atw()
    tc = min(1024, D)
    assert goff % tc == 0 and D % tc == 0
    return min(ROW_TILE, T), tc, goff // tc, D // tc


def _merge(z, pa, pb):
    T, D = pa.shape
    tr, tc, g0, nd = _gate_tiles(T, D)

    def body(ga_ref, gb_ref, pa_ref, pb_ref, o_ref):
        o_ref[...] = (_sigmoid(ga_ref[...]) * pa_ref[...] + _sigmoid(gb_ref[...]) * pb_ref[...]).astype(BF16)

    t = pl.BlockSpec((tr, tc), lambda i, j: (i, j))
    return pl.pallas_call(
        body, name="merge", grid=(T // tr, nd),
        in_specs=[pl.BlockSpec((tr, tc), lambda i, j: (i, g0 + j)),
                  pl.BlockSpec((tr, tc), lambda i, j: (i, g0 + nd + j)), t, t],
        out_specs=t, out_shape=jax.ShapeDtypeStruct((T, D), BF16),
        compiler_params=_cparams(("parallel", "parallel")),
    )(z, z, pa, pb)


def _dmerge(dm, z, pa, pb):
    T, D = pa.shape
    tr, tc, g0, nd = _gate_tiles(T, D)

    def body(dm_ref, ga_ref, gb_ref, pa_ref, pb_ref, dpa_ref, dpb_ref, dga_ref, dgb_ref):
        dmv = dm_ref[...]
        sa, sb = _sigmoid(ga_ref[...]), _sigmoid(gb_ref[...])
        dpa_ref[...] = (dmv * sa).astype(BF16)
        dpb_ref[...] = (dmv * sb).astype(BF16)
        dga_ref[...] = (dmv * pa_ref[...] * sa * (1.0 - sa)).astype(BF16)
        dgb_ref[...] = (dmv * pb_ref[...] * sb * (1.0 - sb)).astype(BF16)

    t = pl.BlockSpec((tr, tc), lambda i, j: (i, j))
    return pl.pallas_call(
        body, name="dmerge", grid=(T // tr, nd),
        in_specs=[t, pl.BlockSpec((tr, tc), lambda i, j: (i, g0 + j)),
                  pl.BlockSpec((tr, tc), lambda i, j: (i, g0 + nd + j)), t, t],
        out_specs=[t, t, t, t],
        out_shape=[jax.ShapeDtypeStruct((T, D), BF16)] * 4,
        compiler_params=_cparams(("parallel", "parallel")),
    )(dm, z, z, pa, pb)


def _hg_gates(xq, xf, lb):
    f = _sigmoid(xf)
    g = lb + (1.0 - lb) * f
    sq = _sigmoid(xq)
    return f, g, jnp.log(g), 1.0 - g, sq, xq * sq * (HG_D ** -0.5)


def _hg_decays(lg, tri_incl, rowi):
    b = _nn(tri_incl, lg, precision=HIGHEST)
    b_last = jnp.sum(lg, axis=0, keepdims=True)
    b_mid = jnp.sum(jnp.where(rowi <= CHUNK // 2, lg, 0.0), axis=0, keepdims=True)
    return b, b_last, b_mid


def _hg_in_specs(T):
    H = HG_HEADS
    return [pl.BlockSpec((T, HG_D), lambda h, s=s: (0, s * H + h)) for s in range(4)]


def _hg_fwd(z, lb_logits, hgw):
    T = z.shape[0]
    H, d, C = HG_HEADS, HG_D, CHUNK
    nc = T // C

    def body(hq_ref, hf_ref, hi_ref, hg_ref, lbl_ref, w_ref, ya_ref, o_ref, s_ref):
        lb = 1.0 / (1.0 + jnp.exp(lbl_ref[1:2, :] - lbl_ref[0:1, :]))
        wv = w_ref[...]
        row = lax.broadcasted_iota(jnp.int32, (C, C), 0)
        col = lax.broadcasted_iota(jnp.int32, (C, C), 1)
        tril = col <= row
        tri_incl = tril.astype(F32)
        rowi = lax.broadcasted_iota(jnp.int32, (C, d), 0)

        def chunk(c, st):
            rows = pl.ds(pl.multiple_of(c * C, C), C)
            xq, xf, v, xg = hq_ref[rows, :], hf_ref[rows, :], hi_ref[rows, :], hg_ref[rows, :]
            _, _, lg, kk, _, q = _hg_gates(xq, xf, lb)
            b, b_last, b_mid = _hg_decays(lg, tri_incl, rowi)
            st_b = st.astype(BF16)
            s_ref[c] = st
            vb = v.astype(BF16)
            o = _nt((q * jnp.exp(b)).astype(BF16), st_b)
            qt = (q * jnp.exp(b - b_mid)).astype(BF16)
            kt = (kk * jnp.exp(jnp.minimum(b_mid - b, EXP_CLAMP))).astype(BF16)
            a = jnp.where(tril, _nt(qt, kt), 0.0).astype(BF16)
            o = o + _nn(a, vb)
            st_new = st * jnp.exp(b_last) + _tn(v, kk * jnp.exp(b_last - b), HIGHEST)
            o_ref[rows, :] = o
            r = lax.rsqrt(jnp.mean(o * o, axis=-1, keepdims=True) + EPS)
            ya_ref[rows, :] = (o * r * wv * (xg * _sigmoid(xg))).astype(BF16)
            return st_new

        lax.fori_loop(0, nc, chunk, jnp.zeros((d, d), F32))

    head = pl.BlockSpec((T, d), lambda h: (0, h))
    return pl.pallas_call(
        body, name="hg_fwd", grid=(H,),
        in_specs=_hg_in_specs(T) + [pl.BlockSpec((2, d), lambda h: (0, h)), pl.BlockSpec((1, d), lambda h: (0, 0))],
        out_specs=[head, head, pl.BlockSpec((None, nc, d, d), lambda h: (h, 0, 0, 0))],
        out_shape=[jax.ShapeDtypeStruct((T, H * d), BF16), jax.ShapeDtypeStruct((T, H * d), F32),
                   jax.ShapeDtypeStruct((H, nc, d, d), F32)],
        compiler_params=_cparams(("parallel",)),
    )(z, z, z, z, lb_logits, hgw)


def _hg_bwd(z, o, dya, states, lb_logits, hgw):
    T = z.shape[0]
    H, d, C = HG_HEADS, HG_D, CHUNK
    nc = T // C
    scale = HG_D ** -0.5

    def body(hq_ref, hf_ref, hi_ref, hg_ref, o_ref, dy_ref, s_ref, lbl_ref, w_ref,
             dq_ref, df_ref, di_ref, dg_ref, dlbl_ref, dw_ref, acc_ref):
        lb = 1.0 / (1.0 + jnp.exp(lbl_ref[1:2, :] - lbl_ref[0:1, :]))
        wv = w_ref[...]
        row = lax.broadcasted_iota(jnp.int32, (C, C), 0)
        col = lax.broadcasted_iota(jnp.int32, (C, C), 1)
        tril = col <= row
        tri_incl = tril.astype(F32)
        triu_incl = (col >= row).astype(F32)
        rowi = lax.broadcasted_iota(jnp.int32, (C, d), 0)
        acc_ref[...] = jnp.zeros_like(acc_ref)

        @pl.when(pl.program_id(0) == 0)
        def _():
            dw_ref[...] = jnp.zeros_like(dw_ref)

        def chunk(i, carry):
            dst, tail = carry
            c = nc - 1 - i
            rows = pl.ds(pl.multiple_of(c * C, C), C)
            xq, xf, v, xg = hq_ref[rows, :], hf_ref[rows, :], hi_ref[rows, :], hg_ref[rows, :]
            f, g, lg, kk, sq, q = _hg_gates(xq, xf, lb)
            b, b_last, b_mid = _hg_decays(lg, tri_incl, rowi)
            e_b, e_qm, e_km = jnp.exp(b), jnp.exp(b - b_mid), jnp.exp(jnp.minimum(b_mid - b, EXP_CLAMP))
            e_kl, e_last = jnp.exp(b_last - b), jnp.exp(b_last)
            ov, dy = o_ref[rows, :], dy_ref[rows, :]
            r = lax.rsqrt(jnp.mean(ov * ov, axis=-1, keepdims=True) + EPS)
            xhat = ov * r
            sg = _sigmoid(xg)
            dxg = dy * xhat * wv * (sg * (1.0 + xg * (1.0 - sg)))
            dyn = dy * (xg * sg)
            acc_ref[0:1, :] += jnp.sum(dyn * xhat, axis=0, keepdims=True)
            dxh = dyn * wv
            dof = r * (dxh - xhat * jnp.mean(dxh * xhat, axis=-1, keepdims=True))
            do, vb = dof.astype(BF16), v.astype(BF16)
            qt, kt = q * e_qm, kk * e_km
            pm = jnp.where(tril, _nt(do, vb), 0.0)
            am = jnp.where(tril, _nt(qt.astype(BF16), kt.astype(BF16)), 0.0).astype(BF16)
            dq = _nn(dof, s_ref[c], HIGHEST) * e_b + _nn(pm, kt, HIGHEST) * e_qm
            dk = _tn(pm, qt, HIGHEST) * e_km + _nn(v, dst, HIGHEST) * e_kl
            dv = _tn(am, do) + _nt((kk * e_kl).astype(BF16), dst.astype(BF16))
            dst_prev = dst * e_last + _tn(dof, q * e_b, HIGHEST)
            db = q * dq - kk * dk
            dlg = _nn(triu_incl, db, precision=HIGHEST) + tail
            dgate = dlg / g - dk
            acc_ref[1:2, :] += jnp.sum(dgate * (1.0 - f), axis=0, keepdims=True)
            dq_ref[rows, :] = (dq * scale * (sq * (1.0 + xq * (1.0 - sq)))).astype(BF16)
            df_ref[rows, :] = (dgate * (1.0 - lb) * f * (1.0 - f)).astype(BF16)
            di_ref[rows, :] = dv.astype(BF16)
            dg_ref[rows, :] = dxg.astype(BF16)
            return dst_prev, tail + jnp.sum(db, axis=0, keepdims=True)

        lax.fori_loop(0, nc, chunk, (jnp.zeros((d, d), F32), jnp.zeros((1, d), F32)))
        dw_ref[...] += acc_ref[0:1, :]
        dl0 = acc_ref[1:2, :] * lb * (1.0 - lb)
        dlbl_ref[0:1, :] = dl0
        dlbl_ref[1:2, :] = -dl0

    head = pl.BlockSpec((T, d), lambda h: (0, h))
    return pl.pallas_call(
        body, name="hg_bwd", grid=(H,),
        in_specs=_hg_in_specs(T) + [head, head, pl.BlockSpec((None, nc, d, d), lambda h: (h, 0, 0, 0)),
                                    pl.BlockSpec((2, d), lambda h: (0, h)), pl.BlockSpec((1, d), lambda h: (0, 0))],
        out_specs=[head, head, head, head, pl.BlockSpec((2, d), lambda h: (0, h)),
                   pl.BlockSpec((1, d), lambda h: (0, 0))],
        out_shape=[jax.ShapeDtypeStruct((T, H * d), BF16)] * 4 + [jax.ShapeDtypeStruct((2, H * d), F32),
                                                                   jax.ShapeDtypeStruct((1, d), F32)],
        scratch_shapes=[pltpu.VMEM((8, d), F32)],
        compiler_params=_cparams(("arbitrary",)),
    )(z, z, z, z, o, dya, states, lb_logits, hgw)


def _at_dims():
    pad = LEFT * CHUNK
    return pad, QB + pad, AT_HEADS * AT_DH // LANE, 4 * _hgw() // LANE


def _bias_window(rel_bias):
    pad, W, _, _ = _at_dims()
    t = jnp.arange(QB)[:, None]
    j = jnp.arange(W)[None, :]
    idx = jnp.clip(pad + t - j, -REL_CLIP, REL_CLIP) + REL_CLIP
    ok = (j // CHUNK >= t // CHUNK) & (j // CHUNK <= t // CHUNK + LEFT)
    return jnp.where(ok[None], rel_bias[:, idx], NEG)


def _bias_window_grad(dbw):
    pad, W, _, _ = _at_dims()
    H = dbw.shape[0]
    V = QB + W - 1
    flat = jnp.pad(jnp.flip(dbw, axis=-1), ((0, 0), (0, 0), (0, V + 1 - W))).reshape(H, QB * (V + 1))
    diag = flat[:, :QB * V].reshape(H, QB, V).sum(axis=1)
    rel = pad + jnp.arange(V) - (W - 1)
    onehot = (jnp.clip(rel, -REL_CLIP, REL_CLIP)[:, None] + REL_CLIP == jnp.arange(2 * REL_CLIP + 1)[None, :])
    return jnp.dot(diag, onehot.astype(F32), precision=HIGHEST)


def _at_softmax(q_half, kw, bias, valid):
    s = _nt(q_half, kw) * (AT_DH ** -0.5) + bias
    s = jnp.where(valid, s, NEG)
    e = jnp.exp(s - jnp.max(s, axis=-1, keepdims=True))
    return e / jnp.sum(e, axis=-1, keepdims=True)


def _at_fwd(z, bias_win):
    T = z.shape[0]
    pad, W, HP, c0 = _at_dims()
    nq = T // QB

    def body(q_ref, k_ref, v_ref, bias_ref, o_ref, kpad, vpad):
        qi = pl.program_id(1)

        @pl.when(qi == 0)
        def _():
            kpad[0:pad, :] = jnp.zeros((pad, LANE), BF16)
            vpad[0:pad, :] = jnp.zeros((pad, LANE), BF16)
            kpad[pad:, :] = k_ref[...].astype(BF16)
            vpad[pad:, :] = v_ref[...].astype(BF16)

        win = pl.ds(pl.multiple_of(qi * QB, QB), W)
        kw, vw = kpad[win, :], vpad[win, :]
        q = q_ref[...]
        lane = lax.broadcasted_iota(jnp.int32, (QB, LANE), 1)
        first = lane < AT_DH
        valid = lax.broadcasted_iota(jnp.int32, (QB, W), 1) + qi * QB >= pad
        pa = _at_softmax(jnp.where(first, q, 0.0).astype(BF16), kw, bias_ref[0], valid)
        pb = _at_softmax(jnp.where(first, 0.0, q).astype(BF16), kw, bias_ref[1], valid)
        o_ref[...] = jnp.where(first, _nn(pa.astype(BF16), vw), _nn(pb.astype(BF16), vw)).astype(BF16)

    full = lambda s: pl.BlockSpec((T, LANE), lambda hp, qi, s=s: (0, c0 + s * HP + hp))
    return pl.pallas_call(
        body, name="at_fwd", grid=(HP, nq),
        in_specs=[pl.BlockSpec((QB, LANE), lambda hp, qi: (qi, c0 + hp)), full(1), full(2),
                  pl.BlockSpec((2, QB, W), lambda hp, qi: (hp, 0, 0))],
        out_specs=pl.BlockSpec((QB, LANE), lambda hp, qi: (qi, hp)),
        out_shape=jax.ShapeDtypeStruct((T, HP * LANE), BF16),
        scratch_shapes=[pltpu.VMEM((T + pad, LANE), BF16)] * 2,
        compiler_params=_cparams(("parallel", "arbitrary")),
    )(z, z, z, bias_win)


def _at_bwd(z, dyb, bias_win):
    T = z.shape[0]
    pad, W, HP, c0 = _at_dims()
    nq = T // QB
    scale = AT_DH ** -0.5

    def body(q_ref, k_ref, v_ref, do_ref, bias_ref, dq_ref, dk_ref, dv_ref, dbias_ref, kpad, vpad, dkpad, dvpad):
        qi = pl.program_id(1)

        @pl.when(qi == 0)
        def _():
            kpad[0:pad, :] = jnp.zeros((pad, LANE), BF16)
            vpad[0:pad, :] = jnp.zeros((pad, LANE), BF16)
            kpad[pad:, :] = k_ref[...].astype(BF16)
            vpad[pad:, :] = v_ref[...].astype(BF16)
            dkpad[...] = jnp.zeros_like(dkpad)
            dvpad[...] = jnp.zeros_like(dvpad)
            dbias_ref[...] = jnp.zeros_like(dbias_ref)

        win = pl.ds(pl.multiple_of(qi * QB, QB), W)
        kw, vw = kpad[win, :], vpad[win, :]
        q, do = q_ref[...], do_ref[...]
        lane = lax.broadcasted_iota(jnp.int32, (QB, LANE), 1)
        first = lane < AT_DH
        valid = lax.broadcasted_iota(jnp.int32, (QB, W), 1) + qi * QB >= pad

        def half(hh, qh, doh):
            p = _at_softmax(qh, kw, bias_ref[hh], valid)
            dp = _nt(doh, vw)
            ds = p * (dp - jnp.sum(p * dp, axis=-1, keepdims=True))
            dbias_ref[hh] += ds
            dss = (ds * scale).astype(BF16)
            return _nn(dss, kw), _tn(dss, qh), _tn(p.astype(BF16), doh)

        dqa, dka, dva = half(0, jnp.where(first, q, 0.0).astype(BF16), jnp.where(first, do, 0.0).astype(BF16))
        dqb, dkb, dvb = half(1, jnp.where(first, 0.0, q).astype(BF16), jnp.where(first, 0.0, do).astype(BF16))
        dq_ref[...] = jnp.where(first, dqa, dqb).astype(BF16)
        dkpad[win, :] += dka + dkb
        dvpad[win, :] += dva + dvb

        @pl.when(qi == nq - 1)
        def _():
            dk_ref[...] = dkpad[pad:, :].astype(BF16)
            dv_ref[...] = dvpad[pad:, :].astype(BF16)

    full = lambda s: pl.BlockSpec((T, LANE), lambda hp, qi, s=s: (0, c0 + s * HP + hp))
    blk = pl.BlockSpec((QB, LANE), lambda hp, qi: (qi, hp))
    col = pl.BlockSpec((T, LANE), lambda hp, qi: (0, hp))
    bw = pl.BlockSpec((2, QB, W), lambda hp, qi: (hp, 0, 0))
    return pl.pallas_call(
        body, name="at_bwd", grid=(HP, nq),
        in_specs=[pl.BlockSpec((QB, LANE), lambda hp, qi: (qi, c0 + hp)), full(1), full(2), blk, bw],
        out_specs=[blk, col, col, bw],
        out_shape=[jax.ShapeDtypeStruct((T, HP * LANE), BF16)] * 3 + [jax.ShapeDtypeStruct(bias_win.shape, F32)],
        scratch_shapes=[pltpu.VMEM((T + pad, LANE), BF16)] * 2 + [pltpu.VMEM((T + pad, LANE), F32)] * 2,
        compiler_params=_cparams(("parallel", "arbitrary")),
    )(z, z, z, dyb, bias_win)


def _ew(name, fn, ins, out_dtypes):
    shape = ins[0].shape
    R, C = shape[-2], shape[-1]
    tr = min(ROW_TILE, R)
    assert R % tr == 0
    lead = shape[:-2]
    if lead:
        grid = (lead[0], R // tr)
        spec = pl.BlockSpec((None, tr, C), lambda a, i: (a, i, 0))
    else:
        grid = (R // tr,)
        spec = pl.BlockSpec((tr, C), lambda i: (i, 0))
    n_in = len(ins)

    def body(*refs):
        res = fn(*[r[...] for r in refs[:n_in]])
        for o_ref, r in zip(refs[n_in:], res):
            o_ref[...] = r.astype(o_ref.dtype)

    outs = pl.pallas_call(
        body, name=name, grid=grid, in_specs=[spec] * n_in, out_specs=[spec] * len(out_dtypes),
        out_shape=[jax.ShapeDtypeStruct(shape, dt) for dt in out_dtypes],
        compiler_params=_cparams(("parallel",) * len(grid)),
    )(*ins)
    return outs[0] if len(out_dtypes) == 1 else outs


def _piece_sum(name, own, got):
    R, C = own.shape
    tr = min(ROW_TILE, R)
    assert R % tr == 0

    def body(own_ref, got_ref, o_ref):
        o_ref[...] = (own_ref[...].astype(F32) + got_ref[0].astype(F32) + got_ref[1].astype(F32)
                      + got_ref[2].astype(F32))

    return pl.pallas_call(
        body, name=name, grid=(R // tr,),
        in_specs=[pl.BlockSpec((tr, C), lambda i: (i, 0)), pl.BlockSpec((3, tr, C), lambda i: (0, i, 0))],
        out_specs=pl.BlockSpec((tr, C), lambda i: (i, 0)),
        out_shape=jax.ShapeDtypeStruct((R, C), F32),
        compiler_params=_cparams(("parallel",)),
    )(own, got)


def _adam_math(w, g, m, v):
    m = ADAM_B1 * m + (1.0 - ADAM_B1) * g
    v = ADAM_B2 * v + (1.0 - ADAM_B2) * (g * g)
    m_hat = m / (1.0 - ADAM_B1 ** ADAM_STEP)
    v_hat = v / (1.0 - ADAM_B2 ** ADAM_STEP)
    return -ADAM_LR * (m_hat / (jnp.sqrt(v_hat) + ADAM_EPS) + ADAM_WD * w), m, v


WEIGHTS = ("w_in", "w_branch_a", "w_branch_b", "w_out", "w_up", "w_down")
ROW_SHARDED = ("w_out", "w_down")
ANY = pl.BlockSpec(memory_space=pl.ANY)
MESH = pl.DeviceIdType.MESH


def _place():
    x, y, c = lax.axis_index("x"), lax.axis_index("y"), lax.axis_index("c")
    chips = [(1 - x, y), (x, 1 - y), (1 - x, 1 - y)]
    return x, y, c, 2 * x + y, chips, [2 * cx + cy for cx, cy in chips]


def _piece(full_ref, name, q, half):
    K, N = full_ref.shape
    if name in ROW_SHARDED:
        rows = K // 8
        return full_ref.at[pl.ds(q * (2 * rows) + half * rows, rows), :]
    return full_ref.at[pl.ds(half * (K // 2), K // 2), pl.ds(q * (N // 4), N // 4)]


def _quarter(full_ref, name, q):
    K, N = full_ref.shape
    if name in ROW_SHARDED:
        return full_ref.at[pl.ds(q * (K // 4), K // 4), :]
    return full_ref.at[:, pl.ds(q * (N // 4), N // 4)]


def _piece_shape(name, full_shape):
    K, N = full_shape
    return (K // 8, N) if name in ROW_SHARDED else (K // 2, N // 4)


def _full_shape(name, quarter_shape):
    Kq, Nq = quarter_shape
    return (4 * Kq, Nq) if name in ROW_SHARDED else (Kq, 4 * Nq)


def _remote(src, dst, send_sem, recv_sem, device):
    return pltpu.make_async_remote_copy(src_ref=src, dst_ref=dst, send_sem=send_sem, recv_sem=recv_sem,
                                        device_id=device, device_id_type=MESH)


def _gather_weights(quarters):
    n = len(WEIGHTS)

    def body(*refs):
        q_refs, f_refs = refs[:n], refs[n:2 * n]
        send_sems, recv_sems, local_sems = refs[2 * n:]
        x, y, c, p, chips, chip_ids = _place()
        sib = (x, y, 1 - c)
        started = []
        for i, name in enumerate(WEIGHTS):
            own = pltpu.make_async_copy(q_refs[i], _quarter(f_refs[i], name, p), local_sems.at[i])
            own.start()
            started.append(own)
        sends = []
        for i, name in enumerate(WEIGHTS):
            rows = q_refs[i].shape[0] // 2
            mine = q_refs[i].at[pl.ds(c * rows, rows), :]
            for j, chip in enumerate(chips):
                cp = _remote(mine, _piece(f_refs[i], name, p, c), send_sems.at[i, j], recv_sems.at[i, j], (*chip, c))
                cp.start()
                sends.append(cp)
        for i, name in enumerate(WEIGHTS):
            for j, cid in enumerate(chip_ids):
                landed = _piece(f_refs[i], name, cid, c)
                _remote(landed, landed, send_sems.at[i, j], recv_sems.at[i, j], sib).wait_recv()
                cp = _remote(landed, landed, send_sems.at[i, 3 + j], recv_sems.at[i, 3 + j], sib)
                cp.start()
                sends.append(cp)
        for i, name in enumerate(WEIGHTS):
            for j, cid in enumerate(chip_ids):
                other = _piece(f_refs[i], name, cid, 1 - c)
                _remote(other, other, send_sems.at[i, 3 + j], recv_sems.at[i, 3 + j], sib).wait_recv()
        for cp in sends:
            cp.wait_send()
        for cp in started:
            cp.wait()

    return pl.pallas_call(
        body, name="gather_weights",
        in_specs=[ANY] * n, out_specs=[ANY] * n,
        out_shape=[jax.ShapeDtypeStruct(_full_shape(name, q.shape), BF16) for name, q in zip(WEIGHTS, quarters)],
        scratch_shapes=[pltpu.SemaphoreType.DMA((n, 6)), pltpu.SemaphoreType.DMA((n, 6)),
                        pltpu.SemaphoreType.DMA((n,))],
    )(*quarters)


def _split_to_sibling(grads):
    n = len(WEIGHTS)

    def body(*refs):
        g_refs, mine_refs, theirs_refs = refs[:n], refs[n:2 * n], refs[2 * n:3 * n]
        send_sems, recv_sems, local_sems = refs[3 * n:]
        x, y, c, _, _, _ = _place()
        sib = (x, y, 1 - c)
        copies = []
        for i, name in enumerate(WEIGHTS):
            for q in range(4):
                keep = pltpu.make_async_copy(_piece(g_refs[i], name, q, c), mine_refs[i].at[q], local_sems.at[i, q])
                give = _remote(_piece(g_refs[i], name, q, 1 - c), theirs_refs[i].at[q],
                               send_sems.at[i, q], recv_sems.at[i, q], sib)
                keep.start()
                give.start()
                copies.append((keep, give))
        for keep, give in copies:
            keep.wait()
            give.wait()

    shapes = [jax.ShapeDtypeStruct((4,) + _piece_shape(name, g.shape), BF16) for name, g in zip(WEIGHTS, grads)]
    outs = pl.pallas_call(
        body, name="split_to_sibling",
        in_specs=[ANY] * n, out_specs=[ANY] * (2 * n), out_shape=shapes + shapes,
        scratch_shapes=[pltpu.SemaphoreType.DMA((n, 4)), pltpu.SemaphoreType.DMA((n, 4)),
                        pltpu.SemaphoreType.DMA((n, 4))],
    )(*grads)
    return outs[:n], outs[n:]


def _exchange_chips(chip_sums):
    n = len(WEIGHTS)

    def body(*refs):
        s_refs, own_refs, got_refs = refs[:n], refs[n:2 * n], refs[2 * n:3 * n]
        send_sems, recv_sems, local_sems = refs[3 * n:]
        _, _, c, p, chips, chip_ids = _place()
        copies = []
        for i in range(n):
            keep = pltpu.make_async_copy(s_refs[i].at[p], own_refs[i], local_sems.at[i])
            keep.start()
            copies.append(keep)
            for j, (chip, cid) in enumerate(zip(chips, chip_ids)):
                cp = _remote(s_refs[i].at[cid], got_refs[i].at[j], send_sems.at[i, j], recv_sems.at[i, j], (*chip, c))
                cp.start()
                copies.append(cp)
        for cp in copies:
            cp.wait()

    own = [jax.ShapeDtypeStruct(s.shape[1:], BF16) for s in chip_sums]
    got = [jax.ShapeDtypeStruct((3,) + s.shape[1:], BF16) for s in chip_sums]
    outs = pl.pallas_call(
        body, name="exchange_chips",
        in_specs=[ANY] * n, out_specs=[ANY] * (2 * n), out_shape=own + got,
        scratch_shapes=[pltpu.SemaphoreType.DMA((n, 3)), pltpu.SemaphoreType.DMA((n, 3)),
                        pltpu.SemaphoreType.DMA((n,))],
    )(*chip_sums)
    return outs[:n], outs[n:]


def _share_with_sibling(halves):
    n = len(WEIGHTS)

    def body(*refs):
        h_refs, o_refs = refs[:n], refs[n:2 * n]
        send_sems, recv_sems, local_sems = refs[2 * n:]
        x, y, c, _, _, _ = _place()
        copies = []
        for i in range(n):
            keep = pltpu.make_async_copy(h_refs[i], o_refs[i].at[c], local_sems.at[i])
            give = _remote(h_refs[i], o_refs[i].at[c], send_sems.at[i], recv_sems.at[i], (x, y, 1 - c))
            keep.start()
            give.start()
            copies += [keep, give]
        for cp in copies:
            cp.wait()

    return pl.pallas_call(
        body, name="share_with_sibling",
        in_specs=[ANY] * n, out_specs=[ANY] * n,
        out_shape=[jax.ShapeDtypeStruct((2,) + h.shape, F32) for h in halves],
        scratch_shapes=[pltpu.SemaphoreType.DMA((n,)), pltpu.SemaphoreType.DMA((n,)), pltpu.SemaphoreType.DMA((n,))],
    )(*halves)


def _small_allreduce_adam(gpart, w, m, v):
    R = gpart.shape[0]

    def body(g_ref, w_ref, m_ref, v_ref, go_ref, d_ref, mo_ref, vo_ref, buf, send_sems, recv_sems):
        x, y, c = lax.axis_index("x"), lax.axis_index("y"), lax.axis_index("c")
        me = 4 * x + 2 * y + c
        buf[me] = g_ref[...]
        copies = []
        for k in range(1, 8):
            fx, fy, fc = (k >> 2) & 1, (k >> 1) & 1, k & 1
            peer = (1 - x if fx else x, 1 - y if fy else y, 1 - c if fc else c)
            cp = _remote(g_ref, buf.at[me], send_sems.at[k - 1], recv_sems.at[k - 1], peer)
            cp.start()
            copies.append((cp, 4 * peer[0] + 2 * peer[1] + peer[2]))
        for k, (cp, pid) in enumerate(copies):
            _remote(g_ref, buf.at[pid], send_sems.at[k], recv_sems.at[k], (x, y, c)).wait_recv()
        for cp, _ in copies:
            cp.wait_send()
        g = buf[0]
        for d in range(1, 8):
            g = g + buf[d]
        delta, mn, vn = _adam_math(w_ref[...], g, m_ref[...], v_ref[...])
        go_ref[...] = g
        d_ref[...] = delta
        mo_ref[...] = mn
        vo_ref[...] = vn

    vm = pl.BlockSpec(memory_space=pltpu.VMEM)
    return pl.pallas_call(
        body, name="small_allreduce_adam",
        in_specs=[vm] * 4, out_specs=[vm] * 4,
        out_shape=[jax.ShapeDtypeStruct((R, LANE), F32)] * 4,
        scratch_shapes=[pltpu.VMEM((8, R, LANE), F32), pltpu.SemaphoreType.DMA((7,)), pltpu.SemaphoreType.DMA((7,))],
    )(gpart, w, m, v)


def _pack(arrs):
    flat = jnp.concatenate([a.reshape(-1).astype(F32) for a in arrs])
    rows = -(-flat.shape[0] // (8 * LANE)) * 8
    return jnp.pad(flat, (0, rows * LANE - flat.shape[0])).reshape(rows, LANE)


def _unpack(packed, like):
    flat, out, off = packed.reshape(-1), [], 0
    for a in like:
        out.append(flat[off:off + a.size].reshape(a.shape))
        off += a.size
    return out


def kernel(x, w_in, lb_logits, hg_norm_w, rel_bias, w_branch_a, w_branch_b, w_out, norm_mix_w, norm_mlp_w, w_up, w_down, norm_final_w, loss_target, m_w_in, m_lb_logits, m_hg_norm_w, m_rel_bias, m_w_branch_a, m_w_branch_b, m_w_out, m_norm_mix_w, m_norm_mlp_w, m_w_up, m_w_down, m_norm_final_w, v_w_in, v_lb_logits, v_hg_norm_w, v_rel_bias, v_w_branch_a, v_w_branch_b, v_w_out, v_norm_mix_w, v_norm_mlp_w, v_w_up, v_w_down, v_norm_final_w):
    T, D = x.shape[1], x.shape[2]
    x2, tgt = x.reshape(T, D), loss_target.reshape(T, D)
    big = dict(w_in=(w_in, m_w_in, v_w_in), w_branch_a=(w_branch_a, m_w_branch_a, v_w_branch_a),
               w_branch_b=(w_branch_b, m_w_branch_b, v_w_branch_b), w_out=(w_out, m_w_out, v_w_out),
               w_up=(w_up, m_w_up, v_w_up), w_down=(w_down, m_w_down, v_w_down))
    big = {k: tuple(a[0] for a in v) for k, v in big.items()}
    nfw = norm_final_w.reshape(1, D)

    quarters = [_ew("cast_" + name, lambda a: (a,), [big[name][0]], [BF16]) for name in WEIGHTS]
    Wf = dict(zip(WEIGHTS, _gather_weights(quarters)))

    u1 = _rms_fwd("norm_mix", x2, norm_mix_w)
    z = _mm("z_proj", u1, Wf["w_in"], "nn", [F32])
    ya, o_hg, states = _hg_fwd(z, lb_logits, hg_norm_w)
    bias_win = _bias_window(rel_bias[0])
    yb = _at_fwd(z, bias_win)
    pa = _mm("branch_a", ya, Wf["w_branch_a"], "nn", [F32])
    pb = _mm("branch_b", yb, Wf["w_branch_b"], "nn", [F32])
    merged = _merge(z, pa, pb)
    add = lambda acc, res: (acc + res,)
    h1 = _mm("out_proj", merged, Wf["w_out"], "nn", [F32], extras=[x2], epilogue=add)
    u2 = _rms_fwd("norm_mlp", h1, norm_mlp_w)
    relu2 = lambda acc: (acc, jnp.square(jnp.maximum(acc, 0.0)))
    a_pre, act = _mm("mlp_up", u2, Wf["w_up"], "nn", [F32, BF16], epilogue=relu2)
    h2 = _mm("mlp_down", act, Wf["w_down"], "nn", [F32], extras=[h1], epilogue=add)
    loss_part, dh2, dh2b, d_nf = _loss_head(h2, tgt, nfw)

    drelu2 = lambda acc, a: (acc * (2.0 * jnp.maximum(a, 0.0)),)
    da = _mm("d_act", dh2b, Wf["w_down"], "nt", [BF16], extras=[a_pre], epilogue=drelu2)
    G = {}
    G["w_down"] = _mm("g_w_down", act, dh2b, "tn", [BF16])
    G["w_up"] = _mm("g_w_up", u2, da, "tn", [BF16])
    du2 = _mm("d_u2", da, Wf["w_up"], "nt", [F32])
    dh1, dh1b, d_nmlp = _rms_bwd("norm_mlp_bwd", du2, h1, norm_mlp_w, dh2)
    dmerged = _mm("d_merged", dh1b, Wf["w_out"], "nt", [F32])
    G["w_out"] = _mm("g_w_out", merged, dh1b, "tn", [BF16])
    dpa, dpb, dz_ga, dz_gb = _dmerge(dmerged, z, pa, pb)
    dya = _mm("d_ya", dpa, Wf["w_branch_a"], "nt", [F32])
    dyb = _mm("d_yb", dpb, Wf["w_branch_b"], "nt", [F32])
    G["w_branch_a"] = _mm("g_w_a", ya, dpa, "tn", [BF16])
    G["w_branch_b"] = _mm("g_w_b", yb, dpb, "tn", [BF16])
    dz_q, dz_f, dz_i, dz_g, d_lbl, d_hgw = _hg_bwd(z, o_hg, dya, states, lb_logits, hg_norm_w)
    dz_aq, dz_ak, dz_av, dbias_win = _at_bwd(z, dyb, bias_win)
    dz = jnp.concatenate([dz_q, dz_f, dz_i, dz_g, dz_aq, dz_ak, dz_av, dz_ga, dz_gb], axis=1)
    G["w_in"] = _mm("g_w_in", u1, dz, "tn", [BF16])
    du1 = _mm("d_u1", dz, Wf["w_in"], "nt", [F32])
    grad_x, _, d_nmix = _rms_bwd("norm_mix_bwd", du1, x2, norm_mix_w, dh1)
    d_rel = _bias_window_grad(dbias_win)

    mine, theirs = _split_to_sibling([G[name] for name in WEIGHTS])
    chip_sums = [_ew("chip_sum_" + name, lambda a, b: (a.astype(F32) + b.astype(F32),), [a, b], [BF16])
                 for name, a, b in zip(WEIGHTS, mine, theirs)]
    own, got = _exchange_chips(chip_sums)
    halves = [_piece_sum("piece_sum_" + name, o, g) for name, o, g in zip(WEIGHTS, own, got)]
    gq = _share_with_sibling(halves)
    big_out = {}
    for name, g in zip(WEIGHTS, gq):
        w, m, v = big[name]
        g2 = g.reshape(w.shape)
        delta, mn, vn = _ew("adam_" + name, _adam_math, [w, g2, m, v], [F32, F32, F32])
        big_out[name] = tuple(a[None] for a in (g2, delta, mn, vn))

    smalls = [("lb_logits", lb_logits, m_lb_logits, v_lb_logits, d_lbl),
              ("hg_norm_w", hg_norm_w, m_hg_norm_w, v_hg_norm_w, d_hgw),
              ("rel_bias", rel_bias, m_rel_bias, v_rel_bias, d_rel),
              ("norm_mix_w", norm_mix_w, m_norm_mix_w, v_norm_mix_w, d_nmix),
              ("norm_mlp_w", norm_mlp_w, m_norm_mlp_w, v_norm_mlp_w, d_nmlp),
              ("norm_final_w", norm_final_w, m_norm_final_w, v_norm_final_w, d_nf)]
    like = [s[1] for s in smalls]
    packed = _small_allreduce_adam(_pack([s[4] for s in smalls]), _pack(like), _pack([s[2] for s in smalls]),
                                   _pack([s[3] for s in smalls]))
    small_out = {s[0]: vals for s, vals in zip(smalls, zip(*[_unpack(p, like) for p in packed]))}

    loss = lax.psum(loss_part[0, 0], ("x", "y", "c"))
    order = ["w_in", "lb_logits", "hg_norm_w", "rel_bias", "w_branch_a", "w_branch_b", "w_out", "norm_mix_w",
             "norm_mlp_w", "w_up", "w_down", "norm_final_w"]
    res = {**big_out, **small_out}
    return (loss, grad_x.reshape(x.shape), *[res[n][0] for n in order], *[res[n][1] for n in order],
            *[res[n][2] for n in order], *[res[n][3] for n in order])
```

```python
import functools

import jax
import jax.numpy as jnp
from jax import lax
from jax.experimental import pallas as pl
from jax.experimental.pallas import tpu as pltpu

F32 = jnp.float32
BF16 = jnp.bfloat16
HIGHEST = lax.Precision.HIGHEST

D_MODEL = 2048
SEQ = 2048
CHUNK = 64
HG_HEADS = 8
HG_D = 128
AT_HEADS = 16
AT_DH = 64
LEFT = 8
REL_CLIP = 256
D_FF = 8192
EPS = 1e-6
ADAM_LR = 0.001
ADAM_B1 = 0.9
ADAM_B2 = 0.999
ADAM_EPS = 1e-08
ADAM_WD = 0.01
ADAM_STEP = 10

LANE = 128
NEG = -1e30
EXP_CLAMP = 80.0
VMEM_LIMIT = 48 * 1024 * 1024
MM_TM, MM_TN, MM_TK = 1024, 1024, 512
ROW_TILE = 256
QB = 2 * CHUNK


def _hgw():
    return HG_HEADS * HG_D


def _atw():
    return AT_HEADS * AT_DH


def _cparams(sem):
    return pltpu.CompilerParams(dimension_semantics=sem, vmem_limit_bytes=VMEM_LIMIT)


def _sigmoid(x):
    return jax.nn.sigmoid(x)


def _dot(a, b, dims, precision=None):
    return lax.dot_general(a, b, (dims, ((), ())), preferred_element_type=F32, precision=precision)


def _nn(a, b, precision=None):
    return _dot(a, b, ((1,), (0,)), precision)


def _nt(a, b, precision=None):
    return _dot(a, b, ((1,), (1,)), precision)


def _tn(a, b, precision=None):
    return _dot(a, b, ((0,), (0,)), precision)


def _mm(name, a, b, mode, out_dtypes, extras=(), epilogue=None):
    if mode == "nn":
        (M, K), (K2, N) = a.shape, b.shape
    elif mode == "nt":
        (M, K), (N, K2) = a.shape, b.shape
    else:
        (K, M), (K2, N) = a.shape, b.shape
    assert K == K2, (name, a.shape, b.shape)
    tm, tn, tk = min(MM_TM, M), min(MM_TN, N), min(MM_TK, K)
    assert M % tm == 0 and N % tn == 0 and K % tk == 0, (name, M, N, K)
    nk = K // tk
    ne, no = len(extras), len(out_dtypes)
    if epilogue is None:
        epilogue = lambda acc: (acc,)

    def body(a_ref, b_ref, *rest):
        extra_refs, out_refs, acc_ref = rest[:ne], rest[ne:ne + no], rest[ne + no]
        k = pl.program_id(2)

        @pl.when(k == 0)
        def _():
            acc_ref[...] = jnp.zeros_like(acc_ref)

        av, bv = a_ref[...].astype(BF16), b_ref[...].astype(BF16)
        if mode == "nn":
            acc_ref[...] += _nn(av, bv)
        elif mode == "nt":
            acc_ref[...] += _nt(av, bv)
        else:
            acc_ref[...] += _tn(av, bv)

        @pl.when(k == nk - 1)
        def _():
            res = epilogue(acc_ref[...], *[e[...] for e in extra_refs])
            for o_ref, r in zip(out_refs, res):
                o_ref[...] = r.astype(o_ref.dtype)

    if mode == "nn":
        a_spec = pl.BlockSpec((tm, tk), lambda i, j, k: (i, k))
        b_spec = pl.BlockSpec((tk, tn), lambda i, j, k: (k, j))
    elif mode == "nt":
        a_spec = pl.BlockSpec((tm, tk), lambda i, j, k: (i, k))
        b_spec = pl.BlockSpec((tn, tk), lambda i, j, k: (j, k))
    else:
        a_spec = pl.BlockSpec((tk, tm), lambda i, j, k: (k, i))
        b_spec = pl.BlockSpec((tk, tn), lambda i, j, k: (k, j))
    o_spec = pl.BlockSpec((tm, tn), lambda i, j, k: (i, j))
    outs = pl.pallas_call(
        body, name=name,
        grid=(M // tm, N // tn, nk),
        in_specs=[a_spec, b_spec] + [o_spec] * ne,
        out_specs=[o_spec] * no,
        out_shape=[jax.ShapeDtypeStruct((M, N), dt) for dt in out_dtypes],
        scratch_shapes=[pltpu.VMEM((tm, tn), F32)],
        compiler_params=_cparams(("parallel", "parallel", "arbitrary")),
    )(a, b, *extras)
    return outs[0] if no == 1 else outs


def _row_spec(tr, d):
    return pl.BlockSpec((tr, d), lambda i: (i, 0))


def _vec_spec(d):
    return pl.BlockSpec((1, d), lambda i: (0, 0))


def _rms_fwd(name, x, w):
    T, D = x.shape
    tr = min(ROW_TILE, T)

    def body(x_ref, w_ref, o_ref):
        xf = x_ref[...]
        r = lax.rsqrt(jnp.mean(xf * xf, axis=-1, keepdims=True) + EPS)
        o_ref[...] = (xf * r * w_ref[...]).astype(BF16)

    return pl.pallas_call(
        body, name=name, grid=(T // tr,),
        in_specs=[_row_spec(tr, D), _vec_spec(D)], out_specs=_row_spec(tr, D),
        out_shape=jax.ShapeDtypeStruct((T, D), BF16),
        compiler_params=_cparams(("parallel",)),
    )(x, w)


def _rms_bwd(name, dy, h, w, dres):
    T, D = h.shape
    tr = min(ROW_TILE, T)

    def body(dy_ref, h_ref, w_ref, dres_ref, dh_ref, dhb_ref, dw_ref):
        @pl.when(pl.program_id(0) == 0)
        def _():
            dw_ref[...] = jnp.zeros_like(dw_ref)

        hf, dyv = h_ref[...], dy_ref[...]
        r = lax.rsqrt(jnp.mean(hf * hf, axis=-1, keepdims=True) + EPS)
        xhat = hf * r
        dw_ref[...] += jnp.sum(dyv * xhat, axis=0, keepdims=True)
        dxh = dyv * w_ref[...]
        dh = dres_ref[...] + r * (dxh - xhat * jnp.mean(dxh * xhat, axis=-1, keepdims=True))
        dh_ref[...] = dh
        dhb_ref[...] = dh.astype(BF16)

    return pl.pallas_call(
        body, name=name, grid=(T // tr,),
        in_specs=[_row_spec(tr, D), _row_spec(tr, D), _vec_spec(D), _row_spec(tr, D)],
        out_specs=[_row_spec(tr, D), _row_spec(tr, D), _vec_spec(D)],
        out_shape=[jax.ShapeDtypeStruct((T, D), F32), jax.ShapeDtypeStruct((T, D), BF16),
                   jax.ShapeDtypeStruct((1, D), F32)],
        compiler_params=_cparams(("arbitrary",)),
    )(dy, h, w, dres)


def _loss_head(h2, target, w):
    T, D = h2.shape
    tr = min(ROW_TILE, T)

    def body(h_ref, t_ref, w_ref, loss_ref, dh_ref, dhb_ref, dw_ref):
        @pl.when(pl.program_id(0) == 0)
        def _():
            dw_ref[...] = jnp.zeros_like(dw_ref)
            loss_ref[...] = jnp.zeros_like(loss_ref)

        hf, wv = h_ref[...], w_ref[...]
        r = lax.rsqrt(jnp.mean(hf * hf, axis=-1, keepdims=True) + EPS)
        xhat = hf * r
        diff = xhat * wv - t_ref[...]
        loss_ref[...] += 0.5 * jnp.sum(jnp.mean(diff * diff, axis=-1, keepdims=True))
        dyv = diff * (1.0 / D)
        dw_ref[...] += jnp.sum(dyv * xhat, axis=0, keepdims=True)
        dxh = dyv * wv
        dh = r * (dxh - xhat * jnp.mean(dxh * xhat, axis=-1, keepdims=True))
        dh_ref[...] = dh
        dhb_ref[...] = dh.astype(BF16)

    return pl.pallas_call(
        body, name="loss_head", grid=(T // tr,),
        in_specs=[_row_spec(tr, D), _row_spec(tr, D), _vec_spec(D)],
        out_specs=[_vec_spec(LANE), _row_spec(tr, D), _row_spec(tr, D), _vec_spec(D)],
        out_shape=[jax.ShapeDtypeStruct((1, LANE), F32), jax.ShapeDtypeStruct((T, D), F32),
                   jax.ShapeDtypeStruct((T, D), BF16), jax.ShapeDtypeStruct((1, D), F32)],
        compiler_params=_cparams(("arbitrary",)),
    )(h2, target, w)


def _gate_tiles(T, D):
    goff = 4 * _hgw() + 3 * _atw()
    tc = min(1024, D)
    assert goff % tc == 0 and D % tc == 0
    return min(ROW_TILE, T), tc, goff // tc, D // tc


def _merge(z, pa, pb):
    T, D = pa.shape
    tr, tc, g0, nd = _gate_tiles(T, D)

    def body(ga_ref, gb_ref, pa_ref, pb_ref, o_ref):
        o_ref[...] = (_sigmoid(ga_ref[...]) * pa_ref[...] + _sigmoid(gb_ref[...]) * pb_ref[...]).astype(BF16)

    t = pl.BlockSpec((tr, tc), lambda i, j: (i, j))
    return pl.pallas_call(
        body, name="merge", grid=(T // tr, nd),
        in_specs=[pl.BlockSpec((tr, tc), lambda i, j: (i, g0 + j)),
                  pl.BlockSpec((tr, tc), lambda i, j: (i, g0 + nd + j)), t, t],
        out_specs=t, out_shape=jax.ShapeDtypeStruct((T, D), BF16),
        compiler_params=_cparams(("parallel", "parallel")),
    )(z, z, pa, pb)


def _dmerge(dm, z, pa, pb):
    T, D = pa.shape
    tr, tc, g0, nd = _gate_tiles(T, D)

    def body(dm_ref, ga_ref, gb_ref, pa_ref, pb_ref, dpa_ref, dpb_ref, dga_ref, dgb_ref):
        dmv = dm_ref[...]
        sa, sb = _sigmoid(ga_ref[...]), _sigmoid(gb_ref[...])
        dpa_ref[...] = (dmv * sa).astype(BF16)
        dpb_ref[...] = (dmv * sb).astype(BF16)
        dga_ref[...] = (dmv * pa_ref[...] * sa * (1.0 - sa)).astype(BF16)
        dgb_ref[...] = (dmv * pb_ref[...] * sb * (1.0 - sb)).astype(BF16)

    t = pl.BlockSpec((tr, tc), lambda i, j: (i, j))
    return pl.pallas_call(
        body, name="dmerge", grid=(T // tr, nd),
        in_specs=[t, pl.BlockSpec((tr, tc), lambda i, j: (i, g0 + j)),
                  pl.BlockSpec((tr, tc), lambda i, j: (i, g0 + nd + j)), t, t],
        out_specs=[t, t, t, t],
        out_shape=[jax.ShapeDtypeStruct((T, D), BF16)] * 4,
        compiler_params=_cparams(("parallel", "parallel")),
    )(dm, z, z, pa, pb)


def _hg_gates(xq, xf, lb):
    f = _sigmoid(xf)
    g = lb + (1.0 - lb) * f
    sq = _sigmoid(xq)
    return f, g, jnp.log(g), 1.0 - g, sq, xq * sq * (HG_D ** -0.5)


def _hg_decays(lg, tri_incl, rowi):
    b = _nn(tri_incl, lg, precision=HIGHEST)
    b_last = jnp.sum(lg, axis=0, keepdims=True)
    b_mid = jnp.sum(jnp.where(rowi <= CHUNK // 2, lg, 0.0), axis=0, keepdims=True)
    return b, b_last, b_mid


def _hg_in_specs(T):
    H = HG_HEADS
    return [pl.BlockSpec((T, HG_D), lambda h, s=s: (0, s * H + h)) for s in range(4)]


def _hg_fwd(z, lb_logits, hgw):
    T = z.shape[0]
    H, d, C = HG_HEADS, HG_D, CHUNK
    nc = T // C

    def body(hq_ref, hf_ref, hi_ref, hg_ref, lbl_ref, w_ref, ya_ref, o_ref, s_ref):
        lb = 1.0 / (1.0 + jnp.exp(lbl_ref[1:2, :] - lbl_ref[0:1, :]))
        wv = w_ref[...]
        row = lax.broadcasted_iota(jnp.int32, (C, C), 0)
        col = lax.broadcasted_iota(jnp.int32, (C, C), 1)
        tril = col <= row
        tri_incl = tril.astype(F32)
        rowi = lax.broadcasted_iota(jnp.int32, (C, d), 0)

        def chunk(c, st):
            rows = pl.ds(pl.multiple_of(c * C, C), C)
            xq, xf, v, xg = hq_ref[rows, :], hf_ref[rows, :], hi_ref[rows, :], hg_ref[rows, :]
            _, _, lg, kk, _, q = _hg_gates(xq, xf, lb)
            b, b_last, b_mid = _hg_decays(lg, tri_incl, rowi)
            st_b = st.astype(BF16)
            s_ref[c] = st
            vb = v.astype(BF16)
            o = _nt((q * jnp.exp(b)).astype(BF16), st_b)
            qt = (q * jnp.exp(b - b_mid)).astype(BF16)
            kt = (kk * jnp.exp(jnp.minimum(b_mid - b, EXP_CLAMP))).astype(BF16)
            a = jnp.where(tril, _nt(qt, kt), 0.0).astype(BF16)
            o = o + _nn(a, vb)
            st_new = st * jnp.exp(b_last) + _tn(v, kk * jnp.exp(b_last - b), HIGHEST)
            o_ref[rows, :] = o
            r = lax.rsqrt(jnp.mean(o * o, axis=-1, keepdims=True) + EPS)
            ya_ref[rows, :] = (o * r * wv * (xg * _sigmoid(xg))).astype(BF16)
            return st_new

        lax.fori_loop(0, nc, chunk, jnp.zeros((d, d), F32))

    head = pl.BlockSpec((T, d), lambda h: (0, h))
    return pl.pallas_call(
        body, name="hg_fwd", grid=(H,),
        in_specs=_hg_in_specs(T) + [pl.BlockSpec((2, d), lambda h: (0, h)), pl.BlockSpec((1, d), lambda h: (0, 0))],
        out_specs=[head, head, pl.BlockSpec((None, nc, d, d), lambda h: (h, 0, 0, 0))],
        out_shape=[jax.ShapeDtypeStruct((T, H * d), BF16), jax.ShapeDtypeStruct((T, H * d), F32),
                   jax.ShapeDtypeStruct((H, nc, d, d), F32)],
        compiler_params=_cparams(("parallel",)),
    )(z, z, z, z, lb_logits, hgw)


def _hg_bwd(z, o, dya, states, lb_logits, hgw):
    T = z.shape[0]
    H, d, C = HG_HEADS, HG_D, CHUNK
    nc = T // C
    scale = HG_D ** -0.5

    def body(hq_ref, hf_ref, hi_ref, hg_ref, o_ref, dy_ref, s_ref, lbl_ref, w_ref,
             dq_ref, df_ref, di_ref, dg_ref, dlbl_ref, dw_ref, acc_ref):
        lb = 1.0 / (1.0 + jnp.exp(lbl_ref[1:2, :] - lbl_ref[0:1, :]))
        wv = w_ref[...]
        row = lax.broadcasted_iota(jnp.int32, (C, C), 0)
        col = lax.broadcasted_iota(jnp.int32, (C, C), 1)
        tril = col <= row
        tri_incl = tril.astype(F32)
        triu_incl = (col >= row).astype(F32)
        rowi = lax.broadcasted_iota(jnp.int32, (C, d), 0)
        acc_ref[...] = jnp.zeros_like(acc_ref)

        @pl.when(pl.program_id(0) == 0)
        def _():
            dw_ref[...] = jnp.zeros_like(dw_ref)

        def chunk(i, carry):
            dst, tail = carry
            c = nc - 1 - i
            rows = pl.ds(pl.multiple_of(c * C, C), C)
            xq, xf, v, xg = hq_ref[rows, :], hf_ref[rows, :], hi_ref[rows, :], hg_ref[rows, :]
            f, g, lg, kk, sq, q = _hg_gates(xq, xf, lb)
            b, b_last, b_mid = _hg_decays(lg, tri_incl, rowi)
            e_b, e_qm, e_km = jnp.exp(b), jnp.exp(b - b_mid), jnp.exp(jnp.minimum(b_mid - b, EXP_CLAMP))
            e_kl, e_last = jnp.exp(b_last - b), jnp.exp(b_last)
            ov, dy = o_ref[rows, :], dy_ref[rows, :]
            r = lax.rsqrt(jnp.mean(ov * ov, axis=-1, keepdims=True) + EPS)
            xhat = ov * r
            sg = _sigmoid(xg)
            dxg = dy * xhat * wv * (sg * (1.0 + xg * (1.0 - sg)))
            dyn = dy * (xg * sg)
            acc_ref[0:1, :] += jnp.sum(dyn * xhat, axis=0, keepdims=True)
            dxh = dyn * wv
            dof = r * (dxh - xhat * jnp.mean(dxh * xhat, axis=-1, keepdims=True))
            do, vb = dof.astype(BF16), v.astype(BF16)
            qt, kt = q * e_qm, kk * e_km
            pm = jnp.where(tril, _nt(do, vb), 0.0)
            am = jnp.where(tril, _nt(qt.astype(BF16), kt.astype(BF16)), 0.0).astype(BF16)
            dq = _nn(dof, s_ref[c], HIGHEST) * e_b + _nn(pm, kt, HIGHEST) * e_qm
            dk = _tn(pm, qt, HIGHEST) * e_km + _nn(v, dst, HIGHEST) * e_kl
            dv = _tn(am, do) + _nt((kk * e_kl).astype(BF16), dst.astype(BF16))
            dst_prev = dst * e_last + _tn(dof, q * e_b, HIGHEST)
            db = q * dq - kk * dk
            dlg = _nn(triu_incl, db, precision=HIGHEST) + tail
            dgate = dlg / g - dk
            acc_ref[1:2, :] += jnp.sum(dgate * (1.0 - f), axis=0, keepdims=True)
            dq_ref[rows, :] = (dq * scale * (sq * (1.0 + xq * (1.0 - sq)))).astype(BF16)
            df_ref[rows, :] = (dgate * (1.0 - lb) * f * (1.0 - f)).astype(BF16)
            di_ref[rows, :] = dv.astype(BF16)
            dg_ref[rows, :] = dxg.astype(BF16)
            return dst_prev, tail + jnp.sum(db, axis=0, keepdims=True)

        lax.fori_loop(0, nc, chunk, (jnp.zeros((d, d), F32), jnp.zeros((1, d), F32)))
        dw_ref[...] += acc_ref[0:1, :]
        dl0 = acc_ref[1:2, :] * lb * (1.0 - lb)
        dlbl_ref[0:1, :] = dl0
        dlbl_ref[1:2, :] = -dl0

    head = pl.BlockSpec((T, d), lambda h: (0, h))
    return pl.pallas_call(
        body, name="hg_bwd", grid=(H,),
        in_specs=_hg_in_specs(T) + [head, head, pl.BlockSpec((None, nc, d, d), lambda h: (h, 0, 0, 0)),
                                    pl.BlockSpec((2, d), lambda h: (0, h)), pl.BlockSpec((1, d), lambda h: (0, 0))],
        out_specs=[head, head, head, head, pl.BlockSpec((2, d), lambda h: (0, h)),
                   pl.BlockSpec((1, d), lambda h: (0, 0))],
        out_shape=[jax.ShapeDtypeStruct((T, H * d), BF16)] * 4 + [jax.ShapeDtypeStruct((2, H * d), F32),
                                                                   jax.ShapeDtypeStruct((1, d), F32)],
        scratch_shapes=[pltpu.VMEM((8, d), F32)],
        compiler_params=_cparams(("arbitrary",)),
    )(z, z, z, z, o, dya, states, lb_logits, hgw)


def _at_dims():
    pad = LEFT * CHUNK
    return pad, QB + pad, AT_HEADS * AT_DH // LANE, 4 * _hgw() // LANE


def _rel_of_period():
    pad, W, _, _ = _at_dims()
    n = jnp.arange(QB + W)
    return jnp.clip(pad - jnp.where(n < W, n, n - (QB + W)), -REL_CLIP, REL_CLIP) + REL_CLIP


def _bias_window(rel_bias):
    pad, W, _, _ = _at_dims()
    H, P = rel_bias.shape[0], QB + W
    per = rel_bias[:, _rel_of_period()]
    win = jnp.tile(per, (1, QB))[:, :QB * (P - 1)].reshape(H, QB, P - 1)[:, :, :W]
    t = jnp.arange(QB)[:, None]
    j = jnp.arange(W)[None, :]
    ok = (j // CHUNK >= t // CHUNK) & (j // CHUNK <= t // CHUNK + LEFT)
    return jnp.where(ok[None], win, NEG)


def _bias_window_grad(dbw):
    pad, W, _, _ = _at_dims()
    H, P = dbw.shape[0], QB + W
    flat = jnp.pad(dbw, ((0, 0), (0, 0), (0, P - 1 - W))).reshape(H, QB * (P - 1))
    per = jnp.pad(flat, ((0, 0), (0, QB))).reshape(H, QB, P).sum(axis=1)
    onehot = _rel_of_period()[:, None] == jnp.arange(2 * REL_CLIP + 1)[None, :]
    return jnp.dot(per, onehot.astype(F32), precision=HIGHEST)


def _at_softmax(q_half, kw, bias, valid):
    s = _nt(q_half, kw) * (AT_DH ** -0.5) + bias
    s = jnp.where(valid, s, NEG)
    e = jnp.exp(s - jnp.max(s, axis=-1, keepdims=True))
    return e / jnp.sum(e, axis=-1, keepdims=True)


def _at_fwd(z, bias_win):
    T = z.shape[0]
    pad, W, HP, c0 = _at_dims()
    nq = T // QB

    def body(q_ref, k_ref, v_ref, bias_ref, o_ref, kpad, vpad):
        qi = pl.program_id(1)

        @pl.when(qi == 0)
        def _():
            kpad[0:pad, :] = jnp.zeros((pad, LANE), BF16)
            vpad[0:pad, :] = jnp.zeros((pad, LANE), BF16)
            kpad[pad:, :] = k_ref[...].astype(BF16)
            vpad[pad:, :] = v_ref[...].astype(BF16)

        win = pl.ds(pl.multiple_of(qi * QB, QB), W)
        kw, vw = kpad[win, :], vpad[win, :]
        q = q_ref[...]
        lane = lax.broadcasted_iota(jnp.int32, (QB, LANE), 1)
        first = lane < AT_DH
        valid = lax.broadcasted_iota(jnp.int32, (QB, W), 1) + qi * QB >= pad
        pa = _at_softmax(jnp.where(first, q, 0.0).astype(BF16), kw, bias_ref[0], valid)
        pb = _at_softmax(jnp.where(first, 0.0, q).astype(BF16), kw, bias_ref[1], valid)
        o_ref[...] = jnp.where(first, _nn(pa.astype(BF16), vw), _nn(pb.astype(BF16), vw)).astype(BF16)

    full = lambda s: pl.BlockSpec((T, LANE), lambda hp, qi, s=s: (0, c0 + s * HP + hp))
    return pl.pallas_call(
        body, name="at_fwd", grid=(HP, nq),
        in_specs=[pl.BlockSpec((QB, LANE), lambda hp, qi: (qi, c0 + hp)), full(1), full(2),
                  pl.BlockSpec((2, QB, W), lambda hp, qi: (hp, 0, 0))],
        out_specs=pl.BlockSpec((QB, LANE), lambda hp, qi: (qi, hp)),
        out_shape=jax.ShapeDtypeStruct((T, HP * LANE), BF16),
        scratch_shapes=[pltpu.VMEM((T + pad, LANE), BF16)] * 2,
        compiler_params=_cparams(("parallel", "arbitrary")),
    )(z, z, z, bias_win)


def _at_bwd(z, dyb, bias_win):
    T = z.shape[0]
    pad, W, HP, c0 = _at_dims()
    nq = T // QB
    scale = AT_DH ** -0.5

    def body(q_ref, k_ref, v_ref, do_ref, bias_ref, dq_ref, dk_ref, dv_ref, dbias_ref, kpad, vpad, dkpad, dvpad):
        qi = pl.program_id(1)

        @pl.when(qi == 0)
        def _():
            kpad[0:pad, :] = jnp.zeros((pad, LANE), BF16)
            vpad[0:pad, :] = jnp.zeros((pad, LANE), BF16)
            kpad[pad:, :] = k_ref[...].astype(BF16)
            vpad[pad:, :] = v_ref[...].astype(BF16)
            dkpad[...] = jnp.zeros_like(dkpad)
            dvpad[...] = jnp.zeros_like(dvpad)
            dbias_ref[...] = jnp.zeros_like(dbias_ref)

        win = pl.ds(pl.multiple_of(qi * QB, QB), W)
        kw, vw = kpad[win, :], vpad[win, :]
        q, do = q_ref[...], do_ref[...]
        lane = lax.broadcasted_iota(jnp.int32, (QB, LANE), 1)
        first = lane < AT_DH
        valid = lax.broadcasted_iota(jnp.int32, (QB, W), 1) + qi * QB >= pad

        def half(hh, qh, doh):
            p = _at_softmax(qh, kw, bias_ref[hh], valid)
            dp = _nt(doh, vw)
            ds = p * (dp - jnp.sum(p * dp, axis=-1, keepdims=True))
            dbias_ref[hh] += ds
            dss = (ds * scale).astype(BF16)
            return _nn(dss, kw), _tn(dss, qh), _tn(p.astype(BF16), doh)

        dqa, dka, dva = half(0, jnp.where(first, q, 0.0).astype(BF16), jnp.where(first, do, 0.0).astype(BF16))
        dqb, dkb, dvb = half(1, jnp.where(first, 0.0, q).astype(BF16), jnp.where(first, 0.0, do).astype(BF16))
        dq_ref[...] = jnp.where(first, dqa, dqb).astype(BF16)
        dkpad[win, :] += dka + dkb
        dvpad[win, :] += dva + dvb

        @pl.when(qi == nq - 1)
        def _():
            dk_ref[...] = dkpad[pad:, :].astype(BF16)
            dv_ref[...] = dvpad[pad:, :].astype(BF16)

    full = lambda s: pl.BlockSpec((T, LANE), lambda hp, qi, s=s: (0, c0 + s * HP + hp))
    blk = pl.BlockSpec((QB, LANE), lambda hp, qi: (qi, hp))
    col = pl.BlockSpec((T, LANE), lambda hp, qi: (0, hp))
    bw = pl.BlockSpec((2, QB, W), lambda hp, qi: (hp, 0, 0))
    return pl.pallas_call(
        body, name="at_bwd", grid=(HP, nq),
        in_specs=[pl.BlockSpec((QB, LANE), lambda hp, qi: (qi, c0 + hp)), full(1), full(2), blk, bw],
        out_specs=[blk, col, col, bw],
        out_shape=[jax.ShapeDtypeStruct((T, HP * LANE), BF16)] * 3 + [jax.ShapeDtypeStruct(bias_win.shape, F32)],
        scratch_shapes=[pltpu.VMEM((T + pad, LANE), BF16)] * 2 + [pltpu.VMEM((T + pad, LANE), F32)] * 2,
        compiler_params=_cparams(("parallel", "arbitrary")),
    )(z, z, z, dyb, bias_win)


def _piece_tiles(name, full_shape):
    pr, pc = _piece_shape(name, full_shape)
    tr = min(ROW_TILE, pr)
    assert pr % tr == 0
    nt = pr // tr
    if name in ROW_SHARDED:
        return tr, nt, lambda q, half, i: ((2 * q + half) * nt + i, 0)
    return tr, nt, lambda q, half, i: (half * nt + i, q)


def _cast_into_full(name, wq, place):
    full = _full_shape(name, wq.shape)
    pc = wq.shape[1]
    tr, nt, at = _piece_tiles(name, full)

    def body(place_ref, w_ref, o_ref):
        o_ref[...] = w_ref[...].astype(BF16)

    return pl.pallas_call(
        body, name="cast_" + name,
        grid_spec=pltpu.PrefetchScalarGridSpec(
            num_scalar_prefetch=1, grid=(2, nt),
            in_specs=[pl.BlockSpec((tr, pc), lambda h, i, s: (h * nt + i, 0))],
            out_specs=pl.BlockSpec((tr, pc), lambda h, i, s: at(s[0], h, i))),
        out_shape=jax.ShapeDtypeStruct(full, BF16),
        compiler_params=_cparams(("parallel", "parallel")),
    )(place, wq)


def _chip_sum(name, grad, theirs, place):
    pr, pc = theirs.shape[1:]
    tr, nt, at = _piece_tiles(name, grad.shape)

    def body(place_ref, g_ref, t_ref, o_ref):
        o_ref[...] = (g_ref[...].astype(F32) + t_ref[...].astype(F32)).astype(BF16)

    piece = pl.BlockSpec((None, tr, pc), lambda q, i, s: (q, i, 0))
    return pl.pallas_call(
        body, name="chip_sum_" + name,
        grid_spec=pltpu.PrefetchScalarGridSpec(
            num_scalar_prefetch=1, grid=(4, nt),
            in_specs=[pl.BlockSpec((tr, pc), lambda q, i, s: at(q, s[1], i)), piece], out_specs=piece),
        out_shape=jax.ShapeDtypeStruct(theirs.shape, BF16),
        compiler_params=_cparams(("parallel", "parallel")),
    )(place, grad, theirs)


def _piece_sum(name, chip_sums, got, place):
    pr, pc = chip_sums.shape[1:]
    tr = min(ROW_TILE, pr)

    def body(place_ref, own_ref, got_ref, o_ref):
        o_ref[...] = (own_ref[...].astype(F32) + got_ref[0].astype(F32) + got_ref[1].astype(F32)
                      + got_ref[2].astype(F32))

    return pl.pallas_call(
        body, name="piece_sum_" + name,
        grid_spec=pltpu.PrefetchScalarGridSpec(
            num_scalar_prefetch=1, grid=(pr // tr,),
            in_specs=[pl.BlockSpec((None, tr, pc), lambda i, s: (s[0], i, 0)),
                      pl.BlockSpec((3, tr, pc), lambda i, s: (0, i, 0))],
            out_specs=pl.BlockSpec((tr, pc), lambda i, s: (i, 0))),
        out_shape=jax.ShapeDtypeStruct((pr, pc), F32),
        compiler_params=_cparams(("parallel",)),
    )(place, chip_sums, got)


def _adam_quarter(name, w, m, v, g_mine, g_sib, place):
    pr, pc = g_mine.shape
    tr = min(ROW_TILE // 2, pr)
    nt = pr // tr

    def body(place_ref, w_ref, m_ref, v_ref, gm_ref, gs_ref, go_ref, d_ref, mo_ref, vo_ref):
        g = jnp.where(pl.program_id(0) == place_ref[1], gm_ref[...], gs_ref[...])
        delta, mn, vn = _adam_math(w_ref[...], g, m_ref[...], v_ref[...])
        go_ref[...] = g
        d_ref[...] = delta
        mo_ref[...] = mn
        vo_ref[...] = vn

    quarter = pl.BlockSpec((tr, pc), lambda h, i, s: (h * nt + i, 0))
    mine = pl.BlockSpec((tr, pc), lambda h, i, s: (jnp.where(h == s[1], i, 0), 0))
    sib = pl.BlockSpec((tr, pc), lambda h, i, s: (jnp.where(h == s[1], 0, i), 0))
    return pl.pallas_call(
        body, name="adam_" + name,
        grid_spec=pltpu.PrefetchScalarGridSpec(
            num_scalar_prefetch=1, grid=(2, nt),
            in_specs=[quarter, quarter, quarter, mine, sib], out_specs=[quarter] * 4),
        out_shape=[jax.ShapeDtypeStruct(w.shape, F32)] * 4,
        compiler_params=_cparams(("parallel", "parallel")),
    )(place, w, m, v, g_mine, g_sib)


def _adam_math(w, g, m, v):
    m = ADAM_B1 * m + (1.0 - ADAM_B1) * g
    v = ADAM_B2 * v + (1.0 - ADAM_B2) * (g * g)
    m_hat = m / (1.0 - ADAM_B1 ** ADAM_STEP)
    v_hat = v / (1.0 - ADAM_B2 ** ADAM_STEP)
    return -ADAM_LR * (m_hat / (jnp.sqrt(v_hat) + ADAM_EPS) + ADAM_WD * w), m, v


WEIGHTS = ("w_in", "w_branch_a", "w_branch_b", "w_out", "w_up", "w_down")
ROW_SHARDED = ("w_out", "w_down")
ANY = pl.BlockSpec(memory_space=pl.ANY)
MESH = pl.DeviceIdType.MESH


def _place():
    x, y, c = lax.axis_index("x"), lax.axis_index("y"), lax.axis_index("c")
    chips = [(1 - x, y), (x, 1 - y), (1 - x, 1 - y)]
    return x, y, c, 2 * x + y, chips, [2 * cx + cy for cx, cy in chips]


def _piece(full_ref, name, q, half):
    K, N = full_ref.shape
    if name in ROW_SHARDED:
        rows = K // 8
        return full_ref.at[pl.ds(q * (2 * rows) + half * rows, rows), :]
    return full_ref.at[pl.ds(half * (K // 2), K // 2), pl.ds(q * (N // 4), N // 4)]


def _quarter(full_ref, name, q):
    K, N = full_ref.shape
    if name in ROW_SHARDED:
        return full_ref.at[pl.ds(q * (K // 4), K // 4), :]
    return full_ref.at[:, pl.ds(q * (N // 4), N // 4)]


def _piece_shape(name, full_shape):
    K, N = full_shape
    return (K // 8, N) if name in ROW_SHARDED else (K // 2, N // 4)


def _full_shape(name, quarter_shape):
    Kq, Nq = quarter_shape
    return (4 * Kq, Nq) if name in ROW_SHARDED else (Kq, 4 * Nq)


def _remote(src, dst, send_sem, recv_sem, device):
    return pltpu.make_async_remote_copy(src_ref=src, dst_ref=dst, send_sem=send_sem, recv_sem=recv_sem,
                                        device_id=device, device_id_type=MESH)


def _gather_weights(fulls):
    n = len(WEIGHTS)

    def body(*refs):
        f_refs = refs[n:2 * n]
        send_sems, recv_sems = refs[2 * n:]
        x, y, c, p, chips, chip_ids = _place()
        sib = (x, y, 1 - c)
        sends = []
        for i, name in enumerate(WEIGHTS):
            mine = _piece(f_refs[i], name, p, c)
            for j, chip in enumerate(chips):
                cp = _remote(mine, mine, send_sems.at[i, j], recv_sems.at[i, j], (*chip, c))
                cp.start()
                sends.append(cp)
        for i, name in enumerate(WEIGHTS):
            for j, cid in enumerate(chip_ids):
                landed = _piece(f_refs[i], name, cid, c)
                _remote(landed, landed, send_sems.at[i, j], recv_sems.at[i, j], sib).wait_recv()
                cp = _remote(landed, landed, send_sems.at[i, 3 + j], recv_sems.at[i, 3 + j], sib)
                cp.start()
                sends.append(cp)
        for i, name in enumerate(WEIGHTS):
            for j, cid in enumerate(chip_ids):
                other = _piece(f_refs[i], name, cid, 1 - c)
                _remote(other, other, send_sems.at[i, 3 + j], recv_sems.at[i, 3 + j], sib).wait_recv()
        for cp in sends:
            cp.wait_send()

    return pl.pallas_call(
        body, name="gather_weights",
        in_specs=[ANY] * n, out_specs=[ANY] * n,
        out_shape=[jax.ShapeDtypeStruct(f.shape, BF16) for f in fulls],
        input_output_aliases={i: i for i in range(n)},
        scratch_shapes=[pltpu.SemaphoreType.DMA((n, 6)), pltpu.SemaphoreType.DMA((n, 6))],
    )(*fulls)


def _send_to_sibling(grads):
    n = len(WEIGHTS)

    def body(*refs):
        g_refs, theirs_refs = refs[:n], refs[n:2 * n]
        send_sems, recv_sems = refs[2 * n:]
        x, y, c, _, _, _ = _place()
        copies = []
        for i, name in enumerate(WEIGHTS):
            for q in range(4):
                cp = _remote(_piece(g_refs[i], name, q, 1 - c), theirs_refs[i].at[q],
                             send_sems.at[i, q], recv_sems.at[i, q], (x, y, 1 - c))
                cp.start()
                copies.append(cp)
        for cp in copies:
            cp.wait()

    return pl.pallas_call(
        body, name="send_to_sibling",
        in_specs=[ANY] * n, out_specs=[ANY] * n,
        out_shape=[jax.ShapeDtypeStruct((4,) + _piece_shape(name, g.shape), BF16) for name, g in zip(WEIGHTS, grads)],
        scratch_shapes=[pltpu.SemaphoreType.DMA((n, 4)), pltpu.SemaphoreType.DMA((n, 4))],
    )(*grads)


def _exchange_chips(chip_sums):
    n = len(WEIGHTS)

    def body(*refs):
        s_refs, got_refs = refs[:n], refs[n:2 * n]
        send_sems, recv_sems = refs[2 * n:]
        _, _, c, p, chips, chip_ids = _place()
        copies = []
        for i in range(n):
            for j, (chip, cid) in enumerate(zip(chips, chip_ids)):
                cp = _remote(s_refs[i].at[cid], got_refs[i].at[j], send_sems.at[i, j], recv_sems.at[i, j], (*chip, c))
                cp.start()
                copies.append(cp)
        for cp in copies:
            cp.wait()

    return pl.pallas_call(
        body, name="exchange_chips",
        in_specs=[ANY] * n, out_specs=[ANY] * n,
        out_shape=[jax.ShapeDtypeStruct((3,) + s.shape[1:], BF16) for s in chip_sums],
        scratch_shapes=[pltpu.SemaphoreType.DMA((n, 3)), pltpu.SemaphoreType.DMA((n, 3))],
    )(*chip_sums)


def _share_with_sibling(halves):
    n = len(WEIGHTS)

    def body(*refs):
        h_refs, o_refs = refs[:n], refs[n:2 * n]
        send_sems, recv_sems = refs[2 * n:]
        x, y, c, _, _, _ = _place()
        copies = []
        for i in range(n):
            cp = _remote(h_refs[i], o_refs[i], send_sems.at[i], recv_sems.at[i], (x, y, 1 - c))
            cp.start()
            copies.append(cp)
        for cp in copies:
            cp.wait()

    return pl.pallas_call(
        body, name="share_with_sibling",
        in_specs=[ANY] * n, out_specs=[ANY] * n,
        out_shape=[jax.ShapeDtypeStruct(h.shape, F32) for h in halves],
        scratch_shapes=[pltpu.SemaphoreType.DMA((n,)), pltpu.SemaphoreType.DMA((n,))],
    )(*halves)


def _small_allreduce_adam(gpart, w, m, v):
    R = gpart.shape[0]

    def body(g_ref, w_ref, m_ref, v_ref, go_ref, d_ref, mo_ref, vo_ref, buf, send_sems, recv_sems):
        x, y, c = lax.axis_index("x"), lax.axis_index("y"), lax.axis_index("c")
        me = 4 * x + 2 * y + c
        buf[me] = g_ref[...]
        copies = []
        for k in range(1, 8):
            fx, fy, fc = (k >> 2) & 1, (k >> 1) & 1, k & 1
            peer = (1 - x if fx else x, 1 - y if fy else y, 1 - c if fc else c)
            cp = _remote(g_ref, buf.at[me], send_sems.at[k - 1], recv_sems.at[k - 1], peer)
            cp.start()
            copies.append((cp, 4 * peer[0] + 2 * peer[1] + peer[2]))
        for k, (cp, pid) in enumerate(copies):
            _remote(g_ref, buf.at[pid], send_sems.at[k], recv_sems.at[k], (x, y, c)).wait_recv()
        for cp, _ in copies:
            cp.wait_send()
        g = buf[0]
        for d in range(1, 8):
            g = g + buf[d]
        delta, mn, vn = _adam_math(w_ref[...], g, m_ref[...], v_ref[...])
        go_ref[...] = g
        d_ref[...] = delta
        mo_ref[...] = mn
        vo_ref[...] = vn

    vm = pl.BlockSpec(memory_space=pltpu.VMEM)
    return pl.pallas_call(
        body, name="small_allreduce_adam",
        in_specs=[vm] * 4, out_specs=[vm] * 4,
        out_shape=[jax.ShapeDtypeStruct((R, LANE), F32)] * 4,
        scratch_shapes=[pltpu.VMEM((8, R, LANE), F32), pltpu.SemaphoreType.DMA((7,)), pltpu.SemaphoreType.DMA((7,))],
    )(gpart, w, m, v)


def _pack(arrs):
    flat = jnp.concatenate([a.reshape(-1).astype(F32) for a in arrs])
    rows = -(-flat.shape[0] // (8 * LANE)) * 8
    return jnp.pad(flat, (0, rows * LANE - flat.shape[0])).reshape(rows, LANE)


def _unpack(packed, like):
    flat, out, off = packed.reshape(-1), [], 0
    for a in like:
        out.append(flat[off:off + a.size].reshape(a.shape))
        off += a.size
    return out


def kernel(x, w_in, lb_logits, hg_norm_w, rel_bias, w_branch_a, w_branch_b, w_out, norm_mix_w, norm_mlp_w, w_up, w_down, norm_final_w, loss_target, m_w_in, m_lb_logits, m_hg_norm_w, m_rel_bias, m_w_branch_a, m_w_branch_b, m_w_out, m_norm_mix_w, m_norm_mlp_w, m_w_up, m_w_down, m_norm_final_w, v_w_in, v_lb_logits, v_hg_norm_w, v_rel_bias, v_w_branch_a, v_w_branch_b, v_w_out, v_norm_mix_w, v_norm_mlp_w, v_w_up, v_w_down, v_norm_final_w):
    T, D = x.shape[1], x.shape[2]
    x2, tgt = x.reshape(T, D), loss_target.reshape(T, D)
    big = dict(w_in=(w_in, m_w_in, v_w_in), w_branch_a=(w_branch_a, m_w_branch_a, v_w_branch_a),
               w_branch_b=(w_branch_b, m_w_branch_b, v_w_branch_b), w_out=(w_out, m_w_out, v_w_out),
               w_up=(w_up, m_w_up, v_w_up), w_down=(w_down, m_w_down, v_w_down))
    big = {k: tuple(a[0] for a in v) for k, v in big.items()}
    nfw = norm_final_w.reshape(1, D)

    place = jnp.stack([2 * lax.axis_index("x") + lax.axis_index("y"), lax.axis_index("c")]).astype(jnp.int32)
    Wf = dict(zip(WEIGHTS, _gather_weights([_cast_into_full(name, big[name][0], place) for name in WEIGHTS])))

    u1 = _rms_fwd("norm_mix", x2, norm_mix_w)
    z = _mm("z_proj", u1, Wf["w_in"], "nn", [F32])
    ya, o_hg, states = _hg_fwd(z, lb_logits, hg_norm_w)
    bias_win = _bias_window(rel_bias[0])
    yb = _at_fwd(z, bias_win)
    pa = _mm("branch_a", ya, Wf["w_branch_a"], "nn", [F32])
    pb = _mm("branch_b", yb, Wf["w_branch_b"], "nn", [F32])
    merged = _merge(z, pa, pb)
    add = lambda acc, res: (acc + res,)
    h1 = _mm("out_proj", merged, Wf["w_out"], "nn", [F32], extras=[x2], epilogue=add)
    u2 = _rms_fwd("norm_mlp", h1, norm_mlp_w)
    relu2 = lambda acc: (acc, jnp.square(jnp.maximum(acc, 0.0)))
    a_pre, act = _mm("mlp_up", u2, Wf["w_up"], "nn", [F32, BF16], epilogue=relu2)
    h2 = _mm("mlp_down", act, Wf["w_down"], "nn", [F32], extras=[h1], epilogue=add)
    loss_part, dh2, dh2b, d_nf = _loss_head(h2, tgt, nfw)

    drelu2 = lambda acc, a: (acc * (2.0 * jnp.maximum(a, 0.0)),)
    da = _mm("d_act", dh2b, Wf["w_down"], "nt", [BF16], extras=[a_pre], epilogue=drelu2)
    G = {}
    G["w_down"] = _mm("g_w_down", act, dh2b, "tn", [BF16])
    G["w_up"] = _mm("g_w_up", u2, da, "tn", [BF16])
    du2 = _mm("d_u2", da, Wf["w_up"], "nt", [F32])
    dh1, dh1b, d_nmlp = _rms_bwd("norm_mlp_bwd", du2, h1, norm_mlp_w, dh2)
    dmerged = _mm("d_merged", dh1b, Wf["w_out"], "nt", [F32])
    G["w_out"] = _mm("g_w_out", merged, dh1b, "tn", [BF16])
    dpa, dpb, dz_ga, dz_gb = _dmerge(dmerged, z, pa, pb)
    dya = _mm("d_ya", dpa, Wf["w_branch_a"], "nt", [F32])
    dyb = _mm("d_yb", dpb, Wf["w_branch_b"], "nt", [F32])
    G["w_branch_a"] = _mm("g_w_a", ya, dpa, "tn", [BF16])
    G["w_branch_b"] = _mm("g_w_b", yb, dpb, "tn", [BF16])
    dz_q, dz_f, dz_i, dz_g, d_lbl, d_hgw = _hg_bwd(z, o_hg, dya, states, lb_logits, hg_norm_w)
    dz_aq, dz_ak, dz_av, dbias_win = _at_bwd(z, dyb, bias_win)
    dz = jnp.concatenate([dz_q, dz_f, dz_i, dz_g, dz_aq, dz_ak, dz_av, dz_ga, dz_gb], axis=1)
    G["w_in"] = _mm("g_w_in", u1, dz, "tn", [BF16])
    du1 = _mm("d_u1", dz, Wf["w_in"], "nt", [F32])
    grad_x, _, d_nmix = _rms_bwd("norm_mix_bwd", du1, x2, norm_mix_w, dh1)
    d_rel = _bias_window_grad(dbias_win)

    theirs = _send_to_sibling([G[name] for name in WEIGHTS])
    chip_sums = [_chip_sum(name, G[name], t, place) for name, t in zip(WEIGHTS, theirs)]
    got = _exchange_chips(chip_sums)
    halves = [_piece_sum(name, s, g, place) for name, s, g in zip(WEIGHTS, chip_sums, got)]
    other = _share_with_sibling(halves)
    big_out = {}
    for name, g_mine, g_sib in zip(WEIGHTS, halves, other):
        outs = _adam_quarter(name, *big[name], g_mine, g_sib, place)
        big_out[name] = tuple(a[None] for a in outs)

    smalls = [("lb_logits", lb_logits, m_lb_logits, v_lb_logits, d_lbl),
              ("hg_norm_w", hg_norm_w, m_hg_norm_w, v_hg_norm_w, d_hgw),
              ("rel_bias", rel_bias, m_rel_bias, v_rel_bias, d_rel),
              ("norm_mix_w", norm_mix_w, m_norm_mix_w, v_norm_mix_w, d_nmix),
              ("norm_mlp_w", norm_mlp_w, m_norm_mlp_w, v_norm_mlp_w, d_nmlp),
              ("norm_final_w", norm_final_w, m_norm_final_w, v_norm_final_w, d_nf)]
    like = [s[1] for s in smalls]
    packed = _small_allreduce_adam(_pack([s[4] for s in smalls]), _pack(like), _pack([s[2] for s in smalls]),
                                   _pack([s[3] for s in smalls]))
    small_out = {s[0]: vals for s, vals in zip(smalls, zip(*[_unpack(p, like) for p in packed]))}

    loss = lax.psum(loss_part[0, 0], ("x", "y", "c"))
    order = ["w_in", "lb_logits", "hg_norm_w", "rel_bias", "w_branch_a", "w_branch_b", "w_out", "norm_mix_w",
             "norm_mlp_w", "w_up", "w_down", "norm_final_w"]
    res = {**big_out, **small_out}
    return (loss, grad_x.reshape(x.shape), *[res[n][0] for n in order], *[res[n][1] for n in order],
            *[res[n][2] for n in order], *[res[n][3] for n in order])
```

```python
import functools

import jax
import jax.numpy as jnp
from jax import lax
from jax.experimental import pallas as pl
from jax.experimental.pallas import tpu as pltpu

F32 = jnp.float32
BF16 = jnp.bfloat16
HIGHEST = lax.Precision.HIGHEST

D_MODEL = 2048
SEQ = 2048
CHUNK = 64
HG_HEADS = 8
HG_D = 128
AT_HEADS = 16
AT_DH = 64
LEFT = 8
REL_CLIP = 256
D_FF = 8192
EPS = 1e-6
ADAM_LR = 0.001
ADAM_B1 = 0.9
ADAM_B2 = 0.999
ADAM_EPS = 1e-08
ADAM_WD = 0.01
ADAM_STEP = 10

LANE = 128
NEG = -1e30
EXP_CLAMP = 80.0
VMEM_LIMIT = 48 * 1024 * 1024
MM_TM, MM_TN, MM_TK = 1024, 1024, 2048
ROW_TILE = 256
QB = 2 * CHUNK


def _hgw():
    return HG_HEADS * HG_D


def _atw():
    return AT_HEADS * AT_DH


def _cparams(sem):
    return pltpu.CompilerParams(dimension_semantics=sem, vmem_limit_bytes=VMEM_LIMIT)


def _sigmoid(x):
    return jax.nn.sigmoid(x)


def _dot(a, b, dims, precision=None):
    return lax.dot_general(a, b, (dims, ((), ())), preferred_element_type=F32, precision=precision)


def _nn(a, b, precision=None):
    return _dot(a, b, ((1,), (0,)), precision)


def _nt(a, b, precision=None):
    return _dot(a, b, ((1,), (1,)), precision)


def _tn(a, b, precision=None):
    return _dot(a, b, ((0,), (0,)), precision)


class _Side:
    def __init__(self, build, nsem, reads=(), aliased=(), fresh=()):
        self.build, self.nsem = build, nsem
        self.reads, self.aliased, self.fresh = list(reads), list(aliased), list(fresh)

    def operands(self):
        return self.reads + self.aliased

    def in_specs(self):
        return [ANY] * len(self.operands())

    def out_specs(self):
        return [ANY] * (len(self.aliased) + len(self.fresh))

    def out_shape(self):
        return [jax.ShapeDtypeStruct(a.shape, a.dtype) for a in self.aliased] + self.fresh

    def aliases(self, n_in, n_out):
        return {n_in + len(self.reads) + t: n_out + t for t in range(len(self.aliased))}

    def scratch(self):
        return [pltpu.SemaphoreType.DMA((self.nsem,)), pltpu.SemaphoreType.DMA((self.nsem,))]

    def hooks(self, in_refs, out_refs, sems, first, last):
        nr, na = len(self.reads), len(self.aliased)
        args = (in_refs[:nr], out_refs[:na], out_refs[na:], *sems)

        @pl.when(first)
        def _():
            for cp in self.build(*args):
                cp.start()

        @pl.when(last)
        def _():
            for cp in self.build(*args):
                cp.wait()


def _side_parts(side):
    if side is None:
        return [], [], [], [], lambda n_in, n_out: {}, []
    return side.operands(), side.in_specs(), side.out_specs(), side.out_shape(), side.aliases, side.scratch()


def _call_with_side(body, name, grid, in_specs, out_specs, out_shape, scratch, sem, operands, side):
    if side is None:
        return pl.pallas_call(body, name=name, grid=grid, in_specs=in_specs, out_specs=out_specs, out_shape=out_shape,
                              scratch_shapes=scratch, compiler_params=_cparams(sem))(*operands)
    n_in, n_out = len(in_specs), len(out_specs)
    n_sin, n_sout = len(side.operands()), len(side.out_specs())

    def wrapped(*refs):
        a, b, c = n_in + n_sin, n_in + n_sin + n_out, n_in + n_sin + n_out + n_sout
        ids = [pl.program_id(d) for d in range(len(grid))]
        first = functools.reduce(lambda p, q: p & q, [i == 0 for i in ids])
        last = functools.reduce(lambda p, q: p & q, [i == g - 1 for i, g in zip(ids, grid)])
        side.hooks(refs[n_in:a], refs[b:c], refs[-2:], first, last)
        body(*refs[:n_in], *refs[a:b], *refs[c:-2])

    return pl.pallas_call(
        wrapped, name=name, grid=grid, in_specs=in_specs + side.in_specs(), out_specs=out_specs + side.out_specs(),
        out_shape=out_shape + side.out_shape(), input_output_aliases=side.aliases(n_in, n_out),
        scratch_shapes=scratch + side.scratch(), compiler_params=_cparams(("arbitrary",) * len(grid)),
    )(*operands, *side.operands())


def _mm_tk(K):
    if K <= MM_TK:
        return K
    return MM_TK if K % MM_TK == 0 else MM_TK // 2


def _mm(name, a, b, mode, out_dtypes, extras=(), epilogue=None, side=None):
    if mode == "nn":
        (M, K), (K2, N) = a.shape, b.shape
    elif mode == "nt":
        (M, K), (N, K2) = a.shape, b.shape
    else:
        (K, M), (K2, N) = a.shape, b.shape
    assert K == K2, (name, a.shape, b.shape)
    tm, tn, tk = min(MM_TM, M), min(MM_TN, N), _mm_tk(K)
    assert M % tm == 0 and N % tn == 0 and K % tk == 0, (name, M, N, K)
    ni, nj, nk = M // tm, N // tn, K // tk
    ne, no = len(extras), len(out_dtypes)
    if epilogue is None:
        epilogue = lambda acc: (acc,)
    s_ops, s_in, s_out, s_shape, s_alias, s_scr = _side_parts(side)
    n_in, n_sin, n_sout = 2 + ne, len(s_ops), len(s_out)

    def body(*refs):
        a_ref, b_ref = refs[:2]
        extra_refs = refs[2:n_in]
        out_refs = refs[n_in + n_sin:n_in + n_sin + no]
        rest = refs[n_in + n_sin + no + n_sout:]
        i, j, k = pl.program_id(0), pl.program_id(1), pl.program_id(2)
        if side is not None:
            side.hooks(refs[n_in:n_in + n_sin], refs[n_in + n_sin + no:n_in + n_sin + no + n_sout], rest[-2:],
                       (i == 0) & (j == 0) & (k == 0), (i == ni - 1) & (j == nj - 1) & (k == nk - 1))
        av, bv = a_ref[...].astype(BF16), b_ref[...].astype(BF16)
        prod = _nn(av, bv) if mode == "nn" else _nt(av, bv) if mode == "nt" else _tn(av, bv)

        def finish(acc):
            res = epilogue(acc, *[e[...] for e in extra_refs])
            for o_ref, r in zip(out_refs, res):
                o_ref[...] = r.astype(o_ref.dtype)

        if nk == 1:
            finish(prod)
        else:
            acc_ref = rest[0]

            @pl.when(k == 0)
            def _():
                acc_ref[...] = prod

            @pl.when((k > 0) & (k < nk - 1))
            def _():
                acc_ref[...] += prod

            @pl.when(k == nk - 1)
            def _():
                finish(acc_ref[...] + prod)

    if mode == "nn":
        a_spec = pl.BlockSpec((tm, tk), lambda i, j, k: (i, k))
        b_spec = pl.BlockSpec((tk, tn), lambda i, j, k: (k, j))
    elif mode == "nt":
        a_spec = pl.BlockSpec((tm, tk), lambda i, j, k: (i, k))
        b_spec = pl.BlockSpec((tn, tk), lambda i, j, k: (j, k))
    else:
        a_spec = pl.BlockSpec((tk, tm), lambda i, j, k: (k, i))
        b_spec = pl.BlockSpec((tk, tn), lambda i, j, k: (k, j))
    o_spec = pl.BlockSpec((tm, tn), lambda i, j, k: (i, j))
    sem = ("arbitrary",) * 3 if side is not None else ("parallel", "parallel", "arbitrary")
    outs = pl.pallas_call(
        body, name=name,
        grid=(ni, nj, nk),
        in_specs=[a_spec, b_spec] + [o_spec] * ne + s_in,
        out_specs=[o_spec] * no + s_out,
        out_shape=[jax.ShapeDtypeStruct((M, N), dt) for dt in out_dtypes] + s_shape,
        input_output_aliases=s_alias(n_in, no),
        scratch_shapes=([pltpu.VMEM((tm, tn), F32)] if nk > 1 else []) + s_scr,
        compiler_params=_cparams(sem),
    )(a, b, *extras, *s_ops)
    return outs[0] if len(outs) == 1 else outs


def _row_spec(tr, d):
    return pl.BlockSpec((tr, d), lambda i: (i, 0))


def _vec_spec(d):
    return pl.BlockSpec((1, d), lambda i: (0, 0))


def _rms_fwd(name, x, w):
    T, D = x.shape
    tr = min(ROW_TILE, T)

    def body(x_ref, w_ref, o_ref):
        xf = x_ref[...]
        r = lax.rsqrt(jnp.mean(xf * xf, axis=-1, keepdims=True) + EPS)
        o_ref[...] = (xf * r * w_ref[...]).astype(BF16)

    return pl.pallas_call(
        body, name=name, grid=(T // tr,),
        in_specs=[_row_spec(tr, D), _vec_spec(D)], out_specs=_row_spec(tr, D),
        out_shape=jax.ShapeDtypeStruct((T, D), BF16),
        compiler_params=_cparams(("parallel",)),
    )(x, w)


def _rms_bwd(name, dy, h, w, dres):
    T, D = h.shape
    tr = min(ROW_TILE, T)

    def body(dy_ref, h_ref, w_ref, dres_ref, dh_ref, dhb_ref, dw_ref):
        @pl.when(pl.program_id(0) == 0)
        def _():
            dw_ref[...] = jnp.zeros_like(dw_ref)

        hf, dyv = h_ref[...], dy_ref[...]
        r = lax.rsqrt(jnp.mean(hf * hf, axis=-1, keepdims=True) + EPS)
        xhat = hf * r
        dw_ref[...] += jnp.sum(dyv * xhat, axis=0, keepdims=True)
        dxh = dyv * w_ref[...]
        dh = dres_ref[...] + r * (dxh - xhat * jnp.mean(dxh * xhat, axis=-1, keepdims=True))
        dh_ref[...] = dh
        dhb_ref[...] = dh.astype(BF16)

    return pl.pallas_call(
        body, name=name, grid=(T // tr,),
        in_specs=[_row_spec(tr, D), _row_spec(tr, D), _vec_spec(D), _row_spec(tr, D)],
        out_specs=[_row_spec(tr, D), _row_spec(tr, D), _vec_spec(D)],
        out_shape=[jax.ShapeDtypeStruct((T, D), F32), jax.ShapeDtypeStruct((T, D), BF16),
                   jax.ShapeDtypeStruct((1, D), F32)],
        compiler_params=_cparams(("arbitrary",)),
    )(dy, h, w, dres)


def _loss_head(h2, target, w):
    T, D = h2.shape
    tr = min(ROW_TILE, T)

    def body(h_ref, t_ref, w_ref, loss_ref, dh_ref, dhb_ref, dw_ref):
        @pl.when(pl.program_id(0) == 0)
        def _():
            dw_ref[...] = jnp.zeros_like(dw_ref)
            loss_ref[...] = jnp.zeros_like(loss_ref)

        hf, wv = h_ref[...], w_ref[...]
        r = lax.rsqrt(jnp.mean(hf * hf, axis=-1, keepdims=True) + EPS)
        xhat = hf * r
        diff = xhat * wv - t_ref[...]
        loss_ref[...] += 0.5 * jnp.sum(jnp.mean(diff * diff, axis=-1, keepdims=True))
        dyv = diff * (1.0 / D)
        dw_ref[...] += jnp.sum(dyv * xhat, axis=0, keepdims=True)
        dxh = dyv * wv
        dh = r * (dxh - xhat * jnp.mean(dxh * xhat, axis=-1, keepdims=True))
        dh_ref[...] = dh
        dhb_ref[...] = dh.astype(BF16)

    return pl.pallas_call(
        body, name="loss_head", grid=(T // tr,),
        in_specs=[_row_spec(tr, D), _row_spec(tr, D), _vec_spec(D)],
        out_specs=[_vec_spec(LANE), _row_spec(tr, D), _row_spec(tr, D), _vec_spec(D)],
        out_shape=[jax.ShapeDtypeStruct((1, LANE), F32), jax.ShapeDtypeStruct((T, D), F32),
                   jax.ShapeDtypeStruct((T, D), BF16), jax.ShapeDtypeStruct((1, D), F32)],
        compiler_params=_cparams(("arbitrary",)),
    )(h2, target, w)


def _gate_tiles(T, D):
    goff = 4 * _hgw() + 3 * _atw()
    tc = min(1024, D)
    assert goff % tc == 0 and D % tc == 0
    return min(ROW_TILE, T), tc, goff // tc, D // tc


def _merge(z, pa, pb):
    T, D = pa.shape
    tr, tc, g0, nd = _gate_tiles(T, D)

    def body(ga_ref, gb_ref, pa_ref, pb_ref, o_ref):
        o_ref[...] = (_sigmoid(ga_ref[...]) * pa_ref[...] + _sigmoid(gb_ref[...]) * pb_ref[...]).astype(BF16)

    t = pl.BlockSpec((tr, tc), lambda i, j: (i, j))
    return pl.pallas_call(
        body, name="merge", grid=(T // tr, nd),
        in_specs=[pl.BlockSpec((tr, tc), lambda i, j: (i, g0 + j)),
                  pl.BlockSpec((tr, tc), lambda i, j: (i, g0 + nd + j)), t, t],
        out_specs=t, out_shape=jax.ShapeDtypeStruct((T, D), BF16),
        compiler_params=_cparams(("parallel", "parallel")),
    )(z, z, pa, pb)


def _dmerge(dm, z, pa, pb):
    T, D = pa.shape
    tr, tc, g0, nd = _gate_tiles(T, D)

    def body(dm_ref, ga_ref, gb_ref, pa_ref, pb_ref, dpa_ref, dpb_ref, dga_ref, dgb_ref):
        dmv = dm_ref[...]
        sa, sb = _sigmoid(ga_ref[...]), _sigmoid(gb_ref[...])
        dpa_ref[...] = (dmv * sa).astype(BF16)
        dpb_ref[...] = (dmv * sb).astype(BF16)
        dga_ref[...] = (dmv * pa_ref[...] * sa * (1.0 - sa)).astype(BF16)
        dgb_ref[...] = (dmv * pb_ref[...] * sb * (1.0 - sb)).astype(BF16)

    t = pl.BlockSpec((tr, tc), lambda i, j: (i, j))
    return pl.pallas_call(
        body, name="dmerge", grid=(T // tr, nd),
        in_specs=[t, pl.BlockSpec((tr, tc), lambda i, j: (i, g0 + j)),
                  pl.BlockSpec((tr, tc), lambda i, j: (i, g0 + nd + j)), t, t],
        out_specs=[t, t, t, t],
        out_shape=[jax.ShapeDtypeStruct((T, D), BF16)] * 4,
        compiler_params=_cparams(("parallel", "parallel")),
    )(dm, z, z, pa, pb)


def _hg_gates(xq, xf, lb):
    f = _sigmoid(xf)
    g = lb + (1.0 - lb) * f
    sq = _sigmoid(xq)
    return f, g, jnp.log(g), 1.0 - g, sq, xq * sq * (HG_D ** -0.5)


def _hg_decays(lg, tri_incl, rowi):
    b = _nn(tri_incl, lg, precision=HIGHEST)
    b_last = jnp.sum(lg, axis=0, keepdims=True)
    b_mid = jnp.sum(jnp.where(rowi <= CHUNK // 2, lg, 0.0), axis=0, keepdims=True)
    return b, b_last, b_mid


def _hg_in_specs(T):
    H = HG_HEADS
    return [pl.BlockSpec((T, HG_D), lambda h, s=s: (0, s * H + h)) for s in range(4)]


def _hg_fwd(z, lb_logits, hgw, side=None):
    T = z.shape[0]
    H, d, C = HG_HEADS, HG_D, CHUNK
    nc = T // C

    def body(hq_ref, hf_ref, hi_ref, hg_ref, lbl_ref, w_ref, ya_ref, o_ref, s_ref):
        lb = 1.0 / (1.0 + jnp.exp(lbl_ref[1:2, :] - lbl_ref[0:1, :]))
        wv = w_ref[...]
        row = lax.broadcasted_iota(jnp.int32, (C, C), 0)
        col = lax.broadcasted_iota(jnp.int32, (C, C), 1)
        tril = col <= row
        tri_incl = tril.astype(F32)
        rowi = lax.broadcasted_iota(jnp.int32, (C, d), 0)

        def chunk(c, st):
            rows = pl.ds(pl.multiple_of(c * C, C), C)
            xq, xf, v, xg = hq_ref[rows, :], hf_ref[rows, :], hi_ref[rows, :], hg_ref[rows, :]
            _, _, lg, kk, _, q = _hg_gates(xq, xf, lb)
            b, b_last, b_mid = _hg_decays(lg, tri_incl, rowi)
            st_b = st.astype(BF16)
            s_ref[c] = st
            vb = v.astype(BF16)
            o = _nt((q * jnp.exp(b)).astype(BF16), st_b)
            qt = (q * jnp.exp(b - b_mid)).astype(BF16)
            kt = (kk * jnp.exp(jnp.minimum(b_mid - b, EXP_CLAMP))).astype(BF16)
            a = jnp.where(tril, _nt(qt, kt), 0.0).astype(BF16)
            o = o + _nn(a, vb)
            st_new = st * jnp.exp(b_last) + _tn(v, kk * jnp.exp(b_last - b), HIGHEST)
            o_ref[rows, :] = o
            r = lax.rsqrt(jnp.mean(o * o, axis=-1, keepdims=True) + EPS)
            ya_ref[rows, :] = (o * r * wv * (xg * _sigmoid(xg))).astype(BF16)
            return st_new

        lax.fori_loop(0, nc, chunk, jnp.zeros((d, d), F32))

    head = pl.BlockSpec((T, d), lambda h: (0, h))
    return _call_with_side(
        body, "hg_fwd", (H,),
        _hg_in_specs(T) + [pl.BlockSpec((2, d), lambda h: (0, h)), pl.BlockSpec((1, d), lambda h: (0, 0))],
        [head, head, pl.BlockSpec((None, nc, d, d), lambda h: (h, 0, 0, 0))],
        [jax.ShapeDtypeStruct((T, H * d), BF16), jax.ShapeDtypeStruct((T, H * d), F32),
         jax.ShapeDtypeStruct((H, nc, d, d), F32)],
        [], ("parallel",), (z, z, z, z, lb_logits, hgw), side)


def _hg_bwd(z, o, dya, states, lb_logits, hgw, side=None):
    T = z.shape[0]
    H, d, C = HG_HEADS, HG_D, CHUNK
    nc = T // C
    scale = HG_D ** -0.5

    def body(hq_ref, hf_ref, hi_ref, hg_ref, o_ref, dy_ref, s_ref, lbl_ref, w_ref,
             dq_ref, df_ref, di_ref, dg_ref, dlbl_ref, dw_ref, acc_ref):
        lb = 1.0 / (1.0 + jnp.exp(lbl_ref[1:2, :] - lbl_ref[0:1, :]))
        wv = w_ref[...]
        row = lax.broadcasted_iota(jnp.int32, (C, C), 0)
        col = lax.broadcasted_iota(jnp.int32, (C, C), 1)
        tril = col <= row
        tri_incl = tril.astype(F32)
        triu_incl = (col >= row).astype(F32)
        rowi = lax.broadcasted_iota(jnp.int32, (C, d), 0)
        acc_ref[...] = jnp.zeros_like(acc_ref)

        @pl.when(pl.program_id(0) == 0)
        def _():
            dw_ref[...] = jnp.zeros_like(dw_ref)

        def chunk(i, carry):
            dst, tail = carry
            c = nc - 1 - i
            rows = pl.ds(pl.multiple_of(c * C, C), C)
            xq, xf, v, xg = hq_ref[rows, :], hf_ref[rows, :], hi_ref[rows, :], hg_ref[rows, :]
            f, g, lg, kk, sq, q = _hg_gates(xq, xf, lb)
            b, b_last, b_mid = _hg_decays(lg, tri_incl, rowi)
            e_b, e_qm, e_km = jnp.exp(b), jnp.exp(b - b_mid), jnp.exp(jnp.minimum(b_mid - b, EXP_CLAMP))
            e_kl, e_last = jnp.exp(b_last - b), jnp.exp(b_last)
            ov, dy = o_ref[rows, :], dy_ref[rows, :]
            r = lax.rsqrt(jnp.mean(ov * ov, axis=-1, keepdims=True) + EPS)
            xhat = ov * r
            sg = _sigmoid(xg)
            dxg = dy * xhat * wv * (sg * (1.0 + xg * (1.0 - sg)))
            dyn = dy * (xg * sg)
            acc_ref[0:1, :] += jnp.sum(dyn * xhat, axis=0, keepdims=True)
            dxh = dyn * wv
            dof = r * (dxh - xhat * jnp.mean(dxh * xhat, axis=-1, keepdims=True))
            do, vb = dof.astype(BF16), v.astype(BF16)
            qt, kt = q * e_qm, kk * e_km
            pm = jnp.where(tril, _nt(do, vb), 0.0)
            am = jnp.where(tril, _nt(qt.astype(BF16), kt.astype(BF16)), 0.0).astype(BF16)
            dq = _nn(dof, s_ref[c], HIGHEST) * e_b + _nn(pm, kt, HIGHEST) * e_qm
            dk = _tn(pm, qt, HIGHEST) * e_km + _nn(v, dst, HIGHEST) * e_kl
            dv = _tn(am, do) + _nt((kk * e_kl).astype(BF16), dst.astype(BF16))
            dst_prev = dst * e_last + _tn(dof, q * e_b, HIGHEST)
            db = q * dq - kk * dk
            dlg = _nn(triu_incl, db, precision=HIGHEST) + tail
            dgate = dlg / g - dk
            acc_ref[1:2, :] += jnp.sum(dgate * (1.0 - f), axis=0, keepdims=True)
            dq_ref[rows, :] = (dq * scale * (sq * (1.0 + xq * (1.0 - sq)))).astype(BF16)
            df_ref[rows, :] = (dgate * (1.0 - lb) * f * (1.0 - f)).astype(BF16)
            di_ref[rows, :] = dv.astype(BF16)
            dg_ref[rows, :] = dxg.astype(BF16)
            return dst_prev, tail + jnp.sum(db, axis=0, keepdims=True)

        lax.fori_loop(0, nc, chunk, (jnp.zeros((d, d), F32), jnp.zeros((1, d), F32)))
        dw_ref[...] += acc_ref[0:1, :]
        dl0 = acc_ref[1:2, :] * lb * (1.0 - lb)
        dlbl_ref[0:1, :] = dl0
        dlbl_ref[1:2, :] = -dl0

    head = pl.BlockSpec((T, d), lambda h: (0, h))
    return _call_with_side(
        body, "hg_bwd", (H,),
        _hg_in_specs(T) + [head, head, pl.BlockSpec((None, nc, d, d), lambda h: (h, 0, 0, 0)),
                           pl.BlockSpec((2, d), lambda h: (0, h)), pl.BlockSpec((1, d), lambda h: (0, 0))],
        [head, head, head, head, pl.BlockSpec((2, d), lambda h: (0, h)), pl.BlockSpec((1, d), lambda h: (0, 0))],
        [jax.ShapeDtypeStruct((T, H * d), BF16)] * 4 + [jax.ShapeDtypeStruct((2, H * d), F32),
                                                        jax.ShapeDtypeStruct((1, d), F32)],
        [pltpu.VMEM((8, d), F32)], ("arbitrary",), (z, z, z, z, o, dya, states, lb_logits, hgw), side)


def _at_dims():
    pad = LEFT * CHUNK
    return pad, QB + pad, AT_HEADS * AT_DH // LANE, 4 * _hgw() // LANE


def _rel_of_period():
    pad, W, _, _ = _at_dims()
    n = jnp.arange(QB + W)
    return jnp.clip(pad - jnp.where(n < W, n, n - (QB + W)), -REL_CLIP, REL_CLIP) + REL_CLIP


def _bias_window(rel_bias):
    pad, W, _, _ = _at_dims()
    H, P = rel_bias.shape[0], QB + W
    per = rel_bias[:, _rel_of_period()]
    win = jnp.tile(per, (1, QB))[:, :QB * (P - 1)].reshape(H, QB, P - 1)[:, :, :W]
    t = jnp.arange(QB)[:, None]
    j = jnp.arange(W)[None, :]
    ok = (j // CHUNK >= t // CHUNK) & (j // CHUNK <= t // CHUNK + LEFT)
    return jnp.where(ok[None], win, NEG)


def _bias_window_grad(dbw):
    pad, W, _, _ = _at_dims()
    H, P = dbw.shape[0], QB + W
    flat = jnp.pad(dbw, ((0, 0), (0, 0), (0, P - 1 - W))).reshape(H, QB * (P - 1))
    per = jnp.pad(flat, ((0, 0), (0, QB))).reshape(H, QB, P).sum(axis=1)
    onehot = _rel_of_period()[:, None] == jnp.arange(2 * REL_CLIP + 1)[None, :]
    return jnp.dot(per, onehot.astype(F32), precision=HIGHEST)


def _at_softmax(q_half, kw, bias, valid):
    s = _nt(q_half, kw) * (AT_DH ** -0.5) + bias
    s = jnp.where(valid, s, NEG)
    e = jnp.exp(s - jnp.max(s, axis=-1, keepdims=True))
    return e / jnp.sum(e, axis=-1, keepdims=True)


def _at_fwd(z, bias_win, side=None):
    T = z.shape[0]
    pad, W, HP, c0 = _at_dims()
    nq = T // QB

    def body(q_ref, k_ref, v_ref, bias_ref, o_ref, kpad, vpad):
        qi = pl.program_id(1)

        @pl.when(qi == 0)
        def _():
            kpad[0:pad, :] = jnp.zeros((pad, LANE), BF16)
            vpad[0:pad, :] = jnp.zeros((pad, LANE), BF16)
            kpad[pad:, :] = k_ref[...].astype(BF16)
            vpad[pad:, :] = v_ref[...].astype(BF16)

        win = pl.ds(pl.multiple_of(qi * QB, QB), W)
        kw, vw = kpad[win, :], vpad[win, :]
        q = q_ref[...]
        lane = lax.broadcasted_iota(jnp.int32, (QB, LANE), 1)
        first = lane < AT_DH
        valid = lax.broadcasted_iota(jnp.int32, (QB, W), 1) + qi * QB >= pad
        pa = _at_softmax(jnp.where(first, q, 0.0).astype(BF16), kw, bias_ref[0], valid)
        pb = _at_softmax(jnp.where(first, 0.0, q).astype(BF16), kw, bias_ref[1], valid)
        o_ref[...] = jnp.where(first, _nn(pa.astype(BF16), vw), _nn(pb.astype(BF16), vw)).astype(BF16)

    full = lambda s: pl.BlockSpec((T, LANE), lambda hp, qi, s=s: (0, c0 + s * HP + hp))
    return _call_with_side(
        body, "at_fwd", (HP, nq),
        [pl.BlockSpec((QB, LANE), lambda hp, qi: (qi, c0 + hp)), full(1), full(2),
         pl.BlockSpec((2, QB, W), lambda hp, qi: (hp, 0, 0))],
        [pl.BlockSpec((QB, LANE), lambda hp, qi: (qi, hp))],
        [jax.ShapeDtypeStruct((T, HP * LANE), BF16)],
        [pltpu.VMEM((T + pad, LANE), BF16)] * 2, ("parallel", "arbitrary"), (z, z, z, bias_win), side)


def _at_bwd(z, dyb, bias_win, side=None):
    T = z.shape[0]
    pad, W, HP, c0 = _at_dims()
    nq = T // QB
    scale = AT_DH ** -0.5

    def body(q_ref, k_ref, v_ref, do_ref, bias_ref, dq_ref, dk_ref, dv_ref, dbias_ref, kpad, vpad, dkpad, dvpad):
        qi = pl.program_id(1)

        @pl.when(qi == 0)
        def _():
            kpad[0:pad, :] = jnp.zeros((pad, LANE), BF16)
            vpad[0:pad, :] = jnp.zeros((pad, LANE), BF16)
            kpad[pad:, :] = k_ref[...].astype(BF16)
            vpad[pad:, :] = v_ref[...].astype(BF16)
            dkpad[...] = jnp.zeros_like(dkpad)
            dvpad[...] = jnp.zeros_like(dvpad)
            dbias_ref[...] = jnp.zeros_like(dbias_ref)

        win = pl.ds(pl.multiple_of(qi * QB, QB), W)
        kw, vw = kpad[win, :], vpad[win, :]
        q, do = q_ref[...], do_ref[...]
        lane = lax.broadcasted_iota(jnp.int32, (QB, LANE), 1)
        first = lane < AT_DH
        valid = lax.broadcasted_iota(jnp.int32, (QB, W), 1) + qi * QB >= pad

        def half(hh, qh, doh):
            p = _at_softmax(qh, kw, bias_ref[hh], valid)
            dp = _nt(doh, vw)
            ds = p * (dp - jnp.sum(p * dp, axis=-1, keepdims=True))
            dbias_ref[hh] += ds
            dss = (ds * scale).astype(BF16)
            return _nn(dss, kw), _tn(dss, qh), _tn(p.astype(BF16), doh)

        dqa, dka, dva = half(0, jnp.where(first, q, 0.0).astype(BF16), jnp.where(first, do, 0.0).astype(BF16))
        dqb, dkb, dvb = half(1, jnp.where(first, 0.0, q).astype(BF16), jnp.where(first, 0.0, do).astype(BF16))
        dq_ref[...] = jnp.where(first, dqa, dqb).astype(BF16)
        dkpad[win, :] += dka + dkb
        dvpad[win, :] += dva + dvb

        @pl.when(qi == nq - 1)
        def _():
            dk_ref[...] = dkpad[pad:, :].astype(BF16)
            dv_ref[...] = dvpad[pad:, :].astype(BF16)

    full = lambda s: pl.BlockSpec((T, LANE), lambda hp, qi, s=s: (0, c0 + s * HP + hp))
    blk = pl.BlockSpec((QB, LANE), lambda hp, qi: (qi, hp))
    col = pl.BlockSpec((T, LANE), lambda hp, qi: (0, hp))
    bw = pl.BlockSpec((2, QB, W), lambda hp, qi: (hp, 0, 0))
    return _call_with_side(
        body, "at_bwd", (HP, nq),
        [pl.BlockSpec((QB, LANE), lambda hp, qi: (qi, c0 + hp)), full(1), full(2), blk, bw],
        [blk, col, col, bw],
        [jax.ShapeDtypeStruct((T, HP * LANE), BF16)] * 3 + [jax.ShapeDtypeStruct(bias_win.shape, F32)],
        [pltpu.VMEM((T + pad, LANE), BF16)] * 2 + [pltpu.VMEM((T + pad, LANE), F32)] * 2,
        ("parallel", "arbitrary"), (z, z, z, dyb, bias_win), side)


def _piece_tiles(name, full_shape):
    pr, pc = _piece_shape(name, full_shape)
    tr = min(ROW_TILE, pr)
    assert pr % tr == 0
    nt = pr // tr
    if name in ROW_SHARDED:
        return tr, nt, lambda q, half, i: ((2 * q + half) * nt + i, 0)
    return tr, nt, lambda q, half, i: (half * nt + i, q)


def _cast_into_full(name, wq, place):
    full = _full_shape(name, wq.shape)
    pc = wq.shape[1]
    tr, nt, at = _piece_tiles(name, full)

    def body(place_ref, w_ref, o_ref):
        o_ref[...] = w_ref[...].astype(BF16)

    return pl.pallas_call(
        body, name="cast_" + name,
        grid_spec=pltpu.PrefetchScalarGridSpec(
            num_scalar_prefetch=1, grid=(2, nt),
            in_specs=[pl.BlockSpec((tr, pc), lambda h, i, s: (h * nt + i, 0))],
            out_specs=pl.BlockSpec((tr, pc), lambda h, i, s: at(s[0], h, i))),
        out_shape=jax.ShapeDtypeStruct(full, BF16),
        compiler_params=_cparams(("parallel", "parallel")),
    )(place, wq)


def _chip_sum(name, grad, theirs, place):
    pr, pc = theirs.shape[1:]
    tr, nt, at = _piece_tiles(name, grad.shape)

    def body(place_ref, g_ref, t_ref, o_ref):
        o_ref[...] = (g_ref[...].astype(F32) + t_ref[...].astype(F32)).astype(BF16)

    piece = pl.BlockSpec((None, tr, pc), lambda q, i, s: (q, i, 0))
    return pl.pallas_call(
        body, name="chip_sum_" + name,
        grid_spec=pltpu.PrefetchScalarGridSpec(
            num_scalar_prefetch=1, grid=(4, nt),
            in_specs=[pl.BlockSpec((tr, pc), lambda q, i, s: at(q, s[1], i)), piece], out_specs=piece),
        out_shape=jax.ShapeDtypeStruct(theirs.shape, BF16),
        compiler_params=_cparams(("parallel", "parallel")),
    )(place, grad, theirs)


def _piece_sum(name, chip_sums, got, place):
    pr, pc = chip_sums.shape[1:]
    tr = min(ROW_TILE, pr)

    def body(place_ref, own_ref, got_ref, o_ref):
        o_ref[...] = (own_ref[...].astype(F32) + got_ref[0].astype(F32) + got_ref[1].astype(F32)
                      + got_ref[2].astype(F32))

    return pl.pallas_call(
        body, name="piece_sum_" + name,
        grid_spec=pltpu.PrefetchScalarGridSpec(
            num_scalar_prefetch=1, grid=(pr // tr,),
            in_specs=[pl.BlockSpec((None, tr, pc), lambda i, s: (s[0], i, 0)),
                      pl.BlockSpec((3, tr, pc), lambda i, s: (0, i, 0))],
            out_specs=pl.BlockSpec((tr, pc), lambda i, s: (i, 0))),
        out_shape=jax.ShapeDtypeStruct((pr, pc), F32),
        compiler_params=_cparams(("parallel",)),
    )(place, chip_sums, got)


def _adam_quarter(name, w, m, v, g_mine, g_sib, place):
    pr, pc = g_mine.shape
    tr = min(ROW_TILE // 2, pr)
    nt = pr // tr

    def body(place_ref, w_ref, m_ref, v_ref, gm_ref, gs_ref, go_ref, d_ref, mo_ref, vo_ref):
        g = jnp.where(pl.program_id(0) == place_ref[1], gm_ref[...], gs_ref[...])
        delta, mn, vn = _adam_math(w_ref[...], g, m_ref[...], v_ref[...])
        go_ref[...] = g
        d_ref[...] = delta
        mo_ref[...] = mn
        vo_ref[...] = vn

    quarter = pl.BlockSpec((tr, pc), lambda h, i, s: (h * nt + i, 0))
    mine = pl.BlockSpec((tr, pc), lambda h, i, s: (jnp.where(h == s[1], i, 0), 0))
    sib = pl.BlockSpec((tr, pc), lambda h, i, s: (jnp.where(h == s[1], 0, i), 0))
    return pl.pallas_call(
        body, name="adam_" + name,
        grid_spec=pltpu.PrefetchScalarGridSpec(
            num_scalar_prefetch=1, grid=(2, nt),
            in_specs=[quarter, quarter, quarter, mine, sib], out_specs=[quarter] * 4),
        out_shape=[jax.ShapeDtypeStruct(w.shape, F32)] * 4,
        compiler_params=_cparams(("parallel", "parallel")),
    )(place, w, m, v, g_mine, g_sib)


def _adam_math(w, g, m, v):
    m = ADAM_B1 * m + (1.0 - ADAM_B1) * g
    v = ADAM_B2 * v + (1.0 - ADAM_B2) * (g * g)
    m_hat = m / (1.0 - ADAM_B1 ** ADAM_STEP)
    v_hat = v / (1.0 - ADAM_B2 ** ADAM_STEP)
    return -ADAM_LR * (m_hat / (jnp.sqrt(v_hat) + ADAM_EPS) + ADAM_WD * w), m, v


WEIGHTS = ("w_in", "w_branch_a", "w_branch_b", "w_out", "w_up", "w_down")
ROW_SHARDED = ("w_out", "w_down")
ANY = pl.BlockSpec(memory_space=pl.ANY)
MESH = pl.DeviceIdType.MESH


def _place():
    x, y, c = lax.axis_index("x"), lax.axis_index("y"), lax.axis_index("c")
    chips = [(1 - x, y), (x, 1 - y), (1 - x, 1 - y)]
    return x, y, c, 2 * x + y, chips, [2 * cx + cy for cx, cy in chips]


def _piece(full_ref, name, q, half):
    K, N = full_ref.shape
    if name in ROW_SHARDED:
        rows = K // 8
        return full_ref.at[pl.ds(q * (2 * rows) + half * rows, rows), :]
    return full_ref.at[pl.ds(half * (K // 2), K // 2), pl.ds(q * (N // 4), N // 4)]


def _piece_shape(name, full_shape):
    K, N = full_shape
    return (K // 8, N) if name in ROW_SHARDED else (K // 2, N // 4)


def _full_shape(name, quarter_shape):
    Kq, Nq = quarter_shape
    return (4 * Kq, Nq) if name in ROW_SHARDED else (Kq, 4 * Nq)


def _remote(src, dst, send_sem, recv_sem, device):
    return pltpu.make_async_remote_copy(src_ref=src, dst_ref=dst, send_sem=send_sem, recv_sem=recv_sem,
                                        device_id=device, device_id_type=MESH)


def _gather_weights(names, fulls):
    n = len(names)

    def body(*refs):
        f_refs = refs[n:2 * n]
        send_sems, recv_sems = refs[2 * n:]
        x, y, c, p, chips, chip_ids = _place()
        sib = (x, y, 1 - c)
        sends = []
        for i, name in enumerate(names):
            mine = _piece(f_refs[i], name, p, c)
            for j, chip in enumerate(chips):
                cp = _remote(mine, mine, send_sems.at[i, j], recv_sems.at[i, j], (*chip, c))
                cp.start()
                sends.append(cp)
        for i, name in enumerate(names):
            for j, cid in enumerate(chip_ids):
                landed = _piece(f_refs[i], name, cid, c)
                _remote(landed, landed, send_sems.at[i, j], recv_sems.at[i, j], sib).wait_recv()
                cp = _remote(landed, landed, send_sems.at[i, 3 + j], recv_sems.at[i, 3 + j], sib)
                cp.start()
                sends.append(cp)
        for i, name in enumerate(names):
            for j, cid in enumerate(chip_ids):
                other = _piece(f_refs[i], name, cid, 1 - c)
                _remote(other, other, send_sems.at[i, 3 + j], recv_sems.at[i, 3 + j], sib).wait_recv()
        for cp in sends:
            cp.wait_send()

    return pl.pallas_call(
        body, name="gather_" + "_".join(names),
        in_specs=[ANY] * n, out_specs=[ANY] * n,
        out_shape=[jax.ShapeDtypeStruct(f.shape, BF16) for f in fulls],
        input_output_aliases={i: i for i in range(n)},
        scratch_shapes=[pltpu.SemaphoreType.DMA((n, 6)), pltpu.SemaphoreType.DMA((n, 6))],
    )(*fulls)


def _ici_gather(names, fulls):
    def build(reads, aliased, fresh, send_sems, recv_sems, off=0):
        _, _, c, p, chips, _ = _place()
        out = []
        for i, (ref, name) in enumerate(zip(aliased, names)):
            mine = _piece(ref, name, p, c)
            for j, chip in enumerate(chips):
                k = off + 3 * i + j
                out.append(_remote(mine, mine, send_sems.at[k], recv_sems.at[k], (*chip, c)))
        return out

    return _Side(build, 3 * len(names), aliased=fulls)


def _d2d_gather(names, fulls):
    def build(reads, aliased, fresh, send_sems, recv_sems, off=0):
        x, y, c, _, _, chip_ids = _place()
        out = []
        for i, (ref, name) in enumerate(zip(aliased, names)):
            for j, cid in enumerate(chip_ids):
                landed, k = _piece(ref, name, cid, c), off + 3 * i + j
                out.append(_remote(landed, landed, send_sems.at[k], recv_sems.at[k], (x, y, 1 - c)))
        return out

    return _Side(build, 3 * len(names), aliased=fulls)


def _sib_send(names, grads):
    def build(reads, aliased, fresh, send_sems, recv_sems, off=0):
        x, y, c, _, _, _ = _place()
        out = []
        for i, name in enumerate(names):
            for q in range(4):
                k = off + 4 * i + q
                out.append(_remote(_piece(reads[i], name, q, 1 - c), fresh[i].at[q], send_sems.at[k], recv_sems.at[k],
                                   (x, y, 1 - c)))
        return out

    shapes = [jax.ShapeDtypeStruct((4,) + _piece_shape(name, g.shape), BF16) for name, g in zip(names, grads)]
    return _Side(build, 4 * len(names), reads=grads, fresh=shapes)


def _chip_exchange(chip_sums):
    def build(reads, aliased, fresh, send_sems, recv_sems, off=0):
        _, _, c, _, chips, chip_ids = _place()
        out = []
        for i in range(len(chip_sums)):
            for j, (chip, cid) in enumerate(zip(chips, chip_ids)):
                k = off + 3 * i + j
                out.append(_remote(reads[i].at[cid], fresh[i].at[j], send_sems.at[k], recv_sems.at[k], (*chip, c)))
        return out

    shapes = [jax.ShapeDtypeStruct((3,) + s.shape[1:], BF16) for s in chip_sums]
    return _Side(build, 3 * len(chip_sums), reads=chip_sums, fresh=shapes)


def _sib_share(halves):
    def build(reads, aliased, fresh, send_sems, recv_sems, off=0):
        x, y, c, _, _, _ = _place()
        return [_remote(reads[i], fresh[i], send_sems.at[off + i], recv_sems.at[off + i], (x, y, 1 - c))
                for i in range(len(halves))]

    return _Side(build, len(halves), reads=halves, fresh=[jax.ShapeDtypeStruct(h.shape, F32) for h in halves])


def _join(a, b):
    def build(reads, aliased, fresh, send_sems, recv_sems, off=0):
        ra, aa, fa = len(a.reads), len(a.aliased), len(a.fresh)
        return (a.build(reads[:ra], aliased[:aa], fresh[:fa], send_sems, recv_sems, off)
                + b.build(reads[ra:], aliased[aa:], fresh[fa:], send_sems, recv_sems, off + a.nsem))

    return _Side(build, a.nsem + b.nsem, a.reads + b.reads, a.aliased + b.aliased, a.fresh + b.fresh)


def _run_side(name, side):
    nr, na = len(side.reads), len(side.aliased)

    def body(*refs):
        n_in, n_out = nr + na, na + len(side.fresh)
        outs = refs[n_in:n_in + n_out]
        copies = side.build(refs[:nr], outs[:na], outs[na:], *refs[-2:])
        for cp in copies:
            cp.start()
        for cp in copies:
            cp.wait()

    return pl.pallas_call(
        body, name=name, in_specs=side.in_specs(), out_specs=side.out_specs(), out_shape=side.out_shape(),
        input_output_aliases=side.aliases(0, 0), scratch_shapes=side.scratch(),
    )(*side.operands())


def _small_allreduce_adam(gpart, w, m, v):
    R = gpart.shape[0]

    def body(g_ref, w_ref, m_ref, v_ref, go_ref, d_ref, mo_ref, vo_ref, buf, send_sems, recv_sems):
        x, y, c = lax.axis_index("x"), lax.axis_index("y"), lax.axis_index("c")
        me = 4 * x + 2 * y + c
        buf[me] = g_ref[...]
        copies = []
        for k in range(1, 8):
            fx, fy, fc = (k >> 2) & 1, (k >> 1) & 1, k & 1
            peer = (1 - x if fx else x, 1 - y if fy else y, 1 - c if fc else c)
            cp = _remote(g_ref, buf.at[me], send_sems.at[k - 1], recv_sems.at[k - 1], peer)
            cp.start()
            copies.append((cp, 4 * peer[0] + 2 * peer[1] + peer[2]))
        for k, (cp, pid) in enumerate(copies):
            _remote(g_ref, buf.at[pid], send_sems.at[k], recv_sems.at[k], (x, y, c)).wait_recv()
        for cp, _ in copies:
            cp.wait_send()
        g = buf[0]
        for d in range(1, 8):
            g = g + buf[d]
        delta, mn, vn = _adam_math(w_ref[...], g, m_ref[...], v_ref[...])
        go_ref[...] = g
        d_ref[...] = delta
        mo_ref[...] = mn
        vo_ref[...] = vn

    vm = pl.BlockSpec(memory_space=pltpu.VMEM)
    return pl.pallas_call(
        body, name="small_allreduce_adam",
        in_specs=[vm] * 4, out_specs=[vm] * 4,
        out_shape=[jax.ShapeDtypeStruct((R, LANE), F32)] * 4,
        scratch_shapes=[pltpu.VMEM((8, R, LANE), F32), pltpu.SemaphoreType.DMA((7,)), pltpu.SemaphoreType.DMA((7,))],
    )(gpart, w, m, v)


def _pack(arrs):
    flat = jnp.concatenate([a.reshape(-1).astype(F32) for a in arrs])
    rows = -(-flat.shape[0] // (8 * LANE)) * 8
    return jnp.pad(flat, (0, rows * LANE - flat.shape[0])).reshape(rows, LANE)


def _unpack(packed, like):
    flat, out, off = packed.reshape(-1), [], 0
    for a in like:
        out.append(flat[off:off + a.size].reshape(a.shape))
        off += a.size
    return out


def kernel(x, w_in, lb_logits, hg_norm_w, rel_bias, w_branch_a, w_branch_b, w_out, norm_mix_w, norm_mlp_w, w_up, w_down, norm_final_w, loss_target, m_w_in, m_lb_logits, m_hg_norm_w, m_rel_bias, m_w_branch_a, m_w_branch_b, m_w_out, m_norm_mix_w, m_norm_mlp_w, m_w_up, m_w_down, m_norm_final_w, v_w_in, v_lb_logits, v_hg_norm_w, v_rel_bias, v_w_branch_a, v_w_branch_b, v_w_out, v_norm_mix_w, v_norm_mlp_w, v_w_up, v_w_down, v_norm_final_w):
    T, D = x.shape[1], x.shape[2]
    x2, tgt = x.reshape(T, D), loss_target.reshape(T, D)
    big = dict(w_in=(w_in, m_w_in, v_w_in), w_branch_a=(w_branch_a, m_w_branch_a, v_w_branch_a),
               w_branch_b=(w_branch_b, m_w_branch_b, v_w_branch_b), w_out=(w_out, m_w_out, v_w_out),
               w_up=(w_up, m_w_up, v_w_up), w_down=(w_down, m_w_down, v_w_down))
    big = {k: tuple(a[0] for a in v) for k, v in big.items()}
    nfw = norm_final_w.reshape(1, D)

    place = jnp.stack([2 * lax.axis_index("x") + lax.axis_index("y"), lax.axis_index("c")]).astype(jnp.int32)
    Wf = {name: _cast_into_full(name, big[name][0], place) for name in WEIGHTS}
    small3 = ["w_branch_a", "w_branch_b", "w_out"]
    (Wf["w_in"],) = _gather_weights(["w_in"], [Wf["w_in"]])

    u1 = _rms_fwd("norm_mix", x2, norm_mix_w)
    z, Wf["w_up"] = _mm("z_proj", u1, Wf["w_in"], "nn", [F32], side=_ici_gather(["w_up"], [Wf["w_up"]]))
    ya, o_hg, states, Wf["w_down"], Wf["w_up"] = _hg_fwd(
        z, lb_logits, hg_norm_w, side=_join(_ici_gather(["w_down"], [Wf["w_down"]]), _d2d_gather(["w_up"], [Wf["w_up"]])))
    bias_win = _bias_window(rel_bias[0])
    yb, *moved = _at_fwd(z, bias_win, side=_join(_ici_gather(small3, [Wf[n] for n in small3]),
                                                _d2d_gather(["w_down"], [Wf["w_down"]])))
    Wf["w_down"] = moved[3]
    Wf.update(zip(small3, _run_side("pass_small_weights", _d2d_gather(small3, moved[:3]))))
    pa = _mm("branch_a", ya, Wf["w_branch_a"], "nn", [F32])
    pb = _mm("branch_b", yb, Wf["w_branch_b"], "nn", [F32])
    merged = _merge(z, pa, pb)
    add = lambda acc, res: (acc + res,)
    h1 = _mm("out_proj", merged, Wf["w_out"], "nn", [F32], extras=[x2], epilogue=add)
    u2 = _rms_fwd("norm_mlp", h1, norm_mlp_w)
    relu2 = lambda acc: (acc, jnp.square(jnp.maximum(acc, 0.0)))
    a_pre, act = _mm("mlp_up", u2, Wf["w_up"], "nn", [F32, BF16], epilogue=relu2)
    h2 = _mm("mlp_down", act, Wf["w_down"], "nn", [F32], extras=[h1], epilogue=add)
    loss_part, dh2, dh2b, d_nf = _loss_head(h2, tgt, nfw)

    drelu2 = lambda acc, a: (acc * (2.0 * jnp.maximum(a, 0.0)),)
    da = _mm("d_act", dh2b, Wf["w_down"], "nt", [BF16], extras=[a_pre], epilogue=drelu2)
    G = {}
    G["w_down"] = _mm("g_w_down", act, dh2b, "tn", [BF16])
    G["w_up"] = _mm("g_w_up", u2, da, "tn", [BF16])
    T_, S_, GOT = {}, {}, {}
    du2, T_["w_down"], T_["w_up"] = _mm("d_u2", da, Wf["w_up"], "nt", [F32],
                                        side=_sib_send(["w_down", "w_up"], [G["w_down"], G["w_up"]]))
    for n in ("w_down", "w_up"):
        S_[n] = _chip_sum(n, G[n], T_[n], place)
    dh1, dh1b, d_nmlp = _rms_bwd("norm_mlp_bwd", du2, h1, norm_mlp_w, dh2)
    dmerged = _mm("d_merged", dh1b, Wf["w_out"], "nt", [F32])
    G["w_out"] = _mm("g_w_out", merged, dh1b, "tn", [BF16])
    dpa, dpb, dz_ga, dz_gb = _dmerge(dmerged, z, pa, pb)
    dya = _mm("d_ya", dpa, Wf["w_branch_a"], "nt", [F32])
    dyb = _mm("d_yb", dpb, Wf["w_branch_b"], "nt", [F32])
    G["w_branch_a"] = _mm("g_w_a", ya, dpa, "tn", [BF16])
    G["w_branch_b"] = _mm("g_w_b", yb, dpb, "tn", [BF16])
    dz_q, dz_f, dz_i, dz_g, d_lbl, d_hgw, GOT["w_down"], *sent = _hg_bwd(
        z, o_hg, dya, states, lb_logits, hg_norm_w,
        side=_join(_chip_exchange([S_["w_down"]]), _sib_send(small3, [G[n] for n in small3])))
    for n, t in zip(small3, sent):
        S_[n] = _chip_sum(n, G[n], t, place)
    dz_aq, dz_ak, dz_av, dbias_win, GOT["w_up"] = _at_bwd(z, dyb, bias_win, side=_chip_exchange([S_["w_up"]]))
    dz = jnp.concatenate([dz_q, dz_f, dz_i, dz_g, dz_aq, dz_ak, dz_av, dz_ga, dz_gb], axis=1)
    G["w_in"], *got3 = _mm("g_w_in", u1, dz, "tn", [BF16], side=_chip_exchange([S_[n] for n in small3]))
    GOT.update(zip(small3, got3))
    (T_["w_in"],) = _run_side("send_w_in_to_sibling", _sib_send(["w_in"], [G["w_in"]]))
    S_["w_in"] = _chip_sum("w_in", G["w_in"], T_["w_in"], place)
    du1, GOT["w_in"] = _mm("d_u1", dz, Wf["w_in"], "nt", [F32], side=_chip_exchange([S_["w_in"]]))
    grad_x, _, d_nmix = _rms_bwd("norm_mix_bwd", du1, x2, norm_mix_w, dh1)
    d_rel = _bias_window_grad(dbias_win)

    halves = [_piece_sum(name, S_[name], GOT[name], place) for name in WEIGHTS]
    other = _run_side("share_with_sibling", _sib_share(halves))
    big_out = {}
    for name, g_mine, g_sib in zip(WEIGHTS, halves, other):
        outs = _adam_quarter(name, *big[name], g_mine, g_sib, place)
        big_out[name] = tuple(a[None] for a in outs)

    smalls = [("lb_logits", lb_logits, m_lb_logits, v_lb_logits, d_lbl),
              ("hg_norm_w", hg_norm_w, m_hg_norm_w, v_hg_norm_w, d_hgw),
              ("rel_bias", rel_bias, m_rel_bias, v_rel_bias, d_rel),
              ("norm_mix_w", norm_mix_w, m_norm_mix_w, v_norm_mix_w, d_nmix),
              ("norm_mlp_w", norm_mlp_w, m_norm_mlp_w, v_norm_mlp_w, d_nmlp),
              ("norm_final_w", norm_final_w, m_norm_final_w, v_norm_final_w, d_nf)]
    like = [s[1] for s in smalls]
    packed = _small_allreduce_adam(_pack([s[4] for s in smalls]), _pack(like), _pack([s[2] for s in smalls]),
                                   _pack([s[3] for s in smalls]))
    small_out = {s[0]: vals for s, vals in zip(smalls, zip(*[_unpack(p, like) for p in packed]))}

    loss = lax.psum(loss_part[0, 0], ("x", "y", "c"))
    order = ["w_in", "lb_logits", "hg_norm_w", "rel_bias", "w_branch_a", "w_branch_b", "w_out", "norm_mix_w",
             "norm_mlp_w", "w_up", "w_down", "norm_final_w"]
    res = {**big_out, **small_out}
    return (loss, grad_x.reshape(x.shape), *[res[n][0] for n in order], *[res[n][1] for n in order],
            *[res[n][2] for n in order], *[res[n][3] for n in order])
```

```python
import functools

import jax
import jax.numpy as jnp
from jax import lax
from jax.experimental import pallas as pl
from jax.experimental.pallas import tpu as pltpu

F32 = jnp.float32
BF16 = jnp.bfloat16
HIGHEST = lax.Precision.HIGHEST

D_MODEL = 2048
SEQ = 2048
CHUNK = 64
HG_HEADS = 8
HG_D = 128
AT_HEADS = 16
AT_DH = 64
LEFT = 8
REL_CLIP = 256
D_FF = 8192
EPS = 1e-6
ADAM_LR = 0.001
ADAM_B1 = 0.9
ADAM_B2 = 0.999
ADAM_EPS = 1e-08
ADAM_WD = 0.01
ADAM_STEP = 10

LANE = 128
NEG = -1e30
EXP_CLAMP = 80.0
VMEM_LIMIT = 48 * 1024 * 1024
MM_TM, MM_TN, MM_TK = 1024, 1024, 2048
ROW_TILE = 256
QB = 2 * CHUNK


def _hgw():
    return HG_HEADS * HG_D


def _atw():
    return AT_HEADS * AT_DH


def _cparams(sem):
    return pltpu.CompilerParams(dimension_semantics=sem, vmem_limit_bytes=VMEM_LIMIT)


def _sigmoid(x):
    return jax.nn.sigmoid(x)


def _dot(a, b, dims, precision=None):
    return lax.dot_general(a, b, (dims, ((), ())), preferred_element_type=F32, precision=precision)


def _nn(a, b, precision=None):
    return _dot(a, b, ((1,), (0,)), precision)


def _nt(a, b, precision=None):
    return _dot(a, b, ((1,), (1,)), precision)


def _tn(a, b, precision=None):
    return _dot(a, b, ((0,), (0,)), precision)


class _Side:
    def __init__(self, build, nsem, reads=(), aliased=(), fresh=()):
        self.build, self.nsem = build, nsem
        self.reads, self.aliased, self.fresh = list(reads), list(aliased), list(fresh)

    def operands(self):
        return self.reads + self.aliased

    def in_specs(self):
        return [ANY] * len(self.operands())

    def out_specs(self):
        return [ANY] * (len(self.aliased) + len(self.fresh))

    def out_shape(self):
        return [jax.ShapeDtypeStruct(a.shape, a.dtype) for a in self.aliased] + self.fresh

    def aliases(self, n_in, n_out):
        return {n_in + len(self.reads) + t: n_out + t for t in range(len(self.aliased))}

    def scratch(self):
        return [pltpu.SemaphoreType.DMA((self.nsem,)), pltpu.SemaphoreType.DMA((self.nsem,))]

    def hooks(self, in_refs, out_refs, sems, first, last):
        nr, na = len(self.reads), len(self.aliased)
        args = (in_refs[:nr], out_refs[:na], out_refs[na:], *sems)

        @pl.when(first)
        def _():
            for cp in self.build(*args):
                cp.start()

        @pl.when(last)
        def _():
            for cp in self.build(*args):
                cp.wait()


def _side_parts(side):
    if side is None:
        return [], [], [], [], lambda n_in, n_out: {}, []
    return side.operands(), side.in_specs(), side.out_specs(), side.out_shape(), side.aliases, side.scratch()


def _call_with_side(body, name, grid, in_specs, out_specs, out_shape, scratch, sem, operands, side, n_prefetch=0):
    s_ops, s_in, s_out, s_shape, s_alias, s_scr = _side_parts(side)
    n_in, n_out = n_prefetch + len(in_specs), len(out_specs)
    n_sin, n_sout = len(s_ops), len(s_out)

    def wrapped(*refs):
        a, b, c = n_in + n_sin, n_in + n_sin + n_out, n_in + n_sin + n_out + n_sout
        ids = [pl.program_id(d) for d in range(len(grid))]
        first = functools.reduce(lambda p, q: p & q, [i == 0 for i in ids])
        last = functools.reduce(lambda p, q: p & q, [i == g - 1 for i, g in zip(ids, grid)])
        side.hooks(refs[n_in:a], refs[b:c], refs[-2:], first, last)
        body(*refs[:n_in], *refs[a:b], *refs[c:-2])

    spec = dict(grid=grid, in_specs=in_specs + s_in, out_specs=out_specs + s_out, scratch_shapes=scratch + s_scr)
    if n_prefetch:
        spec = dict(grid_spec=pltpu.PrefetchScalarGridSpec(num_scalar_prefetch=n_prefetch, **spec))
    return pl.pallas_call(
        body if side is None else wrapped, name=name, out_shape=out_shape + s_shape,
        input_output_aliases=s_alias(n_in, n_out),
        compiler_params=_cparams(sem if side is None else ("arbitrary",) * len(grid)), **spec,
    )(*operands, *s_ops)


def _mm_tk(K):
    if K <= MM_TK:
        return K
    return MM_TK if K % MM_TK == 0 else MM_TK // 2


def _mm(name, a, b, mode, out_dtypes, extras=(), epilogue=None, side=None):
    if mode == "nn":
        (M, K), (K2, N) = a.shape, b.shape
    elif mode == "nt":
        (M, K), (N, K2) = a.shape, b.shape
    else:
        (K, M), (K2, N) = a.shape, b.shape
    assert K == K2, (name, a.shape, b.shape)
    tm, tn, tk = min(MM_TM, M), min(MM_TN, N), _mm_tk(K)
    assert M % tm == 0 and N % tn == 0 and K % tk == 0, (name, M, N, K)
    ni, nj, nk = M // tm, N // tn, K // tk
    ne, no = len(extras), len(out_dtypes)
    if epilogue is None:
        epilogue = lambda acc: (acc,)
    s_ops, s_in, s_out, s_shape, s_alias, s_scr = _side_parts(side)
    n_in, n_sin, n_sout = 2 + ne, len(s_ops), len(s_out)

    def body(*refs):
        a_ref, b_ref = refs[:2]
        extra_refs = refs[2:n_in]
        out_refs = refs[n_in + n_sin:n_in + n_sin + no]
        rest = refs[n_in + n_sin + no + n_sout:]
        i, j, k = pl.program_id(0), pl.program_id(1), pl.program_id(2)
        if side is not None:
            side.hooks(refs[n_in:n_in + n_sin], refs[n_in + n_sin + no:n_in + n_sin + no + n_sout], rest[-2:],
                       (i == 0) & (j == 0) & (k == 0), (i == ni - 1) & (j == nj - 1) & (k == nk - 1))
        av, bv = a_ref[...].astype(BF16), b_ref[...].astype(BF16)
        prod = _nn(av, bv) if mode == "nn" else _nt(av, bv) if mode == "nt" else _tn(av, bv)

        def finish(acc):
            res = epilogue(acc, *[e[...] for e in extra_refs])
            for o_ref, r in zip(out_refs, res):
                o_ref[...] = r.astype(o_ref.dtype)

        if nk == 1:
            finish(prod)
        else:
            acc_ref = rest[0]

            @pl.when(k == 0)
            def _():
                acc_ref[...] = prod

            @pl.when((k > 0) & (k < nk - 1))
            def _():
                acc_ref[...] += prod

            @pl.when(k == nk - 1)
            def _():
                finish(acc_ref[...] + prod)

    if mode == "nn":
        a_spec = pl.BlockSpec((tm, tk), lambda i, j, k: (i, k))
        b_spec = pl.BlockSpec((tk, tn), lambda i, j, k: (k, j))
    elif mode == "nt":
        a_spec = pl.BlockSpec((tm, tk), lambda i, j, k: (i, k))
        b_spec = pl.BlockSpec((tn, tk), lambda i, j, k: (j, k))
    else:
        a_spec = pl.BlockSpec((tk, tm), lambda i, j, k: (k, i))
        b_spec = pl.BlockSpec((tk, tn), lambda i, j, k: (k, j))
    o_spec = pl.BlockSpec((tm, tn), lambda i, j, k: (i, j))
    sem = ("arbitrary",) * 3 if side is not None else ("parallel", "parallel", "arbitrary")
    outs = pl.pallas_call(
        body, name=name,
        grid=(ni, nj, nk),
        in_specs=[a_spec, b_spec] + [o_spec] * ne + s_in,
        out_specs=[o_spec] * no + s_out,
        out_shape=[jax.ShapeDtypeStruct((M, N), dt) for dt in out_dtypes] + s_shape,
        input_output_aliases=s_alias(n_in, no),
        scratch_shapes=([pltpu.VMEM((tm, tn), F32)] if nk > 1 else []) + s_scr,
        compiler_params=_cparams(sem),
    )(a, b, *extras, *s_ops)
    return outs[0] if len(outs) == 1 else outs


def _row_spec(tr, d):
    return pl.BlockSpec((tr, d), lambda i: (i, 0))


def _vec_spec(d):
    return pl.BlockSpec((1, d), lambda i: (0, 0))


def _rms_fwd(name, x, w):
    T, D = x.shape
    tr = min(ROW_TILE, T)

    def body(x_ref, w_ref, o_ref):
        xf = x_ref[...]
        r = lax.rsqrt(jnp.mean(xf * xf, axis=-1, keepdims=True) + EPS)
        o_ref[...] = (xf * r * w_ref[...]).astype(BF16)

    return pl.pallas_call(
        body, name=name, grid=(T // tr,),
        in_specs=[_row_spec(tr, D), _vec_spec(D)], out_specs=_row_spec(tr, D),
        out_shape=jax.ShapeDtypeStruct((T, D), BF16),
        compiler_params=_cparams(("parallel",)),
    )(x, w)


def _rms_bwd(name, dy, h, w, dres, side=None):
    T, D = h.shape
    tr = min(ROW_TILE, T)

    def body(dy_ref, h_ref, w_ref, dres_ref, dh_ref, dhb_ref, dw_ref):
        @pl.when(pl.program_id(0) == 0)
        def _():
            dw_ref[...] = jnp.zeros_like(dw_ref)

        hf, dyv = h_ref[...], dy_ref[...]
        r = lax.rsqrt(jnp.mean(hf * hf, axis=-1, keepdims=True) + EPS)
        xhat = hf * r
        dw_ref[...] += jnp.sum(dyv * xhat, axis=0, keepdims=True)
        dxh = dyv * w_ref[...]
        dh = dres_ref[...] + r * (dxh - xhat * jnp.mean(dxh * xhat, axis=-1, keepdims=True))
        dh_ref[...] = dh
        dhb_ref[...] = dh.astype(BF16)

    return _call_with_side(
        body, name, (T // tr,),
        [_row_spec(tr, D), _row_spec(tr, D), _vec_spec(D), _row_spec(tr, D)],
        [_row_spec(tr, D), _row_spec(tr, D), _vec_spec(D)],
        [jax.ShapeDtypeStruct((T, D), F32), jax.ShapeDtypeStruct((T, D), BF16), jax.ShapeDtypeStruct((1, D), F32)],
        [], ("arbitrary",), (dy, h, w, dres), side)


def _loss_head(h2, target, w):
    T, D = h2.shape
    tr = min(ROW_TILE, T)

    def body(h_ref, t_ref, w_ref, loss_ref, dh_ref, dhb_ref, dw_ref):
        @pl.when(pl.program_id(0) == 0)
        def _():
            dw_ref[...] = jnp.zeros_like(dw_ref)
            loss_ref[...] = jnp.zeros_like(loss_ref)

        hf, wv = h_ref[...], w_ref[...]
        r = lax.rsqrt(jnp.mean(hf * hf, axis=-1, keepdims=True) + EPS)
        xhat = hf * r
        diff = xhat * wv - t_ref[...]
        loss_ref[...] += 0.5 * jnp.sum(jnp.mean(diff * diff, axis=-1, keepdims=True))
        dyv = diff * (1.0 / D)
        dw_ref[...] += jnp.sum(dyv * xhat, axis=0, keepdims=True)
        dxh = dyv * wv
        dh = r * (dxh - xhat * jnp.mean(dxh * xhat, axis=-1, keepdims=True))
        dh_ref[...] = dh
        dhb_ref[...] = dh.astype(BF16)

    return pl.pallas_call(
        body, name="loss_head", grid=(T // tr,),
        in_specs=[_row_spec(tr, D), _row_spec(tr, D), _vec_spec(D)],
        out_specs=[_vec_spec(LANE), _row_spec(tr, D), _row_spec(tr, D), _vec_spec(D)],
        out_shape=[jax.ShapeDtypeStruct((1, LANE), F32), jax.ShapeDtypeStruct((T, D), F32),
                   jax.ShapeDtypeStruct((T, D), BF16), jax.ShapeDtypeStruct((1, D), F32)],
        compiler_params=_cparams(("arbitrary",)),
    )(h2, target, w)


def _gate_tiles(T, D):
    goff = 4 * _hgw() + 3 * _atw()
    tc = min(1024, D)
    assert goff % tc == 0 and D % tc == 0
    return min(ROW_TILE, T), tc, goff // tc, D // tc


def _merge(z, pa, pb):
    T, D = pa.shape
    tr, tc, g0, nd = _gate_tiles(T, D)

    def body(ga_ref, gb_ref, pa_ref, pb_ref, o_ref):
        o_ref[...] = (_sigmoid(ga_ref[...]) * pa_ref[...] + _sigmoid(gb_ref[...]) * pb_ref[...]).astype(BF16)

    t = pl.BlockSpec((tr, tc), lambda i, j: (i, j))
    return pl.pallas_call(
        body, name="merge", grid=(T // tr, nd),
        in_specs=[pl.BlockSpec((tr, tc), lambda i, j: (i, g0 + j)),
                  pl.BlockSpec((tr, tc), lambda i, j: (i, g0 + nd + j)), t, t],
        out_specs=t, out_shape=jax.ShapeDtypeStruct((T, D), BF16),
        compiler_params=_cparams(("parallel", "parallel")),
    )(z, z, pa, pb)


def _dmerge(dm, z, pa, pb):
    T, D = pa.shape
    tr, tc, g0, nd = _gate_tiles(T, D)

    def body(dm_ref, ga_ref, gb_ref, pa_ref, pb_ref, dpa_ref, dpb_ref, dga_ref, dgb_ref):
        dmv = dm_ref[...]
        sa, sb = _sigmoid(ga_ref[...]), _sigmoid(gb_ref[...])
        dpa_ref[...] = (dmv * sa).astype(BF16)
        dpb_ref[...] = (dmv * sb).astype(BF16)
        dga_ref[...] = (dmv * pa_ref[...] * sa * (1.0 - sa)).astype(BF16)
        dgb_ref[...] = (dmv * pb_ref[...] * sb * (1.0 - sb)).astype(BF16)

    t = pl.BlockSpec((tr, tc), lambda i, j: (i, j))
    return pl.pallas_call(
        body, name="dmerge", grid=(T // tr, nd),
        in_specs=[t, pl.BlockSpec((tr, tc), lambda i, j: (i, g0 + j)),
                  pl.BlockSpec((tr, tc), lambda i, j: (i, g0 + nd + j)), t, t],
        out_specs=[t, t, t, t],
        out_shape=[jax.ShapeDtypeStruct((T, D), BF16)] * 4,
        compiler_params=_cparams(("parallel", "parallel")),
    )(dm, z, z, pa, pb)


def _hg_gates(xq, xf, lb):
    f = _sigmoid(xf)
    g = lb + (1.0 - lb) * f
    sq = _sigmoid(xq)
    return f, g, jnp.log(g), 1.0 - g, sq, xq * sq * (HG_D ** -0.5)


def _hg_decays(lg, tri_incl, rowi):
    b = _nn(tri_incl, lg, precision=HIGHEST)
    b_last = jnp.sum(lg, axis=0, keepdims=True)
    b_mid = jnp.sum(jnp.where(rowi <= CHUNK // 2, lg, 0.0), axis=0, keepdims=True)
    return b, b_last, b_mid


def _hg_in_specs(T):
    H = HG_HEADS
    return [pl.BlockSpec((T, HG_D), lambda h, s=s: (0, s * H + h)) for s in range(4)]


def _hg_fwd(z, lb_logits, hgw, side=None):
    T = z.shape[0]
    H, d, C = HG_HEADS, HG_D, CHUNK
    nc = T // C

    def body(hq_ref, hf_ref, hi_ref, hg_ref, lbl_ref, w_ref, ya_ref, o_ref, s_ref):
        lb = 1.0 / (1.0 + jnp.exp(lbl_ref[1:2, :] - lbl_ref[0:1, :]))
        wv = w_ref[...]
        row = lax.broadcasted_iota(jnp.int32, (C, C), 0)
        col = lax.broadcasted_iota(jnp.int32, (C, C), 1)
        tril = col <= row
        tri_incl = tril.astype(F32)
        rowi = lax.broadcasted_iota(jnp.int32, (C, d), 0)

        def chunk(c, st):
            rows = pl.ds(pl.multiple_of(c * C, C), C)
            xq, xf, v, xg = hq_ref[rows, :], hf_ref[rows, :], hi_ref[rows, :], hg_ref[rows, :]
            _, _, lg, kk, _, q = _hg_gates(xq, xf, lb)
            b, b_last, b_mid = _hg_decays(lg, tri_incl, rowi)
            st_b = st.astype(BF16)
            s_ref[c] = st
            vb = v.astype(BF16)
            o = _nt((q * jnp.exp(b)).astype(BF16), st_b)
            qt = (q * jnp.exp(b - b_mid)).astype(BF16)
            kt = (kk * jnp.exp(jnp.minimum(b_mid - b, EXP_CLAMP))).astype(BF16)
            a = jnp.where(tril, _nt(qt, kt), 0.0).astype(BF16)
            o = o + _nn(a, vb)
            st_new = st * jnp.exp(b_last) + _tn(v, kk * jnp.exp(b_last - b), HIGHEST)
            o_ref[rows, :] = o
            r = lax.rsqrt(jnp.mean(o * o, axis=-1, keepdims=True) + EPS)
            ya_ref[rows, :] = (o * r * wv * (xg * _sigmoid(xg))).astype(BF16)
            return st_new

        lax.fori_loop(0, nc // 2, lambda i, st: chunk(2 * i + 1, chunk(2 * i, st)), jnp.zeros((d, d), F32))

    head = pl.BlockSpec((T, d), lambda h: (0, h))
    return _call_with_side(
        body, "hg_fwd", (H,),
        _hg_in_specs(T) + [pl.BlockSpec((2, d), lambda h: (0, h)), pl.BlockSpec((1, d), lambda h: (0, 0))],
        [head, head, pl.BlockSpec((None, nc, d, d), lambda h: (h, 0, 0, 0))],
        [jax.ShapeDtypeStruct((T, H * d), BF16), jax.ShapeDtypeStruct((T, H * d), F32),
         jax.ShapeDtypeStruct((H, nc, d, d), F32)],
        [], ("parallel",), (z, z, z, z, lb_logits, hgw), side)


def _hg_bwd(z, o, dya, states, lb_logits, hgw, side=None):
    T = z.shape[0]
    H, d, C = HG_HEADS, HG_D, CHUNK
    nc = T // C
    scale = HG_D ** -0.5

    def body(hq_ref, hf_ref, hi_ref, hg_ref, o_ref, dy_ref, s_ref, lbl_ref, w_ref,
             dq_ref, df_ref, di_ref, dg_ref, dlbl_ref, dw_ref, acc_ref):
        lb = 1.0 / (1.0 + jnp.exp(lbl_ref[1:2, :] - lbl_ref[0:1, :]))
        wv = w_ref[...]
        row = lax.broadcasted_iota(jnp.int32, (C, C), 0)
        col = lax.broadcasted_iota(jnp.int32, (C, C), 1)
        tril = col <= row
        tri_incl = tril.astype(F32)
        triu_incl = (col >= row).astype(F32)
        rowi = lax.broadcasted_iota(jnp.int32, (C, d), 0)
        acc_ref[...] = jnp.zeros_like(acc_ref)

        @pl.when(pl.program_id(0) == 0)
        def _():
            dw_ref[...] = jnp.zeros_like(dw_ref)

        def chunk(i, carry):
            dst, tail = carry
            c = nc - 1 - i
            rows = pl.ds(pl.multiple_of(c * C, C), C)
            xq, xf, v, xg = hq_ref[rows, :], hf_ref[rows, :], hi_ref[rows, :], hg_ref[rows, :]
            f, g, lg, kk, sq, q = _hg_gates(xq, xf, lb)
            b, b_last, b_mid = _hg_decays(lg, tri_incl, rowi)
            e_b, e_qm, e_km = jnp.exp(b), jnp.exp(b - b_mid), jnp.exp(jnp.minimum(b_mid - b, EXP_CLAMP))
            e_kl, e_last = jnp.exp(b_last - b), jnp.exp(b_last)
            ov, dy = o_ref[rows, :], dy_ref[rows, :]
            r = lax.rsqrt(jnp.mean(ov * ov, axis=-1, keepdims=True) + EPS)
            xhat = ov * r
            sg = _sigmoid(xg)
            dxg = dy * xhat * wv * (sg * (1.0 + xg * (1.0 - sg)))
            dyn = dy * (xg * sg)
            acc_ref[0:1, :] += jnp.sum(dyn * xhat, axis=0, keepdims=True)
            dxh = dyn * wv
            dof = r * (dxh - xhat * jnp.mean(dxh * xhat, axis=-1, keepdims=True))
            do, vb = dof.astype(BF16), v.astype(BF16)
            qt, kt = q * e_qm, kk * e_km
            pm = jnp.where(tril, _nt(do, vb), 0.0)
            am = jnp.where(tril, _nt(qt.astype(BF16), kt.astype(BF16)), 0.0).astype(BF16)
            dq = _nn(dof, s_ref[c], HIGHEST) * e_b + _nn(pm, kt, HIGHEST) * e_qm
            dk = _tn(pm, qt, HIGHEST) * e_km + _nn(v, dst, HIGHEST) * e_kl
            dv = _tn(am, do) + _nt((kk * e_kl).astype(BF16), dst.astype(BF16))
            dst_prev = dst * e_last + _tn(dof, q * e_b, HIGHEST)
            db = q * dq - kk * dk
            dlg = _nn(triu_incl, db, precision=HIGHEST) + tail
            dgate = dlg / g - dk
            acc_ref[1:2, :] += jnp.sum(dgate * (1.0 - f), axis=0, keepdims=True)
            dq_ref[rows, :] = (dq * scale * (sq * (1.0 + xq * (1.0 - sq)))).astype(BF16)
            df_ref[rows, :] = (dgate * (1.0 - lb) * f * (1.0 - f)).astype(BF16)
            di_ref[rows, :] = dv.astype(BF16)
            dg_ref[rows, :] = dxg.astype(BF16)
            return dst_prev, tail + jnp.sum(db, axis=0, keepdims=True)

        lax.fori_loop(0, nc // 2, lambda i, carry: chunk(2 * i + 1, chunk(2 * i, carry)),
                      (jnp.zeros((d, d), F32), jnp.zeros((1, d), F32)))
        dw_ref[...] += acc_ref[0:1, :]
        dl0 = acc_ref[1:2, :] * lb * (1.0 - lb)
        dlbl_ref[0:1, :] = dl0
        dlbl_ref[1:2, :] = -dl0

    head = pl.BlockSpec((T, d), lambda h: (0, h))
    return _call_with_side(
        body, "hg_bwd", (H,),
        _hg_in_specs(T) + [head, head, pl.BlockSpec((None, nc, d, d), lambda h: (h, 0, 0, 0)),
                           pl.BlockSpec((2, d), lambda h: (0, h)), pl.BlockSpec((1, d), lambda h: (0, 0))],
        [head, head, head, head, pl.BlockSpec((2, d), lambda h: (0, h)), pl.BlockSpec((1, d), lambda h: (0, 0))],
        [jax.ShapeDtypeStruct((T, H * d), BF16)] * 4 + [jax.ShapeDtypeStruct((2, H * d), F32),
                                                        jax.ShapeDtypeStruct((1, d), F32)],
        [pltpu.VMEM((8, d), F32)], ("arbitrary",), (z, z, z, z, o, dya, states, lb_logits, hgw), side)


def _at_dims():
    pad = LEFT * CHUNK
    return pad, QB + pad, AT_HEADS * AT_DH // LANE, 4 * _hgw() // LANE


def _rel_of_period():
    pad, W, _, _ = _at_dims()
    n = jnp.arange(QB + W)
    return jnp.clip(pad - jnp.where(n < W, n, n - (QB + W)), -REL_CLIP, REL_CLIP) + REL_CLIP


def _bias_window(rel_bias):
    pad, W, _, _ = _at_dims()
    H, P = rel_bias.shape[0], QB + W
    per = rel_bias[:, _rel_of_period()]
    win = jnp.tile(per, (1, QB))[:, :QB * (P - 1)].reshape(H, QB, P - 1)[:, :, :W]
    t = jnp.arange(QB)[:, None]
    j = jnp.arange(W)[None, :]
    ok = (j // CHUNK >= t // CHUNK) & (j // CHUNK <= t // CHUNK + LEFT)
    return jnp.where(ok[None], win, NEG)


def _bias_window_grad(dbw):
    pad, W, _, _ = _at_dims()
    H, P = dbw.shape[0], QB + W
    flat = jnp.pad(dbw, ((0, 0), (0, 0), (0, P - 1 - W))).reshape(H, QB * (P - 1))
    per = jnp.pad(flat, ((0, 0), (0, QB))).reshape(H, QB, P).sum(axis=1)
    onehot = _rel_of_period()[:, None] == jnp.arange(2 * REL_CLIP + 1)[None, :]
    return jnp.dot(per, onehot.astype(F32), precision=HIGHEST)


def _at_softmax(q_half, kw, bias, valid):
    s = _nt(q_half, kw) * (AT_DH ** -0.5) + bias
    s = jnp.where(valid, s, NEG)
    e = jnp.exp(s - jnp.max(s, axis=-1, keepdims=True))
    return e / jnp.sum(e, axis=-1, keepdims=True)


def _at_fwd(z, bias_win, side=None):
    T = z.shape[0]
    pad, W, HP, c0 = _at_dims()
    nq = T // QB

    def body(q_ref, k_ref, v_ref, bias_ref, o_ref, kpad, vpad):
        qi = pl.program_id(1)

        @pl.when(qi == 0)
        def _():
            kpad[0:pad, :] = jnp.zeros((pad, LANE), BF16)
            vpad[0:pad, :] = jnp.zeros((pad, LANE), BF16)
            kpad[pad:, :] = k_ref[...].astype(BF16)
            vpad[pad:, :] = v_ref[...].astype(BF16)

        win = pl.ds(pl.multiple_of(qi * QB, QB), W)
        kw, vw = kpad[win, :], vpad[win, :]
        q = q_ref[...]
        lane = lax.broadcasted_iota(jnp.int32, (QB, LANE), 1)
        first = lane < AT_DH
        valid = lax.broadcasted_iota(jnp.int32, (QB, W), 1) + qi * QB >= pad
        pa = _at_softmax(jnp.where(first, q, 0.0).astype(BF16), kw, bias_ref[0], valid)
        pb = _at_softmax(jnp.where(first, 0.0, q).astype(BF16), kw, bias_ref[1], valid)
        o_ref[...] = jnp.where(first, _nn(pa.astype(BF16), vw), _nn(pb.astype(BF16), vw)).astype(BF16)

    full = lambda s: pl.BlockSpec((T, LANE), lambda hp, qi, s=s: (0, c0 + s * HP + hp))
    return _call_with_side(
        body, "at_fwd", (HP, nq),
        [pl.BlockSpec((QB, LANE), lambda hp, qi: (qi, c0 + hp)), full(1), full(2),
         pl.BlockSpec((2, QB, W), lambda hp, qi: (hp, 0, 0))],
        [pl.BlockSpec((QB, LANE), lambda hp, qi: (qi, hp))],
        [jax.ShapeDtypeStruct((T, HP * LANE), BF16)],
        [pltpu.VMEM((T + pad, LANE), BF16)] * 2, ("parallel", "arbitrary"), (z, z, z, bias_win), side)


def _at_bwd(z, dyb, bias_win, side=None):
    T = z.shape[0]
    pad, W, HP, c0 = _at_dims()
    nq = T // QB
    scale = AT_DH ** -0.5

    def body(q_ref, k_ref, v_ref, do_ref, bias_ref, dq_ref, dk_ref, dv_ref, dbias_ref, kpad, vpad, dkpad, dvpad):
        qi = pl.program_id(1)

        @pl.when(qi == 0)
        def _():
            kpad[0:pad, :] = jnp.zeros((pad, LANE), BF16)
            vpad[0:pad, :] = jnp.zeros((pad, LANE), BF16)
            kpad[pad:, :] = k_ref[...].astype(BF16)
            vpad[pad:, :] = v_ref[...].astype(BF16)
            dkpad[...] = jnp.zeros_like(dkpad)
            dvpad[...] = jnp.zeros_like(dvpad)
            dbias_ref[...] = jnp.zeros_like(dbias_ref)

        win = pl.ds(pl.multiple_of(qi * QB, QB), W)
        kw, vw = kpad[win, :], vpad[win, :]
        q, do = q_ref[...], do_ref[...]
        lane = lax.broadcasted_iota(jnp.int32, (QB, LANE), 1)
        first = lane < AT_DH
        valid = lax.broadcasted_iota(jnp.int32, (QB, W), 1) + qi * QB >= pad

        def half(hh, qh, doh):
            p = _at_softmax(qh, kw, bias_ref[hh], valid)
            dp = _nt(doh, vw)
            ds = p * (dp - jnp.sum(p * dp, axis=-1, keepdims=True))
            dbias_ref[hh] += ds
            dss = (ds * scale).astype(BF16)
            return _nn(dss, kw), _tn(dss, qh), _tn(p.astype(BF16), doh)

        dqa, dka, dva = half(0, jnp.where(first, q, 0.0).astype(BF16), jnp.where(first, do, 0.0).astype(BF16))
        dqb, dkb, dvb = half(1, jnp.where(first, 0.0, q).astype(BF16), jnp.where(first, 0.0, do).astype(BF16))
        dq_ref[...] = jnp.where(first, dqa, dqb).astype(BF16)
        dkpad[win, :] += dka + dkb
        dvpad[win, :] += dva + dvb

        @pl.when(qi == nq - 1)
        def _():
            dk_ref[...] = dkpad[pad:, :].astype(BF16)
            dv_ref[...] = dvpad[pad:, :].astype(BF16)

    full = lambda s: pl.BlockSpec((T, LANE), lambda hp, qi, s=s: (0, c0 + s * HP + hp))
    blk = pl.BlockSpec((QB, LANE), lambda hp, qi: (qi, hp))
    col = pl.BlockSpec((T, LANE), lambda hp, qi: (0, hp))
    bw = pl.BlockSpec((2, QB, W), lambda hp, qi: (hp, 0, 0))
    return _call_with_side(
        body, "at_bwd", (HP, nq),
        [pl.BlockSpec((QB, LANE), lambda hp, qi: (qi, c0 + hp)), full(1), full(2), blk, bw],
        [blk, col, col, bw],
        [jax.ShapeDtypeStruct((T, HP * LANE), BF16)] * 3 + [jax.ShapeDtypeStruct(bias_win.shape, F32)],
        [pltpu.VMEM((T + pad, LANE), BF16)] * 2 + [pltpu.VMEM((T + pad, LANE), F32)] * 2,
        ("parallel", "arbitrary"), (z, z, z, dyb, bias_win), side)


def _piece_tiles(name, full_shape):
    pr, pc = _piece_shape(name, full_shape)
    tr = min(ROW_TILE, pr)
    assert pr % tr == 0
    nt = pr // tr
    if name in ROW_SHARDED:
        return tr, nt, lambda q, half, i: ((2 * q + half) * nt + i, 0)
    return tr, nt, lambda q, half, i: (half * nt + i, q)


def _cast_into_full(name, wq, place):
    full = _full_shape(name, wq.shape)
    pc = wq.shape[1]
    tr, nt, at = _piece_tiles(name, full)

    def body(place_ref, w_ref, o_ref):
        o_ref[...] = w_ref[...].astype(BF16)

    return pl.pallas_call(
        body, name="cast_" + name,
        grid_spec=pltpu.PrefetchScalarGridSpec(
            num_scalar_prefetch=1, grid=(2, nt),
            in_specs=[pl.BlockSpec((tr, pc), lambda h, i, s: (h * nt + i, 0))],
            out_specs=pl.BlockSpec((tr, pc), lambda h, i, s: at(s[0], h, i))),
        out_shape=jax.ShapeDtypeStruct(full, BF16),
        compiler_params=_cparams(("parallel", "parallel")),
    )(place, wq)


def _chip_sum(name, grad, theirs, place):
    pr, pc = theirs.shape[1:]
    tr, nt, at = _piece_tiles(name, grad.shape)

    def body(place_ref, g_ref, t_ref, o_ref):
        o_ref[...] = (g_ref[...].astype(F32) + t_ref[...].astype(F32)).astype(BF16)

    piece = pl.BlockSpec((None, tr, pc), lambda q, i, s: (q, i, 0))
    return pl.pallas_call(
        body, name="chip_sum_" + name,
        grid_spec=pltpu.PrefetchScalarGridSpec(
            num_scalar_prefetch=1, grid=(4, nt),
            in_specs=[pl.BlockSpec((tr, pc), lambda q, i, s: at(q, s[1], i)), piece], out_specs=piece),
        out_shape=jax.ShapeDtypeStruct(theirs.shape, BF16),
        compiler_params=_cparams(("parallel", "parallel")),
    )(place, grad, theirs)


def _piece_sum(name, chip_sums, got, place):
    pr, pc = chip_sums.shape[1:]
    tr = min(ROW_TILE, pr)

    def body(place_ref, own_ref, got_ref, o_ref):
        o_ref[...] = (own_ref[...].astype(F32) + got_ref[0].astype(F32) + got_ref[1].astype(F32)
                      + got_ref[2].astype(F32))

    return pl.pallas_call(
        body, name="piece_sum_" + name,
        grid_spec=pltpu.PrefetchScalarGridSpec(
            num_scalar_prefetch=1, grid=(pr // tr,),
            in_specs=[pl.BlockSpec((None, tr, pc), lambda i, s: (s[0], i, 0)),
                      pl.BlockSpec((3, tr, pc), lambda i, s: (0, i, 0))],
            out_specs=pl.BlockSpec((tr, pc), lambda i, s: (i, 0))),
        out_shape=jax.ShapeDtypeStruct((pr, pc), F32),
        compiler_params=_cparams(("parallel",)),
    )(place, chip_sums, got)


def _adam_quarter(name, w, m, v, g_mine, g_sib, place, side=None):
    pr, pc = g_mine.shape
    tr = min(ROW_TILE // 2, pr)
    nt = pr // tr

    def body(place_ref, w_ref, m_ref, v_ref, gm_ref, gs_ref, go_ref, d_ref, mo_ref, vo_ref):
        g = jnp.where(pl.program_id(0) == place_ref[1], gm_ref[...], gs_ref[...])
        delta, mn, vn = _adam_math(w_ref[...], g, m_ref[...], v_ref[...])
        go_ref[...] = g
        d_ref[...] = delta
        mo_ref[...] = mn
        vo_ref[...] = vn

    quarter = pl.BlockSpec((tr, pc), lambda h, i, s: (h * nt + i, 0))
    mine = pl.BlockSpec((tr, pc), lambda h, i, s: (jnp.where(h == s[1], i, 0), 0))
    sib = pl.BlockSpec((tr, pc), lambda h, i, s: (jnp.where(h == s[1], 0, i), 0))
    return _call_with_side(
        body, "adam_" + name, (2, nt), [quarter, quarter, quarter, mine, sib], [quarter] * 4,
        [jax.ShapeDtypeStruct(w.shape, F32)] * 4, [], ("parallel", "parallel"),
        (place, w, m, v, g_mine, g_sib), side, n_prefetch=1)


def _adam_math(w, g, m, v):
    m = ADAM_B1 * m + (1.0 - ADAM_B1) * g
    v = ADAM_B2 * v + (1.0 - ADAM_B2) * (g * g)
    m_hat = m / (1.0 - ADAM_B1 ** ADAM_STEP)
    v_hat = v / (1.0 - ADAM_B2 ** ADAM_STEP)
    return -ADAM_LR * (m_hat / (jnp.sqrt(v_hat) + ADAM_EPS) + ADAM_WD * w), m, v


WEIGHTS = ("w_in", "w_branch_a", "w_branch_b", "w_out", "w_up", "w_down")
ROW_SHARDED = ("w_out", "w_down")
ANY = pl.BlockSpec(memory_space=pl.ANY)
MESH = pl.DeviceIdType.MESH


def _place():
    x, y, c = lax.axis_index("x"), lax.axis_index("y"), lax.axis_index("c")
    chips = [(1 - x, y), (x, 1 - y), (1 - x, 1 - y)]
    return x, y, c, 2 * x + y, chips, [2 * cx + cy for cx, cy in chips]


def _piece(full_ref, name, q, half):
    K, N = full_ref.shape
    if name in ROW_SHARDED:
        rows = K // 8
        return full_ref.at[pl.ds(q * (2 * rows) + half * rows, rows), :]
    return full_ref.at[pl.ds(half * (K // 2), K // 2), pl.ds(q * (N // 4), N // 4)]


def _piece_shape(name, full_shape):
    K, N = full_shape
    return (K // 8, N) if name in ROW_SHARDED else (K // 2, N // 4)


def _full_shape(name, quarter_shape):
    Kq, Nq = quarter_shape
    return (4 * Kq, Nq) if name in ROW_SHARDED else (Kq, 4 * Nq)


def _remote(src, dst, send_sem, recv_sem, device):
    return pltpu.make_async_remote_copy(src_ref=src, dst_ref=dst, send_sem=send_sem, recv_sem=recv_sem,
                                        device_id=device, device_id_type=MESH)


def _gather_weights(names, fulls):
    n = len(names)

    def body(*refs):
        f_refs = refs[n:2 * n]
        send_sems, recv_sems = refs[2 * n:]
        x, y, c, p, chips, chip_ids = _place()
        sib = (x, y, 1 - c)
        sends = []
        for i, name in enumerate(names):
            mine = _piece(f_refs[i], name, p, c)
            for j, chip in enumerate(chips):
                cp = _remote(mine, mine, send_sems.at[i, j], recv_sems.at[i, j], (*chip, c))
                cp.start()
                sends.append(cp)
        for i, name in enumerate(names):
            for j, cid in enumerate(chip_ids):
                landed = _piece(f_refs[i], name, cid, c)
                _remote(landed, landed, send_sems.at[i, j], recv_sems.at[i, j], sib).wait_recv()
                cp = _remote(landed, landed, send_sems.at[i, 3 + j], recv_sems.at[i, 3 + j], sib)
                cp.start()
                sends.append(cp)
        for i, name in enumerate(names):
            for j, cid in enumerate(chip_ids):
                other = _piece(f_refs[i], name, cid, 1 - c)
                _remote(other, other, send_sems.at[i, 3 + j], recv_sems.at[i, 3 + j], sib).wait_recv()
        for cp in sends:
            cp.wait_send()

    return pl.pallas_call(
        body, name="gather_" + "_".join(names),
        in_specs=[ANY] * n, out_specs=[ANY] * n,
        out_shape=[jax.ShapeDtypeStruct(f.shape, BF16) for f in fulls],
        input_output_aliases={i: i for i in range(n)},
        scratch_shapes=[pltpu.SemaphoreType.DMA((n, 6)), pltpu.SemaphoreType.DMA((n, 6))],
    )(*fulls)


def _rows(ref, span):
    return ref if span is None else ref.at[pl.ds(span[0], span[1]), :]


def _ici_gather(names, fulls, rows=None):
    rows = rows or [None] * len(names)

    def build(reads, aliased, fresh, send_sems, recv_sems, off=0):
        _, _, c, p, chips, _ = _place()
        out = []
        for i, (ref, name) in enumerate(zip(aliased, names)):
            mine = _rows(_piece(ref, name, p, c), rows[i])
            for j, chip in enumerate(chips):
                k = off + 3 * i + j
                out.append(_remote(mine, mine, send_sems.at[k], recv_sems.at[k], (*chip, c)))
        return out

    return _Side(build, 3 * len(names), aliased=fulls)


def _d2d_gather(names, fulls):
    def build(reads, aliased, fresh, send_sems, recv_sems, off=0):
        x, y, c, _, _, chip_ids = _place()
        out = []
        for i, (ref, name) in enumerate(zip(aliased, names)):
            for j, cid in enumerate(chip_ids):
                landed, k = _piece(ref, name, cid, c), off + 3 * i + j
                out.append(_remote(landed, landed, send_sems.at[k], recv_sems.at[k], (x, y, 1 - c)))
        return out

    return _Side(build, 3 * len(names), aliased=fulls)


def _sib_send(names, grads):
    def build(reads, aliased, fresh, send_sems, recv_sems, off=0):
        x, y, c, _, _, _ = _place()
        out = []
        for i, name in enumerate(names):
            for q in range(4):
                k = off + 4 * i + q
                out.append(_remote(_piece(reads[i], name, q, 1 - c), fresh[i].at[q], send_sems.at[k], recv_sems.at[k],
                                   (x, y, 1 - c)))
        return out

    shapes = [jax.ShapeDtypeStruct((4,) + _piece_shape(name, g.shape), BF16) for name, g in zip(names, grads)]
    return _Side(build, 4 * len(names), reads=grads, fresh=shapes)


def _chip_exchange(chip_sums, rows=None, got=None):
    rows = rows or [None] * len(chip_sums)

    def build(reads, aliased, fresh, send_sems, recv_sems, off=0):
        _, _, c, _, chips, chip_ids = _place()
        out = []
        for i in range(len(chip_sums)):
            for j, (chip, cid) in enumerate(zip(chips, chip_ids)):
                k = off + 3 * i + j
                out.append(_remote(_rows(reads[i].at[cid], rows[i]), _rows((aliased or fresh)[i].at[j], rows[i]),
                                   send_sems.at[k], recv_sems.at[k], (*chip, c)))
        return out

    if got is not None:
        return _Side(build, 3 * len(chip_sums), reads=chip_sums, aliased=got)
    shapes = [jax.ShapeDtypeStruct((3,) + s.shape[1:], BF16) for s in chip_sums]
    return _Side(build, 3 * len(chip_sums), reads=chip_sums, fresh=shapes)


def _sib_share(halves):
    def build(reads, aliased, fresh, send_sems, recv_sems, off=0):
        x, y, c, _, _, _ = _place()
        return [_remote(reads[i], fresh[i], send_sems.at[off + i], recv_sems.at[off + i], (x, y, 1 - c))
                for i in range(len(halves))]

    return _Side(build, len(halves), reads=halves, fresh=[jax.ShapeDtypeStruct(h.shape, F32) for h in halves])


def _join(a, b):
    def build(reads, aliased, fresh, send_sems, recv_sems, off=0):
        ra, aa, fa = len(a.reads), len(a.aliased), len(a.fresh)
        return (a.build(reads[:ra], aliased[:aa], fresh[:fa], send_sems, recv_sems, off)
                + b.build(reads[ra:], aliased[aa:], fresh[fa:], send_sems, recv_sems, off + a.nsem))

    return _Side(build, a.nsem + b.nsem, a.reads + b.reads, a.aliased + b.aliased, a.fresh + b.fresh)


def _run_side(name, side):
    nr, na = len(side.reads), len(side.aliased)

    def body(*refs):
        n_in, n_out = nr + na, na + len(side.fresh)
        outs = refs[n_in:n_in + n_out]
        copies = side.build(refs[:nr], outs[:na], outs[na:], *refs[-2:])
        for cp in copies:
            cp.start()
        for cp in copies:
            cp.wait()

    return pl.pallas_call(
        body, name=name, in_specs=side.in_specs(), out_specs=side.out_specs(), out_shape=side.out_shape(),
        input_output_aliases=side.aliases(0, 0), scratch_shapes=side.scratch(),
    )(*side.operands())


def _small_allreduce_adam(gpart, w, m, v):
    R = gpart.shape[0]

    def body(g_ref, w_ref, m_ref, v_ref, go_ref, d_ref, mo_ref, vo_ref, buf, send_sems, recv_sems):
        x, y, c = lax.axis_index("x"), lax.axis_index("y"), lax.axis_index("c")
        me = 4 * x + 2 * y + c
        buf[me] = g_ref[...]
        copies = []
        for k in range(1, 8):
            fx, fy, fc = (k >> 2) & 1, (k >> 1) & 1, k & 1
            peer = (1 - x if fx else x, 1 - y if fy else y, 1 - c if fc else c)
            cp = _remote(g_ref, buf.at[me], send_sems.at[k - 1], recv_sems.at[k - 1], peer)
            cp.start()
            copies.append((cp, 4 * peer[0] + 2 * peer[1] + peer[2]))
        for k, (cp, pid) in enumerate(copies):
            _remote(g_ref, buf.at[pid], send_sems.at[k], recv_sems.at[k], (x, y, c)).wait_recv()
        for cp, _ in copies:
            cp.wait_send()
        g = buf[0]
        for d in range(1, 8):
            g = g + buf[d]
        delta, mn, vn = _adam_math(w_ref[...], g, m_ref[...], v_ref[...])
        go_ref[...] = g
        d_ref[...] = delta
        mo_ref[...] = mn
        vo_ref[...] = vn

    vm = pl.BlockSpec(memory_space=pltpu.VMEM)
    return pl.pallas_call(
        body, name="small_allreduce_adam",
        in_specs=[vm] * 4, out_specs=[vm] * 4,
        out_shape=[jax.ShapeDtypeStruct((R, LANE), F32)] * 4,
        scratch_shapes=[pltpu.VMEM((8, R, LANE), F32), pltpu.SemaphoreType.DMA((7,)), pltpu.SemaphoreType.DMA((7,))],
    )(gpart, w, m, v)


def _pack(arrs):
    flat = jnp.concatenate([a.reshape(-1).astype(F32) for a in arrs])
    rows = -(-flat.shape[0] // (8 * LANE)) * 8
    return jnp.pad(flat, (0, rows * LANE - flat.shape[0])).reshape(rows, LANE)


def _unpack(packed, like):
    flat, out, off = packed.reshape(-1), [], 0
    for a in like:
        out.append(flat[off:off + a.size].reshape(a.shape))
        off += a.size
    return out


def kernel(x, w_in, lb_logits, hg_norm_w, rel_bias, w_branch_a, w_branch_b, w_out, norm_mix_w, norm_mlp_w, w_up, w_down, norm_final_w, loss_target, m_w_in, m_lb_logits, m_hg_norm_w, m_rel_bias, m_w_branch_a, m_w_branch_b, m_w_out, m_norm_mix_w, m_norm_mlp_w, m_w_up, m_w_down, m_norm_final_w, v_w_in, v_lb_logits, v_hg_norm_w, v_rel_bias, v_w_branch_a, v_w_branch_b, v_w_out, v_norm_mix_w, v_norm_mlp_w, v_w_up, v_w_down, v_norm_final_w):
    T, D = x.shape[1], x.shape[2]
    x2, tgt = x.reshape(T, D), loss_target.reshape(T, D)
    big = dict(w_in=(w_in, m_w_in, v_w_in), w_branch_a=(w_branch_a, m_w_branch_a, v_w_branch_a),
               w_branch_b=(w_branch_b, m_w_branch_b, v_w_branch_b), w_out=(w_out, m_w_out, v_w_out),
               w_up=(w_up, m_w_up, v_w_up), w_down=(w_down, m_w_down, v_w_down))
    big = {k: tuple(a[0] for a in v) for k, v in big.items()}
    nfw = norm_final_w.reshape(1, D)

    place = jnp.stack([2 * lax.axis_index("x") + lax.axis_index("y"), lax.axis_index("c")]).astype(jnp.int32)
    Wf = {name: _cast_into_full(name, big[name][0], place) for name in WEIGHTS}
    small3 = ["w_branch_a", "w_branch_b", "w_out"]
    (Wf["w_in"],) = _gather_weights(["w_in"], [Wf["w_in"]])

    def span(name, lo, hi):
        pr = _piece_shape(name, Wf[name].shape)[0]
        return (pr * lo // 16, pr * (hi - lo) // 16)

    u1 = _rms_fwd("norm_mix", x2, norm_mix_w)
    z, *moved = _mm("z_proj", u1, Wf["w_in"], "nn", [F32],
                    side=_ici_gather(small3 + ["w_up"], [Wf[n] for n in small3 + ["w_up"]],
                                     rows=[None] * 3 + [span("w_up", 0, 2)]))
    ya, o_hg, states, Wf["w_up"], *moved = _hg_fwd(
        z, lb_logits, hg_norm_w, side=_join(_ici_gather(["w_up"], moved[3:], rows=[span("w_up", 2, 16)]),
                                            _d2d_gather(small3, moved[:3])))
    Wf.update(zip(small3, moved))
    bias_win = _bias_window(rel_bias[0])
    yb, Wf["w_down"], Wf["w_up"] = _at_fwd(
        z, bias_win, side=_join(_ici_gather(["w_down"], [Wf["w_down"]], rows=[span("w_down", 0, 10)]),
                                _d2d_gather(["w_up"], [Wf["w_up"]])))
    pa = _mm("branch_a", ya, Wf["w_branch_a"], "nn", [F32])
    pb = _mm("branch_b", yb, Wf["w_branch_b"], "nn", [F32])
    merged = _merge(z, pa, pb)
    add = lambda acc, res: (acc + res,)
    h1 = _mm("out_proj", merged, Wf["w_out"], "nn", [F32], extras=[x2], epilogue=add)
    u2 = _rms_fwd("norm_mlp", h1, norm_mlp_w)
    relu2 = lambda acc: (acc, jnp.square(jnp.maximum(acc, 0.0)))
    a_pre, act, Wf["w_down"] = _mm("mlp_up", u2, Wf["w_up"], "nn", [F32, BF16], epilogue=relu2,
                                   side=_ici_gather(["w_down"], [Wf["w_down"]], rows=[span("w_down", 10, 16)]))
    (Wf["w_down"],) = _run_side("pass_w_down", _d2d_gather(["w_down"], [Wf["w_down"]]))
    h2 = _mm("mlp_down", act, Wf["w_down"], "nn", [F32], extras=[h1], epilogue=add)
    loss_part, dh2, dh2b, d_nf = _loss_head(h2, tgt, nfw)

    drelu2 = lambda acc, a: (acc * (2.0 * jnp.maximum(a, 0.0)),)
    da = _mm("d_act", dh2b, Wf["w_down"], "nt", [BF16], extras=[a_pre], epilogue=drelu2)
    G = {}
    G["w_down"] = _mm("g_w_down", act, dh2b, "tn", [BF16])
    G["w_up"] = _mm("g_w_up", u2, da, "tn", [BF16])
    T_, S_, GOT = {}, {}, {}
    du2, T_["w_down"], T_["w_up"] = _mm("d_u2", da, Wf["w_up"], "nt", [F32],
                                        side=_sib_send(["w_down", "w_up"], [G["w_down"], G["w_up"]]))
    for n in ("w_down", "w_up"):
        S_[n] = _chip_sum(n, G[n], T_[n], place)
    dh1, dh1b, d_nmlp = _rms_bwd("norm_mlp_bwd", du2, h1, norm_mlp_w, dh2)
    dmerged = _mm("d_merged", dh1b, Wf["w_out"], "nt", [F32])
    G["w_out"] = _mm("g_w_out", merged, dh1b, "tn", [BF16])
    dpa, dpb, dz_ga, dz_gb = _dmerge(dmerged, z, pa, pb)
    dya = _mm("d_ya", dpa, Wf["w_branch_a"], "nt", [F32])
    dyb = _mm("d_yb", dpb, Wf["w_branch_b"], "nt", [F32])
    G["w_branch_a"] = _mm("g_w_a", ya, dpa, "tn", [BF16])
    G["w_branch_b"] = _mm("g_w_b", yb, dpb, "tn", [BF16])
    dz_q, dz_f, dz_i, dz_g, d_lbl, d_hgw, GOT["w_down"], *sent = _hg_bwd(
        z, o_hg, dya, states, lb_logits, hg_norm_w,
        side=_join(_chip_exchange([S_["w_down"]]), _sib_send(small3, [G[n] for n in small3])))
    for n, t in zip(small3, sent):
        S_[n] = _chip_sum(n, G[n], t, place)
    dz_aq, dz_ak, dz_av, dbias_win, GOT["w_up"] = _at_bwd(z, dyb, bias_win, side=_chip_exchange([S_["w_up"]]))
    dz = jnp.concatenate([dz_q, dz_f, dz_i, dz_g, dz_aq, dz_ak, dz_av, dz_ga, dz_gb], axis=1)
    G["w_in"], *got3 = _mm("g_w_in", u1, dz, "tn", [BF16], side=_chip_exchange([S_[n] for n in small3]))
    GOT.update(zip(small3, got3))
    (T_["w_in"],) = _run_side("send_w_in_to_sibling", _sib_send(["w_in"], [G["w_in"]]))
    S_["w_in"] = _chip_sum("w_in", G["w_in"], T_["w_in"], place)
    early = [n for n in WEIGHTS if n != "w_in"]
    H_ = {n: _piece_sum(n, S_[n], GOT[n], place) for n in early}

    def w_in_rows(lo, hi, got=None):
        return _chip_exchange([S_["w_in"]], rows=[span("w_in", lo, hi)], got=got)

    du1, got_in, *shared = _mm("d_u1", dz, Wf["w_in"], "nt", [F32],
                               side=_join(w_in_rows(0, 7), _sib_share([H_[n] for n in early])))
    O_ = dict(zip(early, shared))
    grad_x, _, d_nmix, got_in = _rms_bwd("norm_mix_bwd", du1, x2, norm_mix_w, dh1, side=w_in_rows(7, 9, [got_in]))
    d_rel = _bias_window_grad(dbias_win)
    big_out = {}
    for name, lo, hi in (("w_down", 9, 12), ("w_up", 12, 15), ("w_out", 15, 16)):
        *outs, got_in = _adam_quarter(name, *big[name], H_[name], O_[name], place, side=w_in_rows(lo, hi, [got_in]))
        big_out[name] = tuple(a[None] for a in outs)
    H_["w_in"] = _piece_sum("w_in", S_["w_in"], got_in, place)
    (O_["w_in"],) = _run_side("share_w_in", _sib_share([H_["w_in"]]))
    for name in ("w_in", "w_branch_a", "w_branch_b"):
        outs = _adam_quarter(name, *big[name], H_[name], O_[name], place)
        big_out[name] = tuple(a[None] for a in outs)

    smalls = [("lb_logits", lb_logits, m_lb_logits, v_lb_logits, d_lbl),
              ("hg_norm_w", hg_norm_w, m_hg_norm_w, v_hg_norm_w, d_hgw),
              ("rel_bias", rel_bias, m_rel_bias, v_rel_bias, d_rel),
              ("norm_mix_w", norm_mix_w, m_norm_mix_w, v_norm_mix_w, d_nmix),
              ("norm_mlp_w", norm_mlp_w, m_norm_mlp_w, v_norm_mlp_w, d_nmlp),
              ("norm_final_w", norm_final_w, m_norm_final_w, v_norm_final_w, d_nf)]
    like = [s[1] for s in smalls]
    packed = _small_allreduce_adam(_pack([s[4] for s in smalls]), _pack(like), _pack([s[2] for s in smalls]),
                                   _pack([s[3] for s in smalls]))
    small_out = {s[0]: vals for s, vals in zip(smalls, zip(*[_unpack(p, like) for p in packed]))}

    loss = lax.psum(loss_part[0, 0], ("x", "y", "c"))
    order = ["w_in", "lb_logits", "hg_norm_w", "rel_bias", "w_branch_a", "w_branch_b", "w_out", "norm_mix_w",
             "norm_mlp_w", "w_up", "w_down", "norm_final_w"]
    res = {**big_out, **small_out}
    return (loss, grad_x.reshape(x.shape), *[res[n][0] for n in order], *[res[n][1] for n in order],
            *[res[n][2] for n in order], *[res[n][3] for n in order])
```

```python
import functools

import jax
import jax.numpy as jnp
from jax import lax
from jax.experimental import pallas as pl
from jax.experimental.pallas import tpu as pltpu

F32 = jnp.float32
BF16 = jnp.bfloat16
HIGHEST = lax.Precision.HIGHEST

D_MODEL = 2048
SEQ = 2048
CHUNK = 64
HG_HEADS = 8
HG_D = 128
AT_HEADS = 16
AT_DH = 64
LEFT = 8
REL_CLIP = 256
D_FF = 8192
EPS = 1e-6
ADAM_LR = 0.001
ADAM_B1 = 0.9
ADAM_B2 = 0.999
ADAM_EPS = 1e-08
ADAM_WD = 0.01
ADAM_STEP = 10

LANE = 128
NEG = -1e30
EXP_CLAMP = 80.0
VMEM_LIMIT = 48 * 1024 * 1024
MM_TM, MM_TN, MM_TK = 1024, 1024, 2048
ROW_TILE = 256
QB = 2 * CHUNK


def _hgw():
    return HG_HEADS * HG_D


def _atw():
    return AT_HEADS * AT_DH


def _cparams(sem):
    return pltpu.CompilerParams(dimension_semantics=sem, vmem_limit_bytes=VMEM_LIMIT)


def _sigmoid(x):
    return jax.nn.sigmoid(x)


def _dot(a, b, dims, precision=None):
    return lax.dot_general(a, b, (dims, ((), ())), preferred_element_type=F32, precision=precision)


def _nn(a, b, precision=None):
    return _dot(a, b, ((1,), (0,)), precision)


def _nt(a, b, precision=None):
    return _dot(a, b, ((1,), (1,)), precision)


def _tn(a, b, precision=None):
    return _dot(a, b, ((0,), (0,)), precision)


class _Side:
    def __init__(self, build, nsem, reads=(), aliased=(), fresh=()):
        self.build, self.nsem = build, nsem
        self.reads, self.aliased, self.fresh = list(reads), list(aliased), list(fresh)

    def operands(self):
        return self.reads + self.aliased

    def in_specs(self):
        return [ANY] * len(self.operands())

    def out_specs(self):
        return [ANY] * (len(self.aliased) + len(self.fresh))

    def out_shape(self):
        return [jax.ShapeDtypeStruct(a.shape, a.dtype) for a in self.aliased] + self.fresh

    def aliases(self, n_in, n_out):
        return {n_in + len(self.reads) + t: n_out + t for t in range(len(self.aliased))}

    def scratch(self):
        return [pltpu.SemaphoreType.DMA((self.nsem,)), pltpu.SemaphoreType.DMA((self.nsem,))]

    def hooks(self, in_refs, out_refs, sems, first, last):
        nr, na = len(self.reads), len(self.aliased)
        args = (in_refs[:nr], out_refs[:na], out_refs[na:], *sems)

        @pl.when(first)
        def _():
            for cp in self.build(*args):
                cp.start()

        @pl.when(last)
        def _():
            for cp in self.build(*args):
                cp.wait()


def _side_parts(side):
    if side is None:
        return [], [], [], [], lambda n_in, n_out: {}, []
    return side.operands(), side.in_specs(), side.out_specs(), side.out_shape(), side.aliases, side.scratch()


def _call_with_side(body, name, grid, in_specs, out_specs, out_shape, scratch, sem, operands, side, n_prefetch=0):
    s_ops, s_in, s_out, s_shape, s_alias, s_scr = _side_parts(side)
    n_in, n_out = n_prefetch + len(in_specs), len(out_specs)
    n_sin, n_sout = len(s_ops), len(s_out)

    def wrapped(*refs):
        a, b, c = n_in + n_sin, n_in + n_sin + n_out, n_in + n_sin + n_out + n_sout
        ids = [pl.program_id(d) for d in range(len(grid))]
        first = functools.reduce(lambda p, q: p & q, [i == 0 for i in ids])
        last = functools.reduce(lambda p, q: p & q, [i == g - 1 for i, g in zip(ids, grid)])
        side.hooks(refs[n_in:a], refs[b:c], refs[-2:], first, last)
        body(*refs[:n_in], *refs[a:b], *refs[c:-2])

    spec = dict(grid=grid, in_specs=in_specs + s_in, out_specs=out_specs + s_out, scratch_shapes=scratch + s_scr)
    if n_prefetch:
        spec = dict(grid_spec=pltpu.PrefetchScalarGridSpec(num_scalar_prefetch=n_prefetch, **spec))
    return pl.pallas_call(
        body if side is None else wrapped, name=name, out_shape=out_shape + s_shape,
        input_output_aliases=s_alias(n_in, n_out),
        compiler_params=_cparams(sem if side is None else ("arbitrary",) * len(grid)), **spec,
    )(*operands, *s_ops)


def _mm_tk(K):
    if K <= MM_TK:
        return K
    return MM_TK if K % MM_TK == 0 else MM_TK // 2


def _mm(name, a, b, mode, out_dtypes, extras=(), epilogue=None, side=None):
    if mode == "nn":
        (M, K), (K2, N) = a.shape, b.shape
    elif mode == "nt":
        (M, K), (N, K2) = a.shape, b.shape
    else:
        (K, M), (K2, N) = a.shape, b.shape
    assert K == K2, (name, a.shape, b.shape)
    tm, tn, tk = min(MM_TM, M), min(MM_TN, N), _mm_tk(K)
    assert M % tm == 0 and N % tn == 0 and K % tk == 0, (name, M, N, K)
    ni, nj, nk = M // tm, N // tn, K // tk
    ne, no = len(extras), len(out_dtypes)
    if epilogue is None:
        epilogue = lambda acc: (acc,)
    s_ops, s_in, s_out, s_shape, s_alias, s_scr = _side_parts(side)
    n_in, n_sin, n_sout = 2 + ne, len(s_ops), len(s_out)

    def body(*refs):
        a_ref, b_ref = refs[:2]
        extra_refs = refs[2:n_in]
        out_refs = refs[n_in + n_sin:n_in + n_sin + no]
        rest = refs[n_in + n_sin + no + n_sout:]
        i, j, k = pl.program_id(0), pl.program_id(1), pl.program_id(2)
        if side is not None:
            side.hooks(refs[n_in:n_in + n_sin], refs[n_in + n_sin + no:n_in + n_sin + no + n_sout], rest[-2:],
                       (i == 0) & (j == 0) & (k == 0), (i == ni - 1) & (j == nj - 1) & (k == nk - 1))
        av, bv = a_ref[...].astype(BF16), b_ref[...].astype(BF16)
        prod = _nn(av, bv) if mode == "nn" else _nt(av, bv) if mode == "nt" else _tn(av, bv)

        def finish(acc):
            res = epilogue(acc, *[e[...] for e in extra_refs])
            for o_ref, r in zip(out_refs, res):
                o_ref[...] = r.astype(o_ref.dtype)

        if nk == 1:
            finish(prod)
        else:
            acc_ref = rest[0]

            @pl.when(k == 0)
            def _():
                acc_ref[...] = prod

            @pl.when((k > 0) & (k < nk - 1))
            def _():
                acc_ref[...] += prod

            @pl.when(k == nk - 1)
            def _():
                finish(acc_ref[...] + prod)

    if mode == "nn":
        a_spec = pl.BlockSpec((tm, tk), lambda i, j, k: (i, k))
        b_spec = pl.BlockSpec((tk, tn), lambda i, j, k: (k, j))
    elif mode == "nt":
        a_spec = pl.BlockSpec((tm, tk), lambda i, j, k: (i, k))
        b_spec = pl.BlockSpec((tn, tk), lambda i, j, k: (j, k))
    else:
        a_spec = pl.BlockSpec((tk, tm), lambda i, j, k: (k, i))
        b_spec = pl.BlockSpec((tk, tn), lambda i, j, k: (k, j))
    o_spec = pl.BlockSpec((tm, tn), lambda i, j, k: (i, j))
    sem = ("arbitrary",) * 3 if side is not None else ("parallel", "parallel", "arbitrary")
    outs = pl.pallas_call(
        body, name=name,
        grid=(ni, nj, nk),
        in_specs=[a_spec, b_spec] + [o_spec] * ne + s_in,
        out_specs=[o_spec] * no + s_out,
        out_shape=[jax.ShapeDtypeStruct((M, N), dt) for dt in out_dtypes] + s_shape,
        input_output_aliases=s_alias(n_in, no),
        scratch_shapes=([pltpu.VMEM((tm, tn), F32)] if nk > 1 else []) + s_scr,
        compiler_params=_cparams(sem),
    )(a, b, *extras, *s_ops)
    return outs[0] if len(outs) == 1 else outs


def _row_spec(tr, d):
    return pl.BlockSpec((tr, d), lambda i: (i, 0))


def _vec_spec(d):
    return pl.BlockSpec((1, d), lambda i: (0, 0))


def _rms_fwd(name, x, w):
    T, D = x.shape
    tr = min(ROW_TILE, T)

    def body(x_ref, w_ref, o_ref):
        xf = x_ref[...]
        r = lax.rsqrt(jnp.mean(xf * xf, axis=-1, keepdims=True) + EPS)
        o_ref[...] = (xf * r * w_ref[...]).astype(BF16)

    return pl.pallas_call(
        body, name=name, grid=(T // tr,),
        in_specs=[_row_spec(tr, D), _vec_spec(D)], out_specs=_row_spec(tr, D),
        out_shape=jax.ShapeDtypeStruct((T, D), BF16),
        compiler_params=_cparams(("parallel",)),
    )(x, w)


def _rms_bwd(name, dy, h, w, dres, side=None):
    T, D = h.shape
    tr = min(ROW_TILE, T)

    def body(dy_ref, h_ref, w_ref, dres_ref, dh_ref, dhb_ref, dw_ref):
        @pl.when(pl.program_id(0) == 0)
        def _():
            dw_ref[...] = jnp.zeros_like(dw_ref)

        hf, dyv = h_ref[...], dy_ref[...]
        r = lax.rsqrt(jnp.mean(hf * hf, axis=-1, keepdims=True) + EPS)
        xhat = hf * r
        dw_ref[...] += jnp.sum(dyv * xhat, axis=0, keepdims=True)
        dxh = dyv * w_ref[...]
        dh = dres_ref[...] + r * (dxh - xhat * jnp.mean(dxh * xhat, axis=-1, keepdims=True))
        dh_ref[...] = dh
        dhb_ref[...] = dh.astype(BF16)

    return _call_with_side(
        body, name, (T // tr,),
        [_row_spec(tr, D), _row_spec(tr, D), _vec_spec(D), _row_spec(tr, D)],
        [_row_spec(tr, D), _row_spec(tr, D), _vec_spec(D)],
        [jax.ShapeDtypeStruct((T, D), F32), jax.ShapeDtypeStruct((T, D), BF16), jax.ShapeDtypeStruct((1, D), F32)],
        [], ("arbitrary",), (dy, h, w, dres), side)


def _loss_head(h2, target, w):
    T, D = h2.shape
    tr = min(ROW_TILE, T)

    def body(h_ref, t_ref, w_ref, loss_ref, dh_ref, dhb_ref, dw_ref):
        @pl.when(pl.program_id(0) == 0)
        def _():
            dw_ref[...] = jnp.zeros_like(dw_ref)
            loss_ref[...] = jnp.zeros_like(loss_ref)

        hf, wv = h_ref[...], w_ref[...]
        r = lax.rsqrt(jnp.mean(hf * hf, axis=-1, keepdims=True) + EPS)
        xhat = hf * r
        diff = xhat * wv - t_ref[...]
        loss_ref[...] += 0.5 * jnp.sum(jnp.mean(diff * diff, axis=-1, keepdims=True))
        dyv = diff * (1.0 / D)
        dw_ref[...] += jnp.sum(dyv * xhat, axis=0, keepdims=True)
        dxh = dyv * wv
        dh = r * (dxh - xhat * jnp.mean(dxh * xhat, axis=-1, keepdims=True))
        dh_ref[...] = dh
        dhb_ref[...] = dh.astype(BF16)

    return pl.pallas_call(
        body, name="loss_head", grid=(T // tr,),
        in_specs=[_row_spec(tr, D), _row_spec(tr, D), _vec_spec(D)],
        out_specs=[_vec_spec(LANE), _row_spec(tr, D), _row_spec(tr, D), _vec_spec(D)],
        out_shape=[jax.ShapeDtypeStruct((1, LANE), F32), jax.ShapeDtypeStruct((T, D), F32),
                   jax.ShapeDtypeStruct((T, D), BF16), jax.ShapeDtypeStruct((1, D), F32)],
        compiler_params=_cparams(("arbitrary",)),
    )(h2, target, w)


def _gate_tiles(T, D):
    goff = 4 * _hgw() + 3 * _atw()
    tc = min(1024, D)
    assert goff % tc == 0 and D % tc == 0
    return min(ROW_TILE, T), tc, goff // tc, D // tc


def _merge(z, pa, pb):
    T, D = pa.shape
    tr, tc, g0, nd = _gate_tiles(T, D)

    def body(ga_ref, gb_ref, pa_ref, pb_ref, o_ref):
        o_ref[...] = (_sigmoid(ga_ref[...]) * pa_ref[...] + _sigmoid(gb_ref[...]) * pb_ref[...]).astype(BF16)

    t = pl.BlockSpec((tr, tc), lambda i, j: (i, j))
    return pl.pallas_call(
        body, name="merge", grid=(T // tr, nd),
        in_specs=[pl.BlockSpec((tr, tc), lambda i, j: (i, g0 + j)),
                  pl.BlockSpec((tr, tc), lambda i, j: (i, g0 + nd + j)), t, t],
        out_specs=t, out_shape=jax.ShapeDtypeStruct((T, D), BF16),
        compiler_params=_cparams(("parallel", "parallel")),
    )(z, z, pa, pb)


def _dmerge(dm, z, pa, pb):
    T, D = pa.shape
    tr, tc, g0, nd = _gate_tiles(T, D)

    def body(dm_ref, ga_ref, gb_ref, pa_ref, pb_ref, dpa_ref, dpb_ref, dga_ref, dgb_ref):
        dmv = dm_ref[...]
        sa, sb = _sigmoid(ga_ref[...]), _sigmoid(gb_ref[...])
        dpa_ref[...] = (dmv * sa).astype(BF16)
        dpb_ref[...] = (dmv * sb).astype(BF16)
        dga_ref[...] = (dmv * pa_ref[...] * sa * (1.0 - sa)).astype(BF16)
        dgb_ref[...] = (dmv * pb_ref[...] * sb * (1.0 - sb)).astype(BF16)

    t = pl.BlockSpec((tr, tc), lambda i, j: (i, j))
    return pl.pallas_call(
        body, name="dmerge", grid=(T // tr, nd),
        in_specs=[t, pl.BlockSpec((tr, tc), lambda i, j: (i, g0 + j)),
                  pl.BlockSpec((tr, tc), lambda i, j: (i, g0 + nd + j)), t, t],
        out_specs=[t, t, t, t],
        out_shape=[jax.ShapeDtypeStruct((T, D), BF16)] * 4,
        compiler_params=_cparams(("parallel", "parallel")),
    )(dm, z, z, pa, pb)


def _hg_gates(xq, xf, lb):
    f = _sigmoid(xf)
    g = lb + (1.0 - lb) * f
    sq = _sigmoid(xq)
    return f, g, jnp.log(g), 1.0 - g, sq, xq * sq * (HG_D ** -0.5)


def _hg_decays(lg, tri_incl, rowi):
    b = _nn(tri_incl, lg, precision=HIGHEST)
    b_last = jnp.sum(lg, axis=0, keepdims=True)
    b_mid = jnp.sum(jnp.where(rowi <= CHUNK // 2, lg, 0.0), axis=0, keepdims=True)
    return b, b_last, b_mid


def _hg_in_specs(T):
    H = HG_HEADS
    return [pl.BlockSpec((T, HG_D), lambda h, s=s: (0, s * H + h)) for s in range(4)]


def _hg_fwd(z, lb_logits, hgw, side=None):
    T = z.shape[0]
    H, d, C = HG_HEADS, HG_D, CHUNK
    nc = T // C

    def body(hq_ref, hf_ref, hi_ref, hg_ref, lbl_ref, w_ref, ya_ref, o_ref, s_ref):
        lb = 1.0 / (1.0 + jnp.exp(lbl_ref[1:2, :] - lbl_ref[0:1, :]))
        wv = w_ref[...]
        row = lax.broadcasted_iota(jnp.int32, (C, C), 0)
        col = lax.broadcasted_iota(jnp.int32, (C, C), 1)
        tril = col <= row
        tri_incl = tril.astype(F32)
        rowi = lax.broadcasted_iota(jnp.int32, (C, d), 0)

        def chunk(c, st):
            rows = pl.ds(pl.multiple_of(c * C, C), C)
            xq, xf, v, xg = hq_ref[rows, :], hf_ref[rows, :], hi_ref[rows, :], hg_ref[rows, :]
            _, _, lg, kk, _, q = _hg_gates(xq, xf, lb)
            b, b_last, b_mid = _hg_decays(lg, tri_incl, rowi)
            st_b = st.astype(BF16)
            s_ref[c] = st
            vb = v.astype(BF16)
            o = _nt((q * jnp.exp(b)).astype(BF16), st_b)
            qt = (q * jnp.exp(b - b_mid)).astype(BF16)
            kt = (kk * jnp.exp(jnp.minimum(b_mid - b, EXP_CLAMP))).astype(BF16)
            a = jnp.where(tril, _nt(qt, kt), 0.0).astype(BF16)
            o = o + _nn(a, vb)
            st_new = st * jnp.exp(b_last) + _tn(v, kk * jnp.exp(b_last - b), HIGHEST)
            o_ref[rows, :] = o
            r = lax.rsqrt(jnp.mean(o * o, axis=-1, keepdims=True) + EPS)
            ya_ref[rows, :] = (o * r * wv * (xg * _sigmoid(xg))).astype(BF16)
            return st_new

        lax.fori_loop(0, nc // 2, lambda i, st: chunk(2 * i + 1, chunk(2 * i, st)), jnp.zeros((d, d), F32))

    head = pl.BlockSpec((T, d), lambda h: (0, h))
    return _call_with_side(
        body, "hg_fwd", (H,),
        _hg_in_specs(T) + [pl.BlockSpec((2, d), lambda h: (0, h)), pl.BlockSpec((1, d), lambda h: (0, 0))],
        [head, head, pl.BlockSpec((None, nc, d, d), lambda h: (h, 0, 0, 0))],
        [jax.ShapeDtypeStruct((T, H * d), BF16), jax.ShapeDtypeStruct((T, H * d), F32),
         jax.ShapeDtypeStruct((H, nc, d, d), F32)],
        [], ("parallel",), (z, z, z, z, lb_logits, hgw), side)


def _hg_bwd(z, o, dya, states, lb_logits, hgw, side=None):
    T = z.shape[0]
    H, d, C = HG_HEADS, HG_D, CHUNK
    nc = T // C
    scale = HG_D ** -0.5

    def body(hq_ref, hf_ref, hi_ref, hg_ref, o_ref, dy_ref, s_ref, lbl_ref, w_ref,
             dq_ref, df_ref, di_ref, dg_ref, dlbl_ref, dw_ref, acc_ref):
        lb = 1.0 / (1.0 + jnp.exp(lbl_ref[1:2, :] - lbl_ref[0:1, :]))
        wv = w_ref[...]
        row = lax.broadcasted_iota(jnp.int32, (C, C), 0)
        col = lax.broadcasted_iota(jnp.int32, (C, C), 1)
        tril = col <= row
        tri_incl = tril.astype(F32)
        triu_incl = (col >= row).astype(F32)
        rowi = lax.broadcasted_iota(jnp.int32, (C, d), 0)
        acc_ref[...] = jnp.zeros_like(acc_ref)

        @pl.when(pl.program_id(0) == 0)
        def _():
            dw_ref[...] = jnp.zeros_like(dw_ref)

        def chunk(i, carry):
            dst, tail = carry
            c = nc - 1 - i
            rows = pl.ds(pl.multiple_of(c * C, C), C)
            xq, xf, v, xg = hq_ref[rows, :], hf_ref[rows, :], hi_ref[rows, :], hg_ref[rows, :]
            f, g, lg, kk, sq, q = _hg_gates(xq, xf, lb)
            b, b_last, b_mid = _hg_decays(lg, tri_incl, rowi)
            e_b, e_qm, e_km = jnp.exp(b), jnp.exp(b - b_mid), jnp.exp(jnp.minimum(b_mid - b, EXP_CLAMP))
            e_kl, e_last = jnp.exp(b_last - b), jnp.exp(b_last)
            ov, dy = o_ref[rows, :], dy_ref[rows, :]
            r = lax.rsqrt(jnp.mean(ov * ov, axis=-1, keepdims=True) + EPS)
            xhat = ov * r
            sg = _sigmoid(xg)
            dxg = dy * xhat * wv * (sg * (1.0 + xg * (1.0 - sg)))
            dyn = dy * (xg * sg)
            acc_ref[0:1, :] += jnp.sum(dyn * xhat, axis=0, keepdims=True)
            dxh = dyn * wv
            dof = r * (dxh - xhat * jnp.mean(dxh * xhat, axis=-1, keepdims=True))
            do, vb = dof.astype(BF16), v.astype(BF16)
            qt, kt = q * e_qm, kk * e_km
            pm = jnp.where(tril, _nt(do, vb), 0.0)
            am = jnp.where(tril, _nt(qt.astype(BF16), kt.astype(BF16)), 0.0).astype(BF16)
            dq = _nn(dof, s_ref[c], HIGHEST) * e_b + _nn(pm, kt, HIGHEST) * e_qm
            dk = _tn(pm, qt, HIGHEST) * e_km + _nn(v, dst, HIGHEST) * e_kl
            dv = _tn(am, do) + _nt((kk * e_kl).astype(BF16), dst.astype(BF16))
            dst_prev = dst * e_last + _tn(dof, q * e_b, HIGHEST)
            db = q * dq - kk * dk
            dlg = _nn(triu_incl, db, precision=HIGHEST) + tail
            dgate = dlg / g - dk
            acc_ref[1:2, :] += jnp.sum(dgate * (1.0 - f), axis=0, keepdims=True)
            dq_ref[rows, :] = (dq * scale * (sq * (1.0 + xq * (1.0 - sq)))).astype(BF16)
            df_ref[rows, :] = (dgate * (1.0 - lb) * f * (1.0 - f)).astype(BF16)
            di_ref[rows, :] = dv.astype(BF16)
            dg_ref[rows, :] = dxg.astype(BF16)
            return dst_prev, tail + jnp.sum(db, axis=0, keepdims=True)

        lax.fori_loop(0, nc // 2, lambda i, carry: chunk(2 * i + 1, chunk(2 * i, carry)),
                      (jnp.zeros((d, d), F32), jnp.zeros((1, d), F32)))
        dw_ref[...] += acc_ref[0:1, :]
        dl0 = acc_ref[1:2, :] * lb * (1.0 - lb)
        dlbl_ref[0:1, :] = dl0
        dlbl_ref[1:2, :] = -dl0

    head = pl.BlockSpec((T, d), lambda h: (0, h))
    return _call_with_side(
        body, "hg_bwd", (H,),
        _hg_in_specs(T) + [head, head, pl.BlockSpec((None, nc, d, d), lambda h: (h, 0, 0, 0)),
                           pl.BlockSpec((2, d), lambda h: (0, h)), pl.BlockSpec((1, d), lambda h: (0, 0))],
        [head, head, head, head, pl.BlockSpec((2, d), lambda h: (0, h)), pl.BlockSpec((1, d), lambda h: (0, 0))],
        [jax.ShapeDtypeStruct((T, H * d), BF16)] * 4 + [jax.ShapeDtypeStruct((2, H * d), F32),
                                                        jax.ShapeDtypeStruct((1, d), F32)],
        [pltpu.VMEM((8, d), F32)], ("arbitrary",), (z, z, z, z, o, dya, states, lb_logits, hgw), side)


def _at_dims():
    pad = LEFT * CHUNK
    return pad, QB + pad, AT_HEADS * AT_DH // LANE, 4 * _hgw() // LANE


def _rel_of_period():
    pad, W, _, _ = _at_dims()
    n = jnp.arange(QB + W)
    return jnp.clip(pad - jnp.where(n < W, n, n - (QB + W)), -REL_CLIP, REL_CLIP) + REL_CLIP


def _bias_window(rel_bias):
    pad, W, _, _ = _at_dims()
    H, P = rel_bias.shape[0], QB + W
    per = rel_bias[:, _rel_of_period()]
    win = jnp.tile(per, (1, QB))[:, :QB * (P - 1)].reshape(H, QB, P - 1)[:, :, :W]
    t = jnp.arange(QB)[:, None]
    j = jnp.arange(W)[None, :]
    ok = (j // CHUNK >= t // CHUNK) & (j // CHUNK <= t // CHUNK + LEFT)
    return jnp.where(ok[None], win, NEG)


def _bias_window_grad(dbw):
    pad, W, _, _ = _at_dims()
    H, P = dbw.shape[0], QB + W
    flat = jnp.pad(dbw, ((0, 0), (0, 0), (0, P - 1 - W))).reshape(H, QB * (P - 1))
    per = jnp.pad(flat, ((0, 0), (0, QB))).reshape(H, QB, P).sum(axis=1)
    onehot = _rel_of_period()[:, None] == jnp.arange(2 * REL_CLIP + 1)[None, :]
    return jnp.dot(per, onehot.astype(F32), precision=HIGHEST)


def _at_softmax(q_half, kw, bias, valid):
    s = _nt(q_half, kw) * (AT_DH ** -0.5) + bias
    s = jnp.where(valid, s, NEG)
    e = jnp.exp(s - jnp.max(s, axis=-1, keepdims=True))
    return e / jnp.sum(e, axis=-1, keepdims=True)


def _at_fwd(z, bias_win, side=None):
    T = z.shape[0]
    pad, W, HP, c0 = _at_dims()
    nq = T // QB

    def body(q_ref, k_ref, v_ref, bias_ref, o_ref, kpad, vpad):
        qi = pl.program_id(1)

        @pl.when(qi == 0)
        def _():
            kpad[0:pad, :] = jnp.zeros((pad, LANE), BF16)
            vpad[0:pad, :] = jnp.zeros((pad, LANE), BF16)
            kpad[pad:, :] = k_ref[...].astype(BF16)
            vpad[pad:, :] = v_ref[...].astype(BF16)

        win = pl.ds(pl.multiple_of(qi * QB, QB), W)
        kw, vw = kpad[win, :], vpad[win, :]
        q = q_ref[...]
        lane = lax.broadcasted_iota(jnp.int32, (QB, LANE), 1)
        first = lane < AT_DH
        valid = lax.broadcasted_iota(jnp.int32, (QB, W), 1) + qi * QB >= pad
        pa = _at_softmax(jnp.where(first, q, 0.0).astype(BF16), kw, bias_ref[0], valid)
        pb = _at_softmax(jnp.where(first, 0.0, q).astype(BF16), kw, bias_ref[1], valid)
        o_ref[...] = jnp.where(first, _nn(pa.astype(BF16), vw), _nn(pb.astype(BF16), vw)).astype(BF16)

    full = lambda s: pl.BlockSpec((T, LANE), lambda hp, qi, s=s: (0, c0 + s * HP + hp))
    return _call_with_side(
        body, "at_fwd", (HP, nq),
        [pl.BlockSpec((QB, LANE), lambda hp, qi: (qi, c0 + hp)), full(1), full(2),
         pl.BlockSpec((2, QB, W), lambda hp, qi: (hp, 0, 0))],
        [pl.BlockSpec((QB, LANE), lambda hp, qi: (qi, hp))],
        [jax.ShapeDtypeStruct((T, HP * LANE), BF16)],
        [pltpu.VMEM((T + pad, LANE), BF16)] * 2, ("parallel", "arbitrary"), (z, z, z, bias_win), side)


def _at_bwd(z, dyb, bias_win, side=None):
    T = z.shape[0]
    pad, W, HP, c0 = _at_dims()
    nq = T // QB
    scale = AT_DH ** -0.5

    def body(q_ref, k_ref, v_ref, do_ref, bias_ref, dq_ref, dk_ref, dv_ref, dbias_ref, kpad, vpad, dkpad, dvpad):
        qi = pl.program_id(1)

        @pl.when(qi == 0)
        def _():
            kpad[0:pad, :] = jnp.zeros((pad, LANE), BF16)
            vpad[0:pad, :] = jnp.zeros((pad, LANE), BF16)
            kpad[pad:, :] = k_ref[...].astype(BF16)
            vpad[pad:, :] = v_ref[...].astype(BF16)
            dkpad[...] = jnp.zeros_like(dkpad)
            dvpad[...] = jnp.zeros_like(dvpad)
            dbias_ref[...] = jnp.zeros_like(dbias_ref)

        win = pl.ds(pl.multiple_of(qi * QB, QB), W)
        kw, vw = kpad[win, :], vpad[win, :]
        q, do = q_ref[...], do_ref[...]
        lane = lax.broadcasted_iota(jnp.int32, (QB, LANE), 1)
        first = lane < AT_DH
        valid = lax.broadcasted_iota(jnp.int32, (QB, W), 1) + qi * QB >= pad

        def half(hh, qh, doh):
            p = _at_softmax(qh, kw, bias_ref[hh], valid)
            dp = _nt(doh, vw)
            ds = p * (dp - jnp.sum(p * dp, axis=-1, keepdims=True))
            dbias_ref[hh] += ds
            dss = (ds * scale).astype(BF16)
            return _nn(dss, kw), _tn(dss, qh), _tn(p.astype(BF16), doh)

        dqa, dka, dva = half(0, jnp.where(first, q, 0.0).astype(BF16), jnp.where(first, do, 0.0).astype(BF16))
        dqb, dkb, dvb = half(1, jnp.where(first, 0.0, q).astype(BF16), jnp.where(first, 0.0, do).astype(BF16))
        dq_ref[...] = jnp.where(first, dqa, dqb).astype(BF16)
        dkpad[win, :] += dka + dkb
        dvpad[win, :] += dva + dvb

        @pl.when(qi == nq - 1)
        def _():
            dk_ref[...] = dkpad[pad:, :].astype(BF16)
            dv_ref[...] = dvpad[pad:, :].astype(BF16)

    full = lambda s: pl.BlockSpec((T, LANE), lambda hp, qi, s=s: (0, c0 + s * HP + hp))
    blk = pl.BlockSpec((QB, LANE), lambda hp, qi: (qi, hp))
    col = pl.BlockSpec((T, LANE), lambda hp, qi: (0, hp))
    bw = pl.BlockSpec((2, QB, W), lambda hp, qi: (hp, 0, 0))
    return _call_with_side(
        body, "at_bwd", (HP, nq),
        [pl.BlockSpec((QB, LANE), lambda hp, qi: (qi, c0 + hp)), full(1), full(2), blk, bw],
        [blk, col, col, bw],
        [jax.ShapeDtypeStruct((T, HP * LANE), BF16)] * 3 + [jax.ShapeDtypeStruct(bias_win.shape, F32)],
        [pltpu.VMEM((T + pad, LANE), BF16)] * 2 + [pltpu.VMEM((T + pad, LANE), F32)] * 2,
        ("parallel", "arbitrary"), (z, z, z, dyb, bias_win), side)


def _piece_tiles(name, full_shape):
    pr, pc = _piece_shape(name, full_shape)
    tr = min(ROW_TILE, pr)
    assert pr % tr == 0
    nt = pr // tr
    if name in ROW_SHARDED:
        return tr, nt, lambda q, half, i: ((2 * q + half) * nt + i, 0)
    return tr, nt, lambda q, half, i: (half * nt + i, q)


def _cast_into_full(name, wq, place):
    full = _full_shape(name, wq.shape)
    pc = wq.shape[1]
    tr, nt, at = _piece_tiles(name, full)

    def body(place_ref, w_ref, o_ref):
        o_ref[...] = w_ref[...].astype(BF16)

    return pl.pallas_call(
        body, name="cast_" + name,
        grid_spec=pltpu.PrefetchScalarGridSpec(
            num_scalar_prefetch=1, grid=(2, nt),
            in_specs=[pl.BlockSpec((tr, pc), lambda h, i, s: (h * nt + i, 0))],
            out_specs=pl.BlockSpec((tr, pc), lambda h, i, s: at(s[0], h, i))),
        out_shape=jax.ShapeDtypeStruct(full, BF16),
        compiler_params=_cparams(("parallel", "parallel")),
    )(place, wq)


def _chip_sum(name, grad, theirs, place):
    pr, pc = theirs.shape[1:]
    tr, nt, at = _piece_tiles(name, grad.shape)

    def body(place_ref, g_ref, t_ref, o_ref):
        o_ref[...] = (g_ref[...].astype(F32) + t_ref[...].astype(F32)).astype(BF16)

    piece = pl.BlockSpec((None, tr, pc), lambda q, i, s: (q, i, 0))
    return pl.pallas_call(
        body, name="chip_sum_" + name,
        grid_spec=pltpu.PrefetchScalarGridSpec(
            num_scalar_prefetch=1, grid=(4, nt),
            in_specs=[pl.BlockSpec((tr, pc), lambda q, i, s: at(q, s[1], i)), piece], out_specs=piece),
        out_shape=jax.ShapeDtypeStruct(theirs.shape, BF16),
        compiler_params=_cparams(("parallel", "parallel")),
    )(place, grad, theirs)


def _piece_sum(name, chip_sums, got, place):
    pr, pc = chip_sums.shape[1:]
    tr = min(ROW_TILE, pr)

    def body(place_ref, own_ref, got_ref, o_ref):
        o_ref[...] = (own_ref[...].astype(F32) + got_ref[0].astype(F32) + got_ref[1].astype(F32)
                      + got_ref[2].astype(F32))

    return pl.pallas_call(
        body, name="piece_sum_" + name,
        grid_spec=pltpu.PrefetchScalarGridSpec(
            num_scalar_prefetch=1, grid=(pr // tr,),
            in_specs=[pl.BlockSpec((None, tr, pc), lambda i, s: (s[0], i, 0)),
                      pl.BlockSpec((3, tr, pc), lambda i, s: (0, i, 0))],
            out_specs=pl.BlockSpec((tr, pc), lambda i, s: (i, 0))),
        out_shape=jax.ShapeDtypeStruct((pr, pc), F32),
        compiler_params=_cparams(("parallel",)),
    )(place, chip_sums, got)


def _adam_quarter(name, w, m, v, g_mine, g_sib, place, side=None):
    pr, pc = g_mine.shape
    tr = min(ROW_TILE // 2, pr)
    nt = pr // tr

    def body(place_ref, w_ref, m_ref, v_ref, gm_ref, gs_ref, go_ref, d_ref, mo_ref, vo_ref):
        g = jnp.where(pl.program_id(0) == place_ref[1], gm_ref[...], gs_ref[...])
        delta, mn, vn = _adam_math(w_ref[...], g, m_ref[...], v_ref[...])
        go_ref[...] = g
        d_ref[...] = delta
        mo_ref[...] = mn
        vo_ref[...] = vn

    quarter = pl.BlockSpec((tr, pc), lambda h, i, s: (h * nt + i, 0))
    mine = pl.BlockSpec((tr, pc), lambda h, i, s: (jnp.where(h == s[1], i, 0), 0))
    sib = pl.BlockSpec((tr, pc), lambda h, i, s: (jnp.where(h == s[1], 0, i), 0))
    return _call_with_side(
        body, "adam_" + name, (2, nt), [quarter, quarter, quarter, mine, sib], [quarter] * 4,
        [jax.ShapeDtypeStruct(w.shape, F32)] * 4, [], ("parallel", "parallel"),
        (place, w, m, v, g_mine, g_sib), side, n_prefetch=1)


def _adam_math(w, g, m, v):
    m = ADAM_B1 * m + (1.0 - ADAM_B1) * g
    v = ADAM_B2 * v + (1.0 - ADAM_B2) * (g * g)
    m_hat = m / (1.0 - ADAM_B1 ** ADAM_STEP)
    v_hat = v / (1.0 - ADAM_B2 ** ADAM_STEP)
    return -ADAM_LR * (m_hat / (jnp.sqrt(v_hat) + ADAM_EPS) + ADAM_WD * w), m, v


WEIGHTS = ("w_in", "w_branch_a", "w_branch_b", "w_out", "w_up", "w_down")
ROW_SHARDED = ("w_out", "w_down")
ANY = pl.BlockSpec(memory_space=pl.ANY)
MESH = pl.DeviceIdType.MESH


def _place():
    x, y, c = lax.axis_index("x"), lax.axis_index("y"), lax.axis_index("c")
    chips = [(1 - x, y), (x, 1 - y), (1 - x, 1 - y)]
    return x, y, c, 2 * x + y, chips, [2 * cx + cy for cx, cy in chips]


def _piece(full_ref, name, q, half):
    K, N = full_ref.shape
    if name in ROW_SHARDED:
        rows = K // 8
        return full_ref.at[pl.ds(q * (2 * rows) + half * rows, rows), :]
    return full_ref.at[pl.ds(half * (K // 2), K // 2), pl.ds(q * (N // 4), N // 4)]


def _piece_shape(name, full_shape):
    K, N = full_shape
    return (K // 8, N) if name in ROW_SHARDED else (K // 2, N // 4)


def _full_shape(name, quarter_shape):
    Kq, Nq = quarter_shape
    return (4 * Kq, Nq) if name in ROW_SHARDED else (Kq, 4 * Nq)


def _remote(src, dst, send_sem, recv_sem, device):
    return pltpu.make_async_remote_copy(src_ref=src, dst_ref=dst, send_sem=send_sem, recv_sem=recv_sem,
                                        device_id=device, device_id_type=MESH)


def _gather_weights(names, fulls):
    n = len(names)

    def body(*refs):
        f_refs = refs[n:2 * n]
        send_sems, recv_sems = refs[2 * n:]
        x, y, c, p, chips, chip_ids = _place()
        sib = (x, y, 1 - c)
        sends = []
        for i, name in enumerate(names):
            mine = _piece(f_refs[i], name, p, c)
            for j, chip in enumerate(chips):
                cp = _remote(mine, mine, send_sems.at[i, j], recv_sems.at[i, j], (*chip, c))
                cp.start()
                sends.append(cp)
        for i, name in enumerate(names):
            for j, cid in enumerate(chip_ids):
                landed = _piece(f_refs[i], name, cid, c)
                _remote(landed, landed, send_sems.at[i, j], recv_sems.at[i, j], sib).wait_recv()
                cp = _remote(landed, landed, send_sems.at[i, 3 + j], recv_sems.at[i, 3 + j], sib)
                cp.start()
                sends.append(cp)
        for i, name in enumerate(names):
            for j, cid in enumerate(chip_ids):
                other = _piece(f_refs[i], name, cid, 1 - c)
                _remote(other, other, send_sems.at[i, 3 + j], recv_sems.at[i, 3 + j], sib).wait_recv()
        for cp in sends:
            cp.wait_send()

    return pl.pallas_call(
        body, name="gather_" + "_".join(names),
        in_specs=[ANY] * n, out_specs=[ANY] * n,
        out_shape=[jax.ShapeDtypeStruct(f.shape, BF16) for f in fulls],
        input_output_aliases={i: i for i in range(n)},
        scratch_shapes=[pltpu.SemaphoreType.DMA((n, 6)), pltpu.SemaphoreType.DMA((n, 6))],
    )(*fulls)


def _rows(ref, span):
    return ref if span is None else ref.at[pl.ds(span[0], span[1]), :]


def _ici_gather(names, fulls, rows=None):
    rows = rows or [None] * len(names)

    def build(reads, aliased, fresh, send_sems, recv_sems, off=0):
        _, _, c, p, chips, _ = _place()
        out = []
        for i, (ref, name) in enumerate(zip(aliased, names)):
            mine = _rows(_piece(ref, name, p, c), rows[i])
            for j, chip in enumerate(chips):
                k = off + 3 * i + j
                out.append(_remote(mine, mine, send_sems.at[k], recv_sems.at[k], (*chip, c)))
        return out

    return _Side(build, 3 * len(names), aliased=fulls)


def _d2d_gather(names, fulls):
    def build(reads, aliased, fresh, send_sems, recv_sems, off=0):
        x, y, c, _, _, chip_ids = _place()
        out = []
        for i, (ref, name) in enumerate(zip(aliased, names)):
            for j, cid in enumerate(chip_ids):
                landed, k = _piece(ref, name, cid, c), off + 3 * i + j
                out.append(_remote(landed, landed, send_sems.at[k], recv_sems.at[k], (x, y, 1 - c)))
        return out

    return _Side(build, 3 * len(names), aliased=fulls)


def _sib_send(names, grads):
    def build(reads, aliased, fresh, send_sems, recv_sems, off=0):
        x, y, c, _, _, _ = _place()
        out = []
        for i, name in enumerate(names):
            for q in range(4):
                k = off + 4 * i + q
                out.append(_remote(_piece(reads[i], name, q, 1 - c), fresh[i].at[q], send_sems.at[k], recv_sems.at[k],
                                   (x, y, 1 - c)))
        return out

    shapes = [jax.ShapeDtypeStruct((4,) + _piece_shape(name, g.shape), BF16) for name, g in zip(names, grads)]
    return _Side(build, 4 * len(names), reads=grads, fresh=shapes)


def _chip_exchange(chip_sums, rows=None, got=None):
    rows = rows or [None] * len(chip_sums)

    def build(reads, aliased, fresh, send_sems, recv_sems, off=0):
        _, _, c, _, chips, chip_ids = _place()
        out = []
        for i in range(len(chip_sums)):
            for j, (chip, cid) in enumerate(zip(chips, chip_ids)):
                k = off + 3 * i + j
                out.append(_remote(_rows(reads[i].at[cid], rows[i]), _rows((aliased or fresh)[i].at[j], rows[i]),
                                   send_sems.at[k], recv_sems.at[k], (*chip, c)))
        return out

    if got is not None:
        return _Side(build, 3 * len(chip_sums), reads=chip_sums, aliased=got)
    shapes = [jax.ShapeDtypeStruct((3,) + s.shape[1:], BF16) for s in chip_sums]
    return _Side(build, 3 * len(chip_sums), reads=chip_sums, fresh=shapes)


HBM = pl.BlockSpec(memory_space=pltpu.HBM)
SEM = pl.BlockSpec(memory_space=pltpu.SEMAPHORE)


def _exchange_copies(s_ref, land_ref, send_sems, recv_sems):
    _, _, c, _, chips, chip_ids = _place()
    return [_remote(s_ref.at[cid], land_ref.at[j], send_sems.at[j], recv_sems.at[j], (*chip, c))
            for j, (chip, cid) in enumerate(zip(chips, chip_ids))]


def _exchange_start(name, chip_sum):
    def body(s_ref, land_ref, send_sems, recv_sems, s_thru, land_thru, token):
        for cp in _exchange_copies(s_ref, land_ref, send_sems, recv_sems):
            cp.start()
        token[...] = jnp.zeros_like(token)

    land = jax.ShapeDtypeStruct((3,) + chip_sum.shape[1:], chip_sum.dtype)
    return pl.pallas_call(
        body, name="exchange_start_" + name,
        out_shape=(pltpu.SemaphoreType.DMA((3,)), pltpu.SemaphoreType.DMA((3,)),
                   pltpu.HBM(chip_sum.shape, chip_sum.dtype), pltpu.HBM(land.shape, land.dtype),
                   jax.ShapeDtypeStruct((8, LANE), F32)),
        in_specs=(HBM, HBM), out_specs=(SEM, SEM, HBM, HBM, pl.BlockSpec(memory_space=pltpu.VMEM)),
        input_output_aliases={0: 2, 1: 3},
        compiler_params=pltpu.CompilerParams(has_side_effects=pltpu.SideEffectType.DATAFLOW_SIDE_EFFECTING),
    )(pltpu.with_memory_space_constraint(chip_sum, pltpu.HBM),
      pltpu.with_memory_space_constraint(lax.empty(land.shape, land.dtype), pltpu.HBM))


def _exchange_wait(name, flight, after):
    send_sems, recv_sems, s_thru, land_thru, _ = flight

    def body(s_ref, land_ref, send_sems, recv_sems, after_ref, s_out, land_out):
        for cp in _exchange_copies(s_ref, land_ref, send_sems, recv_sems):
            cp.wait_send()
            cp.wait_recv()

    return pl.pallas_call(
        body, name="exchange_wait_" + name,
        out_shape=(pltpu.HBM(s_thru.shape, s_thru.dtype), pltpu.HBM(land_thru.shape, land_thru.dtype)),
        in_specs=(HBM, HBM, SEM, SEM, ANY), out_specs=(HBM, HBM), input_output_aliases={0: 0, 1: 1},
        compiler_params=pltpu.CompilerParams(has_side_effects=pltpu.SideEffectType.DATAFLOW_SIDE_EFFECTING),
    )(s_thru, land_thru, send_sems, recv_sems, after)


def _sib_share(halves):
    def build(reads, aliased, fresh, send_sems, recv_sems, off=0):
        x, y, c, _, _, _ = _place()
        return [_remote(reads[i], fresh[i], send_sems.at[off + i], recv_sems.at[off + i], (x, y, 1 - c))
                for i in range(len(halves))]

    return _Side(build, len(halves), reads=halves, fresh=[jax.ShapeDtypeStruct(h.shape, F32) for h in halves])


def _join(a, b):
    def build(reads, aliased, fresh, send_sems, recv_sems, off=0):
        ra, aa, fa = len(a.reads), len(a.aliased), len(a.fresh)
        return (a.build(reads[:ra], aliased[:aa], fresh[:fa], send_sems, recv_sems, off)
                + b.build(reads[ra:], aliased[aa:], fresh[fa:], send_sems, recv_sems, off + a.nsem))

    return _Side(build, a.nsem + b.nsem, a.reads + b.reads, a.aliased + b.aliased, a.fresh + b.fresh)


def _run_side(name, side):
    nr, na = len(side.reads), len(side.aliased)

    def body(*refs):
        n_in, n_out = nr + na, na + len(side.fresh)
        outs = refs[n_in:n_in + n_out]
        copies = side.build(refs[:nr], outs[:na], outs[na:], *refs[-2:])
        for cp in copies:
            cp.start()
        for cp in copies:
            cp.wait()

    return pl.pallas_call(
        body, name=name, in_specs=side.in_specs(), out_specs=side.out_specs(), out_shape=side.out_shape(),
        input_output_aliases=side.aliases(0, 0), scratch_shapes=side.scratch(),
    )(*side.operands())


def _small_allreduce_adam(gpart, w, m, v):
    R = gpart.shape[0]

    def body(g_ref, w_ref, m_ref, v_ref, go_ref, d_ref, mo_ref, vo_ref, buf, send_sems, recv_sems):
        x, y, c = lax.axis_index("x"), lax.axis_index("y"), lax.axis_index("c")
        me = 4 * x + 2 * y + c
        buf[me] = g_ref[...]
        copies = []
        for k in range(1, 8):
            fx, fy, fc = (k >> 2) & 1, (k >> 1) & 1, k & 1
            peer = (1 - x if fx else x, 1 - y if fy else y, 1 - c if fc else c)
            cp = _remote(g_ref, buf.at[me], send_sems.at[k - 1], recv_sems.at[k - 1], peer)
            cp.start()
            copies.append((cp, 4 * peer[0] + 2 * peer[1] + peer[2]))
        for k, (cp, pid) in enumerate(copies):
            _remote(g_ref, buf.at[pid], send_sems.at[k], recv_sems.at[k], (x, y, c)).wait_recv()
        for cp, _ in copies:
            cp.wait_send()
        g = buf[0]
        for d in range(1, 8):
            g = g + buf[d]
        delta, mn, vn = _adam_math(w_ref[...], g, m_ref[...], v_ref[...])
        go_ref[...] = g
        d_ref[...] = delta
        mo_ref[...] = mn
        vo_ref[...] = vn

    vm = pl.BlockSpec(memory_space=pltpu.VMEM)
    return pl.pallas_call(
        body, name="small_allreduce_adam",
        in_specs=[vm] * 4, out_specs=[vm] * 4,
        out_shape=[jax.ShapeDtypeStruct((R, LANE), F32)] * 4,
        scratch_shapes=[pltpu.VMEM((8, R, LANE), F32), pltpu.SemaphoreType.DMA((7,)), pltpu.SemaphoreType.DMA((7,))],
    )(gpart, w, m, v)


def _pack(arrs):
    flat = jnp.concatenate([a.reshape(-1).astype(F32) for a in arrs])
    rows = -(-flat.shape[0] // (8 * LANE)) * 8
    return jnp.pad(flat, (0, rows * LANE - flat.shape[0])).reshape(rows, LANE)


def _unpack(packed, like):
    flat, out, off = packed.reshape(-1), [], 0
    for a in like:
        out.append(flat[off:off + a.size].reshape(a.shape))
        off += a.size
    return out


def kernel(x, w_in, lb_logits, hg_norm_w, rel_bias, w_branch_a, w_branch_b, w_out, norm_mix_w, norm_mlp_w, w_up, w_down, norm_final_w, loss_target, m_w_in, m_lb_logits, m_hg_norm_w, m_rel_bias, m_w_branch_a, m_w_branch_b, m_w_out, m_norm_mix_w, m_norm_mlp_w, m_w_up, m_w_down, m_norm_final_w, v_w_in, v_lb_logits, v_hg_norm_w, v_rel_bias, v_w_branch_a, v_w_branch_b, v_w_out, v_norm_mix_w, v_norm_mlp_w, v_w_up, v_w_down, v_norm_final_w):
    T, D = x.shape[1], x.shape[2]
    x2, tgt = x.reshape(T, D), loss_target.reshape(T, D)
    big = dict(w_in=(w_in, m_w_in, v_w_in), w_branch_a=(w_branch_a, m_w_branch_a, v_w_branch_a),
               w_branch_b=(w_branch_b, m_w_branch_b, v_w_branch_b), w_out=(w_out, m_w_out, v_w_out),
               w_up=(w_up, m_w_up, v_w_up), w_down=(w_down, m_w_down, v_w_down))
    big = {k: tuple(a[0] for a in v) for k, v in big.items()}
    nfw = norm_final_w.reshape(1, D)

    place = jnp.stack([2 * lax.axis_index("x") + lax.axis_index("y"), lax.axis_index("c")]).astype(jnp.int32)
    Wf = {name: _cast_into_full(name, big[name][0], place) for name in WEIGHTS}
    small3 = ["w_branch_a", "w_branch_b", "w_out"]
    (Wf["w_in"],) = _gather_weights(["w_in"], [Wf["w_in"]])

    def span(name, lo, hi):
        pr = _piece_shape(name, Wf[name].shape)[0]
        return (pr * lo // 16, pr * (hi - lo) // 16)

    u1 = _rms_fwd("norm_mix", x2, norm_mix_w)
    z, *moved = _mm("z_proj", u1, Wf["w_in"], "nn", [F32],
                    side=_ici_gather(small3 + ["w_up"], [Wf[n] for n in small3 + ["w_up"]],
                                     rows=[None] * 3 + [span("w_up", 0, 2)]))
    ya, o_hg, states, Wf["w_up"], *moved = _hg_fwd(
        z, lb_logits, hg_norm_w, side=_join(_ici_gather(["w_up"], moved[3:], rows=[span("w_up", 2, 16)]),
                                            _d2d_gather(small3, moved[:3])))
    Wf.update(zip(small3, moved))
    bias_win = _bias_window(rel_bias[0])
    yb, Wf["w_down"], Wf["w_up"] = _at_fwd(
        z, bias_win, side=_join(_ici_gather(["w_down"], [Wf["w_down"]], rows=[span("w_down", 0, 10)]),
                                _d2d_gather(["w_up"], [Wf["w_up"]])))
    pa = _mm("branch_a", ya, Wf["w_branch_a"], "nn", [F32])
    pb = _mm("branch_b", yb, Wf["w_branch_b"], "nn", [F32])
    merged = _merge(z, pa, pb)
    add = lambda acc, res: (acc + res,)
    h1 = _mm("out_proj", merged, Wf["w_out"], "nn", [F32], extras=[x2], epilogue=add)
    u2 = _rms_fwd("norm_mlp", h1, norm_mlp_w)
    relu2 = lambda acc: (acc, jnp.square(jnp.maximum(acc, 0.0)))
    a_pre, act, Wf["w_down"] = _mm("mlp_up", u2, Wf["w_up"], "nn", [F32, BF16], epilogue=relu2,
                                   side=_ici_gather(["w_down"], [Wf["w_down"]], rows=[span("w_down", 10, 16)]))
    (Wf["w_down"],) = _run_side("pass_w_down", _d2d_gather(["w_down"], [Wf["w_down"]]))
    h2 = _mm("mlp_down", act, Wf["w_down"], "nn", [F32], extras=[h1], epilogue=add)
    loss_part, dh2, dh2b, d_nf = _loss_head(h2, tgt, nfw)

    drelu2 = lambda acc, a: (acc * (2.0 * jnp.maximum(a, 0.0)),)
    da = _mm("d_act", dh2b, Wf["w_down"], "nt", [BF16], extras=[a_pre], epilogue=drelu2)
    G = {}
    G["w_down"] = _mm("g_w_down", act, dh2b, "tn", [BF16])
    G["w_up"] = _mm("g_w_up", u2, da, "tn", [BF16])
    T_, S_, GOT = {}, {}, {}
    du2, T_["w_down"], T_["w_up"] = _mm("d_u2", da, Wf["w_up"], "nt", [F32],
                                        side=_sib_send(["w_down", "w_up"], [G["w_down"], G["w_up"]]))
    for n in ("w_down", "w_up"):
        S_[n] = _chip_sum(n, G[n], T_[n], place)
    dh1, dh1b, d_nmlp = _rms_bwd("norm_mlp_bwd", du2, h1, norm_mlp_w, dh2)
    dmerged = _mm("d_merged", dh1b, Wf["w_out"], "nt", [F32])
    G["w_out"] = _mm("g_w_out", merged, dh1b, "tn", [BF16])
    dpa, dpb, dz_ga, dz_gb = _dmerge(dmerged, z, pa, pb)
    dya = _mm("d_ya", dpa, Wf["w_branch_a"], "nt", [F32])
    dyb = _mm("d_yb", dpb, Wf["w_branch_b"], "nt", [F32])
    G["w_branch_a"] = _mm("g_w_a", ya, dpa, "tn", [BF16])
    G["w_branch_b"] = _mm("g_w_b", yb, dpb, "tn", [BF16])
    dz_q, dz_f, dz_i, dz_g, d_lbl, d_hgw, GOT["w_down"], *sent = _hg_bwd(
        z, o_hg, dya, states, lb_logits, hg_norm_w,
        side=_join(_chip_exchange([S_["w_down"]]), _sib_send(small3, [G[n] for n in small3])))
    for n, t in zip(small3, sent):
        S_[n] = _chip_sum(n, G[n], t, place)
    dz_aq, dz_ak, dz_av, dbias_win, GOT["w_up"] = _at_bwd(z, dyb, bias_win, side=_chip_exchange([S_["w_up"]]))
    dz = jnp.concatenate([dz_q, dz_f, dz_i, dz_g, dz_aq, dz_ak, dz_av, dz_ga, dz_gb], axis=1)
    G["w_in"], *got3 = _mm("g_w_in", u1, dz, "tn", [BF16], side=_chip_exchange([S_[n] for n in small3]))
    GOT.update(zip(small3, got3))
    (T_["w_in"],) = _run_side("send_w_in_to_sibling", _sib_send(["w_in"], [G["w_in"]]))
    S_["w_in"] = _chip_sum("w_in", G["w_in"], T_["w_in"], place)
    early = [n for n in WEIGHTS if n != "w_in"]
    H_ = {n: _piece_sum(n, S_[n], GOT[n], place) for n in early}
    flight = _exchange_start("w_in", S_["w_in"])
    share_early = _sib_share([H_[n] for n in early])
    share_early.reads.append(flight[-1])
    du1, *shared = _mm("d_u1", dz, Wf["w_in"], "nt", [F32], side=share_early)
    O_ = dict(zip(early, shared))
    grad_x, _, d_nmix = _rms_bwd("norm_mix_bwd", du1, x2, norm_mix_w, dh1)
    d_rel = _bias_window_grad(dbias_win)
    big_out = {}
    for name in early:
        outs = _adam_quarter(name, *big[name], H_[name], O_[name], place)
        big_out[name] = tuple(a[None] for a in outs)
    S_["w_in"], got_in = _exchange_wait("w_in", flight, outs[1])
    H_["w_in"] = _piece_sum("w_in", S_["w_in"], got_in, place)
    (O_["w_in"],) = _run_side("share_w_in", _sib_share([H_["w_in"]]))
    outs = _adam_quarter("w_in", *big["w_in"], H_["w_in"], O_["w_in"], place)
    big_out["w_in"] = tuple(a[None] for a in outs)

    smalls = [("lb_logits", lb_logits, m_lb_logits, v_lb_logits, d_lbl),
              ("hg_norm_w", hg_norm_w, m_hg_norm_w, v_hg_norm_w, d_hgw),
              ("rel_bias", rel_bias, m_rel_bias, v_rel_bias, d_rel),
              ("norm_mix_w", norm_mix_w, m_norm_mix_w, v_norm_mix_w, d_nmix),
              ("norm_mlp_w", norm_mlp_w, m_norm_mlp_w, v_norm_mlp_w, d_nmlp),
              ("norm_final_w", norm_final_w, m_norm_final_w, v_norm_final_w, d_nf)]
    like = [s[1] for s in smalls]
    packed = _small_allreduce_adam(_pack([s[4] for s in smalls]), _pack(like), _pack([s[2] for s in smalls]),
                                   _pack([s[3] for s in smalls]))
    small_out = {s[0]: vals for s, vals in zip(smalls, zip(*[_unpack(p, like) for p in packed]))}

    loss = lax.psum(loss_part[0, 0], ("x", "y", "c"))
    order = ["w_in", "lb_logits", "hg_norm_w", "rel_bias", "w_branch_a", "w_branch_b", "w_out", "norm_mix_w",
             "norm_mlp_w", "w_up", "w_down", "norm_final_w"]
    res = {**big_out, **small_out}
    return (loss, grad_x.reshape(x.shape), *[res[n][0] for n in order], *[res[n][1] for n in order],
            *[res[n][2] for n in order], *[res[n][3] for n in order])
```

```python
import functools

import jax
import jax.numpy as jnp
from jax import lax
from jax.experimental import pallas as pl
from jax.experimental.pallas import tpu as pltpu

F32 = jnp.float32
BF16 = jnp.bfloat16
HIGHEST = lax.Precision.HIGHEST

D_MODEL = 2048
SEQ = 2048
CHUNK = 64
HG_HEADS = 8
HG_D = 128
AT_HEADS = 16
AT_DH = 64
LEFT = 8
REL_CLIP = 256
D_FF = 8192
EPS = 1e-6
ADAM_LR = 0.001
ADAM_B1 = 0.9
ADAM_B2 = 0.999
ADAM_EPS = 1e-08
ADAM_WD = 0.01
ADAM_STEP = 10

LANE = 128
NEG = -1e30
EXP_CLAMP = 80.0
VMEM_LIMIT = 48 * 1024 * 1024
MM_TM, MM_TN, MM_TK = 1024, 1024, 2816
ROW_TILE = 256
QB = 2 * CHUNK


def _hgw():
    return HG_HEADS * HG_D


def _atw():
    return AT_HEADS * AT_DH


def _cparams(sem):
    return pltpu.CompilerParams(dimension_semantics=sem, vmem_limit_bytes=VMEM_LIMIT)


def _sigmoid(x):
    return jax.nn.sigmoid(x)


def _dot(a, b, dims, precision=None):
    return lax.dot_general(a, b, (dims, ((), ())), preferred_element_type=F32, precision=precision)


def _nn(a, b, precision=None):
    return _dot(a, b, ((1,), (0,)), precision)


def _nt(a, b, precision=None):
    return _dot(a, b, ((1,), (1,)), precision)


def _tn(a, b, precision=None):
    return _dot(a, b, ((0,), (0,)), precision)


class _Side:
    def __init__(self, build, nsem, reads=(), aliased=(), fresh=()):
        self.build, self.nsem = build, nsem
        self.reads, self.aliased, self.fresh = list(reads), list(aliased), list(fresh)

    def operands(self):
        return self.reads + self.aliased

    def in_specs(self):
        return [ANY] * len(self.operands())

    def out_specs(self):
        return [ANY] * (len(self.aliased) + len(self.fresh))

    def out_shape(self):
        return [jax.ShapeDtypeStruct(a.shape, a.dtype) for a in self.aliased] + self.fresh

    def aliases(self, n_in, n_out):
        return {n_in + len(self.reads) + t: n_out + t for t in range(len(self.aliased))}

    def scratch(self):
        return [pltpu.SemaphoreType.DMA((self.nsem,)), pltpu.SemaphoreType.DMA((self.nsem,))]

    def hooks(self, in_refs, out_refs, sems, first, last):
        nr, na = len(self.reads), len(self.aliased)
        args = (in_refs[:nr], out_refs[:na], out_refs[na:], *sems)

        @pl.when(first)
        def _():
            for cp in self.build(*args):
                cp.start()

        @pl.when(last)
        def _():
            for cp in self.build(*args):
                cp.wait()


def _side_parts(side):
    if side is None:
        return [], [], [], [], lambda n_in, n_out: {}, []
    return side.operands(), side.in_specs(), side.out_specs(), side.out_shape(), side.aliases, side.scratch()


def _call_with_side(body, name, grid, in_specs, out_specs, out_shape, scratch, sem, operands, side, n_prefetch=0,
                    aliases=None):
    s_ops, s_in, s_out, s_shape, s_alias, s_scr = _side_parts(side)
    n_in, n_out = n_prefetch + len(in_specs), len(out_specs)
    n_sin, n_sout = len(s_ops), len(s_out)

    def wrapped(*refs):
        a, b, c = n_in + n_sin, n_in + n_sin + n_out, n_in + n_sin + n_out + n_sout
        ids = [pl.program_id(d) for d in range(len(grid))]
        first = functools.reduce(lambda p, q: p & q, [i == 0 for i in ids])
        last = functools.reduce(lambda p, q: p & q, [i == g - 1 for i, g in zip(ids, grid)])
        side.hooks(refs[n_in:a], refs[b:c], refs[-2:], first, last)
        body(*refs[:n_in], *refs[a:b], *refs[c:-2])

    spec = dict(grid=grid, in_specs=in_specs + s_in, out_specs=out_specs + s_out, scratch_shapes=scratch + s_scr)
    if n_prefetch:
        spec = dict(grid_spec=pltpu.PrefetchScalarGridSpec(num_scalar_prefetch=n_prefetch, **spec))
    return pl.pallas_call(
        body if side is None else wrapped, name=name, out_shape=out_shape + s_shape,
        input_output_aliases={**s_alias(n_in, n_out), **{n_prefetch + i: o for i, o in (aliases or {}).items()}},
        compiler_params=_cparams(sem if side is None else ("arbitrary",) * len(grid)), **spec,
    )(*operands, *s_ops)


def _mm_tk(K):
    if K <= MM_TK:
        return K
    return max(t for t in range(LANE, MM_TK + 1, LANE) if K % t == 0)


def _mm(name, a, b, mode, out_dtypes, extras=(), epilogue=None, side=None):
    if mode == "nn":
        (M, K), (K2, N) = a.shape, b.shape
    elif mode == "nt":
        (M, K), (N, K2) = a.shape, b.shape
    else:
        (K, M), (K2, N) = a.shape, b.shape
    assert K == K2, (name, a.shape, b.shape)
    tm, tn, tk = min(MM_TM, M), min(MM_TN, N), _mm_tk(K)
    assert M % tm == 0 and N % tn == 0 and K % tk == 0, (name, M, N, K)
    ni, nj, nk = M // tm, N // tn, K // tk
    ne, no = len(extras), len(out_dtypes)
    if epilogue is None:
        epilogue = lambda acc: (acc,)
    s_ops, s_in, s_out, s_shape, s_alias, s_scr = _side_parts(side)
    n_in, n_sin, n_sout = 2 + ne, len(s_ops), len(s_out)

    def body(*refs):
        a_ref, b_ref = refs[:2]
        extra_refs = refs[2:n_in]
        out_refs = refs[n_in + n_sin:n_in + n_sin + no]
        rest = refs[n_in + n_sin + no + n_sout:]
        i, j, k = pl.program_id(0), pl.program_id(1), pl.program_id(2)
        if side is not None:
            side.hooks(refs[n_in:n_in + n_sin], refs[n_in + n_sin + no:n_in + n_sin + no + n_sout], rest[-2:],
                       (i == 0) & (j == 0) & (k == 0), (i == ni - 1) & (j == nj - 1) & (k == nk - 1))
        av, bv = a_ref[...].astype(BF16), b_ref[...].astype(BF16)
        prod = _nn(av, bv) if mode == "nn" else _nt(av, bv) if mode == "nt" else _tn(av, bv)

        def finish(acc):
            res = epilogue(acc, *[e[...] for e in extra_refs])
            for o_ref, r in zip(out_refs, res):
                o_ref[...] = r.astype(o_ref.dtype)

        if nk == 1:
            finish(prod)
        else:
            acc_ref = rest[0]

            @pl.when(k == 0)
            def _():
                acc_ref[...] = prod

            @pl.when((k > 0) & (k < nk - 1))
            def _():
                acc_ref[...] += prod

            @pl.when(k == nk - 1)
            def _():
                finish(acc_ref[...] + prod)

    if mode == "nn":
        a_spec = pl.BlockSpec((tm, tk), lambda i, j, k: (i, k))
        b_spec = pl.BlockSpec((tk, tn), lambda i, j, k: (k, j))
    elif mode == "nt":
        a_spec = pl.BlockSpec((tm, tk), lambda i, j, k: (i, k))
        b_spec = pl.BlockSpec((tn, tk), lambda i, j, k: (j, k))
    else:
        a_spec = pl.BlockSpec((tk, tm), lambda i, j, k: (k, i))
        b_spec = pl.BlockSpec((tk, tn), lambda i, j, k: (k, j))
    o_spec = pl.BlockSpec((tm, tn), lambda i, j, k: (i, j))
    sem = ("arbitrary",) * 3 if side is not None else ("parallel", "parallel", "arbitrary")
    outs = pl.pallas_call(
        body, name=name,
        grid=(ni, nj, nk),
        in_specs=[a_spec, b_spec] + [o_spec] * ne + s_in,
        out_specs=[o_spec] * no + s_out,
        out_shape=[jax.ShapeDtypeStruct((M, N), dt) for dt in out_dtypes] + s_shape,
        input_output_aliases=s_alias(n_in, no),
        scratch_shapes=([pltpu.VMEM((tm, tn), F32)] if nk > 1 else []) + s_scr,
        compiler_params=_cparams(sem),
    )(a, b, *extras, *s_ops)
    return outs[0] if len(outs) == 1 else outs


def _row_spec(tr, d):
    return pl.BlockSpec((tr, d), lambda i: (i, 0))


def _vec_spec(d):
    return pl.BlockSpec((1, d), lambda i: (0, 0))


def _rms_fwd(name, x, w):
    T, D = x.shape
    tr = min(ROW_TILE, T)

    def body(x_ref, w_ref, o_ref):
        xf = x_ref[...]
        r = lax.rsqrt(jnp.mean(xf * xf, axis=-1, keepdims=True) + EPS)
        o_ref[...] = (xf * r * w_ref[...]).astype(BF16)

    return pl.pallas_call(
        body, name=name, grid=(T // tr,),
        in_specs=[_row_spec(tr, D), _vec_spec(D)], out_specs=_row_spec(tr, D),
        out_shape=jax.ShapeDtypeStruct((T, D), BF16),
        compiler_params=_cparams(("parallel",)),
    )(x, w)


def _rms_bwd(name, dy, h, w, dres, side=None):
    T, D = h.shape
    tr = min(ROW_TILE, T)

    def body(dy_ref, h_ref, w_ref, dres_ref, dh_ref, dhb_ref, dw_ref):
        @pl.when(pl.program_id(0) == 0)
        def _():
            dw_ref[...] = jnp.zeros_like(dw_ref)

        hf, dyv = h_ref[...], dy_ref[...]
        r = lax.rsqrt(jnp.mean(hf * hf, axis=-1, keepdims=True) + EPS)
        xhat = hf * r
        dw_ref[...] += jnp.sum(dyv * xhat, axis=0, keepdims=True)
        dxh = dyv * w_ref[...]
        dh = dres_ref[...] + r * (dxh - xhat * jnp.mean(dxh * xhat, axis=-1, keepdims=True))
        dh_ref[...] = dh
        dhb_ref[...] = dh.astype(BF16)

    return _call_with_side(
        body, name, (T // tr,),
        [_row_spec(tr, D), _row_spec(tr, D), _vec_spec(D), _row_spec(tr, D)],
        [_row_spec(tr, D), _row_spec(tr, D), _vec_spec(D)],
        [jax.ShapeDtypeStruct((T, D), F32), jax.ShapeDtypeStruct((T, D), BF16), jax.ShapeDtypeStruct((1, D), F32)],
        [], ("arbitrary",), (dy, h, w, dres), side)


def _loss_head(h2, target, w):
    T, D = h2.shape
    tr = min(ROW_TILE, T)

    def body(h_ref, t_ref, w_ref, loss_ref, dh_ref, dhb_ref, dw_ref):
        @pl.when(pl.program_id(0) == 0)
        def _():
            dw_ref[...] = jnp.zeros_like(dw_ref)
            loss_ref[...] = jnp.zeros_like(loss_ref)

        hf, wv = h_ref[...], w_ref[...]
        r = lax.rsqrt(jnp.mean(hf * hf, axis=-1, keepdims=True) + EPS)
        xhat = hf * r
        diff = xhat * wv - t_ref[...]
        loss_ref[...] += 0.5 * jnp.sum(jnp.mean(diff * diff, axis=-1, keepdims=True))
        dyv = diff * (1.0 / D)
        dw_ref[...] += jnp.sum(dyv * xhat, axis=0, keepdims=True)
        dxh = dyv * wv
        dh = r * (dxh - xhat * jnp.mean(dxh * xhat, axis=-1, keepdims=True))
        dh_ref[...] = dh
        dhb_ref[...] = dh.astype(BF16)

    return pl.pallas_call(
        body, name="loss_head", grid=(T // tr,),
        in_specs=[_row_spec(tr, D), _row_spec(tr, D), _vec_spec(D)],
        out_specs=[_vec_spec(LANE), _row_spec(tr, D), _row_spec(tr, D), _vec_spec(D)],
        out_shape=[jax.ShapeDtypeStruct((1, LANE), F32), jax.ShapeDtypeStruct((T, D), F32),
                   jax.ShapeDtypeStruct((T, D), BF16), jax.ShapeDtypeStruct((1, D), F32)],
        compiler_params=_cparams(("arbitrary",)),
    )(h2, target, w)


def _gate_tiles(T, D):
    goff = 4 * _hgw() + 3 * _atw()
    tc = min(1024, D)
    assert goff % tc == 0 and D % tc == 0
    return min(ROW_TILE, T), tc, goff // tc, D // tc


def _merge(z, pa, pb):
    T, D = pa.shape
    tr, tc, g0, nd = _gate_tiles(T, D)

    def body(ga_ref, gb_ref, pa_ref, pb_ref, o_ref):
        o_ref[...] = (_sigmoid(ga_ref[...]) * pa_ref[...] + _sigmoid(gb_ref[...]) * pb_ref[...]).astype(BF16)

    t = pl.BlockSpec((tr, tc), lambda i, j: (i, j))
    return pl.pallas_call(
        body, name="merge", grid=(T // tr, nd),
        in_specs=[pl.BlockSpec((tr, tc), lambda i, j: (i, g0 + j)),
                  pl.BlockSpec((tr, tc), lambda i, j: (i, g0 + nd + j)), t, t],
        out_specs=t, out_shape=jax.ShapeDtypeStruct((T, D), BF16),
        compiler_params=_cparams(("parallel", "parallel")),
    )(z, z, pa, pb)


def _dmerge(dm, z, pa, pb):
    T, D = pa.shape
    tr, tc, g0, nd = _gate_tiles(T, D)

    def body(dm_ref, ga_ref, gb_ref, pa_ref, pb_ref, dpa_ref, dpb_ref, dga_ref, dgb_ref):
        dmv = dm_ref[...]
        sa, sb = _sigmoid(ga_ref[...]), _sigmoid(gb_ref[...])
        dpa_ref[...] = (dmv * sa).astype(BF16)
        dpb_ref[...] = (dmv * sb).astype(BF16)
        dga_ref[...] = (dmv * pa_ref[...] * sa * (1.0 - sa)).astype(BF16)
        dgb_ref[...] = (dmv * pb_ref[...] * sb * (1.0 - sb)).astype(BF16)

    t = pl.BlockSpec((tr, tc), lambda i, j: (i, j))
    return pl.pallas_call(
        body, name="dmerge", grid=(T // tr, nd),
        in_specs=[t, pl.BlockSpec((tr, tc), lambda i, j: (i, g0 + j)),
                  pl.BlockSpec((tr, tc), lambda i, j: (i, g0 + nd + j)), t, t],
        out_specs=[t, t, t, t],
        out_shape=[jax.ShapeDtypeStruct((T, D), BF16)] * 4,
        compiler_params=_cparams(("parallel", "parallel")),
    )(dm, z, z, pa, pb)


def _hg_gates(xq, xf, lb):
    f = _sigmoid(xf)
    g = lb + (1.0 - lb) * f
    sq = _sigmoid(xq)
    return f, g, jnp.log(g), 1.0 - g, sq, xq * sq * (HG_D ** -0.5)


def _hg_decays(lg, tri_incl, rowi):
    b = _nn(tri_incl, lg, precision=HIGHEST)
    b_last = jnp.sum(lg, axis=0, keepdims=True)
    b_mid = jnp.sum(jnp.where(rowi <= CHUNK // 2, lg, 0.0), axis=0, keepdims=True)
    return b, b_last, b_mid


def _hg_in_specs(T):
    H = HG_HEADS
    return [pl.BlockSpec((T, HG_D), lambda h, s=s: (0, s * H + h)) for s in range(4)]


def _hg_fwd(z, lb_logits, hgw, side=None):
    T = z.shape[0]
    H, d, C = HG_HEADS, HG_D, CHUNK
    nc = T // C

    def body(hq_ref, hf_ref, hi_ref, hg_ref, lbl_ref, w_ref, ya_ref, o_ref, s_ref):
        lb = 1.0 / (1.0 + jnp.exp(lbl_ref[1:2, :] - lbl_ref[0:1, :]))
        wv = w_ref[...]
        row = lax.broadcasted_iota(jnp.int32, (C, C), 0)
        col = lax.broadcasted_iota(jnp.int32, (C, C), 1)
        tril = col <= row
        tri_incl = tril.astype(F32)
        rowi = lax.broadcasted_iota(jnp.int32, (C, d), 0)

        def chunk(c, st):
            rows = pl.ds(pl.multiple_of(c * C, C), C)
            xq, xf, v, xg = hq_ref[rows, :], hf_ref[rows, :], hi_ref[rows, :], hg_ref[rows, :]
            _, _, lg, kk, _, q = _hg_gates(xq, xf, lb)
            b, b_last, b_mid = _hg_decays(lg, tri_incl, rowi)
            st_b = st.astype(BF16)
            s_ref[c] = st
            vb = v.astype(BF16)
            o = _nt((q * jnp.exp(b)).astype(BF16), st_b)
            qt = (q * jnp.exp(b - b_mid)).astype(BF16)
            kt = (kk * jnp.exp(jnp.minimum(b_mid - b, EXP_CLAMP))).astype(BF16)
            a = jnp.where(tril, _nt(qt, kt), 0.0).astype(BF16)
            o = o + _nn(a, vb)
            st_new = st * jnp.exp(b_last) + _tn(v, kk * jnp.exp(b_last - b), HIGHEST)
            o_ref[rows, :] = o
            r = lax.rsqrt(jnp.mean(o * o, axis=-1, keepdims=True) + EPS)
            ya_ref[rows, :] = (o * r * wv * (xg * _sigmoid(xg))).astype(BF16)
            return st_new

        lax.fori_loop(0, nc // 2, lambda i, st: chunk(2 * i + 1, chunk(2 * i, st)), jnp.zeros((d, d), F32))

    head = pl.BlockSpec((T, d), lambda h: (0, h))
    return _call_with_side(
        body, "hg_fwd", (H,),
        _hg_in_specs(T) + [pl.BlockSpec((2, d), lambda h: (0, h)), pl.BlockSpec((1, d), lambda h: (0, 0))],
        [head, head, pl.BlockSpec((None, nc, d, d), lambda h: (h, 0, 0, 0))],
        [jax.ShapeDtypeStruct((T, H * d), BF16), jax.ShapeDtypeStruct((T, H * d), F32),
         jax.ShapeDtypeStruct((H, nc, d, d), F32)],
        [], ("parallel",), (z, z, z, z, lb_logits, hgw), side)


def _hg_bwd(z, o, dya, states, lb_logits, hgw, side=None):
    T = z.shape[0]
    H, d, C = HG_HEADS, HG_D, CHUNK
    nc = T // C
    scale = HG_D ** -0.5

    def body(hq_ref, hf_ref, hi_ref, hg_ref, o_ref, dy_ref, s_ref, lbl_ref, w_ref,
             dq_ref, df_ref, di_ref, dg_ref, dlbl_ref, dw_ref, acc_ref):
        lb = 1.0 / (1.0 + jnp.exp(lbl_ref[1:2, :] - lbl_ref[0:1, :]))
        wv = w_ref[...]
        row = lax.broadcasted_iota(jnp.int32, (C, C), 0)
        col = lax.broadcasted_iota(jnp.int32, (C, C), 1)
        tril = col <= row
        tri_incl = tril.astype(F32)
        triu_incl = (col >= row).astype(F32)
        rowi = lax.broadcasted_iota(jnp.int32, (C, d), 0)
        acc_ref[...] = jnp.zeros_like(acc_ref)

        @pl.when(pl.program_id(0) == 0)
        def _():
            dw_ref[...] = jnp.zeros_like(dw_ref)

        def chunk(i, carry):
            dst, tail = carry
            c = nc - 1 - i
            rows = pl.ds(pl.multiple_of(c * C, C), C)
            xq, xf, v, xg = hq_ref[rows, :], hf_ref[rows, :], hi_ref[rows, :], hg_ref[rows, :]
            f, g, lg, kk, sq, q = _hg_gates(xq, xf, lb)
            b, b_last, b_mid = _hg_decays(lg, tri_incl, rowi)
            e_b, e_qm, e_km = jnp.exp(b), jnp.exp(b - b_mid), jnp.exp(jnp.minimum(b_mid - b, EXP_CLAMP))
            e_kl, e_last = jnp.exp(b_last - b), jnp.exp(b_last)
            ov, dy = o_ref[rows, :], dy_ref[rows, :]
            r = lax.rsqrt(jnp.mean(ov * ov, axis=-1, keepdims=True) + EPS)
            xhat = ov * r
            sg = _sigmoid(xg)
            dxg = dy * xhat * wv * (sg * (1.0 + xg * (1.0 - sg)))
            dyn = dy * (xg * sg)
            acc_ref[0:1, :] += jnp.sum(dyn * xhat, axis=0, keepdims=True)
            dxh = dyn * wv
            dof = r * (dxh - xhat * jnp.mean(dxh * xhat, axis=-1, keepdims=True))
            do, vb = dof.astype(BF16), v.astype(BF16)
            qt, kt = q * e_qm, kk * e_km
            pm = jnp.where(tril, _nt(do, vb), 0.0)
            am = jnp.where(tril, _nt(qt.astype(BF16), kt.astype(BF16)), 0.0).astype(BF16)
            dq = _nn(dof, s_ref[c], HIGHEST) * e_b + _nn(pm, kt, HIGHEST) * e_qm
            dk = _tn(pm, qt, HIGHEST) * e_km + _nn(v, dst, HIGHEST) * e_kl
            dv = _tn(am, do) + _nt((kk * e_kl).astype(BF16), dst.astype(BF16))
            dst_prev = dst * e_last + _tn(dof, q * e_b, HIGHEST)
            db = q * dq - kk * dk
            dlg = _nn(triu_incl, db, precision=HIGHEST) + tail
            dgate = dlg / g - dk
            acc_ref[1:2, :] += jnp.sum(dgate * (1.0 - f), axis=0, keepdims=True)
            dq_ref[rows, :] = (dq * scale * (sq * (1.0 + xq * (1.0 - sq)))).astype(BF16)
            df_ref[rows, :] = (dgate * (1.0 - lb) * f * (1.0 - f)).astype(BF16)
            di_ref[rows, :] = dv.astype(BF16)
            dg_ref[rows, :] = dxg.astype(BF16)
            return dst_prev, tail + jnp.sum(db, axis=0, keepdims=True)

        lax.fori_loop(0, nc // 2, lambda i, carry: chunk(2 * i + 1, chunk(2 * i, carry)),
                      (jnp.zeros((d, d), F32), jnp.zeros((1, d), F32)))
        dw_ref[...] += acc_ref[0:1, :]
        dl0 = acc_ref[1:2, :] * lb * (1.0 - lb)
        dlbl_ref[0:1, :] = dl0
        dlbl_ref[1:2, :] = -dl0

    head = pl.BlockSpec((T, d), lambda h: (0, h))
    return _call_with_side(
        body, "hg_bwd", (H,),
        _hg_in_specs(T) + [head, head, pl.BlockSpec((None, nc, d, d), lambda h: (h, 0, 0, 0)),
                           pl.BlockSpec((2, d), lambda h: (0, h)), pl.BlockSpec((1, d), lambda h: (0, 0))],
        [head, head, head, head, pl.BlockSpec((2, d), lambda h: (0, h)), pl.BlockSpec((1, d), lambda h: (0, 0))],
        [jax.ShapeDtypeStruct((T, H * d), BF16)] * 4 + [jax.ShapeDtypeStruct((2, H * d), F32),
                                                        jax.ShapeDtypeStruct((1, d), F32)],
        [pltpu.VMEM((8, d), F32)], ("arbitrary",), (z, z, z, z, o, dya, states, lb_logits, hgw), side)


def _at_dims():
    pad = LEFT * CHUNK
    return pad, QB + pad, AT_HEADS * AT_DH // LANE, 4 * _hgw() // LANE


def _rel_of_period():
    pad, W, _, _ = _at_dims()
    n = jnp.arange(QB + W)
    return jnp.clip(pad - jnp.where(n < W, n, n - (QB + W)), -REL_CLIP, REL_CLIP) + REL_CLIP


def _bias_window(rel_bias):
    pad, W, _, _ = _at_dims()
    H, P = rel_bias.shape[0], QB + W
    per = rel_bias[:, _rel_of_period()]
    win = jnp.tile(per, (1, QB))[:, :QB * (P - 1)].reshape(H, QB, P - 1)[:, :, :W]
    t = jnp.arange(QB)[:, None]
    j = jnp.arange(W)[None, :]
    ok = (j // CHUNK >= t // CHUNK) & (j // CHUNK <= t // CHUNK + LEFT)
    return jnp.where(ok[None], win, NEG)


def _bias_window_grad(dbw):
    pad, W, _, _ = _at_dims()
    H, P = dbw.shape[0], QB + W
    flat = jnp.pad(dbw, ((0, 0), (0, 0), (0, P - 1 - W))).reshape(H, QB * (P - 1))
    per = jnp.pad(flat, ((0, 0), (0, QB))).reshape(H, QB, P).sum(axis=1)
    onehot = _rel_of_period()[:, None] == jnp.arange(2 * REL_CLIP + 1)[None, :]
    return jnp.dot(per, onehot.astype(F32), precision=HIGHEST)


def _at_softmax(q_half, kw, bias, valid):
    s = _nt(q_half, kw) * (AT_DH ** -0.5) + bias
    s = jnp.where(valid, s, NEG)
    e = jnp.exp(s - jnp.max(s, axis=-1, keepdims=True))
    return e / jnp.sum(e, axis=-1, keepdims=True)


def _at_fwd(z, bias_win, side=None):
    T = z.shape[0]
    pad, W, HP, c0 = _at_dims()
    nq = T // QB

    def body(q_ref, k_ref, v_ref, bias_ref, o_ref, kpad, vpad):
        qi = pl.program_id(1)

        @pl.when(qi == 0)
        def _():
            kpad[0:pad, :] = jnp.zeros((pad, LANE), BF16)
            vpad[0:pad, :] = jnp.zeros((pad, LANE), BF16)
            kpad[pad:, :] = k_ref[...].astype(BF16)
            vpad[pad:, :] = v_ref[...].astype(BF16)

        win = pl.ds(pl.multiple_of(qi * QB, QB), W)
        kw, vw = kpad[win, :], vpad[win, :]
        q = q_ref[...]
        lane = lax.broadcasted_iota(jnp.int32, (QB, LANE), 1)
        first = lane < AT_DH
        valid = lax.broadcasted_iota(jnp.int32, (QB, W), 1) + qi * QB >= pad
        pa = _at_softmax(jnp.where(first, q, 0.0).astype(BF16), kw, bias_ref[0], valid)
        pb = _at_softmax(jnp.where(first, 0.0, q).astype(BF16), kw, bias_ref[1], valid)
        o_ref[...] = jnp.where(first, _nn(pa.astype(BF16), vw), _nn(pb.astype(BF16), vw)).astype(BF16)

    full = lambda s: pl.BlockSpec((T, LANE), lambda hp, qi, s=s: (0, c0 + s * HP + hp))
    return _call_with_side(
        body, "at_fwd", (HP, nq),
        [pl.BlockSpec((QB, LANE), lambda hp, qi: (qi, c0 + hp)), full(1), full(2),
         pl.BlockSpec((2, QB, W), lambda hp, qi: (hp, 0, 0))],
        [pl.BlockSpec((QB, LANE), lambda hp, qi: (qi, hp))],
        [jax.ShapeDtypeStruct((T, HP * LANE), BF16)],
        [pltpu.VMEM((T + pad, LANE), BF16)] * 2, ("parallel", "arbitrary"), (z, z, z, bias_win), side)


def _at_bwd(z, dyb, bias_win, side=None):
    T = z.shape[0]
    pad, W, HP, c0 = _at_dims()
    nq = T // QB
    scale = AT_DH ** -0.5

    def body(q_ref, k_ref, v_ref, do_ref, bias_ref, dq_ref, dk_ref, dv_ref, dbias_ref, kpad, vpad, dkpad, dvpad):
        qi = pl.program_id(1)

        @pl.when(qi == 0)
        def _():
            kpad[0:pad, :] = jnp.zeros((pad, LANE), BF16)
            vpad[0:pad, :] = jnp.zeros((pad, LANE), BF16)
            kpad[pad:, :] = k_ref[...].astype(BF16)
            vpad[pad:, :] = v_ref[...].astype(BF16)
            dkpad[...] = jnp.zeros_like(dkpad)
            dvpad[...] = jnp.zeros_like(dvpad)
            dbias_ref[...] = jnp.zeros_like(dbias_ref)

        win = pl.ds(pl.multiple_of(qi * QB, QB), W)
        kw, vw = kpad[win, :], vpad[win, :]
        q, do = q_ref[...], do_ref[...]
        lane = lax.broadcasted_iota(jnp.int32, (QB, LANE), 1)
        first = lane < AT_DH
        valid = lax.broadcasted_iota(jnp.int32, (QB, W), 1) + qi * QB >= pad

        def half(hh, qh, doh):
            p = _at_softmax(qh, kw, bias_ref[hh], valid)
            dp = _nt(doh, vw)
            ds = p * (dp - jnp.sum(p * dp, axis=-1, keepdims=True))
            dbias_ref[hh] += ds
            dss = (ds * scale).astype(BF16)
            return _nn(dss, kw), _tn(dss, qh), _tn(p.astype(BF16), doh)

        dqa, dka, dva = half(0, jnp.where(first, q, 0.0).astype(BF16), jnp.where(first, do, 0.0).astype(BF16))
        dqb, dkb, dvb = half(1, jnp.where(first, 0.0, q).astype(BF16), jnp.where(first, 0.0, do).astype(BF16))
        dq_ref[...] = jnp.where(first, dqa, dqb).astype(BF16)
        dkpad[win, :] += dka + dkb
        dvpad[win, :] += dva + dvb

        @pl.when(qi == nq - 1)
        def _():
            dk_ref[...] = dkpad[pad:, :].astype(BF16)
            dv_ref[...] = dvpad[pad:, :].astype(BF16)

    full = lambda s: pl.BlockSpec((T, LANE), lambda hp, qi, s=s: (0, c0 + s * HP + hp))
    blk = pl.BlockSpec((QB, LANE), lambda hp, qi: (qi, hp))
    col = pl.BlockSpec((T, LANE), lambda hp, qi: (0, hp))
    bw = pl.BlockSpec((2, QB, W), lambda hp, qi: (hp, 0, 0))
    return _call_with_side(
        body, "at_bwd", (HP, nq),
        [pl.BlockSpec((QB, LANE), lambda hp, qi: (qi, c0 + hp)), full(1), full(2), blk, bw],
        [blk, col, col, bw],
        [jax.ShapeDtypeStruct((T, HP * LANE), BF16)] * 3 + [jax.ShapeDtypeStruct(bias_win.shape, F32)],
        [pltpu.VMEM((T + pad, LANE), BF16)] * 2 + [pltpu.VMEM((T + pad, LANE), F32)] * 2,
        ("parallel", "arbitrary"), (z, z, z, dyb, bias_win), side)


def _piece_tiles(name, full_shape):
    pr, pc = _piece_shape(name, full_shape)
    tr = min(ROW_TILE, pr)
    assert pr % tr == 0
    nt = pr // tr
    if name in ROW_SHARDED:
        return tr, nt, lambda q, half, i: ((2 * q + half) * nt + i, 0)
    return tr, nt, lambda q, half, i: (half * nt + i, q)


def _cast_into_full(name, wq, place):
    full = _full_shape(name, wq.shape)
    pc = wq.shape[1]
    tr, nt, at = _piece_tiles(name, full)

    def body(place_ref, w_ref, o_ref):
        o_ref[...] = w_ref[...].astype(BF16)

    return pl.pallas_call(
        body, name="cast_" + name,
        grid_spec=pltpu.PrefetchScalarGridSpec(
            num_scalar_prefetch=1, grid=(2, nt),
            in_specs=[pl.BlockSpec((tr, pc), lambda h, i, s: (h * nt + i, 0))],
            out_specs=pl.BlockSpec((tr, pc), lambda h, i, s: at(s[0], h, i))),
        out_shape=jax.ShapeDtypeStruct(full, BF16),
        compiler_params=_cparams(("parallel", "parallel")),
    )(place, wq)


def _chip_sum(name, grad, theirs, place):
    pr, pc = theirs.shape[1:]
    tr, nt, at = _piece_tiles(name, grad.shape)

    def body(place_ref, g_ref, t_ref, o_ref):
        o_ref[...] = (g_ref[...].astype(F32) + t_ref[...].astype(F32)).astype(BF16)

    piece = pl.BlockSpec((None, tr, pc), lambda q, i, s: (q, i, 0))
    return pl.pallas_call(
        body, name="chip_sum_" + name,
        grid_spec=pltpu.PrefetchScalarGridSpec(
            num_scalar_prefetch=1, grid=(4, nt),
            in_specs=[pl.BlockSpec((tr, pc), lambda q, i, s: at(q, s[1], i)), piece], out_specs=piece),
        out_shape=jax.ShapeDtypeStruct(theirs.shape, BF16),
        compiler_params=_cparams(("parallel", "parallel")),
    )(place, grad, theirs)


def _piece_sum(name, chip_sums, got, place):
    pr, pc = chip_sums.shape[1:]
    tr = min(ROW_TILE, pr)

    def body(place_ref, own_ref, got_ref, o_ref):
        o_ref[...] = (own_ref[...].astype(F32) + got_ref[0].astype(F32) + got_ref[1].astype(F32)
                      + got_ref[2].astype(F32))

    return pl.pallas_call(
        body, name="piece_sum_" + name,
        grid_spec=pltpu.PrefetchScalarGridSpec(
            num_scalar_prefetch=1, grid=(pr // tr,),
            in_specs=[pl.BlockSpec((None, tr, pc), lambda i, s: (s[0], i, 0)),
                      pl.BlockSpec((3, tr, pc), lambda i, s: (0, i, 0))],
            out_specs=pl.BlockSpec((tr, pc), lambda i, s: (i, 0))),
        out_shape=jax.ShapeDtypeStruct((pr, pc), F32),
        compiler_params=_cparams(("parallel",)),
    )(place, chip_sums, got)


def _adam_quarter(name, w, m, v, g_mine, g_sib, place, side=None):
    pr, pc = g_mine.shape
    tr = min(ROW_TILE // 2, pr)
    nt = pr // tr

    def body(place_ref, w_ref, m_ref, v_ref, gm_ref, gs_ref, go_ref, d_ref, mo_ref, vo_ref):
        g = jnp.where(pl.program_id(0) == place_ref[1], gm_ref[...], gs_ref[...])
        delta, mn, vn = _adam_math(w_ref[...], g, m_ref[...], v_ref[...])
        go_ref[...] = g
        d_ref[...] = delta
        mo_ref[...] = mn
        vo_ref[...] = vn

    quarter = pl.BlockSpec((tr, pc), lambda h, i, s: (h * nt + i, 0))
    mine = pl.BlockSpec((tr, pc), lambda h, i, s: (jnp.where(h == s[1], i, 0), 0))
    sib = pl.BlockSpec((tr, pc), lambda h, i, s: (jnp.where(h == s[1], 0, i), 0))
    return _call_with_side(
        body, "adam_" + name, (2, nt), [quarter, quarter, quarter, mine, sib], [quarter] * 4,
        [jax.ShapeDtypeStruct(w.shape, F32)] * 4, [], ("parallel", "parallel"),
        (place, w, m, v, g_mine, g_sib), side, n_prefetch=1)


def _adam_math(w, g, m, v):
    m = ADAM_B1 * m + (1.0 - ADAM_B1) * g
    v = ADAM_B2 * v + (1.0 - ADAM_B2) * (g * g)
    m_hat = m / (1.0 - ADAM_B1 ** ADAM_STEP)
    v_hat = v / (1.0 - ADAM_B2 ** ADAM_STEP)
    return -ADAM_LR * (m_hat / (jnp.sqrt(v_hat) + ADAM_EPS) + ADAM_WD * w), m, v


WEIGHTS = ("w_in", "w_branch_a", "w_branch_b", "w_out", "w_up", "w_down")
ROW_SHARDED = ("w_out", "w_down")
ANY = pl.BlockSpec(memory_space=pl.ANY)
MESH = pl.DeviceIdType.MESH


def _place():
    x, y, c = lax.axis_index("x"), lax.axis_index("y"), lax.axis_index("c")
    chips = [(1 - x, y), (x, 1 - y), (1 - x, 1 - y)]
    return x, y, c, 2 * x + y, chips, [2 * cx + cy for cx, cy in chips]


def _piece(full_ref, name, q, half):
    K, N = full_ref.shape
    if name in ROW_SHARDED:
        rows = K // 8
        return full_ref.at[pl.ds(q * (2 * rows) + half * rows, rows), :]
    return full_ref.at[pl.ds(half * (K // 2), K // 2), pl.ds(q * (N // 4), N // 4)]


def _piece_shape(name, full_shape):
    K, N = full_shape
    return (K // 8, N) if name in ROW_SHARDED else (K // 2, N // 4)


def _full_shape(name, quarter_shape):
    Kq, Nq = quarter_shape
    return (4 * Kq, Nq) if name in ROW_SHARDED else (Kq, 4 * Nq)


def _remote(src, dst, send_sem, recv_sem, device):
    return pltpu.make_async_remote_copy(src_ref=src, dst_ref=dst, send_sem=send_sem, recv_sem=recv_sem,
                                        device_id=device, device_id_type=MESH)


def _z_part(u1, w_in, z_prev, place, k0, count, side=None):
    T, K = u1.shape
    N = w_in.shape[1]
    nq = N // 4
    tn = nq // 2 if (nq // 2) % LANE == 0 else nq
    tm = min(MM_TM, T)
    per = nq // tn
    col = lambda g, j, s: (s[0] ^ (k0 + g)) * per + j
    ins = [pl.BlockSpec((tm, K), lambda g, i, j, s: (i, 0)), pl.BlockSpec((K, tn), lambda g, i, j, s: (0, col(g, j, s)))]
    operands = [place, u1, w_in]
    if z_prev is not None:
        ins.append(ANY)
        operands.append(z_prev)

    def body(place_ref, a_ref, b_ref, *rest):
        rest[-1][...] = _nn(a_ref[...], b_ref[...])

    return _call_with_side(
        body, "z_part_%d" % k0, (count, T // tm, per), ins,
        [pl.BlockSpec((tm, tn), lambda g, i, j, s: (i, col(g, j, s)))], [jax.ShapeDtypeStruct((T, N), F32)],
        [], ("parallel",) * 3, tuple(operands), side, n_prefetch=1, aliases={} if z_prev is None else {2: 0})


def _rows(ref, span):
    return ref if span is None else ref.at[pl.ds(span[0], span[1]), :]


def _ici_near(names, fulls, rows=None):
    rows = rows or [None] * len(names)

    def build(reads, aliased, fresh, send_sems, recv_sems, off=0):
        _, _, c, p, chips, _ = _place()
        out = []
        for i, (ref, name) in enumerate(zip(aliased, names)):
            mine = _rows(_piece(ref, name, p, c), rows[i])
            for j, chip in enumerate(chips[:2]):
                k = off + 2 * i + j
                out.append(_remote(mine, mine, send_sems.at[k], recv_sems.at[k], (*chip, c)))
        return out

    return _Side(build, 2 * len(names), aliased=fulls)


def _ici_far(names, fulls, rows=None):
    rows = rows or [None] * len(names)

    def build(reads, aliased, fresh, send_sems, recv_sems, off=0):
        x, y, c, _, _, chip_ids = _place()
        south = c == 0
        src_chip = jnp.where(south, chip_ids[0], chip_ids[1])
        target = (jnp.where(south, x, 1 - x), jnp.where(south, 1 - y, y), c)
        out = []
        for i, (ref, name) in enumerate(zip(aliased, names)):
            landed = _rows(_piece(ref, name, src_chip, c), rows[i])
            out.append(_remote(landed, landed, send_sems.at[off + i], recv_sems.at[off + i], target))
        return out

    return _Side(build, len(names), aliased=fulls)


def _d2d_gather(names, fulls, which=(0, 1, 2)):
    def build(reads, aliased, fresh, send_sems, recv_sems, off=0):
        x, y, c, _, _, chip_ids = _place()
        out = []
        for i, (ref, name) in enumerate(zip(aliased, names)):
            for n, j in enumerate(which):
                landed, k = _piece(ref, name, chip_ids[j], c), off + len(which) * i + n
                out.append(_remote(landed, landed, send_sems.at[k], recv_sems.at[k], (x, y, 1 - c)))
        return out

    return _Side(build, len(which) * len(names), aliased=fulls)


def _sib_send(names, grads):
    def build(reads, aliased, fresh, send_sems, recv_sems, off=0):
        x, y, c, _, _, _ = _place()
        out = []
        for i, name in enumerate(names):
            for q in range(4):
                k = off + 4 * i + q
                out.append(_remote(_piece(reads[i], name, q, 1 - c), fresh[i].at[q], send_sems.at[k], recv_sems.at[k],
                                   (x, y, 1 - c)))
        return out

    shapes = [jax.ShapeDtypeStruct((4,) + _piece_shape(name, g.shape), BF16) for name, g in zip(names, grads)]
    return _Side(build, 4 * len(names), reads=grads, fresh=shapes)


def _chip_exchange(chip_sums, rows=None, got=None):
    rows = rows or [None] * len(chip_sums)

    def build(reads, aliased, fresh, send_sems, recv_sems, off=0):
        _, _, c, _, chips, chip_ids = _place()
        out = []
        for i in range(len(chip_sums)):
            for j, (chip, cid) in enumerate(zip(chips, chip_ids)):
                k = off + 3 * i + j
                out.append(_remote(_rows(reads[i].at[cid], rows[i]), _rows((aliased or fresh)[i].at[j], rows[i]),
                                   send_sems.at[k], recv_sems.at[k], (*chip, c)))
        return out

    if got is not None:
        return _Side(build, 3 * len(chip_sums), reads=chip_sums, aliased=got)
    shapes = [jax.ShapeDtypeStruct((3,) + s.shape[1:], BF16) for s in chip_sums]
    return _Side(build, 3 * len(chip_sums), reads=chip_sums, fresh=shapes)


HBM = pl.BlockSpec(memory_space=pltpu.HBM)
SEM = pl.BlockSpec(memory_space=pltpu.SEMAPHORE)


def _exchange_copies(s_ref, land_ref, send_sems, recv_sems):
    _, _, c, _, chips, chip_ids = _place()
    return [_remote(s_ref.at[cid], land_ref.at[j], send_sems.at[j], recv_sems.at[j], (*chip, c))
            for j, (chip, cid) in enumerate(zip(chips, chip_ids))]


def _exchange_start(name, chip_sum):
    def body(s_ref, land_ref, send_sems, recv_sems, s_thru, land_thru, token):
        for cp in _exchange_copies(s_ref, land_ref, send_sems, recv_sems):
            cp.start()
        token[...] = jnp.zeros_like(token)

    land = jax.ShapeDtypeStruct((3,) + chip_sum.shape[1:], chip_sum.dtype)
    return pl.pallas_call(
        body, name="exchange_start_" + name,
        out_shape=(pltpu.SemaphoreType.DMA((3,)), pltpu.SemaphoreType.DMA((3,)),
                   pltpu.HBM(chip_sum.shape, chip_sum.dtype), pltpu.HBM(land.shape, land.dtype),
                   jax.ShapeDtypeStruct((8, LANE), F32)),
        in_specs=(HBM, HBM), out_specs=(SEM, SEM, HBM, HBM, pl.BlockSpec(memory_space=pltpu.VMEM)),
        input_output_aliases={0: 2, 1: 3},
        compiler_params=pltpu.CompilerParams(has_side_effects=pltpu.SideEffectType.DATAFLOW_SIDE_EFFECTING),
    )(pltpu.with_memory_space_constraint(chip_sum, pltpu.HBM),
      pltpu.with_memory_space_constraint(lax.empty(land.shape, land.dtype), pltpu.HBM))


def _exchange_wait(name, flight, after):
    send_sems, recv_sems, s_thru, land_thru, _ = flight

    def body(s_ref, land_ref, send_sems, recv_sems, after_ref, s_out, land_out):
        for cp in _exchange_copies(s_ref, land_ref, send_sems, recv_sems):
            cp.wait_send()
            cp.wait_recv()

    return pl.pallas_call(
        body, name="exchange_wait_" + name,
        out_shape=(pltpu.HBM(s_thru.shape, s_thru.dtype), pltpu.HBM(land_thru.shape, land_thru.dtype)),
        in_specs=(HBM, HBM, SEM, SEM, ANY), out_specs=(HBM, HBM), input_output_aliases={0: 0, 1: 1},
        compiler_params=pltpu.CompilerParams(has_side_effects=pltpu.SideEffectType.DATAFLOW_SIDE_EFFECTING),
    )(s_thru, land_thru, send_sems, recv_sems, after)


def _sib_share(halves):
    def build(reads, aliased, fresh, send_sems, recv_sems, off=0):
        x, y, c, _, _, _ = _place()
        return [_remote(reads[i], fresh[i], send_sems.at[off + i], recv_sems.at[off + i], (x, y, 1 - c))
                for i in range(len(halves))]

    return _Side(build, len(halves), reads=halves, fresh=[jax.ShapeDtypeStruct(h.shape, F32) for h in halves])


def _join(a, b):
    def build(reads, aliased, fresh, send_sems, recv_sems, off=0):
        ra, aa, fa = len(a.reads), len(a.aliased), len(a.fresh)
        return (a.build(reads[:ra], aliased[:aa], fresh[:fa], send_sems, recv_sems, off)
                + b.build(reads[ra:], aliased[aa:], fresh[fa:], send_sems, recv_sems, off + a.nsem))

    return _Side(build, a.nsem + b.nsem, a.reads + b.reads, a.aliased + b.aliased, a.fresh + b.fresh)


def _run_side(name, side):
    nr, na = len(side.reads), len(side.aliased)

    def body(*refs):
        n_in, n_out = nr + na, na + len(side.fresh)
        outs = refs[n_in:n_in + n_out]
        copies = side.build(refs[:nr], outs[:na], outs[na:], *refs[-2:])
        for cp in copies:
            cp.start()
        for cp in copies:
            cp.wait()

    return pl.pallas_call(
        body, name=name, in_specs=side.in_specs(), out_specs=side.out_specs(), out_shape=side.out_shape(),
        input_output_aliases=side.aliases(0, 0), scratch_shapes=side.scratch(),
    )(*side.operands())


def _small_allreduce_adam(gpart, w, m, v, after):
    R = gpart.shape[0]

    def body(g_ref, w_ref, m_ref, v_ref, after_ref, go_ref, d_ref, mo_ref, vo_ref, buf, send_sems, recv_sems):
        x, y, c = lax.axis_index("x"), lax.axis_index("y"), lax.axis_index("c")
        me = 4 * x + 2 * y + c
        buf[me] = g_ref[...]
        copies = []
        for k in range(1, 8):
            fx, fy, fc = (k >> 2) & 1, (k >> 1) & 1, k & 1
            peer = (1 - x if fx else x, 1 - y if fy else y, 1 - c if fc else c)
            cp = _remote(g_ref, buf.at[me], send_sems.at[k - 1], recv_sems.at[k - 1], peer)
            cp.start()
            copies.append((cp, 4 * peer[0] + 2 * peer[1] + peer[2]))
        for k, (cp, pid) in enumerate(copies):
            _remote(g_ref, buf.at[pid], send_sems.at[k], recv_sems.at[k], (x, y, c)).wait_recv()
        for cp, _ in copies:
            cp.wait_send()
        g = buf[0]
        for d in range(1, 8):
            g = g + buf[d]
        delta, mn, vn = _adam_math(w_ref[...], g, m_ref[...], v_ref[...])
        go_ref[...] = g
        d_ref[...] = delta
        mo_ref[...] = mn
        vo_ref[...] = vn

    vm = pl.BlockSpec(memory_space=pltpu.VMEM)
    return pl.pallas_call(
        body, name="small_allreduce_adam",
        in_specs=[vm] * 4 + [ANY], out_specs=[vm] * 4,
        out_shape=[jax.ShapeDtypeStruct((R, LANE), F32)] * 4,
        scratch_shapes=[pltpu.VMEM((8, R, LANE), F32), pltpu.SemaphoreType.DMA((7,)), pltpu.SemaphoreType.DMA((7,))],
    )(gpart, w, m, v, after)


def _pack(arrs):
    flat = jnp.concatenate([a.reshape(-1).astype(F32) for a in arrs])
    rows = -(-flat.shape[0] // (8 * LANE)) * 8
    return jnp.pad(flat, (0, rows * LANE - flat.shape[0])).reshape(rows, LANE)


def _unpack(packed, like):
    flat, out, off = packed.reshape(-1), [], 0
    for a in like:
        out.append(flat[off:off + a.size].reshape(a.shape))
        off += a.size
    return out


def kernel(x, w_in, lb_logits, hg_norm_w, rel_bias, w_branch_a, w_branch_b, w_out, norm_mix_w, norm_mlp_w, w_up, w_down, norm_final_w, loss_target, m_w_in, m_lb_logits, m_hg_norm_w, m_rel_bias, m_w_branch_a, m_w_branch_b, m_w_out, m_norm_mix_w, m_norm_mlp_w, m_w_up, m_w_down, m_norm_final_w, v_w_in, v_lb_logits, v_hg_norm_w, v_rel_bias, v_w_branch_a, v_w_branch_b, v_w_out, v_norm_mix_w, v_norm_mlp_w, v_w_up, v_w_down, v_norm_final_w):
    T, D = x.shape[1], x.shape[2]
    x2, tgt = x.reshape(T, D), loss_target.reshape(T, D)
    big = dict(w_in=(w_in, m_w_in, v_w_in), w_branch_a=(w_branch_a, m_w_branch_a, v_w_branch_a),
               w_branch_b=(w_branch_b, m_w_branch_b, v_w_branch_b), w_out=(w_out, m_w_out, v_w_out),
               w_up=(w_up, m_w_up, v_w_up), w_down=(w_down, m_w_down, v_w_down))
    big = {k: tuple(a[0] for a in v) for k, v in big.items()}
    nfw = norm_final_w.reshape(1, D)

    place = jnp.stack([2 * lax.axis_index("x") + lax.axis_index("y"), lax.axis_index("c")]).astype(jnp.int32)
    Wf = {name: _cast_into_full(name, big[name][0], place) for name in WEIGHTS}
    small3 = ["w_branch_a", "w_branch_b", "w_out"]

    def span(name, lo, hi):
        pr = _piece_shape(name, Wf[name].shape)[0]
        return (pr * lo // 16, pr * (hi - lo) // 16)

    ab = ["w_branch_a", "w_branch_b"]
    u1 = _rms_fwd("norm_mix", x2, norm_mix_w)
    z, Wf["w_in"] = _z_part(u1, Wf["w_in"], None, place, 0, 1, side=_ici_near(["w_in"], [Wf["w_in"]]))
    (Wf["w_in"],) = _run_side("pass_w_in_near", _d2d_gather(["w_in"], [Wf["w_in"]], which=(0, 1)))
    z, Wf["w_in"] = _z_part(u1, Wf["w_in"], z, place, 1, 2, side=_ici_far(["w_in"], [Wf["w_in"]]))
    (Wf["w_in"],) = _run_side("pass_w_in_far", _d2d_gather(["w_in"], [Wf["w_in"]], which=(2,)))
    z, *moved = _z_part(u1, Wf["w_in"], z, place, 3, 1, side=_ici_near(ab, [Wf[n] for n in ab]))
    Wf.update(zip(ab, moved))
    ya, o_hg, states, *moved = _hg_fwd(
        z, lb_logits, hg_norm_w,
        side=_join(_ici_near(["w_out", "w_up", "w_down"], [Wf[n] for n in ("w_out", "w_up", "w_down")],
                             rows=[None, None, span("w_down", 0, 6)]),
                   _ici_far(ab, [Wf[n] for n in ab])))
    Wf.update(zip(["w_out", "w_up", "w_down"] + ab, moved))
    bias_win = _bias_window(rel_bias[0])
    yb, *moved = _at_fwd(
        z, bias_win,
        side=_join(_join(_ici_far(["w_out", "w_up"], [Wf["w_out"], Wf["w_up"]]),
                         _ici_near(["w_down"], [Wf["w_down"]], rows=[span("w_down", 6, 16)])),
                   _d2d_gather(ab, [Wf[n] for n in ab])))
    Wf.update(zip(["w_out", "w_up", "w_down"] + ab, moved))
    pa, Wf["w_out"] = _mm("branch_a", ya, Wf["w_branch_a"], "nn", [F32], side=_d2d_gather(["w_out"], [Wf["w_out"]]))
    pb = _mm("branch_b", yb, Wf["w_branch_b"], "nn", [F32])
    merged = _merge(z, pa, pb)
    add = lambda acc, res: (acc + res,)
    h1, Wf["w_up"] = _mm("out_proj", merged, Wf["w_out"], "nn", [F32], extras=[x2], epilogue=add,
                         side=_d2d_gather(["w_up"], [Wf["w_up"]]))
    u2 = _rms_fwd("norm_mlp", h1, norm_mlp_w)
    relu2 = lambda acc: (acc, jnp.square(jnp.maximum(acc, 0.0)))
    a_pre, act, Wf["w_down"] = _mm("mlp_up", u2, Wf["w_up"], "nn", [F32, BF16], epilogue=relu2,
                                   side=_ici_far(["w_down"], [Wf["w_down"]]))
    (Wf["w_down"],) = _run_side("pass_w_down", _d2d_gather(["w_down"], [Wf["w_down"]]))
    h2 = _mm("mlp_down", act, Wf["w_down"], "nn", [F32], extras=[h1], epilogue=add)
    loss_part, dh2, dh2b, d_nf = _loss_head(h2, tgt, nfw)

    drelu2 = lambda acc, a: (acc * (2.0 * jnp.maximum(a, 0.0)),)
    da = _mm("d_act", dh2b, Wf["w_down"], "nt", [BF16], extras=[a_pre], epilogue=drelu2)
    G = {}
    G["w_down"] = _mm("g_w_down", act, dh2b, "tn", [BF16])
    G["w_up"] = _mm("g_w_up", u2, da, "tn", [BF16])
    T_, S_, GOT = {}, {}, {}
    du2, T_["w_down"], T_["w_up"] = _mm("d_u2", da, Wf["w_up"], "nt", [F32],
                                        side=_sib_send(["w_down", "w_up"], [G["w_down"], G["w_up"]]))
    for n in ("w_down", "w_up"):
        S_[n] = _chip_sum(n, G[n], T_[n], place)
    dh1, dh1b, d_nmlp = _rms_bwd("norm_mlp_bwd", du2, h1, norm_mlp_w, dh2)
    dmerged = _mm("d_merged", dh1b, Wf["w_out"], "nt", [F32])
    G["w_out"] = _mm("g_w_out", merged, dh1b, "tn", [BF16])
    dpa, dpb, dz_ga, dz_gb = _dmerge(dmerged, z, pa, pb)
    dya = _mm("d_ya", dpa, Wf["w_branch_a"], "nt", [F32])
    dyb = _mm("d_yb", dpb, Wf["w_branch_b"], "nt", [F32])
    G["w_branch_a"] = _mm("g_w_a", ya, dpa, "tn", [BF16])
    G["w_branch_b"] = _mm("g_w_b", yb, dpb, "tn", [BF16])
    dz_q, dz_f, dz_i, dz_g, d_lbl, d_hgw, GOT["w_down"], *sent = _hg_bwd(
        z, o_hg, dya, states, lb_logits, hg_norm_w,
        side=_join(_chip_exchange([S_["w_down"]]), _sib_send(small3, [G[n] for n in small3])))
    for n, t in zip(small3, sent):
        S_[n] = _chip_sum(n, G[n], t, place)
    dz_aq, dz_ak, dz_av, dbias_win, GOT["w_up"] = _at_bwd(z, dyb, bias_win, side=_chip_exchange([S_["w_up"]]))
    dz = jnp.concatenate([dz_q, dz_f, dz_i, dz_g, dz_aq, dz_ak, dz_av, dz_ga, dz_gb], axis=1)
    G["w_in"], *got3 = _mm("g_w_in", u1, dz, "tn", [BF16], side=_chip_exchange([S_[n] for n in small3]))
    GOT.update(zip(small3, got3))
    (T_["w_in"],) = _run_side("send_w_in_to_sibling", _sib_send(["w_in"], [G["w_in"]]))
    S_["w_in"] = _chip_sum("w_in", G["w_in"], T_["w_in"], place)
    early = [n for n in WEIGHTS if n != "w_in"]
    H_ = {n: _piece_sum(n, S_[n], GOT[n], place) for n in early}
    flight = _exchange_start("w_in", S_["w_in"])
    share_early = _sib_share([H_[n] for n in early])
    share_early.reads.append(flight[-1])
    du1, *shared = _mm("d_u1", dz, Wf["w_in"], "nt", [F32], side=share_early)
    O_ = dict(zip(early, shared))
    grad_x, _, d_nmix = _rms_bwd("norm_mix_bwd", du1, x2, norm_mix_w, dh1)
    d_rel = _bias_window_grad(dbias_win)
    big_out = {}
    for name in early:
        outs = _adam_quarter(name, *big[name], H_[name], O_[name], place)
        big_out[name] = tuple(a[None] for a in outs)
    S_["w_in"], got_in = _exchange_wait("w_in", flight, outs[1])
    H_["w_in"] = _piece_sum("w_in", S_["w_in"], got_in, place)
    (O_["w_in"],) = _run_side("share_w_in", _sib_share([H_["w_in"]]))
    outs = _adam_quarter("w_in", *big["w_in"], H_["w_in"], O_["w_in"], place)
    big_out["w_in"] = tuple(a[None] for a in outs)

    smalls = [("lb_logits", lb_logits, m_lb_logits, v_lb_logits, d_lbl),
              ("hg_norm_w", hg_norm_w, m_hg_norm_w, v_hg_norm_w, d_hgw),
              ("rel_bias", rel_bias, m_rel_bias, v_rel_bias, d_rel),
              ("norm_mix_w", norm_mix_w, m_norm_mix_w, v_norm_mix_w, d_nmix),
              ("norm_mlp_w", norm_mlp_w, m_norm_mlp_w, v_norm_mlp_w, d_nmlp),
              ("norm_final_w", norm_final_w, m_norm_final_w, v_norm_final_w, d_nf)]
    like = [s[1] for s in smalls]
    packed = _small_allreduce_adam(_pack([s[4] for s in smalls]), _pack(like), _pack([s[2] for s in smalls]),
                                   _pack([s[3] for s in smalls]), got_in)
    small_out = {s[0]: vals for s, vals in zip(smalls, zip(*[_unpack(p, like) for p in packed]))}

    loss = lax.psum(loss_part[0, 0], ("x", "y", "c"))
    order = ["w_in", "lb_logits", "hg_norm_w", "rel_bias", "w_branch_a", "w_branch_b", "w_out", "norm_mix_w",
             "norm_mlp_w", "w_up", "w_down", "norm_final_w"]
    res = {**big_out, **small_out}
    return (loss, grad_x.reshape(x.shape), *[res[n][0] for n in order], *[res[n][1] for n in order],
            *[res[n][2] for n in order], *[res[n][3] for n in order])
```

```python
import functools

import jax
import jax.numpy as jnp
from jax import lax
from jax.experimental import pallas as pl
from jax.experimental.pallas import tpu as pltpu

F32 = jnp.float32
BF16 = jnp.bfloat16
HIGHEST = lax.Precision.HIGHEST

D_MODEL = 2048
SEQ = 2048
CHUNK = 64
HG_HEADS = 8
HG_D = 128
AT_HEADS = 16
AT_DH = 64
LEFT = 8
REL_CLIP = 256
D_FF = 8192
EPS = 1e-6
ADAM_LR = 0.001
ADAM_B1 = 0.9
ADAM_B2 = 0.999
ADAM_EPS = 1e-08
ADAM_WD = 0.01
ADAM_STEP = 10

LANE = 128
NEG = -1e30
EXP_CLAMP = 80.0
VMEM_LIMIT = 48 * 1024 * 1024
MM_TM, MM_TN, MM_TK = 1024, 1024, 2816
ROW_TILE = 256
QB = 2 * CHUNK


def _hgw():
    return HG_HEADS * HG_D


def _atw():
    return AT_HEADS * AT_DH


def _cparams(sem):
    return pltpu.CompilerParams(dimension_semantics=sem, vmem_limit_bytes=VMEM_LIMIT)


def _sigmoid(x):
    return jax.nn.sigmoid(x)


def _dot(a, b, dims, precision=None):
    return lax.dot_general(a, b, (dims, ((), ())), preferred_element_type=F32, precision=precision)


def _nn(a, b, precision=None):
    return _dot(a, b, ((1,), (0,)), precision)


def _nt(a, b, precision=None):
    return _dot(a, b, ((1,), (1,)), precision)


def _tn(a, b, precision=None):
    return _dot(a, b, ((0,), (0,)), precision)


class _Side:
    def __init__(self, build, nsem, reads=(), aliased=(), fresh=()):
        self.build, self.nsem = build, nsem
        self.reads, self.aliased, self.fresh = list(reads), list(aliased), list(fresh)

    def operands(self):
        return self.reads + self.aliased

    def in_specs(self):
        return [ANY] * len(self.operands())

    def out_specs(self):
        return [ANY] * (len(self.aliased) + len(self.fresh))

    def out_shape(self):
        return [jax.ShapeDtypeStruct(a.shape, a.dtype) for a in self.aliased] + self.fresh

    def aliases(self, n_in, n_out):
        return {n_in + len(self.reads) + t: n_out + t for t in range(len(self.aliased))}

    def scratch(self):
        return [pltpu.SemaphoreType.DMA((self.nsem,)), pltpu.SemaphoreType.DMA((self.nsem,))]

    def hooks(self, in_refs, out_refs, sems, first, last):
        nr, na = len(self.reads), len(self.aliased)
        args = (in_refs[:nr], out_refs[:na], out_refs[na:], *sems)

        @pl.when(first)
        def _():
            for cp in self.build(*args):
                cp.start()

        @pl.when(last)
        def _():
            for cp in self.build(*args):
                cp.wait()


def _side_parts(side):
    if side is None:
        return [], [], [], [], lambda n_in, n_out: {}, []
    return side.operands(), side.in_specs(), side.out_specs(), side.out_shape(), side.aliases, side.scratch()


def _call_with_side(body, name, grid, in_specs, out_specs, out_shape, scratch, sem, operands, side, n_prefetch=0,
                    aliases=None, borrow=None):
    _, _, s_out, s_shape, _, s_scr = _side_parts(side)
    n_in, n_out = n_prefetch + len(in_specs), len(out_specs)
    borrow = borrow or {}
    s_ops, s_alias = [], {}
    if side is not None:
        keep = [t for t in range(len(side.aliased)) if t not in borrow]
        s_ops = side.reads + [side.aliased[t] for t in keep]
        s_alias = {n_in + len(side.reads) + pos: n_out + t for pos, t in enumerate(keep)}
        s_alias.update({n_prefetch + i: n_out + t for t, i in borrow.items()})
    s_in = [ANY] * len(s_ops)
    n_sin, n_sout = len(s_ops), len(s_out)

    def wrapped(*refs):
        a, b, c = n_in + n_sin, n_in + n_sin + n_out, n_in + n_sin + n_out + n_sout
        ids = [pl.program_id(d) for d in range(len(grid))]
        first = functools.reduce(lambda p, q: p & q, [i == 0 for i in ids])
        last = functools.reduce(lambda p, q: p & q, [i == g - 1 for i, g in zip(ids, grid)])
        side.hooks(refs[n_in:a], refs[b:c], refs[-2:], first, last)
        body(*refs[:n_in], *refs[a:b], *refs[c:-2])

    spec = dict(grid=grid, in_specs=in_specs + s_in, out_specs=out_specs + s_out, scratch_shapes=scratch + s_scr)
    if n_prefetch:
        spec = dict(grid_spec=pltpu.PrefetchScalarGridSpec(num_scalar_prefetch=n_prefetch, **spec))
    return pl.pallas_call(
        body if side is None else wrapped, name=name, out_shape=out_shape + s_shape,
        input_output_aliases={**s_alias, **{n_prefetch + i: o for i, o in (aliases or {}).items()}},
        compiler_params=_cparams(sem if side is None else ("arbitrary",) * len(grid)), **spec,
    )(*operands, *s_ops)


def _mm_tk(K):
    if K <= MM_TK:
        return K
    return max(t for t in range(LANE, MM_TK + 1, LANE) if K % t == 0)


def _mm(name, a, b, mode, out_dtypes, extras=(), epilogue=None, side=None):
    if mode == "nn":
        (M, K), (K2, N) = a.shape, b.shape
    elif mode == "nt":
        (M, K), (N, K2) = a.shape, b.shape
    else:
        (K, M), (K2, N) = a.shape, b.shape
    assert K == K2, (name, a.shape, b.shape)
    tm, tn, tk = min(MM_TM, M), min(MM_TN, N), _mm_tk(K)
    assert M % tm == 0 and N % tn == 0 and K % tk == 0, (name, M, N, K)
    ni, nj, nk = M // tm, N // tn, K // tk
    ne, no = len(extras), len(out_dtypes)
    if epilogue is None:
        epilogue = lambda acc: (acc,)
    s_ops, s_in, s_out, s_shape, s_alias, s_scr = _side_parts(side)
    n_in, n_sin, n_sout = 2 + ne, len(s_ops), len(s_out)

    def body(*refs):
        a_ref, b_ref = refs[:2]
        extra_refs = refs[2:n_in]
        out_refs = refs[n_in + n_sin:n_in + n_sin + no]
        rest = refs[n_in + n_sin + no + n_sout:]
        i, j, k = pl.program_id(0), pl.program_id(1), pl.program_id(2)
        if side is not None:
            side.hooks(refs[n_in:n_in + n_sin], refs[n_in + n_sin + no:n_in + n_sin + no + n_sout], rest[-2:],
                       (i == 0) & (j == 0) & (k == 0), (i == ni - 1) & (j == nj - 1) & (k == nk - 1))
        av, bv = a_ref[...].astype(BF16), b_ref[...].astype(BF16)
        prod = _nn(av, bv) if mode == "nn" else _nt(av, bv) if mode == "nt" else _tn(av, bv)

        def finish(acc):
            res = epilogue(acc, *[e[...] for e in extra_refs])
            for o_ref, r in zip(out_refs, res):
                o_ref[...] = r.astype(o_ref.dtype)

        if nk == 1:
            finish(prod)
        else:
            acc_ref = rest[0]

            @pl.when(k == 0)
            def _():
                acc_ref[...] = prod

            @pl.when((k > 0) & (k < nk - 1))
            def _():
                acc_ref[...] += prod

            @pl.when(k == nk - 1)
            def _():
                finish(acc_ref[...] + prod)

    if mode == "nn":
        a_spec = pl.BlockSpec((tm, tk), lambda i, j, k: (i, k))
        b_spec = pl.BlockSpec((tk, tn), lambda i, j, k: (k, j))
    elif mode == "nt":
        a_spec = pl.BlockSpec((tm, tk), lambda i, j, k: (i, k))
        b_spec = pl.BlockSpec((tn, tk), lambda i, j, k: (j, k))
    else:
        a_spec = pl.BlockSpec((tk, tm), lambda i, j, k: (k, i))
        b_spec = pl.BlockSpec((tk, tn), lambda i, j, k: (k, j))
    o_spec = pl.BlockSpec((tm, tn), lambda i, j, k: (i, j))
    sem = ("arbitrary",) * 3 if side is not None else ("parallel", "parallel", "arbitrary")
    outs = pl.pallas_call(
        body, name=name,
        grid=(ni, nj, nk),
        in_specs=[a_spec, b_spec] + [o_spec] * ne + s_in,
        out_specs=[o_spec] * no + s_out,
        out_shape=[jax.ShapeDtypeStruct((M, N), dt) for dt in out_dtypes] + s_shape,
        input_output_aliases=s_alias(n_in, no),
        scratch_shapes=([pltpu.VMEM((tm, tn), F32)] if nk > 1 else []) + s_scr,
        compiler_params=_cparams(sem),
    )(a, b, *extras, *s_ops)
    return outs[0] if len(outs) == 1 else outs


def _row_spec(tr, d):
    return pl.BlockSpec((tr, d), lambda i: (i, 0))


def _vec_spec(d):
    return pl.BlockSpec((1, d), lambda i: (0, 0))


def _rms_fwd(name, x, w):
    T, D = x.shape
    tr = min(ROW_TILE, T)

    def body(x_ref, w_ref, o_ref):
        xf = x_ref[...]
        r = lax.rsqrt(jnp.mean(xf * xf, axis=-1, keepdims=True) + EPS)
        o_ref[...] = (xf * r * w_ref[...]).astype(BF16)

    return pl.pallas_call(
        body, name=name, grid=(T // tr,),
        in_specs=[_row_spec(tr, D), _vec_spec(D)], out_specs=_row_spec(tr, D),
        out_shape=jax.ShapeDtypeStruct((T, D), BF16),
        compiler_params=_cparams(("parallel",)),
    )(x, w)


def _rms_bwd(name, dy, h, w, dres, side=None):
    T, D = h.shape
    tr = min(ROW_TILE, T)

    def body(dy_ref, h_ref, w_ref, dres_ref, dh_ref, dhb_ref, dw_ref):
        @pl.when(pl.program_id(0) == 0)
        def _():
            dw_ref[...] = jnp.zeros_like(dw_ref)

        hf, dyv = h_ref[...], dy_ref[...]
        r = lax.rsqrt(jnp.mean(hf * hf, axis=-1, keepdims=True) + EPS)
        xhat = hf * r
        dw_ref[...] += jnp.sum(dyv * xhat, axis=0, keepdims=True)
        dxh = dyv * w_ref[...]
        dh = dres_ref[...] + r * (dxh - xhat * jnp.mean(dxh * xhat, axis=-1, keepdims=True))
        dh_ref[...] = dh
        dhb_ref[...] = dh.astype(BF16)

    return _call_with_side(
        body, name, (T // tr,),
        [_row_spec(tr, D), _row_spec(tr, D), _vec_spec(D), _row_spec(tr, D)],
        [_row_spec(tr, D), _row_spec(tr, D), _vec_spec(D)],
        [jax.ShapeDtypeStruct((T, D), F32), jax.ShapeDtypeStruct((T, D), BF16), jax.ShapeDtypeStruct((1, D), F32)],
        [], ("arbitrary",), (dy, h, w, dres), side)


def _loss_head(h2, target, w):
    T, D = h2.shape
    tr = min(ROW_TILE, T)

    def body(h_ref, t_ref, w_ref, loss_ref, dh_ref, dhb_ref, dw_ref):
        @pl.when(pl.program_id(0) == 0)
        def _():
            dw_ref[...] = jnp.zeros_like(dw_ref)
            loss_ref[...] = jnp.zeros_like(loss_ref)

        hf, wv = h_ref[...], w_ref[...]
        r = lax.rsqrt(jnp.mean(hf * hf, axis=-1, keepdims=True) + EPS)
        xhat = hf * r
        diff = xhat * wv - t_ref[...]
        loss_ref[...] += 0.5 * jnp.sum(jnp.mean(diff * diff, axis=-1, keepdims=True))
        dyv = diff * (1.0 / D)
        dw_ref[...] += jnp.sum(dyv * xhat, axis=0, keepdims=True)
        dxh = dyv * wv
        dh = r * (dxh - xhat * jnp.mean(dxh * xhat, axis=-1, keepdims=True))
        dh_ref[...] = dh
        dhb_ref[...] = dh.astype(BF16)

    return pl.pallas_call(
        body, name="loss_head", grid=(T // tr,),
        in_specs=[_row_spec(tr, D), _row_spec(tr, D), _vec_spec(D)],
        out_specs=[_vec_spec(LANE), _row_spec(tr, D), _row_spec(tr, D), _vec_spec(D)],
        out_shape=[jax.ShapeDtypeStruct((1, LANE), F32), jax.ShapeDtypeStruct((T, D), F32),
                   jax.ShapeDtypeStruct((T, D), BF16), jax.ShapeDtypeStruct((1, D), F32)],
        compiler_params=_cparams(("arbitrary",)),
    )(h2, target, w)


def _gate_tiles(T, D):
    goff = 4 * _hgw() + 3 * _atw()
    tc = min(1024, D)
    assert goff % tc == 0 and D % tc == 0
    return min(ROW_TILE, T), tc, goff // tc, D // tc


def _merge(z, pa, pb):
    T, D = pa.shape
    tr, tc, g0, nd = _gate_tiles(T, D)

    def body(ga_ref, gb_ref, pa_ref, pb_ref, o_ref):
        o_ref[...] = (_sigmoid(ga_ref[...]) * pa_ref[...] + _sigmoid(gb_ref[...]) * pb_ref[...]).astype(BF16)

    t = pl.BlockSpec((tr, tc), lambda i, j: (i, j))
    return pl.pallas_call(
        body, name="merge", grid=(T // tr, nd),
        in_specs=[pl.BlockSpec((tr, tc), lambda i, j: (i, g0 + j)),
                  pl.BlockSpec((tr, tc), lambda i, j: (i, g0 + nd + j)), t, t],
        out_specs=t, out_shape=jax.ShapeDtypeStruct((T, D), BF16),
        compiler_params=_cparams(("parallel", "parallel")),
    )(z, z, pa, pb)


def _dmerge(dm, z, pa, pb):
    T, D = pa.shape
    tr, tc, g0, nd = _gate_tiles(T, D)

    def body(dm_ref, ga_ref, gb_ref, pa_ref, pb_ref, dpa_ref, dpb_ref, dga_ref, dgb_ref):
        dmv = dm_ref[...]
        sa, sb = _sigmoid(ga_ref[...]), _sigmoid(gb_ref[...])
        dpa_ref[...] = (dmv * sa).astype(BF16)
        dpb_ref[...] = (dmv * sb).astype(BF16)
        dga_ref[...] = (dmv * pa_ref[...] * sa * (1.0 - sa)).astype(BF16)
        dgb_ref[...] = (dmv * pb_ref[...] * sb * (1.0 - sb)).astype(BF16)

    t = pl.BlockSpec((tr, tc), lambda i, j: (i, j))
    return pl.pallas_call(
        body, name="dmerge", grid=(T // tr, nd),
        in_specs=[t, pl.BlockSpec((tr, tc), lambda i, j: (i, g0 + j)),
                  pl.BlockSpec((tr, tc), lambda i, j: (i, g0 + nd + j)), t, t],
        out_specs=[t, t, t, t],
        out_shape=[jax.ShapeDtypeStruct((T, D), BF16)] * 4,
        compiler_params=_cparams(("parallel", "parallel")),
    )(dm, z, z, pa, pb)


def _hg_gates(xq, xf, lb):
    f = _sigmoid(xf)
    g = lb + (1.0 - lb) * f
    sq = _sigmoid(xq)
    return f, g, jnp.log(g), 1.0 - g, sq, xq * sq * (HG_D ** -0.5)


def _split2(x):
    hi = x.astype(BF16)
    return hi, (x - hi.astype(F32)).astype(BF16)


def _tri_sum(tri, x):
    hi, rest = x.astype(BF16), x - x.astype(BF16).astype(F32)
    mid, lo = _split2(rest)
    return _nn(tri, lo) + _nn(tri, mid) + _nn(tri, hi)


def _hg_decays(lg, tri_incl, rowi):
    b = _tri_sum(tri_incl, lg)
    b_last = jnp.sum(lg, axis=0, keepdims=True)
    b_mid = jnp.sum(jnp.where(rowi <= CHUNK // 2, lg, 0.0), axis=0, keepdims=True)
    return b, b_last, b_mid


HG_GROUP = 2


def _hg_in_specs(T):
    ng = HG_HEADS // HG_GROUP
    return [pl.BlockSpec((T, HG_GROUP * HG_D), lambda h, s=s: (0, s * ng + h)) for s in range(4)]


def _hg_fwd(z, lb_logits, hgw, side=None):
    T = z.shape[0]
    H, d, C, G = HG_HEADS, HG_D, CHUNK, HG_GROUP
    nc = T // C

    def body(hq_ref, hf_ref, hi_ref, hg_ref, lbl_ref, w_ref, ya_ref, o_ref, s_ref):
        lb_all = 1.0 / (1.0 + jnp.exp(lbl_ref[1:2, :] - lbl_ref[0:1, :]))
        wv = w_ref[...]
        row = lax.broadcasted_iota(jnp.int32, (C, C), 0)
        col = lax.broadcasted_iota(jnp.int32, (C, C), 1)
        tril = col <= row
        tri_incl = tril.astype(BF16)
        rowi = lax.broadcasted_iota(jnp.int32, (C, G * d), 0)
        lanes = [slice(hh * d, (hh + 1) * d) for hh in range(G)]
        per_head = lambda fn: jnp.concatenate([fn(hh, sl) for hh, sl in enumerate(lanes)], axis=1)
        wv_all = jnp.tile(wv, (1, G))

        def chunk(c, states):
            rows = pl.ds(pl.multiple_of(c * C, C), C)
            xq, xf, v, xg = hq_ref[rows, :], hf_ref[rows, :], hi_ref[rows, :], hg_ref[rows, :]
            _, _, lg, kk, _, q = _hg_gates(xq, xf, lb_all)
            b, b_last, b_mid = _hg_decays(lg, tri_incl, rowi)
            vb, qe = v.astype(BF16), (q * jnp.exp(b)).astype(BF16)
            qt = (q * jnp.exp(b - b_mid)).astype(BF16)
            kt = (kk * jnp.exp(jnp.minimum(b_mid - b, EXP_CLAMP))).astype(BF16)
            kd, e_last = (kk * jnp.exp(b_last - b)).astype(BF16), jnp.exp(b_last)
            for hh, st in enumerate(states):
                s_ref[hh, c] = st
            o = per_head(lambda hh, sl: _nt(qe[:, sl], states[hh].astype(BF16)))
            a = [jnp.where(tril, _nt(qt[:, sl], kt[:, sl]), 0.0).astype(BF16) for sl in lanes]
            o = o + per_head(lambda hh, sl: _nn(a[hh], vb[:, sl]))
            o_ref[rows, :] = o
            r = per_head(lambda hh, sl: jnp.broadcast_to(
                lax.rsqrt(jnp.mean(o[:, sl] * o[:, sl], axis=-1, keepdims=True) + EPS), (C, d)))
            ya_ref[rows, :] = (o * r * wv_all * (xg * _sigmoid(xg))).astype(BF16)
            return tuple(st * e_last[:, sl] + _tn(vb[:, sl], kd[:, sl]) for st, sl in zip(states, lanes))

        lax.fori_loop(0, nc, chunk, tuple(jnp.zeros((d, d), F32) for _ in range(G)))

    heads = pl.BlockSpec((T, G * d), lambda h: (0, h))
    return _call_with_side(
        body, "hg_fwd", (H // G,),
        _hg_in_specs(T) + [pl.BlockSpec((2, G * d), lambda h: (0, h)), pl.BlockSpec((1, d), lambda h: (0, 0))],
        [heads, heads, pl.BlockSpec((G, nc, d, d), lambda h: (h, 0, 0, 0))],
        [jax.ShapeDtypeStruct((T, H * d), BF16), jax.ShapeDtypeStruct((T, H * d), F32),
         jax.ShapeDtypeStruct((H, nc, d, d), F32)],
        [], ("parallel",), (z, z, z, z, lb_logits, hgw), side)


def _hg_bwd(z, o, dya, states, lb_logits, hgw, side=None):
    T = z.shape[0]
    H, d, C, G = HG_HEADS, HG_D, CHUNK, HG_GROUP
    nc = T // C
    scale = HG_D ** -0.5

    def body(hq_ref, hf_ref, hi_ref, hg_ref, o_ref, dy_ref, s_ref, lbl_ref, w_ref,
             dq_ref, df_ref, di_ref, dg_ref, dlbl_ref, dw_ref, acc_ref):
        lb_all = 1.0 / (1.0 + jnp.exp(lbl_ref[1:2, :] - lbl_ref[0:1, :]))
        wv = w_ref[...]
        row = lax.broadcasted_iota(jnp.int32, (C, C), 0)
        col = lax.broadcasted_iota(jnp.int32, (C, C), 1)
        tril = col <= row
        tri_incl = tril.astype(BF16)
        triu_incl = (col >= row).astype(BF16)
        rowi = lax.broadcasted_iota(jnp.int32, (C, G * d), 0)
        lanes = [slice(hh * d, (hh + 1) * d) for hh in range(G)]
        per_head = lambda fn: jnp.concatenate([fn(hh, sl) for hh, sl in enumerate(lanes)], axis=1)
        head_mean = lambda x: per_head(
            lambda hh, sl: jnp.broadcast_to(jnp.mean(x[:, sl], axis=-1, keepdims=True), (C, d)))
        wv_all = jnp.tile(wv, (1, G))
        lb = lb_all
        acc_ref[...] = jnp.zeros_like(acc_ref)

        @pl.when(pl.program_id(0) == 0)
        def _():
            dw_ref[...] = jnp.zeros_like(dw_ref)

        def chunk(i, carry):
            dsts, tail = carry
            c = nc - 1 - i
            rows = pl.ds(pl.multiple_of(c * C, C), C)
            xq, xf, v, xg = hq_ref[rows, :], hf_ref[rows, :], hi_ref[rows, :], hg_ref[rows, :]
            f, g, lg, kk, sq, q = _hg_gates(xq, xf, lb)
            b, b_last, b_mid = _hg_decays(lg, tri_incl, rowi)
            e_b, e_qm, e_km = jnp.exp(b), jnp.exp(b - b_mid), jnp.exp(jnp.minimum(b_mid - b, EXP_CLAMP))
            e_kl, e_last = jnp.exp(b_last - b), jnp.exp(b_last)
            ov, dy = o_ref[rows, :], dy_ref[rows, :]
            r = lax.rsqrt(head_mean(ov * ov) + EPS)
            xhat = ov * r
            sg = _sigmoid(xg)
            dxg = dy * xhat * wv_all * (sg * (1.0 + xg * (1.0 - sg)))
            dyn = dy * (xg * sg)
            acc_ref[0:1, :] += jnp.sum(dyn * xhat, axis=0, keepdims=True)
            dxh = dyn * wv_all
            dof = r * (dxh - xhat * head_mean(dxh * xhat))
            do, vb = dof.astype(BF16), v.astype(BF16)
            qe, kd, qt, kt = (q * e_b).astype(BF16), (kk * e_kl).astype(BF16), (q * e_qm).astype(BF16), (kk * e_km).astype(BF16)
            pm = [jnp.where(tril, _nt(do[:, sl], vb[:, sl]), 0.0).astype(BF16) for sl in lanes]
            am = [jnp.where(tril, _nt(qt[:, sl], kt[:, sl]), 0.0).astype(BF16) for sl in lanes]
            st = [_split2(s_ref[hh, c]) for hh in range(G)]
            ds = [_split2(x) for x in dsts]
            dq_state = per_head(lambda hh, sl: _nn(do[:, sl], st[hh][1]) + _nn(do[:, sl], st[hh][0]))
            dk_state = per_head(lambda hh, sl: _nn(vb[:, sl], ds[hh][1]) + _nn(vb[:, sl], ds[hh][0]))
            dq_intra = per_head(lambda hh, sl: _nn(pm[hh], kt[:, sl]))
            dk_intra = per_head(lambda hh, sl: _tn(pm[hh], qt[:, sl]))
            dv = per_head(lambda hh, sl: _tn(am[hh], do[:, sl]) + _nt(kd[:, sl], ds[hh][0]))
            new_dsts = tuple(x * e_last[:, sl] + _tn(do[:, sl], qe[:, sl]) for x, sl in zip(dsts, lanes))
            dq = dq_state * e_b + dq_intra * e_qm
            dk = dk_intra * e_km + dk_state * e_kl
            db = (qe.astype(F32) * dq_state + qt.astype(F32) * dq_intra
                  - kt.astype(F32) * dk_intra - kd.astype(F32) * dk_state)
            dlg = _tri_sum(triu_incl, db) + tail
            dgate = dlg / g - dk
            acc_ref[1:2, :] += jnp.sum(dgate * (1.0 - f), axis=0, keepdims=True)
            dq_ref[rows, :] = (dq * scale * (sq * (1.0 + xq * (1.0 - sq)))).astype(BF16)
            df_ref[rows, :] = (dgate * (1.0 - lb) * f * (1.0 - f)).astype(BF16)
            di_ref[rows, :] = dv.astype(BF16)
            dg_ref[rows, :] = dxg.astype(BF16)
            return new_dsts, tail + jnp.sum(db, axis=0, keepdims=True)

        lax.fori_loop(0, nc, chunk, (tuple(jnp.zeros((d, d), F32) for _ in range(G)), jnp.zeros((1, G * d), F32)))
        dw_ref[...] += functools.reduce(lambda p, q: p + q, [acc_ref[0:1, sl] for sl in lanes])
        dl0 = acc_ref[1:2, :] * lb_all * (1.0 - lb_all)
        dlbl_ref[0:1, :] = dl0
        dlbl_ref[1:2, :] = -dl0

    heads = pl.BlockSpec((T, G * d), lambda h: (0, h))
    logits = pl.BlockSpec((2, G * d), lambda h: (0, h))
    return _call_with_side(
        body, "hg_bwd", (H // G,),
        _hg_in_specs(T) + [heads, heads, pl.BlockSpec((G, nc, d, d), lambda h: (h, 0, 0, 0)), logits,
                           pl.BlockSpec((1, d), lambda h: (0, 0))],
        [heads, heads, heads, heads, logits, pl.BlockSpec((1, d), lambda h: (0, 0))],
        [jax.ShapeDtypeStruct((T, H * d), BF16)] * 4 + [jax.ShapeDtypeStruct((2, H * d), F32),
                                                        jax.ShapeDtypeStruct((1, d), F32)],
        [pltpu.VMEM((8, G * d), F32)], ("arbitrary",), (z, z, z, z, o, dya, states, lb_logits, hgw), side)


def _at_dims():
    pad = LEFT * CHUNK
    return pad, QB + pad, AT_HEADS * AT_DH // LANE, 4 * _hgw() // LANE


def _rel_of_period():
    pad, W, _, _ = _at_dims()
    n = jnp.arange(QB + W)
    return jnp.clip(pad - jnp.where(n < W, n, n - (QB + W)), -REL_CLIP, REL_CLIP) + REL_CLIP


def _bias_window(rel_bias):
    pad, W, _, _ = _at_dims()
    H, P = rel_bias.shape[0], QB + W
    per = rel_bias[:, _rel_of_period()]
    win = jnp.tile(per, (1, QB))[:, :QB * (P - 1)].reshape(H, QB, P - 1)[:, :, :W]
    t = jnp.arange(QB)[:, None]
    j = jnp.arange(W)[None, :]
    ok = (j // CHUNK >= t // CHUNK) & (j // CHUNK <= t // CHUNK + LEFT)
    return jnp.where(ok[None], win, NEG)


def _bias_window_grad(dbw):
    pad, W, _, _ = _at_dims()
    H, P = dbw.shape[0], QB + W
    flat = jnp.pad(dbw, ((0, 0), (0, 0), (0, P - 1 - W))).reshape(H, QB * (P - 1))
    per = jnp.pad(flat, ((0, 0), (0, QB))).reshape(H, QB, P).sum(axis=1)
    onehot = _rel_of_period()[:, None] == jnp.arange(2 * REL_CLIP + 1)[None, :]
    return jnp.dot(per, onehot.astype(F32), precision=HIGHEST)


def _at_softmax(q_halves, kw, bias_ref, valid):
    s = [_nt(qh, kw) * (AT_DH ** -0.5) + bias_ref[hh] for hh, qh in enumerate(q_halves)]
    s = [jnp.where(valid, x, NEG) for x in s]
    e = [jnp.exp(x - jnp.max(x, axis=-1, keepdims=True)) for x in s]
    return [x / jnp.sum(x, axis=-1, keepdims=True) for x in e]


def _at_fwd(z, bias_win, side=None):
    T = z.shape[0]
    pad, W, HP, c0 = _at_dims()
    nq = T // QB

    def body(q_ref, k_ref, v_ref, bias_ref, o_ref, kpad, vpad):
        qi = pl.program_id(1)

        @pl.when(qi == 0)
        def _():
            kpad[0:pad, :] = jnp.zeros((pad, LANE), BF16)
            vpad[0:pad, :] = jnp.zeros((pad, LANE), BF16)
            kpad[pad:, :] = k_ref[...].astype(BF16)
            vpad[pad:, :] = v_ref[...].astype(BF16)

        win = pl.ds(pl.multiple_of(qi * QB, QB), W)
        kw, vw = kpad[win, :], vpad[win, :]
        q = q_ref[...]
        lane = lax.broadcasted_iota(jnp.int32, (QB, LANE), 1)
        first = lane < AT_DH
        valid = lax.broadcasted_iota(jnp.int32, (QB, W), 1) + qi * QB >= pad
        qs = [jnp.where(first, q, 0.0).astype(BF16), jnp.where(first, 0.0, q).astype(BF16)]
        oa, ob = [_nn(p.astype(BF16), vw) for p in _at_softmax(qs, kw, bias_ref, valid)]
        o_ref[...] = jnp.where(first, oa, ob).astype(BF16)

    full = lambda s: pl.BlockSpec((T, LANE), lambda hp, qi, s=s: (0, c0 + s * HP + hp))
    return _call_with_side(
        body, "at_fwd", (HP, nq),
        [pl.BlockSpec((QB, LANE), lambda hp, qi: (qi, c0 + hp)), full(1), full(2),
         pl.BlockSpec((2, QB, W), lambda hp, qi: (hp, 0, 0))],
        [pl.BlockSpec((QB, LANE), lambda hp, qi: (qi, hp))],
        [jax.ShapeDtypeStruct((T, HP * LANE), BF16)],
        [pltpu.VMEM((T + pad, LANE), BF16)] * 2, ("parallel", "arbitrary"), (z, z, z, bias_win), side)


def _at_bwd(z, dyb, bias_win, side=None):
    T = z.shape[0]
    pad, W, HP, c0 = _at_dims()
    nq = T // QB
    scale = AT_DH ** -0.5

    def body(q_ref, k_ref, v_ref, do_ref, bias_ref, dq_ref, dk_ref, dv_ref, dbias_ref, kpad, vpad, dkpad, dvpad):
        qi = pl.program_id(1)

        @pl.when(qi == 0)
        def _():
            kpad[0:pad, :] = jnp.zeros((pad, LANE), BF16)
            vpad[0:pad, :] = jnp.zeros((pad, LANE), BF16)
            kpad[pad:, :] = k_ref[...].astype(BF16)
            vpad[pad:, :] = v_ref[...].astype(BF16)
            dkpad[...] = jnp.zeros_like(dkpad)
            dvpad[...] = jnp.zeros_like(dvpad)
            dbias_ref[...] = jnp.zeros_like(dbias_ref)

        win = pl.ds(pl.multiple_of(qi * QB, QB), W)
        kw, vw = kpad[win, :], vpad[win, :]
        q, do = q_ref[...], do_ref[...]
        lane = lax.broadcasted_iota(jnp.int32, (QB, LANE), 1)
        first = lane < AT_DH
        valid = lax.broadcasted_iota(jnp.int32, (QB, W), 1) + qi * QB >= pad

        qs = [jnp.where(first, q, 0.0).astype(BF16), jnp.where(first, 0.0, q).astype(BF16)]
        dos = [jnp.where(first, do, 0.0).astype(BF16), jnp.where(first, 0.0, do).astype(BF16)]
        ps = _at_softmax(qs, kw, bias_ref, valid)
        dps = [_nt(doh, vw) for doh in dos]
        dss = [p * (dp - jnp.sum(p * dp, axis=-1, keepdims=True)) for p, dp in zip(ps, dps)]
        for hh, ds in enumerate(dss):
            dbias_ref[hh] += ds
        dsb = [(ds * scale).astype(BF16) for ds in dss]
        pb = [p.astype(BF16) for p in ps]
        dqa, dqb = [_nn(ds, kw) for ds in dsb]
        dka, dkb = [_tn(ds, qh) for ds, qh in zip(dsb, qs)]
        dva, dvb = [_tn(p, doh) for p, doh in zip(pb, dos)]
        dq_ref[...] = jnp.where(first, dqa, dqb).astype(BF16)
        dkpad[win, :] += dka + dkb
        dvpad[win, :] += dva + dvb

        @pl.when(qi == nq - 1)
        def _():
            dk_ref[...] = dkpad[pad:, :].astype(BF16)
            dv_ref[...] = dvpad[pad:, :].astype(BF16)

    full = lambda s: pl.BlockSpec((T, LANE), lambda hp, qi, s=s: (0, c0 + s * HP + hp))
    blk = pl.BlockSpec((QB, LANE), lambda hp, qi: (qi, hp))
    col = pl.BlockSpec((T, LANE), lambda hp, qi: (0, hp))
    bw = pl.BlockSpec((2, QB, W), lambda hp, qi: (hp, 0, 0))
    return _call_with_side(
        body, "at_bwd", (HP, nq),
        [pl.BlockSpec((QB, LANE), lambda hp, qi: (qi, c0 + hp)), full(1), full(2), blk, bw],
        [blk, col, col, bw],
        [jax.ShapeDtypeStruct((T, HP * LANE), BF16)] * 3 + [jax.ShapeDtypeStruct(bias_win.shape, F32)],
        [pltpu.VMEM((T + pad, LANE), BF16)] * 2 + [pltpu.VMEM((T + pad, LANE), F32)] * 2,
        ("parallel", "arbitrary"), (z, z, z, dyb, bias_win), side)


def _piece_tiles(name, full_shape):
    pr, pc = _piece_shape(name, full_shape)
    tr = min(ROW_TILE, pr)
    assert pr % tr == 0
    nt = pr // tr
    if name in ROW_SHARDED:
        return tr, nt, lambda q, half, i: ((2 * q + half) * nt + i, 0)
    return tr, nt, lambda q, half, i: (half * nt + i, q)


def _cast_into_full(name, wq, place):
    full = _full_shape(name, wq.shape)
    pc = wq.shape[1]
    tr, nt, at = _piece_tiles(name, full)

    def body(place_ref, w_ref, o_ref):
        o_ref[...] = w_ref[...].astype(BF16)

    return pl.pallas_call(
        body, name="cast_" + name,
        grid_spec=pltpu.PrefetchScalarGridSpec(
            num_scalar_prefetch=1, grid=(2, nt),
            in_specs=[pl.BlockSpec((tr, pc), lambda h, i, s: (h * nt + i, 0))],
            out_specs=pl.BlockSpec((tr, pc), lambda h, i, s: at(s[0], h, i))),
        out_shape=jax.ShapeDtypeStruct(full, BF16),
        compiler_params=_cparams(("parallel", "parallel")),
    )(place, wq)


def _chip_sum(name, grad, theirs, place):
    pr, pc = theirs.shape[1:]
    tr, nt, at = _piece_tiles(name, grad.shape)

    def body(place_ref, g_ref, t_ref, o_ref):
        o_ref[...] = (g_ref[...].astype(F32) + t_ref[...].astype(F32)).astype(BF16)

    piece = pl.BlockSpec((None, tr, pc), lambda q, i, s: (q, i, 0))
    return pl.pallas_call(
        body, name="chip_sum_" + name,
        grid_spec=pltpu.PrefetchScalarGridSpec(
            num_scalar_prefetch=1, grid=(4, nt),
            in_specs=[pl.BlockSpec((tr, pc), lambda q, i, s: at(q, s[1], i)), piece], out_specs=piece),
        out_shape=jax.ShapeDtypeStruct(theirs.shape, BF16),
        compiler_params=_cparams(("parallel", "parallel")),
    )(place, grad, theirs)


def _piece_sum(name, chip_sums, got, place):
    pr, pc = chip_sums.shape[1:]
    tr = min(ROW_TILE, pr)

    def body(place_ref, own_ref, got_ref, o_ref):
        o_ref[...] = (own_ref[...].astype(F32) + got_ref[0].astype(F32) + got_ref[1].astype(F32)
                      + got_ref[2].astype(F32))

    return pl.pallas_call(
        body, name="piece_sum_" + name,
        grid_spec=pltpu.PrefetchScalarGridSpec(
            num_scalar_prefetch=1, grid=(pr // tr,),
            in_specs=[pl.BlockSpec((None, tr, pc), lambda i, s: (s[0], i, 0)),
                      pl.BlockSpec((3, tr, pc), lambda i, s: (0, i, 0))],
            out_specs=pl.BlockSpec((tr, pc), lambda i, s: (i, 0))),
        out_shape=jax.ShapeDtypeStruct((pr, pc), F32),
        compiler_params=_cparams(("parallel",)),
    )(place, chip_sums, got)


def _adam_quarter(name, w, m, v, g_mine, g_sib, place, side=None):
    pr, pc = g_mine.shape
    tr = min(ROW_TILE // 2, pr)
    nt = pr // tr

    def body(place_ref, w_ref, m_ref, v_ref, gm_ref, gs_ref, go_ref, d_ref, mo_ref, vo_ref):
        g = jnp.where(pl.program_id(0) == place_ref[1], gm_ref[...], gs_ref[...])
        delta, mn, vn = _adam_math(w_ref[...], g, m_ref[...], v_ref[...])
        go_ref[...] = g
        d_ref[...] = delta
        mo_ref[...] = mn
        vo_ref[...] = vn

    quarter = pl.BlockSpec((tr, pc), lambda h, i, s: (h * nt + i, 0))
    mine = pl.BlockSpec((tr, pc), lambda h, i, s: (jnp.where(h == s[1], i, 0), 0))
    sib = pl.BlockSpec((tr, pc), lambda h, i, s: (jnp.where(h == s[1], 0, i), 0))
    return _call_with_side(
        body, "adam_" + name, (2, nt), [quarter, quarter, quarter, mine, sib], [quarter] * 4,
        [jax.ShapeDtypeStruct(w.shape, F32)] * 4, [], ("parallel", "parallel"),
        (place, w, m, v, g_mine, g_sib), side, n_prefetch=1)


def _adam_math(w, g, m, v):
    m = ADAM_B1 * m + (1.0 - ADAM_B1) * g
    v = ADAM_B2 * v + (1.0 - ADAM_B2) * (g * g)
    m_hat = m / (1.0 - ADAM_B1 ** ADAM_STEP)
    v_hat = v / (1.0 - ADAM_B2 ** ADAM_STEP)
    return -ADAM_LR * (m_hat / (jnp.sqrt(v_hat) + ADAM_EPS) + ADAM_WD * w), m, v


WEIGHTS = ("w_in", "w_branch_a", "w_branch_b", "w_out", "w_up", "w_down")
ROW_SHARDED = ("w_out", "w_down")
ANY = pl.BlockSpec(memory_space=pl.ANY)
MESH = pl.DeviceIdType.MESH


def _place():
    x, y, c = lax.axis_index("x"), lax.axis_index("y"), lax.axis_index("c")
    chips = [(1 - x, y), (x, 1 - y), (1 - x, 1 - y)]
    return x, y, c, 2 * x + y, chips, [2 * cx + cy for cx, cy in chips]


def _piece(full_ref, name, q, half):
    K, N = full_ref.shape
    if name in ROW_SHARDED:
        rows = K // 8
        return full_ref.at[pl.ds(q * (2 * rows) + half * rows, rows), :]
    return full_ref.at[pl.ds(half * (K // 2), K // 2), pl.ds(q * (N // 4), N // 4)]


def _piece_shape(name, full_shape):
    K, N = full_shape
    return (K // 8, N) if name in ROW_SHARDED else (K // 2, N // 4)


def _full_shape(name, quarter_shape):
    Kq, Nq = quarter_shape
    return (4 * Kq, Nq) if name in ROW_SHARDED else (Kq, 4 * Nq)


def _remote(src, dst, send_sem, recv_sem, device):
    return pltpu.make_async_remote_copy(src_ref=src, dst_ref=dst, send_sem=send_sem, recv_sem=recv_sem,
                                        device_id=device, device_id_type=MESH)


def _z_part(u1, w_in, z_prev, place, k0, count, side=None):
    T, K = u1.shape
    N = w_in.shape[1]
    nq = N // 4
    tn = nq // 2 if (nq // 2) % LANE == 0 else nq
    tm = min(MM_TM, T)
    per = nq // tn
    col = lambda g, j, s: (s[0] ^ (k0 + g)) * per + j
    ins = [pl.BlockSpec((tm, K), lambda g, i, j, s: (i, 0)), pl.BlockSpec((K, tn), lambda g, i, j, s: (0, col(g, j, s)))]
    operands = [place, u1, w_in]
    if z_prev is not None:
        ins.append(ANY)
        operands.append(z_prev)

    def body(place_ref, a_ref, b_ref, *rest):
        rest[-1][...] = _nn(a_ref[...], b_ref[...])

    return _call_with_side(
        body, "z_part_%d" % k0, (count, T // tm, per), ins,
        [pl.BlockSpec((tm, tn), lambda g, i, j, s: (i, col(g, j, s)))], [jax.ShapeDtypeStruct((T, N), F32)],
        [], ("parallel",) * 3, tuple(operands), side, n_prefetch=1, aliases={} if z_prev is None else {2: 0},
        borrow={0: 1} if side is not None and side.aliased and side.aliased[0] is w_in else None)


def _rows(ref, span):
    return ref if span is None else ref.at[pl.ds(span[0], span[1]), :]


def _ici_near(names, fulls, rows=None):
    rows = rows or [None] * len(names)

    def build(reads, aliased, fresh, send_sems, recv_sems, off=0):
        _, _, c, p, chips, _ = _place()
        out = []
        for i, (ref, name) in enumerate(zip(aliased, names)):
            mine = _rows(_piece(ref, name, p, c), rows[i])
            for j, chip in enumerate(chips[:2]):
                k = off + 2 * i + j
                out.append(_remote(mine, mine, send_sems.at[k], recv_sems.at[k], (*chip, c)))
        return out

    return _Side(build, 2 * len(names), aliased=fulls)


def _ici_far(names, fulls, rows=None):
    rows = rows or [None] * len(names)

    def build(reads, aliased, fresh, send_sems, recv_sems, off=0):
        x, y, c, _, _, chip_ids = _place()
        south = c == 0
        src_chip = jnp.where(south, chip_ids[0], chip_ids[1])
        target = (jnp.where(south, x, 1 - x), jnp.where(south, 1 - y, y), c)
        out = []
        for i, (ref, name) in enumerate(zip(aliased, names)):
            landed = _rows(_piece(ref, name, src_chip, c), rows[i])
            out.append(_remote(landed, landed, send_sems.at[off + i], recv_sems.at[off + i], target))
        return out

    return _Side(build, len(names), aliased=fulls)


def _d2d_gather(names, fulls, which=(0, 1, 2)):
    def build(reads, aliased, fresh, send_sems, recv_sems, off=0):
        x, y, c, _, _, chip_ids = _place()
        out = []
        for i, (ref, name) in enumerate(zip(aliased, names)):
            for n, j in enumerate(which):
                landed, k = _piece(ref, name, chip_ids[j], c), off + len(which) * i + n
                out.append(_remote(landed, landed, send_sems.at[k], recv_sems.at[k], (x, y, 1 - c)))
        return out

    return _Side(build, len(which) * len(names), aliased=fulls)


def _sib_send(names, grads):
    def build(reads, aliased, fresh, send_sems, recv_sems, off=0):
        x, y, c, _, _, _ = _place()
        out = []
        for i, name in enumerate(names):
            for q in range(4):
                k = off + 4 * i + q
                out.append(_remote(_piece(reads[i], name, q, 1 - c), fresh[i].at[q], send_sems.at[k], recv_sems.at[k],
                                   (x, y, 1 - c)))
        return out

    shapes = [jax.ShapeDtypeStruct((4,) + _piece_shape(name, g.shape), BF16) for name, g in zip(names, grads)]
    return _Side(build, 4 * len(names), reads=grads, fresh=shapes)


def _chip_exchange(chip_sums, rows=None, got=None):
    rows = rows or [None] * len(chip_sums)

    def build(reads, aliased, fresh, send_sems, recv_sems, off=0):
        _, _, c, _, chips, chip_ids = _place()
        out = []
        for i in range(len(chip_sums)):
            for j, (chip, cid) in enumerate(zip(chips, chip_ids)):
                k = off + 3 * i + j
                out.append(_remote(_rows(reads[i].at[cid], rows[i]), _rows((aliased or fresh)[i].at[j], rows[i]),
                                   send_sems.at[k], recv_sems.at[k], (*chip, c)))
        return out

    if got is not None:
        return _Side(build, 3 * len(chip_sums), reads=chip_sums, aliased=got)
    shapes = [jax.ShapeDtypeStruct((3,) + s.shape[1:], BF16) for s in chip_sums]
    return _Side(build, 3 * len(chip_sums), reads=chip_sums, fresh=shapes)


HBM = pl.BlockSpec(memory_space=pltpu.HBM)
SEM = pl.BlockSpec(memory_space=pltpu.SEMAPHORE)


def _exchange_copies(s_ref, land_ref, send_sems, recv_sems):
    _, _, c, _, chips, chip_ids = _place()
    return [_remote(s_ref.at[cid], land_ref.at[j], send_sems.at[j], recv_sems.at[j], (*chip, c))
            for j, (chip, cid) in enumerate(zip(chips, chip_ids))]


def _exchange_start(name, chip_sum):
    def body(s_ref, land_ref, send_sems, recv_sems, s_thru, land_thru, token):
        for cp in _exchange_copies(s_ref, land_ref, send_sems, recv_sems):
            cp.start()
        token[...] = jnp.zeros_like(token)

    land = jax.ShapeDtypeStruct((3,) + chip_sum.shape[1:], chip_sum.dtype)
    return pl.pallas_call(
        body, name="exchange_start_" + name,
        out_shape=(pltpu.SemaphoreType.DMA((3,)), pltpu.SemaphoreType.DMA((3,)),
                   pltpu.HBM(chip_sum.shape, chip_sum.dtype), pltpu.HBM(land.shape, land.dtype),
                   jax.ShapeDtypeStruct((8, LANE), F32)),
        in_specs=(HBM, HBM), out_specs=(SEM, SEM, HBM, HBM, pl.BlockSpec(memory_space=pltpu.VMEM)),
        input_output_aliases={0: 2, 1: 3},
        compiler_params=pltpu.CompilerParams(has_side_effects=pltpu.SideEffectType.DATAFLOW_SIDE_EFFECTING),
    )(pltpu.with_memory_space_constraint(chip_sum, pltpu.HBM),
      pltpu.with_memory_space_constraint(lax.empty(land.shape, land.dtype), pltpu.HBM))


def _exchange_wait(name, flight, after):
    send_sems, recv_sems, s_thru, land_thru, _ = flight

    def body(s_ref, land_ref, send_sems, recv_sems, after_ref, s_out, land_out):
        for cp in _exchange_copies(s_ref, land_ref, send_sems, recv_sems):
            cp.wait_send()
            cp.wait_recv()

    return pl.pallas_call(
        body, name="exchange_wait_" + name,
        out_shape=(pltpu.HBM(s_thru.shape, s_thru.dtype), pltpu.HBM(land_thru.shape, land_thru.dtype)),
        in_specs=(HBM, HBM, SEM, SEM, ANY), out_specs=(HBM, HBM), input_output_aliases={0: 0, 1: 1},
        compiler_params=pltpu.CompilerParams(has_side_effects=pltpu.SideEffectType.DATAFLOW_SIDE_EFFECTING),
    )(s_thru, land_thru, send_sems, recv_sems, after)


def _sib_share(halves):
    def build(reads, aliased, fresh, send_sems, recv_sems, off=0):
        x, y, c, _, _, _ = _place()
        return [_remote(reads[i], fresh[i], send_sems.at[off + i], recv_sems.at[off + i], (x, y, 1 - c))
                for i in range(len(halves))]

    return _Side(build, len(halves), reads=halves, fresh=[jax.ShapeDtypeStruct(h.shape, F32) for h in halves])


def _join(a, b):
    def build(reads, aliased, fresh, send_sems, recv_sems, off=0):
        ra, aa, fa = len(a.reads), len(a.aliased), len(a.fresh)
        return (a.build(reads[:ra], aliased[:aa], fresh[:fa], send_sems, recv_sems, off)
                + b.build(reads[ra:], aliased[aa:], fresh[fa:], send_sems, recv_sems, off + a.nsem))

    return _Side(build, a.nsem + b.nsem, a.reads + b.reads, a.aliased + b.aliased, a.fresh + b.fresh)


def _run_side(name, side):
    nr, na = len(side.reads), len(side.aliased)

    def body(*refs):
        n_in, n_out = nr + na, na + len(side.fresh)
        outs = refs[n_in:n_in + n_out]
        copies = side.build(refs[:nr], outs[:na], outs[na:], *refs[-2:])
        for cp in copies:
            cp.start()
        for cp in copies:
            cp.wait()

    return pl.pallas_call(
        body, name=name, in_specs=side.in_specs(), out_specs=side.out_specs(), out_shape=side.out_shape(),
        input_output_aliases=side.aliases(0, 0), scratch_shapes=side.scratch(),
    )(*side.operands())


def _small_allreduce_adam(gpart, w, m, v, after):
    R = gpart.shape[0]

    def body(g_ref, w_ref, m_ref, v_ref, after_ref, go_ref, d_ref, mo_ref, vo_ref, buf, send_sems, recv_sems):
        x, y, c = lax.axis_index("x"), lax.axis_index("y"), lax.axis_index("c")
        me = 4 * x + 2 * y + c
        buf[me] = g_ref[...]
        copies = []
        for k in range(1, 8):
            fx, fy, fc = (k >> 2) & 1, (k >> 1) & 1, k & 1
            peer = (1 - x if fx else x, 1 - y if fy else y, 1 - c if fc else c)
            cp = _remote(g_ref, buf.at[me], send_sems.at[k - 1], recv_sems.at[k - 1], peer)
            cp.start()
            copies.append((cp, 4 * peer[0] + 2 * peer[1] + peer[2]))
        for k, (cp, pid) in enumerate(copies):
            _remote(g_ref, buf.at[pid], send_sems.at[k], recv_sems.at[k], (x, y, c)).wait_recv()
        for cp, _ in copies:
            cp.wait_send()
        g = buf[0]
        for d in range(1, 8):
            g = g + buf[d]
        delta, mn, vn = _adam_math(w_ref[...], g, m_ref[...], v_ref[...])
        go_ref[...] = g
        d_ref[...] = delta
        mo_ref[...] = mn
        vo_ref[...] = vn

    vm = pl.BlockSpec(memory_space=pltpu.VMEM)
    return pl.pallas_call(
        body, name="small_allreduce_adam",
        in_specs=[vm] * 4 + [ANY], out_specs=[vm] * 4,
        out_shape=[jax.ShapeDtypeStruct((R, LANE), F32)] * 4,
        scratch_shapes=[pltpu.VMEM((8, R, LANE), F32), pltpu.SemaphoreType.DMA((7,)), pltpu.SemaphoreType.DMA((7,))],
    )(gpart, w, m, v, after)


def _pack(arrs):
    flat = jnp.concatenate([a.reshape(-1).astype(F32) for a in arrs])
    rows = -(-flat.shape[0] // (8 * LANE)) * 8
    return jnp.pad(flat, (0, rows * LANE - flat.shape[0])).reshape(rows, LANE)


def _unpack(packed, like):
    flat, out, off = packed.reshape(-1), [], 0
    for a in like:
        out.append(flat[off:off + a.size].reshape(a.shape))
        off += a.size
    return out


def kernel(x, w_in, lb_logits, hg_norm_w, rel_bias, w_branch_a, w_branch_b, w_out, norm_mix_w, norm_mlp_w, w_up, w_down, norm_final_w, loss_target, m_w_in, m_lb_logits, m_hg_norm_w, m_rel_bias, m_w_branch_a, m_w_branch_b, m_w_out, m_norm_mix_w, m_norm_mlp_w, m_w_up, m_w_down, m_norm_final_w, v_w_in, v_lb_logits, v_hg_norm_w, v_rel_bias, v_w_branch_a, v_w_branch_b, v_w_out, v_norm_mix_w, v_norm_mlp_w, v_w_up, v_w_down, v_norm_final_w):
    T, D = x.shape[1], x.shape[2]
    x2, tgt = x.reshape(T, D), loss_target.reshape(T, D)
    big = dict(w_in=(w_in, m_w_in, v_w_in), w_branch_a=(w_branch_a, m_w_branch_a, v_w_branch_a),
               w_branch_b=(w_branch_b, m_w_branch_b, v_w_branch_b), w_out=(w_out, m_w_out, v_w_out),
               w_up=(w_up, m_w_up, v_w_up), w_down=(w_down, m_w_down, v_w_down))
    big = {k: tuple(a[0] for a in v) for k, v in big.items()}
    nfw = norm_final_w.reshape(1, D)

    place = jnp.stack([2 * lax.axis_index("x") + lax.axis_index("y"), lax.axis_index("c")]).astype(jnp.int32)
    Wf = {name: _cast_into_full(name, big[name][0], place) for name in WEIGHTS}
    small3 = ["w_branch_a", "w_branch_b", "w_out"]

    def span(name, lo, hi):
        pr = _piece_shape(name, Wf[name].shape)[0]
        return (pr * lo // 16, pr * (hi - lo) // 16)

    ab = ["w_branch_a", "w_branch_b"]
    u1 = _rms_fwd("norm_mix", x2, norm_mix_w)
    z, Wf["w_in"] = _z_part(u1, Wf["w_in"], None, place, 0, 1, side=_ici_near(["w_in"], [Wf["w_in"]]))
    (Wf["w_in"],) = _run_side("pass_w_in_near", _d2d_gather(["w_in"], [Wf["w_in"]], which=(0, 1)))
    z, Wf["w_in"] = _z_part(u1, Wf["w_in"], z, place, 1, 2, side=_ici_far(["w_in"], [Wf["w_in"]]))
    (Wf["w_in"],) = _run_side("pass_w_in_far", _d2d_gather(["w_in"], [Wf["w_in"]], which=(2,)))
    z, *moved = _z_part(u1, Wf["w_in"], z, place, 3, 1, side=_ici_near(ab, [Wf[n] for n in ab]))
    Wf.update(zip(ab, moved))
    ya, o_hg, states, *moved = _hg_fwd(
        z, lb_logits, hg_norm_w,
        side=_join(_ici_near(["w_out", "w_up", "w_down"], [Wf[n] for n in ("w_out", "w_up", "w_down")],
                             rows=[None, None, span("w_down", 0, 6)]),
                   _ici_far(ab, [Wf[n] for n in ab])))
    Wf.update(zip(["w_out", "w_up", "w_down"] + ab, moved))
    bias_win = _bias_window(rel_bias[0])
    yb, *moved = _at_fwd(
        z, bias_win,
        side=_join(_join(_ici_far(["w_out", "w_up"], [Wf["w_out"], Wf["w_up"]]),
                         _ici_near(["w_down"], [Wf["w_down"]], rows=[span("w_down", 6, 16)])),
                   _d2d_gather(ab, [Wf[n] for n in ab])))
    Wf.update(zip(["w_out", "w_up", "w_down"] + ab, moved))
    pa, Wf["w_out"] = _mm("branch_a", ya, Wf["w_branch_a"], "nn", [F32], side=_d2d_gather(["w_out"], [Wf["w_out"]]))
    pb = _mm("branch_b", yb, Wf["w_branch_b"], "nn", [F32])
    merged = _merge(z, pa, pb)
    add = lambda acc, res: (acc + res,)
    h1, Wf["w_up"] = _mm("out_proj", merged, Wf["w_out"], "nn", [F32], extras=[x2], epilogue=add,
                         side=_d2d_gather(["w_up"], [Wf["w_up"]]))
    u2 = _rms_fwd("norm_mlp", h1, norm_mlp_w)
    relu2 = lambda acc: (acc, jnp.square(jnp.maximum(acc, 0.0)))
    a_pre, act, Wf["w_down"] = _mm("mlp_up", u2, Wf["w_up"], "nn", [F32, BF16], epilogue=relu2,
                                   side=_ici_far(["w_down"], [Wf["w_down"]]))
    (Wf["w_down"],) = _run_side("pass_w_down", _d2d_gather(["w_down"], [Wf["w_down"]]))
    h2 = _mm("mlp_down", act, Wf["w_down"], "nn", [F32], extras=[h1], epilogue=add)
    loss_part, dh2, dh2b, d_nf = _loss_head(h2, tgt, nfw)

    drelu2 = lambda acc, a: (acc * (2.0 * jnp.maximum(a, 0.0)),)
    da = _mm("d_act", dh2b, Wf["w_down"], "nt", [BF16], extras=[a_pre], epilogue=drelu2)
    G = {}
    G["w_down"] = _mm("g_w_down", act, dh2b, "tn", [BF16])
    G["w_up"] = _mm("g_w_up", u2, da, "tn", [BF16])
    T_, S_, GOT = {}, {}, {}
    du2, T_["w_down"], T_["w_up"] = _mm("d_u2", da, Wf["w_up"], "nt", [F32],
                                        side=_sib_send(["w_down", "w_up"], [G["w_down"], G["w_up"]]))
    for n in ("w_down", "w_up"):
        S_[n] = _chip_sum(n, G[n], T_[n], place)
    dh1, dh1b, d_nmlp = _rms_bwd("norm_mlp_bwd", du2, h1, norm_mlp_w, dh2)
    dmerged = _mm("d_merged", dh1b, Wf["w_out"], "nt", [F32])
    G["w_out"] = _mm("g_w_out", merged, dh1b, "tn", [BF16])
    dpa, dpb, dz_ga, dz_gb = _dmerge(dmerged, z, pa, pb)
    dya = _mm("d_ya", dpa, Wf["w_branch_a"], "nt", [F32])
    dyb = _mm("d_yb", dpb, Wf["w_branch_b"], "nt", [F32])
    G["w_branch_a"] = _mm("g_w_a", ya, dpa, "tn", [BF16])
    G["w_branch_b"] = _mm("g_w_b", yb, dpb, "tn", [BF16])
    dz_q, dz_f, dz_i, dz_g, d_lbl, d_hgw, GOT["w_down"], *sent = _hg_bwd(
        z, o_hg, dya, states, lb_logits, hg_norm_w,
        side=_join(_chip_exchange([S_["w_down"]]), _sib_send(small3, [G[n] for n in small3])))
    for n, t in zip(small3, sent):
        S_[n] = _chip_sum(n, G[n], t, place)
    dz_aq, dz_ak, dz_av, dbias_win, GOT["w_up"] = _at_bwd(z, dyb, bias_win, side=_chip_exchange([S_["w_up"]]))
    dz = jnp.concatenate([dz_q, dz_f, dz_i, dz_g, dz_aq, dz_ak, dz_av, dz_ga, dz_gb], axis=1)
    G["w_in"], *got3 = _mm("g_w_in", u1, dz, "tn", [BF16], side=_chip_exchange([S_[n] for n in small3]))
    GOT.update(zip(small3, got3))
    (T_["w_in"],) = _run_side("send_w_in_to_sibling", _sib_send(["w_in"], [G["w_in"]]))
    S_["w_in"] = _chip_sum("w_in", G["w_in"], T_["w_in"], place)
    early = [n for n in WEIGHTS if n != "w_in"]
    H_ = {n: _piece_sum(n, S_[n], GOT[n], place) for n in early}
    flight = _exchange_start("w_in", S_["w_in"])
    share_early = _sib_share([H_[n] for n in early])
    share_early.reads.append(flight[-1])
    du1, *shared = _mm("d_u1", dz, Wf["w_in"], "nt", [F32], side=share_early)
    O_ = dict(zip(early, shared))
    grad_x, _, d_nmix = _rms_bwd("norm_mix_bwd", du1, x2, norm_mix_w, dh1)
    d_rel = _bias_window_grad(dbias_win)
    big_out = {}
    for name in early:
        outs = _adam_quarter(name, *big[name], H_[name], O_[name], place)
        big_out[name] = tuple(a[None] for a in outs)
    S_["w_in"], got_in = _exchange_wait("w_in", flight, outs[1])
    H_["w_in"] = _piece_sum("w_in", S_["w_in"], got_in, place)
    (O_["w_in"],) = _run_side("share_w_in", _sib_share([H_["w_in"]]))
    outs = _adam_quarter("w_in", *big["w_in"], H_["w_in"], O_["w_in"], place)
    big_out["w_in"] = tuple(a[None] for a in outs)

    smalls = [("lb_logits", lb_logits, m_lb_logits, v_lb_logits, d_lbl),
              ("hg_norm_w", hg_norm_w, m_hg_norm_w, v_hg_norm_w, d_hgw),
              ("rel_bias", rel_bias, m_rel_bias, v_rel_bias, d_rel),
              ("norm_mix_w", norm_mix_w, m_norm_mix_w, v_norm_mix_w, d_nmix),
              ("norm_mlp_w", norm_mlp_w, m_norm_mlp_w, v_norm_mlp_w, d_nmlp),
              ("norm_final_w", norm_final_w, m_norm_final_w, v_norm_final_w, d_nf)]
    like = [s[1] for s in smalls]
    packed = _small_allreduce_adam(_pack([s[4] for s in smalls]), _pack(like), _pack([s[2] for s in smalls]),
                                   _pack([s[3] for s in smalls]), got_in)
    small_out = {s[0]: vals for s, vals in zip(smalls, zip(*[_unpack(p, like) for p in packed]))}

    loss = lax.psum(loss_part[0, 0], ("x", "y", "c"))
    order = ["w_in", "lb_logits", "hg_norm_w", "rel_bias", "w_branch_a", "w_branch_b", "w_out", "norm_mix_w",
             "norm_mlp_w", "w_up", "w_down", "norm_final_w"]
    res = {**big_out, **small_out}
    return (loss, grad_x.reshape(x.shape), *[res[n][0] for n in order], *[res[n][1] for n in order],
            *[res[n][2] for n in order], *[res[n][3] for n in order])
```

```python
import functools

import jax
import jax.numpy as jnp
from jax import lax
from jax.experimental import pallas as pl
from jax.experimental.pallas import tpu as pltpu

F32 = jnp.float32
BF16 = jnp.bfloat16
HIGHEST = lax.Precision.HIGHEST

D_MODEL = 2048
SEQ = 2048
CHUNK = 64
HG_HEADS = 8
HG_D = 128
AT_HEADS = 16
AT_DH = 64
LEFT = 8
REL_CLIP = 256
D_FF = 8192
EPS = 1e-6
ADAM_LR = 0.001
ADAM_B1 = 0.9
ADAM_B2 = 0.999
ADAM_EPS = 1e-08
ADAM_WD = 0.01
ADAM_STEP = 10

LANE = 128
NEG = -1e30
EXP_CLAMP = 80.0
VMEM_LIMIT = 48 * 1024 * 1024
MM_TM, MM_TN, MM_TK = 1024, 1024, 2816
ROW_TILE = 256
QB = 2 * CHUNK


def _hgw():
    return HG_HEADS * HG_D


def _atw():
    return AT_HEADS * AT_DH


def _cparams(sem):
    return pltpu.CompilerParams(dimension_semantics=sem, vmem_limit_bytes=VMEM_LIMIT)


def _sigmoid(x):
    return jax.nn.sigmoid(x)


def _dot(a, b, dims, precision=None):
    return lax.dot_general(a, b, (dims, ((), ())), preferred_element_type=F32, precision=precision)


def _nn(a, b, precision=None):
    return _dot(a, b, ((1,), (0,)), precision)


def _nt(a, b, precision=None):
    return _dot(a, b, ((1,), (1,)), precision)


def _tn(a, b, precision=None):
    return _dot(a, b, ((0,), (0,)), precision)


class _Side:
    def __init__(self, build, nsem, reads=(), aliased=(), fresh=()):
        self.build, self.nsem = build, nsem
        self.reads, self.aliased, self.fresh = list(reads), list(aliased), list(fresh)

    def operands(self):
        return self.reads + self.aliased

    def in_specs(self):
        return [ANY] * len(self.operands())

    def out_specs(self):
        return [ANY] * (len(self.aliased) + len(self.fresh))

    def out_shape(self):
        return [jax.ShapeDtypeStruct(a.shape, a.dtype) for a in self.aliased] + self.fresh

    def aliases(self, n_in, n_out):
        return {n_in + len(self.reads) + t: n_out + t for t in range(len(self.aliased))}

    def scratch(self):
        return [pltpu.SemaphoreType.DMA((self.nsem,)), pltpu.SemaphoreType.DMA((self.nsem,))]

    def hooks(self, in_refs, out_refs, sems, first, last):
        nr, na = len(self.reads), len(self.aliased)
        args = (in_refs[:nr], out_refs[:na], out_refs[na:], *sems)

        @pl.when(first)
        def _():
            for cp in self.build(*args):
                cp.start()

        @pl.when(last)
        def _():
            for cp in self.build(*args):
                cp.wait()


def _after(*tokens):
    return _Side(lambda *args: [], 1, reads=tokens)


def _side_parts(side):
    if side is None:
        return [], [], [], [], lambda n_in, n_out: {}, []
    return side.operands(), side.in_specs(), side.out_specs(), side.out_shape(), side.aliases, side.scratch()


def _call_with_side(body, name, grid, in_specs, out_specs, out_shape, scratch, sem, operands, side, n_prefetch=0,
                    aliases=None, borrow=None):
    _, _, s_out, s_shape, _, s_scr = _side_parts(side)
    n_in, n_out = n_prefetch + len(in_specs), len(out_specs)
    borrow = borrow or {}
    s_ops, s_alias = [], {}
    if side is not None:
        keep = [t for t in range(len(side.aliased)) if t not in borrow]
        s_ops = side.reads + [side.aliased[t] for t in keep]
        s_alias = {n_in + len(side.reads) + pos: n_out + t for pos, t in enumerate(keep)}
        s_alias.update({n_prefetch + i: n_out + t for t, i in borrow.items()})
    s_in = [ANY] * len(s_ops)
    n_sin, n_sout = len(s_ops), len(s_out)

    def wrapped(*refs):
        a, b, c = n_in + n_sin, n_in + n_sin + n_out, n_in + n_sin + n_out + n_sout
        ids = [pl.program_id(d) for d in range(len(grid))]
        first = functools.reduce(lambda p, q: p & q, [i == 0 for i in ids])
        last = functools.reduce(lambda p, q: p & q, [i == g - 1 for i, g in zip(ids, grid)])
        side.hooks(refs[n_in:a], refs[b:c], refs[-2:], first, last)
        body(*refs[:n_in], *refs[a:b], *refs[c:-2])

    spec = dict(grid=grid, in_specs=in_specs + s_in, out_specs=out_specs + s_out, scratch_shapes=scratch + s_scr)
    if n_prefetch:
        spec = dict(grid_spec=pltpu.PrefetchScalarGridSpec(num_scalar_prefetch=n_prefetch, **spec))
    return pl.pallas_call(
        body if side is None else wrapped, name=name, out_shape=out_shape + s_shape,
        input_output_aliases={**s_alias, **{n_prefetch + i: o for i, o in (aliases or {}).items()}},
        compiler_params=_cparams(sem if side is None else ("arbitrary",) * len(grid)), **spec,
    )(*operands, *s_ops)


def _mm_tk(K):
    if K <= MM_TK:
        return K
    return max(t for t in range(LANE, MM_TK + 1, LANE) if K % t == 0)


def _mm(name, a, b, mode, out_dtypes, extras=(), epilogue=None, side=None):
    if mode == "nn":
        (M, K), (K2, N) = a.shape, b.shape
    elif mode == "nt":
        (M, K), (N, K2) = a.shape, b.shape
    else:
        (K, M), (K2, N) = a.shape, b.shape
    assert K == K2, (name, a.shape, b.shape)
    tm, tn, tk = min(MM_TM, M), min(MM_TN, N), _mm_tk(K)
    assert M % tm == 0 and N % tn == 0 and K % tk == 0, (name, M, N, K)
    ni, nj, nk = M // tm, N // tn, K // tk
    ne, no = len(extras), len(out_dtypes)
    if epilogue is None:
        epilogue = lambda acc: (acc,)
    s_ops, s_in, s_out, s_shape, s_alias, s_scr = _side_parts(side)
    n_in, n_sin, n_sout = 2 + ne, len(s_ops), len(s_out)

    def body(*refs):
        a_ref, b_ref = refs[:2]
        extra_refs = refs[2:n_in]
        out_refs = refs[n_in + n_sin:n_in + n_sin + no]
        rest = refs[n_in + n_sin + no + n_sout:]
        i, j, k = pl.program_id(0), pl.program_id(1), pl.program_id(2)
        if side is not None:
            side.hooks(refs[n_in:n_in + n_sin], refs[n_in + n_sin + no:n_in + n_sin + no + n_sout], rest[-2:],
                       (i == 0) & (j == 0) & (k == 0), (i == ni - 1) & (j == nj - 1) & (k == nk - 1))
        av, bv = a_ref[...].astype(BF16), b_ref[...].astype(BF16)
        prod = _nn(av, bv) if mode == "nn" else _nt(av, bv) if mode == "nt" else _tn(av, bv)

        def finish(acc):
            res = epilogue(acc, *[e[...] for e in extra_refs])
            for o_ref, r in zip(out_refs, res):
                o_ref[...] = r.astype(o_ref.dtype)

        if nk == 1:
            finish(prod)
        else:
            acc_ref = rest[0]

            @pl.when(k == 0)
            def _():
                acc_ref[...] = prod

            @pl.when((k > 0) & (k < nk - 1))
            def _():
                acc_ref[...] += prod

            @pl.when(k == nk - 1)
            def _():
                finish(acc_ref[...] + prod)

    if mode == "nn":
        a_spec = pl.BlockSpec((tm, tk), lambda i, j, k: (i, k))
        b_spec = pl.BlockSpec((tk, tn), lambda i, j, k: (k, j))
    elif mode == "nt":
        a_spec = pl.BlockSpec((tm, tk), lambda i, j, k: (i, k))
        b_spec = pl.BlockSpec((tn, tk), lambda i, j, k: (j, k))
    else:
        a_spec = pl.BlockSpec((tk, tm), lambda i, j, k: (k, i))
        b_spec = pl.BlockSpec((tk, tn), lambda i, j, k: (k, j))
    o_spec = pl.BlockSpec((tm, tn), lambda i, j, k: (i, j))
    sem = ("arbitrary",) * 3 if side is not None else ("parallel", "parallel", "arbitrary")
    outs = pl.pallas_call(
        body, name=name,
        grid=(ni, nj, nk),
        in_specs=[a_spec, b_spec] + [o_spec] * ne + s_in,
        out_specs=[o_spec] * no + s_out,
        out_shape=[jax.ShapeDtypeStruct((M, N), dt) for dt in out_dtypes] + s_shape,
        input_output_aliases=s_alias(n_in, no),
        scratch_shapes=([pltpu.VMEM((tm, tn), F32)] if nk > 1 else []) + s_scr,
        compiler_params=_cparams(sem),
    )(a, b, *extras, *s_ops)
    return outs[0] if len(outs) == 1 else outs


def _row_spec(tr, d):
    return pl.BlockSpec((tr, d), lambda i: (i, 0))


def _vec_spec(d):
    return pl.BlockSpec((1, d), lambda i: (0, 0))


def _rms_fwd(name, x, w):
    T, D = x.shape
    tr = min(ROW_TILE, T)

    def body(x_ref, w_ref, o_ref):
        xf = x_ref[...]
        r = lax.rsqrt(jnp.mean(xf * xf, axis=-1, keepdims=True) + EPS)
        o_ref[...] = (xf * r * w_ref[...]).astype(BF16)

    return pl.pallas_call(
        body, name=name, grid=(T // tr,),
        in_specs=[_row_spec(tr, D), _vec_spec(D)], out_specs=_row_spec(tr, D),
        out_shape=jax.ShapeDtypeStruct((T, D), BF16),
        compiler_params=_cparams(("parallel",)),
    )(x, w)


def _rms_bwd(name, dy, h, w, dres, side=None):
    T, D = h.shape
    tr = min(ROW_TILE, T)

    def body(dy_ref, h_ref, w_ref, dres_ref, dh_ref, dhb_ref, dw_ref):
        @pl.when(pl.program_id(0) == 0)
        def _():
            dw_ref[...] = jnp.zeros_like(dw_ref)

        hf, dyv = h_ref[...], dy_ref[...]
        r = lax.rsqrt(jnp.mean(hf * hf, axis=-1, keepdims=True) + EPS)
        xhat = hf * r
        dw_ref[...] += jnp.sum(dyv * xhat, axis=0, keepdims=True)
        dxh = dyv * w_ref[...]
        dh = dres_ref[...] + r * (dxh - xhat * jnp.mean(dxh * xhat, axis=-1, keepdims=True))
        dh_ref[...] = dh
        dhb_ref[...] = dh.astype(BF16)

    return _call_with_side(
        body, name, (T // tr,),
        [_row_spec(tr, D), _row_spec(tr, D), _vec_spec(D), _row_spec(tr, D)],
        [_row_spec(tr, D), _row_spec(tr, D), _vec_spec(D)],
        [jax.ShapeDtypeStruct((T, D), F32), jax.ShapeDtypeStruct((T, D), BF16), jax.ShapeDtypeStruct((1, D), F32)],
        [], ("arbitrary",), (dy, h, w, dres), side)


def _loss_head(h2, target, w):
    T, D = h2.shape
    tr = min(ROW_TILE, T)

    def body(h_ref, t_ref, w_ref, loss_ref, dh_ref, dhb_ref, dw_ref):
        @pl.when(pl.program_id(0) == 0)
        def _():
            dw_ref[...] = jnp.zeros_like(dw_ref)
            loss_ref[...] = jnp.zeros_like(loss_ref)

        hf, wv = h_ref[...], w_ref[...]
        r = lax.rsqrt(jnp.mean(hf * hf, axis=-1, keepdims=True) + EPS)
        xhat = hf * r
        diff = xhat * wv - t_ref[...]
        loss_ref[...] += 0.5 * jnp.sum(jnp.mean(diff * diff, axis=-1, keepdims=True))
        dyv = diff * (1.0 / D)
        dw_ref[...] += jnp.sum(dyv * xhat, axis=0, keepdims=True)
        dxh = dyv * wv
        dh = r * (dxh - xhat * jnp.mean(dxh * xhat, axis=-1, keepdims=True))
        dh_ref[...] = dh
        dhb_ref[...] = dh.astype(BF16)

    return pl.pallas_call(
        body, name="loss_head", grid=(T // tr,),
        in_specs=[_row_spec(tr, D), _row_spec(tr, D), _vec_spec(D)],
        out_specs=[_vec_spec(LANE), _row_spec(tr, D), _row_spec(tr, D), _vec_spec(D)],
        out_shape=[jax.ShapeDtypeStruct((1, LANE), F32), jax.ShapeDtypeStruct((T, D), F32),
                   jax.ShapeDtypeStruct((T, D), BF16), jax.ShapeDtypeStruct((1, D), F32)],
        compiler_params=_cparams(("arbitrary",)),
    )(h2, target, w)


def _gate_tiles(T, D):
    goff = 4 * _hgw() + 3 * _atw()
    tc = min(1024, D)
    assert goff % tc == 0 and D % tc == 0
    return min(ROW_TILE, T), tc, goff // tc, D // tc


def _merge(z, pa, pb):
    T, D = pa.shape
    tr, tc, g0, nd = _gate_tiles(T, D)

    def body(ga_ref, gb_ref, pa_ref, pb_ref, o_ref):
        o_ref[...] = (_sigmoid(ga_ref[...]) * pa_ref[...] + _sigmoid(gb_ref[...]) * pb_ref[...]).astype(BF16)

    t = pl.BlockSpec((tr, tc), lambda i, j: (i, j))
    return pl.pallas_call(
        body, name="merge", grid=(T // tr, nd),
        in_specs=[pl.BlockSpec((tr, tc), lambda i, j: (i, g0 + j)),
                  pl.BlockSpec((tr, tc), lambda i, j: (i, g0 + nd + j)), t, t],
        out_specs=t, out_shape=jax.ShapeDtypeStruct((T, D), BF16),
        compiler_params=_cparams(("parallel", "parallel")),
    )(z, z, pa, pb)


def _dmerge(dm, z, pa, pb):
    T, D = pa.shape
    tr, tc, g0, nd = _gate_tiles(T, D)

    def body(dm_ref, ga_ref, gb_ref, pa_ref, pb_ref, dpa_ref, dpb_ref, dga_ref, dgb_ref):
        dmv = dm_ref[...]
        sa, sb = _sigmoid(ga_ref[...]), _sigmoid(gb_ref[...])
        dpa_ref[...] = (dmv * sa).astype(BF16)
        dpb_ref[...] = (dmv * sb).astype(BF16)
        dga_ref[...] = (dmv * pa_ref[...] * sa * (1.0 - sa)).astype(BF16)
        dgb_ref[...] = (dmv * pb_ref[...] * sb * (1.0 - sb)).astype(BF16)

    t = pl.BlockSpec((tr, tc), lambda i, j: (i, j))
    return pl.pallas_call(
        body, name="dmerge", grid=(T // tr, nd),
        in_specs=[t, pl.BlockSpec((tr, tc), lambda i, j: (i, g0 + j)),
                  pl.BlockSpec((tr, tc), lambda i, j: (i, g0 + nd + j)), t, t],
        out_specs=[t, t, t, t],
        out_shape=[jax.ShapeDtypeStruct((T, D), BF16)] * 4,
        compiler_params=_cparams(("parallel", "parallel")),
    )(dm, z, z, pa, pb)


def _hg_gates(xq, xf, lb):
    f = _sigmoid(xf)
    g = lb + (1.0 - lb) * f
    sq = _sigmoid(xq)
    return f, g, jnp.log(g), 1.0 - g, sq, xq * sq * (HG_D ** -0.5)


def _split2(x):
    hi = x.astype(BF16)
    return hi, (x - hi.astype(F32)).astype(BF16)


def _tri_sum(tri, x):
    hi, rest = x.astype(BF16), x - x.astype(BF16).astype(F32)
    mid, lo = _split2(rest)
    return _nn(tri, lo) + _nn(tri, mid) + _nn(tri, hi)


def _hg_decays(lg, tri_incl, rowi):
    b = _tri_sum(tri_incl, lg)
    b_last = jnp.sum(lg, axis=0, keepdims=True)
    b_mid = jnp.sum(jnp.where(rowi <= CHUNK // 2, lg, 0.0), axis=0, keepdims=True)
    return b, b_last, b_mid


HG_GROUP = 2


def _hg_in_specs(T):
    ng = HG_HEADS // HG_GROUP
    return [pl.BlockSpec((T, HG_GROUP * HG_D), lambda h, s=s: (0, s * ng + h)) for s in range(4)]


def _hg_fwd(z, lb_logits, hgw, side=None):
    T = z.shape[0]
    H, d, C, G = HG_HEADS, HG_D, CHUNK, HG_GROUP
    nc = T // C

    def body(hq_ref, hf_ref, hi_ref, hg_ref, lbl_ref, w_ref, ya_ref, o_ref, s_ref):
        lb_all = 1.0 / (1.0 + jnp.exp(lbl_ref[1:2, :] - lbl_ref[0:1, :]))
        wv = w_ref[...]
        row = lax.broadcasted_iota(jnp.int32, (C, C), 0)
        col = lax.broadcasted_iota(jnp.int32, (C, C), 1)
        tril = col <= row
        tri_incl = tril.astype(BF16)
        rowi = lax.broadcasted_iota(jnp.int32, (C, G * d), 0)
        lanes = [slice(hh * d, (hh + 1) * d) for hh in range(G)]
        per_head = lambda fn: jnp.concatenate([fn(hh, sl) for hh, sl in enumerate(lanes)], axis=1)
        wv_all = jnp.tile(wv, (1, G))

        def chunk(c, states):
            rows = pl.ds(pl.multiple_of(c * C, C), C)
            xq, xf, v, xg = hq_ref[rows, :], hf_ref[rows, :], hi_ref[rows, :], hg_ref[rows, :]
            _, _, lg, kk, _, q = _hg_gates(xq, xf, lb_all)
            b, b_last, b_mid = _hg_decays(lg, tri_incl, rowi)
            vb, qe = v.astype(BF16), (q * jnp.exp(b)).astype(BF16)
            qt = (q * jnp.exp(b - b_mid)).astype(BF16)
            kt = (kk * jnp.exp(jnp.minimum(b_mid - b, EXP_CLAMP))).astype(BF16)
            kd, e_last = (kk * jnp.exp(b_last - b)).astype(BF16), jnp.exp(b_last)
            for hh, st in enumerate(states):
                s_ref[hh, c] = st
            o = per_head(lambda hh, sl: _nt(qe[:, sl], states[hh].astype(BF16)))
            a = [jnp.where(tril, _nt(qt[:, sl], kt[:, sl]), 0.0).astype(BF16) for sl in lanes]
            o = o + per_head(lambda hh, sl: _nn(a[hh], vb[:, sl]))
            o_ref[rows, :] = o
            r = per_head(lambda hh, sl: jnp.broadcast_to(
                lax.rsqrt(jnp.mean(o[:, sl] * o[:, sl], axis=-1, keepdims=True) + EPS), (C, d)))
            ya_ref[rows, :] = (o * r * wv_all * (xg * _sigmoid(xg))).astype(BF16)
            return tuple(st * e_last[:, sl] + _tn(vb[:, sl], kd[:, sl]) for st, sl in zip(states, lanes))

        lax.fori_loop(0, nc, chunk, tuple(jnp.zeros((d, d), F32) for _ in range(G)))

    heads = pl.BlockSpec((T, G * d), lambda h: (0, h))
    return _call_with_side(
        body, "hg_fwd", (H // G,),
        _hg_in_specs(T) + [pl.BlockSpec((2, G * d), lambda h: (0, h)), pl.BlockSpec((1, d), lambda h: (0, 0))],
        [heads, heads, pl.BlockSpec((G, nc, d, d), lambda h: (h, 0, 0, 0))],
        [jax.ShapeDtypeStruct((T, H * d), BF16), jax.ShapeDtypeStruct((T, H * d), F32),
         jax.ShapeDtypeStruct((H, nc, d, d), F32)],
        [], ("parallel",), (z, z, z, z, lb_logits, hgw), side)


def _hg_bwd(z, o, dya, states, lb_logits, hgw, side=None):
    T = z.shape[0]
    H, d, C, G = HG_HEADS, HG_D, CHUNK, HG_GROUP
    nc = T // C
    scale = HG_D ** -0.5

    def body(hq_ref, hf_ref, hi_ref, hg_ref, o_ref, dy_ref, s_ref, lbl_ref, w_ref,
             dq_ref, df_ref, di_ref, dg_ref, dlbl_ref, dw_ref, acc_ref):
        lb_all = 1.0 / (1.0 + jnp.exp(lbl_ref[1:2, :] - lbl_ref[0:1, :]))
        wv = w_ref[...]
        row = lax.broadcasted_iota(jnp.int32, (C, C), 0)
        col = lax.broadcasted_iota(jnp.int32, (C, C), 1)
        tril = col <= row
        tri_incl = tril.astype(BF16)
        triu_incl = (col >= row).astype(BF16)
        rowi = lax.broadcasted_iota(jnp.int32, (C, G * d), 0)
        lanes = [slice(hh * d, (hh + 1) * d) for hh in range(G)]
        per_head = lambda fn: jnp.concatenate([fn(hh, sl) for hh, sl in enumerate(lanes)], axis=1)
        head_mean = lambda x: per_head(
            lambda hh, sl: jnp.broadcast_to(jnp.mean(x[:, sl], axis=-1, keepdims=True), (C, d)))
        wv_all = jnp.tile(wv, (1, G))
        lb = lb_all
        acc_ref[...] = jnp.zeros_like(acc_ref)

        @pl.when(pl.program_id(0) == 0)
        def _():
            dw_ref[...] = jnp.zeros_like(dw_ref)

        def chunk(i, carry):
            dsts, tail = carry
            c = nc - 1 - i
            rows = pl.ds(pl.multiple_of(c * C, C), C)
            xq, xf, v, xg = hq_ref[rows, :], hf_ref[rows, :], hi_ref[rows, :], hg_ref[rows, :]
            f, g, lg, kk, sq, q = _hg_gates(xq, xf, lb)
            b, b_last, b_mid = _hg_decays(lg, tri_incl, rowi)
            e_b, e_qm, e_km = jnp.exp(b), jnp.exp(b - b_mid), jnp.exp(jnp.minimum(b_mid - b, EXP_CLAMP))
            e_kl, e_last = jnp.exp(b_last - b), jnp.exp(b_last)
            ov, dy = o_ref[rows, :], dy_ref[rows, :]
            r = lax.rsqrt(head_mean(ov * ov) + EPS)
            xhat = ov * r
            sg = _sigmoid(xg)
            dxg = dy * xhat * wv_all * (sg * (1.0 + xg * (1.0 - sg)))
            dyn = dy * (xg * sg)
            acc_ref[0:1, :] += jnp.sum(dyn * xhat, axis=0, keepdims=True)
            dxh = dyn * wv_all
            dof = r * (dxh - xhat * head_mean(dxh * xhat))
            do, vb = dof.astype(BF16), v.astype(BF16)
            qe, kd, qt, kt = (q * e_b).astype(BF16), (kk * e_kl).astype(BF16), (q * e_qm).astype(BF16), (kk * e_km).astype(BF16)
            pm = [jnp.where(tril, _nt(do[:, sl], vb[:, sl]), 0.0).astype(BF16) for sl in lanes]
            am = [jnp.where(tril, _nt(qt[:, sl], kt[:, sl]), 0.0).astype(BF16) for sl in lanes]
            st = [_split2(s_ref[hh, c]) for hh in range(G)]
            ds = [_split2(x) for x in dsts]
            dq_state = per_head(lambda hh, sl: _nn(do[:, sl], st[hh][1]) + _nn(do[:, sl], st[hh][0]))
            dk_state = per_head(lambda hh, sl: _nn(vb[:, sl], ds[hh][1]) + _nn(vb[:, sl], ds[hh][0]))
            dq_intra = per_head(lambda hh, sl: _nn(pm[hh], kt[:, sl]))
            dk_intra = per_head(lambda hh, sl: _tn(pm[hh], qt[:, sl]))
            dv = per_head(lambda hh, sl: _tn(am[hh], do[:, sl]) + _nt(kd[:, sl], ds[hh][0]))
            new_dsts = tuple(x * e_last[:, sl] + _tn(do[:, sl], qe[:, sl]) for x, sl in zip(dsts, lanes))
            dq = dq_state * e_b + dq_intra * e_qm
            dk = dk_intra * e_km + dk_state * e_kl
            db = (qe.astype(F32) * dq_state + qt.astype(F32) * dq_intra
                  - kt.astype(F32) * dk_intra - kd.astype(F32) * dk_state)
            dlg = _tri_sum(triu_incl, db) + tail
            dgate = dlg / g - dk
            acc_ref[1:2, :] += jnp.sum(dgate * (1.0 - f), axis=0, keepdims=True)
            dq_ref[rows, :] = (dq * scale * (sq * (1.0 + xq * (1.0 - sq)))).astype(BF16)
            df_ref[rows, :] = (dgate * (1.0 - lb) * f * (1.0 - f)).astype(BF16)
            di_ref[rows, :] = dv.astype(BF16)
            dg_ref[rows, :] = dxg.astype(BF16)
            return new_dsts, tail + jnp.sum(db, axis=0, keepdims=True)

        lax.fori_loop(0, nc, chunk, (tuple(jnp.zeros((d, d), F32) for _ in range(G)), jnp.zeros((1, G * d), F32)))
        dw_ref[...] += functools.reduce(lambda p, q: p + q, [acc_ref[0:1, sl] for sl in lanes])
        dl0 = acc_ref[1:2, :] * lb_all * (1.0 - lb_all)
        dlbl_ref[0:1, :] = dl0
        dlbl_ref[1:2, :] = -dl0

    heads = pl.BlockSpec((T, G * d), lambda h: (0, h))
    logits = pl.BlockSpec((2, G * d), lambda h: (0, h))
    return _call_with_side(
        body, "hg_bwd", (H // G,),
        _hg_in_specs(T) + [heads, heads, pl.BlockSpec((G, nc, d, d), lambda h: (h, 0, 0, 0)), logits,
                           pl.BlockSpec((1, d), lambda h: (0, 0))],
        [heads, heads, heads, heads, logits, pl.BlockSpec((1, d), lambda h: (0, 0))],
        [jax.ShapeDtypeStruct((T, H * d), BF16)] * 4 + [jax.ShapeDtypeStruct((2, H * d), F32),
                                                        jax.ShapeDtypeStruct((1, d), F32)],
        [pltpu.VMEM((8, G * d), F32)], ("arbitrary",), (z, z, z, z, o, dya, states, lb_logits, hgw), side)


def _at_dims():
    pad = LEFT * CHUNK
    return pad, QB + pad, AT_HEADS * AT_DH // LANE, 4 * _hgw() // LANE


def _rel_of_period():
    pad, W, _, _ = _at_dims()
    n = jnp.arange(QB + W)
    return jnp.clip(pad - jnp.where(n < W, n, n - (QB + W)), -REL_CLIP, REL_CLIP) + REL_CLIP


def _bias_window(rel_bias):
    pad, W, _, _ = _at_dims()
    H, P = rel_bias.shape[0], QB + W
    per = rel_bias[:, _rel_of_period()]
    win = jnp.tile(per, (1, QB))[:, :QB * (P - 1)].reshape(H, QB, P - 1)[:, :, :W]
    t = jnp.arange(QB)[:, None]
    j = jnp.arange(W)[None, :]
    ok = (j // CHUNK >= t // CHUNK) & (j // CHUNK <= t // CHUNK + LEFT)
    return jnp.where(ok[None], win, NEG)


def _bias_window_grad(dbw):
    pad, W, _, _ = _at_dims()
    H, P = dbw.shape[0], QB + W
    flat = jnp.pad(dbw, ((0, 0), (0, 0), (0, P - 1 - W))).reshape(H, QB * (P - 1))
    per = jnp.pad(flat, ((0, 0), (0, QB))).reshape(H, QB, P).sum(axis=1)
    onehot = _rel_of_period()[:, None] == jnp.arange(2 * REL_CLIP + 1)[None, :]
    return jnp.dot(per, onehot.astype(F32), precision=HIGHEST)


def _at_softmax(q_halves, kw, bias_ref, valid):
    s = [_nt(qh, kw) * (AT_DH ** -0.5) + bias_ref[hh] for hh, qh in enumerate(q_halves)]
    s = [jnp.where(valid, x, NEG) for x in s]
    e = [jnp.exp(x - jnp.max(x, axis=-1, keepdims=True)) for x in s]
    return [x / jnp.sum(x, axis=-1, keepdims=True) for x in e]


def _at_fwd(z, bias_win, side=None):
    T = z.shape[0]
    pad, W, HP, c0 = _at_dims()
    nq = T // QB

    def body(q_ref, k_ref, v_ref, bias_ref, o_ref, kpad, vpad):
        qi = pl.program_id(1)

        @pl.when(qi == 0)
        def _():
            kpad[0:pad, :] = jnp.zeros((pad, LANE), BF16)
            vpad[0:pad, :] = jnp.zeros((pad, LANE), BF16)
            kpad[pad:, :] = k_ref[...].astype(BF16)
            vpad[pad:, :] = v_ref[...].astype(BF16)

        win = pl.ds(pl.multiple_of(qi * QB, QB), W)
        kw, vw = kpad[win, :], vpad[win, :]
        q = q_ref[...]
        lane = lax.broadcasted_iota(jnp.int32, (QB, LANE), 1)
        first = lane < AT_DH
        valid = lax.broadcasted_iota(jnp.int32, (QB, W), 1) + qi * QB >= pad
        qs = [jnp.where(first, q, 0.0).astype(BF16), jnp.where(first, 0.0, q).astype(BF16)]
        oa, ob = [_nn(p.astype(BF16), vw) for p in _at_softmax(qs, kw, bias_ref, valid)]
        o_ref[...] = jnp.where(first, oa, ob).astype(BF16)

    full = lambda s: pl.BlockSpec((T, LANE), lambda hp, qi, s=s: (0, c0 + s * HP + hp))
    return _call_with_side(
        body, "at_fwd", (HP, nq),
        [pl.BlockSpec((QB, LANE), lambda hp, qi: (qi, c0 + hp)), full(1), full(2),
         pl.BlockSpec((2, QB, W), lambda hp, qi: (hp, 0, 0))],
        [pl.BlockSpec((QB, LANE), lambda hp, qi: (qi, hp))],
        [jax.ShapeDtypeStruct((T, HP * LANE), BF16)],
        [pltpu.VMEM((T + pad, LANE), BF16)] * 2, ("parallel", "arbitrary"), (z, z, z, bias_win), side)


def _at_bwd(z, dyb, bias_win, side=None):
    T = z.shape[0]
    pad, W, HP, c0 = _at_dims()
    nq = T // QB
    scale = AT_DH ** -0.5

    def body(q_ref, k_ref, v_ref, do_ref, bias_ref, dq_ref, dk_ref, dv_ref, dbias_ref, kpad, vpad, dkpad, dvpad):
        qi = pl.program_id(1)

        @pl.when(qi == 0)
        def _():
            kpad[0:pad, :] = jnp.zeros((pad, LANE), BF16)
            vpad[0:pad, :] = jnp.zeros((pad, LANE), BF16)
            kpad[pad:, :] = k_ref[...].astype(BF16)
            vpad[pad:, :] = v_ref[...].astype(BF16)
            dkpad[...] = jnp.zeros_like(dkpad)
            dvpad[...] = jnp.zeros_like(dvpad)
            dbias_ref[...] = jnp.zeros_like(dbias_ref)

        win = pl.ds(pl.multiple_of(qi * QB, QB), W)
        kw, vw = kpad[win, :], vpad[win, :]
        q, do = q_ref[...], do_ref[...]
        lane = lax.broadcasted_iota(jnp.int32, (QB, LANE), 1)
        first = lane < AT_DH
        valid = lax.broadcasted_iota(jnp.int32, (QB, W), 1) + qi * QB >= pad

        qs = [jnp.where(first, q, 0.0).astype(BF16), jnp.where(first, 0.0, q).astype(BF16)]
        dos = [jnp.where(first, do, 0.0).astype(BF16), jnp.where(first, 0.0, do).astype(BF16)]
        ps = _at_softmax(qs, kw, bias_ref, valid)
        dps = [_nt(doh, vw) for doh in dos]
        dss = [p * (dp - jnp.sum(p * dp, axis=-1, keepdims=True)) for p, dp in zip(ps, dps)]
        for hh, ds in enumerate(dss):
            dbias_ref[hh] += ds
        dsb = [(ds * scale).astype(BF16) for ds in dss]
        pb = [p.astype(BF16) for p in ps]
        dqa, dqb = [_nn(ds, kw) for ds in dsb]
        dka, dkb = [_tn(ds, qh) for ds, qh in zip(dsb, qs)]
        dva, dvb = [_tn(p, doh) for p, doh in zip(pb, dos)]
        dq_ref[...] = jnp.where(first, dqa, dqb).astype(BF16)
        dkpad[win, :] += dka + dkb
        dvpad[win, :] += dva + dvb

        @pl.when(qi == nq - 1)
        def _():
            dk_ref[...] = dkpad[pad:, :].astype(BF16)
            dv_ref[...] = dvpad[pad:, :].astype(BF16)

    full = lambda s: pl.BlockSpec((T, LANE), lambda hp, qi, s=s: (0, c0 + s * HP + hp))
    blk = pl.BlockSpec((QB, LANE), lambda hp, qi: (qi, hp))
    col = pl.BlockSpec((T, LANE), lambda hp, qi: (0, hp))
    bw = pl.BlockSpec((2, QB, W), lambda hp, qi: (hp, 0, 0))
    return _call_with_side(
        body, "at_bwd", (HP, nq),
        [pl.BlockSpec((QB, LANE), lambda hp, qi: (qi, c0 + hp)), full(1), full(2), blk, bw],
        [blk, col, col, bw],
        [jax.ShapeDtypeStruct((T, HP * LANE), BF16)] * 3 + [jax.ShapeDtypeStruct(bias_win.shape, F32)],
        [pltpu.VMEM((T + pad, LANE), BF16)] * 2 + [pltpu.VMEM((T + pad, LANE), F32)] * 2,
        ("parallel", "arbitrary"), (z, z, z, dyb, bias_win), side)


def _piece_tiles(name, full_shape):
    pr, pc = _piece_shape(name, full_shape)
    tr = min(ROW_TILE, pr)
    assert pr % tr == 0
    nt = pr // tr
    if name in ROW_SHARDED:
        return tr, nt, lambda q, half, i: ((2 * q + half) * nt + i, 0)
    return tr, nt, lambda q, half, i: (half * nt + i, q)


def _cast_into_full(name, wq, place):
    full = _full_shape(name, wq.shape)
    pc = wq.shape[1]
    tr, nt, at = _piece_tiles(name, full)

    def body(place_ref, w_ref, o_ref):
        o_ref[...] = w_ref[...].astype(BF16)

    return pl.pallas_call(
        body, name="cast_" + name,
        grid_spec=pltpu.PrefetchScalarGridSpec(
            num_scalar_prefetch=1, grid=(2, nt),
            in_specs=[pl.BlockSpec((tr, pc), lambda h, i, s: (h * nt + i, 0))],
            out_specs=pl.BlockSpec((tr, pc), lambda h, i, s: at(s[0], h, i))),
        out_shape=jax.ShapeDtypeStruct(full, BF16),
        compiler_params=_cparams(("parallel", "parallel")),
    )(place, wq)


def _chip_sum(name, grad, theirs, place):
    pr, pc = theirs.shape[1:]
    tr, nt, at = _piece_tiles(name, grad.shape)

    def body(place_ref, g_ref, t_ref, o_ref):
        o_ref[...] = (g_ref[...].astype(F32) + t_ref[...].astype(F32)).astype(BF16)

    piece = pl.BlockSpec((None, tr, pc), lambda q, i, s: (q, i, 0))
    return pl.pallas_call(
        body, name="chip_sum_" + name,
        grid_spec=pltpu.PrefetchScalarGridSpec(
            num_scalar_prefetch=1, grid=(4, nt),
            in_specs=[pl.BlockSpec((tr, pc), lambda q, i, s: at(q, s[1], i)), piece], out_specs=piece),
        out_shape=jax.ShapeDtypeStruct(theirs.shape, BF16),
        compiler_params=_cparams(("parallel", "parallel")),
    )(place, grad, theirs)


def _piece_sum(name, chip_sums, got, place):
    pr, pc = chip_sums.shape[1:]
    tr = min(ROW_TILE, pr)

    def body(place_ref, own_ref, got_ref, o_ref):
        o_ref[...] = (own_ref[...].astype(F32) + got_ref[0].astype(F32) + got_ref[1].astype(F32)
                      + got_ref[2].astype(F32))

    return pl.pallas_call(
        body, name="piece_sum_" + name,
        grid_spec=pltpu.PrefetchScalarGridSpec(
            num_scalar_prefetch=1, grid=(pr // tr,),
            in_specs=[pl.BlockSpec((None, tr, pc), lambda i, s: (s[0], i, 0)),
                      pl.BlockSpec((3, tr, pc), lambda i, s: (0, i, 0))],
            out_specs=pl.BlockSpec((tr, pc), lambda i, s: (i, 0))),
        out_shape=jax.ShapeDtypeStruct((pr, pc), F32),
        compiler_params=_cparams(("parallel",)),
    )(place, chip_sums, got)


def _adam_quarter(name, w, m, v, g_mine, g_sib, place, side=None):
    pr, pc = g_mine.shape
    tr = min(ROW_TILE // 2, pr)
    nt = pr // tr

    def body(place_ref, w_ref, m_ref, v_ref, gm_ref, gs_ref, go_ref, d_ref, mo_ref, vo_ref):
        g = jnp.where(pl.program_id(0) == place_ref[1], gm_ref[...], gs_ref[...])
        delta, mn, vn = _adam_math(w_ref[...], g, m_ref[...], v_ref[...])
        go_ref[...] = g
        d_ref[...] = delta
        mo_ref[...] = mn
        vo_ref[...] = vn

    quarter = pl.BlockSpec((tr, pc), lambda h, i, s: (h * nt + i, 0))
    mine = pl.BlockSpec((tr, pc), lambda h, i, s: (jnp.where(h == s[1], i, 0), 0))
    sib = pl.BlockSpec((tr, pc), lambda h, i, s: (jnp.where(h == s[1], 0, i), 0))
    return _call_with_side(
        body, "adam_" + name, (2, nt), [quarter, quarter, quarter, mine, sib], [quarter] * 4,
        [jax.ShapeDtypeStruct(w.shape, F32)] * 4, [], ("parallel", "parallel"),
        (place, w, m, v, g_mine, g_sib), side, n_prefetch=1)


def _adam_math(w, g, m, v):
    m = ADAM_B1 * m + (1.0 - ADAM_B1) * g
    v = ADAM_B2 * v + (1.0 - ADAM_B2) * (g * g)
    m_hat = m / (1.0 - ADAM_B1 ** ADAM_STEP)
    v_hat = v / (1.0 - ADAM_B2 ** ADAM_STEP)
    return -ADAM_LR * (m_hat / (jnp.sqrt(v_hat) + ADAM_EPS) + ADAM_WD * w), m, v


WEIGHTS = ("w_in", "w_branch_a", "w_branch_b", "w_out", "w_up", "w_down")
ROW_SHARDED = ("w_out", "w_down")
ANY = pl.BlockSpec(memory_space=pl.ANY)
MESH = pl.DeviceIdType.MESH


def _place():
    x, y, c = lax.axis_index("x"), lax.axis_index("y"), lax.axis_index("c")
    chips = [(1 - x, y), (x, 1 - y), (1 - x, 1 - y)]
    return x, y, c, 2 * x + y, chips, [2 * cx + cy for cx, cy in chips]


def _piece(full_ref, name, q, half):
    K, N = full_ref.shape
    if name in ROW_SHARDED:
        rows = K // 8
        return full_ref.at[pl.ds(q * (2 * rows) + half * rows, rows), :]
    return full_ref.at[pl.ds(half * (K // 2), K // 2), pl.ds(q * (N // 4), N // 4)]


def _piece_shape(name, full_shape):
    K, N = full_shape
    return (K // 8, N) if name in ROW_SHARDED else (K // 2, N // 4)


def _full_shape(name, quarter_shape):
    Kq, Nq = quarter_shape
    return (4 * Kq, Nq) if name in ROW_SHARDED else (Kq, 4 * Nq)


def _remote(src, dst, send_sem, recv_sem, device):
    return pltpu.make_async_remote_copy(src_ref=src, dst_ref=dst, send_sem=send_sem, recv_sem=recv_sem,
                                        device_id=device, device_id_type=MESH)


def _z_part(u1, w_in, z_prev, place, k0, count, side=None):
    T, K = u1.shape
    N = w_in.shape[1]
    nq = N // 4
    tn = nq // 2 if (nq // 2) % LANE == 0 else nq
    tm = min(MM_TM, T)
    per = nq // tn
    col = lambda g, j, s: (s[0] ^ (k0 + g)) * per + j
    ins = [pl.BlockSpec((tm, K), lambda g, i, j, s: (i, 0)), pl.BlockSpec((K, tn), lambda g, i, j, s: (0, col(g, j, s)))]
    operands = [place, u1, w_in]
    if z_prev is not None:
        ins.append(ANY)
        operands.append(z_prev)

    def body(place_ref, a_ref, b_ref, *rest):
        rest[-1][...] = _nn(a_ref[...], b_ref[...])

    return _call_with_side(
        body, "z_part_%d" % k0, (count, T // tm, per), ins,
        [pl.BlockSpec((tm, tn), lambda g, i, j, s: (i, col(g, j, s)))], [jax.ShapeDtypeStruct((T, N), F32)],
        [], ("parallel",) * 3, tuple(operands), side, n_prefetch=1, aliases={} if z_prev is None else {2: 0},
        borrow={0: 1} if side is not None and side.aliased and side.aliased[0] is w_in else None)


def _rows(ref, span):
    return ref if span is None else ref.at[pl.ds(span[0], span[1]), :]


def _ici_near(names, fulls, rows=None):
    rows = rows or [None] * len(names)

    def build(reads, aliased, fresh, send_sems, recv_sems, off=0):
        _, _, c, p, chips, _ = _place()
        out = []
        for i, (ref, name) in enumerate(zip(aliased, names)):
            mine = _rows(_piece(ref, name, p, c), rows[i])
            for j, chip in enumerate(chips[:2]):
                k = off + 2 * i + j
                out.append(_remote(mine, mine, send_sems.at[k], recv_sems.at[k], (*chip, c)))
        return out

    return _Side(build, 2 * len(names), aliased=fulls)


def _ici_far(names, fulls, rows=None):
    rows = rows or [None] * len(names)

    def build(reads, aliased, fresh, send_sems, recv_sems, off=0):
        x, y, c, _, _, chip_ids = _place()
        south = c == 0
        src_chip = jnp.where(south, chip_ids[0], chip_ids[1])
        target = (jnp.where(south, x, 1 - x), jnp.where(south, 1 - y, y), c)
        out = []
        for i, (ref, name) in enumerate(zip(aliased, names)):
            landed = _rows(_piece(ref, name, src_chip, c), rows[i])
            out.append(_remote(landed, landed, send_sems.at[off + i], recv_sems.at[off + i], target))
        return out

    return _Side(build, len(names), aliased=fulls)


def _d2d_gather(names, fulls, which=(0, 1, 2)):
    def build(reads, aliased, fresh, send_sems, recv_sems, off=0):
        x, y, c, _, _, chip_ids = _place()
        out = []
        for i, (ref, name) in enumerate(zip(aliased, names)):
            for n, j in enumerate(which):
                landed, k = _piece(ref, name, chip_ids[j], c), off + len(which) * i + n
                out.append(_remote(landed, landed, send_sems.at[k], recv_sems.at[k], (x, y, 1 - c)))
        return out

    return _Side(build, len(which) * len(names), aliased=fulls)


def _sib_send(names, grads):
    def build(reads, aliased, fresh, send_sems, recv_sems, off=0):
        x, y, c, _, _, _ = _place()
        out = []
        for i, name in enumerate(names):
            for q in range(4):
                k = off + 4 * i + q
                out.append(_remote(_piece(reads[i], name, q, 1 - c), fresh[i].at[q], send_sems.at[k], recv_sems.at[k],
                                   (x, y, 1 - c)))
        return out

    shapes = [jax.ShapeDtypeStruct((4,) + _piece_shape(name, g.shape), BF16) for name, g in zip(names, grads)]
    return _Side(build, 4 * len(names), reads=grads, fresh=shapes)


def _chip_exchange(chip_sums, rows=None, got=None):
    rows = rows or [None] * len(chip_sums)

    def build(reads, aliased, fresh, send_sems, recv_sems, off=0):
        _, _, c, _, chips, chip_ids = _place()
        out = []
        for i in range(len(chip_sums)):
            for j, (chip, cid) in enumerate(zip(chips, chip_ids)):
                k = off + 3 * i + j
                out.append(_remote(_rows(reads[i].at[cid], rows[i]), _rows((aliased or fresh)[i].at[j], rows[i]),
                                   send_sems.at[k], recv_sems.at[k], (*chip, c)))
        return out

    if got is not None:
        return _Side(build, 3 * len(chip_sums), reads=chip_sums, aliased=got)
    shapes = [jax.ShapeDtypeStruct((3,) + s.shape[1:], BF16) for s in chip_sums]
    return _Side(build, 3 * len(chip_sums), reads=chip_sums, fresh=shapes)


HBM = pl.BlockSpec(memory_space=pltpu.HBM)
SEM = pl.BlockSpec(memory_space=pltpu.SEMAPHORE)


def _exchange_copies(s_ref, land_ref, send_sems, recv_sems):
    _, _, c, _, chips, chip_ids = _place()
    return [_remote(s_ref.at[cid], land_ref.at[j], send_sems.at[j], recv_sems.at[j], (*chip, c))
            for j, (chip, cid) in enumerate(zip(chips, chip_ids))]


def _exchange_start(name, chip_sum):
    def body(s_ref, land_ref, send_sems, recv_sems, s_thru, land_thru, token):
        for cp in _exchange_copies(s_ref, land_ref, send_sems, recv_sems):
            cp.start()
        token[...] = jnp.zeros_like(token)

    land = jax.ShapeDtypeStruct((3,) + chip_sum.shape[1:], chip_sum.dtype)
    return pl.pallas_call(
        body, name="exchange_start_" + name,
        out_shape=(pltpu.SemaphoreType.DMA((3,)), pltpu.SemaphoreType.DMA((3,)),
                   pltpu.HBM(chip_sum.shape, chip_sum.dtype), pltpu.HBM(land.shape, land.dtype),
                   jax.ShapeDtypeStruct((8, LANE), F32)),
        in_specs=(HBM, HBM), out_specs=(SEM, SEM, HBM, HBM, pl.BlockSpec(memory_space=pltpu.VMEM)),
        input_output_aliases={0: 2, 1: 3},
        compiler_params=pltpu.CompilerParams(has_side_effects=pltpu.SideEffectType.DATAFLOW_SIDE_EFFECTING),
    )(pltpu.with_memory_space_constraint(chip_sum, pltpu.HBM),
      pltpu.with_memory_space_constraint(lax.empty(land.shape, land.dtype), pltpu.HBM))


def _exchange_wait(name, flight, after):
    send_sems, recv_sems, s_thru, land_thru, _ = flight

    def body(s_ref, land_ref, send_sems, recv_sems, after_ref, s_out, land_out):
        for cp in _exchange_copies(s_ref, land_ref, send_sems, recv_sems):
            cp.wait_send()
            cp.wait_recv()

    return pl.pallas_call(
        body, name="exchange_wait_" + name,
        out_shape=(pltpu.HBM(s_thru.shape, s_thru.dtype), pltpu.HBM(land_thru.shape, land_thru.dtype)),
        in_specs=(HBM, HBM, SEM, SEM, ANY), out_specs=(HBM, HBM), input_output_aliases={0: 0, 1: 1},
        compiler_params=pltpu.CompilerParams(has_side_effects=pltpu.SideEffectType.DATAFLOW_SIDE_EFFECTING),
    )(s_thru, land_thru, send_sems, recv_sems, after)


def _sib_share(halves):
    def build(reads, aliased, fresh, send_sems, recv_sems, off=0):
        x, y, c, _, _, _ = _place()
        return [_remote(reads[i], fresh[i], send_sems.at[off + i], recv_sems.at[off + i], (x, y, 1 - c))
                for i in range(len(halves))]

    return _Side(build, len(halves), reads=halves, fresh=[jax.ShapeDtypeStruct(h.shape, F32) for h in halves])


def _join(a, b):
    def build(reads, aliased, fresh, send_sems, recv_sems, off=0):
        ra, aa, fa = len(a.reads), len(a.aliased), len(a.fresh)
        return (a.build(reads[:ra], aliased[:aa], fresh[:fa], send_sems, recv_sems, off)
                + b.build(reads[ra:], aliased[aa:], fresh[fa:], send_sems, recv_sems, off + a.nsem))

    return _Side(build, a.nsem + b.nsem, a.reads + b.reads, a.aliased + b.aliased, a.fresh + b.fresh)


def _run_side(name, side):
    nr, na = len(side.reads), len(side.aliased)

    def body(*refs):
        n_in, n_out = nr + na, na + len(side.fresh)
        outs = refs[n_in:n_in + n_out]
        copies = side.build(refs[:nr], outs[:na], outs[na:], *refs[-2:])
        for cp in copies:
            cp.start()
        for cp in copies:
            cp.wait()

    return pl.pallas_call(
        body, name=name, in_specs=side.in_specs(), out_specs=side.out_specs(), out_shape=side.out_shape(),
        input_output_aliases=side.aliases(0, 0), scratch_shapes=side.scratch(),
    )(*side.operands())


def _small_allreduce_adam(gpart, w, m, v, after):
    R = gpart.shape[0]

    def body(g_ref, w_ref, m_ref, v_ref, after_ref, go_ref, d_ref, mo_ref, vo_ref, buf, send_sems, recv_sems):
        x, y, c = lax.axis_index("x"), lax.axis_index("y"), lax.axis_index("c")
        me = 4 * x + 2 * y + c
        buf[me] = g_ref[...]
        copies = []
        for k in range(1, 8):
            fx, fy, fc = (k >> 2) & 1, (k >> 1) & 1, k & 1
            peer = (1 - x if fx else x, 1 - y if fy else y, 1 - c if fc else c)
            cp = _remote(g_ref, buf.at[me], send_sems.at[k - 1], recv_sems.at[k - 1], peer)
            cp.start()
            copies.append((cp, 4 * peer[0] + 2 * peer[1] + peer[2]))
        for k, (cp, pid) in enumerate(copies):
            _remote(g_ref, buf.at[pid], send_sems.at[k], recv_sems.at[k], (x, y, c)).wait_recv()
        for cp, _ in copies:
            cp.wait_send()
        g = buf[0]
        for d in range(1, 8):
            g = g + buf[d]
        delta, mn, vn = _adam_math(w_ref[...], g, m_ref[...], v_ref[...])
        go_ref[...] = g
        d_ref[...] = delta
        mo_ref[...] = mn
        vo_ref[...] = vn

    vm = pl.BlockSpec(memory_space=pltpu.VMEM)
    return pl.pallas_call(
        body, name="small_allreduce_adam",
        in_specs=[vm] * 4 + [ANY], out_specs=[vm] * 4,
        out_shape=[jax.ShapeDtypeStruct((R, LANE), F32)] * 4,
        scratch_shapes=[pltpu.VMEM((8, R, LANE), F32), pltpu.SemaphoreType.DMA((7,)), pltpu.SemaphoreType.DMA((7,))],
    )(gpart, w, m, v, after)


def _pack(arrs):
    flat = jnp.concatenate([a.reshape(-1).astype(F32) for a in arrs])
    rows = -(-flat.shape[0] // (8 * LANE)) * 8
    return jnp.pad(flat, (0, rows * LANE - flat.shape[0])).reshape(rows, LANE)


def _unpack(packed, like):
    flat, out, off = packed.reshape(-1), [], 0
    for a in like:
        out.append(flat[off:off + a.size].reshape(a.shape))
        off += a.size
    return out


def kernel(x, w_in, lb_logits, hg_norm_w, rel_bias, w_branch_a, w_branch_b, w_out, norm_mix_w, norm_mlp_w, w_up, w_down, norm_final_w, loss_target, m_w_in, m_lb_logits, m_hg_norm_w, m_rel_bias, m_w_branch_a, m_w_branch_b, m_w_out, m_norm_mix_w, m_norm_mlp_w, m_w_up, m_w_down, m_norm_final_w, v_w_in, v_lb_logits, v_hg_norm_w, v_rel_bias, v_w_branch_a, v_w_branch_b, v_w_out, v_norm_mix_w, v_norm_mlp_w, v_w_up, v_w_down, v_norm_final_w):
    T, D = x.shape[1], x.shape[2]
    x2, tgt = x.reshape(T, D), loss_target.reshape(T, D)
    big = dict(w_in=(w_in, m_w_in, v_w_in), w_branch_a=(w_branch_a, m_w_branch_a, v_w_branch_a),
               w_branch_b=(w_branch_b, m_w_branch_b, v_w_branch_b), w_out=(w_out, m_w_out, v_w_out),
               w_up=(w_up, m_w_up, v_w_up), w_down=(w_down, m_w_down, v_w_down))
    big = {k: tuple(a[0] for a in v) for k, v in big.items()}
    nfw = norm_final_w.reshape(1, D)

    place = jnp.stack([2 * lax.axis_index("x") + lax.axis_index("y"), lax.axis_index("c")]).astype(jnp.int32)
    Wf = {name: _cast_into_full(name, big[name][0], place) for name in WEIGHTS}
    small3 = ["w_branch_a", "w_branch_b", "w_out"]

    def span(name, lo, hi):
        pr = _piece_shape(name, Wf[name].shape)[0]
        return (pr * lo // 16, pr * (hi - lo) // 16)

    ab = ["w_branch_a", "w_branch_b"]
    u1 = _rms_fwd("norm_mix", x2, norm_mix_w)
    z, Wf["w_in"] = _z_part(u1, Wf["w_in"], None, place, 0, 1, side=_ici_near(["w_in"], [Wf["w_in"]]))
    (Wf["w_in"],) = _run_side("pass_w_in_near", _d2d_gather(["w_in"], [Wf["w_in"]], which=(0, 1)))
    z, Wf["w_in"] = _z_part(u1, Wf["w_in"], z, place, 1, 2, side=_ici_far(["w_in"], [Wf["w_in"]]))
    (Wf["w_in"],) = _run_side("pass_w_in_far", _d2d_gather(["w_in"], [Wf["w_in"]], which=(2,)))
    z, *moved = _z_part(u1, Wf["w_in"], z, place, 3, 1, side=_ici_near(ab, [Wf[n] for n in ab]))
    Wf.update(zip(ab, moved))
    ya, o_hg, states, *moved = _hg_fwd(
        z, lb_logits, hg_norm_w,
        side=_join(_ici_near(["w_out", "w_up", "w_down"], [Wf[n] for n in ("w_out", "w_up", "w_down")],
                             rows=[None, None, span("w_down", 0, 6)]),
                   _ici_far(ab, [Wf[n] for n in ab])))
    Wf.update(zip(["w_out", "w_up", "w_down"] + ab, moved))
    bias_win = _bias_window(rel_bias[0])
    yb, *moved = _at_fwd(
        z, bias_win,
        side=_join(_join(_ici_far(["w_out", "w_up"], [Wf["w_out"], Wf["w_up"]]),
                         _ici_near(["w_down"], [Wf["w_down"]], rows=[span("w_down", 6, 16)])),
                   _d2d_gather(ab, [Wf[n] for n in ab])))
    Wf.update(zip(["w_out", "w_up", "w_down"] + ab, moved))
    pa, Wf["w_out"] = _mm("branch_a", ya, Wf["w_branch_a"], "nn", [F32], side=_d2d_gather(["w_out"], [Wf["w_out"]]))
    pb = _mm("branch_b", yb, Wf["w_branch_b"], "nn", [F32])
    merged = _merge(z, pa, pb)
    add = lambda acc, res: (acc + res,)
    h1, Wf["w_up"] = _mm("out_proj", merged, Wf["w_out"], "nn", [F32], extras=[x2], epilogue=add,
                         side=_d2d_gather(["w_up"], [Wf["w_up"]]))
    u2 = _rms_fwd("norm_mlp", h1, norm_mlp_w)
    relu2 = lambda acc: (acc, jnp.square(jnp.maximum(acc, 0.0)))
    a_pre, act, Wf["w_down"] = _mm("mlp_up", u2, Wf["w_up"], "nn", [F32, BF16], epilogue=relu2,
                                   side=_ici_far(["w_down"], [Wf["w_down"]]))
    (Wf["w_down"],) = _run_side("pass_w_down", _d2d_gather(["w_down"], [Wf["w_down"]]))
    h2 = _mm("mlp_down", act, Wf["w_down"], "nn", [F32], extras=[h1], epilogue=add)
    loss_part, dh2, dh2b, d_nf = _loss_head(h2, tgt, nfw)

    drelu2 = lambda acc, a: (acc * (2.0 * jnp.maximum(a, 0.0)),)
    da = _mm("d_act", dh2b, Wf["w_down"], "nt", [BF16], extras=[a_pre], epilogue=drelu2)
    G = {}
    G["w_down"] = _mm("g_w_down", act, dh2b, "tn", [BF16])
    G["w_up"] = _mm("g_w_up", u2, da, "tn", [BF16])
    T_, S_, GOT = {}, {}, {}
    du2, T_["w_down"], T_["w_up"] = _mm("d_u2", da, Wf["w_up"], "nt", [F32],
                                        side=_sib_send(["w_down", "w_up"], [G["w_down"], G["w_up"]]))
    flights = {n: _exchange_start(n, _chip_sum(n, G[n], T_[n], place)) for n in ("w_down", "w_up")}
    dh1, dh1b, d_nmlp = _rms_bwd("norm_mlp_bwd", du2, h1, norm_mlp_w, dh2,
                                 side=_after(flights["w_down"][-1], flights["w_up"][-1]))
    dmerged = _mm("d_merged", dh1b, Wf["w_out"], "nt", [F32])
    G["w_out"] = _mm("g_w_out", merged, dh1b, "tn", [BF16])
    dpa, dpb, dz_ga, dz_gb = _dmerge(dmerged, z, pa, pb)
    dya = _mm("d_ya", dpa, Wf["w_branch_a"], "nt", [F32])
    dyb = _mm("d_yb", dpb, Wf["w_branch_b"], "nt", [F32])
    G["w_branch_a"] = _mm("g_w_a", ya, dpa, "tn", [BF16])
    G["w_branch_b"] = _mm("g_w_b", yb, dpb, "tn", [BF16])
    dz_q, dz_f, dz_i, dz_g, d_lbl, d_hgw, *sent = _hg_bwd(
        z, o_hg, dya, states, lb_logits, hg_norm_w, side=_sib_send(small3, [G[n] for n in small3]))
    for n, t in zip(small3, sent):
        S_[n] = _chip_sum(n, G[n], t, place)
    dz_aq, dz_ak, dz_av, dbias_win = _at_bwd(z, dyb, bias_win)
    dz = jnp.concatenate([dz_q, dz_f, dz_i, dz_g, dz_aq, dz_ak, dz_av, dz_ga, dz_gb], axis=1)
    G["w_in"], *got3 = _mm("g_w_in", u1, dz, "tn", [BF16], side=_chip_exchange([S_[n] for n in small3]))
    GOT.update(zip(small3, got3))
    for n in ("w_down", "w_up"):
        S_[n], GOT[n] = _exchange_wait(n, flights[n], G["w_in"])
    (T_["w_in"],) = _run_side("send_w_in_to_sibling", _sib_send(["w_in"], [G["w_in"]]))
    S_["w_in"] = _chip_sum("w_in", G["w_in"], T_["w_in"], place)
    early = [n for n in WEIGHTS if n != "w_in"]
    H_ = {n: _piece_sum(n, S_[n], GOT[n], place) for n in early}
    flight = _exchange_start("w_in", S_["w_in"])
    share_early = _sib_share([H_[n] for n in early])
    share_early.reads.append(flight[-1])
    du1, *shared = _mm("d_u1", dz, Wf["w_in"], "nt", [F32], side=share_early)
    O_ = dict(zip(early, shared))
    grad_x, _, d_nmix = _rms_bwd("norm_mix_bwd", du1, x2, norm_mix_w, dh1)
    d_rel = _bias_window_grad(dbias_win)
    big_out = {}
    for name in early:
        outs = _adam_quarter(name, *big[name], H_[name], O_[name], place)
        big_out[name] = tuple(a[None] for a in outs)
    S_["w_in"], got_in = _exchange_wait("w_in", flight, outs[1])
    H_["w_in"] = _piece_sum("w_in", S_["w_in"], got_in, place)
    (O_["w_in"],) = _run_side("share_w_in", _sib_share([H_["w_in"]]))
    outs = _adam_quarter("w_in", *big["w_in"], H_["w_in"], O_["w_in"], place)
    big_out["w_in"] = tuple(a[None] for a in outs)

    smalls = [("lb_logits", lb_logits, m_lb_logits, v_lb_logits, d_lbl),
              ("hg_norm_w", hg_norm_w, m_hg_norm_w, v_hg_norm_w, d_hgw),
              ("rel_bias", rel_bias, m_rel_bias, v_rel_bias, d_rel),
              ("norm_mix_w", norm_mix_w, m_norm_mix_w, v_norm_mix_w, d_nmix),
              ("norm_mlp_w", norm_mlp_w, m_norm_mlp_w, v_norm_mlp_w, d_nmlp),
              ("norm_final_w", norm_final_w, m_norm_final_w, v_norm_final_w, d_nf)]
    like = [s[1] for s in smalls]
    packed = _small_allreduce_adam(_pack([s[4] for s in smalls]), _pack(like), _pack([s[2] for s in smalls]),
                                   _pack([s[3] for s in smalls]), got_in)
    small_out = {s[0]: vals for s, vals in zip(smalls, zip(*[_unpack(p, like) for p in packed]))}

    loss = lax.psum(loss_part[0, 0], ("x", "y", "c"))
    order = ["w_in", "lb_logits", "hg_norm_w", "rel_bias", "w_branch_a", "w_branch_b", "w_out", "norm_mix_w",
             "norm_mlp_w", "w_up", "w_down", "norm_final_w"]
    res = {**big_out, **small_out}
    return (loss, grad_x.reshape(x.shape), *[res[n][0] for n in order], *[res[n][1] for n in order],
            *[res[n][2] for n in order], *[res[n][3] for n in order])
```

```python
import functools

import jax
import jax.numpy as jnp
from jax import lax
from jax.experimental import pallas as pl
from jax.experimental.pallas import tpu as pltpu

F32 = jnp.float32
BF16 = jnp.bfloat16
HIGHEST = lax.Precision.HIGHEST

D_MODEL = 2048
SEQ = 2048
CHUNK = 64
HG_HEADS = 8
HG_D = 128
AT_HEADS = 16
AT_DH = 64
LEFT = 8
REL_CLIP = 256
D_FF = 8192
EPS = 1e-6
ADAM_LR = 0.001
ADAM_B1 = 0.9
ADAM_B2 = 0.999
ADAM_EPS = 1e-08
ADAM_WD = 0.01
ADAM_STEP = 10

LANE = 128
NEG = -1e30
EXP_CLAMP = 80.0
VMEM_LIMIT = 48 * 1024 * 1024
MM_TM, MM_TN, MM_TK = 1024, 1024, 2816
ROW_TILE = 256
QB = 2 * CHUNK


def _hgw():
    return HG_HEADS * HG_D


def _atw():
    return AT_HEADS * AT_DH


def _cparams(sem):
    return pltpu.CompilerParams(dimension_semantics=sem, vmem_limit_bytes=VMEM_LIMIT)


def _sigmoid(x):
    return jax.nn.sigmoid(x)


def _dot(a, b, dims, precision=None):
    return lax.dot_general(a, b, (dims, ((), ())), preferred_element_type=F32, precision=precision)


def _nn(a, b, precision=None):
    return _dot(a, b, ((1,), (0,)), precision)


def _nt(a, b, precision=None):
    return _dot(a, b, ((1,), (1,)), precision)


def _tn(a, b, precision=None):
    return _dot(a, b, ((0,), (0,)), precision)


class _Side:
    def __init__(self, build, nsem, reads=(), aliased=(), fresh=()):
        self.build, self.nsem = build, nsem
        self.reads, self.aliased, self.fresh = list(reads), list(aliased), list(fresh)

    def operands(self):
        return self.reads + self.aliased

    def in_specs(self):
        return [ANY] * len(self.operands())

    def out_specs(self):
        return [ANY] * (len(self.aliased) + len(self.fresh))

    def out_shape(self):
        return [jax.ShapeDtypeStruct(a.shape, a.dtype) for a in self.aliased] + self.fresh

    def aliases(self, n_in, n_out):
        return {n_in + len(self.reads) + t: n_out + t for t in range(len(self.aliased))}

    def scratch(self):
        return [pltpu.SemaphoreType.DMA((self.nsem,)), pltpu.SemaphoreType.DMA((self.nsem,))]

    def hooks(self, in_refs, out_refs, sems, first, last):
        nr, na = len(self.reads), len(self.aliased)
        args = (in_refs[:nr], out_refs[:na], out_refs[na:], *sems)

        @pl.when(first)
        def _():
            for cp in self.build(*args):
                cp.start()

        @pl.when(last)
        def _():
            for cp in self.build(*args):
                cp.wait()


def _after(*tokens):
    return _Side(lambda *args: [], 1, reads=tokens)


def _side_parts(side):
    if side is None:
        return [], [], [], [], lambda n_in, n_out: {}, []
    return side.operands(), side.in_specs(), side.out_specs(), side.out_shape(), side.aliases, side.scratch()


def _call_with_side(body, name, grid, in_specs, out_specs, out_shape, scratch, sem, operands, side, n_prefetch=0,
                    aliases=None, borrow=None):
    _, _, s_out, s_shape, _, s_scr = _side_parts(side)
    n_in, n_out = n_prefetch + len(in_specs), len(out_specs)
    borrow = borrow or {}
    s_ops, s_alias = [], {}
    if side is not None:
        keep = [t for t in range(len(side.aliased)) if t not in borrow]
        s_ops = side.reads + [side.aliased[t] for t in keep]
        s_alias = {n_in + len(side.reads) + pos: n_out + t for pos, t in enumerate(keep)}
        s_alias.update({n_prefetch + i: n_out + t for t, i in borrow.items()})
    s_in = [ANY] * len(s_ops)
    n_sin, n_sout = len(s_ops), len(s_out)

    def wrapped(*refs):
        a, b, c = n_in + n_sin, n_in + n_sin + n_out, n_in + n_sin + n_out + n_sout
        ids = [pl.program_id(d) for d in range(len(grid))]
        first = functools.reduce(lambda p, q: p & q, [i == 0 for i in ids])
        last = functools.reduce(lambda p, q: p & q, [i == g - 1 for i, g in zip(ids, grid)])
        side.hooks(refs[n_in:a], refs[b:c], refs[-2:], first, last)
        body(*refs[:n_in], *refs[a:b], *refs[c:-2])

    spec = dict(grid=grid, in_specs=in_specs + s_in, out_specs=out_specs + s_out, scratch_shapes=scratch + s_scr)
    if n_prefetch:
        spec = dict(grid_spec=pltpu.PrefetchScalarGridSpec(num_scalar_prefetch=n_prefetch, **spec))
    return pl.pallas_call(
        body if side is None else wrapped, name=name, out_shape=out_shape + s_shape,
        input_output_aliases={**s_alias, **{n_prefetch + i: o for i, o in (aliases or {}).items()}},
        compiler_params=_cparams(sem if side is None else ("arbitrary",) * len(grid)), **spec,
    )(*operands, *s_ops)


def _mm_tk(K):
    if K <= MM_TK:
        return K
    return max(t for t in range(LANE, MM_TK + 1, LANE) if K % t == 0)


def _mm(name, a, b, mode, out_dtypes, extras=(), epilogue=None, side=None):
    if mode == "nn":
        (M, K), (K2, N) = a.shape, b.shape
    elif mode == "nt":
        (M, K), (N, K2) = a.shape, b.shape
    else:
        (K, M), (K2, N) = a.shape, b.shape
    assert K == K2, (name, a.shape, b.shape)
    tm, tn, tk = min(MM_TM, M), min(MM_TN, N), _mm_tk(K)
    assert M % tm == 0 and N % tn == 0 and K % tk == 0, (name, M, N, K)
    ni, nj, nk = M // tm, N // tn, K // tk
    ne, no = len(extras), len(out_dtypes)
    if epilogue is None:
        epilogue = lambda acc: (acc,)
    s_ops, s_in, s_out, s_shape, s_alias, s_scr = _side_parts(side)
    n_in, n_sin, n_sout = 2 + ne, len(s_ops), len(s_out)

    def body(*refs):
        a_ref, b_ref = refs[:2]
        extra_refs = refs[2:n_in]
        out_refs = refs[n_in + n_sin:n_in + n_sin + no]
        rest = refs[n_in + n_sin + no + n_sout:]
        i, j, k = pl.program_id(0), pl.program_id(1), pl.program_id(2)
        if side is not None:
            side.hooks(refs[n_in:n_in + n_sin], refs[n_in + n_sin + no:n_in + n_sin + no + n_sout], rest[-2:],
                       (i == 0) & (j == 0) & (k == 0), (i == ni - 1) & (j == nj - 1) & (k == nk - 1))
        av, bv = a_ref[...].astype(BF16), b_ref[...].astype(BF16)
        prod = _nn(av, bv) if mode == "nn" else _nt(av, bv) if mode == "nt" else _tn(av, bv)

        def finish(acc):
            res = epilogue(acc, *[e[...] for e in extra_refs])
            for o_ref, r in zip(out_refs, res):
                o_ref[...] = r.astype(o_ref.dtype)

        if nk == 1:
            finish(prod)
        else:
            acc_ref = rest[0]

            @pl.when(k == 0)
            def _():
                acc_ref[...] = prod

            @pl.when((k > 0) & (k < nk - 1))
            def _():
                acc_ref[...] += prod

            @pl.when(k == nk - 1)
            def _():
                finish(acc_ref[...] + prod)

    if mode == "nn":
        a_spec = pl.BlockSpec((tm, tk), lambda i, j, k: (i, k))
        b_spec = pl.BlockSpec((tk, tn), lambda i, j, k: (k, j))
    elif mode == "nt":
        a_spec = pl.BlockSpec((tm, tk), lambda i, j, k: (i, k))
        b_spec = pl.BlockSpec((tn, tk), lambda i, j, k: (j, k))
    else:
        a_spec = pl.BlockSpec((tk, tm), lambda i, j, k: (k, i))
        b_spec = pl.BlockSpec((tk, tn), lambda i, j, k: (k, j))
    o_spec = pl.BlockSpec((tm, tn), lambda i, j, k: (i, j))
    sem = ("arbitrary",) * 3 if side is not None else ("parallel", "parallel", "arbitrary")
    outs = pl.pallas_call(
        body, name=name,
        grid=(ni, nj, nk),
        in_specs=[a_spec, b_spec] + [o_spec] * ne + s_in,
        out_specs=[o_spec] * no + s_out,
        out_shape=[jax.ShapeDtypeStruct((M, N), dt) for dt in out_dtypes] + s_shape,
        input_output_aliases=s_alias(n_in, no),
        scratch_shapes=([pltpu.VMEM((tm, tn), F32)] if nk > 1 else []) + s_scr,
        compiler_params=_cparams(sem),
    )(a, b, *extras, *s_ops)
    return outs[0] if len(outs) == 1 else outs


def _row_spec(tr, d):
    return pl.BlockSpec((tr, d), lambda i: (i, 0))


def _vec_spec(d):
    return pl.BlockSpec((1, d), lambda i: (0, 0))


def _rms_fwd(name, x, w):
    T, D = x.shape
    tr = min(ROW_TILE, T)

    def body(x_ref, w_ref, o_ref):
        xf = x_ref[...]
        r = lax.rsqrt(jnp.mean(xf * xf, axis=-1, keepdims=True) + EPS)
        o_ref[...] = (xf * r * w_ref[...]).astype(BF16)

    return pl.pallas_call(
        body, name=name, grid=(T // tr,),
        in_specs=[_row_spec(tr, D), _vec_spec(D)], out_specs=_row_spec(tr, D),
        out_shape=jax.ShapeDtypeStruct((T, D), BF16),
        compiler_params=_cparams(("parallel",)),
    )(x, w)


def _rms_bwd(name, dy, h, w, dres, side=None):
    T, D = h.shape
    tr = min(ROW_TILE, T)

    def body(dy_ref, h_ref, w_ref, dres_ref, dh_ref, dhb_ref, dw_ref):
        @pl.when(pl.program_id(0) == 0)
        def _():
            dw_ref[...] = jnp.zeros_like(dw_ref)

        hf, dyv = h_ref[...], dy_ref[...]
        r = lax.rsqrt(jnp.mean(hf * hf, axis=-1, keepdims=True) + EPS)
        xhat = hf * r
        dw_ref[...] += jnp.sum(dyv * xhat, axis=0, keepdims=True)
        dxh = dyv * w_ref[...]
        dh = dres_ref[...] + r * (dxh - xhat * jnp.mean(dxh * xhat, axis=-1, keepdims=True))
        dh_ref[...] = dh
        dhb_ref[...] = dh.astype(BF16)

    return _call_with_side(
        body, name, (T // tr,),
        [_row_spec(tr, D), _row_spec(tr, D), _vec_spec(D), _row_spec(tr, D)],
        [_row_spec(tr, D), _row_spec(tr, D), _vec_spec(D)],
        [jax.ShapeDtypeStruct((T, D), F32), jax.ShapeDtypeStruct((T, D), BF16), jax.ShapeDtypeStruct((1, D), F32)],
        [], ("arbitrary",), (dy, h, w, dres), side)


def _loss_head(h2, target, w):
    T, D = h2.shape
    tr = min(ROW_TILE, T)

    def body(h_ref, t_ref, w_ref, loss_ref, dh_ref, dhb_ref, dw_ref):
        @pl.when(pl.program_id(0) == 0)
        def _():
            dw_ref[...] = jnp.zeros_like(dw_ref)
            loss_ref[...] = jnp.zeros_like(loss_ref)

        hf, wv = h_ref[...], w_ref[...]
        r = lax.rsqrt(jnp.mean(hf * hf, axis=-1, keepdims=True) + EPS)
        xhat = hf * r
        diff = xhat * wv - t_ref[...]
        loss_ref[...] += 0.5 * jnp.sum(jnp.mean(diff * diff, axis=-1, keepdims=True))
        dyv = diff * (1.0 / D)
        dw_ref[...] += jnp.sum(dyv * xhat, axis=0, keepdims=True)
        dxh = dyv * wv
        dh = r * (dxh - xhat * jnp.mean(dxh * xhat, axis=-1, keepdims=True))
        dh_ref[...] = dh
        dhb_ref[...] = dh.astype(BF16)

    return pl.pallas_call(
        body, name="loss_head", grid=(T // tr,),
        in_specs=[_row_spec(tr, D), _row_spec(tr, D), _vec_spec(D)],
        out_specs=[_vec_spec(LANE), _row_spec(tr, D), _row_spec(tr, D), _vec_spec(D)],
        out_shape=[jax.ShapeDtypeStruct((1, LANE), F32), jax.ShapeDtypeStruct((T, D), F32),
                   jax.ShapeDtypeStruct((T, D), BF16), jax.ShapeDtypeStruct((1, D), F32)],
        compiler_params=_cparams(("arbitrary",)),
    )(h2, target, w)


def _gate_tiles(T, D):
    goff = 4 * _hgw() + 3 * _atw()
    tc = min(1024, D)
    assert goff % tc == 0 and D % tc == 0
    return min(ROW_TILE, T), tc, goff // tc, D // tc


def _merge(z, pa, pb):
    T, D = pa.shape
    tr, tc, g0, nd = _gate_tiles(T, D)

    def body(ga_ref, gb_ref, pa_ref, pb_ref, o_ref):
        o_ref[...] = (_sigmoid(ga_ref[...]) * pa_ref[...] + _sigmoid(gb_ref[...]) * pb_ref[...]).astype(BF16)

    t = pl.BlockSpec((tr, tc), lambda i, j: (i, j))
    return pl.pallas_call(
        body, name="merge", grid=(T // tr, nd),
        in_specs=[pl.BlockSpec((tr, tc), lambda i, j: (i, g0 + j)),
                  pl.BlockSpec((tr, tc), lambda i, j: (i, g0 + nd + j)), t, t],
        out_specs=t, out_shape=jax.ShapeDtypeStruct((T, D), BF16),
        compiler_params=_cparams(("parallel", "parallel")),
    )(z, z, pa, pb)


def _dmerge(dm, z, pa, pb):
    T, D = pa.shape
    tr, tc, g0, nd = _gate_tiles(T, D)

    def body(dm_ref, ga_ref, gb_ref, pa_ref, pb_ref, dpa_ref, dpb_ref, dga_ref, dgb_ref):
        dmv = dm_ref[...]
        sa, sb = _sigmoid(ga_ref[...]), _sigmoid(gb_ref[...])
        dpa_ref[...] = (dmv * sa).astype(BF16)
        dpb_ref[...] = (dmv * sb).astype(BF16)
        dga_ref[...] = (dmv * pa_ref[...] * sa * (1.0 - sa)).astype(BF16)
        dgb_ref[...] = (dmv * pb_ref[...] * sb * (1.0 - sb)).astype(BF16)

    t = pl.BlockSpec((tr, tc), lambda i, j: (i, j))
    return pl.pallas_call(
        body, name="dmerge", grid=(T // tr, nd),
        in_specs=[t, pl.BlockSpec((tr, tc), lambda i, j: (i, g0 + j)),
                  pl.BlockSpec((tr, tc), lambda i, j: (i, g0 + nd + j)), t, t],
        out_specs=[t, t, t, t],
        out_shape=[jax.ShapeDtypeStruct((T, D), BF16)] * 4,
        compiler_params=_cparams(("parallel", "parallel")),
    )(dm, z, z, pa, pb)


def _hg_gates(xq, xf, lb):
    f = _sigmoid(xf)
    g = lb + (1.0 - lb) * f
    sq = _sigmoid(xq)
    return f, g, jnp.log(g), 1.0 - g, sq, xq * sq * (HG_D ** -0.5)


def _split2(x):
    hi = x.astype(BF16)
    return hi, (x - hi.astype(F32)).astype(BF16)


def _tri_sum(tri, x):
    hi, rest = x.astype(BF16), x - x.astype(BF16).astype(F32)
    mid, lo = _split2(rest)
    return _nn(tri, lo) + _nn(tri, mid) + _nn(tri, hi)


def _hg_decays(lg, tri_incl, rowi):
    b = _tri_sum(tri_incl, lg)
    b_last = jnp.sum(lg, axis=0, keepdims=True)
    b_mid = jnp.sum(jnp.where(rowi <= CHUNK // 2, lg, 0.0), axis=0, keepdims=True)
    return b, b_last, b_mid


HG_GROUP = 2


def _hg_in_specs(T):
    ng = HG_HEADS // HG_GROUP
    return [pl.BlockSpec((T, HG_GROUP * HG_D), lambda h, s=s: (0, s * ng + h)) for s in range(4)]


def _hg_fwd(z, lb_logits, hgw, side=None):
    T = z.shape[0]
    H, d, C, G = HG_HEADS, HG_D, CHUNK, HG_GROUP
    nc = T // C

    def body(hq_ref, hf_ref, hi_ref, hg_ref, lbl_ref, w_ref, ya_ref, o_ref, s_ref):
        lb_all = 1.0 / (1.0 + jnp.exp(lbl_ref[1:2, :] - lbl_ref[0:1, :]))
        wv = w_ref[...]
        row = lax.broadcasted_iota(jnp.int32, (C, C), 0)
        col = lax.broadcasted_iota(jnp.int32, (C, C), 1)
        tril = col <= row
        tri_incl = tril.astype(BF16)
        rowi = lax.broadcasted_iota(jnp.int32, (C, G * d), 0)
        lanes = [slice(hh * d, (hh + 1) * d) for hh in range(G)]
        per_head = lambda fn: jnp.concatenate([fn(hh, sl) for hh, sl in enumerate(lanes)], axis=1)
        wv_all = jnp.tile(wv, (1, G))

        def chunk(c, states):
            rows = pl.ds(pl.multiple_of(c * C, C), C)
            xq, xf, v, xg = hq_ref[rows, :], hf_ref[rows, :], hi_ref[rows, :], hg_ref[rows, :]
            _, _, lg, kk, _, q = _hg_gates(xq, xf, lb_all)
            b, b_last, b_mid = _hg_decays(lg, tri_incl, rowi)
            vb, qe = v.astype(BF16), (q * jnp.exp(b)).astype(BF16)
            qt = (q * jnp.exp(b - b_mid)).astype(BF16)
            kt = (kk * jnp.exp(jnp.minimum(b_mid - b, EXP_CLAMP))).astype(BF16)
            kd, e_last = (kk * jnp.exp(b_last - b)).astype(BF16), jnp.exp(b_last)
            for hh, st in enumerate(states):
                s_ref[hh, c] = st
            o = per_head(lambda hh, sl: _nt(qe[:, sl], states[hh].astype(BF16)))
            a = [jnp.where(tril, _nt(qt[:, sl], kt[:, sl]), 0.0).astype(BF16) for sl in lanes]
            o = o + per_head(lambda hh, sl: _nn(a[hh], vb[:, sl]))
            o_ref[rows, :] = o
            r = per_head(lambda hh, sl: jnp.broadcast_to(
                lax.rsqrt(jnp.mean(o[:, sl] * o[:, sl], axis=-1, keepdims=True) + EPS), (C, d)))
            ya_ref[rows, :] = (o * r * wv_all * (xg * _sigmoid(xg))).astype(BF16)
            return tuple(st * e_last[:, sl] + _tn(vb[:, sl], kd[:, sl]) for st, sl in zip(states, lanes))

        lax.fori_loop(0, nc, chunk, tuple(jnp.zeros((d, d), F32) for _ in range(G)))

    heads = pl.BlockSpec((T, G * d), lambda h: (0, h))
    return _call_with_side(
        body, "hg_fwd", (H // G,),
        _hg_in_specs(T) + [pl.BlockSpec((2, G * d), lambda h: (0, h)), pl.BlockSpec((1, d), lambda h: (0, 0))],
        [heads, heads, pl.BlockSpec((G, nc, d, d), lambda h: (h, 0, 0, 0))],
        [jax.ShapeDtypeStruct((T, H * d), BF16), jax.ShapeDtypeStruct((T, H * d), F32),
         jax.ShapeDtypeStruct((H, nc, d, d), F32)],
        [], ("parallel",), (z, z, z, z, lb_logits, hgw), side)


def _hg_bwd(z, o, dya, states, lb_logits, hgw, side=None):
    T = z.shape[0]
    H, d, C, G = HG_HEADS, HG_D, CHUNK, HG_GROUP
    nc = T // C
    scale = HG_D ** -0.5

    def body(hq_ref, hf_ref, hi_ref, hg_ref, o_ref, dy_ref, s_ref, lbl_ref, w_ref,
             dq_ref, df_ref, di_ref, dg_ref, dlbl_ref, dw_ref, acc_ref):
        lb_all = 1.0 / (1.0 + jnp.exp(lbl_ref[1:2, :] - lbl_ref[0:1, :]))
        wv = w_ref[...]
        row = lax.broadcasted_iota(jnp.int32, (C, C), 0)
        col = lax.broadcasted_iota(jnp.int32, (C, C), 1)
        tril = col <= row
        tri_incl = tril.astype(BF16)
        triu_incl = (col >= row).astype(BF16)
        rowi = lax.broadcasted_iota(jnp.int32, (C, G * d), 0)
        lanes = [slice(hh * d, (hh + 1) * d) for hh in range(G)]
        per_head = lambda fn: jnp.concatenate([fn(hh, sl) for hh, sl in enumerate(lanes)], axis=1)
        head_mean = lambda x: per_head(
            lambda hh, sl: jnp.broadcast_to(jnp.mean(x[:, sl], axis=-1, keepdims=True), (C, d)))
        wv_all = jnp.tile(wv, (1, G))
        lb = lb_all
        acc_ref[...] = jnp.zeros_like(acc_ref)

        @pl.when(pl.program_id(0) == 0)
        def _():
            dw_ref[...] = jnp.zeros_like(dw_ref)

        def chunk(i, carry):
            dsts, tail = carry
            c = nc - 1 - i
            rows = pl.ds(pl.multiple_of(c * C, C), C)
            xq, xf, v, xg = hq_ref[rows, :], hf_ref[rows, :], hi_ref[rows, :], hg_ref[rows, :]
            f, g, lg, kk, sq, q = _hg_gates(xq, xf, lb)
            b, b_last, b_mid = _hg_decays(lg, tri_incl, rowi)
            e_b, e_qm, e_km = jnp.exp(b), jnp.exp(b - b_mid), jnp.exp(jnp.minimum(b_mid - b, EXP_CLAMP))
            e_kl, e_last = jnp.exp(b_last - b), jnp.exp(b_last)
            ov, dy = o_ref[rows, :], dy_ref[rows, :]
            r = lax.rsqrt(head_mean(ov * ov) + EPS)
            xhat = ov * r
            sg = _sigmoid(xg)
            dxg = dy * xhat * wv_all * (sg * (1.0 + xg * (1.0 - sg)))
            dyn = dy * (xg * sg)
            acc_ref[0:1, :] += jnp.sum(dyn * xhat, axis=0, keepdims=True)
            dxh = dyn * wv_all
            dof = r * (dxh - xhat * head_mean(dxh * xhat))
            do, vb = dof.astype(BF16), v.astype(BF16)
            qe, kd, qt, kt = (q * e_b).astype(BF16), (kk * e_kl).astype(BF16), (q * e_qm).astype(BF16), (kk * e_km).astype(BF16)
            pm = [jnp.where(tril, _nt(do[:, sl], vb[:, sl]), 0.0).astype(BF16) for sl in lanes]
            am = [jnp.where(tril, _nt(qt[:, sl], kt[:, sl]), 0.0).astype(BF16) for sl in lanes]
            st = [_split2(s_ref[hh, c]) for hh in range(G)]
            ds = [_split2(x) for x in dsts]
            dq_state = per_head(lambda hh, sl: _nn(do[:, sl], st[hh][1]) + _nn(do[:, sl], st[hh][0]))
            dk_state = per_head(lambda hh, sl: _nn(vb[:, sl], ds[hh][1]) + _nn(vb[:, sl], ds[hh][0]))
            dq_intra = per_head(lambda hh, sl: _nn(pm[hh], kt[:, sl]))
            dk_intra = per_head(lambda hh, sl: _tn(pm[hh], qt[:, sl]))
            dv = per_head(lambda hh, sl: _tn(am[hh], do[:, sl]) + _nt(kd[:, sl], ds[hh][0]))
            new_dsts = tuple(x * e_last[:, sl] + _tn(do[:, sl], qe[:, sl]) for x, sl in zip(dsts, lanes))
            dq = dq_state * e_b + dq_intra * e_qm
            dk = dk_intra * e_km + dk_state * e_kl
            db = (qe.astype(F32) * dq_state + qt.astype(F32) * dq_intra
                  - kt.astype(F32) * dk_intra - kd.astype(F32) * dk_state)
            dlg = _tri_sum(triu_incl, db) + tail
            dgate = dlg / g - dk
            acc_ref[1:2, :] += jnp.sum(dgate * (1.0 - f), axis=0, keepdims=True)
            dq_ref[rows, :] = (dq * scale * (sq * (1.0 + xq * (1.0 - sq)))).astype(BF16)
            df_ref[rows, :] = (dgate * (1.0 - lb) * f * (1.0 - f)).astype(BF16)
            di_ref[rows, :] = dv.astype(BF16)
            dg_ref[rows, :] = dxg.astype(BF16)
            return new_dsts, tail + jnp.sum(db, axis=0, keepdims=True)

        lax.fori_loop(0, nc, chunk, (tuple(jnp.zeros((d, d), F32) for _ in range(G)), jnp.zeros((1, G * d), F32)))
        dw_ref[...] += functools.reduce(lambda p, q: p + q, [acc_ref[0:1, sl] for sl in lanes])
        dl0 = acc_ref[1:2, :] * lb_all * (1.0 - lb_all)
        dlbl_ref[0:1, :] = dl0
        dlbl_ref[1:2, :] = -dl0

    heads = pl.BlockSpec((T, G * d), lambda h: (0, h))
    logits = pl.BlockSpec((2, G * d), lambda h: (0, h))
    return _call_with_side(
        body, "hg_bwd", (H // G,),
        _hg_in_specs(T) + [heads, heads, pl.BlockSpec((G, nc, d, d), lambda h: (h, 0, 0, 0)), logits,
                           pl.BlockSpec((1, d), lambda h: (0, 0))],
        [heads, heads, heads, heads, logits, pl.BlockSpec((1, d), lambda h: (0, 0))],
        [jax.ShapeDtypeStruct((T, H * d), BF16)] * 4 + [jax.ShapeDtypeStruct((2, H * d), F32),
                                                        jax.ShapeDtypeStruct((1, d), F32)],
        [pltpu.VMEM((8, G * d), F32)], ("arbitrary",), (z, z, z, z, o, dya, states, lb_logits, hgw), side)


def _at_dims():
    pad = LEFT * CHUNK
    return pad, QB + pad, AT_HEADS * AT_DH // LANE, 4 * _hgw() // LANE


def _rel_of_period():
    pad, W, _, _ = _at_dims()
    n = jnp.arange(QB + W)
    return jnp.clip(pad - jnp.where(n < W, n, n - (QB + W)), -REL_CLIP, REL_CLIP) + REL_CLIP


def _bias_window(rel_bias):
    pad, W, _, _ = _at_dims()
    H, P = rel_bias.shape[0], QB + W
    per = rel_bias[:, _rel_of_period()]
    win = jnp.tile(per, (1, QB))[:, :QB * (P - 1)].reshape(H, QB, P - 1)[:, :, :W]
    t = jnp.arange(QB)[:, None]
    j = jnp.arange(W)[None, :]
    ok = (j // CHUNK >= t // CHUNK) & (j // CHUNK <= t // CHUNK + LEFT)
    return jnp.where(ok[None], win, NEG)


def _bias_window_grad(dbw):
    pad, W, _, _ = _at_dims()
    H, P = dbw.shape[0], QB + W
    flat = jnp.pad(dbw, ((0, 0), (0, 0), (0, P - 1 - W))).reshape(H, QB * (P - 1))
    per = jnp.pad(flat, ((0, 0), (0, QB))).reshape(H, QB, P).sum(axis=1)
    onehot = _rel_of_period()[:, None] == jnp.arange(2 * REL_CLIP + 1)[None, :]
    return jnp.dot(per, onehot.astype(F32), precision=HIGHEST)


def _at_stack(x):
    first = lax.broadcasted_iota(jnp.int32, x.shape, 1) < AT_DH
    return jnp.concatenate([jnp.where(first, x, 0.0), jnp.where(first, 0.0, x)], axis=0).astype(BF16)


def _at_unstack(x):
    first = lax.broadcasted_iota(jnp.int32, (QB, LANE), 1) < AT_DH
    return jnp.where(first, x[:QB], x[QB:])


def _at_softmax(qs, kw, bias_ref, qi):
    pad, W, _, _ = _at_dims()
    s = _nt(qs, kw) * (AT_DH ** -0.5) + bias_ref[...].reshape(2 * QB, W)
    valid = lax.broadcasted_iota(jnp.int32, (2 * QB, W), 1) + qi * QB >= pad
    s = jnp.where(valid, s, NEG)
    e = jnp.exp(s - jnp.max(s, axis=-1, keepdims=True))
    return e / jnp.sum(e, axis=-1, keepdims=True)


def _at_fwd(z, bias_win, side=None):
    T = z.shape[0]
    pad, W, HP, c0 = _at_dims()
    nq = T // QB

    def body(q_ref, k_ref, v_ref, bias_ref, o_ref, kpad, vpad):
        qi = pl.program_id(1)

        @pl.when(qi == 0)
        def _():
            kpad[0:pad, :] = jnp.zeros((pad, LANE), BF16)
            vpad[0:pad, :] = jnp.zeros((pad, LANE), BF16)
            kpad[pad:, :] = k_ref[...].astype(BF16)
            vpad[pad:, :] = v_ref[...].astype(BF16)

        win = pl.ds(pl.multiple_of(qi * QB, QB), W)
        kw, vw = kpad[win, :], vpad[win, :]
        p = _at_softmax(_at_stack(q_ref[...]), kw, bias_ref, qi)
        o_ref[...] = _at_unstack(_nn(p.astype(BF16), vw)).astype(BF16)

    full = lambda s: pl.BlockSpec((T, LANE), lambda hp, qi, s=s: (0, c0 + s * HP + hp))
    return _call_with_side(
        body, "at_fwd", (HP, nq),
        [pl.BlockSpec((QB, LANE), lambda hp, qi: (qi, c0 + hp)), full(1), full(2),
         pl.BlockSpec((2, QB, W), lambda hp, qi: (hp, 0, 0))],
        [pl.BlockSpec((QB, LANE), lambda hp, qi: (qi, hp))],
        [jax.ShapeDtypeStruct((T, HP * LANE), BF16)],
        [pltpu.VMEM((T + pad, LANE), BF16)] * 2, ("parallel", "arbitrary"), (z, z, z, bias_win), side)


def _at_bwd(z, dyb, bias_win, side=None):
    T = z.shape[0]
    pad, W, HP, c0 = _at_dims()
    nq = T // QB
    scale = AT_DH ** -0.5

    def body(q_ref, k_ref, v_ref, do_ref, bias_ref, dq_ref, dk_ref, dv_ref, dbias_ref, kpad, vpad, dkpad, dvpad):
        qi = pl.program_id(1)

        @pl.when(qi == 0)
        def _():
            kpad[0:pad, :] = jnp.zeros((pad, LANE), BF16)
            vpad[0:pad, :] = jnp.zeros((pad, LANE), BF16)
            kpad[pad:, :] = k_ref[...].astype(BF16)
            vpad[pad:, :] = v_ref[...].astype(BF16)
            dkpad[...] = jnp.zeros_like(dkpad)
            dvpad[...] = jnp.zeros_like(dvpad)
            dbias_ref[...] = jnp.zeros_like(dbias_ref)

        win = pl.ds(pl.multiple_of(qi * QB, QB), W)
        kw, vw = kpad[win, :], vpad[win, :]
        qs, dos = _at_stack(q_ref[...]), _at_stack(do_ref[...])
        p = _at_softmax(qs, kw, bias_ref, qi)
        dp = _nt(dos, vw)
        ds = p * (dp - jnp.sum(p * dp, axis=-1, keepdims=True))
        dbias_ref[...] += ds.reshape(2, QB, W)
        dsb = (ds * scale).astype(BF16)
        dq_ref[...] = _at_unstack(_nn(dsb, kw)).astype(BF16)
        dkpad[win, :] += _tn(dsb, qs)
        dvpad[win, :] += _tn(p.astype(BF16), dos)

        @pl.when(qi == nq - 1)
        def _():
            dk_ref[...] = dkpad[pad:, :].astype(BF16)
            dv_ref[...] = dvpad[pad:, :].astype(BF16)

    full = lambda s: pl.BlockSpec((T, LANE), lambda hp, qi, s=s: (0, c0 + s * HP + hp))
    blk = pl.BlockSpec((QB, LANE), lambda hp, qi: (qi, hp))
    col = pl.BlockSpec((T, LANE), lambda hp, qi: (0, hp))
    bw = pl.BlockSpec((2, QB, W), lambda hp, qi: (hp, 0, 0))
    return _call_with_side(
        body, "at_bwd", (HP, nq),
        [pl.BlockSpec((QB, LANE), lambda hp, qi: (qi, c0 + hp)), full(1), full(2), blk, bw],
        [blk, col, col, bw],
        [jax.ShapeDtypeStruct((T, HP * LANE), BF16)] * 3 + [jax.ShapeDtypeStruct(bias_win.shape, F32)],
        [pltpu.VMEM((T + pad, LANE), BF16)] * 2 + [pltpu.VMEM((T + pad, LANE), F32)] * 2,
        ("parallel", "arbitrary"), (z, z, z, dyb, bias_win), side)


def _piece_tiles(name, full_shape):
    pr, pc = _piece_shape(name, full_shape)
    tr = min(ROW_TILE, pr)
    assert pr % tr == 0
    nt = pr // tr
    if name in ROW_SHARDED:
        return tr, nt, lambda q, half, i: ((2 * q + half) * nt + i, 0)
    return tr, nt, lambda q, half, i: (half * nt + i, q)


def _cast_into_full(name, wq, place, side=None):
    full = _full_shape(name, wq.shape)
    pc = wq.shape[1]
    tr, nt, at = _piece_tiles(name, full)

    def body(place_ref, w_ref, o_ref):
        o_ref[...] = w_ref[...].astype(BF16)

    outs = _call_with_side(
        body, "cast_" + name, (2, nt), [pl.BlockSpec((tr, pc), lambda h, i, s: (h * nt + i, 0))],
        [pl.BlockSpec((tr, pc), lambda h, i, s: at(s[0], h, i))], [jax.ShapeDtypeStruct(full, BF16)],
        [], ("parallel", "parallel"), (place, wq), side, n_prefetch=1)
    return outs[0] if side is None else outs


def _g_w_in_half(u1, dz, place, own, side=None):
    T, K = u1.shape
    N = dz.shape[1]
    hk, tn = K // 2, min(MM_TN, N)
    half = (lambda s: s[1]) if own else (lambda s: 1 - s[1])

    def body(place_ref, a_ref, b_ref, o_ref):
        o_ref[...] = _tn(a_ref[...], b_ref[...]).astype(BF16)

    outs = _call_with_side(
        body, "g_w_in_keep" if own else "g_w_in_send", (N // tn,),
        [pl.BlockSpec((T, hk), lambda j, s: (0, half(s))), pl.BlockSpec((T, tn), lambda j, s: (0, j))],
        [pl.BlockSpec((hk, tn), lambda j, s: (0, j))], [jax.ShapeDtypeStruct((hk, N), BF16)],
        [], ("parallel",), (place, u1, dz), side, n_prefetch=1)
    return outs[0] if side is None else outs


def _chip_sum(name, grad, theirs, place, kept_rows=False):
    pr, pc = theirs.shape[1:]
    tr, nt, at = _piece_tiles(name, (2 * grad.shape[0], grad.shape[1]) if kept_rows else grad.shape)
    if kept_rows:
        at = lambda q, half, i: (i, q)

    def body(place_ref, g_ref, t_ref, o_ref):
        o_ref[...] = (g_ref[...].astype(F32) + t_ref[...].astype(F32)).astype(BF16)

    piece = pl.BlockSpec((None, tr, pc), lambda q, i, s: (q, i, 0))
    return pl.pallas_call(
        body, name="chip_sum_" + name,
        grid_spec=pltpu.PrefetchScalarGridSpec(
            num_scalar_prefetch=1, grid=(4, nt),
            in_specs=[pl.BlockSpec((tr, pc), lambda q, i, s: at(q, s[1], i)), piece], out_specs=piece),
        out_shape=jax.ShapeDtypeStruct(theirs.shape, BF16),
        compiler_params=_cparams(("parallel", "parallel")),
    )(place, grad, theirs)


def _piece_sum(name, chip_sums, got, place):
    pr, pc = chip_sums.shape[1:]
    tr = min(ROW_TILE, pr)

    def body(place_ref, own_ref, got_ref, o_ref):
        o_ref[...] = (own_ref[...].astype(F32) + got_ref[0].astype(F32) + got_ref[1].astype(F32)
                      + got_ref[2].astype(F32))

    return pl.pallas_call(
        body, name="piece_sum_" + name,
        grid_spec=pltpu.PrefetchScalarGridSpec(
            num_scalar_prefetch=1, grid=(pr // tr,),
            in_specs=[pl.BlockSpec((None, tr, pc), lambda i, s: (s[0], i, 0)),
                      pl.BlockSpec((3, tr, pc), lambda i, s: (0, i, 0))],
            out_specs=pl.BlockSpec((tr, pc), lambda i, s: (i, 0))),
        out_shape=jax.ShapeDtypeStruct((pr, pc), F32),
        compiler_params=_cparams(("parallel",)),
    )(place, chip_sums, got)


def _adam_quarter(name, w, m, v, g_mine, g_sib, place, side=None):
    pr, pc = g_mine.shape
    tr = min(ROW_TILE // 2, pr)
    nt = pr // tr

    def body(place_ref, w_ref, m_ref, v_ref, gm_ref, gs_ref, go_ref, d_ref, mo_ref, vo_ref):
        g = jnp.where(pl.program_id(0) == place_ref[1], gm_ref[...], gs_ref[...])
        delta, mn, vn = _adam_math(w_ref[...], g, m_ref[...], v_ref[...])
        go_ref[...] = g
        d_ref[...] = delta
        mo_ref[...] = mn
        vo_ref[...] = vn

    quarter = pl.BlockSpec((tr, pc), lambda h, i, s: (h * nt + i, 0))
    mine = pl.BlockSpec((tr, pc), lambda h, i, s: (jnp.where(h == s[1], i, 0), 0))
    sib = pl.BlockSpec((tr, pc), lambda h, i, s: (jnp.where(h == s[1], 0, i), 0))
    return _call_with_side(
        body, "adam_" + name, (2, nt), [quarter, quarter, quarter, mine, sib], [quarter] * 4,
        [jax.ShapeDtypeStruct(w.shape, F32)] * 4, [], ("parallel", "parallel"),
        (place, w, m, v, g_mine, g_sib), side, n_prefetch=1)


def _adam_math(w, g, m, v):
    m = ADAM_B1 * m + (1.0 - ADAM_B1) * g
    v = ADAM_B2 * v + (1.0 - ADAM_B2) * (g * g)
    m_hat = m / (1.0 - ADAM_B1 ** ADAM_STEP)
    v_hat = v / (1.0 - ADAM_B2 ** ADAM_STEP)
    return -ADAM_LR * (m_hat / (jnp.sqrt(v_hat) + ADAM_EPS) + ADAM_WD * w), m, v


WEIGHTS = ("w_in", "w_branch_a", "w_branch_b", "w_out", "w_up", "w_down")
ROW_SHARDED = ("w_out", "w_down")
ANY = pl.BlockSpec(memory_space=pl.ANY)
MESH = pl.DeviceIdType.MESH


def _place():
    x, y, c = lax.axis_index("x"), lax.axis_index("y"), lax.axis_index("c")
    chips = [(1 - x, y), (x, 1 - y), (1 - x, 1 - y)]
    return x, y, c, 2 * x + y, chips, [2 * cx + cy for cx, cy in chips]


def _piece(full_ref, name, q, half):
    K, N = full_ref.shape
    if name in ROW_SHARDED:
        rows = K // 8
        return full_ref.at[pl.ds(q * (2 * rows) + half * rows, rows), :]
    return full_ref.at[pl.ds(half * (K // 2), K // 2), pl.ds(q * (N // 4), N // 4)]


def _piece_shape(name, full_shape):
    K, N = full_shape
    return (K // 8, N) if name in ROW_SHARDED else (K // 2, N // 4)


def _full_shape(name, quarter_shape):
    Kq, Nq = quarter_shape
    return (4 * Kq, Nq) if name in ROW_SHARDED else (Kq, 4 * Nq)


def _remote(src, dst, send_sem, recv_sem, device):
    return pltpu.make_async_remote_copy(src_ref=src, dst_ref=dst, send_sem=send_sem, recv_sem=recv_sem,
                                        device_id=device, device_id_type=MESH)


def _z_part(u1, w_in, z_prev, place, k0, count, side=None):
    T, K = u1.shape
    N = w_in.shape[1]
    nq = N // 4
    tn = nq // 2 if (nq // 2) % LANE == 0 else nq
    tm = min(MM_TM, T)
    per = nq // tn
    col = lambda g, j, s: (s[0] ^ (k0 + g)) * per + j
    ins = [pl.BlockSpec((tm, K), lambda g, i, j, s: (i, 0)), pl.BlockSpec((K, tn), lambda g, i, j, s: (0, col(g, j, s)))]
    operands = [place, u1, w_in]
    if z_prev is not None:
        ins.append(ANY)
        operands.append(z_prev)

    def body(place_ref, a_ref, b_ref, *rest):
        rest[-1][...] = _nn(a_ref[...], b_ref[...])

    return _call_with_side(
        body, "z_part_%d" % k0, (count, T // tm, per), ins,
        [pl.BlockSpec((tm, tn), lambda g, i, j, s: (i, col(g, j, s)))], [jax.ShapeDtypeStruct((T, N), F32)],
        [], ("parallel",) * 3, tuple(operands), side, n_prefetch=1, aliases={} if z_prev is None else {2: 0},
        borrow={0: 1} if side is not None and side.aliased and side.aliased[0] is w_in else None)


def _rows(ref, span):
    return ref if span is None else ref.at[pl.ds(span[0], span[1]), :]


def _ici_near(names, fulls, rows=None):
    rows = rows or [None] * len(names)

    def build(reads, aliased, fresh, send_sems, recv_sems, off=0):
        _, _, c, p, chips, _ = _place()
        out = []
        for i, (ref, name) in enumerate(zip(aliased, names)):
            mine = _rows(_piece(ref, name, p, c), rows[i])
            for j, chip in enumerate(chips[:2]):
                k = off + 2 * i + j
                out.append(_remote(mine, mine, send_sems.at[k], recv_sems.at[k], (*chip, c)))
        return out

    return _Side(build, 2 * len(names), aliased=fulls)


def _ici_far(names, fulls, rows=None):
    rows = rows or [None] * len(names)

    def build(reads, aliased, fresh, send_sems, recv_sems, off=0):
        x, y, c, _, _, chip_ids = _place()
        south = c == 0
        src_chip = jnp.where(south, chip_ids[0], chip_ids[1])
        target = (jnp.where(south, x, 1 - x), jnp.where(south, 1 - y, y), c)
        out = []
        for i, (ref, name) in enumerate(zip(aliased, names)):
            landed = _rows(_piece(ref, name, src_chip, c), rows[i])
            out.append(_remote(landed, landed, send_sems.at[off + i], recv_sems.at[off + i], target))
        return out

    return _Side(build, len(names), aliased=fulls)


def _d2d_gather(names, fulls, which=(0, 1, 2)):
    def build(reads, aliased, fresh, send_sems, recv_sems, off=0):
        x, y, c, _, _, chip_ids = _place()
        out = []
        for i, (ref, name) in enumerate(zip(aliased, names)):
            for n, j in enumerate(which):
                landed, k = _piece(ref, name, chip_ids[j], c), off + len(which) * i + n
                out.append(_remote(landed, landed, send_sems.at[k], recv_sems.at[k], (x, y, 1 - c)))
        return out

    return _Side(build, len(which) * len(names), aliased=fulls)


def _sib_send(names, grads):
    def build(reads, aliased, fresh, send_sems, recv_sems, off=0):
        x, y, c, _, _, _ = _place()
        out = []
        for i, name in enumerate(names):
            for q in range(4):
                k = off + 4 * i + q
                out.append(_remote(_piece(reads[i], name, q, 1 - c), fresh[i].at[q], send_sems.at[k], recv_sems.at[k],
                                   (x, y, 1 - c)))
        return out

    shapes = [jax.ShapeDtypeStruct((4,) + _piece_shape(name, g.shape), BF16) for name, g in zip(names, grads)]
    return _Side(build, 4 * len(names), reads=grads, fresh=shapes)


def _sib_send_half(sent):
    K2, N = sent.shape

    def build(reads, aliased, fresh, send_sems, recv_sems, off=0):
        x, y, c, _, _, _ = _place()
        return [_remote(reads[0].at[:, pl.ds(q * (N // 4), N // 4)], fresh[0].at[q], send_sems.at[off + q],
                        recv_sems.at[off + q], (x, y, 1 - c)) for q in range(4)]

    return _Side(build, 4, reads=[sent], fresh=[jax.ShapeDtypeStruct((4, K2, N // 4), BF16)])


def _chip_exchange(chip_sums, rows=None, got=None):
    rows = rows or [None] * len(chip_sums)

    def build(reads, aliased, fresh, send_sems, recv_sems, off=0):
        _, _, c, _, chips, chip_ids = _place()
        out = []
        for i in range(len(chip_sums)):
            for j, (chip, cid) in enumerate(zip(chips, chip_ids)):
                k = off + 3 * i + j
                out.append(_remote(_rows(reads[i].at[cid], rows[i]), _rows((aliased or fresh)[i].at[j], rows[i]),
                                   send_sems.at[k], recv_sems.at[k], (*chip, c)))
        return out

    if got is not None:
        return _Side(build, 3 * len(chip_sums), reads=chip_sums, aliased=got)
    shapes = [jax.ShapeDtypeStruct((3,) + s.shape[1:], BF16) for s in chip_sums]
    return _Side(build, 3 * len(chip_sums), reads=chip_sums, fresh=shapes)


HBM = pl.BlockSpec(memory_space=pltpu.HBM)
SEM = pl.BlockSpec(memory_space=pltpu.SEMAPHORE)


def _exchange_copies(s_ref, land_ref, send_sems, recv_sems):
    _, _, c, _, chips, chip_ids = _place()
    return [_remote(s_ref.at[cid], land_ref.at[j], send_sems.at[j], recv_sems.at[j], (*chip, c))
            for j, (chip, cid) in enumerate(zip(chips, chip_ids))]


def _exchange_start(name, chip_sum):
    def body(s_ref, land_ref, send_sems, recv_sems, s_thru, land_thru, token):
        for cp in _exchange_copies(s_ref, land_ref, send_sems, recv_sems):
            cp.start()
        token[...] = jnp.zeros_like(token)

    land = jax.ShapeDtypeStruct((3,) + chip_sum.shape[1:], chip_sum.dtype)
    return pl.pallas_call(
        body, name="exchange_start_" + name,
        out_shape=(pltpu.SemaphoreType.DMA((3,)), pltpu.SemaphoreType.DMA((3,)),
                   pltpu.HBM(chip_sum.shape, chip_sum.dtype), pltpu.HBM(land.shape, land.dtype),
                   jax.ShapeDtypeStruct((8, LANE), F32)),
        in_specs=(HBM, HBM), out_specs=(SEM, SEM, HBM, HBM, pl.BlockSpec(memory_space=pltpu.VMEM)),
        input_output_aliases={0: 2, 1: 3},
        compiler_params=pltpu.CompilerParams(has_side_effects=pltpu.SideEffectType.DATAFLOW_SIDE_EFFECTING),
    )(pltpu.with_memory_space_constraint(chip_sum, pltpu.HBM),
      pltpu.with_memory_space_constraint(lax.empty(land.shape, land.dtype), pltpu.HBM))


def _exchange_wait(name, flight, after):
    send_sems, recv_sems, s_thru, land_thru, _ = flight

    def body(s_ref, land_ref, send_sems, recv_sems, after_ref, s_out, land_out):
        for cp in _exchange_copies(s_ref, land_ref, send_sems, recv_sems):
            cp.wait_send()
            cp.wait_recv()

    return pl.pallas_call(
        body, name="exchange_wait_" + name,
        out_shape=(pltpu.HBM(s_thru.shape, s_thru.dtype), pltpu.HBM(land_thru.shape, land_thru.dtype)),
        in_specs=(HBM, HBM, SEM, SEM, ANY), out_specs=(HBM, HBM), input_output_aliases={0: 0, 1: 1},
        compiler_params=pltpu.CompilerParams(has_side_effects=pltpu.SideEffectType.DATAFLOW_SIDE_EFFECTING),
    )(s_thru, land_thru, send_sems, recv_sems, after)


def _sib_share(halves):
    def build(reads, aliased, fresh, send_sems, recv_sems, off=0):
        x, y, c, _, _, _ = _place()
        return [_remote(reads[i], fresh[i], send_sems.at[off + i], recv_sems.at[off + i], (x, y, 1 - c))
                for i in range(len(halves))]

    return _Side(build, len(halves), reads=halves, fresh=[jax.ShapeDtypeStruct(h.shape, F32) for h in halves])


def _join(a, b):
    def build(reads, aliased, fresh, send_sems, recv_sems, off=0):
        ra, aa, fa = len(a.reads), len(a.aliased), len(a.fresh)
        return (a.build(reads[:ra], aliased[:aa], fresh[:fa], send_sems, recv_sems, off)
                + b.build(reads[ra:], aliased[aa:], fresh[fa:], send_sems, recv_sems, off + a.nsem))

    return _Side(build, a.nsem + b.nsem, a.reads + b.reads, a.aliased + b.aliased, a.fresh + b.fresh)


def _run_side(name, side):
    nr, na = len(side.reads), len(side.aliased)

    def body(*refs):
        n_in, n_out = nr + na, na + len(side.fresh)
        outs = refs[n_in:n_in + n_out]
        copies = side.build(refs[:nr], outs[:na], outs[na:], *refs[-2:])
        for cp in copies:
            cp.start()
        for cp in copies:
            cp.wait()

    return pl.pallas_call(
        body, name=name, in_specs=side.in_specs(), out_specs=side.out_specs(), out_shape=side.out_shape(),
        input_output_aliases=side.aliases(0, 0), scratch_shapes=side.scratch(),
    )(*side.operands())


def _small_allreduce_adam(gpart, w, m, v, after):
    R = gpart.shape[0]

    def body(g_ref, w_ref, m_ref, v_ref, after_ref, go_ref, d_ref, mo_ref, vo_ref, buf, send_sems, recv_sems):
        x, y, c = lax.axis_index("x"), lax.axis_index("y"), lax.axis_index("c")
        me = 4 * x + 2 * y + c
        buf[me] = g_ref[...]
        copies = []
        for k in range(1, 8):
            fx, fy, fc = (k >> 2) & 1, (k >> 1) & 1, k & 1
            peer = (1 - x if fx else x, 1 - y if fy else y, 1 - c if fc else c)
            cp = _remote(g_ref, buf.at[me], send_sems.at[k - 1], recv_sems.at[k - 1], peer)
            cp.start()
            copies.append((cp, 4 * peer[0] + 2 * peer[1] + peer[2]))
        for k, (cp, pid) in enumerate(copies):
            _remote(g_ref, buf.at[pid], send_sems.at[k], recv_sems.at[k], (x, y, c)).wait_recv()
        for cp, _ in copies:
            cp.wait_send()
        g = buf[0]
        for d in range(1, 8):
            g = g + buf[d]
        delta, mn, vn = _adam_math(w_ref[...], g, m_ref[...], v_ref[...])
        go_ref[...] = g
        d_ref[...] = delta
        mo_ref[...] = mn
        vo_ref[...] = vn

    vm = pl.BlockSpec(memory_space=pltpu.VMEM)
    return pl.pallas_call(
        body, name="small_allreduce_adam",
        in_specs=[vm] * 4 + [ANY], out_specs=[vm] * 4,
        out_shape=[jax.ShapeDtypeStruct((R, LANE), F32)] * 4,
        scratch_shapes=[pltpu.VMEM((8, R, LANE), F32), pltpu.SemaphoreType.DMA((7,)), pltpu.SemaphoreType.DMA((7,))],
    )(gpart, w, m, v, after)


def _pack(arrs):
    flat = jnp.concatenate([a.reshape(-1).astype(F32) for a in arrs])
    rows = -(-flat.shape[0] // (8 * LANE)) * 8
    return jnp.pad(flat, (0, rows * LANE - flat.shape[0])).reshape(rows, LANE)


def _unpack(packed, like):
    flat, out, off = packed.reshape(-1), [], 0
    for a in like:
        out.append(flat[off:off + a.size].reshape(a.shape))
        off += a.size
    return out


def kernel(x, w_in, lb_logits, hg_norm_w, rel_bias, w_branch_a, w_branch_b, w_out, norm_mix_w, norm_mlp_w, w_up, w_down, norm_final_w, loss_target, m_w_in, m_lb_logits, m_hg_norm_w, m_rel_bias, m_w_branch_a, m_w_branch_b, m_w_out, m_norm_mix_w, m_norm_mlp_w, m_w_up, m_w_down, m_norm_final_w, v_w_in, v_lb_logits, v_hg_norm_w, v_rel_bias, v_w_branch_a, v_w_branch_b, v_w_out, v_norm_mix_w, v_norm_mlp_w, v_w_up, v_w_down, v_norm_final_w):
    T, D = x.shape[1], x.shape[2]
    x2, tgt = x.reshape(T, D), loss_target.reshape(T, D)
    big = dict(w_in=(w_in, m_w_in, v_w_in), w_branch_a=(w_branch_a, m_w_branch_a, v_w_branch_a),
               w_branch_b=(w_branch_b, m_w_branch_b, v_w_branch_b), w_out=(w_out, m_w_out, v_w_out),
               w_up=(w_up, m_w_up, v_w_up), w_down=(w_down, m_w_down, v_w_down))
    big = {k: tuple(a[0] for a in v) for k, v in big.items()}
    nfw = norm_final_w.reshape(1, D)

    place = jnp.stack([2 * lax.axis_index("x") + lax.axis_index("y"), lax.axis_index("c")]).astype(jnp.int32)
    small3 = ["w_branch_a", "w_branch_b", "w_out"]

    def span(name, lo, hi):
        pr = big[name][0].shape[0] // 2
        return (pr * lo // 16, pr * (hi - lo) // 16)

    Wf = {"w_in": _cast_into_full("w_in", big["w_in"][0], place)}
    for name, lo, hi in (("w_up", 0, 2), ("w_down", 2, 4)):
        Wf[name], Wf["w_in"] = _cast_into_full(
            name, big[name][0], place, side=_ici_near(["w_in"], [Wf["w_in"]], rows=[span("w_in", lo, hi)]))
    for name in small3:
        Wf[name] = _cast_into_full(name, big[name][0], place)

    ab = ["w_branch_a", "w_branch_b"]
    u1 = _rms_fwd("norm_mix", x2, norm_mix_w)
    z, Wf["w_in"] = _z_part(u1, Wf["w_in"], None, place, 0, 1,
                            side=_ici_near(["w_in"], [Wf["w_in"]], rows=[span("w_in", 4, 16)]))
    (Wf["w_in"],) = _run_side("pass_w_in_near", _d2d_gather(["w_in"], [Wf["w_in"]], which=(0, 1)))
    z, Wf["w_in"] = _z_part(u1, Wf["w_in"], z, place, 1, 2, side=_ici_far(["w_in"], [Wf["w_in"]]))
    (Wf["w_in"],) = _run_side("pass_w_in_far", _d2d_gather(["w_in"], [Wf["w_in"]], which=(2,)))
    z, *moved = _z_part(u1, Wf["w_in"], z, place, 3, 1, side=_ici_near(ab, [Wf[n] for n in ab]))
    Wf.update(zip(ab, moved))
    ya, o_hg, states, *moved = _hg_fwd(
        z, lb_logits, hg_norm_w,
        side=_join(_ici_near(["w_out", "w_up", "w_down"], [Wf[n] for n in ("w_out", "w_up", "w_down")],
                             rows=[None, None, span("w_down", 0, 6)]),
                   _ici_far(ab, [Wf[n] for n in ab])))
    Wf.update(zip(["w_out", "w_up", "w_down"] + ab, moved))
    bias_win = _bias_window(rel_bias[0])
    yb, *moved = _at_fwd(
        z, bias_win,
        side=_join(_join(_ici_far(["w_out", "w_up"], [Wf["w_out"], Wf["w_up"]]),
                         _ici_near(["w_down"], [Wf["w_down"]], rows=[span("w_down", 6, 16)])),
                   _d2d_gather(ab, [Wf[n] for n in ab])))
    Wf.update(zip(["w_out", "w_up", "w_down"] + ab, moved))
    pa, Wf["w_out"] = _mm("branch_a", ya, Wf["w_branch_a"], "nn", [F32], side=_d2d_gather(["w_out"], [Wf["w_out"]]))
    pb = _mm("branch_b", yb, Wf["w_branch_b"], "nn", [F32])
    merged = _merge(z, pa, pb)
    add = lambda acc, res: (acc + res,)
    h1, Wf["w_up"] = _mm("out_proj", merged, Wf["w_out"], "nn", [F32], extras=[x2], epilogue=add,
                         side=_d2d_gather(["w_up"], [Wf["w_up"]]))
    u2 = _rms_fwd("norm_mlp", h1, norm_mlp_w)
    relu2 = lambda acc: (acc, jnp.square(jnp.maximum(acc, 0.0)))
    a_pre, act, Wf["w_down"] = _mm("mlp_up", u2, Wf["w_up"], "nn", [F32, BF16], epilogue=relu2,
                                   side=_ici_far(["w_down"], [Wf["w_down"]]))
    (Wf["w_down"],) = _run_side("pass_w_down", _d2d_gather(["w_down"], [Wf["w_down"]]))
    h2 = _mm("mlp_down", act, Wf["w_down"], "nn", [F32], extras=[h1], epilogue=add)
    loss_part, dh2, dh2b, d_nf = _loss_head(h2, tgt, nfw)

    drelu2 = lambda acc, a: (acc * (2.0 * jnp.maximum(a, 0.0)),)
    da = _mm("d_act", dh2b, Wf["w_down"], "nt", [BF16], extras=[a_pre], epilogue=drelu2)
    G = {}
    G["w_down"] = _mm("g_w_down", act, dh2b, "tn", [BF16])
    G["w_up"] = _mm("g_w_up", u2, da, "tn", [BF16])
    T_, S_, GOT = {}, {}, {}
    du2, T_["w_down"], T_["w_up"] = _mm("d_u2", da, Wf["w_up"], "nt", [F32],
                                        side=_sib_send(["w_down", "w_up"], [G["w_down"], G["w_up"]]))
    flights = {n: _exchange_start(n, _chip_sum(n, G[n], T_[n], place)) for n in ("w_down", "w_up")}
    dh1, dh1b, d_nmlp = _rms_bwd("norm_mlp_bwd", du2, h1, norm_mlp_w, dh2,
                                 side=_after(flights["w_down"][-1], flights["w_up"][-1]))
    dmerged = _mm("d_merged", dh1b, Wf["w_out"], "nt", [F32])
    G["w_out"] = _mm("g_w_out", merged, dh1b, "tn", [BF16])
    dpa, dpb, dz_ga, dz_gb = _dmerge(dmerged, z, pa, pb)
    dya = _mm("d_ya", dpa, Wf["w_branch_a"], "nt", [F32])
    dyb = _mm("d_yb", dpb, Wf["w_branch_b"], "nt", [F32])
    G["w_branch_a"] = _mm("g_w_a", ya, dpa, "tn", [BF16])
    G["w_branch_b"] = _mm("g_w_b", yb, dpb, "tn", [BF16])
    dz_q, dz_f, dz_i, dz_g, d_lbl, d_hgw, *sent = _hg_bwd(
        z, o_hg, dya, states, lb_logits, hg_norm_w, side=_sib_send(small3, [G[n] for n in small3]))
    for n, t in zip(small3, sent):
        S_[n] = _chip_sum(n, G[n], t, place)
    dz_aq, dz_ak, dz_av, dbias_win = _at_bwd(z, dyb, bias_win)
    dz = jnp.concatenate([dz_q, dz_f, dz_i, dz_g, dz_aq, dz_ak, dz_av, dz_ga, dz_gb], axis=1)
    g_send, *got3 = _g_w_in_half(u1, dz, place, False, side=_chip_exchange([S_[n] for n in small3]))
    GOT.update(zip(small3, got3))
    g_keep, T_["w_in"] = _g_w_in_half(u1, dz, place, True, side=_sib_send_half(g_send))
    for n in ("w_down", "w_up"):
        S_[n], GOT[n] = _exchange_wait(n, flights[n], g_keep)
    S_["w_in"] = _chip_sum("w_in", g_keep, T_["w_in"], place, kept_rows=True)
    early = [n for n in WEIGHTS if n != "w_in"]
    H_ = {n: _piece_sum(n, S_[n], GOT[n], place) for n in early}
    flight = _exchange_start("w_in", S_["w_in"])
    share_early = _sib_share([H_[n] for n in early])
    share_early.reads.append(flight[-1])
    du1, *shared = _mm("d_u1", dz, Wf["w_in"], "nt", [F32], side=share_early)
    O_ = dict(zip(early, shared))
    grad_x, _, d_nmix = _rms_bwd("norm_mix_bwd", du1, x2, norm_mix_w, dh1)
    d_rel = _bias_window_grad(dbias_win)
    big_out = {}
    for name in early:
        outs = _adam_quarter(name, *big[name], H_[name], O_[name], place)
        big_out[name] = tuple(a[None] for a in outs)
    S_["w_in"], got_in = _exchange_wait("w_in", flight, outs[1])
    H_["w_in"] = _piece_sum("w_in", S_["w_in"], got_in, place)
    (O_["w_in"],) = _run_side("share_w_in", _sib_share([H_["w_in"]]))
    outs = _adam_quarter("w_in", *big["w_in"], H_["w_in"], O_["w_in"], place)
    big_out["w_in"] = tuple(a[None] for a in outs)

    smalls = [("lb_logits", lb_logits, m_lb_logits, v_lb_logits, d_lbl),
              ("hg_norm_w", hg_norm_w, m_hg_norm_w, v_hg_norm_w, d_hgw),
              ("rel_bias", rel_bias, m_rel_bias, v_rel_bias, d_rel),
              ("norm_mix_w", norm_mix_w, m_norm_mix_w, v_norm_mix_w, d_nmix),
              ("norm_mlp_w", norm_mlp_w, m_norm_mlp_w, v_norm_mlp_w, d_nmlp),
              ("norm_final_w", norm_final_w, m_norm_final_w, v_norm_final_w, d_nf)]
    like = [s[1] for s in smalls]
    packed = _small_allreduce_adam(_pack([s[4] for s in smalls]), _pack(like), _pack([s[2] for s in smalls]),
                                   _pack([s[3] for s in smalls]), got_in)
    small_out = {s[0]: vals for s, vals in zip(smalls, zip(*[_unpack(p, like) for p in packed]))}

    loss = lax.psum(loss_part[0, 0], ("x", "y", "c"))
    order = ["w_in", "lb_logits", "hg_norm_w", "rel_bias", "w_branch_a", "w_branch_b", "w_out", "norm_mix_w",
             "norm_mlp_w", "w_up", "w_down", "norm_final_w"]
    res = {**big_out, **small_out}
    return (loss, grad_x.reshape(x.shape), *[res[n][0] for n in order], *[res[n][1] for n in order],
            *[res[n][2] for n in order], *[res[n][3] for n in order])
```

```python
import functools

import jax
import jax.numpy as jnp
from jax import lax
from jax.experimental import pallas as pl
from jax.experimental.pallas import tpu as pltpu

F32 = jnp.float32
BF16 = jnp.bfloat16
HIGHEST = lax.Precision.HIGHEST

D_MODEL = 2048
SEQ = 2048
CHUNK = 64
HG_HEADS = 8
HG_D = 128
AT_HEADS = 16
AT_DH = 64
LEFT = 8
REL_CLIP = 256
D_FF = 8192
EPS = 1e-6
ADAM_LR = 0.001
ADAM_B1 = 0.9
ADAM_B2 = 0.999
ADAM_EPS = 1e-08
ADAM_WD = 0.01
ADAM_STEP = 10

LANE = 128
NEG = -1e30
EXP_CLAMP = 80.0
VMEM_LIMIT = 48 * 1024 * 1024
MM_TM, MM_TN, MM_TK = 1024, 1024, 2816
ROW_TILE = 256
QB = 2 * CHUNK


def _hgw():
    return HG_HEADS * HG_D


def _atw():
    return AT_HEADS * AT_DH


def _cparams(sem):
    return pltpu.CompilerParams(dimension_semantics=sem, vmem_limit_bytes=VMEM_LIMIT)


def _sigmoid(x):
    return jax.nn.sigmoid(x)


def _dot(a, b, dims, precision=None):
    return lax.dot_general(a, b, (dims, ((), ())), preferred_element_type=F32, precision=precision)


def _nn(a, b, precision=None):
    return _dot(a, b, ((1,), (0,)), precision)


def _nt(a, b, precision=None):
    return _dot(a, b, ((1,), (1,)), precision)


def _tn(a, b, precision=None):
    return _dot(a, b, ((0,), (0,)), precision)


class _Side:
    def __init__(self, build, nsem, reads=(), aliased=(), fresh=()):
        self.build, self.nsem = build, nsem
        self.reads, self.aliased, self.fresh = list(reads), list(aliased), list(fresh)

    def operands(self):
        return self.reads + self.aliased

    def in_specs(self):
        return [ANY] * len(self.operands())

    def out_specs(self):
        return [ANY] * (len(self.aliased) + len(self.fresh))

    def out_shape(self):
        return [jax.ShapeDtypeStruct(a.shape, a.dtype) for a in self.aliased] + self.fresh

    def aliases(self, n_in, n_out):
        return {n_in + len(self.reads) + t: n_out + t for t in range(len(self.aliased))}

    def scratch(self):
        return [pltpu.SemaphoreType.DMA((self.nsem,)), pltpu.SemaphoreType.DMA((self.nsem,))]

    def hooks(self, in_refs, out_refs, sems, first, last):
        nr, na = len(self.reads), len(self.aliased)
        args = (in_refs[:nr], out_refs[:na], out_refs[na:], *sems)

        @pl.when(first)
        def _():
            for cp in self.build(*args):
                cp.start()

        @pl.when(last)
        def _():
            for cp in self.build(*args):
                cp.wait()


def _after(*tokens):
    return _Side(lambda *args: [], 1, reads=tokens)


def _side_parts(side):
    if side is None:
        return [], [], [], [], lambda n_in, n_out: {}, []
    return side.operands(), side.in_specs(), side.out_specs(), side.out_shape(), side.aliases, side.scratch()


def _call_with_side(body, name, grid, in_specs, out_specs, out_shape, scratch, sem, operands, side, n_prefetch=0,
                    aliases=None, borrow=None):
    _, _, s_out, s_shape, _, s_scr = _side_parts(side)
    n_in, n_out = n_prefetch + len(in_specs), len(out_specs)
    borrow = borrow or {}
    s_ops, s_alias = [], {}
    if side is not None:
        keep = [t for t in range(len(side.aliased)) if t not in borrow]
        s_ops = side.reads + [side.aliased[t] for t in keep]
        s_alias = {n_in + len(side.reads) + pos: n_out + t for pos, t in enumerate(keep)}
        s_alias.update({n_prefetch + i: n_out + t for t, i in borrow.items()})
    s_in = [ANY] * len(s_ops)
    n_sin, n_sout = len(s_ops), len(s_out)

    def wrapped(*refs):
        a, b, c = n_in + n_sin, n_in + n_sin + n_out, n_in + n_sin + n_out + n_sout
        ids = [pl.program_id(d) for d in range(len(grid))]
        first = functools.reduce(lambda p, q: p & q, [i == 0 for i in ids])
        last = functools.reduce(lambda p, q: p & q, [i == g - 1 for i, g in zip(ids, grid)])
        side.hooks(refs[n_in:a], refs[b:c], refs[-2:], first, last)
        body(*refs[:n_in], *refs[a:b], *refs[c:-2])

    spec = dict(grid=grid, in_specs=in_specs + s_in, out_specs=out_specs + s_out, scratch_shapes=scratch + s_scr)
    if n_prefetch:
        spec = dict(grid_spec=pltpu.PrefetchScalarGridSpec(num_scalar_prefetch=n_prefetch, **spec))
    return pl.pallas_call(
        body if side is None else wrapped, name=name, out_shape=out_shape + s_shape,
        input_output_aliases={**s_alias, **{n_prefetch + i: o for i, o in (aliases or {}).items()}},
        compiler_params=_cparams(sem if side is None else ("arbitrary",) * len(grid)), **spec,
    )(*operands, *s_ops)


def _mm_tk(K):
    if K <= MM_TK:
        return K
    return max(t for t in range(LANE, MM_TK + 1, LANE) if K % t == 0)


def _mm(name, a, b, mode, out_dtypes, extras=(), epilogue=None, side=None):
    if mode == "nn":
        (M, K), (K2, N) = a.shape, b.shape
    elif mode == "nt":
        (M, K), (N, K2) = a.shape, b.shape
    else:
        (K, M), (K2, N) = a.shape, b.shape
    assert K == K2, (name, a.shape, b.shape)
    tm, tn, tk = min(MM_TM, M), min(MM_TN, N), _mm_tk(K)
    assert M % tm == 0 and N % tn == 0 and K % tk == 0, (name, M, N, K)
    ni, nj, nk = M // tm, N // tn, K // tk
    ne, no = len(extras), len(out_dtypes)
    if epilogue is None:
        epilogue = lambda acc: (acc,)
    s_ops, s_in, s_out, s_shape, s_alias, s_scr = _side_parts(side)
    n_in, n_sin, n_sout = 2 + ne, len(s_ops), len(s_out)

    def body(*refs):
        a_ref, b_ref = refs[:2]
        extra_refs = refs[2:n_in]
        out_refs = refs[n_in + n_sin:n_in + n_sin + no]
        rest = refs[n_in + n_sin + no + n_sout:]
        i, j, k = pl.program_id(0), pl.program_id(1), pl.program_id(2)
        if side is not None:
            side.hooks(refs[n_in:n_in + n_sin], refs[n_in + n_sin + no:n_in + n_sin + no + n_sout], rest[-2:],
                       (i == 0) & (j == 0) & (k == 0), (i == ni - 1) & (j == nj - 1) & (k == nk - 1))
        av, bv = a_ref[...].astype(BF16), b_ref[...].astype(BF16)
        prod = _nn(av, bv) if mode == "nn" else _nt(av, bv) if mode == "nt" else _tn(av, bv)

        def finish(acc):
            res = epilogue(acc, *[e[...] for e in extra_refs])
            for o_ref, r in zip(out_refs, res):
                o_ref[...] = r.astype(o_ref.dtype)

        if nk == 1:
            finish(prod)
        else:
            acc_ref = rest[0]

            @pl.when(k == 0)
            def _():
                acc_ref[...] = prod

            @pl.when((k > 0) & (k < nk - 1))
            def _():
                acc_ref[...] += prod

            @pl.when(k == nk - 1)
            def _():
                finish(acc_ref[...] + prod)

    if mode == "nn":
        a_spec = pl.BlockSpec((tm, tk), lambda i, j, k: (i, k))
        b_spec = pl.BlockSpec((tk, tn), lambda i, j, k: (k, j))
    elif mode == "nt":
        a_spec = pl.BlockSpec((tm, tk), lambda i, j, k: (i, k))
        b_spec = pl.BlockSpec((tn, tk), lambda i, j, k: (j, k))
    else:
        a_spec = pl.BlockSpec((tk, tm), lambda i, j, k: (k, i))
        b_spec = pl.BlockSpec((tk, tn), lambda i, j, k: (k, j))
    o_spec = pl.BlockSpec((tm, tn), lambda i, j, k: (i, j))
    sem = ("arbitrary",) * 3 if side is not None else ("parallel", "parallel", "arbitrary")
    outs = pl.pallas_call(
        body, name=name,
        grid=(ni, nj, nk),
        in_specs=[a_spec, b_spec] + [o_spec] * ne + s_in,
        out_specs=[o_spec] * no + s_out,
        out_shape=[jax.ShapeDtypeStruct((M, N), dt) for dt in out_dtypes] + s_shape,
        input_output_aliases=s_alias(n_in, no),
        scratch_shapes=([pltpu.VMEM((tm, tn), F32)] if nk > 1 else []) + s_scr,
        compiler_params=_cparams(sem),
    )(a, b, *extras, *s_ops)
    return outs[0] if len(outs) == 1 else outs


def _row_spec(tr, d):
    return pl.BlockSpec((tr, d), lambda i: (i, 0))


def _vec_spec(d):
    return pl.BlockSpec((1, d), lambda i: (0, 0))


def _rms_fwd(name, x, w):
    T, D = x.shape
    tr = min(ROW_TILE, T)

    def body(x_ref, w_ref, o_ref):
        xf = x_ref[...]
        r = lax.rsqrt(jnp.mean(xf * xf, axis=-1, keepdims=True) + EPS)
        o_ref[...] = (xf * r * w_ref[...]).astype(BF16)

    return pl.pallas_call(
        body, name=name, grid=(T // tr,),
        in_specs=[_row_spec(tr, D), _vec_spec(D)], out_specs=_row_spec(tr, D),
        out_shape=jax.ShapeDtypeStruct((T, D), BF16),
        compiler_params=_cparams(("parallel",)),
    )(x, w)


def _rms_bwd(name, dy, h, w, dres, side=None):
    T, D = h.shape
    tr = min(ROW_TILE, T)

    def body(dy_ref, h_ref, w_ref, dres_ref, dh_ref, dhb_ref, dw_ref):
        @pl.when(pl.program_id(0) == 0)
        def _():
            dw_ref[...] = jnp.zeros_like(dw_ref)

        hf, dyv = h_ref[...], dy_ref[...]
        r = lax.rsqrt(jnp.mean(hf * hf, axis=-1, keepdims=True) + EPS)
        xhat = hf * r
        dw_ref[...] += jnp.sum(dyv * xhat, axis=0, keepdims=True)
        dxh = dyv * w_ref[...]
        dh = dres_ref[...] + r * (dxh - xhat * jnp.mean(dxh * xhat, axis=-1, keepdims=True))
        dh_ref[...] = dh
        dhb_ref[...] = dh.astype(BF16)

    return _call_with_side(
        body, name, (T // tr,),
        [_row_spec(tr, D), _row_spec(tr, D), _vec_spec(D), _row_spec(tr, D)],
        [_row_spec(tr, D), _row_spec(tr, D), _vec_spec(D)],
        [jax.ShapeDtypeStruct((T, D), F32), jax.ShapeDtypeStruct((T, D), BF16), jax.ShapeDtypeStruct((1, D), F32)],
        [], ("arbitrary",), (dy, h, w, dres), side)


def _loss_head(h2, target, w):
    T, D = h2.shape
    tr = min(ROW_TILE, T)

    def body(h_ref, t_ref, w_ref, loss_ref, dh_ref, dhb_ref, dw_ref):
        @pl.when(pl.program_id(0) == 0)
        def _():
            dw_ref[...] = jnp.zeros_like(dw_ref)
            loss_ref[...] = jnp.zeros_like(loss_ref)

        hf, wv = h_ref[...], w_ref[...]
        r = lax.rsqrt(jnp.mean(hf * hf, axis=-1, keepdims=True) + EPS)
        xhat = hf * r
        diff = xhat * wv - t_ref[...]
        loss_ref[...] += 0.5 * jnp.sum(jnp.mean(diff * diff, axis=-1, keepdims=True))
        dyv = diff * (1.0 / D)
        dw_ref[...] += jnp.sum(dyv * xhat, axis=0, keepdims=True)
        dxh = dyv * wv
        dh = r * (dxh - xhat * jnp.mean(dxh * xhat, axis=-1, keepdims=True))
        dh_ref[...] = dh
        dhb_ref[...] = dh.astype(BF16)

    return pl.pallas_call(
        body, name="loss_head", grid=(T // tr,),
        in_specs=[_row_spec(tr, D), _row_spec(tr, D), _vec_spec(D)],
        out_specs=[_vec_spec(LANE), _row_spec(tr, D), _row_spec(tr, D), _vec_spec(D)],
        out_shape=[jax.ShapeDtypeStruct((1, LANE), F32), jax.ShapeDtypeStruct((T, D), F32),
                   jax.ShapeDtypeStruct((T, D), BF16), jax.ShapeDtypeStruct((1, D), F32)],
        compiler_params=_cparams(("arbitrary",)),
    )(h2, target, w)


def _gate_tiles(T, D):
    goff = 4 * _hgw() + 3 * _atw()
    tc = min(1024, D)
    assert goff % tc == 0 and D % tc == 0
    return min(ROW_TILE, T), tc, goff // tc, D // tc


def _merge(z, pa, pb):
    T, D = pa.shape
    tr, tc, g0, nd = _gate_tiles(T, D)

    def body(ga_ref, gb_ref, pa_ref, pb_ref, o_ref):
        o_ref[...] = (_sigmoid(ga_ref[...]) * pa_ref[...] + _sigmoid(gb_ref[...]) * pb_ref[...]).astype(BF16)

    t = pl.BlockSpec((tr, tc), lambda i, j: (i, j))
    return pl.pallas_call(
        body, name="merge", grid=(T // tr, nd),
        in_specs=[pl.BlockSpec((tr, tc), lambda i, j: (i, g0 + j)),
                  pl.BlockSpec((tr, tc), lambda i, j: (i, g0 + nd + j)), t, t],
        out_specs=t, out_shape=jax.ShapeDtypeStruct((T, D), BF16),
        compiler_params=_cparams(("parallel", "parallel")),
    )(z, z, pa, pb)


def _dmerge(dm, z, pa, pb):
    T, D = pa.shape
    tr, tc, g0, nd = _gate_tiles(T, D)

    def body(dm_ref, ga_ref, gb_ref, pa_ref, pb_ref, dpa_ref, dpb_ref, dga_ref, dgb_ref):
        dmv = dm_ref[...]
        sa, sb = _sigmoid(ga_ref[...]), _sigmoid(gb_ref[...])
        dpa_ref[...] = (dmv * sa).astype(BF16)
        dpb_ref[...] = (dmv * sb).astype(BF16)
        dga_ref[...] = (dmv * pa_ref[...] * sa * (1.0 - sa)).astype(BF16)
        dgb_ref[...] = (dmv * pb_ref[...] * sb * (1.0 - sb)).astype(BF16)

    t = pl.BlockSpec((tr, tc), lambda i, j: (i, j))
    return pl.pallas_call(
        body, name="dmerge", grid=(T // tr, nd),
        in_specs=[t, pl.BlockSpec((tr, tc), lambda i, j: (i, g0 + j)),
                  pl.BlockSpec((tr, tc), lambda i, j: (i, g0 + nd + j)), t, t],
        out_specs=[t, t, t, t],
        out_shape=[jax.ShapeDtypeStruct((T, D), BF16)] * 4,
        compiler_params=_cparams(("parallel", "parallel")),
    )(dm, z, z, pa, pb)


def _hg_gates(xq, xf, lb):
    f = _sigmoid(xf)
    g = lb + (1.0 - lb) * f
    sq = _sigmoid(xq)
    return f, g, jnp.log(g), 1.0 - g, sq, xq * sq * (HG_D ** -0.5)


def _split2(x):
    hi = x.astype(BF16)
    return hi, (x - hi.astype(F32)).astype(BF16)


def _tri_sum(tri, x):
    hi, rest = x.astype(BF16), x - x.astype(BF16).astype(F32)
    mid, lo = _split2(rest)
    return _nn(tri, lo) + _nn(tri, mid) + _nn(tri, hi)


def _hg_decays(lg, tri_incl, rowi):
    b = _tri_sum(tri_incl, lg)
    b_last = jnp.sum(lg, axis=0, keepdims=True)
    b_mid = jnp.sum(jnp.where(rowi <= CHUNK // 2, lg, 0.0), axis=0, keepdims=True)
    return b, b_last, b_mid


HG_GROUP = 2


def _hg_in_specs(T):
    ng = HG_HEADS // HG_GROUP
    return [pl.BlockSpec((T, HG_GROUP * HG_D), lambda h, s=s: (0, s * ng + h)) for s in range(4)]


def _hg_fwd(z, lb_logits, hgw, side=None):
    T = z.shape[0]
    H, d, C, G = HG_HEADS, HG_D, CHUNK, HG_GROUP
    nc = T // C

    def body(hq_ref, hf_ref, hi_ref, hg_ref, lbl_ref, w_ref, ya_ref, o_ref, s_ref):
        lb_all = 1.0 / (1.0 + jnp.exp(lbl_ref[1:2, :] - lbl_ref[0:1, :]))
        wv = w_ref[...]
        row = lax.broadcasted_iota(jnp.int32, (C, C), 0)
        col = lax.broadcasted_iota(jnp.int32, (C, C), 1)
        tril = col <= row
        tri_incl = tril.astype(BF16)
        rowi = lax.broadcasted_iota(jnp.int32, (C, G * d), 0)
        lanes = [slice(hh * d, (hh + 1) * d) for hh in range(G)]
        per_head = lambda fn: jnp.concatenate([fn(hh, sl) for hh, sl in enumerate(lanes)], axis=1)
        wv_all = jnp.tile(wv, (1, G))

        def chunk(c, states):
            rows = pl.ds(pl.multiple_of(c * C, C), C)
            xq, xf, v, xg = hq_ref[rows, :], hf_ref[rows, :], hi_ref[rows, :], hg_ref[rows, :]
            _, _, lg, kk, _, q = _hg_gates(xq, xf, lb_all)
            b, b_last, b_mid = _hg_decays(lg, tri_incl, rowi)
            vb, qe = v.astype(BF16), (q * jnp.exp(b)).astype(BF16)
            qt = (q * jnp.exp(b - b_mid)).astype(BF16)
            kt = (kk * jnp.exp(jnp.minimum(b_mid - b, EXP_CLAMP))).astype(BF16)
            kd, e_last = (kk * jnp.exp(b_last - b)).astype(BF16), jnp.exp(b_last)
            for hh, st in enumerate(states):
                s_ref[hh, c] = st
            o = per_head(lambda hh, sl: _nt(qe[:, sl], states[hh].astype(BF16)))
            a = [jnp.where(tril, _nt(qt[:, sl], kt[:, sl]), 0.0).astype(BF16) for sl in lanes]
            o = o + per_head(lambda hh, sl: _nn(a[hh], vb[:, sl]))
            o_ref[rows, :] = o
            r = per_head(lambda hh, sl: jnp.broadcast_to(
                lax.rsqrt(jnp.mean(o[:, sl] * o[:, sl], axis=-1, keepdims=True) + EPS), (C, d)))
            ya_ref[rows, :] = (o * r * wv_all * (xg * _sigmoid(xg))).astype(BF16)
            return tuple(st * e_last[:, sl] + _tn(vb[:, sl], kd[:, sl]) for st, sl in zip(states, lanes))

        lax.fori_loop(0, nc, chunk, tuple(jnp.zeros((d, d), F32) for _ in range(G)))

    heads = pl.BlockSpec((T, G * d), lambda h: (0, h))
    return _call_with_side(
        body, "hg_fwd", (H // G,),
        _hg_in_specs(T) + [pl.BlockSpec((2, G * d), lambda h: (0, h)), pl.BlockSpec((1, d), lambda h: (0, 0))],
        [heads, heads, pl.BlockSpec((G, nc, d, d), lambda h: (h, 0, 0, 0))],
        [jax.ShapeDtypeStruct((T, H * d), BF16), jax.ShapeDtypeStruct((T, H * d), F32),
         jax.ShapeDtypeStruct((H, nc, d, d), F32)],
        [], ("parallel",), (z, z, z, z, lb_logits, hgw), side)


def _hg_bwd(z, o, dya, states, lb_logits, hgw, side=None):
    T = z.shape[0]
    H, d, C, G = HG_HEADS, HG_D, CHUNK, HG_GROUP
    nc = T // C
    scale = HG_D ** -0.5

    def body(hq_ref, hf_ref, hi_ref, hg_ref, o_ref, dy_ref, s_ref, lbl_ref, w_ref,
             dq_ref, df_ref, di_ref, dg_ref, dlbl_ref, dw_ref, acc_ref):
        lb_all = 1.0 / (1.0 + jnp.exp(lbl_ref[1:2, :] - lbl_ref[0:1, :]))
        wv = w_ref[...]
        row = lax.broadcasted_iota(jnp.int32, (C, C), 0)
        col = lax.broadcasted_iota(jnp.int32, (C, C), 1)
        tril = col <= row
        tri_incl = tril.astype(BF16)
        triu_incl = (col >= row).astype(BF16)
        rowi = lax.broadcasted_iota(jnp.int32, (C, G * d), 0)
        lanes = [slice(hh * d, (hh + 1) * d) for hh in range(G)]
        per_head = lambda fn: jnp.concatenate([fn(hh, sl) for hh, sl in enumerate(lanes)], axis=1)
        head_mean = lambda x: per_head(
            lambda hh, sl: jnp.broadcast_to(jnp.mean(x[:, sl], axis=-1, keepdims=True), (C, d)))
        wv_all = jnp.tile(wv, (1, G))
        lb = lb_all
        acc_ref[...] = jnp.zeros_like(acc_ref)

        @pl.when(pl.program_id(0) == 0)
        def _():
            dw_ref[...] = jnp.zeros_like(dw_ref)

        def chunk(i, carry):
            dsts, tail = carry
            c = nc - 1 - i
            rows = pl.ds(pl.multiple_of(c * C, C), C)
            xq, xf, v, xg = hq_ref[rows, :], hf_ref[rows, :], hi_ref[rows, :], hg_ref[rows, :]
            f, g, lg, kk, sq, q = _hg_gates(xq, xf, lb)
            b, b_last, b_mid = _hg_decays(lg, tri_incl, rowi)
            e_b, e_qm, e_km = jnp.exp(b), jnp.exp(b - b_mid), jnp.exp(jnp.minimum(b_mid - b, EXP_CLAMP))
            e_kl, e_last = jnp.exp(b_last - b), jnp.exp(b_last)
            ov, dy = o_ref[rows, :], dy_ref[rows, :]
            r = lax.rsqrt(head_mean(ov * ov) + EPS)
            xhat = ov * r
            sg = _sigmoid(xg)
            dxg = dy * xhat * wv_all * (sg * (1.0 + xg * (1.0 - sg)))
            dyn = dy * (xg * sg)
            acc_ref[0:1, :] += jnp.sum(dyn * xhat, axis=0, keepdims=True)
            dxh = dyn * wv_all
            dof = r * (dxh - xhat * head_mean(dxh * xhat))
            do, vb = dof.astype(BF16), v.astype(BF16)
            qe, kd, qt, kt = (q * e_b).astype(BF16), (kk * e_kl).astype(BF16), (q * e_qm).astype(BF16), (kk * e_km).astype(BF16)
            pm = [jnp.where(tril, _nt(do[:, sl], vb[:, sl]), 0.0).astype(BF16) for sl in lanes]
            am = [jnp.where(tril, _nt(qt[:, sl], kt[:, sl]), 0.0).astype(BF16) for sl in lanes]
            st = [_split2(s_ref[hh, c]) for hh in range(G)]
            ds = [_split2(x) for x in dsts]
            dq_state = per_head(lambda hh, sl: _nn(do[:, sl], st[hh][1]) + _nn(do[:, sl], st[hh][0]))
            dk_state = per_head(lambda hh, sl: _nn(vb[:, sl], ds[hh][1]) + _nn(vb[:, sl], ds[hh][0]))
            dq_intra = per_head(lambda hh, sl: _nn(pm[hh], kt[:, sl]))
            dk_intra = per_head(lambda hh, sl: _tn(pm[hh], qt[:, sl]))
            dv = per_head(lambda hh, sl: _tn(am[hh], do[:, sl]) + _nt(kd[:, sl], ds[hh][0]))
            new_dsts = tuple(x * e_last[:, sl] + _tn(do[:, sl], qe[:, sl]) for x, sl in zip(dsts, lanes))
            dq = dq_state * e_b + dq_intra * e_qm
            dk = dk_intra * e_km + dk_state * e_kl
            db = (qe.astype(F32) * dq_state + qt.astype(F32) * dq_intra
                  - kt.astype(F32) * dk_intra - kd.astype(F32) * dk_state)
            dlg = _tri_sum(triu_incl, db) + tail
            dgate = dlg / g - dk
            acc_ref[1:2, :] += jnp.sum(dgate * (1.0 - f), axis=0, keepdims=True)
            dq_ref[rows, :] = (dq * scale * (sq * (1.0 + xq * (1.0 - sq)))).astype(BF16)
            df_ref[rows, :] = (dgate * (1.0 - lb) * f * (1.0 - f)).astype(BF16)
            di_ref[rows, :] = dv.astype(BF16)
            dg_ref[rows, :] = dxg.astype(BF16)
            return new_dsts, tail + jnp.sum(db, axis=0, keepdims=True)

        lax.fori_loop(0, nc, chunk, (tuple(jnp.zeros((d, d), F32) for _ in range(G)), jnp.zeros((1, G * d), F32)))
        dw_ref[...] += functools.reduce(lambda p, q: p + q, [acc_ref[0:1, sl] for sl in lanes])
        dl0 = acc_ref[1:2, :] * lb_all * (1.0 - lb_all)
        dlbl_ref[0:1, :] = dl0
        dlbl_ref[1:2, :] = -dl0

    heads = pl.BlockSpec((T, G * d), lambda h: (0, h))
    logits = pl.BlockSpec((2, G * d), lambda h: (0, h))
    return _call_with_side(
        body, "hg_bwd", (H // G,),
        _hg_in_specs(T) + [heads, heads, pl.BlockSpec((G, nc, d, d), lambda h: (h, 0, 0, 0)), logits,
                           pl.BlockSpec((1, d), lambda h: (0, 0))],
        [heads, heads, heads, heads, logits, pl.BlockSpec((1, d), lambda h: (0, 0))],
        [jax.ShapeDtypeStruct((T, H * d), BF16)] * 4 + [jax.ShapeDtypeStruct((2, H * d), F32),
                                                        jax.ShapeDtypeStruct((1, d), F32)],
        [pltpu.VMEM((8, G * d), F32)], ("arbitrary",), (z, z, z, z, o, dya, states, lb_logits, hgw), side)


def _at_dims():
    pad = LEFT * CHUNK
    return pad, QB + pad, AT_HEADS * AT_DH // LANE, 4 * _hgw() // LANE


def _rel_of_period():
    pad, W, _, _ = _at_dims()
    n = jnp.arange(QB + W)
    return jnp.clip(pad - jnp.where(n < W, n, n - (QB + W)), -REL_CLIP, REL_CLIP) + REL_CLIP


def _bias_window(rel_bias):
    pad, W, _, _ = _at_dims()
    H, P = rel_bias.shape[0], QB + W
    per = rel_bias[:, _rel_of_period()]
    win = jnp.tile(per, (1, QB))[:, :QB * (P - 1)].reshape(H, QB, P - 1)[:, :, :W]
    t = jnp.arange(QB)[:, None]
    j = jnp.arange(W)[None, :]
    ok = (j // CHUNK >= t // CHUNK) & (j // CHUNK <= t // CHUNK + LEFT)
    return jnp.where(ok[None], win, NEG)


def _bias_window_grad(dbw):
    pad, W, _, _ = _at_dims()
    H, P = dbw.shape[0], QB + W
    flat = jnp.pad(dbw, ((0, 0), (0, 0), (0, P - 1 - W))).reshape(H, QB * (P - 1))
    per = jnp.pad(flat, ((0, 0), (0, QB))).reshape(H, QB, P).sum(axis=1)
    onehot = _rel_of_period()[:, None] == jnp.arange(2 * REL_CLIP + 1)[None, :]
    return jnp.dot(per, onehot.astype(F32), precision=HIGHEST)


def _at_stack(x):
    first = lax.broadcasted_iota(jnp.int32, x.shape, 1) < AT_DH
    return jnp.concatenate([jnp.where(first, x, 0.0), jnp.where(first, 0.0, x)], axis=0).astype(BF16)


def _at_unstack(x):
    first = lax.broadcasted_iota(jnp.int32, (QB, LANE), 1) < AT_DH
    return jnp.where(first, x[:QB], x[QB:])


def _at_softmax(qs, kw, bias_ref, qi):
    pad, W, _, _ = _at_dims()
    s = _nt(qs, kw) * (AT_DH ** -0.5) + bias_ref[...].reshape(2 * QB, W)
    valid = lax.broadcasted_iota(jnp.int32, (2 * QB, W), 1) + qi * QB >= pad
    s = jnp.where(valid, s, NEG)
    e = jnp.exp(s - jnp.max(s, axis=-1, keepdims=True))
    return e / jnp.sum(e, axis=-1, keepdims=True)


def _at_fwd(z, bias_win, side=None):
    T = z.shape[0]
    pad, W, HP, c0 = _at_dims()
    nq = T // QB

    def body(q_ref, k_ref, v_ref, bias_ref, o_ref, kpad, vpad):
        qi = pl.program_id(1)

        @pl.when(qi == 0)
        def _():
            kpad[0:pad, :] = jnp.zeros((pad, LANE), BF16)
            vpad[0:pad, :] = jnp.zeros((pad, LANE), BF16)
            kpad[pad:, :] = k_ref[...].astype(BF16)
            vpad[pad:, :] = v_ref[...].astype(BF16)

        win = pl.ds(pl.multiple_of(qi * QB, QB), W)
        kw, vw = kpad[win, :], vpad[win, :]
        p = _at_softmax(_at_stack(q_ref[...]), kw, bias_ref, qi)
        o_ref[...] = _at_unstack(_nn(p.astype(BF16), vw)).astype(BF16)

    full = lambda s: pl.BlockSpec((T, LANE), lambda hp, qi, s=s: (0, c0 + s * HP + hp))
    return _call_with_side(
        body, "at_fwd", (HP, nq),
        [pl.BlockSpec((QB, LANE), lambda hp, qi: (qi, c0 + hp)), full(1), full(2),
         pl.BlockSpec((2, QB, W), lambda hp, qi: (hp, 0, 0))],
        [pl.BlockSpec((QB, LANE), lambda hp, qi: (qi, hp))],
        [jax.ShapeDtypeStruct((T, HP * LANE), BF16)],
        [pltpu.VMEM((T + pad, LANE), BF16)] * 2, ("parallel", "arbitrary"), (z, z, z, bias_win), side)


def _at_bwd(z, dyb, bias_win, side=None):
    T = z.shape[0]
    pad, W, HP, c0 = _at_dims()
    nq = T // QB
    scale = AT_DH ** -0.5

    def body(q_ref, k_ref, v_ref, do_ref, bias_ref, dq_ref, dk_ref, dv_ref, dbias_ref, kpad, vpad, dkpad, dvpad):
        qi = pl.program_id(1)

        @pl.when(qi == 0)
        def _():
            kpad[0:pad, :] = jnp.zeros((pad, LANE), BF16)
            vpad[0:pad, :] = jnp.zeros((pad, LANE), BF16)
            kpad[pad:, :] = k_ref[...].astype(BF16)
            vpad[pad:, :] = v_ref[...].astype(BF16)
            dkpad[...] = jnp.zeros_like(dkpad)
            dvpad[...] = jnp.zeros_like(dvpad)
            dbias_ref[...] = jnp.zeros_like(dbias_ref)

        win = pl.ds(pl.multiple_of(qi * QB, QB), W)
        kw, vw = kpad[win, :], vpad[win, :]
        qs, dos = _at_stack(q_ref[...]), _at_stack(do_ref[...])
        p = _at_softmax(qs, kw, bias_ref, qi)
        dp = _nt(dos, vw)
        ds = p * (dp - jnp.sum(p * dp, axis=-1, keepdims=True))
        dbias_ref[...] += ds.reshape(2, QB, W)
        dsb = (ds * scale).astype(BF16)
        dq_ref[...] = _at_unstack(_nn(dsb, kw)).astype(BF16)
        dkpad[win, :] += _tn(dsb, qs)
        dvpad[win, :] += _tn(p.astype(BF16), dos)

        @pl.when(qi == nq - 1)
        def _():
            dk_ref[...] = dkpad[pad:, :].astype(BF16)
            dv_ref[...] = dvpad[pad:, :].astype(BF16)

    full = lambda s: pl.BlockSpec((T, LANE), lambda hp, qi, s=s: (0, c0 + s * HP + hp))
    blk = pl.BlockSpec((QB, LANE), lambda hp, qi: (qi, hp))
    col = pl.BlockSpec((T, LANE), lambda hp, qi: (0, hp))
    bw = pl.BlockSpec((2, QB, W), lambda hp, qi: (hp, 0, 0))
    return _call_with_side(
        body, "at_bwd", (HP, nq),
        [pl.BlockSpec((QB, LANE), lambda hp, qi: (qi, c0 + hp)), full(1), full(2), blk, bw],
        [blk, col, col, bw],
        [jax.ShapeDtypeStruct((T, HP * LANE), BF16)] * 3 + [jax.ShapeDtypeStruct(bias_win.shape, F32)],
        [pltpu.VMEM((T + pad, LANE), BF16)] * 2 + [pltpu.VMEM((T + pad, LANE), F32)] * 2,
        ("parallel", "arbitrary"), (z, z, z, dyb, bias_win), side)


def _piece_tiles(name, full_shape):
    pr, pc = _piece_shape(name, full_shape)
    tr = min(ROW_TILE, pr)
    assert pr % tr == 0
    nt = pr // tr
    if name in ROW_SHARDED:
        return tr, nt, lambda q, half, i: ((2 * q + half) * nt + i, 0)
    return tr, nt, lambda q, half, i: (half * nt + i, q)


def _cast_into_full(name, wq, place, side=None):
    full = _full_shape(name, wq.shape)
    pc = wq.shape[1]
    tr, nt, at = _piece_tiles(name, full)

    def body(place_ref, w_ref, o_ref):
        o_ref[...] = w_ref[...].astype(BF16)

    outs = _call_with_side(
        body, "cast_" + name, (2, nt), [pl.BlockSpec((tr, pc), lambda h, i, s: (h * nt + i, 0))],
        [pl.BlockSpec((tr, pc), lambda h, i, s: at(s[0], h, i))], [jax.ShapeDtypeStruct(full, BF16)],
        [], ("parallel", "parallel"), (place, wq), side, n_prefetch=1)
    return outs[0] if side is None else outs


def _g_w_in_half(u1, dz, place, own, side=None):
    T, K = u1.shape
    N = dz.shape[1]
    hk, tn = K // 2, min(MM_TN, N)
    half = (lambda s: s[1]) if own else (lambda s: 1 - s[1])

    def body(place_ref, a_ref, b_ref, o_ref):
        o_ref[...] = _tn(a_ref[...], b_ref[...]).astype(BF16)

    outs = _call_with_side(
        body, "g_w_in_keep" if own else "g_w_in_send", (N // tn,),
        [pl.BlockSpec((T, hk), lambda j, s: (0, half(s))), pl.BlockSpec((T, tn), lambda j, s: (0, j))],
        [pl.BlockSpec((hk, tn), lambda j, s: (0, j))], [jax.ShapeDtypeStruct((hk, N), BF16)],
        [], ("parallel",), (place, u1, dz), side, n_prefetch=1)
    return outs[0] if side is None else outs


def _chip_sum(name, grad, theirs, place, kept_rows=False):
    pr, pc = theirs.shape[1:]
    tr, nt, at = _piece_tiles(name, (2 * grad.shape[0], grad.shape[1]) if kept_rows else grad.shape)
    if kept_rows:
        at = lambda q, half, i: (i, q)

    def body(place_ref, g_ref, t_ref, o_ref):
        o_ref[...] = (g_ref[...].astype(F32) + t_ref[...].astype(F32)).astype(BF16)

    piece = pl.BlockSpec((None, tr, pc), lambda q, i, s: (q, i, 0))
    return pl.pallas_call(
        body, name="chip_sum_" + name,
        grid_spec=pltpu.PrefetchScalarGridSpec(
            num_scalar_prefetch=1, grid=(4, nt),
            in_specs=[pl.BlockSpec((tr, pc), lambda q, i, s: at(q, s[1], i)), piece], out_specs=piece),
        out_shape=jax.ShapeDtypeStruct(theirs.shape, BF16),
        compiler_params=_cparams(("parallel", "parallel")),
    )(place, grad, theirs)


def _piece_sum(name, chip_sums, got, place):
    pr, pc = chip_sums.shape[1:]
    tr = min(ROW_TILE, pr)

    def body(place_ref, own_ref, got_ref, o_ref):
        o_ref[...] = (own_ref[...].astype(F32) + got_ref[0].astype(F32) + got_ref[1].astype(F32)
                      + got_ref[2].astype(F32))

    return pl.pallas_call(
        body, name="piece_sum_" + name,
        grid_spec=pltpu.PrefetchScalarGridSpec(
            num_scalar_prefetch=1, grid=(pr // tr,),
            in_specs=[pl.BlockSpec((None, tr, pc), lambda i, s: (s[0], i, 0)),
                      pl.BlockSpec((3, tr, pc), lambda i, s: (0, i, 0))],
            out_specs=pl.BlockSpec((tr, pc), lambda i, s: (i, 0))),
        out_shape=jax.ShapeDtypeStruct((pr, pc), F32),
        compiler_params=_cparams(("parallel",)),
    )(place, chip_sums, got)


def _adam_quarter(name, w, m, v, g_mine, g_sib, place, side=None):
    pr, pc = g_mine.shape
    tr = min(ROW_TILE // 2, pr)
    nt = pr // tr

    def body(place_ref, w_ref, m_ref, v_ref, gm_ref, gs_ref, go_ref, d_ref, mo_ref, vo_ref):
        g = jnp.where(pl.program_id(0) == place_ref[1], gm_ref[...], gs_ref[...])
        delta, mn, vn = _adam_math(w_ref[...], g, m_ref[...], v_ref[...])
        go_ref[...] = g
        d_ref[...] = delta
        mo_ref[...] = mn
        vo_ref[...] = vn

    quarter = pl.BlockSpec((tr, pc), lambda h, i, s: (h * nt + i, 0))
    mine = pl.BlockSpec((tr, pc), lambda h, i, s: (jnp.where(h == s[1], i, 0), 0))
    sib = pl.BlockSpec((tr, pc), lambda h, i, s: (jnp.where(h == s[1], 0, i), 0))
    return _call_with_side(
        body, "adam_" + name, (2, nt), [quarter, quarter, quarter, mine, sib], [quarter] * 4,
        [jax.ShapeDtypeStruct(w.shape, F32)] * 4, [], ("parallel", "parallel"),
        (place, w, m, v, g_mine, g_sib), side, n_prefetch=1)


def _adam_math(w, g, m, v):
    m = ADAM_B1 * m + (1.0 - ADAM_B1) * g
    v = ADAM_B2 * v + (1.0 - ADAM_B2) * (g * g)
    m_hat = m / (1.0 - ADAM_B1 ** ADAM_STEP)
    v_hat = v / (1.0 - ADAM_B2 ** ADAM_STEP)
    return -ADAM_LR * (m_hat / (jnp.sqrt(v_hat) + ADAM_EPS) + ADAM_WD * w), m, v


WEIGHTS = ("w_in", "w_branch_a", "w_branch_b", "w_out", "w_up", "w_down")
ROW_SHARDED = ("w_out", "w_down")
ANY = pl.BlockSpec(memory_space=pl.ANY)
MESH = pl.DeviceIdType.MESH


def _place():
    x, y, c = lax.axis_index("x"), lax.axis_index("y"), lax.axis_index("c")
    chips = [(1 - x, y), (x, 1 - y), (1 - x, 1 - y)]
    return x, y, c, 2 * x + y, chips, [2 * cx + cy for cx, cy in chips]


def _piece(full_ref, name, q, half):
    K, N = full_ref.shape
    if name in ROW_SHARDED:
        rows = K // 8
        return full_ref.at[pl.ds(q * (2 * rows) + half * rows, rows), :]
    return full_ref.at[pl.ds(half * (K // 2), K // 2), pl.ds(q * (N // 4), N // 4)]


def _piece_shape(name, full_shape):
    K, N = full_shape
    return (K // 8, N) if name in ROW_SHARDED else (K // 2, N // 4)


def _full_shape(name, quarter_shape):
    Kq, Nq = quarter_shape
    return (4 * Kq, Nq) if name in ROW_SHARDED else (Kq, 4 * Nq)


def _remote(src, dst, send_sem, recv_sem, device):
    return pltpu.make_async_remote_copy(src_ref=src, dst_ref=dst, send_sem=send_sem, recv_sem=recv_sem,
                                        device_id=device, device_id_type=MESH)


def _z_part(u1, w_in, z_prev, place, k0, count, side=None):
    T, K = u1.shape
    N = w_in.shape[1]
    nq = N // 4
    tn = nq // 2 if (nq // 2) % LANE == 0 else nq
    tm = min(MM_TM, T)
    per = nq // tn
    col = lambda g, j, s: (s[0] ^ (k0 + g)) * per + j
    ins = [pl.BlockSpec((tm, K), lambda g, i, j, s: (i, 0)), pl.BlockSpec((K, tn), lambda g, i, j, s: (0, col(g, j, s)))]
    operands = [place, u1, w_in]
    if z_prev is not None:
        ins.append(ANY)
        operands.append(z_prev)

    def body(place_ref, a_ref, b_ref, *rest):
        rest[-1][...] = _nn(a_ref[...], b_ref[...])

    return _call_with_side(
        body, "z_part_%d" % k0, (count, T // tm, per), ins,
        [pl.BlockSpec((tm, tn), lambda g, i, j, s: (i, col(g, j, s)))], [jax.ShapeDtypeStruct((T, N), F32)],
        [], ("parallel",) * 3, tuple(operands), side, n_prefetch=1, aliases={} if z_prev is None else {2: 0},
        borrow={0: 1} if side is not None and side.aliased and side.aliased[0] is w_in else None)


def _rows(ref, span):
    return ref if span is None else ref.at[pl.ds(span[0], span[1]), :]


def _ici_near(names, fulls, rows=None):
    rows = rows or [None] * len(names)

    def build(reads, aliased, fresh, send_sems, recv_sems, off=0):
        _, _, c, p, chips, _ = _place()
        out = []
        for i, (ref, name) in enumerate(zip(aliased, names)):
            mine = _rows(_piece(ref, name, p, c), rows[i])
            for j, chip in enumerate(chips[:2]):
                k = off + 2 * i + j
                out.append(_remote(mine, mine, send_sems.at[k], recv_sems.at[k], (*chip, c)))
        return out

    return _Side(build, 2 * len(names), aliased=fulls)


def _ici_far(names, fulls, rows=None):
    rows = rows or [None] * len(names)

    def build(reads, aliased, fresh, send_sems, recv_sems, off=0):
        x, y, c, _, _, chip_ids = _place()
        south = c == 0
        src_chip = jnp.where(south, chip_ids[0], chip_ids[1])
        target = (jnp.where(south, x, 1 - x), jnp.where(south, 1 - y, y), c)
        out = []
        for i, (ref, name) in enumerate(zip(aliased, names)):
            landed = _rows(_piece(ref, name, src_chip, c), rows[i])
            out.append(_remote(landed, landed, send_sems.at[off + i], recv_sems.at[off + i], target))
        return out

    return _Side(build, len(names), aliased=fulls)


def _d2d_gather(names, fulls, which=(0, 1, 2)):
    def build(reads, aliased, fresh, send_sems, recv_sems, off=0):
        x, y, c, _, _, chip_ids = _place()
        out = []
        for i, (ref, name) in enumerate(zip(aliased, names)):
            for n, j in enumerate(which):
                landed, k = _piece(ref, name, chip_ids[j], c), off + len(which) * i + n
                out.append(_remote(landed, landed, send_sems.at[k], recv_sems.at[k], (x, y, 1 - c)))
        return out

    return _Side(build, len(which) * len(names), aliased=fulls)


def _sib_send(names, grads):
    def build(reads, aliased, fresh, send_sems, recv_sems, off=0):
        x, y, c, _, _, _ = _place()
        out = []
        for i, name in enumerate(names):
            for q in range(4):
                k = off + 4 * i + q
                out.append(_remote(_piece(reads[i], name, q, 1 - c), fresh[i].at[q], send_sems.at[k], recv_sems.at[k],
                                   (x, y, 1 - c)))
        return out

    shapes = [jax.ShapeDtypeStruct((4,) + _piece_shape(name, g.shape), BF16) for name, g in zip(names, grads)]
    return _Side(build, 4 * len(names), reads=grads, fresh=shapes)


def _sib_send_half(sent):
    K2, N = sent.shape

    def build(reads, aliased, fresh, send_sems, recv_sems, off=0):
        x, y, c, _, _, _ = _place()
        return [_remote(reads[0].at[:, pl.ds(q * (N // 4), N // 4)], fresh[0].at[q], send_sems.at[off + q],
                        recv_sems.at[off + q], (x, y, 1 - c)) for q in range(4)]

    return _Side(build, 4, reads=[sent], fresh=[jax.ShapeDtypeStruct((4, K2, N // 4), BF16)])


def _chip_exchange(chip_sums, rows=None, got=None):
    rows = rows or [None] * len(chip_sums)

    def build(reads, aliased, fresh, send_sems, recv_sems, off=0):
        _, _, c, _, chips, chip_ids = _place()
        out = []
        for i in range(len(chip_sums)):
            for j, (chip, cid) in enumerate(zip(chips, chip_ids)):
                k = off + 3 * i + j
                out.append(_remote(_rows(reads[i].at[cid], rows[i]), _rows((aliased or fresh)[i].at[j], rows[i]),
                                   send_sems.at[k], recv_sems.at[k], (*chip, c)))
        return out

    if got is not None:
        return _Side(build, 3 * len(chip_sums), reads=chip_sums, aliased=got)
    shapes = [jax.ShapeDtypeStruct((3,) + s.shape[1:], BF16) for s in chip_sums]
    return _Side(build, 3 * len(chip_sums), reads=chip_sums, fresh=shapes)


HBM = pl.BlockSpec(memory_space=pltpu.HBM)
SEM = pl.BlockSpec(memory_space=pltpu.SEMAPHORE)


def _exchange_copies(s_refs, land_refs, send_sems, recv_sems):
    _, _, c, _, chips, chip_ids = _place()
    return [_remote(s_ref.at[cid], land_ref.at[j], send_sems.at[3 * i + j], recv_sems.at[3 * i + j], (*chip, c))
            for i, (s_ref, land_ref) in enumerate(zip(s_refs, land_refs))
            for j, (chip, cid) in enumerate(zip(chips, chip_ids))]


def _exchange_start(name, chip_sums):
    n = len(chip_sums)

    def body(*refs):
        for cp in _exchange_copies(refs[:n], refs[n:2 * n], refs[2 * n], refs[2 * n + 1]):
            cp.start()
        refs[-1][...] = jnp.zeros_like(refs[-1])

    lands = [jax.ShapeDtypeStruct((3,) + s.shape[1:], s.dtype) for s in chip_sums]
    hbm = lambda a: pltpu.with_memory_space_constraint(a, pltpu.HBM)
    outs = pl.pallas_call(
        body, name="exchange_start_" + name,
        out_shape=(pltpu.SemaphoreType.DMA((3 * n,)), pltpu.SemaphoreType.DMA((3 * n,)),
                   *[pltpu.HBM(a.shape, a.dtype) for a in chip_sums + lands], jax.ShapeDtypeStruct((8, LANE), F32)),
        in_specs=(HBM,) * (2 * n), out_specs=(SEM, SEM) + (HBM,) * (2 * n) + (pl.BlockSpec(memory_space=pltpu.VMEM),),
        input_output_aliases={i: 2 + i for i in range(2 * n)},
        compiler_params=pltpu.CompilerParams(has_side_effects=pltpu.SideEffectType.DATAFLOW_SIDE_EFFECTING),
    )(*[hbm(s) for s in chip_sums], *[hbm(lax.empty(a.shape, a.dtype)) for a in lands])
    return outs[0], outs[1], list(outs[2:2 + n]), list(outs[2 + n:2 + 2 * n]), outs[-1]


def _exchange_wait(name, flight, after):
    send_sems, recv_sems, s_thru, land_thru, _ = flight
    n = len(s_thru)

    def body(*refs):
        for cp in _exchange_copies(refs[:n], refs[n:2 * n], refs[2 * n], refs[2 * n + 1]):
            cp.wait_send()
            cp.wait_recv()

    outs = pl.pallas_call(
        body, name="exchange_wait_" + name,
        out_shape=tuple(pltpu.HBM(a.shape, a.dtype) for a in s_thru + land_thru),
        in_specs=(HBM,) * (2 * n) + (SEM, SEM, ANY), out_specs=(HBM,) * (2 * n),
        input_output_aliases={i: i for i in range(2 * n)},
        compiler_params=pltpu.CompilerParams(has_side_effects=pltpu.SideEffectType.DATAFLOW_SIDE_EFFECTING),
    )(*s_thru, *land_thru, send_sems, recv_sems, after)
    return list(outs[:n]), list(outs[n:])


def _sib_share(halves):
    def build(reads, aliased, fresh, send_sems, recv_sems, off=0):
        x, y, c, _, _, _ = _place()
        return [_remote(reads[i], fresh[i], send_sems.at[off + i], recv_sems.at[off + i], (x, y, 1 - c))
                for i in range(len(halves))]

    return _Side(build, len(halves), reads=halves, fresh=[jax.ShapeDtypeStruct(h.shape, F32) for h in halves])


def _join(a, b):
    def build(reads, aliased, fresh, send_sems, recv_sems, off=0):
        ra, aa, fa = len(a.reads), len(a.aliased), len(a.fresh)
        return (a.build(reads[:ra], aliased[:aa], fresh[:fa], send_sems, recv_sems, off)
                + b.build(reads[ra:], aliased[aa:], fresh[fa:], send_sems, recv_sems, off + a.nsem))

    return _Side(build, a.nsem + b.nsem, a.reads + b.reads, a.aliased + b.aliased, a.fresh + b.fresh)


def _run_side(name, side):
    nr, na = len(side.reads), len(side.aliased)

    def body(*refs):
        n_in, n_out = nr + na, na + len(side.fresh)
        outs = refs[n_in:n_in + n_out]
        copies = side.build(refs[:nr], outs[:na], outs[na:], *refs[-2:])
        for cp in copies:
            cp.start()
        for cp in copies:
            cp.wait()

    return pl.pallas_call(
        body, name=name, in_specs=side.in_specs(), out_specs=side.out_specs(), out_shape=side.out_shape(),
        input_output_aliases=side.aliases(0, 0), scratch_shapes=side.scratch(),
    )(*side.operands())


def _small_allreduce_adam(gpart, w, m, v, after):
    R = gpart.shape[0]

    def body(g_ref, w_ref, m_ref, v_ref, after_ref, go_ref, d_ref, mo_ref, vo_ref, buf, send_sems, recv_sems):
        x, y, c = lax.axis_index("x"), lax.axis_index("y"), lax.axis_index("c")
        me = 4 * x + 2 * y + c
        buf[me] = g_ref[...]
        copies = []
        for k in range(1, 8):
            fx, fy, fc = (k >> 2) & 1, (k >> 1) & 1, k & 1
            peer = (1 - x if fx else x, 1 - y if fy else y, 1 - c if fc else c)
            cp = _remote(g_ref, buf.at[me], send_sems.at[k - 1], recv_sems.at[k - 1], peer)
            cp.start()
            copies.append((cp, 4 * peer[0] + 2 * peer[1] + peer[2]))
        for k, (cp, pid) in enumerate(copies):
            _remote(g_ref, buf.at[pid], send_sems.at[k], recv_sems.at[k], (x, y, c)).wait_recv()
        for cp, _ in copies:
            cp.wait_send()
        g = buf[0]
        for d in range(1, 8):
            g = g + buf[d]
        delta, mn, vn = _adam_math(w_ref[...], g, m_ref[...], v_ref[...])
        go_ref[...] = g
        d_ref[...] = delta
        mo_ref[...] = mn
        vo_ref[...] = vn

    vm = pl.BlockSpec(memory_space=pltpu.VMEM)
    return pl.pallas_call(
        body, name="small_allreduce_adam",
        in_specs=[vm] * 4 + [ANY], out_specs=[vm] * 4,
        out_shape=[jax.ShapeDtypeStruct((R, LANE), F32)] * 4,
        scratch_shapes=[pltpu.VMEM((8, R, LANE), F32), pltpu.SemaphoreType.DMA((7,)), pltpu.SemaphoreType.DMA((7,))],
    )(gpart, w, m, v, after)


def _pack(arrs):
    flat = jnp.concatenate([a.reshape(-1).astype(F32) for a in arrs])
    rows = -(-flat.shape[0] // (8 * LANE)) * 8
    return jnp.pad(flat, (0, rows * LANE - flat.shape[0])).reshape(rows, LANE)


def _unpack(packed, like):
    flat, out, off = packed.reshape(-1), [], 0
    for a in like:
        out.append(flat[off:off + a.size].reshape(a.shape))
        off += a.size
    return out


def kernel(x, w_in, lb_logits, hg_norm_w, rel_bias, w_branch_a, w_branch_b, w_out, norm_mix_w, norm_mlp_w, w_up, w_down, norm_final_w, loss_target, m_w_in, m_lb_logits, m_hg_norm_w, m_rel_bias, m_w_branch_a, m_w_branch_b, m_w_out, m_norm_mix_w, m_norm_mlp_w, m_w_up, m_w_down, m_norm_final_w, v_w_in, v_lb_logits, v_hg_norm_w, v_rel_bias, v_w_branch_a, v_w_branch_b, v_w_out, v_norm_mix_w, v_norm_mlp_w, v_w_up, v_w_down, v_norm_final_w):
    T, D = x.shape[1], x.shape[2]
    x2, tgt = x.reshape(T, D), loss_target.reshape(T, D)
    big = dict(w_in=(w_in, m_w_in, v_w_in), w_branch_a=(w_branch_a, m_w_branch_a, v_w_branch_a),
               w_branch_b=(w_branch_b, m_w_branch_b, v_w_branch_b), w_out=(w_out, m_w_out, v_w_out),
               w_up=(w_up, m_w_up, v_w_up), w_down=(w_down, m_w_down, v_w_down))
    big = {k: tuple(a[0] for a in v) for k, v in big.items()}
    nfw = norm_final_w.reshape(1, D)

    place = jnp.stack([2 * lax.axis_index("x") + lax.axis_index("y"), lax.axis_index("c")]).astype(jnp.int32)
    small3 = ["w_branch_a", "w_branch_b", "w_out"]

    def span(name, lo, hi):
        pr = big[name][0].shape[0] // 2
        return (pr * lo // 16, pr * (hi - lo) // 16)

    Wf = {"w_in": _cast_into_full("w_in", big["w_in"][0], place)}
    for name, lo, hi in (("w_up", 0, 2), ("w_down", 2, 4)):
        Wf[name], Wf["w_in"] = _cast_into_full(
            name, big[name][0], place, side=_ici_near(["w_in"], [Wf["w_in"]], rows=[span("w_in", lo, hi)]))
    for name in small3:
        Wf[name] = _cast_into_full(name, big[name][0], place)

    ab = ["w_branch_a", "w_branch_b"]
    u1 = _rms_fwd("norm_mix", x2, norm_mix_w)
    z, Wf["w_in"] = _z_part(u1, Wf["w_in"], None, place, 0, 1,
                            side=_ici_near(["w_in"], [Wf["w_in"]], rows=[span("w_in", 4, 16)]))
    (Wf["w_in"],) = _run_side("pass_w_in_near", _d2d_gather(["w_in"], [Wf["w_in"]], which=(0, 1)))
    z, Wf["w_in"] = _z_part(u1, Wf["w_in"], z, place, 1, 2, side=_ici_far(["w_in"], [Wf["w_in"]]))
    (Wf["w_in"],) = _run_side("pass_w_in_far", _d2d_gather(["w_in"], [Wf["w_in"]], which=(2,)))
    z, *moved = _z_part(u1, Wf["w_in"], z, place, 3, 1, side=_ici_near(ab, [Wf[n] for n in ab]))
    Wf.update(zip(ab, moved))
    ya, o_hg, states, *moved = _hg_fwd(
        z, lb_logits, hg_norm_w,
        side=_join(_ici_near(["w_out", "w_up", "w_down"], [Wf[n] for n in ("w_out", "w_up", "w_down")],
                             rows=[None, None, span("w_down", 0, 6)]),
                   _ici_far(ab, [Wf[n] for n in ab])))
    Wf.update(zip(["w_out", "w_up", "w_down"] + ab, moved))
    bias_win = _bias_window(rel_bias[0])
    yb, *moved = _at_fwd(
        z, bias_win,
        side=_join(_join(_ici_far(["w_out", "w_up"], [Wf["w_out"], Wf["w_up"]]),
                         _ici_near(["w_down"], [Wf["w_down"]], rows=[span("w_down", 6, 16)])),
                   _d2d_gather(ab, [Wf[n] for n in ab])))
    Wf.update(zip(["w_out", "w_up", "w_down"] + ab, moved))
    pa, Wf["w_out"] = _mm("branch_a", ya, Wf["w_branch_a"], "nn", [F32], side=_d2d_gather(["w_out"], [Wf["w_out"]]))
    pb = _mm("branch_b", yb, Wf["w_branch_b"], "nn", [F32])
    merged = _merge(z, pa, pb)
    add = lambda acc, res: (acc + res,)
    h1, Wf["w_up"] = _mm("out_proj", merged, Wf["w_out"], "nn", [F32], extras=[x2], epilogue=add,
                         side=_d2d_gather(["w_up"], [Wf["w_up"]]))
    u2 = _rms_fwd("norm_mlp", h1, norm_mlp_w)
    relu2 = lambda acc: (acc, jnp.square(jnp.maximum(acc, 0.0)))
    a_pre, act, Wf["w_down"] = _mm("mlp_up", u2, Wf["w_up"], "nn", [F32, BF16], epilogue=relu2,
                                   side=_ici_far(["w_down"], [Wf["w_down"]]))
    (Wf["w_down"],) = _run_side("pass_w_down", _d2d_gather(["w_down"], [Wf["w_down"]]))
    h2 = _mm("mlp_down", act, Wf["w_down"], "nn", [F32], extras=[h1], epilogue=add)
    loss_part, dh2, dh2b, d_nf = _loss_head(h2, tgt, nfw)

    drelu2 = lambda acc, a: (acc * (2.0 * jnp.maximum(a, 0.0)),)
    da = _mm("d_act", dh2b, Wf["w_down"], "nt", [BF16], extras=[a_pre], epilogue=drelu2)
    G = {}
    G["w_down"] = _mm("g_w_down", act, dh2b, "tn", [BF16])
    G["w_up"] = _mm("g_w_up", u2, da, "tn", [BF16])
    T_, S_, GOT = {}, {}, {}
    du2, T_["w_down"], T_["w_up"] = _mm("d_u2", da, Wf["w_up"], "nt", [F32],
                                        side=_sib_send(["w_down", "w_up"], [G["w_down"], G["w_up"]]))
    mlp2 = ["w_down", "w_up"]
    flight_mlp = _exchange_start("mlp", [_chip_sum(n, G[n], T_[n], place) for n in mlp2])
    dh1, dh1b, d_nmlp = _rms_bwd("norm_mlp_bwd", du2, h1, norm_mlp_w, dh2, side=_after(flight_mlp[-1]))
    dmerged = _mm("d_merged", dh1b, Wf["w_out"], "nt", [F32])
    G["w_out"] = _mm("g_w_out", merged, dh1b, "tn", [BF16])
    dpa, dpb, dz_ga, dz_gb = _dmerge(dmerged, z, pa, pb)
    dya = _mm("d_ya", dpa, Wf["w_branch_a"], "nt", [F32])
    dyb = _mm("d_yb", dpb, Wf["w_branch_b"], "nt", [F32])
    G["w_branch_a"] = _mm("g_w_a", ya, dpa, "tn", [BF16])
    G["w_branch_b"] = _mm("g_w_b", yb, dpb, "tn", [BF16])
    dz_q, dz_f, dz_i, dz_g, d_lbl, d_hgw, *sent = _hg_bwd(
        z, o_hg, dya, states, lb_logits, hg_norm_w, side=_sib_send(small3, [G[n] for n in small3]))
    flight_small = _exchange_start("small", [_chip_sum(n, G[n], t, place) for n, t in zip(small3, sent)])
    dz_aq, dz_ak, dz_av, dbias_win = _at_bwd(z, dyb, bias_win, side=_after(flight_small[-1]))
    dz = jnp.concatenate([dz_q, dz_f, dz_i, dz_g, dz_aq, dz_ak, dz_av, dz_ga, dz_gb], axis=1)
    g_send = _g_w_in_half(u1, dz, place, False)
    g_keep, T_["w_in"] = _g_w_in_half(u1, dz, place, True, side=_sib_send_half(g_send))
    for names, flight in ((mlp2, flight_mlp), (small3, flight_small)):
        sums, got = _exchange_wait("_".join(names), flight, g_keep)
        S_.update(zip(names, sums))
        GOT.update(zip(names, got))
    S_["w_in"] = _chip_sum("w_in", g_keep, T_["w_in"], place, kept_rows=True)
    early = [n for n in WEIGHTS if n != "w_in"]
    H_ = {n: _piece_sum(n, S_[n], GOT[n], place) for n in early}
    flight = _exchange_start("w_in", [S_["w_in"]])
    share_early = _sib_share([H_[n] for n in early])
    share_early.reads.append(flight[-1])
    du1, *shared = _mm("d_u1", dz, Wf["w_in"], "nt", [F32], side=share_early)
    O_ = dict(zip(early, shared))
    grad_x, _, d_nmix = _rms_bwd("norm_mix_bwd", du1, x2, norm_mix_w, dh1)
    d_rel = _bias_window_grad(dbias_win)
    big_out = {}
    for name in early:
        outs = _adam_quarter(name, *big[name], H_[name], O_[name], place)
        big_out[name] = tuple(a[None] for a in outs)
    (S_["w_in"],), (got_in,) = _exchange_wait("w_in", flight, outs[1])
    H_["w_in"] = _piece_sum("w_in", S_["w_in"], got_in, place)
    (O_["w_in"],) = _run_side("share_w_in", _sib_share([H_["w_in"]]))
    outs = _adam_quarter("w_in", *big["w_in"], H_["w_in"], O_["w_in"], place)
    big_out["w_in"] = tuple(a[None] for a in outs)

    smalls = [("lb_logits", lb_logits, m_lb_logits, v_lb_logits, d_lbl),
              ("hg_norm_w", hg_norm_w, m_hg_norm_w, v_hg_norm_w, d_hgw),
              ("rel_bias", rel_bias, m_rel_bias, v_rel_bias, d_rel),
              ("norm_mix_w", norm_mix_w, m_norm_mix_w, v_norm_mix_w, d_nmix),
              ("norm_mlp_w", norm_mlp_w, m_norm_mlp_w, v_norm_mlp_w, d_nmlp),
              ("norm_final_w", norm_final_w, m_norm_final_w, v_norm_final_w, d_nf)]
    like = [s[1] for s in smalls]
    packed = _small_allreduce_adam(_pack([s[4] for s in smalls]), _pack(like), _pack([s[2] for s in smalls]),
                                   _pack([s[3] for s in smalls]), got_in)
    small_out = {s[0]: vals for s, vals in zip(smalls, zip(*[_unpack(p, like) for p in packed]))}

    loss = lax.psum(loss_part[0, 0], ("x", "y", "c"))
    order = ["w_in", "lb_logits", "hg_norm_w", "rel_bias", "w_branch_a", "w_branch_b", "w_out", "norm_mix_w",
             "norm_mlp_w", "w_up", "w_down", "norm_final_w"]
    res = {**big_out, **small_out}
    return (loss, grad_x.reshape(x.shape), *[res[n][0] for n in order], *[res[n][1] for n in order],
            *[res[n][2] for n in order], *[res[n][3] for n in order])
```

```python
import functools

import jax
import jax.numpy as jnp
from jax import lax
from jax.experimental import pallas as pl
from jax.experimental.pallas import tpu as pltpu

F32 = jnp.float32
BF16 = jnp.bfloat16
HIGHEST = lax.Precision.HIGHEST

D_MODEL = 2048
SEQ = 2048
CHUNK = 64
HG_HEADS = 8
HG_D = 128
AT_HEADS = 16
AT_DH = 64
LEFT = 8
REL_CLIP = 256
D_FF = 8192
EPS = 1e-6
ADAM_LR = 0.001
ADAM_B1 = 0.9
ADAM_B2 = 0.999
ADAM_EPS = 1e-08
ADAM_WD = 0.01
ADAM_STEP = 10

LANE = 128
NEG = -1e30
EXP_CLAMP = 80.0
VMEM_LIMIT = 48 * 1024 * 1024
MM_TM, MM_TN, MM_TK = 1024, 1024, 2816
ROW_TILE = 256
QB = 2 * CHUNK


def _hgw():
    return HG_HEADS * HG_D


def _atw():
    return AT_HEADS * AT_DH


def _cparams(sem):
    return pltpu.CompilerParams(dimension_semantics=sem, vmem_limit_bytes=VMEM_LIMIT)


def _sigmoid(x):
    return jax.nn.sigmoid(x)


def _dot(a, b, dims, precision=None):
    return lax.dot_general(a, b, (dims, ((), ())), preferred_element_type=F32, precision=precision)


def _nn(a, b, precision=None):
    return _dot(a, b, ((1,), (0,)), precision)


def _nt(a, b, precision=None):
    return _dot(a, b, ((1,), (1,)), precision)


def _tn(a, b, precision=None):
    return _dot(a, b, ((0,), (0,)), precision)


class _Side:
    def __init__(self, build, nsem, reads=(), aliased=(), fresh=()):
        self.build, self.nsem = build, nsem
        self.reads, self.aliased, self.fresh = list(reads), list(aliased), list(fresh)

    def operands(self):
        return self.reads + self.aliased

    def in_specs(self):
        return [ANY] * len(self.operands())

    def out_specs(self):
        return [ANY] * (len(self.aliased) + len(self.fresh))

    def out_shape(self):
        return [jax.ShapeDtypeStruct(a.shape, a.dtype) for a in self.aliased] + self.fresh

    def aliases(self, n_in, n_out):
        return {n_in + len(self.reads) + t: n_out + t for t in range(len(self.aliased))}

    def scratch(self):
        return [pltpu.SemaphoreType.DMA((self.nsem,)), pltpu.SemaphoreType.DMA((self.nsem,))]

    def hooks(self, in_refs, out_refs, sems, first, last):
        nr, na = len(self.reads), len(self.aliased)
        args = (in_refs[:nr], out_refs[:na], out_refs[na:], *sems)

        @pl.when(first)
        def _():
            for cp in self.build(*args):
                cp.start()

        @pl.when(last)
        def _():
            for cp in self.build(*args):
                cp.wait()


def _after(*tokens):
    return _Side(lambda *args: [], 1, reads=tokens)


def _side_parts(side):
    if side is None:
        return [], [], [], [], lambda n_in, n_out: {}, []
    return side.operands(), side.in_specs(), side.out_specs(), side.out_shape(), side.aliases, side.scratch()


def _call_with_side(body, name, grid, in_specs, out_specs, out_shape, scratch, sem, operands, side, n_prefetch=0,
                    aliases=None, borrow=None):
    _, _, s_out, s_shape, _, s_scr = _side_parts(side)
    n_in, n_out = n_prefetch + len(in_specs), len(out_specs)
    borrow = borrow or {}
    s_ops, s_alias = [], {}
    if side is not None:
        keep = [t for t in range(len(side.aliased)) if t not in borrow]
        s_ops = side.reads + [side.aliased[t] for t in keep]
        s_alias = {n_in + len(side.reads) + pos: n_out + t for pos, t in enumerate(keep)}
        s_alias.update({n_prefetch + i: n_out + t for t, i in borrow.items()})
    s_in = [ANY] * len(s_ops)
    n_sin, n_sout = len(s_ops), len(s_out)

    def wrapped(*refs):
        a, b, c = n_in + n_sin, n_in + n_sin + n_out, n_in + n_sin + n_out + n_sout
        ids = [pl.program_id(d) for d in range(len(grid))]
        first = functools.reduce(lambda p, q: p & q, [i == 0 for i in ids])
        last = functools.reduce(lambda p, q: p & q, [i == g - 1 for i, g in zip(ids, grid)])
        side.hooks(refs[n_in:a], refs[b:c], refs[-2:], first, last)
        body(*refs[:n_in], *refs[a:b], *refs[c:-2])

    spec = dict(grid=grid, in_specs=in_specs + s_in, out_specs=out_specs + s_out, scratch_shapes=scratch + s_scr)
    if n_prefetch:
        spec = dict(grid_spec=pltpu.PrefetchScalarGridSpec(num_scalar_prefetch=n_prefetch, **spec))
    return pl.pallas_call(
        body if side is None else wrapped, name=name, out_shape=out_shape + s_shape,
        input_output_aliases={**s_alias, **{n_prefetch + i: o for i, o in (aliases or {}).items()}},
        compiler_params=_cparams(sem if side is None else ("arbitrary",) * len(grid)), **spec,
    )(*operands, *s_ops)


def _mm_tk(K):
    if K <= MM_TK:
        return K
    return max(t for t in range(LANE, MM_TK + 1, LANE) if K % t == 0)


def _mm(name, a, b, mode, out_dtypes, extras=(), epilogue=None, side=None):
    if mode == "nn":
        (M, K), (K2, N) = a.shape, b.shape
    elif mode == "nt":
        (M, K), (N, K2) = a.shape, b.shape
    else:
        (K, M), (K2, N) = a.shape, b.shape
    assert K == K2, (name, a.shape, b.shape)
    tm, tn, tk = min(MM_TM, M), min(MM_TN, N), _mm_tk(K)
    assert M % tm == 0 and N % tn == 0 and K % tk == 0, (name, M, N, K)
    ni, nj, nk = M // tm, N // tn, K // tk
    ne, no = len(extras), len(out_dtypes)
    if epilogue is None:
        epilogue = lambda acc: (acc,)
    s_ops, s_in, s_out, s_shape, s_alias, s_scr = _side_parts(side)
    n_in, n_sin, n_sout = 2 + ne, len(s_ops), len(s_out)

    def body(*refs):
        a_ref, b_ref = refs[:2]
        extra_refs = refs[2:n_in]
        out_refs = refs[n_in + n_sin:n_in + n_sin + no]
        rest = refs[n_in + n_sin + no + n_sout:]
        i, j, k = pl.program_id(0), pl.program_id(1), pl.program_id(2)
        if side is not None:
            side.hooks(refs[n_in:n_in + n_sin], refs[n_in + n_sin + no:n_in + n_sin + no + n_sout], rest[-2:],
                       (i == 0) & (j == 0) & (k == 0), (i == ni - 1) & (j == nj - 1) & (k == nk - 1))
        av, bv = a_ref[...].astype(BF16), b_ref[...].astype(BF16)
        prod = _nn(av, bv) if mode == "nn" else _nt(av, bv) if mode == "nt" else _tn(av, bv)

        def finish(acc):
            res = epilogue(acc, *[e[...] for e in extra_refs])
            for o_ref, r in zip(out_refs, res):
                o_ref[...] = r.astype(o_ref.dtype)

        if nk == 1:
            finish(prod)
        else:
            acc_ref = rest[0]

            @pl.when(k == 0)
            def _():
                acc_ref[...] = prod

            @pl.when((k > 0) & (k < nk - 1))
            def _():
                acc_ref[...] += prod

            @pl.when(k == nk - 1)
            def _():
                finish(acc_ref[...] + prod)

    if mode == "nn":
        a_spec = pl.BlockSpec((tm, tk), lambda i, j, k: (i, k))
        b_spec = pl.BlockSpec((tk, tn), lambda i, j, k: (k, j))
    elif mode == "nt":
        a_spec = pl.BlockSpec((tm, tk), lambda i, j, k: (i, k))
        b_spec = pl.BlockSpec((tn, tk), lambda i, j, k: (j, k))
    else:
        a_spec = pl.BlockSpec((tk, tm), lambda i, j, k: (k, i))
        b_spec = pl.BlockSpec((tk, tn), lambda i, j, k: (k, j))
    o_spec = pl.BlockSpec((tm, tn), lambda i, j, k: (i, j))
    sem = ("arbitrary",) * 3 if side is not None else ("parallel", "parallel", "arbitrary")
    outs = pl.pallas_call(
        body, name=name,
        grid=(ni, nj, nk),
        in_specs=[a_spec, b_spec] + [o_spec] * ne + s_in,
        out_specs=[o_spec] * no + s_out,
        out_shape=[jax.ShapeDtypeStruct((M, N), dt) for dt in out_dtypes] + s_shape,
        input_output_aliases=s_alias(n_in, no),
        scratch_shapes=([pltpu.VMEM((tm, tn), F32)] if nk > 1 else []) + s_scr,
        compiler_params=_cparams(sem),
    )(a, b, *extras, *s_ops)
    return outs[0] if len(outs) == 1 else outs


def _row_spec(tr, d):
    return pl.BlockSpec((tr, d), lambda i: (i, 0))


def _vec_spec(d):
    return pl.BlockSpec((1, d), lambda i: (0, 0))


def _rms_fwd(name, x, w, side=None):
    T, D = x.shape
    tr = min(ROW_TILE, T)

    def body(x_ref, w_ref, o_ref):
        xf = x_ref[...]
        r = lax.rsqrt(jnp.mean(xf * xf, axis=-1, keepdims=True) + EPS)
        o_ref[...] = (xf * r * w_ref[...]).astype(BF16)

    outs = _call_with_side(body, name, (T // tr,), [_row_spec(tr, D), _vec_spec(D)], [_row_spec(tr, D)],
                           [jax.ShapeDtypeStruct((T, D), BF16)], [], ("parallel",), (x, w), side)
    return outs[0] if side is None else outs


def _rms_bwd(name, dy, h, w, dres, side=None):
    T, D = h.shape
    tr = min(ROW_TILE, T)

    def body(dy_ref, h_ref, w_ref, dres_ref, dh_ref, dhb_ref, dw_ref):
        @pl.when(pl.program_id(0) == 0)
        def _():
            dw_ref[...] = jnp.zeros_like(dw_ref)

        hf, dyv = h_ref[...], dy_ref[...]
        r = lax.rsqrt(jnp.mean(hf * hf, axis=-1, keepdims=True) + EPS)
        xhat = hf * r
        dw_ref[...] += jnp.sum(dyv * xhat, axis=0, keepdims=True)
        dxh = dyv * w_ref[...]
        dh = dres_ref[...] + r * (dxh - xhat * jnp.mean(dxh * xhat, axis=-1, keepdims=True))
        dh_ref[...] = dh
        dhb_ref[...] = dh.astype(BF16)

    return _call_with_side(
        body, name, (T // tr,),
        [_row_spec(tr, D), _row_spec(tr, D), _vec_spec(D), _row_spec(tr, D)],
        [_row_spec(tr, D), _row_spec(tr, D), _vec_spec(D)],
        [jax.ShapeDtypeStruct((T, D), F32), jax.ShapeDtypeStruct((T, D), BF16), jax.ShapeDtypeStruct((1, D), F32)],
        [], ("arbitrary",), (dy, h, w, dres), side)


def _loss_head(h2, target, w):
    T, D = h2.shape
    tr = min(ROW_TILE, T)

    def body(h_ref, t_ref, w_ref, loss_ref, dh_ref, dhb_ref, dw_ref):
        @pl.when(pl.program_id(0) == 0)
        def _():
            dw_ref[...] = jnp.zeros_like(dw_ref)
            loss_ref[...] = jnp.zeros_like(loss_ref)

        hf, wv = h_ref[...], w_ref[...]
        r = lax.rsqrt(jnp.mean(hf * hf, axis=-1, keepdims=True) + EPS)
        xhat = hf * r
        diff = xhat * wv - t_ref[...]
        loss_ref[...] += 0.5 * jnp.sum(jnp.mean(diff * diff, axis=-1, keepdims=True))
        dyv = diff * (1.0 / D)
        dw_ref[...] += jnp.sum(dyv * xhat, axis=0, keepdims=True)
        dxh = dyv * wv
        dh = r * (dxh - xhat * jnp.mean(dxh * xhat, axis=-1, keepdims=True))
        dh_ref[...] = dh
        dhb_ref[...] = dh.astype(BF16)

    return pl.pallas_call(
        body, name="loss_head", grid=(T // tr,),
        in_specs=[_row_spec(tr, D), _row_spec(tr, D), _vec_spec(D)],
        out_specs=[_vec_spec(LANE), _row_spec(tr, D), _row_spec(tr, D), _vec_spec(D)],
        out_shape=[jax.ShapeDtypeStruct((1, LANE), F32), jax.ShapeDtypeStruct((T, D), F32),
                   jax.ShapeDtypeStruct((T, D), BF16), jax.ShapeDtypeStruct((1, D), F32)],
        compiler_params=_cparams(("arbitrary",)),
    )(h2, target, w)


def _gate_tiles(T, D):
    goff = 4 * _hgw() + 3 * _atw()
    tc = min(1024, D)
    assert goff % tc == 0 and D % tc == 0
    return min(ROW_TILE, T), tc, goff // tc, D // tc


def _merge(z, pa, pb, side=None):
    T, D = pa.shape
    tr, tc, g0, nd = _gate_tiles(T, D)

    def body(ga_ref, gb_ref, pa_ref, pb_ref, o_ref):
        o_ref[...] = (_sigmoid(ga_ref[...]) * pa_ref[...] + _sigmoid(gb_ref[...]) * pb_ref[...]).astype(BF16)

    t = pl.BlockSpec((tr, tc), lambda i, j: (i, j))
    outs = _call_with_side(
        body, "merge", (T // tr, nd),
        [pl.BlockSpec((tr, tc), lambda i, j: (i, g0 + j)), pl.BlockSpec((tr, tc), lambda i, j: (i, g0 + nd + j)), t, t],
        [t], [jax.ShapeDtypeStruct((T, D), BF16)], [], ("parallel", "parallel"), (z, z, pa, pb), side)
    return outs[0] if side is None else outs


def _dmerge(dm, z, pa, pb):
    T, D = pa.shape
    tr, tc, g0, nd = _gate_tiles(T, D)

    def body(dm_ref, ga_ref, gb_ref, pa_ref, pb_ref, dpa_ref, dpb_ref, dga_ref, dgb_ref):
        dmv = dm_ref[...]
        sa, sb = _sigmoid(ga_ref[...]), _sigmoid(gb_ref[...])
        dpa_ref[...] = (dmv * sa).astype(BF16)
        dpb_ref[...] = (dmv * sb).astype(BF16)
        dga_ref[...] = (dmv * pa_ref[...] * sa * (1.0 - sa)).astype(BF16)
        dgb_ref[...] = (dmv * pb_ref[...] * sb * (1.0 - sb)).astype(BF16)

    t = pl.BlockSpec((tr, tc), lambda i, j: (i, j))
    return pl.pallas_call(
        body, name="dmerge", grid=(T // tr, nd),
        in_specs=[t, pl.BlockSpec((tr, tc), lambda i, j: (i, g0 + j)),
                  pl.BlockSpec((tr, tc), lambda i, j: (i, g0 + nd + j)), t, t],
        out_specs=[t, t, t, t],
        out_shape=[jax.ShapeDtypeStruct((T, D), BF16)] * 4,
        compiler_params=_cparams(("parallel", "parallel")),
    )(dm, z, z, pa, pb)


def _hg_gates(xq, xf, lb):
    f = _sigmoid(xf)
    g = lb + (1.0 - lb) * f
    sq = _sigmoid(xq)
    return f, g, jnp.log(g), 1.0 - g, sq, xq * sq * (HG_D ** -0.5)


def _split2(x):
    hi = x.astype(BF16)
    return hi, (x - hi.astype(F32)).astype(BF16)


def _tri_sum(tri, x):
    hi, rest = x.astype(BF16), x - x.astype(BF16).astype(F32)
    mid, lo = _split2(rest)
    return _nn(tri, lo) + _nn(tri, mid) + _nn(tri, hi)


def _hg_decays(lg, tri_incl, rowi):
    b = _tri_sum(tri_incl, lg)
    b_last = jnp.sum(lg, axis=0, keepdims=True)
    b_mid = jnp.sum(jnp.where(rowi <= CHUNK // 2, lg, 0.0), axis=0, keepdims=True)
    return b, b_last, b_mid


HG_GROUP = 2


def _hg_in_specs(T):
    ng = HG_HEADS // HG_GROUP
    return [pl.BlockSpec((T, HG_GROUP * HG_D), lambda h, s=s: (0, s * ng + h)) for s in range(4)]


def _hg_fwd(z, lb_logits, hgw, side=None):
    T = z.shape[0]
    H, d, C, G = HG_HEADS, HG_D, CHUNK, HG_GROUP
    nc = T // C

    def body(hq_ref, hf_ref, hi_ref, hg_ref, lbl_ref, w_ref, ya_ref, o_ref, s_ref):
        lb_all = 1.0 / (1.0 + jnp.exp(lbl_ref[1:2, :] - lbl_ref[0:1, :]))
        wv = w_ref[...]
        row = lax.broadcasted_iota(jnp.int32, (C, C), 0)
        col = lax.broadcasted_iota(jnp.int32, (C, C), 1)
        tril = col <= row
        tri_incl = tril.astype(BF16)
        rowi = lax.broadcasted_iota(jnp.int32, (C, G * d), 0)
        lanes = [slice(hh * d, (hh + 1) * d) for hh in range(G)]
        per_head = lambda fn: jnp.concatenate([fn(hh, sl) for hh, sl in enumerate(lanes)], axis=1)
        wv_all = jnp.tile(wv, (1, G))

        def chunk(c, states):
            rows = pl.ds(pl.multiple_of(c * C, C), C)
            xq, xf, v, xg = hq_ref[rows, :], hf_ref[rows, :], hi_ref[rows, :], hg_ref[rows, :]
            _, _, lg, kk, _, q = _hg_gates(xq, xf, lb_all)
            b, b_last, b_mid = _hg_decays(lg, tri_incl, rowi)
            vb, qe = v.astype(BF16), (q * jnp.exp(b)).astype(BF16)
            qt = (q * jnp.exp(b - b_mid)).astype(BF16)
            kt = (kk * jnp.exp(jnp.minimum(b_mid - b, EXP_CLAMP))).astype(BF16)
            kd, e_last = (kk * jnp.exp(b_last - b)).astype(BF16), jnp.exp(b_last)
            for hh, st in enumerate(states):
                s_ref[hh, c] = st
            o = per_head(lambda hh, sl: _nt(qe[:, sl], states[hh].astype(BF16)))
            a = [jnp.where(tril, _nt(qt[:, sl], kt[:, sl]), 0.0).astype(BF16) for sl in lanes]
            o = o + per_head(lambda hh, sl: _nn(a[hh], vb[:, sl]))
            o_ref[rows, :] = o
            r = per_head(lambda hh, sl: jnp.broadcast_to(
                lax.rsqrt(jnp.mean(o[:, sl] * o[:, sl], axis=-1, keepdims=True) + EPS), (C, d)))
            ya_ref[rows, :] = (o * r * wv_all * (xg * _sigmoid(xg))).astype(BF16)
            return tuple(st * e_last[:, sl] + _tn(vb[:, sl], kd[:, sl]) for st, sl in zip(states, lanes))

        lax.fori_loop(0, nc, chunk, tuple(jnp.zeros((d, d), F32) for _ in range(G)))

    heads = pl.BlockSpec((T, G * d), lambda h: (0, h))
    return _call_with_side(
        body, "hg_fwd", (H // G,),
        _hg_in_specs(T) + [pl.BlockSpec((2, G * d), lambda h: (0, h)), pl.BlockSpec((1, d), lambda h: (0, 0))],
        [heads, heads, pl.BlockSpec((G, nc, d, d), lambda h: (h, 0, 0, 0))],
        [jax.ShapeDtypeStruct((T, H * d), BF16), jax.ShapeDtypeStruct((T, H * d), F32),
         jax.ShapeDtypeStruct((H, nc, d, d), F32)],
        [], ("parallel",), (z, z, z, z, lb_logits, hgw), side)


def _hg_bwd(z, o, dya, states, lb_logits, hgw, side=None):
    T = z.shape[0]
    H, d, C, G = HG_HEADS, HG_D, CHUNK, HG_GROUP
    nc = T // C
    scale = HG_D ** -0.5

    def body(hq_ref, hf_ref, hi_ref, hg_ref, o_ref, dy_ref, s_ref, lbl_ref, w_ref,
             dq_ref, df_ref, di_ref, dg_ref, dlbl_ref, dw_ref, acc_ref):
        lb_all = 1.0 / (1.0 + jnp.exp(lbl_ref[1:2, :] - lbl_ref[0:1, :]))
        wv = w_ref[...]
        row = lax.broadcasted_iota(jnp.int32, (C, C), 0)
        col = lax.broadcasted_iota(jnp.int32, (C, C), 1)
        tril = col <= row
        tri_incl = tril.astype(BF16)
        triu_incl = (col >= row).astype(BF16)
        rowi = lax.broadcasted_iota(jnp.int32, (C, G * d), 0)
        lanes = [slice(hh * d, (hh + 1) * d) for hh in range(G)]
        per_head = lambda fn: jnp.concatenate([fn(hh, sl) for hh, sl in enumerate(lanes)], axis=1)
        head_mean = lambda x: per_head(
            lambda hh, sl: jnp.broadcast_to(jnp.mean(x[:, sl], axis=-1, keepdims=True), (C, d)))
        wv_all = jnp.tile(wv, (1, G))
        lb = lb_all
        acc_ref[...] = jnp.zeros_like(acc_ref)

        @pl.when(pl.program_id(0) == 0)
        def _():
            dw_ref[...] = jnp.zeros_like(dw_ref)

        def chunk(i, carry):
            dsts, tail = carry
            c = nc - 1 - i
            rows = pl.ds(pl.multiple_of(c * C, C), C)
            xq, xf, v, xg = hq_ref[rows, :], hf_ref[rows, :], hi_ref[rows, :], hg_ref[rows, :]
            f, g, lg, kk, sq, q = _hg_gates(xq, xf, lb)
            b, b_last, b_mid = _hg_decays(lg, tri_incl, rowi)
            e_b, e_qm, e_km = jnp.exp(b), jnp.exp(b - b_mid), jnp.exp(jnp.minimum(b_mid - b, EXP_CLAMP))
            e_kl, e_last = jnp.exp(b_last - b), jnp.exp(b_last)
            ov, dy = o_ref[rows, :], dy_ref[rows, :]
            r = lax.rsqrt(head_mean(ov * ov) + EPS)
            xhat = ov * r
            sg = _sigmoid(xg)
            dxg = dy * xhat * wv_all * (sg * (1.0 + xg * (1.0 - sg)))
            dyn = dy * (xg * sg)
            acc_ref[0:1, :] += jnp.sum(dyn * xhat, axis=0, keepdims=True)
            dxh = dyn * wv_all
            dof = r * (dxh - xhat * head_mean(dxh * xhat))
            do, vb = dof.astype(BF16), v.astype(BF16)
            qe, kd, qt, kt = (q * e_b).astype(BF16), (kk * e_kl).astype(BF16), (q * e_qm).astype(BF16), (kk * e_km).astype(BF16)
            pm = [jnp.where(tril, _nt(do[:, sl], vb[:, sl]), 0.0).astype(BF16) for sl in lanes]
            am = [jnp.where(tril, _nt(qt[:, sl], kt[:, sl]), 0.0).astype(BF16) for sl in lanes]
            st = [_split2(s_ref[hh, c]) for hh in range(G)]
            ds = [_split2(x) for x in dsts]
            dq_state = per_head(lambda hh, sl: _nn(do[:, sl], st[hh][1]) + _nn(do[:, sl], st[hh][0]))
            dk_state = per_head(lambda hh, sl: _nn(vb[:, sl], ds[hh][1]) + _nn(vb[:, sl], ds[hh][0]))
            dq_intra = per_head(lambda hh, sl: _nn(pm[hh], kt[:, sl]))
            dk_intra = per_head(lambda hh, sl: _tn(pm[hh], qt[:, sl]))
            dv = per_head(lambda hh, sl: _tn(am[hh], do[:, sl]) + _nt(kd[:, sl], ds[hh][0]))
            new_dsts = tuple(x * e_last[:, sl] + _tn(do[:, sl], qe[:, sl]) for x, sl in zip(dsts, lanes))
            dq = dq_state * e_b + dq_intra * e_qm
            dk = dk_intra * e_km + dk_state * e_kl
            db = (qe.astype(F32) * dq_state + qt.astype(F32) * dq_intra
                  - kt.astype(F32) * dk_intra - kd.astype(F32) * dk_state)
            dlg = _tri_sum(triu_incl, db) + tail
            dgate = dlg / g - dk
            acc_ref[1:2, :] += jnp.sum(dgate * (1.0 - f), axis=0, keepdims=True)
            dq_ref[rows, :] = (dq * scale * (sq * (1.0 + xq * (1.0 - sq)))).astype(BF16)
            df_ref[rows, :] = (dgate * (1.0 - lb) * f * (1.0 - f)).astype(BF16)
            di_ref[rows, :] = dv.astype(BF16)
            dg_ref[rows, :] = dxg.astype(BF16)
            return new_dsts, tail + jnp.sum(db, axis=0, keepdims=True)

        lax.fori_loop(0, nc, chunk, (tuple(jnp.zeros((d, d), F32) for _ in range(G)), jnp.zeros((1, G * d), F32)))
        dw_ref[...] += functools.reduce(lambda p, q: p + q, [acc_ref[0:1, sl] for sl in lanes])
        dl0 = acc_ref[1:2, :] * lb_all * (1.0 - lb_all)
        dlbl_ref[0:1, :] = dl0
        dlbl_ref[1:2, :] = -dl0

    heads = pl.BlockSpec((T, G * d), lambda h: (0, h))
    logits = pl.BlockSpec((2, G * d), lambda h: (0, h))
    return _call_with_side(
        body, "hg_bwd", (H // G,),
        _hg_in_specs(T) + [heads, heads, pl.BlockSpec((G, nc, d, d), lambda h: (h, 0, 0, 0)), logits,
                           pl.BlockSpec((1, d), lambda h: (0, 0))],
        [heads, heads, heads, heads, logits, pl.BlockSpec((1, d), lambda h: (0, 0))],
        [jax.ShapeDtypeStruct((T, H * d), BF16)] * 4 + [jax.ShapeDtypeStruct((2, H * d), F32),
                                                        jax.ShapeDtypeStruct((1, d), F32)],
        [pltpu.VMEM((8, G * d), F32)], ("arbitrary",), (z, z, z, z, o, dya, states, lb_logits, hgw), side)


def _at_dims():
    pad = LEFT * CHUNK
    return pad, QB + pad, AT_HEADS * AT_DH // LANE, 4 * _hgw() // LANE


def _rel_of_period():
    pad, W, _, _ = _at_dims()
    n = jnp.arange(QB + W)
    return jnp.clip(pad - jnp.where(n < W, n, n - (QB + W)), -REL_CLIP, REL_CLIP) + REL_CLIP


def _bias_window(rel_bias):
    pad, W, _, _ = _at_dims()
    H, P = rel_bias.shape[0], QB + W
    per = rel_bias[:, _rel_of_period()]
    win = jnp.tile(per, (1, QB))[:, :QB * (P - 1)].reshape(H, QB, P - 1)[:, :, :W]
    t = jnp.arange(QB)[:, None]
    j = jnp.arange(W)[None, :]
    ok = (j // CHUNK >= t // CHUNK) & (j // CHUNK <= t // CHUNK + LEFT)
    return jnp.where(ok[None], win, NEG)


def _bias_window_grad(dbw):
    pad, W, _, _ = _at_dims()
    H, P = dbw.shape[0], QB + W
    flat = jnp.pad(dbw, ((0, 0), (0, 0), (0, P - 1 - W))).reshape(H, QB * (P - 1))
    per = jnp.pad(flat, ((0, 0), (0, QB))).reshape(H, QB, P).sum(axis=1)
    onehot = _rel_of_period()[:, None] == jnp.arange(2 * REL_CLIP + 1)[None, :]
    return jnp.dot(per, onehot.astype(F32), precision=HIGHEST)


def _at_stack(x):
    first = lax.broadcasted_iota(jnp.int32, x.shape, 1) < AT_DH
    return jnp.concatenate([jnp.where(first, x, 0.0), jnp.where(first, 0.0, x)], axis=0).astype(BF16)


def _at_unstack(x):
    first = lax.broadcasted_iota(jnp.int32, (QB, LANE), 1) < AT_DH
    return jnp.where(first, x[:QB], x[QB:])


def _at_softmax(qs, kw, bias_ref, qi):
    pad, W, _, _ = _at_dims()
    s = _nt(qs, kw) * (AT_DH ** -0.5) + bias_ref[...].reshape(2 * QB, W)
    valid = lax.broadcasted_iota(jnp.int32, (2 * QB, W), 1) + qi * QB >= pad
    s = jnp.where(valid, s, NEG)
    e = jnp.exp(s - jnp.max(s, axis=-1, keepdims=True))
    return e / jnp.sum(e, axis=-1, keepdims=True)


def _at_fwd(z, bias_win, side=None):
    T = z.shape[0]
    pad, W, HP, c0 = _at_dims()
    nq = T // QB

    def body(q_ref, k_ref, v_ref, bias_ref, o_ref, kpad, vpad):
        qi = pl.program_id(1)

        @pl.when(qi == 0)
        def _():
            kpad[0:pad, :] = jnp.zeros((pad, LANE), BF16)
            vpad[0:pad, :] = jnp.zeros((pad, LANE), BF16)
            kpad[pad:, :] = k_ref[...].astype(BF16)
            vpad[pad:, :] = v_ref[...].astype(BF16)

        win = pl.ds(pl.multiple_of(qi * QB, QB), W)
        kw, vw = kpad[win, :], vpad[win, :]
        p = _at_softmax(_at_stack(q_ref[...]), kw, bias_ref, qi)
        o_ref[...] = _at_unstack(_nn(p.astype(BF16), vw)).astype(BF16)

    full = lambda s: pl.BlockSpec((T, LANE), lambda hp, qi, s=s: (0, c0 + s * HP + hp))
    return _call_with_side(
        body, "at_fwd", (HP, nq),
        [pl.BlockSpec((QB, LANE), lambda hp, qi: (qi, c0 + hp)), full(1), full(2),
         pl.BlockSpec((2, QB, W), lambda hp, qi: (hp, 0, 0))],
        [pl.BlockSpec((QB, LANE), lambda hp, qi: (qi, hp))],
        [jax.ShapeDtypeStruct((T, HP * LANE), BF16)],
        [pltpu.VMEM((T + pad, LANE), BF16)] * 2, ("parallel", "arbitrary"), (z, z, z, bias_win), side)


def _at_bwd(z, dyb, bias_win, side=None):
    T = z.shape[0]
    pad, W, HP, c0 = _at_dims()
    nq = T // QB
    scale = AT_DH ** -0.5

    def body(q_ref, k_ref, v_ref, do_ref, bias_ref, dq_ref, dk_ref, dv_ref, dbias_ref, kpad, vpad, dkpad, dvpad):
        qi = pl.program_id(1)

        @pl.when(qi == 0)
        def _():
            kpad[0:pad, :] = jnp.zeros((pad, LANE), BF16)
            vpad[0:pad, :] = jnp.zeros((pad, LANE), BF16)
            kpad[pad:, :] = k_ref[...].astype(BF16)
            vpad[pad:, :] = v_ref[...].astype(BF16)
            dkpad[...] = jnp.zeros_like(dkpad)
            dvpad[...] = jnp.zeros_like(dvpad)
            dbias_ref[...] = jnp.zeros_like(dbias_ref)

        win = pl.ds(pl.multiple_of(qi * QB, QB), W)
        kw, vw = kpad[win, :], vpad[win, :]
        qs, dos = _at_stack(q_ref[...]), _at_stack(do_ref[...])
        p = _at_softmax(qs, kw, bias_ref, qi)
        dp = _nt(dos, vw)
        ds = p * (dp - jnp.sum(p * dp, axis=-1, keepdims=True))
        dbias_ref[...] += ds.reshape(2, QB, W)
        dsb = (ds * scale).astype(BF16)
        dq_ref[...] = _at_unstack(_nn(dsb, kw)).astype(BF16)
        dkpad[win, :] += _tn(dsb, qs)
        dvpad[win, :] += _tn(p.astype(BF16), dos)

        @pl.when(qi == nq - 1)
        def _():
            dk_ref[...] = dkpad[pad:, :].astype(BF16)
            dv_ref[...] = dvpad[pad:, :].astype(BF16)

    full = lambda s: pl.BlockSpec((T, LANE), lambda hp, qi, s=s: (0, c0 + s * HP + hp))
    blk = pl.BlockSpec((QB, LANE), lambda hp, qi: (qi, hp))
    col = pl.BlockSpec((T, LANE), lambda hp, qi: (0, hp))
    bw = pl.BlockSpec((2, QB, W), lambda hp, qi: (hp, 0, 0))
    return _call_with_side(
        body, "at_bwd", (HP, nq),
        [pl.BlockSpec((QB, LANE), lambda hp, qi: (qi, c0 + hp)), full(1), full(2), blk, bw],
        [blk, col, col, bw],
        [jax.ShapeDtypeStruct((T, HP * LANE), BF16)] * 3 + [jax.ShapeDtypeStruct(bias_win.shape, F32)],
        [pltpu.VMEM((T + pad, LANE), BF16)] * 2 + [pltpu.VMEM((T + pad, LANE), F32)] * 2,
        ("parallel", "arbitrary"), (z, z, z, dyb, bias_win), side)


def _piece_tiles(name, full_shape):
    pr, pc = _piece_shape(name, full_shape)
    tr = min(ROW_TILE, pr)
    assert pr % tr == 0
    nt = pr // tr
    if name in ROW_SHARDED:
        return tr, nt, lambda q, half, i: ((2 * q + half) * nt + i, 0)
    return tr, nt, lambda q, half, i: (half * nt + i, q)


def _cast_into_full(name, wq, place, side=None):
    full = _full_shape(name, wq.shape)
    pc = wq.shape[1]
    tr, nt, at = _piece_tiles(name, full)

    def body(place_ref, w_ref, o_ref):
        o_ref[...] = w_ref[...].astype(BF16)

    outs = _call_with_side(
        body, "cast_" + name, (2, nt), [pl.BlockSpec((tr, pc), lambda h, i, s: (h * nt + i, 0))],
        [pl.BlockSpec((tr, pc), lambda h, i, s: at(s[0], h, i))], [jax.ShapeDtypeStruct(full, BF16)],
        [], ("parallel", "parallel"), (place, wq), side, n_prefetch=1)
    return outs[0] if side is None else outs


def _g_w_in_half(u1, dz, place, own, side=None):
    T, K = u1.shape
    N = dz.shape[1]
    hk, tn = K // 2, min(MM_TN, N)
    half = (lambda s: s[1]) if own else (lambda s: 1 - s[1])

    def body(place_ref, a_ref, b_ref, o_ref):
        o_ref[...] = _tn(a_ref[...], b_ref[...]).astype(BF16)

    outs = _call_with_side(
        body, "g_w_in_keep" if own else "g_w_in_send", (N // tn,),
        [pl.BlockSpec((T, hk), lambda j, s: (0, half(s))), pl.BlockSpec((T, tn), lambda j, s: (0, j))],
        [pl.BlockSpec((hk, tn), lambda j, s: (0, j))], [jax.ShapeDtypeStruct((hk, N), BF16)],
        [], ("parallel",), (place, u1, dz), side, n_prefetch=1)
    return outs[0] if side is None else outs


def _chip_sum(name, grad, theirs, place, kept_rows=False):
    pr, pc = theirs.shape[1:]
    tr, nt, at = _piece_tiles(name, (2 * grad.shape[0], grad.shape[1]) if kept_rows else grad.shape)
    if kept_rows:
        at = lambda q, half, i: (i, q)

    def body(place_ref, g_ref, t_ref, o_ref):
        o_ref[...] = (g_ref[...].astype(F32) + t_ref[...].astype(F32)).astype(BF16)

    piece = pl.BlockSpec((None, tr, pc), lambda q, i, s: (q, i, 0))
    return pl.pallas_call(
        body, name="chip_sum_" + name,
        grid_spec=pltpu.PrefetchScalarGridSpec(
            num_scalar_prefetch=1, grid=(4, nt),
            in_specs=[pl.BlockSpec((tr, pc), lambda q, i, s: at(q, s[1], i)), piece], out_specs=piece),
        out_shape=jax.ShapeDtypeStruct(theirs.shape, BF16),
        compiler_params=_cparams(("parallel", "parallel")),
    )(place, grad, theirs)


def _piece_sum(name, chip_sums, got, place):
    pr, pc = chip_sums.shape[1:]
    tr = min(ROW_TILE, pr)

    def body(place_ref, own_ref, got_ref, o_ref):
        o_ref[...] = (own_ref[...].astype(F32) + got_ref[0].astype(F32) + got_ref[1].astype(F32)
                      + got_ref[2].astype(F32))

    return pl.pallas_call(
        body, name="piece_sum_" + name,
        grid_spec=pltpu.PrefetchScalarGridSpec(
            num_scalar_prefetch=1, grid=(pr // tr,),
            in_specs=[pl.BlockSpec((None, tr, pc), lambda i, s: (s[0], i, 0)),
                      pl.BlockSpec((3, tr, pc), lambda i, s: (0, i, 0))],
            out_specs=pl.BlockSpec((tr, pc), lambda i, s: (i, 0))),
        out_shape=jax.ShapeDtypeStruct((pr, pc), F32),
        compiler_params=_cparams(("parallel",)),
    )(place, chip_sums, got)


def _adam_quarter(name, w, m, v, g_mine, g_sib, place, side=None):
    pr, pc = g_mine.shape
    tr = min(ROW_TILE // 2, pr)
    nt = pr // tr

    def body(place_ref, w_ref, m_ref, v_ref, gm_ref, gs_ref, go_ref, d_ref, mo_ref, vo_ref):
        g = jnp.where(pl.program_id(0) == place_ref[1], gm_ref[...], gs_ref[...])
        delta, mn, vn = _adam_math(w_ref[...], g, m_ref[...], v_ref[...])
        go_ref[...] = g
        d_ref[...] = delta
        mo_ref[...] = mn
        vo_ref[...] = vn

    quarter = pl.BlockSpec((tr, pc), lambda h, i, s: (h * nt + i, 0))
    mine = pl.BlockSpec((tr, pc), lambda h, i, s: (jnp.where(h == s[1], i, 0), 0))
    sib = pl.BlockSpec((tr, pc), lambda h, i, s: (jnp.where(h == s[1], 0, i), 0))
    return _call_with_side(
        body, "adam_" + name, (2, nt), [quarter, quarter, quarter, mine, sib], [quarter] * 4,
        [jax.ShapeDtypeStruct(w.shape, F32)] * 4, [], ("parallel", "parallel"),
        (place, w, m, v, g_mine, g_sib), side, n_prefetch=1)


def _adam_math(w, g, m, v):
    m = ADAM_B1 * m + (1.0 - ADAM_B1) * g
    v = ADAM_B2 * v + (1.0 - ADAM_B2) * (g * g)
    m_hat = m / (1.0 - ADAM_B1 ** ADAM_STEP)
    v_hat = v / (1.0 - ADAM_B2 ** ADAM_STEP)
    return -ADAM_LR * (m_hat / (jnp.sqrt(v_hat) + ADAM_EPS) + ADAM_WD * w), m, v


WEIGHTS = ("w_in", "w_branch_a", "w_branch_b", "w_out", "w_up", "w_down")
ROW_SHARDED = ("w_out", "w_down")
ANY = pl.BlockSpec(memory_space=pl.ANY)
MESH = pl.DeviceIdType.MESH


def _place():
    x, y, c = lax.axis_index("x"), lax.axis_index("y"), lax.axis_index("c")
    chips = [(1 - x, y), (x, 1 - y), (1 - x, 1 - y)]
    return x, y, c, 2 * x + y, chips, [2 * cx + cy for cx, cy in chips]


def _piece(full_ref, name, q, half):
    K, N = full_ref.shape
    if name in ROW_SHARDED:
        rows = K // 8
        return full_ref.at[pl.ds(q * (2 * rows) + half * rows, rows), :]
    return full_ref.at[pl.ds(half * (K // 2), K // 2), pl.ds(q * (N // 4), N // 4)]


def _piece_shape(name, full_shape):
    K, N = full_shape
    return (K // 8, N) if name in ROW_SHARDED else (K // 2, N // 4)


def _full_shape(name, quarter_shape):
    Kq, Nq = quarter_shape
    return (4 * Kq, Nq) if name in ROW_SHARDED else (Kq, 4 * Nq)


def _remote(src, dst, send_sem, recv_sem, device):
    return pltpu.make_async_remote_copy(src_ref=src, dst_ref=dst, send_sem=send_sem, recv_sem=recv_sem,
                                        device_id=device, device_id_type=MESH)


def _z_part(u1, w_in, z_prev, place, k0, count, side=None):
    T, K = u1.shape
    N = w_in.shape[1]
    nq = N // 4
    tn = nq // 2 if (nq // 2) % LANE == 0 else nq
    tm = min(MM_TM, T)
    per = nq // tn
    col = lambda g, j, s: (s[0] ^ (k0 + g)) * per + j
    ins = [pl.BlockSpec((tm, K), lambda g, i, j, s: (i, 0)), pl.BlockSpec((K, tn), lambda g, i, j, s: (0, col(g, j, s)))]
    operands = [place, u1, w_in]
    if z_prev is not None:
        ins.append(ANY)
        operands.append(z_prev)

    def body(place_ref, a_ref, b_ref, *rest):
        rest[-1][...] = _nn(a_ref[...], b_ref[...])

    return _call_with_side(
        body, "z_part_%d" % k0, (count, T // tm, per), ins,
        [pl.BlockSpec((tm, tn), lambda g, i, j, s: (i, col(g, j, s)))], [jax.ShapeDtypeStruct((T, N), F32)],
        [], ("parallel",) * 3, tuple(operands), side, n_prefetch=1, aliases={} if z_prev is None else {2: 0},
        borrow={0: 1} if side is not None and side.aliased and side.aliased[0] is w_in else None)


def _rows(ref, span):
    return ref if span is None else ref.at[pl.ds(span[0], span[1]), :]


def _gather_moves(items):
    count = {"near": lambda arg: 2, "far": lambda arg: 1, "pass": len}

    def build(reads, aliased, fresh, send_sems, recv_sems, off=0):
        x, y, c, p, chips, chip_ids = _place()
        south = c == 0
        far_src = jnp.where(south, chip_ids[0], chip_ids[1])
        far_dst = (jnp.where(south, x, 1 - x), jnp.where(south, 1 - y, y), c)
        out = []

        def add(ref, device):
            k = off + len(out)
            out.append(_remote(ref, ref, send_sems.at[k], recv_sems.at[k], device))

        for (name, _, moves), ref in zip(items, aliased):
            for kind, arg in moves:
                if kind == "near":
                    for chip in chips[:2]:
                        add(_rows(_piece(ref, name, p, c), arg), (*chip, c))
                elif kind == "far":
                    add(_rows(_piece(ref, name, far_src, c), arg), far_dst)
                else:
                    for j in arg:
                        add(_piece(ref, name, chip_ids[j], c), (x, y, 1 - c))
        return out

    nsem = sum(count[kind](arg) for _, _, moves in items for kind, arg in moves)
    return _Side(build, nsem, aliased=[a for _, a, _ in items])


def _ici_near(names, fulls, rows=None):
    return _gather_moves([(n, a, [("near", r)]) for n, a, r in zip(names, fulls, rows or [None] * len(names))])


def _ici_far(names, fulls, rows=None):
    return _gather_moves([(n, a, [("far", r)]) for n, a, r in zip(names, fulls, rows or [None] * len(names))])


def _d2d_gather(names, fulls, which=(0, 1, 2)):
    return _gather_moves([(n, a, [("pass", which)]) for n, a in zip(names, fulls)])


def _sib_send(names, grads):
    def build(reads, aliased, fresh, send_sems, recv_sems, off=0):
        x, y, c, _, _, _ = _place()
        out = []
        for i, name in enumerate(names):
            for q in range(4):
                k = off + 4 * i + q
                out.append(_remote(_piece(reads[i], name, q, 1 - c), fresh[i].at[q], send_sems.at[k], recv_sems.at[k],
                                   (x, y, 1 - c)))
        return out

    shapes = [jax.ShapeDtypeStruct((4,) + _piece_shape(name, g.shape), BF16) for name, g in zip(names, grads)]
    return _Side(build, 4 * len(names), reads=grads, fresh=shapes)


def _sib_send_half(sent):
    K2, N = sent.shape

    def build(reads, aliased, fresh, send_sems, recv_sems, off=0):
        x, y, c, _, _, _ = _place()
        return [_remote(reads[0].at[:, pl.ds(q * (N // 4), N // 4)], fresh[0].at[q], send_sems.at[off + q],
                        recv_sems.at[off + q], (x, y, 1 - c)) for q in range(4)]

    return _Side(build, 4, reads=[sent], fresh=[jax.ShapeDtypeStruct((4, K2, N // 4), BF16)])


def _chip_exchange(chip_sums, rows=None, got=None):
    rows = rows or [None] * len(chip_sums)

    def build(reads, aliased, fresh, send_sems, recv_sems, off=0):
        _, _, c, _, chips, chip_ids = _place()
        out = []
        for i in range(len(chip_sums)):
            for j, (chip, cid) in enumerate(zip(chips, chip_ids)):
                k = off + 3 * i + j
                out.append(_remote(_rows(reads[i].at[cid], rows[i]), _rows((aliased or fresh)[i].at[j], rows[i]),
                                   send_sems.at[k], recv_sems.at[k], (*chip, c)))
        return out

    if got is not None:
        return _Side(build, 3 * len(chip_sums), reads=chip_sums, aliased=got)
    shapes = [jax.ShapeDtypeStruct((3,) + s.shape[1:], BF16) for s in chip_sums]
    return _Side(build, 3 * len(chip_sums), reads=chip_sums, fresh=shapes)


HBM = pl.BlockSpec(memory_space=pltpu.HBM)
SEM = pl.BlockSpec(memory_space=pltpu.SEMAPHORE)


def _exchange_copies(s_refs, land_refs, send_sems, recv_sems):
    _, _, c, _, chips, chip_ids = _place()
    return [_remote(s_ref.at[cid], land_ref.at[j], send_sems.at[3 * i + j], recv_sems.at[3 * i + j], (*chip, c))
            for i, (s_ref, land_ref) in enumerate(zip(s_refs, land_refs))
            for j, (chip, cid) in enumerate(zip(chips, chip_ids))]


def _exchange_start(name, chip_sums):
    n = len(chip_sums)

    def body(*refs):
        for cp in _exchange_copies(refs[:n], refs[n:2 * n], refs[2 * n], refs[2 * n + 1]):
            cp.start()
        refs[-1][...] = jnp.zeros_like(refs[-1])

    lands = [jax.ShapeDtypeStruct((3,) + s.shape[1:], s.dtype) for s in chip_sums]
    hbm = lambda a: pltpu.with_memory_space_constraint(a, pltpu.HBM)
    outs = pl.pallas_call(
        body, name="exchange_start_" + name,
        out_shape=(pltpu.SemaphoreType.DMA((3 * n,)), pltpu.SemaphoreType.DMA((3 * n,)),
                   *[pltpu.HBM(a.shape, a.dtype) for a in chip_sums + lands], jax.ShapeDtypeStruct((8, LANE), F32)),
        in_specs=(HBM,) * (2 * n), out_specs=(SEM, SEM) + (HBM,) * (2 * n) + (pl.BlockSpec(memory_space=pltpu.VMEM),),
        input_output_aliases={i: 2 + i for i in range(2 * n)},
        compiler_params=pltpu.CompilerParams(has_side_effects=pltpu.SideEffectType.DATAFLOW_SIDE_EFFECTING),
    )(*[hbm(s) for s in chip_sums], *[hbm(lax.empty(a.shape, a.dtype)) for a in lands])
    return outs[0], outs[1], list(outs[2:2 + n]), list(outs[2 + n:2 + 2 * n]), outs[-1]


def _exchange_wait(name, flight, after):
    send_sems, recv_sems, s_thru, land_thru, _ = flight
    n = len(s_thru)

    def body(*refs):
        for cp in _exchange_copies(refs[:n], refs[n:2 * n], refs[2 * n], refs[2 * n + 1]):
            cp.wait_send()
            cp.wait_recv()

    outs = pl.pallas_call(
        body, name="exchange_wait_" + name,
        out_shape=tuple(pltpu.HBM(a.shape, a.dtype) for a in s_thru + land_thru),
        in_specs=(HBM,) * (2 * n) + (SEM, SEM, ANY), out_specs=(HBM,) * (2 * n),
        input_output_aliases={i: i for i in range(2 * n)},
        compiler_params=pltpu.CompilerParams(has_side_effects=pltpu.SideEffectType.DATAFLOW_SIDE_EFFECTING),
    )(*s_thru, *land_thru, send_sems, recv_sems, after)
    return list(outs[:n]), list(outs[n:])


def _sib_share(halves):
    def build(reads, aliased, fresh, send_sems, recv_sems, off=0):
        x, y, c, _, _, _ = _place()
        return [_remote(reads[i], fresh[i], send_sems.at[off + i], recv_sems.at[off + i], (x, y, 1 - c))
                for i in range(len(halves))]

    return _Side(build, len(halves), reads=halves, fresh=[jax.ShapeDtypeStruct(h.shape, F32) for h in halves])


def _join(a, b):
    def build(reads, aliased, fresh, send_sems, recv_sems, off=0):
        ra, aa, fa = len(a.reads), len(a.aliased), len(a.fresh)
        return (a.build(reads[:ra], aliased[:aa], fresh[:fa], send_sems, recv_sems, off)
                + b.build(reads[ra:], aliased[aa:], fresh[fa:], send_sems, recv_sems, off + a.nsem))

    return _Side(build, a.nsem + b.nsem, a.reads + b.reads, a.aliased + b.aliased, a.fresh + b.fresh)


def _run_side(name, side):
    nr, na = len(side.reads), len(side.aliased)

    def body(*refs):
        n_in, n_out = nr + na, na + len(side.fresh)
        outs = refs[n_in:n_in + n_out]
        copies = side.build(refs[:nr], outs[:na], outs[na:], *refs[-2:])
        for cp in copies:
            cp.start()
        for cp in copies:
            cp.wait()

    return pl.pallas_call(
        body, name=name, in_specs=side.in_specs(), out_specs=side.out_specs(), out_shape=side.out_shape(),
        input_output_aliases=side.aliases(0, 0), scratch_shapes=side.scratch(),
    )(*side.operands())


def _small_allreduce_adam(gpart, w, m, v, after):
    R = gpart.shape[0]

    def body(g_ref, w_ref, m_ref, v_ref, after_ref, go_ref, d_ref, mo_ref, vo_ref, buf, send_sems, recv_sems):
        x, y, c = lax.axis_index("x"), lax.axis_index("y"), lax.axis_index("c")
        me = 4 * x + 2 * y + c
        buf[me] = g_ref[...]
        copies = []
        for k in range(1, 8):
            fx, fy, fc = (k >> 2) & 1, (k >> 1) & 1, k & 1
            peer = (1 - x if fx else x, 1 - y if fy else y, 1 - c if fc else c)
            cp = _remote(g_ref, buf.at[me], send_sems.at[k - 1], recv_sems.at[k - 1], peer)
            cp.start()
            copies.append((cp, 4 * peer[0] + 2 * peer[1] + peer[2]))
        for k, (cp, pid) in enumerate(copies):
            _remote(g_ref, buf.at[pid], send_sems.at[k], recv_sems.at[k], (x, y, c)).wait_recv()
        for cp, _ in copies:
            cp.wait_send()
        g = buf[0]
        for d in range(1, 8):
            g = g + buf[d]
        delta, mn, vn = _adam_math(w_ref[...], g, m_ref[...], v_ref[...])
        go_ref[...] = g
        d_ref[...] = delta
        mo_ref[...] = mn
        vo_ref[...] = vn

    vm = pl.BlockSpec(memory_space=pltpu.VMEM)
    return pl.pallas_call(
        body, name="small_allreduce_adam",
        in_specs=[vm] * 4 + [ANY], out_specs=[vm] * 4,
        out_shape=[jax.ShapeDtypeStruct((R, LANE), F32)] * 4,
        scratch_shapes=[pltpu.VMEM((8, R, LANE), F32), pltpu.SemaphoreType.DMA((7,)), pltpu.SemaphoreType.DMA((7,))],
    )(gpart, w, m, v, after)


def _pack(arrs):
    flat = jnp.concatenate([a.reshape(-1).astype(F32) for a in arrs])
    rows = -(-flat.shape[0] // (8 * LANE)) * 8
    return jnp.pad(flat, (0, rows * LANE - flat.shape[0])).reshape(rows, LANE)


def _unpack(packed, like):
    flat, out, off = packed.reshape(-1), [], 0
    for a in like:
        out.append(flat[off:off + a.size].reshape(a.shape))
        off += a.size
    return out


def kernel(x, w_in, lb_logits, hg_norm_w, rel_bias, w_branch_a, w_branch_b, w_out, norm_mix_w, norm_mlp_w, w_up, w_down, norm_final_w, loss_target, m_w_in, m_lb_logits, m_hg_norm_w, m_rel_bias, m_w_branch_a, m_w_branch_b, m_w_out, m_norm_mix_w, m_norm_mlp_w, m_w_up, m_w_down, m_norm_final_w, v_w_in, v_lb_logits, v_hg_norm_w, v_rel_bias, v_w_branch_a, v_w_branch_b, v_w_out, v_norm_mix_w, v_norm_mlp_w, v_w_up, v_w_down, v_norm_final_w):
    T, D = x.shape[1], x.shape[2]
    x2, tgt = x.reshape(T, D), loss_target.reshape(T, D)
    big = dict(w_in=(w_in, m_w_in, v_w_in), w_branch_a=(w_branch_a, m_w_branch_a, v_w_branch_a),
               w_branch_b=(w_branch_b, m_w_branch_b, v_w_branch_b), w_out=(w_out, m_w_out, v_w_out),
               w_up=(w_up, m_w_up, v_w_up), w_down=(w_down, m_w_down, v_w_down))
    big = {k: tuple(a[0] for a in v) for k, v in big.items()}
    nfw = norm_final_w.reshape(1, D)

    place = jnp.stack([2 * lax.axis_index("x") + lax.axis_index("y"), lax.axis_index("c")]).astype(jnp.int32)
    small3 = ["w_branch_a", "w_branch_b", "w_out"]

    def span(name, lo, hi):
        pr = big[name][0].shape[0] // 2
        return (pr * lo // 16, pr * (hi - lo) // 16)

    Wf = {"w_in": _cast_into_full("w_in", big["w_in"][0], place)}
    for name, lo, hi in (("w_up", 0, 2), ("w_down", 2, 4)):
        Wf[name], Wf["w_in"] = _cast_into_full(
            name, big[name][0], place, side=_ici_near(["w_in"], [Wf["w_in"]], rows=[span("w_in", lo, hi)]))
    for name in small3:
        Wf[name] = _cast_into_full(name, big[name][0], place)

    ab = ["w_branch_a", "w_branch_b"]
    u1 = _rms_fwd("norm_mix", x2, norm_mix_w)
    z, Wf["w_in"] = _z_part(u1, Wf["w_in"], None, place, 0, 1,
                            side=_ici_near(["w_in"], [Wf["w_in"]], rows=[span("w_in", 4, 16)]))
    (Wf["w_in"],) = _run_side("pass_w_in_near", _d2d_gather(["w_in"], [Wf["w_in"]], which=(0, 1)))
    def carried(**moves):
        sized = {n: [(k, span(n, *a) if k != "pass" and a else a or (None if k != "pass" else (0, 1, 2)))
                     for k, *a in ms] for n, ms in moves.items()}
        return _gather_moves([(n, Wf[n], ms) for n, ms in sized.items()]), list(moves)

    def land(names, outs):
        Wf.update(zip(names, outs[-len(names):]))
        return outs[:-len(names)]

    side, names = carried(w_in=[("far",)], w_branch_a=[("near",)], w_branch_b=[("near",)])
    (z,) = land(names, _z_part(u1, Wf["w_in"], z, place, 1, 2, side=side))
    (Wf["w_in"],) = _run_side("pass_w_in_far", _d2d_gather(["w_in"], [Wf["w_in"]], which=(2,)))
    side, names = carried(w_branch_a=[("far",)], w_branch_b=[("far",)], w_out=[("near", 0, 8)])
    (z,) = land(names, _z_part(u1, Wf["w_in"], z, place, 3, 1, side=side))
    side, names = carried(w_branch_a=[("pass",)], w_branch_b=[("pass",)], w_out=[("near", 8, 16)],
                          w_up=[("near", 0, 10)])
    ya, o_hg, states = land(names, _hg_fwd(z, lb_logits, hg_norm_w, side=side))
    bias_win = _bias_window(rel_bias[0])
    side, names = carried(w_out=[("far",)], w_up=[("near", 10, 16), ("far", 0, 10)], w_down=[("near", 0, 3)])
    (yb,) = land(names, _at_fwd(z, bias_win, side=side))
    side, names = carried(w_out=[("pass",)], w_up=[("far", 10, 14)])
    (pa,) = land(names, _mm("branch_a", ya, Wf["w_branch_a"], "nn", [F32], side=side))
    side, names = carried(w_up=[("far", 14, 16)], w_down=[("near", 3, 4)])
    (pb,) = land(names, _mm("branch_b", yb, Wf["w_branch_b"], "nn", [F32], side=side))
    side, names = carried(w_down=[("near", 4, 8)])
    (merged,) = land(names, _merge(z, pa, pb, side=side))
    add = lambda acc, res: (acc + res,)
    side, names = carried(w_up=[("pass",)], w_down=[("near", 8, 12)])
    (h1,) = land(names, _mm("out_proj", merged, Wf["w_out"], "nn", [F32], extras=[x2], epilogue=add, side=side))
    side, names = carried(w_down=[("near", 12, 14)])
    (u2,) = land(names, _rms_fwd("norm_mlp", h1, norm_mlp_w, side=side))
    relu2 = lambda acc: (acc, jnp.square(jnp.maximum(acc, 0.0)))
    side, names = carried(w_down=[("near", 14, 16), ("far", 0, 14)])
    a_pre, act = land(names, _mm("mlp_up", u2, Wf["w_up"], "nn", [F32, BF16], epilogue=relu2, side=side))
    (Wf["w_down"],) = _run_side("far_w_down", _ici_far(["w_down"], [Wf["w_down"]], rows=[span("w_down", 14, 16)]))
    (Wf["w_down"],) = _run_side("pass_w_down", _d2d_gather(["w_down"], [Wf["w_down"]]))
    h2 = _mm("mlp_down", act, Wf["w_down"], "nn", [F32], extras=[h1], epilogue=add)
    loss_part, dh2, dh2b, d_nf = _loss_head(h2, tgt, nfw)

    drelu2 = lambda acc, a: (acc * (2.0 * jnp.maximum(a, 0.0)),)
    da = _mm("d_act", dh2b, Wf["w_down"], "nt", [BF16], extras=[a_pre], epilogue=drelu2)
    G = {}
    G["w_down"] = _mm("g_w_down", act, dh2b, "tn", [BF16])
    G["w_up"] = _mm("g_w_up", u2, da, "tn", [BF16])
    T_, S_, GOT = {}, {}, {}
    du2, T_["w_down"], T_["w_up"] = _mm("d_u2", da, Wf["w_up"], "nt", [F32],
                                        side=_sib_send(["w_down", "w_up"], [G["w_down"], G["w_up"]]))
    mlp2 = ["w_down", "w_up"]
    flight_mlp = _exchange_start("mlp", [_chip_sum(n, G[n], T_[n], place) for n in mlp2])
    dh1, dh1b, d_nmlp = _rms_bwd("norm_mlp_bwd", du2, h1, norm_mlp_w, dh2, side=_after(flight_mlp[-1]))
    dmerged = _mm("d_merged", dh1b, Wf["w_out"], "nt", [F32])
    G["w_out"] = _mm("g_w_out", merged, dh1b, "tn", [BF16])
    dpa, dpb, dz_ga, dz_gb = _dmerge(dmerged, z, pa, pb)
    dya = _mm("d_ya", dpa, Wf["w_branch_a"], "nt", [F32])
    dyb = _mm("d_yb", dpb, Wf["w_branch_b"], "nt", [F32])
    G["w_branch_a"] = _mm("g_w_a", ya, dpa, "tn", [BF16])
    G["w_branch_b"] = _mm("g_w_b", yb, dpb, "tn", [BF16])
    dz_q, dz_f, dz_i, dz_g, d_lbl, d_hgw, *sent = _hg_bwd(
        z, o_hg, dya, states, lb_logits, hg_norm_w, side=_sib_send(small3, [G[n] for n in small3]))
    flight_small = _exchange_start("small", [_chip_sum(n, G[n], t, place) for n, t in zip(small3, sent)])
    dz_aq, dz_ak, dz_av, dbias_win = _at_bwd(z, dyb, bias_win, side=_after(flight_small[-1]))
    dz = jnp.concatenate([dz_q, dz_f, dz_i, dz_g, dz_aq, dz_ak, dz_av, dz_ga, dz_gb], axis=1)
    g_send = _g_w_in_half(u1, dz, place, False)
    g_keep, T_["w_in"] = _g_w_in_half(u1, dz, place, True, side=_sib_send_half(g_send))
    for names, flight in ((mlp2, flight_mlp), (small3, flight_small)):
        sums, got = _exchange_wait("_".join(names), flight, g_keep)
        S_.update(zip(names, sums))
        GOT.update(zip(names, got))
    S_["w_in"] = _chip_sum("w_in", g_keep, T_["w_in"], place, kept_rows=True)
    early = [n for n in WEIGHTS if n != "w_in"]
    H_ = {n: _piece_sum(n, S_[n], GOT[n], place) for n in early}
    flight = _exchange_start("w_in", [S_["w_in"]])
    share_early = _sib_share([H_[n] for n in early])
    share_early.reads.append(flight[-1])
    du1, *shared = _mm("d_u1", dz, Wf["w_in"], "nt", [F32], side=share_early)
    O_ = dict(zip(early, shared))
    grad_x, _, d_nmix = _rms_bwd("norm_mix_bwd", du1, x2, norm_mix_w, dh1)
    d_rel = _bias_window_grad(dbias_win)
    big_out = {}
    for name in early:
        outs = _adam_quarter(name, *big[name], H_[name], O_[name], place)
        big_out[name] = tuple(a[None] for a in outs)
    (S_["w_in"],), (got_in,) = _exchange_wait("w_in", flight, outs[1])
    H_["w_in"] = _piece_sum("w_in", S_["w_in"], got_in, place)
    (O_["w_in"],) = _run_side("share_w_in", _sib_share([H_["w_in"]]))
    outs = _adam_quarter("w_in", *big["w_in"], H_["w_in"], O_["w_in"], place)
    big_out["w_in"] = tuple(a[None] for a in outs)

    smalls = [("lb_logits", lb_logits, m_lb_logits, v_lb_logits, d_lbl),
              ("hg_norm_w", hg_norm_w, m_hg_norm_w, v_hg_norm_w, d_hgw),
              ("rel_bias", rel_bias, m_rel_bias, v_rel_bias, d_rel),
              ("norm_mix_w", norm_mix_w, m_norm_mix_w, v_norm_mix_w, d_nmix),
              ("norm_mlp_w", norm_mlp_w, m_norm_mlp_w, v_norm_mlp_w, d_nmlp),
              ("norm_final_w", norm_final_w, m_norm_final_w, v_norm_final_w, d_nf)]
    like = [s[1] for s in smalls]
    packed = _small_allreduce_adam(_pack([s[4] for s in smalls]), _pack(like), _pack([s[2] for s in smalls]),
                                   _pack([s[3] for s in smalls]), got_in)
    small_out = {s[0]: vals for s, vals in zip(smalls, zip(*[_unpack(p, like) for p in packed]))}

    loss = lax.psum(loss_part[0, 0], ("x", "y", "c"))
    order = ["w_in", "lb_logits", "hg_norm_w", "rel_bias", "w_branch_a", "w_branch_b", "w_out", "norm_mix_w",
             "norm_mlp_w", "w_up", "w_down", "norm_final_w"]
    res = {**big_out, **small_out}
    return (loss, grad_x.reshape(x.shape), *[res[n][0] for n in order], *[res[n][1] for n in order],
            *[res[n][2] for n in order], *[res[n][3] for n in order])
```

```python
import functools

import jax
import jax.numpy as jnp
from jax import lax
from jax.experimental import pallas as pl
from jax.experimental.pallas import tpu as pltpu

F32 = jnp.float32
BF16 = jnp.bfloat16
HIGHEST = lax.Precision.HIGHEST

D_MODEL = 2048
SEQ = 2048
CHUNK = 64
HG_HEADS = 8
HG_D = 128
AT_HEADS = 16
AT_DH = 64
LEFT = 8
REL_CLIP = 256
D_FF = 8192
EPS = 1e-6
ADAM_LR = 0.001
ADAM_B1 = 0.9
ADAM_B2 = 0.999
ADAM_EPS = 1e-08
ADAM_WD = 0.01
ADAM_STEP = 10

LANE = 128
NEG = -1e30
EXP_CLAMP = 80.0
VMEM_LIMIT = 48 * 1024 * 1024
MM_TM, MM_TN, MM_TK = 1024, 1024, 2816
ROW_TILE = 256
QB = 2 * CHUNK


def _hgw():
    return HG_HEADS * HG_D


def _atw():
    return AT_HEADS * AT_DH


def _cparams(sem):
    return pltpu.CompilerParams(dimension_semantics=sem, vmem_limit_bytes=VMEM_LIMIT)


def _sigmoid(x):
    return jax.nn.sigmoid(x)


def _dot(a, b, dims, precision=None):
    return lax.dot_general(a, b, (dims, ((), ())), preferred_element_type=F32, precision=precision)


def _nn(a, b, precision=None):
    return _dot(a, b, ((1,), (0,)), precision)


def _nt(a, b, precision=None):
    return _dot(a, b, ((1,), (1,)), precision)


def _tn(a, b, precision=None):
    return _dot(a, b, ((0,), (0,)), precision)


class _Side:
    def __init__(self, build, nsem, reads=(), aliased=(), fresh=()):
        self.build, self.nsem = build, nsem
        self.reads, self.aliased, self.fresh = list(reads), list(aliased), list(fresh)

    def operands(self):
        return self.reads + self.aliased

    def in_specs(self):
        return [ANY] * len(self.operands())

    def out_specs(self):
        return [ANY] * (len(self.aliased) + len(self.fresh))

    def out_shape(self):
        return [jax.ShapeDtypeStruct(a.shape, a.dtype) for a in self.aliased] + self.fresh

    def aliases(self, n_in, n_out):
        return {n_in + len(self.reads) + t: n_out + t for t in range(len(self.aliased))}

    def scratch(self):
        return [pltpu.SemaphoreType.DMA((self.nsem,)), pltpu.SemaphoreType.DMA((self.nsem,))]

    def hooks(self, in_refs, out_refs, sems, first, last):
        nr, na = len(self.reads), len(self.aliased)
        args = (in_refs[:nr], out_refs[:na], out_refs[na:], *sems)

        @pl.when(first)
        def _():
            for cp in self.build(*args):
                cp.start()

        @pl.when(last)
        def _():
            for cp in self.build(*args):
                cp.wait()


def _after(*tokens):
    return _Side(lambda *args: [], 1, reads=tokens)


def _side_parts(side):
    if side is None:
        return [], [], [], [], lambda n_in, n_out: {}, []
    return side.operands(), side.in_specs(), side.out_specs(), side.out_shape(), side.aliases, side.scratch()


def _call_with_side(body, name, grid, in_specs, out_specs, out_shape, scratch, sem, operands, side, n_prefetch=0,
                    aliases=None, borrow=None):
    _, _, s_out, s_shape, _, s_scr = _side_parts(side)
    n_in, n_out = n_prefetch + len(in_specs), len(out_specs)
    borrow = borrow or {}
    s_ops, s_alias = [], {}
    if side is not None:
        keep = [t for t in range(len(side.aliased)) if t not in borrow]
        s_ops = side.reads + [side.aliased[t] for t in keep]
        s_alias = {n_in + len(side.reads) + pos: n_out + t for pos, t in enumerate(keep)}
        s_alias.update({n_prefetch + i: n_out + t for t, i in borrow.items()})
    s_in = [ANY] * len(s_ops)
    n_sin, n_sout = len(s_ops), len(s_out)

    def wrapped(*refs):
        a, b, c = n_in + n_sin, n_in + n_sin + n_out, n_in + n_sin + n_out + n_sout
        ids = [pl.program_id(d) for d in range(len(grid))]
        first = functools.reduce(lambda p, q: p & q, [i == 0 for i in ids])
        last = functools.reduce(lambda p, q: p & q, [i == g - 1 for i, g in zip(ids, grid)])
        side.hooks(refs[n_in:a], refs[b:c], refs[-2:], first, last)
        body(*refs[:n_in], *refs[a:b], *refs[c:-2])

    spec = dict(grid=grid, in_specs=in_specs + s_in, out_specs=out_specs + s_out, scratch_shapes=scratch + s_scr)
    if n_prefetch:
        spec = dict(grid_spec=pltpu.PrefetchScalarGridSpec(num_scalar_prefetch=n_prefetch, **spec))
    return pl.pallas_call(
        body if side is None else wrapped, name=name, out_shape=out_shape + s_shape,
        input_output_aliases={**s_alias, **{n_prefetch + i: o for i, o in (aliases or {}).items()}},
        compiler_params=_cparams(sem if side is None else ("arbitrary",) * len(grid)), **spec,
    )(*operands, *s_ops)


def _mm_tk(K):
    if K <= MM_TK:
        return K
    return max(t for t in range(LANE, MM_TK + 1, LANE) if K % t == 0)


def _mm(name, a, b, mode, out_dtypes, extras=(), epilogue=None, side=None):
    if mode == "nn":
        (M, K), (K2, N) = a.shape, b.shape
    elif mode == "nt":
        (M, K), (N, K2) = a.shape, b.shape
    else:
        (K, M), (K2, N) = a.shape, b.shape
    assert K == K2, (name, a.shape, b.shape)
    tm, tn, tk = min(MM_TM, M), min(MM_TN, N), _mm_tk(K)
    assert M % tm == 0 and N % tn == 0 and K % tk == 0, (name, M, N, K)
    ni, nj, nk = M // tm, N // tn, K // tk
    ne, no = len(extras), len(out_dtypes)
    if epilogue is None:
        epilogue = lambda acc: (acc,)
    s_ops, s_in, s_out, s_shape, s_alias, s_scr = _side_parts(side)
    n_in, n_sin, n_sout = 2 + ne, len(s_ops), len(s_out)

    def body(*refs):
        a_ref, b_ref = refs[:2]
        extra_refs = refs[2:n_in]
        out_refs = refs[n_in + n_sin:n_in + n_sin + no]
        rest = refs[n_in + n_sin + no + n_sout:]
        i, j, k = pl.program_id(0), pl.program_id(1), pl.program_id(2)
        if side is not None:
            side.hooks(refs[n_in:n_in + n_sin], refs[n_in + n_sin + no:n_in + n_sin + no + n_sout], rest[-2:],
                       (i == 0) & (j == 0) & (k == 0), (i == ni - 1) & (j == nj - 1) & (k == nk - 1))
        av, bv = a_ref[...].astype(BF16), b_ref[...].astype(BF16)
        prod = _nn(av, bv) if mode == "nn" else _nt(av, bv) if mode == "nt" else _tn(av, bv)

        def finish(acc):
            res = epilogue(acc, *[e[...] for e in extra_refs])
            for o_ref, r in zip(out_refs, res):
                o_ref[...] = r.astype(o_ref.dtype)

        if nk == 1:
            finish(prod)
        else:
            acc_ref = rest[0]

            @pl.when(k == 0)
            def _():
                acc_ref[...] = prod

            @pl.when((k > 0) & (k < nk - 1))
            def _():
                acc_ref[...] += prod

            @pl.when(k == nk - 1)
            def _():
                finish(acc_ref[...] + prod)

    if mode == "nn":
        a_spec = pl.BlockSpec((tm, tk), lambda i, j, k: (i, k))
        b_spec = pl.BlockSpec((tk, tn), lambda i, j, k: (k, j))
    elif mode == "nt":
        a_spec = pl.BlockSpec((tm, tk), lambda i, j, k: (i, k))
        b_spec = pl.BlockSpec((tn, tk), lambda i, j, k: (j, k))
    else:
        a_spec = pl.BlockSpec((tk, tm), lambda i, j, k: (k, i))
        b_spec = pl.BlockSpec((tk, tn), lambda i, j, k: (k, j))
    o_spec = pl.BlockSpec((tm, tn), lambda i, j, k: (i, j))
    sem = ("arbitrary",) * 3 if side is not None else ("parallel", "parallel", "arbitrary")
    outs = pl.pallas_call(
        body, name=name,
        grid=(ni, nj, nk),
        in_specs=[a_spec, b_spec] + [o_spec] * ne + s_in,
        out_specs=[o_spec] * no + s_out,
        out_shape=[jax.ShapeDtypeStruct((M, N), dt) for dt in out_dtypes] + s_shape,
        input_output_aliases=s_alias(n_in, no),
        scratch_shapes=([pltpu.VMEM((tm, tn), F32)] if nk > 1 else []) + s_scr,
        compiler_params=_cparams(sem),
    )(a, b, *extras, *s_ops)
    return outs[0] if len(outs) == 1 else outs


def _row_spec(tr, d):
    return pl.BlockSpec((tr, d), lambda i: (i, 0))


def _vec_spec(d):
    return pl.BlockSpec((1, d), lambda i: (0, 0))


def _rms_fwd(name, x, w, side=None):
    T, D = x.shape
    tr = min(ROW_TILE, T)

    def body(x_ref, w_ref, o_ref):
        xf = x_ref[...]
        r = lax.rsqrt(jnp.mean(xf * xf, axis=-1, keepdims=True) + EPS)
        o_ref[...] = (xf * r * w_ref[...]).astype(BF16)

    outs = _call_with_side(body, name, (T // tr,), [_row_spec(tr, D), _vec_spec(D)], [_row_spec(tr, D)],
                           [jax.ShapeDtypeStruct((T, D), BF16)], [], ("parallel",), (x, w), side)
    return outs[0] if side is None else outs


def _rms_bwd(name, dy, h, w, dres, side=None):
    T, D = h.shape
    tr = min(ROW_TILE, T)

    def body(dy_ref, h_ref, w_ref, dres_ref, dh_ref, dhb_ref, dw_ref):
        @pl.when(pl.program_id(0) == 0)
        def _():
            dw_ref[...] = jnp.zeros_like(dw_ref)

        hf, dyv = h_ref[...], dy_ref[...]
        r = lax.rsqrt(jnp.mean(hf * hf, axis=-1, keepdims=True) + EPS)
        xhat = hf * r
        dw_ref[...] += jnp.sum(dyv * xhat, axis=0, keepdims=True)
        dxh = dyv * w_ref[...]
        dh = dres_ref[...] + r * (dxh - xhat * jnp.mean(dxh * xhat, axis=-1, keepdims=True))
        dh_ref[...] = dh
        dhb_ref[...] = dh.astype(BF16)

    return _call_with_side(
        body, name, (T // tr,),
        [_row_spec(tr, D), _row_spec(tr, D), _vec_spec(D), _row_spec(tr, D)],
        [_row_spec(tr, D), _row_spec(tr, D), _vec_spec(D)],
        [jax.ShapeDtypeStruct((T, D), F32), jax.ShapeDtypeStruct((T, D), BF16), jax.ShapeDtypeStruct((1, D), F32)],
        [], ("arbitrary",), (dy, h, w, dres), side)


def _loss_head(h2, target, w):
    T, D = h2.shape
    tr = min(ROW_TILE, T)

    def body(h_ref, t_ref, w_ref, loss_ref, dh_ref, dhb_ref, dw_ref):
        @pl.when(pl.program_id(0) == 0)
        def _():
            dw_ref[...] = jnp.zeros_like(dw_ref)
            loss_ref[...] = jnp.zeros_like(loss_ref)

        hf, wv = h_ref[...], w_ref[...]
        r = lax.rsqrt(jnp.mean(hf * hf, axis=-1, keepdims=True) + EPS)
        xhat = hf * r
        diff = xhat * wv - t_ref[...]
        loss_ref[...] += 0.5 * jnp.sum(jnp.mean(diff * diff, axis=-1, keepdims=True))
        dyv = diff * (1.0 / D)
        dw_ref[...] += jnp.sum(dyv * xhat, axis=0, keepdims=True)
        dxh = dyv * wv
        dh = r * (dxh - xhat * jnp.mean(dxh * xhat, axis=-1, keepdims=True))
        dh_ref[...] = dh
        dhb_ref[...] = dh.astype(BF16)

    return pl.pallas_call(
        body, name="loss_head", grid=(T // tr,),
        in_specs=[_row_spec(tr, D), _row_spec(tr, D), _vec_spec(D)],
        out_specs=[_vec_spec(LANE), _row_spec(tr, D), _row_spec(tr, D), _vec_spec(D)],
        out_shape=[jax.ShapeDtypeStruct((1, LANE), F32), jax.ShapeDtypeStruct((T, D), F32),
                   jax.ShapeDtypeStruct((T, D), BF16), jax.ShapeDtypeStruct((1, D), F32)],
        compiler_params=_cparams(("arbitrary",)),
    )(h2, target, w)


def _gate_tiles(T, D):
    goff = 4 * _hgw() + 3 * _atw()
    tc = min(1024, D)
    assert goff % tc == 0 and D % tc == 0
    return min(ROW_TILE, T), tc, goff // tc, D // tc


def _merge(z, pa, pb, side=None):
    T, D = pa.shape
    tr, tc, g0, nd = _gate_tiles(T, D)

    def body(ga_ref, gb_ref, pa_ref, pb_ref, o_ref):
        o_ref[...] = (_sigmoid(ga_ref[...]) * pa_ref[...] + _sigmoid(gb_ref[...]) * pb_ref[...]).astype(BF16)

    t = pl.BlockSpec((tr, tc), lambda i, j: (i, j))
    outs = _call_with_side(
        body, "merge", (T // tr, nd),
        [pl.BlockSpec((tr, tc), lambda i, j: (i, g0 + j)), pl.BlockSpec((tr, tc), lambda i, j: (i, g0 + nd + j)), t, t],
        [t], [jax.ShapeDtypeStruct((T, D), BF16)], [], ("parallel", "parallel"), (z, z, pa, pb), side)
    return outs[0] if side is None else outs


def _dmerge(dm, z, pa, pb):
    T, D = pa.shape
    tr, tc, g0, nd = _gate_tiles(T, D)

    def body(dm_ref, ga_ref, gb_ref, pa_ref, pb_ref, dpa_ref, dpb_ref, dga_ref, dgb_ref):
        dmv = dm_ref[...]
        sa, sb = _sigmoid(ga_ref[...]), _sigmoid(gb_ref[...])
        dpa_ref[...] = (dmv * sa).astype(BF16)
        dpb_ref[...] = (dmv * sb).astype(BF16)
        dga_ref[...] = (dmv * pa_ref[...] * sa * (1.0 - sa)).astype(BF16)
        dgb_ref[...] = (dmv * pb_ref[...] * sb * (1.0 - sb)).astype(BF16)

    t = pl.BlockSpec((tr, tc), lambda i, j: (i, j))
    return pl.pallas_call(
        body, name="dmerge", grid=(T // tr, nd),
        in_specs=[t, pl.BlockSpec((tr, tc), lambda i, j: (i, g0 + j)),
                  pl.BlockSpec((tr, tc), lambda i, j: (i, g0 + nd + j)), t, t],
        out_specs=[t, t, t, t],
        out_shape=[jax.ShapeDtypeStruct((T, D), BF16)] * 4,
        compiler_params=_cparams(("parallel", "parallel")),
    )(dm, z, z, pa, pb)


def _hg_gates(xq, xf, lb):
    f = _sigmoid(xf)
    g = lb + (1.0 - lb) * f
    sq = _sigmoid(xq)
    return f, g, jnp.log(g), 1.0 - g, sq, xq * sq * (HG_D ** -0.5)


def _split2(x):
    hi = x.astype(BF16)
    return hi, (x - hi.astype(F32)).astype(BF16)


def _tri_sum(tri, x):
    hi, rest = x.astype(BF16), x - x.astype(BF16).astype(F32)
    mid, lo = _split2(rest)
    return _nn(tri, lo) + _nn(tri, mid) + _nn(tri, hi)


def _hg_decays(lg, tri_incl, rowi):
    b = _tri_sum(tri_incl, lg)
    b_last = jnp.sum(lg, axis=0, keepdims=True)
    b_mid = jnp.sum(jnp.where(rowi <= CHUNK // 2, lg, 0.0), axis=0, keepdims=True)
    return b, b_last, b_mid


HG_GROUP = 2


def _hg_in_specs(T):
    ng = HG_HEADS // HG_GROUP
    return [pl.BlockSpec((T, HG_GROUP * HG_D), lambda h, s=s: (0, s * ng + h)) for s in range(4)]


def _hg_fwd(z, lb_logits, hgw, side=None):
    T = z.shape[0]
    H, d, C, G = HG_HEADS, HG_D, CHUNK, HG_GROUP
    nc = T // C

    def body(hq_ref, hf_ref, hi_ref, hg_ref, lbl_ref, w_ref, ya_ref, o_ref, s_ref):
        lb_all = 1.0 / (1.0 + jnp.exp(lbl_ref[1:2, :] - lbl_ref[0:1, :]))
        wv = w_ref[...]
        row = lax.broadcasted_iota(jnp.int32, (C, C), 0)
        col = lax.broadcasted_iota(jnp.int32, (C, C), 1)
        tril = col <= row
        tri_incl = tril.astype(BF16)
        rowi = lax.broadcasted_iota(jnp.int32, (C, G * d), 0)
        lanes = [slice(hh * d, (hh + 1) * d) for hh in range(G)]
        per_head = lambda fn: jnp.concatenate([fn(hh, sl) for hh, sl in enumerate(lanes)], axis=1)
        wv_all = jnp.tile(wv, (1, G))

        def chunk(c, states):
            rows = pl.ds(pl.multiple_of(c * C, C), C)
            xq, xf, v, xg = hq_ref[rows, :], hf_ref[rows, :], hi_ref[rows, :], hg_ref[rows, :]
            _, _, lg, kk, _, q = _hg_gates(xq, xf, lb_all)
            b, b_last, b_mid = _hg_decays(lg, tri_incl, rowi)
            vb, qe = v.astype(BF16), (q * jnp.exp(b)).astype(BF16)
            qt = (q * jnp.exp(b - b_mid)).astype(BF16)
            kt = (kk * jnp.exp(jnp.minimum(b_mid - b, EXP_CLAMP))).astype(BF16)
            kd, e_last = (kk * jnp.exp(b_last - b)).astype(BF16), jnp.exp(b_last)
            for hh, st in enumerate(states):
                s_ref[hh, c] = st
            o = per_head(lambda hh, sl: _nt(qe[:, sl], states[hh].astype(BF16)))
            a = [jnp.where(tril, _nt(qt[:, sl], kt[:, sl]), 0.0).astype(BF16) for sl in lanes]
            o = o + per_head(lambda hh, sl: _nn(a[hh], vb[:, sl]))
            o_ref[rows, :] = o
            r = per_head(lambda hh, sl: jnp.broadcast_to(
                lax.rsqrt(jnp.mean(o[:, sl] * o[:, sl], axis=-1, keepdims=True) + EPS), (C, d)))
            ya_ref[rows, :] = (o * r * wv_all * (xg * _sigmoid(xg))).astype(BF16)
            return tuple(st * e_last[:, sl] + _tn(vb[:, sl], kd[:, sl]) for st, sl in zip(states, lanes))

        lax.fori_loop(0, nc, chunk, tuple(jnp.zeros((d, d), F32) for _ in range(G)))

    heads = pl.BlockSpec((T, G * d), lambda h: (0, h))
    return _call_with_side(
        body, "hg_fwd", (H // G,),
        _hg_in_specs(T) + [pl.BlockSpec((2, G * d), lambda h: (0, h)), pl.BlockSpec((1, d), lambda h: (0, 0))],
        [heads, heads, pl.BlockSpec((G, nc, d, d), lambda h: (h, 0, 0, 0))],
        [jax.ShapeDtypeStruct((T, H * d), BF16), jax.ShapeDtypeStruct((T, H * d), F32),
         jax.ShapeDtypeStruct((H, nc, d, d), F32)],
        [], ("parallel",), (z, z, z, z, lb_logits, hgw), side)


def _hg_bwd(z, o, dya, states, lb_logits, hgw, side=None):
    T = z.shape[0]
    H, d, C, G = HG_HEADS, HG_D, CHUNK, HG_GROUP
    nc = T // C
    scale = HG_D ** -0.5

    def body(hq_ref, hf_ref, hi_ref, hg_ref, o_ref, dy_ref, s_ref, lbl_ref, w_ref,
             dq_ref, df_ref, di_ref, dg_ref, dlbl_ref, dw_ref, acc_ref):
        lb_all = 1.0 / (1.0 + jnp.exp(lbl_ref[1:2, :] - lbl_ref[0:1, :]))
        wv = w_ref[...]
        row = lax.broadcasted_iota(jnp.int32, (C, C), 0)
        col = lax.broadcasted_iota(jnp.int32, (C, C), 1)
        tril = col <= row
        tri_incl = tril.astype(BF16)
        triu_incl = (col >= row).astype(BF16)
        rowi = lax.broadcasted_iota(jnp.int32, (C, G * d), 0)
        lanes = [slice(hh * d, (hh + 1) * d) for hh in range(G)]
        per_head = lambda fn: jnp.concatenate([fn(hh, sl) for hh, sl in enumerate(lanes)], axis=1)
        head_mean = lambda x: per_head(
            lambda hh, sl: jnp.broadcast_to(jnp.mean(x[:, sl], axis=-1, keepdims=True), (C, d)))
        wv_all = jnp.tile(wv, (1, G))
        lb = lb_all
        acc_ref[...] = jnp.zeros_like(acc_ref)

        @pl.when(pl.program_id(0) == 0)
        def _():
            dw_ref[...] = jnp.zeros_like(dw_ref)

        def chunk(i, carry):
            dsts, tail = carry
            c = nc - 1 - i
            rows = pl.ds(pl.multiple_of(c * C, C), C)
            xq, xf, v, xg = hq_ref[rows, :], hf_ref[rows, :], hi_ref[rows, :], hg_ref[rows, :]
            f, g, lg, kk, sq, q = _hg_gates(xq, xf, lb)
            b, b_last, b_mid = _hg_decays(lg, tri_incl, rowi)
            e_b, e_qm, e_km = jnp.exp(b), jnp.exp(b - b_mid), jnp.exp(jnp.minimum(b_mid - b, EXP_CLAMP))
            e_kl, e_last = jnp.exp(b_last - b), jnp.exp(b_last)
            ov, dy = o_ref[rows, :], dy_ref[rows, :]
            r = lax.rsqrt(head_mean(ov * ov) + EPS)
            xhat = ov * r
            sg = _sigmoid(xg)
            dxg = dy * xhat * wv_all * (sg * (1.0 + xg * (1.0 - sg)))
            dyn = dy * (xg * sg)
            acc_ref[0:1, :] += jnp.sum(dyn * xhat, axis=0, keepdims=True)
            dxh = dyn * wv_all
            dof = r * (dxh - xhat * head_mean(dxh * xhat))
            do, vb = dof.astype(BF16), v.astype(BF16)
            qe, kd, qt, kt = (q * e_b).astype(BF16), (kk * e_kl).astype(BF16), (q * e_qm).astype(BF16), (kk * e_km).astype(BF16)
            pm = [jnp.where(tril, _nt(do[:, sl], vb[:, sl]), 0.0).astype(BF16) for sl in lanes]
            am = [jnp.where(tril, _nt(qt[:, sl], kt[:, sl]), 0.0).astype(BF16) for sl in lanes]
            st = [_split2(s_ref[hh, c]) for hh in range(G)]
            ds = [_split2(x) for x in dsts]
            dq_state = per_head(lambda hh, sl: _nn(do[:, sl], st[hh][1]) + _nn(do[:, sl], st[hh][0]))
            dk_state = per_head(lambda hh, sl: _nn(vb[:, sl], ds[hh][1]) + _nn(vb[:, sl], ds[hh][0]))
            dq_intra = per_head(lambda hh, sl: _nn(pm[hh], kt[:, sl]))
            dk_intra = per_head(lambda hh, sl: _tn(pm[hh], qt[:, sl]))
            dv = per_head(lambda hh, sl: _tn(am[hh], do[:, sl]) + _nt(kd[:, sl], ds[hh][0]))
            new_dsts = tuple(x * e_last[:, sl] + _tn(do[:, sl], qe[:, sl]) for x, sl in zip(dsts, lanes))
            dq = dq_state * e_b + dq_intra * e_qm
            dk = dk_intra * e_km + dk_state * e_kl
            db = (qe.astype(F32) * dq_state + qt.astype(F32) * dq_intra
                  - kt.astype(F32) * dk_intra - kd.astype(F32) * dk_state)
            dlg = _tri_sum(triu_incl, db) + tail
            dgate = dlg / g - dk
            acc_ref[1:2, :] += jnp.sum(dgate * (1.0 - f), axis=0, keepdims=True)
            dq_ref[rows, :] = (dq * scale * (sq * (1.0 + xq * (1.0 - sq)))).astype(BF16)
            df_ref[rows, :] = (dgate * (1.0 - lb) * f * (1.0 - f)).astype(BF16)
            di_ref[rows, :] = dv.astype(BF16)
            dg_ref[rows, :] = dxg.astype(BF16)
            return new_dsts, tail + jnp.sum(db, axis=0, keepdims=True)

        lax.fori_loop(0, nc, chunk, (tuple(jnp.zeros((d, d), F32) for _ in range(G)), jnp.zeros((1, G * d), F32)))
        dw_ref[...] += functools.reduce(lambda p, q: p + q, [acc_ref[0:1, sl] for sl in lanes])
        dl0 = acc_ref[1:2, :] * lb_all * (1.0 - lb_all)
        dlbl_ref[0:1, :] = dl0
        dlbl_ref[1:2, :] = -dl0

    heads = pl.BlockSpec((T, G * d), lambda h: (0, h))
    logits = pl.BlockSpec((2, G * d), lambda h: (0, h))
    return _call_with_side(
        body, "hg_bwd", (H // G,),
        _hg_in_specs(T) + [heads, heads, pl.BlockSpec((G, nc, d, d), lambda h: (h, 0, 0, 0)), logits,
                           pl.BlockSpec((1, d), lambda h: (0, 0))],
        [heads, heads, heads, heads, logits, pl.BlockSpec((1, d), lambda h: (0, 0))],
        [jax.ShapeDtypeStruct((T, H * d), BF16)] * 4 + [jax.ShapeDtypeStruct((2, H * d), F32),
                                                        jax.ShapeDtypeStruct((1, d), F32)],
        [pltpu.VMEM((8, G * d), F32)], ("arbitrary",), (z, z, z, z, o, dya, states, lb_logits, hgw), side)


def _at_dims():
    pad = LEFT * CHUNK
    return pad, QB + pad, AT_HEADS * AT_DH // LANE, 4 * _hgw() // LANE


def _rel_of_period():
    pad, W, _, _ = _at_dims()
    n = jnp.arange(QB + W)
    return jnp.clip(pad - jnp.where(n < W, n, n - (QB + W)), -REL_CLIP, REL_CLIP) + REL_CLIP


def _bias_window(rel_bias):
    pad, W, _, _ = _at_dims()
    H, P = rel_bias.shape[0], QB + W
    per = rel_bias[:, _rel_of_period()]
    win = jnp.tile(per, (1, QB))[:, :QB * (P - 1)].reshape(H, QB, P - 1)[:, :, :W]
    t = jnp.arange(QB)[:, None]
    j = jnp.arange(W)[None, :]
    ok = (j // CHUNK >= t // CHUNK) & (j // CHUNK <= t // CHUNK + LEFT)
    return jnp.where(ok[None], win, NEG)


def _bias_window_grad(dbw):
    pad, W, _, _ = _at_dims()
    H, P = dbw.shape[0], QB + W
    flat = jnp.pad(dbw, ((0, 0), (0, 0), (0, P - 1 - W))).reshape(H, QB * (P - 1))
    per = jnp.pad(flat, ((0, 0), (0, QB))).reshape(H, QB, P).sum(axis=1)
    onehot = _rel_of_period()[:, None] == jnp.arange(2 * REL_CLIP + 1)[None, :]
    return jnp.dot(per, onehot.astype(F32), precision=HIGHEST)


def _at_stack(x):
    first = lax.broadcasted_iota(jnp.int32, x.shape, 1) < AT_DH
    return jnp.concatenate([jnp.where(first, x, 0.0), jnp.where(first, 0.0, x)], axis=0).astype(BF16)


def _at_unstack(x):
    first = lax.broadcasted_iota(jnp.int32, (QB, LANE), 1) < AT_DH
    return jnp.where(first, x[:QB], x[QB:])


def _at_softmax(qs, kw, bias_ref, qi):
    pad, W, _, _ = _at_dims()
    s = _nt(qs, kw) + bias_ref[...].reshape(2 * QB, W)
    valid = lax.broadcasted_iota(jnp.int32, (2 * QB, W), 1) + qi * QB >= pad
    s = jnp.where(valid, s, NEG)
    e = jnp.exp(s - jnp.max(s, axis=-1, keepdims=True))
    return e * (1.0 / jnp.sum(e, axis=-1, keepdims=True))


def _at_fwd(z, bias_win, side=None):
    T = z.shape[0]
    pad, W, HP, c0 = _at_dims()
    nq = T // QB

    def body(q_ref, k_ref, v_ref, bias_ref, o_ref, kpad, vpad):
        qi = pl.program_id(1)

        @pl.when(qi == 0)
        def _():
            kpad[0:pad, :] = jnp.zeros((pad, LANE), BF16)
            vpad[0:pad, :] = jnp.zeros((pad, LANE), BF16)
            kpad[pad:, :] = k_ref[...].astype(BF16)
            vpad[pad:, :] = v_ref[...].astype(BF16)

        win = pl.ds(pl.multiple_of(qi * QB, QB), W)
        kw, vw = kpad[win, :], vpad[win, :]
        p = _at_softmax(_at_stack(q_ref[...] * (AT_DH ** -0.5)), kw, bias_ref, qi)
        o_ref[...] = _at_unstack(_nn(p.astype(BF16), vw)).astype(BF16)

    full = lambda s: pl.BlockSpec((T, LANE), lambda hp, qi, s=s: (0, c0 + s * HP + hp))
    return _call_with_side(
        body, "at_fwd", (HP, nq),
        [pl.BlockSpec((QB, LANE), lambda hp, qi: (qi, c0 + hp)), full(1), full(2),
         pl.BlockSpec((2, QB, W), lambda hp, qi: (hp, 0, 0))],
        [pl.BlockSpec((QB, LANE), lambda hp, qi: (qi, hp))],
        [jax.ShapeDtypeStruct((T, HP * LANE), BF16)],
        [pltpu.VMEM((T + pad, LANE), BF16)] * 2, ("parallel", "arbitrary"), (z, z, z, bias_win), side)


def _at_bwd(z, dyb, bias_win, side=None):
    T = z.shape[0]
    pad, W, HP, c0 = _at_dims()
    nq = T // QB
    scale = AT_DH ** -0.5

    def body(q_ref, k_ref, v_ref, do_ref, bias_ref, dq_ref, dk_ref, dv_ref, dbias_ref, kpad, vpad, dkpad, dvpad):
        qi = pl.program_id(1)

        @pl.when(qi == 0)
        def _():
            kpad[0:pad, :] = jnp.zeros((pad, LANE), BF16)
            vpad[0:pad, :] = jnp.zeros((pad, LANE), BF16)
            kpad[pad:, :] = k_ref[...].astype(BF16)
            vpad[pad:, :] = v_ref[...].astype(BF16)
            dkpad[...] = jnp.zeros_like(dkpad)
            dvpad[...] = jnp.zeros_like(dvpad)
            dbias_ref[...] = jnp.zeros_like(dbias_ref)

        win = pl.ds(pl.multiple_of(qi * QB, QB), W)
        kw, vw = kpad[win, :], vpad[win, :]
        qs, dos = _at_stack(q_ref[...] * scale), _at_stack(do_ref[...])
        p = _at_softmax(qs, kw, bias_ref, qi)
        dp = _nt(dos, vw)
        ds = p * (dp - jnp.sum(p * dp, axis=-1, keepdims=True))
        dbias_ref[...] += ds.reshape(2, QB, W)
        dsb = ds.astype(BF16)
        dq_ref[...] = (_at_unstack(_nn(dsb, kw)) * scale).astype(BF16)
        dkpad[win, :] += _tn(dsb, qs)
        dvpad[win, :] += _tn(p.astype(BF16), dos)

        @pl.when(qi == nq - 1)
        def _():
            dk_ref[...] = dkpad[pad:, :].astype(BF16)
            dv_ref[...] = dvpad[pad:, :].astype(BF16)

    full = lambda s: pl.BlockSpec((T, LANE), lambda hp, qi, s=s: (0, c0 + s * HP + hp))
    blk = pl.BlockSpec((QB, LANE), lambda hp, qi: (qi, hp))
    col = pl.BlockSpec((T, LANE), lambda hp, qi: (0, hp))
    bw = pl.BlockSpec((2, QB, W), lambda hp, qi: (hp, 0, 0))
    return _call_with_side(
        body, "at_bwd", (HP, nq),
        [pl.BlockSpec((QB, LANE), lambda hp, qi: (qi, c0 + hp)), full(1), full(2), blk, bw],
        [blk, col, col, bw],
        [jax.ShapeDtypeStruct((T, HP * LANE), BF16)] * 3 + [jax.ShapeDtypeStruct(bias_win.shape, F32)],
        [pltpu.VMEM((T + pad, LANE), BF16)] * 2 + [pltpu.VMEM((T + pad, LANE), F32)] * 2,
        ("parallel", "arbitrary"), (z, z, z, dyb, bias_win), side)


def _piece_tiles(name, full_shape):
    pr, pc = _piece_shape(name, full_shape)
    tr = min(ROW_TILE, pr)
    assert pr % tr == 0
    nt = pr // tr
    if name in ROW_SHARDED:
        return tr, nt, lambda q, half, i: ((2 * q + half) * nt + i, 0)
    return tr, nt, lambda q, half, i: (half * nt + i, q)


def _cast_into_full(name, wq, place, side=None):
    full = _full_shape(name, wq.shape)
    pc = wq.shape[1]
    tr, nt, at = _piece_tiles(name, full)

    def body(place_ref, w_ref, o_ref):
        o_ref[...] = w_ref[...].astype(BF16)

    outs = _call_with_side(
        body, "cast_" + name, (2, nt), [pl.BlockSpec((tr, pc), lambda h, i, s: (h * nt + i, 0))],
        [pl.BlockSpec((tr, pc), lambda h, i, s: at(s[0], h, i))], [jax.ShapeDtypeStruct(full, BF16)],
        [], ("parallel", "parallel"), (place, wq), side, n_prefetch=1)
    return outs[0] if side is None else outs


def _g_w_in_half(u1, dz, place, own, side=None):
    T, K = u1.shape
    N = dz.shape[1]
    hk, tn = K // 2, min(MM_TN, N)
    half = (lambda s: s[1]) if own else (lambda s: 1 - s[1])

    def body(place_ref, a_ref, b_ref, o_ref):
        o_ref[...] = _tn(a_ref[...], b_ref[...]).astype(BF16)

    outs = _call_with_side(
        body, "g_w_in_keep" if own else "g_w_in_send", (N // tn,),
        [pl.BlockSpec((T, hk), lambda j, s: (0, half(s))), pl.BlockSpec((T, tn), lambda j, s: (0, j))],
        [pl.BlockSpec((hk, tn), lambda j, s: (0, j))], [jax.ShapeDtypeStruct((hk, N), BF16)],
        [], ("parallel",), (place, u1, dz), side, n_prefetch=1)
    return outs[0] if side is None else outs


def _chip_sum(name, grad, theirs, place, kept_rows=False):
    pr, pc = theirs.shape[1:]
    tr, nt, at = _piece_tiles(name, (2 * grad.shape[0], grad.shape[1]) if kept_rows else grad.shape)
    if kept_rows:
        at = lambda q, half, i: (i, q)

    def body(place_ref, g_ref, t_ref, o_ref):
        o_ref[...] = (g_ref[...].astype(F32) + t_ref[...].astype(F32)).astype(BF16)

    piece = pl.BlockSpec((None, tr, pc), lambda q, i, s: (q, i, 0))
    return pl.pallas_call(
        body, name="chip_sum_" + name,
        grid_spec=pltpu.PrefetchScalarGridSpec(
            num_scalar_prefetch=1, grid=(4, nt),
            in_specs=[pl.BlockSpec((tr, pc), lambda q, i, s: at(q, s[1], i)), piece], out_specs=piece),
        out_shape=jax.ShapeDtypeStruct(theirs.shape, BF16),
        compiler_params=_cparams(("parallel", "parallel")),
    )(place, grad, theirs)


def _piece_sum(name, chip_sums, got, place):
    pr, pc = chip_sums.shape[1:]
    tr = min(ROW_TILE, pr)

    def body(place_ref, own_ref, got_ref, o_ref):
        o_ref[...] = (own_ref[...].astype(F32) + got_ref[0].astype(F32) + got_ref[1].astype(F32)
                      + got_ref[2].astype(F32))

    return pl.pallas_call(
        body, name="piece_sum_" + name,
        grid_spec=pltpu.PrefetchScalarGridSpec(
            num_scalar_prefetch=1, grid=(pr // tr,),
            in_specs=[pl.BlockSpec((None, tr, pc), lambda i, s: (s[0], i, 0)),
                      pl.BlockSpec((3, tr, pc), lambda i, s: (0, i, 0))],
            out_specs=pl.BlockSpec((tr, pc), lambda i, s: (i, 0))),
        out_shape=jax.ShapeDtypeStruct((pr, pc), F32),
        compiler_params=_cparams(("parallel",)),
    )(place, chip_sums, got)


def _adam_quarter(name, w, m, v, g_mine, g_sib, place, side=None):
    pr, pc = g_mine.shape
    tr = min(ROW_TILE // 2, pr)
    nt = pr // tr

    def body(place_ref, w_ref, m_ref, v_ref, gm_ref, gs_ref, go_ref, d_ref, mo_ref, vo_ref):
        g = jnp.where(pl.program_id(0) == place_ref[1], gm_ref[...], gs_ref[...])
        delta, mn, vn = _adam_math(w_ref[...], g, m_ref[...], v_ref[...])
        go_ref[...] = g
        d_ref[...] = delta
        mo_ref[...] = mn
        vo_ref[...] = vn

    quarter = pl.BlockSpec((tr, pc), lambda h, i, s: (h * nt + i, 0))
    mine = pl.BlockSpec((tr, pc), lambda h, i, s: (jnp.where(h == s[1], i, 0), 0))
    sib = pl.BlockSpec((tr, pc), lambda h, i, s: (jnp.where(h == s[1], 0, i), 0))
    return _call_with_side(
        body, "adam_" + name, (2, nt), [quarter, quarter, quarter, mine, sib], [quarter] * 4,
        [jax.ShapeDtypeStruct(w.shape, F32)] * 4, [], ("parallel", "parallel"),
        (place, w, m, v, g_mine, g_sib), side, n_prefetch=1)


def _adam_math(w, g, m, v):
    m = ADAM_B1 * m + (1.0 - ADAM_B1) * g
    v = ADAM_B2 * v + (1.0 - ADAM_B2) * (g * g)
    m_hat = m / (1.0 - ADAM_B1 ** ADAM_STEP)
    v_hat = v / (1.0 - ADAM_B2 ** ADAM_STEP)
    return -ADAM_LR * (m_hat / (jnp.sqrt(v_hat) + ADAM_EPS) + ADAM_WD * w), m, v


WEIGHTS = ("w_in", "w_branch_a", "w_branch_b", "w_out", "w_up", "w_down")
ROW_SHARDED = ("w_out", "w_down")
ANY = pl.BlockSpec(memory_space=pl.ANY)
MESH = pl.DeviceIdType.MESH


def _place():
    x, y, c = lax.axis_index("x"), lax.axis_index("y"), lax.axis_index("c")
    chips = [(1 - x, y), (x, 1 - y), (1 - x, 1 - y)]
    return x, y, c, 2 * x + y, chips, [2 * cx + cy for cx, cy in chips]


def _piece(full_ref, name, q, half):
    K, N = full_ref.shape
    if name in ROW_SHARDED:
        rows = K // 8
        return full_ref.at[pl.ds(q * (2 * rows) + half * rows, rows), :]
    return full_ref.at[pl.ds(half * (K // 2), K // 2), pl.ds(q * (N // 4), N // 4)]


def _piece_shape(name, full_shape):
    K, N = full_shape
    return (K // 8, N) if name in ROW_SHARDED else (K // 2, N // 4)


def _full_shape(name, quarter_shape):
    Kq, Nq = quarter_shape
    return (4 * Kq, Nq) if name in ROW_SHARDED else (Kq, 4 * Nq)


def _remote(src, dst, send_sem, recv_sem, device):
    return pltpu.make_async_remote_copy(src_ref=src, dst_ref=dst, send_sem=send_sem, recv_sem=recv_sem,
                                        device_id=device, device_id_type=MESH)


def _z_part(u1, w_in, z_prev, place, k0, count, side=None):
    T, K = u1.shape
    N = w_in.shape[1]
    nq = N // 4
    tn = nq // 2 if (nq // 2) % LANE == 0 else nq
    tm = min(MM_TM, T)
    per = nq // tn
    col = lambda g, j, s: (s[0] ^ (k0 + g)) * per + j
    ins = [pl.BlockSpec((tm, K), lambda g, i, j, s: (i, 0)), pl.BlockSpec((K, tn), lambda g, i, j, s: (0, col(g, j, s)))]
    operands = [place, u1, w_in]
    if z_prev is not None:
        ins.append(ANY)
        operands.append(z_prev)

    def body(place_ref, a_ref, b_ref, *rest):
        rest[-1][...] = _nn(a_ref[...], b_ref[...])

    return _call_with_side(
        body, "z_part_%d" % k0, (count, T // tm, per), ins,
        [pl.BlockSpec((tm, tn), lambda g, i, j, s: (i, col(g, j, s)))], [jax.ShapeDtypeStruct((T, N), F32)],
        [], ("parallel",) * 3, tuple(operands), side, n_prefetch=1, aliases={} if z_prev is None else {2: 0},
        borrow={0: 1} if side is not None and side.aliased and side.aliased[0] is w_in else None)


def _rows(ref, span):
    return ref if span is None else ref.at[pl.ds(span[0], span[1]), :]


def _gather_moves(items):
    count = {"near": lambda arg: 2, "far": lambda arg: 1, "pass": len}

    def build(reads, aliased, fresh, send_sems, recv_sems, off=0):
        x, y, c, p, chips, chip_ids = _place()
        south = c == 0
        far_src = jnp.where(south, chip_ids[0], chip_ids[1])
        far_dst = (jnp.where(south, x, 1 - x), jnp.where(south, 1 - y, y), c)
        out = []

        def add(ref, device):
            k = off + len(out)
            out.append(_remote(ref, ref, send_sems.at[k], recv_sems.at[k], device))

        for (name, _, moves), ref in zip(items, aliased):
            for kind, arg in moves:
                if kind == "near":
                    for chip in chips[:2]:
                        add(_rows(_piece(ref, name, p, c), arg), (*chip, c))
                elif kind == "far":
                    add(_rows(_piece(ref, name, far_src, c), arg), far_dst)
                else:
                    for j in arg:
                        add(_piece(ref, name, chip_ids[j], c), (x, y, 1 - c))
        return out

    nsem = sum(count[kind](arg) for _, _, moves in items for kind, arg in moves)
    return _Side(build, nsem, aliased=[a for _, a, _ in items])


def _ici_near(names, fulls, rows=None):
    return _gather_moves([(n, a, [("near", r)]) for n, a, r in zip(names, fulls, rows or [None] * len(names))])


def _ici_far(names, fulls, rows=None):
    return _gather_moves([(n, a, [("far", r)]) for n, a, r in zip(names, fulls, rows or [None] * len(names))])


def _d2d_gather(names, fulls, which=(0, 1, 2)):
    return _gather_moves([(n, a, [("pass", which)]) for n, a in zip(names, fulls)])


def _sib_send(names, grads):
    def build(reads, aliased, fresh, send_sems, recv_sems, off=0):
        x, y, c, _, _, _ = _place()
        out = []
        for i, name in enumerate(names):
            for q in range(4):
                k = off + 4 * i + q
                out.append(_remote(_piece(reads[i], name, q, 1 - c), fresh[i].at[q], send_sems.at[k], recv_sems.at[k],
                                   (x, y, 1 - c)))
        return out

    shapes = [jax.ShapeDtypeStruct((4,) + _piece_shape(name, g.shape), BF16) for name, g in zip(names, grads)]
    return _Side(build, 4 * len(names), reads=grads, fresh=shapes)


def _sib_send_half(sent):
    K2, N = sent.shape

    def build(reads, aliased, fresh, send_sems, recv_sems, off=0):
        x, y, c, _, _, _ = _place()
        return [_remote(reads[0].at[:, pl.ds(q * (N // 4), N // 4)], fresh[0].at[q], send_sems.at[off + q],
                        recv_sems.at[off + q], (x, y, 1 - c)) for q in range(4)]

    return _Side(build, 4, reads=[sent], fresh=[jax.ShapeDtypeStruct((4, K2, N // 4), BF16)])


def _chip_exchange(chip_sums, rows=None, got=None):
    rows = rows or [None] * len(chip_sums)

    def build(reads, aliased, fresh, send_sems, recv_sems, off=0):
        _, _, c, _, chips, chip_ids = _place()
        out = []
        for i in range(len(chip_sums)):
            for j, (chip, cid) in enumerate(zip(chips, chip_ids)):
                k = off + 3 * i + j
                out.append(_remote(_rows(reads[i].at[cid], rows[i]), _rows((aliased or fresh)[i].at[j], rows[i]),
                                   send_sems.at[k], recv_sems.at[k], (*chip, c)))
        return out

    if got is not None:
        return _Side(build, 3 * len(chip_sums), reads=chip_sums, aliased=got)
    shapes = [jax.ShapeDtypeStruct((3,) + s.shape[1:], BF16) for s in chip_sums]
    return _Side(build, 3 * len(chip_sums), reads=chip_sums, fresh=shapes)


HBM = pl.BlockSpec(memory_space=pltpu.HBM)
SEM = pl.BlockSpec(memory_space=pltpu.SEMAPHORE)


def _exchange_copies(s_refs, land_refs, send_sems, recv_sems):
    _, _, c, _, chips, chip_ids = _place()
    return [_remote(s_ref.at[cid], land_ref.at[j], send_sems.at[3 * i + j], recv_sems.at[3 * i + j], (*chip, c))
            for i, (s_ref, land_ref) in enumerate(zip(s_refs, land_refs))
            for j, (chip, cid) in enumerate(zip(chips, chip_ids))]


def _exchange_start(name, chip_sums):
    n = len(chip_sums)

    def body(*refs):
        for cp in _exchange_copies(refs[:n], refs[n:2 * n], refs[2 * n], refs[2 * n + 1]):
            cp.start()
        refs[-1][...] = jnp.zeros_like(refs[-1])

    lands = [jax.ShapeDtypeStruct((3,) + s.shape[1:], s.dtype) for s in chip_sums]
    hbm = lambda a: pltpu.with_memory_space_constraint(a, pltpu.HBM)
    outs = pl.pallas_call(
        body, name="exchange_start_" + name,
        out_shape=(pltpu.SemaphoreType.DMA((3 * n,)), pltpu.SemaphoreType.DMA((3 * n,)),
                   *[pltpu.HBM(a.shape, a.dtype) for a in chip_sums + lands], jax.ShapeDtypeStruct((8, LANE), F32)),
        in_specs=(HBM,) * (2 * n), out_specs=(SEM, SEM) + (HBM,) * (2 * n) + (pl.BlockSpec(memory_space=pltpu.VMEM),),
        input_output_aliases={i: 2 + i for i in range(2 * n)},
        compiler_params=pltpu.CompilerParams(has_side_effects=pltpu.SideEffectType.DATAFLOW_SIDE_EFFECTING),
    )(*[hbm(s) for s in chip_sums], *[hbm(lax.empty(a.shape, a.dtype)) for a in lands])
    return outs[0], outs[1], list(outs[2:2 + n]), list(outs[2 + n:2 + 2 * n]), outs[-1]


def _exchange_wait(name, flight, after):
    send_sems, recv_sems, s_thru, land_thru, _ = flight
    n = len(s_thru)

    def body(*refs):
        for cp in _exchange_copies(refs[:n], refs[n:2 * n], refs[2 * n], refs[2 * n + 1]):
            cp.wait_send()
            cp.wait_recv()

    outs = pl.pallas_call(
        body, name="exchange_wait_" + name,
        out_shape=tuple(pltpu.HBM(a.shape, a.dtype) for a in s_thru + land_thru),
        in_specs=(HBM,) * (2 * n) + (SEM, SEM, ANY), out_specs=(HBM,) * (2 * n),
        input_output_aliases={i: i for i in range(2 * n)},
        compiler_params=pltpu.CompilerParams(has_side_effects=pltpu.SideEffectType.DATAFLOW_SIDE_EFFECTING),
    )(*s_thru, *land_thru, send_sems, recv_sems, after)
    return list(outs[:n]), list(outs[n:])


def _sib_share(halves):
    def build(reads, aliased, fresh, send_sems, recv_sems, off=0):
        x, y, c, _, _, _ = _place()
        return [_remote(reads[i], fresh[i], send_sems.at[off + i], recv_sems.at[off + i], (x, y, 1 - c))
                for i in range(len(halves))]

    return _Side(build, len(halves), reads=halves, fresh=[jax.ShapeDtypeStruct(h.shape, F32) for h in halves])


def _join(a, b):
    def build(reads, aliased, fresh, send_sems, recv_sems, off=0):
        ra, aa, fa = len(a.reads), len(a.aliased), len(a.fresh)
        return (a.build(reads[:ra], aliased[:aa], fresh[:fa], send_sems, recv_sems, off)
                + b.build(reads[ra:], aliased[aa:], fresh[fa:], send_sems, recv_sems, off + a.nsem))

    return _Side(build, a.nsem + b.nsem, a.reads + b.reads, a.aliased + b.aliased, a.fresh + b.fresh)


def _run_side(name, side):
    nr, na = len(side.reads), len(side.aliased)

    def body(*refs):
        n_in, n_out = nr + na, na + len(side.fresh)
        outs = refs[n_in:n_in + n_out]
        copies = side.build(refs[:nr], outs[:na], outs[na:], *refs[-2:])
        for cp in copies:
            cp.start()
        for cp in copies:
            cp.wait()

    return pl.pallas_call(
        body, name=name, in_specs=side.in_specs(), out_specs=side.out_specs(), out_shape=side.out_shape(),
        input_output_aliases=side.aliases(0, 0), scratch_shapes=side.scratch(),
    )(*side.operands())


def _small_allreduce_adam(gpart, w, m, v, after):
    R = gpart.shape[0]

    def body(g_ref, w_ref, m_ref, v_ref, after_ref, go_ref, d_ref, mo_ref, vo_ref, buf, send_sems, recv_sems):
        x, y, c = lax.axis_index("x"), lax.axis_index("y"), lax.axis_index("c")
        me = 4 * x + 2 * y + c
        buf[me] = g_ref[...]
        copies = []
        for k in range(1, 8):
            fx, fy, fc = (k >> 2) & 1, (k >> 1) & 1, k & 1
            peer = (1 - x if fx else x, 1 - y if fy else y, 1 - c if fc else c)
            cp = _remote(g_ref, buf.at[me], send_sems.at[k - 1], recv_sems.at[k - 1], peer)
            cp.start()
            copies.append((cp, 4 * peer[0] + 2 * peer[1] + peer[2]))
        for k, (cp, pid) in enumerate(copies):
            _remote(g_ref, buf.at[pid], send_sems.at[k], recv_sems.at[k], (x, y, c)).wait_recv()
        for cp, _ in copies:
            cp.wait_send()
        g = buf[0]
        for d in range(1, 8):
            g = g + buf[d]
        delta, mn, vn = _adam_math(w_ref[...], g, m_ref[...], v_ref[...])
        go_ref[...] = g
        d_ref[...] = delta
        mo_ref[...] = mn
        vo_ref[...] = vn

    vm = pl.BlockSpec(memory_space=pltpu.VMEM)
    return pl.pallas_call(
        body, name="small_allreduce_adam",
        in_specs=[vm] * 4 + [ANY], out_specs=[vm] * 4,
        out_shape=[jax.ShapeDtypeStruct((R, LANE), F32)] * 4,
        scratch_shapes=[pltpu.VMEM((8, R, LANE), F32), pltpu.SemaphoreType.DMA((7,)), pltpu.SemaphoreType.DMA((7,))],
    )(gpart, w, m, v, after)


def _pack(arrs):
    flat = jnp.concatenate([a.reshape(-1).astype(F32) for a in arrs])
    rows = -(-flat.shape[0] // (8 * LANE)) * 8
    return jnp.pad(flat, (0, rows * LANE - flat.shape[0])).reshape(rows, LANE)


def _unpack(packed, like):
    flat, out, off = packed.reshape(-1), [], 0
    for a in like:
        out.append(flat[off:off + a.size].reshape(a.shape))
        off += a.size
    return out


def kernel(x, w_in, lb_logits, hg_norm_w, rel_bias, w_branch_a, w_branch_b, w_out, norm_mix_w, norm_mlp_w, w_up, w_down, norm_final_w, loss_target, m_w_in, m_lb_logits, m_hg_norm_w, m_rel_bias, m_w_branch_a, m_w_branch_b, m_w_out, m_norm_mix_w, m_norm_mlp_w, m_w_up, m_w_down, m_norm_final_w, v_w_in, v_lb_logits, v_hg_norm_w, v_rel_bias, v_w_branch_a, v_w_branch_b, v_w_out, v_norm_mix_w, v_norm_mlp_w, v_w_up, v_w_down, v_norm_final_w):
    T, D = x.shape[1], x.shape[2]
    x2, tgt = x.reshape(T, D), loss_target.reshape(T, D)
    big = dict(w_in=(w_in, m_w_in, v_w_in), w_branch_a=(w_branch_a, m_w_branch_a, v_w_branch_a),
               w_branch_b=(w_branch_b, m_w_branch_b, v_w_branch_b), w_out=(w_out, m_w_out, v_w_out),
               w_up=(w_up, m_w_up, v_w_up), w_down=(w_down, m_w_down, v_w_down))
    big = {k: tuple(a[0] for a in v) for k, v in big.items()}
    nfw = norm_final_w.reshape(1, D)

    place = jnp.stack([2 * lax.axis_index("x") + lax.axis_index("y"), lax.axis_index("c")]).astype(jnp.int32)
    small3 = ["w_branch_a", "w_branch_b", "w_out"]

    def span(name, lo, hi):
        pr = big[name][0].shape[0] // 2
        return (pr * lo // 16, pr * (hi - lo) // 16)

    Wf = {"w_in": _cast_into_full("w_in", big["w_in"][0], place)}
    for name, lo, hi in (("w_up", 0, 2), ("w_down", 2, 4)):
        Wf[name], Wf["w_in"] = _cast_into_full(
            name, big[name][0], place, side=_ici_near(["w_in"], [Wf["w_in"]], rows=[span("w_in", lo, hi)]))
    for name in small3:
        Wf[name] = _cast_into_full(name, big[name][0], place)

    ab = ["w_branch_a", "w_branch_b"]
    u1 = _rms_fwd("norm_mix", x2, norm_mix_w)
    z, Wf["w_in"] = _z_part(u1, Wf["w_in"], None, place, 0, 1,
                            side=_ici_near(["w_in"], [Wf["w_in"]], rows=[span("w_in", 4, 16)]))
    def carried(**moves):
        def arg(n, k, a):
            if k == "pass":
                return a[0] if a else (0, 1, 2)
            return span(n, *a) if a else None

        return _gather_moves([(n, Wf[n], [(k, arg(n, k, a)) for k, *a in ms]) for n, ms in moves.items()]), list(moves)

    def land(names, outs):
        Wf.update(zip(names, outs[-len(names):]))
        return outs[:-len(names)]

    side, names = carried(w_in=[("pass", (0, 1))], w_out=[("near", 0, 8)])
    land(names, _run_side("pass_w_in_near", side))
    side, names = carried(w_in=[("far",)])
    (z,) = land(names, _z_part(u1, Wf["w_in"], z, place, 1, 2, side=side))
    side, names = carried(w_in=[("pass", (2,))], w_branch_a=[("near",)], w_branch_b=[("near",)])
    land(names, _run_side("pass_w_in_far", side))
    side, names = carried(w_branch_a=[("far",)], w_branch_b=[("far",)], w_out=[("near", 8, 16)])
    (z,) = land(names, _z_part(u1, Wf["w_in"], z, place, 3, 1, side=side))
    side, names = carried(w_branch_a=[("pass",)], w_branch_b=[("pass",)], w_up=[("near", 0, 10)])
    ya, o_hg, states = land(names, _hg_fwd(z, lb_logits, hg_norm_w, side=side))
    bias_win = _bias_window(rel_bias[0])
    side, names = carried(w_out=[("far",)], w_up=[("near", 10, 16), ("far", 0, 10)], w_down=[("near", 0, 3)])
    (yb,) = land(names, _at_fwd(z, bias_win, side=side))
    side, names = carried(w_out=[("pass",)], w_up=[("far", 10, 14)])
    (pa,) = land(names, _mm("branch_a", ya, Wf["w_branch_a"], "nn", [BF16], side=side))
    side, names = carried(w_up=[("far", 14, 16)], w_down=[("near", 3, 4)])
    (pb,) = land(names, _mm("branch_b", yb, Wf["w_branch_b"], "nn", [BF16], side=side))
    side, names = carried(w_down=[("near", 4, 8)])
    (merged,) = land(names, _merge(z, pa, pb, side=side))
    add = lambda acc, res: (acc + res,)
    side, names = carried(w_up=[("pass",)], w_down=[("near", 8, 12)])
    (h1,) = land(names, _mm("out_proj", merged, Wf["w_out"], "nn", [F32], extras=[x2], epilogue=add, side=side))
    side, names = carried(w_down=[("near", 12, 14)])
    (u2,) = land(names, _rms_fwd("norm_mlp", h1, norm_mlp_w, side=side))
    relu2 = lambda acc: (acc, jnp.square(jnp.maximum(acc, 0.0)))
    side, names = carried(w_down=[("near", 14, 16), ("far", 0, 14)])
    a_pre, act = land(names, _mm("mlp_up", u2, Wf["w_up"], "nn", [F32, BF16], epilogue=relu2, side=side))
    (Wf["w_down"],) = _run_side("far_w_down", _ici_far(["w_down"], [Wf["w_down"]], rows=[span("w_down", 14, 16)]))
    (Wf["w_down"],) = _run_side("pass_w_down", _d2d_gather(["w_down"], [Wf["w_down"]]))
    h2 = _mm("mlp_down", act, Wf["w_down"], "nn", [F32], extras=[h1], epilogue=add)
    loss_part, dh2, dh2b, d_nf = _loss_head(h2, tgt, nfw)

    drelu2 = lambda acc, a: (acc * (2.0 * jnp.maximum(a, 0.0)),)
    da = _mm("d_act", dh2b, Wf["w_down"], "nt", [BF16], extras=[a_pre], epilogue=drelu2)
    G = {}
    G["w_down"] = _mm("g_w_down", act, dh2b, "tn", [BF16])
    G["w_up"] = _mm("g_w_up", u2, da, "tn", [BF16])
    T_, S_, GOT = {}, {}, {}
    du2, T_["w_down"], T_["w_up"] = _mm("d_u2", da, Wf["w_up"], "nt", [F32],
                                        side=_sib_send(["w_down", "w_up"], [G["w_down"], G["w_up"]]))
    mlp2 = ["w_down", "w_up"]
    flight_mlp = _exchange_start("mlp", [_chip_sum(n, G[n], T_[n], place) for n in mlp2])
    dh1, dh1b, d_nmlp = _rms_bwd("norm_mlp_bwd", du2, h1, norm_mlp_w, dh2, side=_after(flight_mlp[-1]))
    dmerged = _mm("d_merged", dh1b, Wf["w_out"], "nt", [F32])
    G["w_out"] = _mm("g_w_out", merged, dh1b, "tn", [BF16])
    dpa, dpb, dz_ga, dz_gb = _dmerge(dmerged, z, pa, pb)
    dya = _mm("d_ya", dpa, Wf["w_branch_a"], "nt", [F32])
    dyb = _mm("d_yb", dpb, Wf["w_branch_b"], "nt", [F32])
    G["w_branch_a"] = _mm("g_w_a", ya, dpa, "tn", [BF16])
    G["w_branch_b"] = _mm("g_w_b", yb, dpb, "tn", [BF16])
    dz_q, dz_f, dz_i, dz_g, d_lbl, d_hgw, *sent = _hg_bwd(
        z, o_hg, dya, states, lb_logits, hg_norm_w, side=_sib_send(small3, [G[n] for n in small3]))
    flight_small = _exchange_start("small", [_chip_sum(n, G[n], t, place) for n, t in zip(small3, sent)])
    dz_aq, dz_ak, dz_av, dbias_win = _at_bwd(z, dyb, bias_win, side=_after(flight_small[-1]))
    dz = jnp.concatenate([dz_q, dz_f, dz_i, dz_g, dz_aq, dz_ak, dz_av, dz_ga, dz_gb], axis=1)
    g_send = _g_w_in_half(u1, dz, place, False)
    g_keep, T_["w_in"] = _g_w_in_half(u1, dz, place, True, side=_sib_send_half(g_send))
    for names, flight in ((mlp2, flight_mlp), (small3, flight_small)):
        sums, got = _exchange_wait("_".join(names), flight, g_keep)
        S_.update(zip(names, sums))
        GOT.update(zip(names, got))
    S_["w_in"] = _chip_sum("w_in", g_keep, T_["w_in"], place, kept_rows=True)
    early = [n for n in WEIGHTS if n != "w_in"]
    H_ = {n: _piece_sum(n, S_[n], GOT[n], place) for n in early}
    flight = _exchange_start("w_in", [S_["w_in"]])
    share_early = _sib_share([H_[n] for n in early])
    share_early.reads.append(flight[-1])
    du1, *shared = _mm("d_u1", dz, Wf["w_in"], "nt", [F32], side=share_early)
    O_ = dict(zip(early, shared))
    grad_x, _, d_nmix = _rms_bwd("norm_mix_bwd", du1, x2, norm_mix_w, dh1)
    d_rel = _bias_window_grad(dbias_win)
    big_out = {}
    for name in early:
        outs = _adam_quarter(name, *big[name], H_[name], O_[name], place)
        big_out[name] = tuple(a[None] for a in outs)
    (S_["w_in"],), (got_in,) = _exchange_wait("w_in", flight, outs[1])
    H_["w_in"] = _piece_sum("w_in", S_["w_in"], got_in, place)
    (O_["w_in"],) = _run_side("share_w_in", _sib_share([H_["w_in"]]))
    outs = _adam_quarter("w_in", *big["w_in"], H_["w_in"], O_["w_in"], place)
    big_out["w_in"] = tuple(a[None] for a in outs)

    smalls = [("lb_logits", lb_logits, m_lb_logits, v_lb_logits, d_lbl),
              ("hg_norm_w", hg_norm_w, m_hg_norm_w, v_hg_norm_w, d_hgw),
              ("rel_bias", rel_bias, m_rel_bias, v_rel_bias, d_rel),
              ("norm_mix_w", norm_mix_w, m_norm_mix_w, v_norm_mix_w, d_nmix),
              ("norm_mlp_w", norm_mlp_w, m_norm_mlp_w, v_norm_mlp_w, d_nmlp),
              ("norm_final_w", norm_final_w, m_norm_final_w, v_norm_final_w, d_nf)]
    like = [s[1] for s in smalls]
    packed = _small_allreduce_adam(_pack([s[4] for s in smalls]), _pack(like), _pack([s[2] for s in smalls]),
                                   _pack([s[3] for s in smalls]), got_in)
    small_out = {s[0]: vals for s, vals in zip(smalls, zip(*[_unpack(p, like) for p in packed]))}

    loss = lax.psum(loss_part[0, 0], ("x", "y", "c"))
    order = ["w_in", "lb_logits", "hg_norm_w", "rel_bias", "w_branch_a", "w_branch_b", "w_out", "norm_mix_w",
             "norm_mlp_w", "w_up", "w_down", "norm_final_w"]
    res = {**big_out, **small_out}
    return (loss, grad_x.reshape(x.shape), *[res[n][0] for n in order], *[res[n][1] for n in order],
            *[res[n][2] for n in order], *[res[n][3] for n in order])
```

```python
import functools

import jax
import jax.numpy as jnp
from jax import lax
from jax.experimental import pallas as pl
from jax.experimental.pallas import tpu as pltpu

F32 = jnp.float32
BF16 = jnp.bfloat16
HIGHEST = lax.Precision.HIGHEST

D_MODEL = 2048
SEQ = 2048
CHUNK = 64
HG_HEADS = 8
HG_D = 128
AT_HEADS = 16
AT_DH = 64
LEFT = 8
REL_CLIP = 256
D_FF = 8192
EPS = 1e-6
ADAM_LR = 0.001
ADAM_B1 = 0.9
ADAM_B2 = 0.999
ADAM_EPS = 1e-08
ADAM_WD = 0.01
ADAM_STEP = 10

LANE = 128
NEG = -1e30
EXP_CLAMP = 80.0
VMEM_LIMIT = 48 * 1024 * 1024
MM_TM, MM_TN, MM_TK = 1024, 1024, 2816
ROW_TILE = 256
QB = 2 * CHUNK


def _hgw():
    return HG_HEADS * HG_D


def _atw():
    return AT_HEADS * AT_DH


def _cparams(sem):
    return pltpu.CompilerParams(dimension_semantics=sem, vmem_limit_bytes=VMEM_LIMIT)


def _sigmoid(x):
    return jax.nn.sigmoid(x)


def _dot(a, b, dims, precision=None):
    return lax.dot_general(a, b, (dims, ((), ())), preferred_element_type=F32, precision=precision)


def _nn(a, b, precision=None):
    return _dot(a, b, ((1,), (0,)), precision)


def _nt(a, b, precision=None):
    return _dot(a, b, ((1,), (1,)), precision)


def _tn(a, b, precision=None):
    return _dot(a, b, ((0,), (0,)), precision)


class _Side:
    def __init__(self, build, nsem, reads=(), aliased=(), fresh=()):
        self.build, self.nsem = build, nsem
        self.reads, self.aliased, self.fresh = list(reads), list(aliased), list(fresh)

    def operands(self):
        return self.reads + self.aliased

    def in_specs(self):
        return [ANY] * len(self.operands())

    def out_specs(self):
        return [ANY] * (len(self.aliased) + len(self.fresh))

    def out_shape(self):
        return [jax.ShapeDtypeStruct(a.shape, a.dtype) for a in self.aliased] + self.fresh

    def aliases(self, n_in, n_out):
        return {n_in + len(self.reads) + t: n_out + t for t in range(len(self.aliased))}

    def scratch(self):
        return [pltpu.SemaphoreType.DMA((self.nsem,)), pltpu.SemaphoreType.DMA((self.nsem,))]

    def hooks(self, in_refs, out_refs, sems, first, last):
        nr, na = len(self.reads), len(self.aliased)
        args = (in_refs[:nr], out_refs[:na], out_refs[na:], *sems)

        @pl.when(first)
        def _():
            for cp in self.build(*args):
                cp.start()

        @pl.when(last)
        def _():
            for cp in self.build(*args):
                cp.wait()


def _after(*tokens):
    return _Side(lambda *args: [], 1, reads=tokens)


def _side_parts(side):
    if side is None:
        return [], [], [], [], lambda n_in, n_out: {}, []
    return side.operands(), side.in_specs(), side.out_specs(), side.out_shape(), side.aliases, side.scratch()


def _call_with_side(body, name, grid, in_specs, out_specs, out_shape, scratch, sem, operands, side, n_prefetch=0,
                    aliases=None, borrow=None):
    _, _, s_out, s_shape, _, s_scr = _side_parts(side)
    n_in, n_out = n_prefetch + len(in_specs), len(out_specs)
    borrow = borrow or {}
    s_ops, s_alias = [], {}
    if side is not None:
        keep = [t for t in range(len(side.aliased)) if t not in borrow]
        s_ops = side.reads + [side.aliased[t] for t in keep]
        s_alias = {n_in + len(side.reads) + pos: n_out + t for pos, t in enumerate(keep)}
        s_alias.update({n_prefetch + i: n_out + t for t, i in borrow.items()})
    s_in = [ANY] * len(s_ops)
    n_sin, n_sout = len(s_ops), len(s_out)

    def wrapped(*refs):
        a, b, c = n_in + n_sin, n_in + n_sin + n_out, n_in + n_sin + n_out + n_sout
        ids = [pl.program_id(d) for d in range(len(grid))]
        first = functools.reduce(lambda p, q: p & q, [i == 0 for i in ids])
        last = functools.reduce(lambda p, q: p & q, [i == g - 1 for i, g in zip(ids, grid)])
        side.hooks(refs[n_in:a], refs[b:c], refs[-2:], first, last)
        body(*refs[:n_in], *refs[a:b], *refs[c:-2])

    spec = dict(grid=grid, in_specs=in_specs + s_in, out_specs=out_specs + s_out, scratch_shapes=scratch + s_scr)
    if n_prefetch:
        spec = dict(grid_spec=pltpu.PrefetchScalarGridSpec(num_scalar_prefetch=n_prefetch, **spec))
    return pl.pallas_call(
        body if side is None else wrapped, name=name, out_shape=out_shape + s_shape,
        input_output_aliases={**s_alias, **{n_prefetch + i: o for i, o in (aliases or {}).items()}},
        compiler_params=_cparams(sem if side is None else ("arbitrary",) * len(grid)), **spec,
    )(*operands, *s_ops)


def _mm_tk(K):
    if K <= MM_TK:
        return K
    return max(t for t in range(LANE, MM_TK + 1, LANE) if K % t == 0)


def _mm(name, a, b, mode, out_dtypes, extras=(), epilogue=None, side=None):
    if mode == "nn":
        (M, K), (K2, N) = a.shape, b.shape
    elif mode == "nt":
        (M, K), (N, K2) = a.shape, b.shape
    else:
        (K, M), (K2, N) = a.shape, b.shape
    assert K == K2, (name, a.shape, b.shape)
    tm, tn, tk = min(MM_TM, M), min(MM_TN, N), _mm_tk(K)
    assert M % tm == 0 and N % tn == 0 and K % tk == 0, (name, M, N, K)
    ni, nj, nk = M // tm, N // tn, K // tk
    ne, no = len(extras), len(out_dtypes)
    if epilogue is None:
        epilogue = lambda acc: (acc,)
    s_ops, s_in, s_out, s_shape, s_alias, s_scr = _side_parts(side)
    n_in, n_sin, n_sout = 2 + ne, len(s_ops), len(s_out)

    def body(*refs):
        a_ref, b_ref = refs[:2]
        extra_refs = refs[2:n_in]
        out_refs = refs[n_in + n_sin:n_in + n_sin + no]
        rest = refs[n_in + n_sin + no + n_sout:]
        i, j, k = pl.program_id(0), pl.program_id(1), pl.program_id(2)
        if side is not None:
            side.hooks(refs[n_in:n_in + n_sin], refs[n_in + n_sin + no:n_in + n_sin + no + n_sout], rest[-2:],
                       (i == 0) & (j == 0) & (k == 0), (i == ni - 1) & (j == nj - 1) & (k == nk - 1))
        av, bv = a_ref[...].astype(BF16), b_ref[...].astype(BF16)
        prod = _nn(av, bv) if mode == "nn" else _nt(av, bv) if mode == "nt" else _tn(av, bv)

        def finish(acc):
            res = epilogue(acc, *[e[...] for e in extra_refs])
            for o_ref, r in zip(out_refs, res):
                o_ref[...] = r.astype(o_ref.dtype)

        if nk == 1:
            finish(prod)
        else:
            acc_ref = rest[0]

            @pl.when(k == 0)
            def _():
                acc_ref[...] = prod

            @pl.when((k > 0) & (k < nk - 1))
            def _():
                acc_ref[...] += prod

            @pl.when(k == nk - 1)
            def _():
                finish(acc_ref[...] + prod)

    if mode == "nn":
        a_spec = pl.BlockSpec((tm, tk), lambda i, j, k: (i, k))
        b_spec = pl.BlockSpec((tk, tn), lambda i, j, k: (k, j))
    elif mode == "nt":
        a_spec = pl.BlockSpec((tm, tk), lambda i, j, k: (i, k))
        b_spec = pl.BlockSpec((tn, tk), lambda i, j, k: (j, k))
    else:
        a_spec = pl.BlockSpec((tk, tm), lambda i, j, k: (k, i))
        b_spec = pl.BlockSpec((tk, tn), lambda i, j, k: (k, j))
    o_spec = pl.BlockSpec((tm, tn), lambda i, j, k: (i, j))
    sem = ("arbitrary",) * 3 if side is not None else ("parallel", "parallel", "arbitrary")
    outs = pl.pallas_call(
        body, name=name,
        grid=(ni, nj, nk),
        in_specs=[a_spec, b_spec] + [o_spec] * ne + s_in,
        out_specs=[o_spec] * no + s_out,
        out_shape=[jax.ShapeDtypeStruct((M, N), dt) for dt in out_dtypes] + s_shape,
        input_output_aliases=s_alias(n_in, no),
        scratch_shapes=([pltpu.VMEM((tm, tn), F32)] if nk > 1 else []) + s_scr,
        compiler_params=_cparams(sem),
    )(a, b, *extras, *s_ops)
    return outs[0] if len(outs) == 1 else outs


def _row_spec(tr, d):
    return pl.BlockSpec((tr, d), lambda i: (i, 0))


def _vec_spec(d):
    return pl.BlockSpec((1, d), lambda i: (0, 0))


def _rms_fwd(name, x, w, side=None):
    T, D = x.shape
    tr = min(ROW_TILE, T)

    def body(x_ref, w_ref, o_ref):
        xf = x_ref[...]
        r = lax.rsqrt(jnp.mean(xf * xf, axis=-1, keepdims=True) + EPS)
        o_ref[...] = (xf * r * w_ref[...]).astype(BF16)

    outs = _call_with_side(body, name, (T // tr,), [_row_spec(tr, D), _vec_spec(D)], [_row_spec(tr, D)],
                           [jax.ShapeDtypeStruct((T, D), BF16)], [], ("parallel",), (x, w), side)
    return outs[0] if side is None else outs


def _rms_bwd(name, dy, h, w, dres, side=None):
    T, D = h.shape
    tr = min(ROW_TILE, T)

    def body(dy_ref, h_ref, w_ref, dres_ref, dh_ref, dhb_ref, dw_ref):
        @pl.when(pl.program_id(0) == 0)
        def _():
            dw_ref[...] = jnp.zeros_like(dw_ref)

        hf, dyv = h_ref[...], dy_ref[...]
        r = lax.rsqrt(jnp.mean(hf * hf, axis=-1, keepdims=True) + EPS)
        xhat = hf * r
        dw_ref[...] += jnp.sum(dyv * xhat, axis=0, keepdims=True)
        dxh = dyv * w_ref[...]
        dh = dres_ref[...] + r * (dxh - xhat * jnp.mean(dxh * xhat, axis=-1, keepdims=True))
        dh_ref[...] = dh
        dhb_ref[...] = dh.astype(BF16)

    return _call_with_side(
        body, name, (T // tr,),
        [_row_spec(tr, D), _row_spec(tr, D), _vec_spec(D), _row_spec(tr, D)],
        [_row_spec(tr, D), _row_spec(tr, D), _vec_spec(D)],
        [jax.ShapeDtypeStruct((T, D), F32), jax.ShapeDtypeStruct((T, D), BF16), jax.ShapeDtypeStruct((1, D), F32)],
        [], ("arbitrary",), (dy, h, w, dres), side)


def _loss_head(h2, target, w):
    T, D = h2.shape
    tr = min(ROW_TILE, T)

    def body(h_ref, t_ref, w_ref, loss_ref, dh_ref, dhb_ref, dw_ref):
        @pl.when(pl.program_id(0) == 0)
        def _():
            dw_ref[...] = jnp.zeros_like(dw_ref)
            loss_ref[...] = jnp.zeros_like(loss_ref)

        hf, wv = h_ref[...], w_ref[...]
        r = lax.rsqrt(jnp.mean(hf * hf, axis=-1, keepdims=True) + EPS)
        xhat = hf * r
        diff = xhat * wv - t_ref[...]
        loss_ref[...] += 0.5 * jnp.sum(jnp.mean(diff * diff, axis=-1, keepdims=True))
        dyv = diff * (1.0 / D)
        dw_ref[...] += jnp.sum(dyv * xhat, axis=0, keepdims=True)
        dxh = dyv * wv
        dh = r * (dxh - xhat * jnp.mean(dxh * xhat, axis=-1, keepdims=True))
        dh_ref[...] = dh
        dhb_ref[...] = dh.astype(BF16)

    return pl.pallas_call(
        body, name="loss_head", grid=(T // tr,),
        in_specs=[_row_spec(tr, D), _row_spec(tr, D), _vec_spec(D)],
        out_specs=[_vec_spec(LANE), _row_spec(tr, D), _row_spec(tr, D), _vec_spec(D)],
        out_shape=[jax.ShapeDtypeStruct((1, LANE), F32), jax.ShapeDtypeStruct((T, D), F32),
                   jax.ShapeDtypeStruct((T, D), BF16), jax.ShapeDtypeStruct((1, D), F32)],
        compiler_params=_cparams(("arbitrary",)),
    )(h2, target, w)


def _gate_tiles(T, D):
    goff = 4 * _hgw() + 3 * _atw()
    tc = min(1024, D)
    assert goff % tc == 0 and D % tc == 0
    return min(ROW_TILE, T), tc, goff // tc, D // tc


def _merge(z, pa, pb, side=None):
    T, D = pa.shape
    tr, tc, g0, nd = _gate_tiles(T, D)

    def body(ga_ref, gb_ref, pa_ref, pb_ref, o_ref):
        o_ref[...] = (_sigmoid(ga_ref[...]) * pa_ref[...] + _sigmoid(gb_ref[...]) * pb_ref[...]).astype(BF16)

    t = pl.BlockSpec((tr, tc), lambda i, j: (i, j))
    outs = _call_with_side(
        body, "merge", (T // tr, nd),
        [pl.BlockSpec((tr, tc), lambda i, j: (i, g0 + j)), pl.BlockSpec((tr, tc), lambda i, j: (i, g0 + nd + j)), t, t],
        [t], [jax.ShapeDtypeStruct((T, D), BF16)], [], ("parallel", "parallel"), (z, z, pa, pb), side)
    return outs[0] if side is None else outs


def _dmerge(dm, z, pa, pb):
    T, D = pa.shape
    tr, tc, g0, nd = _gate_tiles(T, D)

    def body(dm_ref, ga_ref, gb_ref, pa_ref, pb_ref, dpa_ref, dpb_ref, dga_ref, dgb_ref):
        dmv = dm_ref[...]
        sa, sb = _sigmoid(ga_ref[...]), _sigmoid(gb_ref[...])
        dpa_ref[...] = (dmv * sa).astype(BF16)
        dpb_ref[...] = (dmv * sb).astype(BF16)
        dga_ref[...] = (dmv * pa_ref[...] * sa * (1.0 - sa)).astype(BF16)
        dgb_ref[...] = (dmv * pb_ref[...] * sb * (1.0 - sb)).astype(BF16)

    t = pl.BlockSpec((tr, tc), lambda i, j: (i, j))
    return pl.pallas_call(
        body, name="dmerge", grid=(T // tr, nd),
        in_specs=[t, pl.BlockSpec((tr, tc), lambda i, j: (i, g0 + j)),
                  pl.BlockSpec((tr, tc), lambda i, j: (i, g0 + nd + j)), t, t],
        out_specs=[t, t, t, t],
        out_shape=[jax.ShapeDtypeStruct((T, D), BF16)] * 4,
        compiler_params=_cparams(("parallel", "parallel")),
    )(dm, z, z, pa, pb)


def _hg_gates(xq, xf, lb):
    f = _sigmoid(xf)
    g = lb + (1.0 - lb) * f
    sq = _sigmoid(xq)
    return f, g, jnp.log(g), 1.0 - g, sq, xq * sq * (HG_D ** -0.5)


def _split2(x):
    hi = x.astype(BF16)
    return hi, (x - hi.astype(F32)).astype(BF16)


def _tri_sum(tri, x):
    hi, rest = x.astype(BF16), x - x.astype(BF16).astype(F32)
    mid, lo = _split2(rest)
    return _nn(tri, lo) + _nn(tri, mid) + _nn(tri, hi)


def _hg_decays(lg, tri_incl, rowi):
    b = _tri_sum(tri_incl, lg)
    b_last = jnp.sum(lg, axis=0, keepdims=True)
    b_mid = jnp.sum(jnp.where(rowi <= CHUNK // 2, lg, 0.0), axis=0, keepdims=True)
    return b, b_last, b_mid


HG_GROUP = 2


def _hg_in_specs(T):
    ng = HG_HEADS // HG_GROUP
    return [pl.BlockSpec((T, HG_GROUP * HG_D), lambda h, s=s: (0, s * ng + h)) for s in range(4)]


def _hg_fwd(z, lb_logits, hgw, side=None):
    T = z.shape[0]
    H, d, C, G = HG_HEADS, HG_D, CHUNK, HG_GROUP
    nc = T // C

    def body(hq_ref, hf_ref, hi_ref, hg_ref, lbl_ref, w_ref, ya_ref, o_ref, s_ref):
        lb_all = 1.0 / (1.0 + jnp.exp(lbl_ref[1:2, :] - lbl_ref[0:1, :]))
        wv = w_ref[...]
        row = lax.broadcasted_iota(jnp.int32, (C, C), 0)
        col = lax.broadcasted_iota(jnp.int32, (C, C), 1)
        tril = col <= row
        tri_incl = tril.astype(BF16)
        rowi = lax.broadcasted_iota(jnp.int32, (C, G * d), 0)
        lanes = [slice(hh * d, (hh + 1) * d) for hh in range(G)]
        per_head = lambda fn: jnp.concatenate([fn(hh, sl) for hh, sl in enumerate(lanes)], axis=1)
        wv_all = jnp.tile(wv, (1, G))

        def chunk(c, states):
            rows = pl.ds(pl.multiple_of(c * C, C), C)
            xq, xf, v, xg = hq_ref[rows, :], hf_ref[rows, :], hi_ref[rows, :], hg_ref[rows, :]
            _, _, lg, kk, _, q = _hg_gates(xq, xf, lb_all)
            b, b_last, b_mid = _hg_decays(lg, tri_incl, rowi)
            vb, qe = v.astype(BF16), (q * jnp.exp(b)).astype(BF16)
            qt = (q * jnp.exp(b - b_mid)).astype(BF16)
            kt = (kk * jnp.exp(jnp.minimum(b_mid - b, EXP_CLAMP))).astype(BF16)
            kd, e_last = (kk * jnp.exp(b_last - b)).astype(BF16), jnp.exp(b_last)
            for hh, st in enumerate(states):
                s_ref[hh, c] = st
            o = per_head(lambda hh, sl: _nt(qe[:, sl], states[hh].astype(BF16)))
            a = [jnp.where(tril, _nt(qt[:, sl], kt[:, sl]), 0.0).astype(BF16) for sl in lanes]
            o = o + per_head(lambda hh, sl: _nn(a[hh], vb[:, sl]))
            o_ref[rows, :] = o
            r = per_head(lambda hh, sl: jnp.broadcast_to(
                lax.rsqrt(jnp.mean(o[:, sl] * o[:, sl], axis=-1, keepdims=True) + EPS), (C, d)))
            ya_ref[rows, :] = (o * r * wv_all * (xg * _sigmoid(xg))).astype(BF16)
            return tuple(st * e_last[:, sl] + _tn(vb[:, sl], kd[:, sl]) for st, sl in zip(states, lanes))

        lax.fori_loop(0, nc, chunk, tuple(jnp.zeros((d, d), F32) for _ in range(G)))

    heads = pl.BlockSpec((T, G * d), lambda h: (0, h))
    return _call_with_side(
        body, "hg_fwd", (H // G,),
        _hg_in_specs(T) + [pl.BlockSpec((2, G * d), lambda h: (0, h)), pl.BlockSpec((1, d), lambda h: (0, 0))],
        [heads, heads, pl.BlockSpec((G, nc, d, d), lambda h: (h, 0, 0, 0))],
        [jax.ShapeDtypeStruct((T, H * d), BF16), jax.ShapeDtypeStruct((T, H * d), F32),
         jax.ShapeDtypeStruct((H, nc, d, d), F32)],
        [], ("parallel",), (z, z, z, z, lb_logits, hgw), side)


def _hg_bwd(z, o, dya, states, lb_logits, hgw, side=None):
    T = z.shape[0]
    H, d, C, G = HG_HEADS, HG_D, CHUNK, HG_GROUP
    nc = T // C
    scale = HG_D ** -0.5

    def body(hq_ref, hf_ref, hi_ref, hg_ref, o_ref, dy_ref, s_ref, lbl_ref, w_ref,
             dq_ref, df_ref, di_ref, dg_ref, dlbl_ref, dw_ref, acc_ref):
        lb_all = 1.0 / (1.0 + jnp.exp(lbl_ref[1:2, :] - lbl_ref[0:1, :]))
        wv = w_ref[...]
        row = lax.broadcasted_iota(jnp.int32, (C, C), 0)
        col = lax.broadcasted_iota(jnp.int32, (C, C), 1)
        tril = col <= row
        tri_incl = tril.astype(BF16)
        triu_incl = (col >= row).astype(BF16)
        rowi = lax.broadcasted_iota(jnp.int32, (C, G * d), 0)
        lanes = [slice(hh * d, (hh + 1) * d) for hh in range(G)]
        per_head = lambda fn: jnp.concatenate([fn(hh, sl) for hh, sl in enumerate(lanes)], axis=1)
        head_mean = lambda x: per_head(
            lambda hh, sl: jnp.broadcast_to(jnp.mean(x[:, sl], axis=-1, keepdims=True), (C, d)))
        wv_all = jnp.tile(wv, (1, G))
        lb = lb_all
        acc_ref[...] = jnp.zeros_like(acc_ref)

        @pl.when(pl.program_id(0) == 0)
        def _():
            dw_ref[...] = jnp.zeros_like(dw_ref)

        def chunk(i, carry):
            dsts, tail = carry
            c = nc - 1 - i
            rows = pl.ds(pl.multiple_of(c * C, C), C)
            xq, xf, v, xg = hq_ref[rows, :], hf_ref[rows, :], hi_ref[rows, :], hg_ref[rows, :]
            f, g, lg, kk, sq, q = _hg_gates(xq, xf, lb)
            b, b_last, b_mid = _hg_decays(lg, tri_incl, rowi)
            e_b, e_qm, e_km = jnp.exp(b), jnp.exp(b - b_mid), jnp.exp(jnp.minimum(b_mid - b, EXP_CLAMP))
            e_kl, e_last = jnp.exp(b_last - b), jnp.exp(b_last)
            ov, dy = o_ref[rows, :], dy_ref[rows, :]
            r = lax.rsqrt(head_mean(ov * ov) + EPS)
            xhat = ov * r
            sg = _sigmoid(xg)
            dxg = dy * xhat * wv_all * (sg * (1.0 + xg * (1.0 - sg)))
            dyn = dy * (xg * sg)
            acc_ref[0:1, :] += jnp.sum(dyn * xhat, axis=0, keepdims=True)
            dxh = dyn * wv_all
            dof = r * (dxh - xhat * head_mean(dxh * xhat))
            do, vb = dof.astype(BF16), v.astype(BF16)
            qe, kd, qt, kt = (q * e_b).astype(BF16), (kk * e_kl).astype(BF16), (q * e_qm).astype(BF16), (kk * e_km).astype(BF16)
            pm = [jnp.where(tril, _nt(do[:, sl], vb[:, sl]), 0.0).astype(BF16) for sl in lanes]
            am = [jnp.where(tril, _nt(qt[:, sl], kt[:, sl]), 0.0).astype(BF16) for sl in lanes]
            st = [_split2(s_ref[hh, c]) for hh in range(G)]
            ds = [_split2(x) for x in dsts]
            dq_state = per_head(lambda hh, sl: _nn(do[:, sl], st[hh][1]) + _nn(do[:, sl], st[hh][0]))
            dk_state = per_head(lambda hh, sl: _nn(vb[:, sl], ds[hh][1]) + _nn(vb[:, sl], ds[hh][0]))
            dq_intra = per_head(lambda hh, sl: _nn(pm[hh], kt[:, sl]))
            dk_intra = per_head(lambda hh, sl: _tn(pm[hh], qt[:, sl]))
            dv = per_head(lambda hh, sl: _tn(am[hh], do[:, sl]) + _nt(kd[:, sl], ds[hh][0]))
            new_dsts = tuple(x * e_last[:, sl] + _tn(do[:, sl], qe[:, sl]) for x, sl in zip(dsts, lanes))
            dq = dq_state * e_b + dq_intra * e_qm
            dk = dk_intra * e_km + dk_state * e_kl
            db = (qe.astype(F32) * dq_state + qt.astype(F32) * dq_intra
                  - kt.astype(F32) * dk_intra - kd.astype(F32) * dk_state)
            dlg = _tri_sum(triu_incl, db) + tail
            dgate = dlg / g - dk
            acc_ref[1:2, :] += jnp.sum(dgate * (1.0 - f), axis=0, keepdims=True)
            dq_ref[rows, :] = (dq * scale * (sq * (1.0 + xq * (1.0 - sq)))).astype(BF16)
            df_ref[rows, :] = (dgate * (1.0 - lb) * f * (1.0 - f)).astype(BF16)
            di_ref[rows, :] = dv.astype(BF16)
            dg_ref[rows, :] = dxg.astype(BF16)
            return new_dsts, tail + jnp.sum(db, axis=0, keepdims=True)

        lax.fori_loop(0, nc, chunk, (tuple(jnp.zeros((d, d), F32) for _ in range(G)), jnp.zeros((1, G * d), F32)))
        dw_ref[...] += functools.reduce(lambda p, q: p + q, [acc_ref[0:1, sl] for sl in lanes])
        dl0 = acc_ref[1:2, :] * lb_all * (1.0 - lb_all)
        dlbl_ref[0:1, :] = dl0
        dlbl_ref[1:2, :] = -dl0

    heads = pl.BlockSpec((T, G * d), lambda h: (0, h))
    logits = pl.BlockSpec((2, G * d), lambda h: (0, h))
    return _call_with_side(
        body, "hg_bwd", (H // G,),
        _hg_in_specs(T) + [heads, heads, pl.BlockSpec((G, nc, d, d), lambda h: (h, 0, 0, 0)), logits,
                           pl.BlockSpec((1, d), lambda h: (0, 0))],
        [heads, heads, heads, heads, logits, pl.BlockSpec((1, d), lambda h: (0, 0))],
        [jax.ShapeDtypeStruct((T, H * d), BF16)] * 4 + [jax.ShapeDtypeStruct((2, H * d), F32),
                                                        jax.ShapeDtypeStruct((1, d), F32)],
        [pltpu.VMEM((8, G * d), F32)], ("arbitrary",), (z, z, z, z, o, dya, states, lb_logits, hgw), side)


def _at_dims():
    pad = LEFT * CHUNK
    return pad, QB + pad, AT_HEADS * AT_DH // LANE, 4 * _hgw() // LANE


def _rel_of_period():
    pad, W, _, _ = _at_dims()
    n = jnp.arange(QB + W)
    return jnp.clip(pad - jnp.where(n < W, n, n - (QB + W)), -REL_CLIP, REL_CLIP) + REL_CLIP


def _bias_window(rel_bias):
    pad, W, _, _ = _at_dims()
    H, P = rel_bias.shape[0], QB + W
    per = rel_bias[:, _rel_of_period()]
    win = jnp.tile(per, (1, QB))[:, :QB * (P - 1)].reshape(H, QB, P - 1)[:, :, :W]
    t = jnp.arange(QB)[:, None]
    j = jnp.arange(W)[None, :]
    ok = (j // CHUNK >= t // CHUNK) & (j // CHUNK <= t // CHUNK + LEFT)
    return jnp.where(ok[None], win, NEG)


def _bias_window_grad(dbw):
    pad, W, _, _ = _at_dims()
    H, P = dbw.shape[0], QB + W
    flat = jnp.pad(dbw, ((0, 0), (0, 0), (0, P - 1 - W))).reshape(H, QB * (P - 1))
    per = jnp.pad(flat, ((0, 0), (0, QB))).reshape(H, QB, P).sum(axis=1)
    onehot = _rel_of_period()[:, None] == jnp.arange(2 * REL_CLIP + 1)[None, :]
    return jnp.dot(per, onehot.astype(F32), precision=HIGHEST)


def _at_stack(x):
    first = lax.broadcasted_iota(jnp.int32, x.shape, 1) < AT_DH
    return jnp.concatenate([jnp.where(first, x, 0.0), jnp.where(first, 0.0, x)], axis=0).astype(BF16)


def _at_unstack(x):
    first = lax.broadcasted_iota(jnp.int32, (QB, LANE), 1) < AT_DH
    return jnp.where(first, x[:QB], x[QB:])


def _at_softmax(qs, kw, bias_ref, qi):
    pad, W, _, _ = _at_dims()
    s = _nt(qs, kw) + bias_ref[...].reshape(2 * QB, W)
    valid = lax.broadcasted_iota(jnp.int32, (2 * QB, W), 1) + qi * QB >= pad
    s = jnp.where(valid, s, NEG)
    e = jnp.exp(s - jnp.max(s, axis=-1, keepdims=True))
    return e * (1.0 / jnp.sum(e, axis=-1, keepdims=True))


def _at_fwd(z, bias_win, side=None):
    T = z.shape[0]
    pad, W, HP, c0 = _at_dims()
    nq = T // QB

    def body(q_ref, k_ref, v_ref, bias_ref, o_ref, kpad, vpad):
        qi = pl.program_id(1)

        @pl.when(qi == 0)
        def _():
            kpad[0:pad, :] = jnp.zeros((pad, LANE), BF16)
            vpad[0:pad, :] = jnp.zeros((pad, LANE), BF16)
            kpad[pad:, :] = k_ref[...].astype(BF16)
            vpad[pad:, :] = v_ref[...].astype(BF16)

        win = pl.ds(pl.multiple_of(qi * QB, QB), W)
        kw, vw = kpad[win, :], vpad[win, :]
        p = _at_softmax(_at_stack(q_ref[...] * (AT_DH ** -0.5)), kw, bias_ref, qi)
        o_ref[...] = _at_unstack(_nn(p.astype(BF16), vw)).astype(BF16)

    full = lambda s: pl.BlockSpec((T, LANE), lambda hp, qi, s=s: (0, c0 + s * HP + hp))
    return _call_with_side(
        body, "at_fwd", (HP, nq),
        [pl.BlockSpec((QB, LANE), lambda hp, qi: (qi, c0 + hp)), full(1), full(2),
         pl.BlockSpec((2, QB, W), lambda hp, qi: (hp, 0, 0))],
        [pl.BlockSpec((QB, LANE), lambda hp, qi: (qi, hp))],
        [jax.ShapeDtypeStruct((T, HP * LANE), BF16)],
        [pltpu.VMEM((T + pad, LANE), BF16)] * 2, ("parallel", "arbitrary"), (z, z, z, bias_win), side)


def _at_bwd(z, dyb, bias_win, side=None):
    T = z.shape[0]
    pad, W, HP, c0 = _at_dims()
    nq = T // QB
    scale = AT_DH ** -0.5

    def body(q_ref, k_ref, v_ref, do_ref, bias_ref, dq_ref, dk_ref, dv_ref, dbias_ref, kpad, vpad, dkpad, dvpad):
        qi = pl.program_id(1)

        @pl.when(qi == 0)
        def _():
            kpad[0:pad, :] = jnp.zeros((pad, LANE), BF16)
            vpad[0:pad, :] = jnp.zeros((pad, LANE), BF16)
            kpad[pad:, :] = k_ref[...].astype(BF16)
            vpad[pad:, :] = v_ref[...].astype(BF16)
            dkpad[...] = jnp.zeros_like(dkpad)
            dvpad[...] = jnp.zeros_like(dvpad)
            dbias_ref[...] = jnp.zeros_like(dbias_ref)

        win = pl.ds(pl.multiple_of(qi * QB, QB), W)
        kw, vw = kpad[win, :], vpad[win, :]
        qs, dos = _at_stack(q_ref[...] * scale), _at_stack(do_ref[...])
        p = _at_softmax(qs, kw, bias_ref, qi)
        dp = _nt(dos, vw)
        ds = p * (dp - jnp.sum(p * dp, axis=-1, keepdims=True))
        dbias_ref[...] += ds.reshape(2, QB, W)
        dsb = ds.astype(BF16)
        dq_ref[...] = (_at_unstack(_nn(dsb, kw)) * scale).astype(BF16)
        dkpad[win, :] += _tn(dsb, qs)
        dvpad[win, :] += _tn(p.astype(BF16), dos)

        @pl.when(qi == nq - 1)
        def _():
            dk_ref[...] = dkpad[pad:, :].astype(BF16)
            dv_ref[...] = dvpad[pad:, :].astype(BF16)

    full = lambda s: pl.BlockSpec((T, LANE), lambda hp, qi, s=s: (0, c0 + s * HP + hp))
    blk = pl.BlockSpec((QB, LANE), lambda hp, qi: (qi, hp))
    col = pl.BlockSpec((T, LANE), lambda hp, qi: (0, hp))
    bw = pl.BlockSpec((2, QB, W), lambda hp, qi: (hp, 0, 0))
    return _call_with_side(
        body, "at_bwd", (HP, nq),
        [pl.BlockSpec((QB, LANE), lambda hp, qi: (qi, c0 + hp)), full(1), full(2), blk, bw],
        [blk, col, col, bw],
        [jax.ShapeDtypeStruct((T, HP * LANE), BF16)] * 3 + [jax.ShapeDtypeStruct(bias_win.shape, F32)],
        [pltpu.VMEM((T + pad, LANE), BF16)] * 2 + [pltpu.VMEM((T + pad, LANE), F32)] * 2,
        ("parallel", "arbitrary"), (z, z, z, dyb, bias_win), side)


def _piece_tiles(name, full_shape):
    pr, pc = _piece_shape(name, full_shape)
    tr = min(ROW_TILE, pr)
    assert pr % tr == 0
    nt = pr // tr
    if name in ROW_SHARDED:
        return tr, nt, lambda q, half, i: ((2 * q + half) * nt + i, 0)
    return tr, nt, lambda q, half, i: (half * nt + i, q)


def _cast_into_full(name, wq, place, side=None):
    full = _full_shape(name, wq.shape)
    pc = wq.shape[1]
    tr, nt, at = _piece_tiles(name, full)

    def body(place_ref, w_ref, o_ref):
        o_ref[...] = w_ref[...].astype(BF16)

    outs = _call_with_side(
        body, "cast_" + name, (2, nt), [pl.BlockSpec((tr, pc), lambda h, i, s: (h * nt + i, 0))],
        [pl.BlockSpec((tr, pc), lambda h, i, s: at(s[0], h, i))], [jax.ShapeDtypeStruct(full, BF16)],
        [], ("parallel", "parallel"), (place, wq), side, n_prefetch=1)
    return outs[0] if side is None else outs


def _g_w_in_half(u1, dz, place, own, side=None):
    T, K = u1.shape
    N = dz.shape[1]
    hk, tn = K // 2, min(MM_TN, N)
    half = (lambda s: s[1]) if own else (lambda s: 1 - s[1])

    def body(place_ref, a_ref, b_ref, o_ref):
        o_ref[...] = _tn(a_ref[...], b_ref[...]).astype(BF16)

    outs = _call_with_side(
        body, "g_w_in_keep" if own else "g_w_in_send", (N // tn,),
        [pl.BlockSpec((T, hk), lambda j, s: (0, half(s))), pl.BlockSpec((T, tn), lambda j, s: (0, j))],
        [pl.BlockSpec((hk, tn), lambda j, s: (0, j))], [jax.ShapeDtypeStruct((hk, N), BF16)],
        [], ("parallel",), (place, u1, dz), side, n_prefetch=1)
    return outs[0] if side is None else outs


def _chip_sum(name, grad, theirs, place, kept_rows=False):
    pr, pc = theirs.shape[1:]
    tr, nt, at = _piece_tiles(name, (2 * grad.shape[0], grad.shape[1]) if kept_rows else grad.shape)
    if kept_rows:
        at = lambda q, half, i: (i, q)

    def body(place_ref, g_ref, t_ref, o_ref):
        o_ref[...] = (g_ref[...].astype(F32) + t_ref[...].astype(F32)).astype(BF16)

    piece = pl.BlockSpec((None, tr, pc), lambda q, i, s: (q, i, 0))
    return pl.pallas_call(
        body, name="chip_sum_" + name,
        grid_spec=pltpu.PrefetchScalarGridSpec(
            num_scalar_prefetch=1, grid=(4, nt),
            in_specs=[pl.BlockSpec((tr, pc), lambda q, i, s: at(q, s[1], i)), piece], out_specs=piece),
        out_shape=jax.ShapeDtypeStruct(theirs.shape, BF16),
        compiler_params=_cparams(("parallel", "parallel")),
    )(place, grad, theirs)


def _piece_sum(name, chip_sums, got, place):
    pr, pc = chip_sums.shape[1:]
    tr = min(ROW_TILE, pr)

    def body(place_ref, own_ref, got_ref, o_ref):
        o_ref[...] = (own_ref[...].astype(F32) + got_ref[0].astype(F32) + got_ref[1].astype(F32)
                      + got_ref[2].astype(F32))

    return pl.pallas_call(
        body, name="piece_sum_" + name,
        grid_spec=pltpu.PrefetchScalarGridSpec(
            num_scalar_prefetch=1, grid=(pr // tr,),
            in_specs=[pl.BlockSpec((None, tr, pc), lambda i, s: (s[0], i, 0)),
                      pl.BlockSpec((3, tr, pc), lambda i, s: (0, i, 0))],
            out_specs=pl.BlockSpec((tr, pc), lambda i, s: (i, 0))),
        out_shape=jax.ShapeDtypeStruct((pr, pc), F32),
        compiler_params=_cparams(("parallel",)),
    )(place, chip_sums, got)


def _adam_quarter(name, w, m, v, g_mine, g_sib, place, side=None):
    pr, pc = g_mine.shape
    tr = min(ROW_TILE // 2, pr)
    nt = pr // tr

    def body(place_ref, w_ref, m_ref, v_ref, gm_ref, gs_ref, go_ref, d_ref, mo_ref, vo_ref):
        g = jnp.where(pl.program_id(0) == place_ref[1], gm_ref[...], gs_ref[...])
        delta, mn, vn = _adam_math(w_ref[...], g, m_ref[...], v_ref[...])
        go_ref[...] = g
        d_ref[...] = delta
        mo_ref[...] = mn
        vo_ref[...] = vn

    quarter = pl.BlockSpec((tr, pc), lambda h, i, s: (h * nt + i, 0))
    mine = pl.BlockSpec((tr, pc), lambda h, i, s: (jnp.where(h == s[1], i, 0), 0))
    sib = pl.BlockSpec((tr, pc), lambda h, i, s: (jnp.where(h == s[1], 0, i), 0))
    return _call_with_side(
        body, "adam_" + name, (2, nt), [quarter, quarter, quarter, mine, sib], [quarter] * 4,
        [jax.ShapeDtypeStruct(w.shape, F32)] * 4, [], ("parallel", "parallel"),
        (place, w, m, v, g_mine, g_sib), side, n_prefetch=1)


def _adam_math(w, g, m, v):
    m = ADAM_B1 * m + (1.0 - ADAM_B1) * g
    v = ADAM_B2 * v + (1.0 - ADAM_B2) * (g * g)
    m_hat = m / (1.0 - ADAM_B1 ** ADAM_STEP)
    v_hat = v / (1.0 - ADAM_B2 ** ADAM_STEP)
    return -ADAM_LR * (m_hat / (jnp.sqrt(v_hat) + ADAM_EPS) + ADAM_WD * w), m, v


WEIGHTS = ("w_in", "w_branch_a", "w_branch_b", "w_out", "w_up", "w_down")
ROW_SHARDED = ("w_out", "w_down")
ANY = pl.BlockSpec(memory_space=pl.ANY)
MESH = pl.DeviceIdType.MESH


def _place():
    x, y, c = lax.axis_index("x"), lax.axis_index("y"), lax.axis_index("c")
    chips = [(1 - x, y), (x, 1 - y), (1 - x, 1 - y)]
    return x, y, c, 2 * x + y, chips, [2 * cx + cy for cx, cy in chips]


def _piece(full_ref, name, q, half):
    K, N = full_ref.shape
    if name in ROW_SHARDED:
        rows = K // 8
        return full_ref.at[pl.ds(q * (2 * rows) + half * rows, rows), :]
    return full_ref.at[pl.ds(half * (K // 2), K // 2), pl.ds(q * (N // 4), N // 4)]


def _piece_shape(name, full_shape):
    K, N = full_shape
    return (K // 8, N) if name in ROW_SHARDED else (K // 2, N // 4)


def _full_shape(name, quarter_shape):
    Kq, Nq = quarter_shape
    return (4 * Kq, Nq) if name in ROW_SHARDED else (Kq, 4 * Nq)


def _remote(src, dst, send_sem, recv_sem, device):
    return pltpu.make_async_remote_copy(src_ref=src, dst_ref=dst, send_sem=send_sem, recv_sem=recv_sem,
                                        device_id=device, device_id_type=MESH)


def _z_part(u1, w_in, z_prev, place, k0, count, side=None, own_quarter=False):
    T, K = u1.shape
    nq = w_in.shape[1] if own_quarter else w_in.shape[1] // 4
    N = 4 * nq
    tn = nq // 2 if (nq // 2) % LANE == 0 else nq
    tm = min(MM_TM, T)
    per = nq // tn
    col = lambda g, j, s: (s[0] ^ (k0 + g)) * per + j
    ins = [pl.BlockSpec((tm, K), lambda g, i, j, s: (i, 0)),
           pl.BlockSpec((K, tn), (lambda g, i, j, s: (0, j)) if own_quarter else (lambda g, i, j, s: (0, col(g, j, s))))]
    operands = [place, u1, w_in]
    if z_prev is not None:
        ins.append(ANY)
        operands.append(z_prev)

    def body(place_ref, a_ref, b_ref, *rest):
        rest[-1][...] = _nn(a_ref[...], b_ref[...].astype(BF16))

    return _call_with_side(
        body, "z_part_%d" % k0, (count, T // tm, per), ins,
        [pl.BlockSpec((tm, tn), lambda g, i, j, s: (i, col(g, j, s)))], [jax.ShapeDtypeStruct((T, N), F32)],
        [], ("parallel",) * 3, tuple(operands), side, n_prefetch=1, aliases={} if z_prev is None else {2: 0},
        borrow={0: 1} if side is not None and side.aliased and side.aliased[0] is w_in else None)


def _rows(ref, span):
    return ref if span is None else ref.at[pl.ds(span[0], span[1]), :]


def _gather_moves(items):
    count = {"near": lambda arg: 2, "far": lambda arg: 1, "pass": len}

    def build(reads, aliased, fresh, send_sems, recv_sems, off=0):
        x, y, c, p, chips, chip_ids = _place()
        south = c == 0
        far_src = jnp.where(south, chip_ids[0], chip_ids[1])
        far_dst = (jnp.where(south, x, 1 - x), jnp.where(south, 1 - y, y), c)
        out = []

        def add(ref, device):
            k = off + len(out)
            out.append(_remote(ref, ref, send_sems.at[k], recv_sems.at[k], device))

        for (name, _, moves), ref in zip(items, aliased):
            for kind, arg in moves:
                if kind == "near":
                    for chip in chips[:2]:
                        add(_rows(_piece(ref, name, p, c), arg), (*chip, c))
                elif kind == "far":
                    add(_rows(_piece(ref, name, far_src, c), arg), far_dst)
                else:
                    for j in arg:
                        add(_piece(ref, name, chip_ids[j], c), (x, y, 1 - c))
        return out

    nsem = sum(count[kind](arg) for _, _, moves in items for kind, arg in moves)
    return _Side(build, nsem, aliased=[a for _, a, _ in items])


def _ici_near(names, fulls, rows=None):
    return _gather_moves([(n, a, [("near", r)]) for n, a, r in zip(names, fulls, rows or [None] * len(names))])


def _ici_far(names, fulls, rows=None):
    return _gather_moves([(n, a, [("far", r)]) for n, a, r in zip(names, fulls, rows or [None] * len(names))])


def _d2d_gather(names, fulls, which=(0, 1, 2)):
    return _gather_moves([(n, a, [("pass", which)]) for n, a in zip(names, fulls)])


def _sib_send(names, grads):
    def build(reads, aliased, fresh, send_sems, recv_sems, off=0):
        x, y, c, _, _, _ = _place()
        out = []
        for i, name in enumerate(names):
            for q in range(4):
                k = off + 4 * i + q
                out.append(_remote(_piece(reads[i], name, q, 1 - c), fresh[i].at[q], send_sems.at[k], recv_sems.at[k],
                                   (x, y, 1 - c)))
        return out

    shapes = [jax.ShapeDtypeStruct((4,) + _piece_shape(name, g.shape), BF16) for name, g in zip(names, grads)]
    return _Side(build, 4 * len(names), reads=grads, fresh=shapes)


def _sib_send_half(sent):
    K2, N = sent.shape

    def build(reads, aliased, fresh, send_sems, recv_sems, off=0):
        x, y, c, _, _, _ = _place()
        return [_remote(reads[0].at[:, pl.ds(q * (N // 4), N // 4)], fresh[0].at[q], send_sems.at[off + q],
                        recv_sems.at[off + q], (x, y, 1 - c)) for q in range(4)]

    return _Side(build, 4, reads=[sent], fresh=[jax.ShapeDtypeStruct((4, K2, N // 4), BF16)])


def _chip_exchange(chip_sums, rows=None, got=None):
    rows = rows or [None] * len(chip_sums)

    def build(reads, aliased, fresh, send_sems, recv_sems, off=0):
        _, _, c, _, chips, chip_ids = _place()
        out = []
        for i in range(len(chip_sums)):
            for j, (chip, cid) in enumerate(zip(chips, chip_ids)):
                k = off + 3 * i + j
                out.append(_remote(_rows(reads[i].at[cid], rows[i]), _rows((aliased or fresh)[i].at[j], rows[i]),
                                   send_sems.at[k], recv_sems.at[k], (*chip, c)))
        return out

    if got is not None:
        return _Side(build, 3 * len(chip_sums), reads=chip_sums, aliased=got)
    shapes = [jax.ShapeDtypeStruct((3,) + s.shape[1:], BF16) for s in chip_sums]
    return _Side(build, 3 * len(chip_sums), reads=chip_sums, fresh=shapes)


HBM = pl.BlockSpec(memory_space=pltpu.HBM)
SEM = pl.BlockSpec(memory_space=pltpu.SEMAPHORE)


def _exchange_copies(s_refs, land_refs, send_sems, recv_sems):
    _, _, c, _, chips, chip_ids = _place()
    return [_remote(s_ref.at[cid], land_ref.at[j], send_sems.at[3 * i + j], recv_sems.at[3 * i + j], (*chip, c))
            for i, (s_ref, land_ref) in enumerate(zip(s_refs, land_refs))
            for j, (chip, cid) in enumerate(zip(chips, chip_ids))]


def _exchange_start(name, chip_sums):
    n = len(chip_sums)

    def body(*refs):
        for cp in _exchange_copies(refs[:n], refs[n:2 * n], refs[2 * n], refs[2 * n + 1]):
            cp.start()
        refs[-1][...] = jnp.zeros_like(refs[-1])

    lands = [jax.ShapeDtypeStruct((3,) + s.shape[1:], s.dtype) for s in chip_sums]
    hbm = lambda a: pltpu.with_memory_space_constraint(a, pltpu.HBM)
    outs = pl.pallas_call(
        body, name="exchange_start_" + name,
        out_shape=(pltpu.SemaphoreType.DMA((3 * n,)), pltpu.SemaphoreType.DMA((3 * n,)),
                   *[pltpu.HBM(a.shape, a.dtype) for a in chip_sums + lands], jax.ShapeDtypeStruct((8, LANE), F32)),
        in_specs=(HBM,) * (2 * n), out_specs=(SEM, SEM) + (HBM,) * (2 * n) + (pl.BlockSpec(memory_space=pltpu.VMEM),),
        input_output_aliases={i: 2 + i for i in range(2 * n)},
        compiler_params=pltpu.CompilerParams(has_side_effects=pltpu.SideEffectType.DATAFLOW_SIDE_EFFECTING),
    )(*[hbm(s) for s in chip_sums], *[hbm(lax.empty(a.shape, a.dtype)) for a in lands])
    return outs[0], outs[1], list(outs[2:2 + n]), list(outs[2 + n:2 + 2 * n]), outs[-1]


def _exchange_wait(name, flight, after):
    send_sems, recv_sems, s_thru, land_thru, _ = flight
    n = len(s_thru)

    def body(*refs):
        for cp in _exchange_copies(refs[:n], refs[n:2 * n], refs[2 * n], refs[2 * n + 1]):
            cp.wait_send()
            cp.wait_recv()

    outs = pl.pallas_call(
        body, name="exchange_wait_" + name,
        out_shape=tuple(pltpu.HBM(a.shape, a.dtype) for a in s_thru + land_thru),
        in_specs=(HBM,) * (2 * n) + (SEM, SEM, ANY), out_specs=(HBM,) * (2 * n),
        input_output_aliases={i: i for i in range(2 * n)},
        compiler_params=pltpu.CompilerParams(has_side_effects=pltpu.SideEffectType.DATAFLOW_SIDE_EFFECTING),
    )(*s_thru, *land_thru, send_sems, recv_sems, after)
    return list(outs[:n]), list(outs[n:])


def _near_copies(name, f_ref, send_sems, recv_sems):
    _, _, c, p, chips, _ = _place()
    mine = _piece(f_ref, name, p, c)
    return [_remote(mine, mine, send_sems.at[j], recv_sems.at[j], (*chip, c)) for j, chip in enumerate(chips[:2])]


def _near_start(name, full):
    def body(f_ref, send_sems, recv_sems, f_thru, token):
        for cp in _near_copies(name, f_ref, send_sems, recv_sems):
            cp.start()
        token[...] = jnp.zeros_like(token)

    return pl.pallas_call(
        body, name="near_start_" + name,
        out_shape=(pltpu.SemaphoreType.DMA((2,)), pltpu.SemaphoreType.DMA((2,)), pltpu.HBM(full.shape, full.dtype),
                   jax.ShapeDtypeStruct((8, LANE), F32)),
        in_specs=(HBM,), out_specs=(SEM, SEM, HBM, pl.BlockSpec(memory_space=pltpu.VMEM)),
        input_output_aliases={0: 2},
        compiler_params=pltpu.CompilerParams(has_side_effects=pltpu.SideEffectType.DATAFLOW_SIDE_EFFECTING),
    )(pltpu.with_memory_space_constraint(full, pltpu.HBM))


def _near_wait(name, flight, after):
    send_sems, recv_sems, f_thru, _ = flight

    def body(f_ref, send_sems, recv_sems, after_ref, f_out):
        for cp in _near_copies(name, f_ref, send_sems, recv_sems):
            cp.wait_send()
            cp.wait_recv()

    return pl.pallas_call(
        body, name="near_wait_" + name, out_shape=pltpu.HBM(f_thru.shape, f_thru.dtype),
        in_specs=(HBM, SEM, SEM, ANY), out_specs=HBM, input_output_aliases={0: 0},
        compiler_params=pltpu.CompilerParams(has_side_effects=pltpu.SideEffectType.DATAFLOW_SIDE_EFFECTING),
    )(f_thru, send_sems, recv_sems, after)


def _sib_share(halves):
    def build(reads, aliased, fresh, send_sems, recv_sems, off=0):
        x, y, c, _, _, _ = _place()
        return [_remote(reads[i], fresh[i], send_sems.at[off + i], recv_sems.at[off + i], (x, y, 1 - c))
                for i in range(len(halves))]

    return _Side(build, len(halves), reads=halves, fresh=[jax.ShapeDtypeStruct(h.shape, F32) for h in halves])


def _join(a, b):
    def build(reads, aliased, fresh, send_sems, recv_sems, off=0):
        ra, aa, fa = len(a.reads), len(a.aliased), len(a.fresh)
        return (a.build(reads[:ra], aliased[:aa], fresh[:fa], send_sems, recv_sems, off)
                + b.build(reads[ra:], aliased[aa:], fresh[fa:], send_sems, recv_sems, off + a.nsem))

    return _Side(build, a.nsem + b.nsem, a.reads + b.reads, a.aliased + b.aliased, a.fresh + b.fresh)


def _run_side(name, side):
    nr, na = len(side.reads), len(side.aliased)

    def body(*refs):
        n_in, n_out = nr + na, na + len(side.fresh)
        outs = refs[n_in:n_in + n_out]
        copies = side.build(refs[:nr], outs[:na], outs[na:], *refs[-2:])
        for cp in copies:
            cp.start()
        for cp in copies:
            cp.wait()

    return pl.pallas_call(
        body, name=name, in_specs=side.in_specs(), out_specs=side.out_specs(), out_shape=side.out_shape(),
        input_output_aliases=side.aliases(0, 0), scratch_shapes=side.scratch(),
    )(*side.operands())


def _small_allreduce_adam(gpart, w, m, v, after):
    R = gpart.shape[0]

    def body(g_ref, w_ref, m_ref, v_ref, after_ref, go_ref, d_ref, mo_ref, vo_ref, buf, send_sems, recv_sems):
        x, y, c = lax.axis_index("x"), lax.axis_index("y"), lax.axis_index("c")
        me = 4 * x + 2 * y + c
        buf[me] = g_ref[...]
        copies = []
        for k in range(1, 8):
            fx, fy, fc = (k >> 2) & 1, (k >> 1) & 1, k & 1
            peer = (1 - x if fx else x, 1 - y if fy else y, 1 - c if fc else c)
            cp = _remote(g_ref, buf.at[me], send_sems.at[k - 1], recv_sems.at[k - 1], peer)
            cp.start()
            copies.append((cp, 4 * peer[0] + 2 * peer[1] + peer[2]))
        for k, (cp, pid) in enumerate(copies):
            _remote(g_ref, buf.at[pid], send_sems.at[k], recv_sems.at[k], (x, y, c)).wait_recv()
        for cp, _ in copies:
            cp.wait_send()
        g = buf[0]
        for d in range(1, 8):
            g = g + buf[d]
        delta, mn, vn = _adam_math(w_ref[...], g, m_ref[...], v_ref[...])
        go_ref[...] = g
        d_ref[...] = delta
        mo_ref[...] = mn
        vo_ref[...] = vn

    vm = pl.BlockSpec(memory_space=pltpu.VMEM)
    return pl.pallas_call(
        body, name="small_allreduce_adam",
        in_specs=[vm] * 4 + [ANY], out_specs=[vm] * 4,
        out_shape=[jax.ShapeDtypeStruct((R, LANE), F32)] * 4,
        scratch_shapes=[pltpu.VMEM((8, R, LANE), F32), pltpu.SemaphoreType.DMA((7,)), pltpu.SemaphoreType.DMA((7,))],
    )(gpart, w, m, v, after)


def _pack(arrs):
    flat = jnp.concatenate([a.reshape(-1).astype(F32) for a in arrs])
    rows = -(-flat.shape[0] // (8 * LANE)) * 8
    return jnp.pad(flat, (0, rows * LANE - flat.shape[0])).reshape(rows, LANE)


def _unpack(packed, like):
    flat, out, off = packed.reshape(-1), [], 0
    for a in like:
        out.append(flat[off:off + a.size].reshape(a.shape))
        off += a.size
    return out


def kernel(x, w_in, lb_logits, hg_norm_w, rel_bias, w_branch_a, w_branch_b, w_out, norm_mix_w, norm_mlp_w, w_up, w_down, norm_final_w, loss_target, m_w_in, m_lb_logits, m_hg_norm_w, m_rel_bias, m_w_branch_a, m_w_branch_b, m_w_out, m_norm_mix_w, m_norm_mlp_w, m_w_up, m_w_down, m_norm_final_w, v_w_in, v_lb_logits, v_hg_norm_w, v_rel_bias, v_w_branch_a, v_w_branch_b, v_w_out, v_norm_mix_w, v_norm_mlp_w, v_w_up, v_w_down, v_norm_final_w):
    T, D = x.shape[1], x.shape[2]
    x2, tgt = x.reshape(T, D), loss_target.reshape(T, D)
    big = dict(w_in=(w_in, m_w_in, v_w_in), w_branch_a=(w_branch_a, m_w_branch_a, v_w_branch_a),
               w_branch_b=(w_branch_b, m_w_branch_b, v_w_branch_b), w_out=(w_out, m_w_out, v_w_out),
               w_up=(w_up, m_w_up, v_w_up), w_down=(w_down, m_w_down, v_w_down))
    big = {k: tuple(a[0] for a in v) for k, v in big.items()}
    nfw = norm_final_w.reshape(1, D)

    place = jnp.stack([2 * lax.axis_index("x") + lax.axis_index("y"), lax.axis_index("c")]).astype(jnp.int32)
    small3 = ["w_branch_a", "w_branch_b", "w_out"]

    def span(name, lo, hi):
        pr = big[name][0].shape[0] // 2
        return (pr * lo // 16, pr * (hi - lo) // 16)

    flight_in = _near_start("w_in", _cast_into_full("w_in", big["w_in"][0], place))
    Wf = {name: _cast_into_full(name, big[name][0], place) for name in WEIGHTS if name != "w_in"}

    u1 = _rms_fwd("norm_mix", x2, norm_mix_w, side=_after(flight_in[-1]))
    z = _z_part(u1[0], big["w_in"][0], None, place, 0, 1, own_quarter=True)[0]
    u1 = u1[0]
    Wf["w_in"] = _near_wait("w_in", flight_in, z)

    def carried(**moves):
        def arg(n, k, a):
            if k == "pass":
                return a[0] if a else (0, 1, 2)
            return span(n, *a) if a else None

        return _gather_moves([(n, Wf[n], [(k, arg(n, k, a)) for k, *a in ms]) for n, ms in moves.items()]), list(moves)

    def land(names, outs):
        Wf.update(zip(names, outs[-len(names):]))
        return outs[:-len(names)]

    side, names = carried(w_in=[("pass", (0, 1))], w_out=[("near", 0, 8)])
    land(names, _run_side("pass_w_in_near", side))
    side, names = carried(w_in=[("far",)])
    (z,) = land(names, _z_part(u1, Wf["w_in"], z, place, 1, 2, side=side))
    side, names = carried(w_in=[("pass", (2,))], w_branch_a=[("near",)], w_branch_b=[("near",)])
    land(names, _run_side("pass_w_in_far", side))
    side, names = carried(w_branch_a=[("far",)], w_branch_b=[("far",)], w_out=[("near", 8, 16)])
    (z,) = land(names, _z_part(u1, Wf["w_in"], z, place, 3, 1, side=side))
    side, names = carried(w_branch_a=[("pass",)], w_branch_b=[("pass",)], w_up=[("near", 0, 10)])
    ya, o_hg, states = land(names, _hg_fwd(z, lb_logits, hg_norm_w, side=side))
    bias_win = _bias_window(rel_bias[0])
    side, names = carried(w_out=[("far",)], w_up=[("near", 10, 16), ("far", 0, 10)], w_down=[("near", 0, 3)])
    (yb,) = land(names, _at_fwd(z, bias_win, side=side))
    side, names = carried(w_out=[("pass",)], w_up=[("far", 10, 14)])
    (pa,) = land(names, _mm("branch_a", ya, Wf["w_branch_a"], "nn", [BF16], side=side))
    side, names = carried(w_up=[("far", 14, 16)], w_down=[("near", 3, 4)])
    (pb,) = land(names, _mm("branch_b", yb, Wf["w_branch_b"], "nn", [BF16], side=side))
    side, names = carried(w_down=[("near", 4, 8)])
    (merged,) = land(names, _merge(z, pa, pb, side=side))
    add = lambda acc, res: (acc + res,)
    side, names = carried(w_up=[("pass",)], w_down=[("near", 8, 12)])
    (h1,) = land(names, _mm("out_proj", merged, Wf["w_out"], "nn", [F32], extras=[x2], epilogue=add, side=side))
    side, names = carried(w_down=[("near", 12, 14)])
    (u2,) = land(names, _rms_fwd("norm_mlp", h1, norm_mlp_w, side=side))
    relu2 = lambda acc: (acc, jnp.square(jnp.maximum(acc, 0.0)))
    side, names = carried(w_down=[("near", 14, 16), ("far", 0, 14)])
    a_pre, act = land(names, _mm("mlp_up", u2, Wf["w_up"], "nn", [F32, BF16], epilogue=relu2, side=side))
    (Wf["w_down"],) = _run_side("far_w_down", _ici_far(["w_down"], [Wf["w_down"]], rows=[span("w_down", 14, 16)]))
    (Wf["w_down"],) = _run_side("pass_w_down", _d2d_gather(["w_down"], [Wf["w_down"]]))
    h2 = _mm("mlp_down", act, Wf["w_down"], "nn", [F32], extras=[h1], epilogue=add)
    loss_part, dh2, dh2b, d_nf = _loss_head(h2, tgt, nfw)

    drelu2 = lambda acc, a: (acc * (2.0 * jnp.maximum(a, 0.0)),)
    da = _mm("d_act", dh2b, Wf["w_down"], "nt", [BF16], extras=[a_pre], epilogue=drelu2)
    G = {}
    G["w_down"] = _mm("g_w_down", act, dh2b, "tn", [BF16])
    G["w_up"] = _mm("g_w_up", u2, da, "tn", [BF16])
    T_, S_, GOT = {}, {}, {}
    du2, T_["w_down"], T_["w_up"] = _mm("d_u2", da, Wf["w_up"], "nt", [F32],
                                        side=_sib_send(["w_down", "w_up"], [G["w_down"], G["w_up"]]))
    mlp2 = ["w_down", "w_up"]
    flight_mlp = _exchange_start("mlp", [_chip_sum(n, G[n], T_[n], place) for n in mlp2])
    dh1, dh1b, d_nmlp = _rms_bwd("norm_mlp_bwd", du2, h1, norm_mlp_w, dh2, side=_after(flight_mlp[-1]))
    dmerged = _mm("d_merged", dh1b, Wf["w_out"], "nt", [F32])
    G["w_out"] = _mm("g_w_out", merged, dh1b, "tn", [BF16])
    dpa, dpb, dz_ga, dz_gb = _dmerge(dmerged, z, pa, pb)
    dya = _mm("d_ya", dpa, Wf["w_branch_a"], "nt", [F32])
    dyb = _mm("d_yb", dpb, Wf["w_branch_b"], "nt", [F32])
    G["w_branch_a"] = _mm("g_w_a", ya, dpa, "tn", [BF16])
    G["w_branch_b"] = _mm("g_w_b", yb, dpb, "tn", [BF16])
    dz_q, dz_f, dz_i, dz_g, d_lbl, d_hgw, *sent = _hg_bwd(
        z, o_hg, dya, states, lb_logits, hg_norm_w, side=_sib_send(small3, [G[n] for n in small3]))
    flight_small = _exchange_start("small", [_chip_sum(n, G[n], t, place) for n, t in zip(small3, sent)])
    dz_aq, dz_ak, dz_av, dbias_win = _at_bwd(z, dyb, bias_win, side=_after(flight_small[-1]))
    dz = jnp.concatenate([dz_q, dz_f, dz_i, dz_g, dz_aq, dz_ak, dz_av, dz_ga, dz_gb], axis=1)
    g_send = _g_w_in_half(u1, dz, place, False)
    g_keep, T_["w_in"] = _g_w_in_half(u1, dz, place, True, side=_sib_send_half(g_send))
    for names, flight in ((mlp2, flight_mlp), (small3, flight_small)):
        sums, got = _exchange_wait("_".join(names), flight, g_keep)
        S_.update(zip(names, sums))
        GOT.update(zip(names, got))
    S_["w_in"] = _chip_sum("w_in", g_keep, T_["w_in"], place, kept_rows=True)
    early = [n for n in WEIGHTS if n != "w_in"]
    H_ = {n: _piece_sum(n, S_[n], GOT[n], place) for n in early}
    flight = _exchange_start("w_in", [S_["w_in"]])
    share_early = _sib_share([H_[n] for n in early])
    share_early.reads.append(flight[-1])
    du1, *shared = _mm("d_u1", dz, Wf["w_in"], "nt", [F32], side=share_early)
    O_ = dict(zip(early, shared))
    grad_x, _, d_nmix = _rms_bwd("norm_mix_bwd", du1, x2, norm_mix_w, dh1)
    d_rel = _bias_window_grad(dbias_win)
    big_out = {}
    for name in early:
        outs = _adam_quarter(name, *big[name], H_[name], O_[name], place)
        big_out[name] = tuple(a[None] for a in outs)
    (S_["w_in"],), (got_in,) = _exchange_wait("w_in", flight, outs[1])
    H_["w_in"] = _piece_sum("w_in", S_["w_in"], got_in, place)
    (O_["w_in"],) = _run_side("share_w_in", _sib_share([H_["w_in"]]))
    outs = _adam_quarter("w_in", *big["w_in"], H_["w_in"], O_["w_in"], place)
    big_out["w_in"] = tuple(a[None] for a in outs)

    smalls = [("lb_logits", lb_logits, m_lb_logits, v_lb_logits, d_lbl),
              ("hg_norm_w", hg_norm_w, m_hg_norm_w, v_hg_norm_w, d_hgw),
              ("rel_bias", rel_bias, m_rel_bias, v_rel_bias, d_rel),
              ("norm_mix_w", norm_mix_w, m_norm_mix_w, v_norm_mix_w, d_nmix),
              ("norm_mlp_w", norm_mlp_w, m_norm_mlp_w, v_norm_mlp_w, d_nmlp),
              ("norm_final_w", norm_final_w, m_norm_final_w, v_norm_final_w, d_nf)]
    like = [s[1] for s in smalls]
    packed = _small_allreduce_adam(_pack([s[4] for s in smalls]), _pack(like), _pack([s[2] for s in smalls]),
                                   _pack([s[3] for s in smalls]), got_in)
    small_out = {s[0]: vals for s, vals in zip(smalls, zip(*[_unpack(p, like) for p in packed]))}

    loss = lax.psum(loss_part[0, 0], ("x", "y", "c"))
    order = ["w_in", "lb_logits", "hg_norm_w", "rel_bias", "w_branch_a", "w_branch_b", "w_out", "norm_mix_w",
             "norm_mlp_w", "w_up", "w_down", "norm_final_w"]
    res = {**big_out, **small_out}
    return (loss, grad_x.reshape(x.shape), *[res[n][0] for n in order], *[res[n][1] for n in order],
            *[res[n][2] for n in order], *[res[n][3] for n in order])
```

```python
import functools

import jax
import jax.numpy as jnp
from jax import lax
from jax.experimental import pallas as pl
from jax.experimental.pallas import tpu as pltpu

F32 = jnp.float32
BF16 = jnp.bfloat16
HIGHEST = lax.Precision.HIGHEST

D_MODEL = 2048
SEQ = 2048
CHUNK = 64
HG_HEADS = 8
HG_D = 128
AT_HEADS = 16
AT_DH = 64
LEFT = 8
REL_CLIP = 256
D_FF = 8192
EPS = 1e-6
ADAM_LR = 0.001
ADAM_B1 = 0.9
ADAM_B2 = 0.999
ADAM_EPS = 1e-08
ADAM_WD = 0.01
ADAM_STEP = 10

LANE = 128
NEG = -1e30
EXP_CLAMP = 80.0
VMEM_LIMIT = 48 * 1024 * 1024
MM_TM, MM_TN, MM_TK = 1024, 1024, 2816
ROW_TILE = 256
QB = 2 * CHUNK


def _hgw():
    return HG_HEADS * HG_D


def _atw():
    return AT_HEADS * AT_DH


def _cparams(sem):
    return pltpu.CompilerParams(dimension_semantics=sem, vmem_limit_bytes=VMEM_LIMIT)


def _sigmoid(x):
    return jax.nn.sigmoid(x)


def _dot(a, b, dims, precision=None):
    return lax.dot_general(a, b, (dims, ((), ())), preferred_element_type=F32, precision=precision)


def _nn(a, b, precision=None):
    return _dot(a, b, ((1,), (0,)), precision)


def _nt(a, b, precision=None):
    return _dot(a, b, ((1,), (1,)), precision)


def _tn(a, b, precision=None):
    return _dot(a, b, ((0,), (0,)), precision)


class _Side:
    def __init__(self, build, nsem, reads=(), aliased=(), fresh=()):
        self.build, self.nsem = build, nsem
        self.reads, self.aliased, self.fresh = list(reads), list(aliased), list(fresh)

    def operands(self):
        return self.reads + self.aliased

    def in_specs(self):
        return [ANY] * len(self.operands())

    def out_specs(self):
        return [ANY] * (len(self.aliased) + len(self.fresh))

    def out_shape(self):
        return [jax.ShapeDtypeStruct(a.shape, a.dtype) for a in self.aliased] + self.fresh

    def aliases(self, n_in, n_out):
        return {n_in + len(self.reads) + t: n_out + t for t in range(len(self.aliased))}

    def scratch(self):
        return [pltpu.SemaphoreType.DMA((self.nsem,)), pltpu.SemaphoreType.DMA((self.nsem,))]

    def hooks(self, in_refs, out_refs, sems, first, last):
        nr, na = len(self.reads), len(self.aliased)
        args = (in_refs[:nr], out_refs[:na], out_refs[na:], *sems)

        @pl.when(first)
        def _():
            for cp in self.build(*args):
                cp.start()

        @pl.when(last)
        def _():
            for cp in self.build(*args):
                cp.wait()


def _after(*tokens):
    return _Side(lambda *args: [], 1, reads=tokens)


def _side_parts(side):
    if side is None:
        return [], [], [], [], lambda n_in, n_out: {}, []
    return side.operands(), side.in_specs(), side.out_specs(), side.out_shape(), side.aliases, side.scratch()


def _call_with_side(body, name, grid, in_specs, out_specs, out_shape, scratch, sem, operands, side, n_prefetch=0,
                    aliases=None, borrow=None):
    _, _, s_out, s_shape, _, s_scr = _side_parts(side)
    n_in, n_out = n_prefetch + len(in_specs), len(out_specs)
    borrow = borrow or {}
    s_ops, s_alias = [], {}
    if side is not None:
        keep = [t for t in range(len(side.aliased)) if t not in borrow]
        s_ops = side.reads + [side.aliased[t] for t in keep]
        s_alias = {n_in + len(side.reads) + pos: n_out + t for pos, t in enumerate(keep)}
        s_alias.update({n_prefetch + i: n_out + t for t, i in borrow.items()})
    s_in = [ANY] * len(s_ops)
    n_sin, n_sout = len(s_ops), len(s_out)

    def wrapped(*refs):
        a, b, c = n_in + n_sin, n_in + n_sin + n_out, n_in + n_sin + n_out + n_sout
        ids = [pl.program_id(d) for d in range(len(grid))]
        first = functools.reduce(lambda p, q: p & q, [i == 0 for i in ids])
        last = functools.reduce(lambda p, q: p & q, [i == g - 1 for i, g in zip(ids, grid)])
        side.hooks(refs[n_in:a], refs[b:c], refs[-2:], first, last)
        body(*refs[:n_in], *refs[a:b], *refs[c:-2])

    spec = dict(grid=grid, in_specs=in_specs + s_in, out_specs=out_specs + s_out, scratch_shapes=scratch + s_scr)
    if n_prefetch:
        spec = dict(grid_spec=pltpu.PrefetchScalarGridSpec(num_scalar_prefetch=n_prefetch, **spec))
    return pl.pallas_call(
        body if side is None else wrapped, name=name, out_shape=out_shape + s_shape,
        input_output_aliases={**s_alias, **{n_prefetch + i: o for i, o in (aliases or {}).items()}},
        compiler_params=_cparams(sem if side is None else ("arbitrary",) * len(grid)), **spec,
    )(*operands, *s_ops)


def _mm_tk(K):
    if K <= MM_TK:
        return K
    return max(t for t in range(LANE, MM_TK + 1, LANE) if K % t == 0)


def _mm(name, a, b, mode, out_dtypes, extras=(), epilogue=None, side=None):
    if mode == "nn":
        (M, K), (K2, N) = a.shape, b.shape
    elif mode == "nt":
        (M, K), (N, K2) = a.shape, b.shape
    else:
        (K, M), (K2, N) = a.shape, b.shape
    assert K == K2, (name, a.shape, b.shape)
    tm, tn, tk = min(MM_TM, M), min(MM_TN, N), _mm_tk(K)
    assert M % tm == 0 and N % tn == 0 and K % tk == 0, (name, M, N, K)
    ni, nj, nk = M // tm, N // tn, K // tk
    ne, no = len(extras), len(out_dtypes)
    if epilogue is None:
        epilogue = lambda acc: (acc,)
    s_ops, s_in, s_out, s_shape, s_alias, s_scr = _side_parts(side)
    n_in, n_sin, n_sout = 2 + ne, len(s_ops), len(s_out)

    def body(*refs):
        a_ref, b_ref = refs[:2]
        extra_refs = refs[2:n_in]
        out_refs = refs[n_in + n_sin:n_in + n_sin + no]
        rest = refs[n_in + n_sin + no + n_sout:]
        i, j, k = pl.program_id(0), pl.program_id(1), pl.program_id(2)
        if side is not None:
            side.hooks(refs[n_in:n_in + n_sin], refs[n_in + n_sin + no:n_in + n_sin + no + n_sout], rest[-2:],
                       (i == 0) & (j == 0) & (k == 0), (i == ni - 1) & (j == nj - 1) & (k == nk - 1))
        av, bv = a_ref[...].astype(BF16), b_ref[...].astype(BF16)
        prod = _nn(av, bv) if mode == "nn" else _nt(av, bv) if mode == "nt" else _tn(av, bv)

        def finish(acc):
            res = epilogue(acc, *[e[...] for e in extra_refs])
            for o_ref, r in zip(out_refs, res):
                o_ref[...] = r.astype(o_ref.dtype)

        if nk == 1:
            finish(prod)
        else:
            acc_ref = rest[0]

            @pl.when(k == 0)
            def _():
                acc_ref[...] = prod

            @pl.when((k > 0) & (k < nk - 1))
            def _():
                acc_ref[...] += prod

            @pl.when(k == nk - 1)
            def _():
                finish(acc_ref[...] + prod)

    if mode == "nn":
        a_spec = pl.BlockSpec((tm, tk), lambda i, j, k: (i, k))
        b_spec = pl.BlockSpec((tk, tn), lambda i, j, k: (k, j))
    elif mode == "nt":
        a_spec = pl.BlockSpec((tm, tk), lambda i, j, k: (i, k))
        b_spec = pl.BlockSpec((tn, tk), lambda i, j, k: (j, k))
    else:
        a_spec = pl.BlockSpec((tk, tm), lambda i, j, k: (k, i))
        b_spec = pl.BlockSpec((tk, tn), lambda i, j, k: (k, j))
    o_spec = pl.BlockSpec((tm, tn), lambda i, j, k: (i, j))
    sem = ("arbitrary",) * 3 if side is not None else ("parallel", "parallel", "arbitrary")
    outs = pl.pallas_call(
        body, name=name,
        grid=(ni, nj, nk),
        in_specs=[a_spec, b_spec] + [o_spec] * ne + s_in,
        out_specs=[o_spec] * no + s_out,
        out_shape=[jax.ShapeDtypeStruct((M, N), dt) for dt in out_dtypes] + s_shape,
        input_output_aliases=s_alias(n_in, no),
        scratch_shapes=([pltpu.VMEM((tm, tn), F32)] if nk > 1 else []) + s_scr,
        compiler_params=_cparams(sem),
    )(a, b, *extras, *s_ops)
    return outs[0] if len(outs) == 1 else outs


def _row_spec(tr, d):
    return pl.BlockSpec((tr, d), lambda i: (i, 0))


def _vec_spec(d):
    return pl.BlockSpec((1, d), lambda i: (0, 0))


def _rms_fwd(name, x, w, side=None):
    T, D = x.shape
    tr = min(ROW_TILE, T)

    def body(x_ref, w_ref, o_ref):
        xf = x_ref[...]
        r = lax.rsqrt(jnp.mean(xf * xf, axis=-1, keepdims=True) + EPS)
        o_ref[...] = (xf * r * w_ref[...]).astype(BF16)

    outs = _call_with_side(body, name, (T // tr,), [_row_spec(tr, D), _vec_spec(D)], [_row_spec(tr, D)],
                           [jax.ShapeDtypeStruct((T, D), BF16)], [], ("parallel",), (x, w), side)
    return outs[0] if side is None else outs


def _rms_bwd(name, dy, h, w, dres, side=None):
    T, D = h.shape
    tr = min(ROW_TILE, T)

    def body(dy_ref, h_ref, w_ref, dres_ref, dh_ref, dhb_ref, dw_ref):
        @pl.when(pl.program_id(0) == 0)
        def _():
            dw_ref[...] = jnp.zeros_like(dw_ref)

        hf, dyv = h_ref[...], dy_ref[...]
        r = lax.rsqrt(jnp.mean(hf * hf, axis=-1, keepdims=True) + EPS)
        xhat = hf * r
        dw_ref[...] += jnp.sum(dyv * xhat, axis=0, keepdims=True)
        dxh = dyv * w_ref[...]
        dh = dres_ref[...] + r * (dxh - xhat * jnp.mean(dxh * xhat, axis=-1, keepdims=True))
        dh_ref[...] = dh
        dhb_ref[...] = dh.astype(BF16)

    return _call_with_side(
        body, name, (T // tr,),
        [_row_spec(tr, D), _row_spec(tr, D), _vec_spec(D), _row_spec(tr, D)],
        [_row_spec(tr, D), _row_spec(tr, D), _vec_spec(D)],
        [jax.ShapeDtypeStruct((T, D), F32), jax.ShapeDtypeStruct((T, D), BF16), jax.ShapeDtypeStruct((1, D), F32)],
        [], ("arbitrary",), (dy, h, w, dres), side)


def _loss_head(h2, target, w):
    T, D = h2.shape
    tr = min(ROW_TILE, T)

    def body(h_ref, t_ref, w_ref, loss_ref, dh_ref, dhb_ref, dw_ref):
        @pl.when(pl.program_id(0) == 0)
        def _():
            dw_ref[...] = jnp.zeros_like(dw_ref)
            loss_ref[...] = jnp.zeros_like(loss_ref)

        hf, wv = h_ref[...], w_ref[...]
        r = lax.rsqrt(jnp.mean(hf * hf, axis=-1, keepdims=True) + EPS)
        xhat = hf * r
        diff = xhat * wv - t_ref[...]
        loss_ref[...] += 0.5 * jnp.sum(jnp.mean(diff * diff, axis=-1, keepdims=True))
        dyv = diff * (1.0 / D)
        dw_ref[...] += jnp.sum(dyv * xhat, axis=0, keepdims=True)
        dxh = dyv * wv
        dh = r * (dxh - xhat * jnp.mean(dxh * xhat, axis=-1, keepdims=True))
        dh_ref[...] = dh
        dhb_ref[...] = dh.astype(BF16)

    return pl.pallas_call(
        body, name="loss_head", grid=(T // tr,),
        in_specs=[_row_spec(tr, D), _row_spec(tr, D), _vec_spec(D)],
        out_specs=[_vec_spec(LANE), _row_spec(tr, D), _row_spec(tr, D), _vec_spec(D)],
        out_shape=[jax.ShapeDtypeStruct((1, LANE), F32), jax.ShapeDtypeStruct((T, D), F32),
                   jax.ShapeDtypeStruct((T, D), BF16), jax.ShapeDtypeStruct((1, D), F32)],
        compiler_params=_cparams(("arbitrary",)),
    )(h2, target, w)


def _gate_tiles(T, D):
    goff = 4 * _hgw() + 3 * _atw()
    tc = min(1024, D)
    assert goff % tc == 0 and D % tc == 0
    return min(ROW_TILE, T), tc, goff // tc, D // tc


def _merge(z, pa, pb, side=None):
    T, D = pa.shape
    tr, tc, g0, nd = _gate_tiles(T, D)

    def body(ga_ref, gb_ref, pa_ref, pb_ref, o_ref):
        o_ref[...] = (_sigmoid(ga_ref[...]) * pa_ref[...] + _sigmoid(gb_ref[...]) * pb_ref[...]).astype(BF16)

    t = pl.BlockSpec((tr, tc), lambda i, j: (i, j))
    outs = _call_with_side(
        body, "merge", (T // tr, nd),
        [pl.BlockSpec((tr, tc), lambda i, j: (i, g0 + j)), pl.BlockSpec((tr, tc), lambda i, j: (i, g0 + nd + j)), t, t],
        [t], [jax.ShapeDtypeStruct((T, D), BF16)], [], ("parallel", "parallel"), (z, z, pa, pb), side)
    return outs[0] if side is None else outs


def _dmerge(dm, z, pa, pb):
    T, D = pa.shape
    tr, tc, g0, nd = _gate_tiles(T, D)

    def body(dm_ref, ga_ref, gb_ref, pa_ref, pb_ref, dpa_ref, dpb_ref, dga_ref, dgb_ref):
        dmv = dm_ref[...]
        sa, sb = _sigmoid(ga_ref[...]), _sigmoid(gb_ref[...])
        dpa_ref[...] = (dmv * sa).astype(BF16)
        dpb_ref[...] = (dmv * sb).astype(BF16)
        dga_ref[...] = (dmv * pa_ref[...] * sa * (1.0 - sa)).astype(BF16)
        dgb_ref[...] = (dmv * pb_ref[...] * sb * (1.0 - sb)).astype(BF16)

    t = pl.BlockSpec((tr, tc), lambda i, j: (i, j))
    return pl.pallas_call(
        body, name="dmerge", grid=(T // tr, nd),
        in_specs=[t, pl.BlockSpec((tr, tc), lambda i, j: (i, g0 + j)),
                  pl.BlockSpec((tr, tc), lambda i, j: (i, g0 + nd + j)), t, t],
        out_specs=[t, t, t, t],
        out_shape=[jax.ShapeDtypeStruct((T, D), BF16)] * 4,
        compiler_params=_cparams(("parallel", "parallel")),
    )(dm, z, z, pa, pb)


def _hg_gates(xq, xf, lb):
    f = _sigmoid(xf)
    g = lb + (1.0 - lb) * f
    sq = _sigmoid(xq)
    return f, g, jnp.log(g), 1.0 - g, sq, xq * sq * (HG_D ** -0.5)


def _split2(x):
    hi = x.astype(BF16)
    return hi, (x - hi.astype(F32)).astype(BF16)


def _tri_sum(tri, x):
    hi, rest = x.astype(BF16), x - x.astype(BF16).astype(F32)
    mid, lo = _split2(rest)
    return _nn(tri, lo) + _nn(tri, mid) + _nn(tri, hi)


def _hg_decays(lg, tri_incl, rowi):
    b = _tri_sum(tri_incl, lg)
    b_last = jnp.sum(lg, axis=0, keepdims=True)
    b_mid = jnp.sum(jnp.where(rowi <= CHUNK // 2, lg, 0.0), axis=0, keepdims=True)
    return b, b_last, b_mid


HG_GROUP = 2


def _hg_in_specs(T):
    ng = HG_HEADS // HG_GROUP
    return [pl.BlockSpec((T, HG_GROUP * HG_D), lambda h, s=s: (0, s * ng + h)) for s in range(4)]


def _hg_fwd(z, lb_logits, hgw, side=None):
    T = z.shape[0]
    H, d, C, G = HG_HEADS, HG_D, CHUNK, HG_GROUP
    nc = T // C

    def body(hq_ref, hf_ref, hi_ref, hg_ref, lbl_ref, w_ref, ya_ref, o_ref, s_ref):
        lb_all = 1.0 / (1.0 + jnp.exp(lbl_ref[1:2, :] - lbl_ref[0:1, :]))
        wv = w_ref[...]
        row = lax.broadcasted_iota(jnp.int32, (C, C), 0)
        col = lax.broadcasted_iota(jnp.int32, (C, C), 1)
        tril = col <= row
        tri_incl = tril.astype(BF16)
        rowi = lax.broadcasted_iota(jnp.int32, (C, G * d), 0)
        lanes = [slice(hh * d, (hh + 1) * d) for hh in range(G)]
        per_head = lambda fn: jnp.concatenate([fn(hh, sl) for hh, sl in enumerate(lanes)], axis=1)
        wv_all = jnp.tile(wv, (1, G))

        def chunk(c, states):
            rows = pl.ds(pl.multiple_of(c * C, C), C)
            xq, xf, v, xg = hq_ref[rows, :], hf_ref[rows, :], hi_ref[rows, :], hg_ref[rows, :]
            _, _, lg, kk, _, q = _hg_gates(xq, xf, lb_all)
            b, b_last, b_mid = _hg_decays(lg, tri_incl, rowi)
            vb, qe = v.astype(BF16), (q * jnp.exp(b)).astype(BF16)
            qt = (q * jnp.exp(b - b_mid)).astype(BF16)
            kt = (kk * jnp.exp(jnp.minimum(b_mid - b, EXP_CLAMP))).astype(BF16)
            kd, e_last = (kk * jnp.exp(b_last - b)).astype(BF16), jnp.exp(b_last)
            for hh, st in enumerate(states):
                s_ref[hh, c] = st
            o = per_head(lambda hh, sl: _nt(qe[:, sl], states[hh].astype(BF16)))
            a = [jnp.where(tril, _nt(qt[:, sl], kt[:, sl]), 0.0).astype(BF16) for sl in lanes]
            o = o + per_head(lambda hh, sl: _nn(a[hh], vb[:, sl]))
            o_ref[rows, :] = o
            r = per_head(lambda hh, sl: jnp.broadcast_to(
                lax.rsqrt(jnp.mean(o[:, sl] * o[:, sl], axis=-1, keepdims=True) + EPS), (C, d)))
            ya_ref[rows, :] = (o * r * wv_all * (xg * _sigmoid(xg))).astype(BF16)
            return tuple(st * e_last[:, sl] + _tn(vb[:, sl], kd[:, sl]) for st, sl in zip(states, lanes))

        lax.fori_loop(0, nc, chunk, tuple(jnp.zeros((d, d), F32) for _ in range(G)))

    heads = pl.BlockSpec((T, G * d), lambda h: (0, h))
    return _call_with_side(
        body, "hg_fwd", (H // G,),
        _hg_in_specs(T) + [pl.BlockSpec((2, G * d), lambda h: (0, h)), pl.BlockSpec((1, d), lambda h: (0, 0))],
        [heads, heads, pl.BlockSpec((G, nc, d, d), lambda h: (h, 0, 0, 0))],
        [jax.ShapeDtypeStruct((T, H * d), BF16), jax.ShapeDtypeStruct((T, H * d), F32),
         jax.ShapeDtypeStruct((H, nc, d, d), F32)],
        [], ("parallel",), (z, z, z, z, lb_logits, hgw), side)


def _hg_bwd(z, o, dya, states, lb_logits, hgw, side=None):
    T = z.shape[0]
    H, d, C, G = HG_HEADS, HG_D, CHUNK, HG_GROUP
    nc = T // C
    scale = HG_D ** -0.5

    def body(hq_ref, hf_ref, hi_ref, hg_ref, o_ref, dy_ref, s_ref, lbl_ref, w_ref,
             dq_ref, df_ref, di_ref, dg_ref, dlbl_ref, dw_ref, acc_ref):
        lb_all = 1.0 / (1.0 + jnp.exp(lbl_ref[1:2, :] - lbl_ref[0:1, :]))
        wv = w_ref[...]
        row = lax.broadcasted_iota(jnp.int32, (C, C), 0)
        col = lax.broadcasted_iota(jnp.int32, (C, C), 1)
        tril = col <= row
        tri_incl = tril.astype(BF16)
        triu_incl = (col >= row).astype(BF16)
        rowi = lax.broadcasted_iota(jnp.int32, (C, G * d), 0)
        lanes = [slice(hh * d, (hh + 1) * d) for hh in range(G)]
        per_head = lambda fn: jnp.concatenate([fn(hh, sl) for hh, sl in enumerate(lanes)], axis=1)
        head_mean = lambda x: per_head(
            lambda hh, sl: jnp.broadcast_to(jnp.mean(x[:, sl], axis=-1, keepdims=True), (C, d)))
        wv_all = jnp.tile(wv, (1, G))
        lb = lb_all
        acc_ref[...] = jnp.zeros_like(acc_ref)

        @pl.when(pl.program_id(0) == 0)
        def _():
            dw_ref[...] = jnp.zeros_like(dw_ref)

        def chunk(i, carry):
            dsts, tail = carry
            c = nc - 1 - i
            rows = pl.ds(pl.multiple_of(c * C, C), C)
            xq, xf, v, xg = hq_ref[rows, :], hf_ref[rows, :], hi_ref[rows, :], hg_ref[rows, :]
            f, g, lg, kk, sq, q = _hg_gates(xq, xf, lb)
            b, b_last, b_mid = _hg_decays(lg, tri_incl, rowi)
            e_b, e_qm, e_km = jnp.exp(b), jnp.exp(b - b_mid), jnp.exp(jnp.minimum(b_mid - b, EXP_CLAMP))
            e_kl, e_last = jnp.exp(b_last - b), jnp.exp(b_last)
            ov, dy = o_ref[rows, :], dy_ref[rows, :]
            r = lax.rsqrt(head_mean(ov * ov) + EPS)
            xhat = ov * r
            sg = _sigmoid(xg)
            dxg = dy * xhat * wv_all * (sg * (1.0 + xg * (1.0 - sg)))
            dyn = dy * (xg * sg)
            acc_ref[0:1, :] += jnp.sum(dyn * xhat, axis=0, keepdims=True)
            dxh = dyn * wv_all
            dof = r * (dxh - xhat * head_mean(dxh * xhat))
            do, vb = dof.astype(BF16), v.astype(BF16)
            qe, kd, qt, kt = (q * e_b).astype(BF16), (kk * e_kl).astype(BF16), (q * e_qm).astype(BF16), (kk * e_km).astype(BF16)
            pm = [jnp.where(tril, _nt(do[:, sl], vb[:, sl]), 0.0).astype(BF16) for sl in lanes]
            am = [jnp.where(tril, _nt(qt[:, sl], kt[:, sl]), 0.0).astype(BF16) for sl in lanes]
            st = [_split2(s_ref[hh, c]) for hh in range(G)]
            ds = [_split2(x) for x in dsts]
            dq_state = per_head(lambda hh, sl: _nn(do[:, sl], st[hh][1]) + _nn(do[:, sl], st[hh][0]))
            dk_state = per_head(lambda hh, sl: _nn(vb[:, sl], ds[hh][1]) + _nn(vb[:, sl], ds[hh][0]))
            dq_intra = per_head(lambda hh, sl: _nn(pm[hh], kt[:, sl]))
            dk_intra = per_head(lambda hh, sl: _tn(pm[hh], qt[:, sl]))
            dv = per_head(lambda hh, sl: _tn(am[hh], do[:, sl]) + _nt(kd[:, sl], ds[hh][0]))
            new_dsts = tuple(x * e_last[:, sl] + _tn(do[:, sl], qe[:, sl]) for x, sl in zip(dsts, lanes))
            dq = dq_state * e_b + dq_intra * e_qm
            dk = dk_intra * e_km + dk_state * e_kl
            db = (qe.astype(F32) * dq_state + qt.astype(F32) * dq_intra
                  - kt.astype(F32) * dk_intra - kd.astype(F32) * dk_state)
            dlg = _tri_sum(triu_incl, db) + tail
            dgate = dlg / g - dk
            acc_ref[1:2, :] += jnp.sum(dgate * (1.0 - f), axis=0, keepdims=True)
            dq_ref[rows, :] = (dq * scale * (sq * (1.0 + xq * (1.0 - sq)))).astype(BF16)
            df_ref[rows, :] = (dgate * (1.0 - lb) * f * (1.0 - f)).astype(BF16)
            di_ref[rows, :] = dv.astype(BF16)
            dg_ref[rows, :] = dxg.astype(BF16)
            return new_dsts, tail + jnp.sum(db, axis=0, keepdims=True)

        lax.fori_loop(0, nc, chunk, (tuple(jnp.zeros((d, d), F32) for _ in range(G)), jnp.zeros((1, G * d), F32)))
        dw_ref[...] += functools.reduce(lambda p, q: p + q, [acc_ref[0:1, sl] for sl in lanes])
        dl0 = acc_ref[1:2, :] * lb_all * (1.0 - lb_all)
        dlbl_ref[0:1, :] = dl0
        dlbl_ref[1:2, :] = -dl0

    heads = pl.BlockSpec((T, G * d), lambda h: (0, h))
    logits = pl.BlockSpec((2, G * d), lambda h: (0, h))
    return _call_with_side(
        body, "hg_bwd", (H // G,),
        _hg_in_specs(T) + [heads, heads, pl.BlockSpec((G, nc, d, d), lambda h: (h, 0, 0, 0)), logits,
                           pl.BlockSpec((1, d), lambda h: (0, 0))],
        [heads, heads, heads, heads, logits, pl.BlockSpec((1, d), lambda h: (0, 0))],
        [jax.ShapeDtypeStruct((T, H * d), BF16)] * 4 + [jax.ShapeDtypeStruct((2, H * d), F32),
                                                        jax.ShapeDtypeStruct((1, d), F32)],
        [pltpu.VMEM((8, G * d), F32)], ("arbitrary",), (z, z, z, z, o, dya, states, lb_logits, hgw), side)


def _at_dims():
    pad = LEFT * CHUNK
    return pad, QB + pad, AT_HEADS * AT_DH // LANE, 4 * _hgw() // LANE


def _rel_of_period():
    pad, W, _, _ = _at_dims()
    n = jnp.arange(QB + W)
    return jnp.clip(pad - jnp.where(n < W, n, n - (QB + W)), -REL_CLIP, REL_CLIP) + REL_CLIP


def _bias_window(rel_bias):
    pad, W, _, _ = _at_dims()
    H, P = rel_bias.shape[0], QB + W
    per = rel_bias[:, _rel_of_period()]
    win = jnp.tile(per, (1, QB))[:, :QB * (P - 1)].reshape(H, QB, P - 1)[:, :, :W]
    t = jnp.arange(QB)[:, None]
    j = jnp.arange(W)[None, :]
    ok = (j // CHUNK >= t // CHUNK) & (j // CHUNK <= t // CHUNK + LEFT)
    return jnp.where(ok[None], win, NEG)


def _bias_window_grad(dbw):
    pad, W, _, _ = _at_dims()
    H, P = dbw.shape[0], QB + W
    flat = jnp.pad(dbw, ((0, 0), (0, 0), (0, P - 1 - W))).reshape(H, QB * (P - 1))
    per = jnp.pad(flat, ((0, 0), (0, QB))).reshape(H, QB, P).sum(axis=1)
    onehot = _rel_of_period()[:, None] == jnp.arange(2 * REL_CLIP + 1)[None, :]
    return jnp.dot(per, onehot.astype(F32), precision=HIGHEST)


def _at_stack(x):
    first = lax.broadcasted_iota(jnp.int32, x.shape, 1) < AT_DH
    return jnp.concatenate([jnp.where(first, x, 0.0), jnp.where(first, 0.0, x)], axis=0).astype(BF16)


def _at_unstack(x):
    first = lax.broadcasted_iota(jnp.int32, (QB, LANE), 1) < AT_DH
    return jnp.where(first, x[:QB], x[QB:])


def _at_softmax(qs, kw, bias_ref, qi):
    pad, W, _, _ = _at_dims()
    s = _nt(qs, kw) + bias_ref[...].reshape(2 * QB, W)
    valid = lax.broadcasted_iota(jnp.int32, (2 * QB, W), 1) + qi * QB >= pad
    s = jnp.where(valid, s, NEG)
    e = jnp.exp(s - jnp.max(s, axis=-1, keepdims=True))
    return e * (1.0 / jnp.sum(e, axis=-1, keepdims=True))


def _at_fwd(z, bias_win, side=None):
    T = z.shape[0]
    pad, W, HP, c0 = _at_dims()
    nq = T // QB

    def body(q_ref, k_ref, v_ref, bias_ref, o_ref, kpad, vpad):
        qi = pl.program_id(1)

        @pl.when(qi == 0)
        def _():
            kpad[0:pad, :] = jnp.zeros((pad, LANE), BF16)
            vpad[0:pad, :] = jnp.zeros((pad, LANE), BF16)
            kpad[pad:, :] = k_ref[...].astype(BF16)
            vpad[pad:, :] = v_ref[...].astype(BF16)

        win = pl.ds(pl.multiple_of(qi * QB, QB), W)
        kw, vw = kpad[win, :], vpad[win, :]
        p = _at_softmax(_at_stack(q_ref[...] * (AT_DH ** -0.5)), kw, bias_ref, qi)
        o_ref[...] = _at_unstack(_nn(p.astype(BF16), vw)).astype(BF16)

    full = lambda s: pl.BlockSpec((T, LANE), lambda hp, qi, s=s: (0, c0 + s * HP + hp))
    return _call_with_side(
        body, "at_fwd", (HP, nq),
        [pl.BlockSpec((QB, LANE), lambda hp, qi: (qi, c0 + hp)), full(1), full(2),
         pl.BlockSpec((2, QB, W), lambda hp, qi: (hp, 0, 0))],
        [pl.BlockSpec((QB, LANE), lambda hp, qi: (qi, hp))],
        [jax.ShapeDtypeStruct((T, HP * LANE), BF16)],
        [pltpu.VMEM((T + pad, LANE), BF16)] * 2, ("parallel", "arbitrary"), (z, z, z, bias_win), side)


def _at_bwd(z, dyb, bias_win, side=None):
    T = z.shape[0]
    pad, W, HP, c0 = _at_dims()
    nq = T // QB
    scale = AT_DH ** -0.5

    def body(q_ref, k_ref, v_ref, do_ref, bias_ref, dq_ref, dk_ref, dv_ref, dbias_ref, kpad, vpad, dkpad, dvpad):
        qi = pl.program_id(1)

        @pl.when(qi == 0)
        def _():
            kpad[0:pad, :] = jnp.zeros((pad, LANE), BF16)
            vpad[0:pad, :] = jnp.zeros((pad, LANE), BF16)
            kpad[pad:, :] = k_ref[...].astype(BF16)
            vpad[pad:, :] = v_ref[...].astype(BF16)
            dkpad[...] = jnp.zeros_like(dkpad)
            dvpad[...] = jnp.zeros_like(dvpad)
            dbias_ref[...] = jnp.zeros_like(dbias_ref)

        win = pl.ds(pl.multiple_of(qi * QB, QB), W)
        kw, vw = kpad[win, :], vpad[win, :]
        qs, dos = _at_stack(q_ref[...] * scale), _at_stack(do_ref[...])
        p = _at_softmax(qs, kw, bias_ref, qi)
        dp = _nt(dos, vw)
        ds = p * (dp - jnp.sum(p * dp, axis=-1, keepdims=True))
        dbias_ref[...] += ds.reshape(2, QB, W)
        dsb = ds.astype(BF16)
        dq_ref[...] = (_at_unstack(_nn(dsb, kw)) * scale).astype(BF16)
        dkpad[win, :] += _tn(dsb, qs)
        dvpad[win, :] += _tn(p.astype(BF16), dos)

        @pl.when(qi == nq - 1)
        def _():
            dk_ref[...] = dkpad[pad:, :].astype(BF16)
            dv_ref[...] = dvpad[pad:, :].astype(BF16)

    full = lambda s: pl.BlockSpec((T, LANE), lambda hp, qi, s=s: (0, c0 + s * HP + hp))
    blk = pl.BlockSpec((QB, LANE), lambda hp, qi: (qi, hp))
    col = pl.BlockSpec((T, LANE), lambda hp, qi: (0, hp))
    bw = pl.BlockSpec((2, QB, W), lambda hp, qi: (hp, 0, 0))
    return _call_with_side(
        body, "at_bwd", (HP, nq),
        [pl.BlockSpec((QB, LANE), lambda hp, qi: (qi, c0 + hp)), full(1), full(2), blk, bw],
        [blk, col, col, bw],
        [jax.ShapeDtypeStruct((T, HP * LANE), BF16)] * 3 + [jax.ShapeDtypeStruct(bias_win.shape, F32)],
        [pltpu.VMEM((T + pad, LANE), BF16)] * 2 + [pltpu.VMEM((T + pad, LANE), F32)] * 2,
        ("parallel", "arbitrary"), (z, z, z, dyb, bias_win), side)


def _piece_tiles(name, full_shape):
    pr, pc = _piece_shape(name, full_shape)
    tr = min(ROW_TILE, pr)
    assert pr % tr == 0
    nt = pr // tr
    if name in ROW_SHARDED:
        return tr, nt, lambda q, half, i: ((2 * q + half) * nt + i, 0)
    return tr, nt, lambda q, half, i: (half * nt + i, q)


def _cast_into_full(name, wq, place, side=None):
    full = _full_shape(name, wq.shape)
    pc = wq.shape[1]
    tr, nt, at = _piece_tiles(name, full)

    def body(place_ref, w_ref, o_ref):
        o_ref[...] = w_ref[...].astype(BF16)

    outs = _call_with_side(
        body, "cast_" + name, (2, nt), [pl.BlockSpec((tr, pc), lambda h, i, s: (h * nt + i, 0))],
        [pl.BlockSpec((tr, pc), lambda h, i, s: at(s[0], h, i))], [jax.ShapeDtypeStruct(full, BF16)],
        [], ("parallel", "parallel"), (place, wq), side, n_prefetch=1)
    return outs[0] if side is None else outs


def _g_w_in_half(u1, dz, place, own, side=None):
    T, K = u1.shape
    N = dz.shape[1]
    hk, tn = K // 2, min(MM_TN, N)
    half = (lambda s: s[1]) if own else (lambda s: 1 - s[1])

    def body(place_ref, a_ref, b_ref, o_ref):
        o_ref[...] = _tn(a_ref[...], b_ref[...]).astype(BF16)

    outs = _call_with_side(
        body, "g_w_in_keep" if own else "g_w_in_send", (N // tn,),
        [pl.BlockSpec((T, hk), lambda j, s: (0, half(s))), pl.BlockSpec((T, tn), lambda j, s: (0, j))],
        [pl.BlockSpec((hk, tn), lambda j, s: (0, j))], [jax.ShapeDtypeStruct((hk, N), BF16)],
        [], ("parallel",), (place, u1, dz), side, n_prefetch=1)
    return outs[0] if side is None else outs


def _chip_sum(name, grad, theirs, place, kept_rows=False):
    pr, pc = theirs.shape[1:]
    tr, nt, at = _piece_tiles(name, (2 * grad.shape[0], grad.shape[1]) if kept_rows else grad.shape)
    if kept_rows:
        at = lambda q, half, i: (i, q)

    def body(place_ref, g_ref, t_ref, o_ref):
        o_ref[...] = (g_ref[...].astype(F32) + t_ref[...].astype(F32)).astype(BF16)

    piece = pl.BlockSpec((None, tr, pc), lambda q, i, s: (q, i, 0))
    return pl.pallas_call(
        body, name="chip_sum_" + name,
        grid_spec=pltpu.PrefetchScalarGridSpec(
            num_scalar_prefetch=1, grid=(4, nt),
            in_specs=[pl.BlockSpec((tr, pc), lambda q, i, s: at(q, s[1], i)), piece], out_specs=piece),
        out_shape=jax.ShapeDtypeStruct(theirs.shape, BF16),
        compiler_params=_cparams(("parallel", "parallel")),
    )(place, grad, theirs)


def _piece_sum(name, chip_sums, got, place):
    pr, pc = chip_sums.shape[1:]
    tr = min(ROW_TILE, pr)

    def body(place_ref, own_ref, got_ref, o_ref):
        o_ref[...] = (own_ref[...].astype(F32) + got_ref[0].astype(F32) + got_ref[1].astype(F32)
                      + got_ref[2].astype(F32))

    return pl.pallas_call(
        body, name="piece_sum_" + name,
        grid_spec=pltpu.PrefetchScalarGridSpec(
            num_scalar_prefetch=1, grid=(pr // tr,),
            in_specs=[pl.BlockSpec((None, tr, pc), lambda i, s: (s[0], i, 0)),
                      pl.BlockSpec((3, tr, pc), lambda i, s: (0, i, 0))],
            out_specs=pl.BlockSpec((tr, pc), lambda i, s: (i, 0))),
        out_shape=jax.ShapeDtypeStruct((pr, pc), F32),
        compiler_params=_cparams(("parallel",)),
    )(place, chip_sums, got)


def _adam_quarter(name, w, m, v, g_mine, g_sib, place, side=None):
    pr, pc = g_mine.shape
    tr = min(ROW_TILE // 2, pr)
    nt = pr // tr

    def body(place_ref, w_ref, m_ref, v_ref, gm_ref, gs_ref, go_ref, d_ref, mo_ref, vo_ref):
        g = jnp.where(pl.program_id(0) == place_ref[1], gm_ref[...], gs_ref[...])
        delta, mn, vn = _adam_math(w_ref[...], g, m_ref[...], v_ref[...])
        go_ref[...] = g
        d_ref[...] = delta
        mo_ref[...] = mn
        vo_ref[...] = vn

    quarter = pl.BlockSpec((tr, pc), lambda h, i, s: (h * nt + i, 0))
    mine = pl.BlockSpec((tr, pc), lambda h, i, s: (jnp.where(h == s[1], i, 0), 0))
    sib = pl.BlockSpec((tr, pc), lambda h, i, s: (jnp.where(h == s[1], 0, i), 0))
    return _call_with_side(
        body, "adam_" + name, (2, nt), [quarter, quarter, quarter, mine, sib], [quarter] * 4,
        [jax.ShapeDtypeStruct(w.shape, F32)] * 4, [], ("parallel", "parallel"),
        (place, w, m, v, g_mine, g_sib), side, n_prefetch=1)


def _adam_math(w, g, m, v):
    m = ADAM_B1 * m + (1.0 - ADAM_B1) * g
    v = ADAM_B2 * v + (1.0 - ADAM_B2) * (g * g)
    m_hat = m / (1.0 - ADAM_B1 ** ADAM_STEP)
    v_hat = v / (1.0 - ADAM_B2 ** ADAM_STEP)
    return -ADAM_LR * (m_hat / (jnp.sqrt(v_hat) + ADAM_EPS) + ADAM_WD * w), m, v


WEIGHTS = ("w_in", "w_branch_a", "w_branch_b", "w_out", "w_up", "w_down")
ROW_SHARDED = ("w_out", "w_down")
ANY = pl.BlockSpec(memory_space=pl.ANY)
MESH = pl.DeviceIdType.MESH


def _place():
    x, y, c = lax.axis_index("x"), lax.axis_index("y"), lax.axis_index("c")
    chips = [(1 - x, y), (x, 1 - y), (1 - x, 1 - y)]
    return x, y, c, 2 * x + y, chips, [2 * cx + cy for cx, cy in chips]


def _piece(full_ref, name, q, half):
    K, N = full_ref.shape
    if name in ROW_SHARDED:
        rows = K // 8
        return full_ref.at[pl.ds(q * (2 * rows) + half * rows, rows), :]
    return full_ref.at[pl.ds(half * (K // 2), K // 2), pl.ds(q * (N // 4), N // 4)]


def _piece_shape(name, full_shape):
    K, N = full_shape
    return (K // 8, N) if name in ROW_SHARDED else (K // 2, N // 4)


def _full_shape(name, quarter_shape):
    Kq, Nq = quarter_shape
    return (4 * Kq, Nq) if name in ROW_SHARDED else (Kq, 4 * Nq)


def _remote(src, dst, send_sem, recv_sem, device):
    return pltpu.make_async_remote_copy(src_ref=src, dst_ref=dst, send_sem=send_sem, recv_sem=recv_sem,
                                        device_id=device, device_id_type=MESH)


def _z_part(u1, w_in, z_prev, place, k0, count, side=None, own_quarter=False):
    T, K = u1.shape
    nq = w_in.shape[1] if own_quarter else w_in.shape[1] // 4
    N = 4 * nq
    tn = nq // 2 if (nq // 2) % LANE == 0 else nq
    tm = min(MM_TM, T)
    per = nq // tn
    col = lambda g, j, s: (s[0] ^ (k0 + g)) * per + j
    ins = [pl.BlockSpec((tm, K), lambda g, i, j, s: (i, 0)),
           pl.BlockSpec((K, tn), (lambda g, i, j, s: (0, j)) if own_quarter else (lambda g, i, j, s: (0, col(g, j, s))))]
    operands = [place, u1, w_in]
    if z_prev is not None:
        ins.append(ANY)
        operands.append(z_prev)

    def body(place_ref, a_ref, b_ref, *rest):
        rest[-1][...] = _nn(a_ref[...], b_ref[...].astype(BF16))

    return _call_with_side(
        body, "z_part_%d" % k0, (count, T // tm, per), ins,
        [pl.BlockSpec((tm, tn), lambda g, i, j, s: (i, col(g, j, s)))], [jax.ShapeDtypeStruct((T, N), F32)],
        [], ("parallel",) * 3, tuple(operands), side, n_prefetch=1, aliases={} if z_prev is None else {2: 0},
        borrow={0: 1} if side is not None and side.aliased and side.aliased[0] is w_in else None)


def _rows(ref, span):
    return ref if span is None else ref.at[pl.ds(span[0], span[1]), :]


def _gather_moves(items):
    count = {"near": lambda arg: 2, "far": lambda arg: 1, "pass": len}

    def build(reads, aliased, fresh, send_sems, recv_sems, off=0):
        x, y, c, p, chips, chip_ids = _place()
        south = c == 0
        far_src = jnp.where(south, chip_ids[0], chip_ids[1])
        far_dst = (jnp.where(south, x, 1 - x), jnp.where(south, 1 - y, y), c)
        out = []

        def add(ref, device):
            k = off + len(out)
            out.append(_remote(ref, ref, send_sems.at[k], recv_sems.at[k], device))

        for (name, _, moves), ref in zip(items, aliased):
            for kind, arg in moves:
                if kind == "near":
                    for chip in chips[:2]:
                        add(_rows(_piece(ref, name, p, c), arg), (*chip, c))
                elif kind == "far":
                    add(_rows(_piece(ref, name, far_src, c), arg), far_dst)
                else:
                    for j in arg:
                        add(_piece(ref, name, chip_ids[j], c), (x, y, 1 - c))
        return out

    nsem = sum(count[kind](arg) for _, _, moves in items for kind, arg in moves)
    return _Side(build, nsem, aliased=[a for _, a, _ in items])


def _ici_near(names, fulls, rows=None):
    return _gather_moves([(n, a, [("near", r)]) for n, a, r in zip(names, fulls, rows or [None] * len(names))])


def _ici_far(names, fulls, rows=None):
    return _gather_moves([(n, a, [("far", r)]) for n, a, r in zip(names, fulls, rows or [None] * len(names))])


def _d2d_gather(names, fulls, which=(0, 1, 2)):
    return _gather_moves([(n, a, [("pass", which)]) for n, a in zip(names, fulls)])


def _sib_send(names, grads):
    def build(reads, aliased, fresh, send_sems, recv_sems, off=0):
        x, y, c, _, _, _ = _place()
        out = []
        for i, name in enumerate(names):
            for q in range(4):
                k = off + 4 * i + q
                out.append(_remote(_piece(reads[i], name, q, 1 - c), fresh[i].at[q], send_sems.at[k], recv_sems.at[k],
                                   (x, y, 1 - c)))
        return out

    shapes = [jax.ShapeDtypeStruct((4,) + _piece_shape(name, g.shape), BF16) for name, g in zip(names, grads)]
    return _Side(build, 4 * len(names), reads=grads, fresh=shapes)


def _sib_send_half(sent):
    K2, N = sent.shape

    def build(reads, aliased, fresh, send_sems, recv_sems, off=0):
        x, y, c, _, _, _ = _place()
        return [_remote(reads[0].at[:, pl.ds(q * (N // 4), N // 4)], fresh[0].at[q], send_sems.at[off + q],
                        recv_sems.at[off + q], (x, y, 1 - c)) for q in range(4)]

    return _Side(build, 4, reads=[sent], fresh=[jax.ShapeDtypeStruct((4, K2, N // 4), BF16)])


def _chip_exchange(chip_sums, rows=None, got=None):
    rows = rows or [None] * len(chip_sums)

    def build(reads, aliased, fresh, send_sems, recv_sems, off=0):
        _, _, c, _, chips, chip_ids = _place()
        out = []
        for i in range(len(chip_sums)):
            for j, (chip, cid) in enumerate(zip(chips, chip_ids)):
                k = off + 3 * i + j
                out.append(_remote(_rows(reads[i].at[cid], rows[i]), _rows((aliased or fresh)[i].at[j], rows[i]),
                                   send_sems.at[k], recv_sems.at[k], (*chip, c)))
        return out

    if got is not None:
        return _Side(build, 3 * len(chip_sums), reads=chip_sums, aliased=got)
    shapes = [jax.ShapeDtypeStruct((3,) + s.shape[1:], BF16) for s in chip_sums]
    return _Side(build, 3 * len(chip_sums), reads=chip_sums, fresh=shapes)


HBM = pl.BlockSpec(memory_space=pltpu.HBM)
SEM = pl.BlockSpec(memory_space=pltpu.SEMAPHORE)


def _exchange_copies(s_refs, land_refs, send_sems, recv_sems):
    _, _, c, _, chips, chip_ids = _place()
    return [_remote(s_ref.at[cid], land_ref.at[j], send_sems.at[3 * i + j], recv_sems.at[3 * i + j], (*chip, c))
            for i, (s_ref, land_ref) in enumerate(zip(s_refs, land_refs))
            for j, (chip, cid) in enumerate(zip(chips, chip_ids))]


def _exchange_start(name, chip_sums):
    n = len(chip_sums)

    def body(*refs):
        for cp in _exchange_copies(refs[:n], refs[n:2 * n], refs[2 * n], refs[2 * n + 1]):
            cp.start()
        refs[-1][...] = jnp.zeros_like(refs[-1])

    lands = [jax.ShapeDtypeStruct((3,) + s.shape[1:], s.dtype) for s in chip_sums]
    hbm = lambda a: pltpu.with_memory_space_constraint(a, pltpu.HBM)
    outs = pl.pallas_call(
        body, name="exchange_start_" + name,
        out_shape=(pltpu.SemaphoreType.DMA((3 * n,)), pltpu.SemaphoreType.DMA((3 * n,)),
                   *[pltpu.HBM(a.shape, a.dtype) for a in chip_sums + lands], jax.ShapeDtypeStruct((8, LANE), F32)),
        in_specs=(HBM,) * (2 * n), out_specs=(SEM, SEM) + (HBM,) * (2 * n) + (pl.BlockSpec(memory_space=pltpu.VMEM),),
        input_output_aliases={i: 2 + i for i in range(2 * n)},
        compiler_params=pltpu.CompilerParams(has_side_effects=pltpu.SideEffectType.DATAFLOW_SIDE_EFFECTING),
    )(*[hbm(s) for s in chip_sums], *[hbm(lax.empty(a.shape, a.dtype)) for a in lands])
    return outs[0], outs[1], list(outs[2:2 + n]), list(outs[2 + n:2 + 2 * n]), outs[-1]


def _exchange_wait(name, flight, after):
    send_sems, recv_sems, s_thru, land_thru, _ = flight
    n = len(s_thru)

    def body(*refs):
        for cp in _exchange_copies(refs[:n], refs[n:2 * n], refs[2 * n], refs[2 * n + 1]):
            cp.wait_send()
            cp.wait_recv()

    outs = pl.pallas_call(
        body, name="exchange_wait_" + name,
        out_shape=tuple(pltpu.HBM(a.shape, a.dtype) for a in s_thru + land_thru),
        in_specs=(HBM,) * (2 * n) + (SEM, SEM, ANY), out_specs=(HBM,) * (2 * n),
        input_output_aliases={i: i for i in range(2 * n)},
        compiler_params=pltpu.CompilerParams(has_side_effects=pltpu.SideEffectType.DATAFLOW_SIDE_EFFECTING),
    )(*s_thru, *land_thru, send_sems, recv_sems, after)
    return list(outs[:n]), list(outs[n:])


def _move_copies(kind, name, f_ref, send_sems, recv_sems):
    return _gather_moves([(name, None, [(kind, None)])]).build([], [f_ref], [], send_sems, recv_sems)


def _move_start(kind, name, full):
    def body(f_ref, send_sems, recv_sems, f_thru, token):
        for cp in _move_copies(kind, name, f_ref, send_sems, recv_sems):
            cp.start()
        token[...] = jnp.zeros_like(token)

    return pl.pallas_call(
        body, name=kind + "_start_" + name,
        out_shape=(pltpu.SemaphoreType.DMA((2,)), pltpu.SemaphoreType.DMA((2,)), pltpu.HBM(full.shape, full.dtype),
                   jax.ShapeDtypeStruct((8, LANE), F32)),
        in_specs=(HBM,), out_specs=(SEM, SEM, HBM, pl.BlockSpec(memory_space=pltpu.VMEM)),
        input_output_aliases={0: 2},
        compiler_params=pltpu.CompilerParams(has_side_effects=pltpu.SideEffectType.DATAFLOW_SIDE_EFFECTING),
    )(pltpu.with_memory_space_constraint(full, pltpu.HBM))


def _move_wait(kind, name, flight, after):
    send_sems, recv_sems, f_thru, _ = flight

    def body(f_ref, send_sems, recv_sems, after_ref, f_out):
        for cp in _move_copies(kind, name, f_ref, send_sems, recv_sems):
            cp.wait_send()
            cp.wait_recv()

    return pl.pallas_call(
        body, name=kind + "_wait_" + name, out_shape=pltpu.HBM(f_thru.shape, f_thru.dtype),
        in_specs=(HBM, SEM, SEM, ANY), out_specs=HBM, input_output_aliases={0: 0},
        compiler_params=pltpu.CompilerParams(has_side_effects=pltpu.SideEffectType.DATAFLOW_SIDE_EFFECTING),
    )(f_thru, send_sems, recv_sems, after)


def _sib_share(halves):
    def build(reads, aliased, fresh, send_sems, recv_sems, off=0):
        x, y, c, _, _, _ = _place()
        return [_remote(reads[i], fresh[i], send_sems.at[off + i], recv_sems.at[off + i], (x, y, 1 - c))
                for i in range(len(halves))]

    return _Side(build, len(halves), reads=halves, fresh=[jax.ShapeDtypeStruct(h.shape, F32) for h in halves])


def _join(a, b):
    def build(reads, aliased, fresh, send_sems, recv_sems, off=0):
        ra, aa, fa = len(a.reads), len(a.aliased), len(a.fresh)
        return (a.build(reads[:ra], aliased[:aa], fresh[:fa], send_sems, recv_sems, off)
                + b.build(reads[ra:], aliased[aa:], fresh[fa:], send_sems, recv_sems, off + a.nsem))

    return _Side(build, a.nsem + b.nsem, a.reads + b.reads, a.aliased + b.aliased, a.fresh + b.fresh)


def _run_side(name, side):
    nr, na = len(side.reads), len(side.aliased)

    def body(*refs):
        n_in, n_out = nr + na, na + len(side.fresh)
        outs = refs[n_in:n_in + n_out]
        copies = side.build(refs[:nr], outs[:na], outs[na:], *refs[-2:])
        for cp in copies:
            cp.start()
        for cp in copies:
            cp.wait()

    return pl.pallas_call(
        body, name=name, in_specs=side.in_specs(), out_specs=side.out_specs(), out_shape=side.out_shape(),
        input_output_aliases=side.aliases(0, 0), scratch_shapes=side.scratch(),
    )(*side.operands())


def _small_allreduce_adam(gpart, w, m, v, after):
    R = gpart.shape[0]

    def body(g_ref, w_ref, m_ref, v_ref, after_ref, go_ref, d_ref, mo_ref, vo_ref, buf, send_sems, recv_sems):
        x, y, c = lax.axis_index("x"), lax.axis_index("y"), lax.axis_index("c")
        me = 4 * x + 2 * y + c
        buf[me] = g_ref[...]
        copies = []
        for k in range(1, 8):
            fx, fy, fc = (k >> 2) & 1, (k >> 1) & 1, k & 1
            peer = (1 - x if fx else x, 1 - y if fy else y, 1 - c if fc else c)
            cp = _remote(g_ref, buf.at[me], send_sems.at[k - 1], recv_sems.at[k - 1], peer)
            cp.start()
            copies.append((cp, 4 * peer[0] + 2 * peer[1] + peer[2]))
        for k, (cp, pid) in enumerate(copies):
            _remote(g_ref, buf.at[pid], send_sems.at[k], recv_sems.at[k], (x, y, c)).wait_recv()
        for cp, _ in copies:
            cp.wait_send()
        g = buf[0]
        for d in range(1, 8):
            g = g + buf[d]
        delta, mn, vn = _adam_math(w_ref[...], g, m_ref[...], v_ref[...])
        go_ref[...] = g
        d_ref[...] = delta
        mo_ref[...] = mn
        vo_ref[...] = vn

    vm = pl.BlockSpec(memory_space=pltpu.VMEM)
    return pl.pallas_call(
        body, name="small_allreduce_adam",
        in_specs=[vm] * 4 + [ANY], out_specs=[vm] * 4,
        out_shape=[jax.ShapeDtypeStruct((R, LANE), F32)] * 4,
        scratch_shapes=[pltpu.VMEM((8, R, LANE), F32), pltpu.SemaphoreType.DMA((7,)), pltpu.SemaphoreType.DMA((7,))],
    )(gpart, w, m, v, after)


def _pack(arrs):
    flat = jnp.concatenate([a.reshape(-1).astype(F32) for a in arrs])
    rows = -(-flat.shape[0] // (8 * LANE)) * 8
    return jnp.pad(flat, (0, rows * LANE - flat.shape[0])).reshape(rows, LANE)


def _unpack(packed, like):
    flat, out, off = packed.reshape(-1), [], 0
    for a in like:
        out.append(flat[off:off + a.size].reshape(a.shape))
        off += a.size
    return out


def kernel(x, w_in, lb_logits, hg_norm_w, rel_bias, w_branch_a, w_branch_b, w_out, norm_mix_w, norm_mlp_w, w_up, w_down, norm_final_w, loss_target, m_w_in, m_lb_logits, m_hg_norm_w, m_rel_bias, m_w_branch_a, m_w_branch_b, m_w_out, m_norm_mix_w, m_norm_mlp_w, m_w_up, m_w_down, m_norm_final_w, v_w_in, v_lb_logits, v_hg_norm_w, v_rel_bias, v_w_branch_a, v_w_branch_b, v_w_out, v_norm_mix_w, v_norm_mlp_w, v_w_up, v_w_down, v_norm_final_w):
    T, D = x.shape[1], x.shape[2]
    x2, tgt = x.reshape(T, D), loss_target.reshape(T, D)
    big = dict(w_in=(w_in, m_w_in, v_w_in), w_branch_a=(w_branch_a, m_w_branch_a, v_w_branch_a),
               w_branch_b=(w_branch_b, m_w_branch_b, v_w_branch_b), w_out=(w_out, m_w_out, v_w_out),
               w_up=(w_up, m_w_up, v_w_up), w_down=(w_down, m_w_down, v_w_down))
    big = {k: tuple(a[0] for a in v) for k, v in big.items()}
    nfw = norm_final_w.reshape(1, D)

    place = jnp.stack([2 * lax.axis_index("x") + lax.axis_index("y"), lax.axis_index("c")]).astype(jnp.int32)
    small3 = ["w_branch_a", "w_branch_b", "w_out"]

    def span(name, lo, hi):
        pr = big[name][0].shape[0] // 2
        return (pr * lo // 16, pr * (hi - lo) // 16)

    flight_in = _move_start("near", "w_in", _cast_into_full("w_in", big["w_in"][0], place))
    Wf = {name: _cast_into_full(name, big[name][0], place) for name in WEIGHTS if name != "w_in"}

    u1 = _rms_fwd("norm_mix", x2, norm_mix_w, side=_after(flight_in[-1]))
    z = _z_part(u1[0], big["w_in"][0], None, place, 0, 1, own_quarter=True)[0]
    u1 = u1[0]
    Wf["w_in"] = _move_wait("near", "w_in", flight_in, z)

    def carried(**moves):
        def arg(n, k, a):
            if k == "pass":
                return a[0] if a else (0, 1, 2)
            return span(n, *a) if a else None

        return _gather_moves([(n, Wf[n], [(k, arg(n, k, a)) for k, *a in ms]) for n, ms in moves.items()]), list(moves)

    def land(names, outs):
        Wf.update(zip(names, outs[-len(names):]))
        return outs[:-len(names)]

    side, names = carried(w_in=[("pass", (0, 1))], w_out=[("near", 0, 8)])
    land(names, _run_side("pass_w_in_near", side))
    flight_in = _move_start("far", "w_in", Wf["w_in"])
    z = _z_part(u1, flight_in[2], z, place, 1, 2)[0]
    Wf["w_in"] = _move_wait("far", "w_in", flight_in, z)
    side, names = carried(w_in=[("pass", (2,))], w_branch_a=[("near",)], w_branch_b=[("near",)])
    land(names, _run_side("pass_w_in_far", side))
    side, names = carried(w_branch_a=[("far",)], w_branch_b=[("far",)], w_out=[("near", 8, 16)])
    (z,) = land(names, _z_part(u1, Wf["w_in"], z, place, 3, 1, side=side))
    side, names = carried(w_branch_a=[("pass",)], w_branch_b=[("pass",)], w_up=[("near", 0, 10)])
    ya, o_hg, states = land(names, _hg_fwd(z, lb_logits, hg_norm_w, side=side))
    bias_win = _bias_window(rel_bias[0])
    side, names = carried(w_out=[("far",)], w_up=[("near", 10, 16), ("far", 0, 10)], w_down=[("near", 0, 3)])
    (yb,) = land(names, _at_fwd(z, bias_win, side=side))
    side, names = carried(w_out=[("pass",)], w_up=[("far", 10, 14)])
    (pa,) = land(names, _mm("branch_a", ya, Wf["w_branch_a"], "nn", [BF16], side=side))
    side, names = carried(w_up=[("far", 14, 16)], w_down=[("near", 3, 4)])
    (pb,) = land(names, _mm("branch_b", yb, Wf["w_branch_b"], "nn", [BF16], side=side))
    side, names = carried(w_down=[("near", 4, 8)])
    (merged,) = land(names, _merge(z, pa, pb, side=side))
    add = lambda acc, res: (acc + res,)
    side, names = carried(w_up=[("pass",)], w_down=[("near", 8, 12)])
    (h1,) = land(names, _mm("out_proj", merged, Wf["w_out"], "nn", [F32], extras=[x2], epilogue=add, side=side))
    side, names = carried(w_down=[("near", 12, 14)])
    (u2,) = land(names, _rms_fwd("norm_mlp", h1, norm_mlp_w, side=side))
    relu2 = lambda acc: (acc, jnp.square(jnp.maximum(acc, 0.0)))
    side, names = carried(w_down=[("near", 14, 16), ("far", 0, 14)])
    a_pre, act = land(names, _mm("mlp_up", u2, Wf["w_up"], "nn", [F32, BF16], epilogue=relu2, side=side))
    (Wf["w_down"],) = _run_side("far_w_down", _ici_far(["w_down"], [Wf["w_down"]], rows=[span("w_down", 14, 16)]))
    (Wf["w_down"],) = _run_side("pass_w_down", _d2d_gather(["w_down"], [Wf["w_down"]]))
    h2 = _mm("mlp_down", act, Wf["w_down"], "nn", [F32], extras=[h1], epilogue=add)
    loss_part, dh2, dh2b, d_nf = _loss_head(h2, tgt, nfw)

    drelu2 = lambda acc, a: (acc * (2.0 * jnp.maximum(a, 0.0)),)
    da = _mm("d_act", dh2b, Wf["w_down"], "nt", [BF16], extras=[a_pre], epilogue=drelu2)
    G = {}
    G["w_down"] = _mm("g_w_down", act, dh2b, "tn", [BF16])
    G["w_up"] = _mm("g_w_up", u2, da, "tn", [BF16])
    T_, S_, GOT = {}, {}, {}
    du2, T_["w_down"], T_["w_up"] = _mm("d_u2", da, Wf["w_up"], "nt", [F32],
                                        side=_sib_send(["w_down", "w_up"], [G["w_down"], G["w_up"]]))
    mlp2 = ["w_down", "w_up"]
    flight_mlp = _exchange_start("mlp", [_chip_sum(n, G[n], T_[n], place) for n in mlp2])
    dh1, dh1b, d_nmlp = _rms_bwd("norm_mlp_bwd", du2, h1, norm_mlp_w, dh2, side=_after(flight_mlp[-1]))
    dmerged = _mm("d_merged", dh1b, Wf["w_out"], "nt", [F32])
    G["w_out"] = _mm("g_w_out", merged, dh1b, "tn", [BF16])
    dpa, dpb, dz_ga, dz_gb = _dmerge(dmerged, z, pa, pb)
    dya = _mm("d_ya", dpa, Wf["w_branch_a"], "nt", [F32])
    dyb = _mm("d_yb", dpb, Wf["w_branch_b"], "nt", [F32])
    G["w_branch_a"] = _mm("g_w_a", ya, dpa, "tn", [BF16])
    G["w_branch_b"] = _mm("g_w_b", yb, dpb, "tn", [BF16])
    dz_q, dz_f, dz_i, dz_g, d_lbl, d_hgw, *sent = _hg_bwd(
        z, o_hg, dya, states, lb_logits, hg_norm_w, side=_sib_send(small3, [G[n] for n in small3]))
    flight_small = _exchange_start("small", [_chip_sum(n, G[n], t, place) for n, t in zip(small3, sent)])
    dz_aq, dz_ak, dz_av, dbias_win = _at_bwd(z, dyb, bias_win, side=_after(flight_small[-1]))
    dz = jnp.concatenate([dz_q, dz_f, dz_i, dz_g, dz_aq, dz_ak, dz_av, dz_ga, dz_gb], axis=1)
    g_send = _g_w_in_half(u1, dz, place, False)
    g_keep, T_["w_in"] = _g_w_in_half(u1, dz, place, True, side=_sib_send_half(g_send))
    for names, flight in ((mlp2, flight_mlp), (small3, flight_small)):
        sums, got = _exchange_wait("_".join(names), flight, g_keep)
        S_.update(zip(names, sums))
        GOT.update(zip(names, got))
    S_["w_in"] = _chip_sum("w_in", g_keep, T_["w_in"], place, kept_rows=True)
    early = [n for n in WEIGHTS if n != "w_in"]
    H_ = {n: _piece_sum(n, S_[n], GOT[n], place) for n in early}
    flight = _exchange_start("w_in", [S_["w_in"]])
    share_early = _sib_share([H_[n] for n in early])
    share_early.reads.append(flight[-1])
    du1, *shared = _mm("d_u1", dz, Wf["w_in"], "nt", [F32], side=share_early)
    O_ = dict(zip(early, shared))
    grad_x, _, d_nmix = _rms_bwd("norm_mix_bwd", du1, x2, norm_mix_w, dh1)
    d_rel = _bias_window_grad(dbias_win)
    big_out = {}
    for name in early:
        outs = _adam_quarter(name, *big[name], H_[name], O_[name], place)
        big_out[name] = tuple(a[None] for a in outs)
    (S_["w_in"],), (got_in,) = _exchange_wait("w_in", flight, outs[1])
    H_["w_in"] = _piece_sum("w_in", S_["w_in"], got_in, place)
    (O_["w_in"],) = _run_side("share_w_in", _sib_share([H_["w_in"]]))
    outs = _adam_quarter("w_in", *big["w_in"], H_["w_in"], O_["w_in"], place)
    big_out["w_in"] = tuple(a[None] for a in outs)

    smalls = [("lb_logits", lb_logits, m_lb_logits, v_lb_logits, d_lbl),
              ("hg_norm_w", hg_norm_w, m_hg_norm_w, v_hg_norm_w, d_hgw),
              ("rel_bias", rel_bias, m_rel_bias, v_rel_bias, d_rel),
              ("norm_mix_w", norm_mix_w, m_norm_mix_w, v_norm_mix_w, d_nmix),
              ("norm_mlp_w", norm_mlp_w, m_norm_mlp_w, v_norm_mlp_w, d_nmlp),
              ("norm_final_w", norm_final_w, m_norm_final_w, v_norm_final_w, d_nf)]
    like = [s[1] for s in smalls]
    packed = _small_allreduce_adam(_pack([s[4] for s in smalls]), _pack(like), _pack([s[2] for s in smalls]),
                                   _pack([s[3] for s in smalls]), got_in)
    small_out = {s[0]: vals for s, vals in zip(smalls, zip(*[_unpack(p, like) for p in packed]))}

    loss = lax.psum(loss_part[0, 0], ("x", "y", "c"))
    order = ["w_in", "lb_logits", "hg_norm_w", "rel_bias", "w_branch_a", "w_branch_b", "w_out", "norm_mix_w",
             "norm_mlp_w", "w_up", "w_down", "norm_final_w"]
    res = {**big_out, **small_out}
    return (loss, grad_x.reshape(x.shape), *[res[n][0] for n in order], *[res[n][1] for n in order],
            *[res[n][2] for n in order], *[res[n][3] for n in order])
```

```python
import functools

import jax
import jax.numpy as jnp
from jax import lax
from jax.experimental import pallas as pl
from jax.experimental.pallas import tpu as pltpu

F32 = jnp.float32
BF16 = jnp.bfloat16
HIGHEST = lax.Precision.HIGHEST

D_MODEL = 2048
SEQ = 2048
CHUNK = 64
HG_HEADS = 8
HG_D = 128
AT_HEADS = 16
AT_DH = 64
LEFT = 8
REL_CLIP = 256
D_FF = 8192
EPS = 1e-6
ADAM_LR = 0.001
ADAM_B1 = 0.9
ADAM_B2 = 0.999
ADAM_EPS = 1e-08
ADAM_WD = 0.01
ADAM_STEP = 10

LANE = 128
NEG = -1e30
EXP_CLAMP = 80.0
VMEM_LIMIT = 48 * 1024 * 1024
MM_TM, MM_TN, MM_TK = 1024, 1024, 2816
ROW_TILE = 256
QB = 2 * CHUNK


def _hgw():
    return HG_HEADS * HG_D


def _atw():
    return AT_HEADS * AT_DH


def _cparams(sem):
    return pltpu.CompilerParams(dimension_semantics=sem, vmem_limit_bytes=VMEM_LIMIT)


def _sigmoid(x):
    return jax.nn.sigmoid(x)


def _dot(a, b, dims, precision=None):
    return lax.dot_general(a, b, (dims, ((), ())), preferred_element_type=F32, precision=precision)


def _nn(a, b, precision=None):
    return _dot(a, b, ((1,), (0,)), precision)


def _nt(a, b, precision=None):
    return _dot(a, b, ((1,), (1,)), precision)


def _tn(a, b, precision=None):
    return _dot(a, b, ((0,), (0,)), precision)


class _Side:
    def __init__(self, build, nsem, reads=(), aliased=(), fresh=()):
        self.build, self.nsem = build, nsem
        self.reads, self.aliased, self.fresh = list(reads), list(aliased), list(fresh)

    def operands(self):
        return self.reads + self.aliased

    def in_specs(self):
        return [ANY] * len(self.operands())

    def out_specs(self):
        return [ANY] * (len(self.aliased) + len(self.fresh))

    def out_shape(self):
        return [jax.ShapeDtypeStruct(a.shape, a.dtype) for a in self.aliased] + self.fresh

    def aliases(self, n_in, n_out):
        return {n_in + len(self.reads) + t: n_out + t for t in range(len(self.aliased))}

    def scratch(self):
        return [pltpu.SemaphoreType.DMA((self.nsem,)), pltpu.SemaphoreType.DMA((self.nsem,))]

    def hooks(self, in_refs, out_refs, sems, first, last):
        nr, na = len(self.reads), len(self.aliased)
        args = (in_refs[:nr], out_refs[:na], out_refs[na:], *sems)

        @pl.when(first)
        def _():
            for cp in self.build(*args):
                cp.start()

        @pl.when(last)
        def _():
            for cp in self.build(*args):
                cp.wait()


def _after(*tokens):
    return _Side(lambda *args: [], 1, reads=tokens)


def _side_parts(side):
    if side is None:
        return [], [], [], [], lambda n_in, n_out: {}, []
    return side.operands(), side.in_specs(), side.out_specs(), side.out_shape(), side.aliases, side.scratch()


def _call_with_side(body, name, grid, in_specs, out_specs, out_shape, scratch, sem, operands, side, n_prefetch=0,
                    aliases=None, borrow=None):
    _, _, s_out, s_shape, _, s_scr = _side_parts(side)
    n_in, n_out = n_prefetch + len(in_specs), len(out_specs)
    borrow = borrow or {}
    s_ops, s_alias = [], {}
    if side is not None:
        keep = [t for t in range(len(side.aliased)) if t not in borrow]
        s_ops = side.reads + [side.aliased[t] for t in keep]
        s_alias = {n_in + len(side.reads) + pos: n_out + t for pos, t in enumerate(keep)}
        s_alias.update({n_prefetch + i: n_out + t for t, i in borrow.items()})
    s_in = [ANY] * len(s_ops)
    n_sin, n_sout = len(s_ops), len(s_out)

    def wrapped(*refs):
        a, b, c = n_in + n_sin, n_in + n_sin + n_out, n_in + n_sin + n_out + n_sout
        ids = [pl.program_id(d) for d in range(len(grid))]
        first = functools.reduce(lambda p, q: p & q, [i == 0 for i in ids])
        last = functools.reduce(lambda p, q: p & q, [i == g - 1 for i, g in zip(ids, grid)])
        side.hooks(refs[n_in:a], refs[b:c], refs[-2:], first, last)
        body(*refs[:n_in], *refs[a:b], *refs[c:-2])

    spec = dict(grid=grid, in_specs=in_specs + s_in, out_specs=out_specs + s_out, scratch_shapes=scratch + s_scr)
    if n_prefetch:
        spec = dict(grid_spec=pltpu.PrefetchScalarGridSpec(num_scalar_prefetch=n_prefetch, **spec))
    return pl.pallas_call(
        body if side is None else wrapped, name=name, out_shape=out_shape + s_shape,
        input_output_aliases={**s_alias, **{n_prefetch + i: o for i, o in (aliases or {}).items()}},
        compiler_params=_cparams(sem if side is None else ("arbitrary",) * len(grid)), **spec,
    )(*operands, *s_ops)


def _mm_tk(K):
    if K <= MM_TK:
        return K
    return max(t for t in range(LANE, MM_TK + 1, LANE) if K % t == 0)


def _mm(name, a, b, mode, out_dtypes, extras=(), epilogue=None, side=None):
    if mode == "nn":
        (M, K), (K2, N) = a.shape, b.shape
    elif mode == "nt":
        (M, K), (N, K2) = a.shape, b.shape
    else:
        (K, M), (K2, N) = a.shape, b.shape
    assert K == K2, (name, a.shape, b.shape)
    tm, tn, tk = min(MM_TM, M), min(MM_TN, N), _mm_tk(K)
    assert M % tm == 0 and N % tn == 0 and K % tk == 0, (name, M, N, K)
    ni, nj, nk = M // tm, N // tn, K // tk
    ne, no = len(extras), len(out_dtypes)
    if epilogue is None:
        epilogue = lambda acc: (acc,)
    s_ops, s_in, s_out, s_shape, s_alias, s_scr = _side_parts(side)
    n_in, n_sin, n_sout = 2 + ne, len(s_ops), len(s_out)

    def body(*refs):
        a_ref, b_ref = refs[:2]
        extra_refs = refs[2:n_in]
        out_refs = refs[n_in + n_sin:n_in + n_sin + no]
        rest = refs[n_in + n_sin + no + n_sout:]
        i, j, k = pl.program_id(0), pl.program_id(1), pl.program_id(2)
        if side is not None:
            side.hooks(refs[n_in:n_in + n_sin], refs[n_in + n_sin + no:n_in + n_sin + no + n_sout], rest[-2:],
                       (i == 0) & (j == 0) & (k == 0), (i == ni - 1) & (j == nj - 1) & (k == nk - 1))
        av, bv = a_ref[...].astype(BF16), b_ref[...].astype(BF16)
        prod = _nn(av, bv) if mode == "nn" else _nt(av, bv) if mode == "nt" else _tn(av, bv)

        def finish(acc):
            res = epilogue(acc, *[e[...] for e in extra_refs])
            for o_ref, r in zip(out_refs, res):
                o_ref[...] = r.astype(o_ref.dtype)

        if nk == 1:
            finish(prod)
        else:
            acc_ref = rest[0]

            @pl.when(k == 0)
            def _():
                acc_ref[...] = prod

            @pl.when((k > 0) & (k < nk - 1))
            def _():
                acc_ref[...] += prod

            @pl.when(k == nk - 1)
            def _():
                finish(acc_ref[...] + prod)

    if mode == "nn":
        a_spec = pl.BlockSpec((tm, tk), lambda i, j, k: (i, k))
        b_spec = pl.BlockSpec((tk, tn), lambda i, j, k: (k, j))
    elif mode == "nt":
        a_spec = pl.BlockSpec((tm, tk), lambda i, j, k: (i, k))
        b_spec = pl.BlockSpec((tn, tk), lambda i, j, k: (j, k))
    else:
        a_spec = pl.BlockSpec((tk, tm), lambda i, j, k: (k, i))
        b_spec = pl.BlockSpec((tk, tn), lambda i, j, k: (k, j))
    o_spec = pl.BlockSpec((tm, tn), lambda i, j, k: (i, j))
    sem = ("arbitrary",) * 3 if side is not None else ("parallel", "parallel", "arbitrary")
    outs = pl.pallas_call(
        body, name=name,
        grid=(ni, nj, nk),
        in_specs=[a_spec, b_spec] + [o_spec] * ne + s_in,
        out_specs=[o_spec] * no + s_out,
        out_shape=[jax.ShapeDtypeStruct((M, N), dt) for dt in out_dtypes] + s_shape,
        input_output_aliases=s_alias(n_in, no),
        scratch_shapes=([pltpu.VMEM((tm, tn), F32)] if nk > 1 else []) + s_scr,
        compiler_params=_cparams(sem),
    )(a, b, *extras, *s_ops)
    return outs[0] if len(outs) == 1 else outs


def _row_spec(tr, d):
    return pl.BlockSpec((tr, d), lambda i: (i, 0))


def _vec_spec(d):
    return pl.BlockSpec((1, d), lambda i: (0, 0))


def _rms_fwd(name, x, w, side=None):
    T, D = x.shape
    tr = min(ROW_TILE, T)

    def body(x_ref, w_ref, o_ref):
        xf = x_ref[...]
        r = lax.rsqrt(jnp.mean(xf * xf, axis=-1, keepdims=True) + EPS)
        o_ref[...] = (xf * r * w_ref[...]).astype(BF16)

    outs = _call_with_side(body, name, (T // tr,), [_row_spec(tr, D), _vec_spec(D)], [_row_spec(tr, D)],
                           [jax.ShapeDtypeStruct((T, D), BF16)], [], ("parallel",), (x, w), side)
    return outs[0] if side is None else outs


def _rms_bwd(name, dy, h, w, dres, side=None):
    T, D = h.shape
    tr = min(ROW_TILE, T)

    def body(dy_ref, h_ref, w_ref, dres_ref, dh_ref, dhb_ref, dw_ref):
        @pl.when(pl.program_id(0) == 0)
        def _():
            dw_ref[...] = jnp.zeros_like(dw_ref)

        hf, dyv = h_ref[...], dy_ref[...]
        r = lax.rsqrt(jnp.mean(hf * hf, axis=-1, keepdims=True) + EPS)
        xhat = hf * r
        dw_ref[...] += jnp.sum(dyv * xhat, axis=0, keepdims=True)
        dxh = dyv * w_ref[...]
        dh = dres_ref[...] + r * (dxh - xhat * jnp.mean(dxh * xhat, axis=-1, keepdims=True))
        dh_ref[...] = dh
        dhb_ref[...] = dh.astype(BF16)

    return _call_with_side(
        body, name, (T // tr,),
        [_row_spec(tr, D), _row_spec(tr, D), _vec_spec(D), _row_spec(tr, D)],
        [_row_spec(tr, D), _row_spec(tr, D), _vec_spec(D)],
        [jax.ShapeDtypeStruct((T, D), F32), jax.ShapeDtypeStruct((T, D), BF16), jax.ShapeDtypeStruct((1, D), F32)],
        [], ("arbitrary",), (dy, h, w, dres), side)


def _loss_head(h2, target, w):
    T, D = h2.shape
    tr = min(ROW_TILE, T)

    def body(h_ref, t_ref, w_ref, loss_ref, dh_ref, dhb_ref, dw_ref):
        @pl.when(pl.program_id(0) == 0)
        def _():
            dw_ref[...] = jnp.zeros_like(dw_ref)
            loss_ref[...] = jnp.zeros_like(loss_ref)

        hf, wv = h_ref[...], w_ref[...]
        r = lax.rsqrt(jnp.mean(hf * hf, axis=-1, keepdims=True) + EPS)
        xhat = hf * r
        diff = xhat * wv - t_ref[...]
        loss_ref[...] += 0.5 * jnp.sum(jnp.mean(diff * diff, axis=-1, keepdims=True))
        dyv = diff * (1.0 / D)
        dw_ref[...] += jnp.sum(dyv * xhat, axis=0, keepdims=True)
        dxh = dyv * wv
        dh = r * (dxh - xhat * jnp.mean(dxh * xhat, axis=-1, keepdims=True))
        dh_ref[...] = dh
        dhb_ref[...] = dh.astype(BF16)

    return pl.pallas_call(
        body, name="loss_head", grid=(T // tr,),
        in_specs=[_row_spec(tr, D), _row_spec(tr, D), _vec_spec(D)],
        out_specs=[_vec_spec(LANE), _row_spec(tr, D), _row_spec(tr, D), _vec_spec(D)],
        out_shape=[jax.ShapeDtypeStruct((1, LANE), F32), jax.ShapeDtypeStruct((T, D), F32),
                   jax.ShapeDtypeStruct((T, D), BF16), jax.ShapeDtypeStruct((1, D), F32)],
        compiler_params=_cparams(("arbitrary",)),
    )(h2, target, w)


def _gate_tiles(T, D):
    goff = 4 * _hgw() + 3 * _atw()
    tc = min(1024, D)
    assert goff % tc == 0 and D % tc == 0
    return min(ROW_TILE, T), tc, goff // tc, D // tc


def _merge(z, pa, pb, side=None):
    T, D = pa.shape
    tr, tc, g0, nd = _gate_tiles(T, D)

    def body(ga_ref, gb_ref, pa_ref, pb_ref, o_ref):
        o_ref[...] = (_sigmoid(ga_ref[...]) * pa_ref[...] + _sigmoid(gb_ref[...]) * pb_ref[...]).astype(BF16)

    t = pl.BlockSpec((tr, tc), lambda i, j: (i, j))
    outs = _call_with_side(
        body, "merge", (T // tr, nd),
        [pl.BlockSpec((tr, tc), lambda i, j: (i, g0 + j)), pl.BlockSpec((tr, tc), lambda i, j: (i, g0 + nd + j)), t, t],
        [t], [jax.ShapeDtypeStruct((T, D), BF16)], [], ("parallel", "parallel"), (z, z, pa, pb), side)
    return outs[0] if side is None else outs


def _dmerge(dm, z, pa, pb):
    T, D = pa.shape
    tr, tc, g0, nd = _gate_tiles(T, D)

    def body(dm_ref, ga_ref, gb_ref, pa_ref, pb_ref, dpa_ref, dpb_ref, dga_ref, dgb_ref):
        dmv = dm_ref[...]
        sa, sb = _sigmoid(ga_ref[...]), _sigmoid(gb_ref[...])
        dpa_ref[...] = (dmv * sa).astype(BF16)
        dpb_ref[...] = (dmv * sb).astype(BF16)
        dga_ref[...] = (dmv * pa_ref[...] * sa * (1.0 - sa)).astype(BF16)
        dgb_ref[...] = (dmv * pb_ref[...] * sb * (1.0 - sb)).astype(BF16)

    t = pl.BlockSpec((tr, tc), lambda i, j: (i, j))
    return pl.pallas_call(
        body, name="dmerge", grid=(T // tr, nd),
        in_specs=[t, pl.BlockSpec((tr, tc), lambda i, j: (i, g0 + j)),
                  pl.BlockSpec((tr, tc), lambda i, j: (i, g0 + nd + j)), t, t],
        out_specs=[t, t, t, t],
        out_shape=[jax.ShapeDtypeStruct((T, D), BF16)] * 4,
        compiler_params=_cparams(("parallel", "parallel")),
    )(dm, z, z, pa, pb)


def _hg_gates(xq, xf, lb):
    f = _sigmoid(xf)
    g = lb + (1.0 - lb) * f
    sq = _sigmoid(xq)
    return f, g, jnp.log(g), 1.0 - g, sq, xq * sq * (HG_D ** -0.5)


def _split2(x):
    hi = x.astype(BF16)
    return hi, (x - hi.astype(F32)).astype(BF16)


def _tri_sum(tri, x):
    hi, rest = x.astype(BF16), x - x.astype(BF16).astype(F32)
    mid, lo = _split2(rest)
    return _nn(tri, lo) + _nn(tri, mid) + _nn(tri, hi)


def _hg_decays(lg, tri_incl, rowi):
    b = _tri_sum(tri_incl, lg)
    b_last = jnp.sum(lg, axis=0, keepdims=True)
    b_mid = jnp.sum(jnp.where(rowi <= CHUNK // 2, lg, 0.0), axis=0, keepdims=True)
    return b, b_last, b_mid


HG_GROUP = 2


def _hg_in_specs(T):
    ng = HG_HEADS // HG_GROUP
    return [pl.BlockSpec((T, HG_GROUP * HG_D), lambda h, s=s: (0, s * ng + h)) for s in range(4)]


def _hg_fwd(z, lb_logits, hgw, side=None):
    T = z.shape[0]
    H, d, C, G = HG_HEADS, HG_D, CHUNK, HG_GROUP
    nc = T // C

    def body(hq_ref, hf_ref, hi_ref, hg_ref, lbl_ref, w_ref, ya_ref, o_ref, s_ref):
        lb_all = 1.0 / (1.0 + jnp.exp(lbl_ref[1:2, :] - lbl_ref[0:1, :]))
        wv = w_ref[...]
        row = lax.broadcasted_iota(jnp.int32, (C, C), 0)
        col = lax.broadcasted_iota(jnp.int32, (C, C), 1)
        tril = col <= row
        tri_incl = tril.astype(BF16)
        rowi = lax.broadcasted_iota(jnp.int32, (C, G * d), 0)
        lanes = [slice(hh * d, (hh + 1) * d) for hh in range(G)]
        per_head = lambda fn: jnp.concatenate([fn(hh, sl) for hh, sl in enumerate(lanes)], axis=1)
        wv_all = jnp.tile(wv, (1, G))

        def chunk(c, states):
            rows = pl.ds(pl.multiple_of(c * C, C), C)
            xq, xf, v, xg = hq_ref[rows, :], hf_ref[rows, :], hi_ref[rows, :], hg_ref[rows, :]
            _, _, lg, kk, _, q = _hg_gates(xq, xf, lb_all)
            b, b_last, b_mid = _hg_decays(lg, tri_incl, rowi)
            vb, qe = v.astype(BF16), (q * jnp.exp(b)).astype(BF16)
            qt = (q * jnp.exp(b - b_mid)).astype(BF16)
            kt = (kk * jnp.exp(jnp.minimum(b_mid - b, EXP_CLAMP))).astype(BF16)
            kd, e_last = (kk * jnp.exp(b_last - b)).astype(BF16), jnp.exp(b_last)
            for hh, st in enumerate(states):
                s_ref[hh, c] = st
            o = per_head(lambda hh, sl: _nt(qe[:, sl], states[hh].astype(BF16)))
            a = [jnp.where(tril, _nt(qt[:, sl], kt[:, sl]), 0.0).astype(BF16) for sl in lanes]
            o = o + per_head(lambda hh, sl: _nn(a[hh], vb[:, sl]))
            o_ref[rows, :] = o
            r = per_head(lambda hh, sl: jnp.broadcast_to(
                lax.rsqrt(jnp.mean(o[:, sl] * o[:, sl], axis=-1, keepdims=True) + EPS), (C, d)))
            ya_ref[rows, :] = (o * r * wv_all * (xg * _sigmoid(xg))).astype(BF16)
            return tuple(st * e_last[:, sl] + _tn(vb[:, sl], kd[:, sl]) for st, sl in zip(states, lanes))

        lax.fori_loop(0, nc, chunk, tuple(jnp.zeros((d, d), F32) for _ in range(G)))

    heads = pl.BlockSpec((T, G * d), lambda h: (0, h))
    return _call_with_side(
        body, "hg_fwd", (H // G,),
        _hg_in_specs(T) + [pl.BlockSpec((2, G * d), lambda h: (0, h)), pl.BlockSpec((1, d), lambda h: (0, 0))],
        [heads, heads, pl.BlockSpec((G, nc, d, d), lambda h: (h, 0, 0, 0))],
        [jax.ShapeDtypeStruct((T, H * d), BF16), jax.ShapeDtypeStruct((T, H * d), F32),
         jax.ShapeDtypeStruct((H, nc, d, d), F32)],
        [], ("parallel",), (z, z, z, z, lb_logits, hgw), side)


def _hg_bwd(z, o, dya, states, lb_logits, hgw, side=None):
    T = z.shape[0]
    H, d, C, G = HG_HEADS, HG_D, CHUNK, HG_GROUP
    nc = T // C
    scale = HG_D ** -0.5

    def body(hq_ref, hf_ref, hi_ref, hg_ref, o_ref, dy_ref, s_ref, lbl_ref, w_ref,
             dq_ref, df_ref, di_ref, dg_ref, dlbl_ref, dw_ref, acc_ref):
        lb_all = 1.0 / (1.0 + jnp.exp(lbl_ref[1:2, :] - lbl_ref[0:1, :]))
        wv = w_ref[...]
        row = lax.broadcasted_iota(jnp.int32, (C, C), 0)
        col = lax.broadcasted_iota(jnp.int32, (C, C), 1)
        tril = col <= row
        tri_incl = tril.astype(BF16)
        triu_incl = (col >= row).astype(BF16)
        rowi = lax.broadcasted_iota(jnp.int32, (C, G * d), 0)
        lanes = [slice(hh * d, (hh + 1) * d) for hh in range(G)]
        per_head = lambda fn: jnp.concatenate([fn(hh, sl) for hh, sl in enumerate(lanes)], axis=1)
        head_mean = lambda x: per_head(
            lambda hh, sl: jnp.broadcast_to(jnp.mean(x[:, sl], axis=-1, keepdims=True), (C, d)))
        wv_all = jnp.tile(wv, (1, G))
        lb = lb_all
        acc_ref[...] = jnp.zeros_like(acc_ref)

        @pl.when(pl.program_id(0) == 0)
        def _():
            dw_ref[...] = jnp.zeros_like(dw_ref)

        def chunk(i, carry):
            dsts, tail = carry
            c = nc - 1 - i
            rows = pl.ds(pl.multiple_of(c * C, C), C)
            xq, xf, v, xg = hq_ref[rows, :], hf_ref[rows, :], hi_ref[rows, :], hg_ref[rows, :]
            f, g, lg, kk, sq, q = _hg_gates(xq, xf, lb)
            b, b_last, b_mid = _hg_decays(lg, tri_incl, rowi)
            e_b, e_qm, e_km = jnp.exp(b), jnp.exp(b - b_mid), jnp.exp(jnp.minimum(b_mid - b, EXP_CLAMP))
            e_kl, e_last = jnp.exp(b_last - b), jnp.exp(b_last)
            ov, dy = o_ref[rows, :], dy_ref[rows, :]
            r = lax.rsqrt(head_mean(ov * ov) + EPS)
            xhat = ov * r
            sg = _sigmoid(xg)
            dxg = dy * xhat * wv_all * (sg * (1.0 + xg * (1.0 - sg)))
            dyn = dy * (xg * sg)
            acc_ref[0:1, :] += jnp.sum(dyn * xhat, axis=0, keepdims=True)
            dxh = dyn * wv_all
            dof = r * (dxh - xhat * head_mean(dxh * xhat))
            do, vb = dof.astype(BF16), v.astype(BF16)
            qe, kd, qt, kt = (q * e_b).astype(BF16), (kk * e_kl).astype(BF16), (q * e_qm).astype(BF16), (kk * e_km).astype(BF16)
            pm = [jnp.where(tril, _nt(do[:, sl], vb[:, sl]), 0.0).astype(BF16) for sl in lanes]
            am = [jnp.where(tril, _nt(qt[:, sl], kt[:, sl]), 0.0).astype(BF16) for sl in lanes]
            st = [_split2(s_ref[hh, c]) for hh in range(G)]
            ds = [_split2(x) for x in dsts]
            dq_state = per_head(lambda hh, sl: _nn(do[:, sl], st[hh][1]) + _nn(do[:, sl], st[hh][0]))
            dk_state = per_head(lambda hh, sl: _nn(vb[:, sl], ds[hh][1]) + _nn(vb[:, sl], ds[hh][0]))
            dq_intra = per_head(lambda hh, sl: _nn(pm[hh], kt[:, sl]))
            dk_intra = per_head(lambda hh, sl: _tn(pm[hh], qt[:, sl]))
            dv = per_head(lambda hh, sl: _tn(am[hh], do[:, sl]) + _nt(kd[:, sl], ds[hh][0]))
            new_dsts = tuple(x * e_last[:, sl] + _tn(do[:, sl], qe[:, sl]) for x, sl in zip(dsts, lanes))
            dq = dq_state * e_b + dq_intra * e_qm
            dk = dk_intra * e_km + dk_state * e_kl
            db = (qe.astype(F32) * dq_state + qt.astype(F32) * dq_intra
                  - kt.astype(F32) * dk_intra - kd.astype(F32) * dk_state)
            dlg = _tri_sum(triu_incl, db) + tail
            dgate = dlg / g - dk
            acc_ref[1:2, :] += jnp.sum(dgate * (1.0 - f), axis=0, keepdims=True)
            dq_ref[rows, :] = (dq * scale * (sq * (1.0 + xq * (1.0 - sq)))).astype(BF16)
            df_ref[rows, :] = (dgate * (1.0 - lb) * f * (1.0 - f)).astype(BF16)
            di_ref[rows, :] = dv.astype(BF16)
            dg_ref[rows, :] = dxg.astype(BF16)
            return new_dsts, tail + jnp.sum(db, axis=0, keepdims=True)

        lax.fori_loop(0, nc, chunk, (tuple(jnp.zeros((d, d), F32) for _ in range(G)), jnp.zeros((1, G * d), F32)))
        dw_ref[...] += functools.reduce(lambda p, q: p + q, [acc_ref[0:1, sl] for sl in lanes])
        dl0 = acc_ref[1:2, :] * lb_all * (1.0 - lb_all)
        dlbl_ref[0:1, :] = dl0
        dlbl_ref[1:2, :] = -dl0

    heads = pl.BlockSpec((T, G * d), lambda h: (0, h))
    logits = pl.BlockSpec((2, G * d), lambda h: (0, h))
    return _call_with_side(
        body, "hg_bwd", (H // G,),
        _hg_in_specs(T) + [heads, heads, pl.BlockSpec((G, nc, d, d), lambda h: (h, 0, 0, 0)), logits,
                           pl.BlockSpec((1, d), lambda h: (0, 0))],
        [heads, heads, heads, heads, logits, pl.BlockSpec((1, d), lambda h: (0, 0))],
        [jax.ShapeDtypeStruct((T, H * d), BF16)] * 4 + [jax.ShapeDtypeStruct((2, H * d), F32),
                                                        jax.ShapeDtypeStruct((1, d), F32)],
        [pltpu.VMEM((8, G * d), F32)], ("arbitrary",), (z, z, z, z, o, dya, states, lb_logits, hgw), side)


def _at_dims():
    pad = LEFT * CHUNK
    return pad, QB + pad, AT_HEADS * AT_DH // LANE, 4 * _hgw() // LANE


def _rel_of_period():
    pad, W, _, _ = _at_dims()
    n = jnp.arange(QB + W)
    return jnp.clip(pad - jnp.where(n < W, n, n - (QB + W)), -REL_CLIP, REL_CLIP) + REL_CLIP


def _bias_window(rel_bias):
    pad, W, _, _ = _at_dims()
    H, P = rel_bias.shape[0], QB + W
    per = rel_bias[:, _rel_of_period()]
    win = jnp.tile(per, (1, QB))[:, :QB * (P - 1)].reshape(H, QB, P - 1)[:, :, :W]
    t = jnp.arange(QB)[:, None]
    j = jnp.arange(W)[None, :]
    ok = (j // CHUNK >= t // CHUNK) & (j // CHUNK <= t // CHUNK + LEFT)
    return jnp.where(ok[None], win, NEG)


def _bias_window_grad(dbw):
    pad, W, _, _ = _at_dims()
    H, P = dbw.shape[0], QB + W
    flat = jnp.pad(dbw, ((0, 0), (0, 0), (0, P - 1 - W))).reshape(H, QB * (P - 1))
    per = jnp.pad(flat, ((0, 0), (0, QB))).reshape(H, QB, P).sum(axis=1)
    onehot = _rel_of_period()[:, None] == jnp.arange(2 * REL_CLIP + 1)[None, :]
    return jnp.dot(per, onehot.astype(F32), precision=HIGHEST)


def _at_stack(x):
    first = lax.broadcasted_iota(jnp.int32, x.shape, 1) < AT_DH
    return jnp.concatenate([jnp.where(first, x, 0.0), jnp.where(first, 0.0, x)], axis=0).astype(BF16)


def _at_unstack(x):
    first = lax.broadcasted_iota(jnp.int32, (QB, LANE), 1) < AT_DH
    return jnp.where(first, x[:QB], x[QB:])


def _at_softmax(qs, kw, bias_ref, qi):
    pad, W, _, _ = _at_dims()
    s = _nt(qs, kw) + bias_ref[...].reshape(2 * QB, W)
    valid = lax.broadcasted_iota(jnp.int32, (2 * QB, W), 1) + qi * QB >= pad
    s = jnp.where(valid, s, NEG)
    e = jnp.exp(s - jnp.max(s, axis=-1, keepdims=True))
    return e * (1.0 / jnp.sum(e, axis=-1, keepdims=True))


def _at_fwd(z, bias_win, side=None):
    T = z.shape[0]
    pad, W, HP, c0 = _at_dims()
    nq = T // QB

    def body(q_ref, k_ref, v_ref, bias_ref, o_ref, kpad, vpad):
        qi = pl.program_id(1)

        @pl.when(qi == 0)
        def _():
            kpad[0:pad, :] = jnp.zeros((pad, LANE), BF16)
            vpad[0:pad, :] = jnp.zeros((pad, LANE), BF16)
            kpad[pad:, :] = k_ref[...].astype(BF16)
            vpad[pad:, :] = v_ref[...].astype(BF16)

        win = pl.ds(pl.multiple_of(qi * QB, QB), W)
        kw, vw = kpad[win, :], vpad[win, :]
        p = _at_softmax(_at_stack(q_ref[...] * (AT_DH ** -0.5)), kw, bias_ref, qi)
        o_ref[...] = _at_unstack(_nn(p.astype(BF16), vw)).astype(BF16)

    full = lambda s: pl.BlockSpec((T, LANE), lambda hp, qi, s=s: (0, c0 + s * HP + hp))
    return _call_with_side(
        body, "at_fwd", (HP, nq),
        [pl.BlockSpec((QB, LANE), lambda hp, qi: (qi, c0 + hp)), full(1), full(2),
         pl.BlockSpec((2, QB, W), lambda hp, qi: (hp, 0, 0))],
        [pl.BlockSpec((QB, LANE), lambda hp, qi: (qi, hp))],
        [jax.ShapeDtypeStruct((T, HP * LANE), BF16)],
        [pltpu.VMEM((T + pad, LANE), BF16)] * 2, ("parallel", "arbitrary"), (z, z, z, bias_win), side)


def _at_bwd(z, dyb, bias_win, side=None):
    T = z.shape[0]
    pad, W, HP, c0 = _at_dims()
    nq = T // QB
    scale = AT_DH ** -0.5

    def body(q_ref, k_ref, v_ref, do_ref, bias_ref, dq_ref, dk_ref, dv_ref, dbias_ref, kpad, vpad, dkpad, dvpad):
        qi = pl.program_id(1)

        @pl.when(qi == 0)
        def _():
            kpad[0:pad, :] = jnp.zeros((pad, LANE), BF16)
            vpad[0:pad, :] = jnp.zeros((pad, LANE), BF16)
            kpad[pad:, :] = k_ref[...].astype(BF16)
            vpad[pad:, :] = v_ref[...].astype(BF16)
            dkpad[...] = jnp.zeros_like(dkpad)
            dvpad[...] = jnp.zeros_like(dvpad)
            dbias_ref[...] = jnp.zeros_like(dbias_ref)

        win = pl.ds(pl.multiple_of(qi * QB, QB), W)
        kw, vw = kpad[win, :], vpad[win, :]
        qs, dos = _at_stack(q_ref[...] * scale), _at_stack(do_ref[...])
        p = _at_softmax(qs, kw, bias_ref, qi)
        dp = _nt(dos, vw)
        ds = p * (dp - jnp.sum(p * dp, axis=-1, keepdims=True))
        dbias_ref[...] += ds.reshape(2, QB, W)
        dsb = ds.astype(BF16)
        dq_ref[...] = (_at_unstack(_nn(dsb, kw)) * scale).astype(BF16)
        dkpad[win, :] += _tn(dsb, qs)
        dvpad[win, :] += _tn(p.astype(BF16), dos)

        @pl.when(qi == nq - 1)
        def _():
            dk_ref[...] = dkpad[pad:, :].astype(BF16)
            dv_ref[...] = dvpad[pad:, :].astype(BF16)

    full = lambda s: pl.BlockSpec((T, LANE), lambda hp, qi, s=s: (0, c0 + s * HP + hp))
    blk = pl.BlockSpec((QB, LANE), lambda hp, qi: (qi, hp))
    col = pl.BlockSpec((T, LANE), lambda hp, qi: (0, hp))
    bw = pl.BlockSpec((2, QB, W), lambda hp, qi: (hp, 0, 0))
    return _call_with_side(
        body, "at_bwd", (HP, nq),
        [pl.BlockSpec((QB, LANE), lambda hp, qi: (qi, c0 + hp)), full(1), full(2), blk, bw],
        [blk, col, col, bw],
        [jax.ShapeDtypeStruct((T, HP * LANE), BF16)] * 3 + [jax.ShapeDtypeStruct(bias_win.shape, F32)],
        [pltpu.VMEM((T + pad, LANE), BF16)] * 2 + [pltpu.VMEM((T + pad, LANE), F32)] * 2,
        ("parallel", "arbitrary"), (z, z, z, dyb, bias_win), side)


def _piece_tiles(name, full_shape):
    pr, pc = _piece_shape(name, full_shape)
    tr = min(ROW_TILE, pr)
    assert pr % tr == 0
    nt = pr // tr
    if name in ROW_SHARDED:
        return tr, nt, lambda q, half, i: ((2 * q + half) * nt + i, 0)
    return tr, nt, lambda q, half, i: (half * nt + i, q)


def _cast_into_full(name, wq, place, also_alone=False):
    full = _full_shape(name, wq.shape)
    pc = wq.shape[1]
    tr, nt, at = _piece_tiles(name, full)

    def body(place_ref, w_ref, *o_refs):
        for o_ref in o_refs:
            o_ref[...] = w_ref[...].astype(BF16)

    quarter = pl.BlockSpec((tr, pc), lambda h, i, s: (h * nt + i, 0))
    outs = _call_with_side(
        body, "cast_" + name, (2, nt), [quarter],
        [pl.BlockSpec((tr, pc), lambda h, i, s: at(s[0], h, i))] + [quarter] * also_alone,
        [jax.ShapeDtypeStruct(full, BF16)] + [jax.ShapeDtypeStruct(wq.shape, BF16)] * also_alone,
        [], ("parallel", "parallel"), (place, wq), None, n_prefetch=1)
    return tuple(outs) if also_alone else outs[0]


def _g_w_in_half(u1, dz, place, own, side=None):
    T, K = u1.shape
    N = dz.shape[1]
    hk, tn = K // 2, min(MM_TN, N)
    half = (lambda s: s[1]) if own else (lambda s: 1 - s[1])

    def body(place_ref, a_ref, b_ref, o_ref):
        o_ref[...] = _tn(a_ref[...], b_ref[...]).astype(BF16)

    outs = _call_with_side(
        body, "g_w_in_keep" if own else "g_w_in_send", (N // tn,),
        [pl.BlockSpec((T, hk), lambda j, s: (0, half(s))), pl.BlockSpec((T, tn), lambda j, s: (0, j))],
        [pl.BlockSpec((hk, tn), lambda j, s: (0, j))], [jax.ShapeDtypeStruct((hk, N), BF16)],
        [], ("parallel",), (place, u1, dz), side, n_prefetch=1)
    return outs[0] if side is None else outs


def _chip_sum(name, grad, theirs, place, kept_rows=False):
    pr, pc = theirs.shape[1:]
    tr, nt, at = _piece_tiles(name, (2 * grad.shape[0], grad.shape[1]) if kept_rows else grad.shape)
    if kept_rows:
        at = lambda q, half, i: (i, q)

    def body(place_ref, g_ref, t_ref, o_ref):
        o_ref[...] = (g_ref[...].astype(F32) + t_ref[...].astype(F32)).astype(BF16)

    piece = pl.BlockSpec((None, tr, pc), lambda q, i, s: (q, i, 0))
    return pl.pallas_call(
        body, name="chip_sum_" + name,
        grid_spec=pltpu.PrefetchScalarGridSpec(
            num_scalar_prefetch=1, grid=(4, nt),
            in_specs=[pl.BlockSpec((tr, pc), lambda q, i, s: at(q, s[1], i)), piece], out_specs=piece),
        out_shape=jax.ShapeDtypeStruct(theirs.shape, BF16),
        compiler_params=_cparams(("parallel", "parallel")),
    )(place, grad, theirs)


def _piece_sum(name, chip_sums, got, place):
    pr, pc = chip_sums.shape[1:]
    tr = min(ROW_TILE, pr)

    def body(place_ref, own_ref, got_ref, o_ref):
        o_ref[...] = (own_ref[...].astype(F32) + got_ref[0].astype(F32) + got_ref[1].astype(F32)
                      + got_ref[2].astype(F32))

    return pl.pallas_call(
        body, name="piece_sum_" + name,
        grid_spec=pltpu.PrefetchScalarGridSpec(
            num_scalar_prefetch=1, grid=(pr // tr,),
            in_specs=[pl.BlockSpec((None, tr, pc), lambda i, s: (s[0], i, 0)),
                      pl.BlockSpec((3, tr, pc), lambda i, s: (0, i, 0))],
            out_specs=pl.BlockSpec((tr, pc), lambda i, s: (i, 0))),
        out_shape=jax.ShapeDtypeStruct((pr, pc), F32),
        compiler_params=_cparams(("parallel",)),
    )(place, chip_sums, got)


def _adam_quarter(name, w, m, v, g_mine, g_sib, place, side=None):
    pr, pc = g_mine.shape
    tr = min(ROW_TILE // 2, pr)
    nt = pr // tr

    def body(place_ref, w_ref, m_ref, v_ref, gm_ref, gs_ref, go_ref, d_ref, mo_ref, vo_ref):
        g = jnp.where(pl.program_id(0) == place_ref[1], gm_ref[...], gs_ref[...])
        delta, mn, vn = _adam_math(w_ref[...], g, m_ref[...], v_ref[...])
        go_ref[...] = g
        d_ref[...] = delta
        mo_ref[...] = mn
        vo_ref[...] = vn

    quarter = pl.BlockSpec((tr, pc), lambda h, i, s: (h * nt + i, 0))
    mine = pl.BlockSpec((tr, pc), lambda h, i, s: (jnp.where(h == s[1], i, 0), 0))
    sib = pl.BlockSpec((tr, pc), lambda h, i, s: (jnp.where(h == s[1], 0, i), 0))
    return _call_with_side(
        body, "adam_" + name, (2, nt), [quarter, quarter, quarter, mine, sib], [quarter] * 4,
        [jax.ShapeDtypeStruct(w.shape, F32)] * 4, [], ("parallel", "parallel"),
        (place, w, m, v, g_mine, g_sib), side, n_prefetch=1)


def _adam_math(w, g, m, v):
    m = ADAM_B1 * m + (1.0 - ADAM_B1) * g
    v = ADAM_B2 * v + (1.0 - ADAM_B2) * (g * g)
    m_hat = m / (1.0 - ADAM_B1 ** ADAM_STEP)
    v_hat = v / (1.0 - ADAM_B2 ** ADAM_STEP)
    return -ADAM_LR * (m_hat / (jnp.sqrt(v_hat) + ADAM_EPS) + ADAM_WD * w), m, v


WEIGHTS = ("w_in", "w_branch_a", "w_branch_b", "w_out", "w_up", "w_down")
ROW_SHARDED = ("w_out", "w_down")
ANY = pl.BlockSpec(memory_space=pl.ANY)
MESH = pl.DeviceIdType.MESH


def _place():
    x, y, c = lax.axis_index("x"), lax.axis_index("y"), lax.axis_index("c")
    chips = [(1 - x, y), (x, 1 - y), (1 - x, 1 - y)]
    return x, y, c, 2 * x + y, chips, [2 * cx + cy for cx, cy in chips]


def _piece(full_ref, name, q, half):
    K, N = full_ref.shape
    if name in ROW_SHARDED:
        rows = K // 8
        return full_ref.at[pl.ds(q * (2 * rows) + half * rows, rows), :]
    return full_ref.at[pl.ds(half * (K // 2), K // 2), pl.ds(q * (N // 4), N // 4)]


def _piece_shape(name, full_shape):
    K, N = full_shape
    return (K // 8, N) if name in ROW_SHARDED else (K // 2, N // 4)


def _full_shape(name, quarter_shape):
    Kq, Nq = quarter_shape
    return (4 * Kq, Nq) if name in ROW_SHARDED else (Kq, 4 * Nq)


def _remote(src, dst, send_sem, recv_sem, device):
    return pltpu.make_async_remote_copy(src_ref=src, dst_ref=dst, send_sem=send_sem, recv_sem=recv_sem,
                                        device_id=device, device_id_type=MESH)


def _z_part(u1, w_in, z_prev, place, k0, count, side=None, own_quarter=False):
    T, K = u1.shape
    nq = w_in.shape[1] if own_quarter else w_in.shape[1] // 4
    N = 4 * nq
    tn = nq // 2 if (nq // 2) % LANE == 0 else nq
    tm = min(MM_TM, T)
    per = nq // tn
    col = lambda g, j, s: (s[0] ^ (k0 + g)) * per + j
    ins = [pl.BlockSpec((tm, K), lambda g, i, j, s: (i, 0)),
           pl.BlockSpec((K, tn), (lambda g, i, j, s: (0, j)) if own_quarter else (lambda g, i, j, s: (0, col(g, j, s))))]
    operands = [place, u1, w_in]
    if z_prev is not None:
        ins.append(ANY)
        operands.append(z_prev)

    def body(place_ref, a_ref, b_ref, *rest):
        rest[-1][...] = _nn(a_ref[...], b_ref[...].astype(BF16))

    return _call_with_side(
        body, "z_part_%d" % k0, (count, T // tm, per), ins,
        [pl.BlockSpec((tm, tn), lambda g, i, j, s: (i, col(g, j, s)))], [jax.ShapeDtypeStruct((T, N), F32)],
        [], ("parallel",) * 3, tuple(operands), side, n_prefetch=1, aliases={} if z_prev is None else {2: 0},
        borrow={0: 1} if side is not None and side.aliased and side.aliased[0] is w_in else None)


def _rows(ref, span):
    return ref if span is None else ref.at[pl.ds(span[0], span[1]), :]


def _gather_moves(items):
    count = {"near": lambda arg: 2, "far": lambda arg: 1, "pass": len}

    def build(reads, aliased, fresh, send_sems, recv_sems, off=0):
        x, y, c, p, chips, chip_ids = _place()
        south = c == 0
        far_src = jnp.where(south, chip_ids[0], chip_ids[1])
        far_dst = (jnp.where(south, x, 1 - x), jnp.where(south, 1 - y, y), c)
        out = []

        def add(ref, device):
            k = off + len(out)
            out.append(_remote(ref, ref, send_sems.at[k], recv_sems.at[k], device))

        for (name, _, moves), ref in zip(items, aliased):
            for kind, arg in moves:
                if kind == "near":
                    for chip in chips[:2]:
                        add(_rows(_piece(ref, name, p, c), arg), (*chip, c))
                elif kind == "far":
                    add(_rows(_piece(ref, name, far_src, c), arg), far_dst)
                else:
                    for j in arg:
                        add(_piece(ref, name, chip_ids[j], c), (x, y, 1 - c))
        return out

    nsem = sum(count[kind](arg) for _, _, moves in items for kind, arg in moves)
    return _Side(build, nsem, aliased=[a for _, a, _ in items])


def _ici_near(names, fulls, rows=None):
    return _gather_moves([(n, a, [("near", r)]) for n, a, r in zip(names, fulls, rows or [None] * len(names))])


def _ici_far(names, fulls, rows=None):
    return _gather_moves([(n, a, [("far", r)]) for n, a, r in zip(names, fulls, rows or [None] * len(names))])


def _d2d_gather(names, fulls, which=(0, 1, 2)):
    return _gather_moves([(n, a, [("pass", which)]) for n, a in zip(names, fulls)])


def _sib_send(names, grads):
    def build(reads, aliased, fresh, send_sems, recv_sems, off=0):
        x, y, c, _, _, _ = _place()
        out = []
        for i, name in enumerate(names):
            for q in range(4):
                k = off + 4 * i + q
                out.append(_remote(_piece(reads[i], name, q, 1 - c), fresh[i].at[q], send_sems.at[k], recv_sems.at[k],
                                   (x, y, 1 - c)))
        return out

    shapes = [jax.ShapeDtypeStruct((4,) + _piece_shape(name, g.shape), BF16) for name, g in zip(names, grads)]
    return _Side(build, 4 * len(names), reads=grads, fresh=shapes)


def _sib_send_half(sent):
    K2, N = sent.shape

    def build(reads, aliased, fresh, send_sems, recv_sems, off=0):
        x, y, c, _, _, _ = _place()
        return [_remote(reads[0].at[:, pl.ds(q * (N // 4), N // 4)], fresh[0].at[q], send_sems.at[off + q],
                        recv_sems.at[off + q], (x, y, 1 - c)) for q in range(4)]

    return _Side(build, 4, reads=[sent], fresh=[jax.ShapeDtypeStruct((4, K2, N // 4), BF16)])


def _chip_exchange(chip_sums, rows=None, got=None):
    rows = rows or [None] * len(chip_sums)

    def build(reads, aliased, fresh, send_sems, recv_sems, off=0):
        _, _, c, _, chips, chip_ids = _place()
        out = []
        for i in range(len(chip_sums)):
            for j, (chip, cid) in enumerate(zip(chips, chip_ids)):
                k = off + 3 * i + j
                out.append(_remote(_rows(reads[i].at[cid], rows[i]), _rows((aliased or fresh)[i].at[j], rows[i]),
                                   send_sems.at[k], recv_sems.at[k], (*chip, c)))
        return out

    if got is not None:
        return _Side(build, 3 * len(chip_sums), reads=chip_sums, aliased=got)
    shapes = [jax.ShapeDtypeStruct((3,) + s.shape[1:], BF16) for s in chip_sums]
    return _Side(build, 3 * len(chip_sums), reads=chip_sums, fresh=shapes)


HBM = pl.BlockSpec(memory_space=pltpu.HBM)
SEM = pl.BlockSpec(memory_space=pltpu.SEMAPHORE)


def _exchange_copies(s_refs, land_refs, send_sems, recv_sems):
    _, _, c, _, chips, chip_ids = _place()
    return [_remote(s_ref.at[cid], land_ref.at[j], send_sems.at[3 * i + j], recv_sems.at[3 * i + j], (*chip, c))
            for i, (s_ref, land_ref) in enumerate(zip(s_refs, land_refs))
            for j, (chip, cid) in enumerate(zip(chips, chip_ids))]


def _exchange_start(name, chip_sums):
    n = len(chip_sums)

    def body(*refs):
        for cp in _exchange_copies(refs[:n], refs[n:2 * n], refs[2 * n], refs[2 * n + 1]):
            cp.start()
        refs[-1][...] = jnp.zeros_like(refs[-1])

    lands = [jax.ShapeDtypeStruct((3,) + s.shape[1:], s.dtype) for s in chip_sums]
    hbm = lambda a: pltpu.with_memory_space_constraint(a, pltpu.HBM)
    outs = pl.pallas_call(
        body, name="exchange_start_" + name,
        out_shape=(pltpu.SemaphoreType.DMA((3 * n,)), pltpu.SemaphoreType.DMA((3 * n,)),
                   *[pltpu.HBM(a.shape, a.dtype) for a in chip_sums + lands], jax.ShapeDtypeStruct((8, LANE), F32)),
        in_specs=(HBM,) * (2 * n), out_specs=(SEM, SEM) + (HBM,) * (2 * n) + (pl.BlockSpec(memory_space=pltpu.VMEM),),
        input_output_aliases={i: 2 + i for i in range(2 * n)},
        compiler_params=pltpu.CompilerParams(has_side_effects=pltpu.SideEffectType.DATAFLOW_SIDE_EFFECTING),
    )(*[hbm(s) for s in chip_sums], *[hbm(lax.empty(a.shape, a.dtype)) for a in lands])
    return outs[0], outs[1], list(outs[2:2 + n]), list(outs[2 + n:2 + 2 * n]), outs[-1]


def _exchange_wait(name, flight, after):
    send_sems, recv_sems, s_thru, land_thru, _ = flight
    n = len(s_thru)

    def body(*refs):
        for cp in _exchange_copies(refs[:n], refs[n:2 * n], refs[2 * n], refs[2 * n + 1]):
            cp.wait_send()
            cp.wait_recv()

    outs = pl.pallas_call(
        body, name="exchange_wait_" + name,
        out_shape=tuple(pltpu.HBM(a.shape, a.dtype) for a in s_thru + land_thru),
        in_specs=(HBM,) * (2 * n) + (SEM, SEM, ANY), out_specs=(HBM,) * (2 * n),
        input_output_aliases={i: i for i in range(2 * n)},
        compiler_params=pltpu.CompilerParams(has_side_effects=pltpu.SideEffectType.DATAFLOW_SIDE_EFFECTING),
    )(*s_thru, *land_thru, send_sems, recv_sems, after)
    return list(outs[:n]), list(outs[n:])


def _move_copies(kind, name, f_ref, send_sems, recv_sems):
    return _gather_moves([(name, None, [(kind, None)])]).build([], [f_ref], [], send_sems, recv_sems)


def _move_start(kind, name, full):
    def body(f_ref, send_sems, recv_sems, f_thru, token):
        for cp in _move_copies(kind, name, f_ref, send_sems, recv_sems):
            cp.start()
        token[...] = jnp.zeros_like(token)

    return pl.pallas_call(
        body, name=kind + "_start_" + name,
        out_shape=(pltpu.SemaphoreType.DMA((2,)), pltpu.SemaphoreType.DMA((2,)), pltpu.HBM(full.shape, full.dtype),
                   jax.ShapeDtypeStruct((8, LANE), F32)),
        in_specs=(HBM,), out_specs=(SEM, SEM, HBM, pl.BlockSpec(memory_space=pltpu.VMEM)),
        input_output_aliases={0: 2},
        compiler_params=pltpu.CompilerParams(has_side_effects=pltpu.SideEffectType.DATAFLOW_SIDE_EFFECTING),
    )(pltpu.with_memory_space_constraint(full, pltpu.HBM))


def _move_wait(kind, name, flight, after):
    send_sems, recv_sems, f_thru, _ = flight

    def body(f_ref, send_sems, recv_sems, after_ref, f_out):
        for cp in _move_copies(kind, name, f_ref, send_sems, recv_sems):
            cp.wait_send()
            cp.wait_recv()

    return pl.pallas_call(
        body, name=kind + "_wait_" + name, out_shape=pltpu.HBM(f_thru.shape, f_thru.dtype),
        in_specs=(HBM, SEM, SEM, ANY), out_specs=HBM, input_output_aliases={0: 0},
        compiler_params=pltpu.CompilerParams(has_side_effects=pltpu.SideEffectType.DATAFLOW_SIDE_EFFECTING),
    )(f_thru, send_sems, recv_sems, after)


def _sib_share(halves):
    def build(reads, aliased, fresh, send_sems, recv_sems, off=0):
        x, y, c, _, _, _ = _place()
        return [_remote(reads[i], fresh[i], send_sems.at[off + i], recv_sems.at[off + i], (x, y, 1 - c))
                for i in range(len(halves))]

    return _Side(build, len(halves), reads=halves, fresh=[jax.ShapeDtypeStruct(h.shape, F32) for h in halves])


def _join(a, b):
    def build(reads, aliased, fresh, send_sems, recv_sems, off=0):
        ra, aa, fa = len(a.reads), len(a.aliased), len(a.fresh)
        return (a.build(reads[:ra], aliased[:aa], fresh[:fa], send_sems, recv_sems, off)
                + b.build(reads[ra:], aliased[aa:], fresh[fa:], send_sems, recv_sems, off + a.nsem))

    return _Side(build, a.nsem + b.nsem, a.reads + b.reads, a.aliased + b.aliased, a.fresh + b.fresh)


def _run_side(name, side):
    nr, na = len(side.reads), len(side.aliased)

    def body(*refs):
        n_in, n_out = nr + na, na + len(side.fresh)
        outs = refs[n_in:n_in + n_out]
        copies = side.build(refs[:nr], outs[:na], outs[na:], *refs[-2:])
        for cp in copies:
            cp.start()
        for cp in copies:
            cp.wait()

    return pl.pallas_call(
        body, name=name, in_specs=side.in_specs(), out_specs=side.out_specs(), out_shape=side.out_shape(),
        input_output_aliases=side.aliases(0, 0), scratch_shapes=side.scratch(),
    )(*side.operands())


def _small_allreduce_adam(gpart, w, m, v, after):
    R = gpart.shape[0]

    def body(g_ref, w_ref, m_ref, v_ref, after_ref, go_ref, d_ref, mo_ref, vo_ref, buf, send_sems, recv_sems):
        x, y, c = lax.axis_index("x"), lax.axis_index("y"), lax.axis_index("c")
        me = 4 * x + 2 * y + c
        buf[me] = g_ref[...]
        copies = []
        for k in range(1, 8):
            fx, fy, fc = (k >> 2) & 1, (k >> 1) & 1, k & 1
            peer = (1 - x if fx else x, 1 - y if fy else y, 1 - c if fc else c)
            cp = _remote(g_ref, buf.at[me], send_sems.at[k - 1], recv_sems.at[k - 1], peer)
            cp.start()
            copies.append((cp, 4 * peer[0] + 2 * peer[1] + peer[2]))
        for k, (cp, pid) in enumerate(copies):
            _remote(g_ref, buf.at[pid], send_sems.at[k], recv_sems.at[k], (x, y, c)).wait_recv()
        for cp, _ in copies:
            cp.wait_send()
        g = buf[0]
        for d in range(1, 8):
            g = g + buf[d]
        delta, mn, vn = _adam_math(w_ref[...], g, m_ref[...], v_ref[...])
        go_ref[...] = g
        d_ref[...] = delta
        mo_ref[...] = mn
        vo_ref[...] = vn

    vm = pl.BlockSpec(memory_space=pltpu.VMEM)
    return pl.pallas_call(
        body, name="small_allreduce_adam",
        in_specs=[vm] * 4 + [ANY], out_specs=[vm] * 4,
        out_shape=[jax.ShapeDtypeStruct((R, LANE), F32)] * 4,
        scratch_shapes=[pltpu.VMEM((8, R, LANE), F32), pltpu.SemaphoreType.DMA((7,)), pltpu.SemaphoreType.DMA((7,))],
    )(gpart, w, m, v, after)


def _pack(arrs):
    flat = jnp.concatenate([a.reshape(-1).astype(F32) for a in arrs])
    rows = -(-flat.shape[0] // (8 * LANE)) * 8
    return jnp.pad(flat, (0, rows * LANE - flat.shape[0])).reshape(rows, LANE)


def _unpack(packed, like):
    flat, out, off = packed.reshape(-1), [], 0
    for a in like:
        out.append(flat[off:off + a.size].reshape(a.shape))
        off += a.size
    return out


def kernel(x, w_in, lb_logits, hg_norm_w, rel_bias, w_branch_a, w_branch_b, w_out, norm_mix_w, norm_mlp_w, w_up, w_down, norm_final_w, loss_target, m_w_in, m_lb_logits, m_hg_norm_w, m_rel_bias, m_w_branch_a, m_w_branch_b, m_w_out, m_norm_mix_w, m_norm_mlp_w, m_w_up, m_w_down, m_norm_final_w, v_w_in, v_lb_logits, v_hg_norm_w, v_rel_bias, v_w_branch_a, v_w_branch_b, v_w_out, v_norm_mix_w, v_norm_mlp_w, v_w_up, v_w_down, v_norm_final_w):
    T, D = x.shape[1], x.shape[2]
    x2, tgt = x.reshape(T, D), loss_target.reshape(T, D)
    big = dict(w_in=(w_in, m_w_in, v_w_in), w_branch_a=(w_branch_a, m_w_branch_a, v_w_branch_a),
               w_branch_b=(w_branch_b, m_w_branch_b, v_w_branch_b), w_out=(w_out, m_w_out, v_w_out),
               w_up=(w_up, m_w_up, v_w_up), w_down=(w_down, m_w_down, v_w_down))
    big = {k: tuple(a[0] for a in v) for k, v in big.items()}
    nfw = norm_final_w.reshape(1, D)

    place = jnp.stack([2 * lax.axis_index("x") + lax.axis_index("y"), lax.axis_index("c")]).astype(jnp.int32)
    small3 = ["w_branch_a", "w_branch_b", "w_out"]

    def span(name, lo, hi):
        pr = big[name][0].shape[0] // 2
        return (pr * lo // 16, pr * (hi - lo) // 16)

    w_in_full, w_in_own = _cast_into_full("w_in", big["w_in"][0], place, also_alone=True)
    flight_in = _move_start("near", "w_in", w_in_full)
    Wf = {name: _cast_into_full(name, big[name][0], place) for name in WEIGHTS if name != "w_in"}

    u1 = _rms_fwd("norm_mix", x2, norm_mix_w, side=_after(flight_in[-1]))
    z = _z_part(u1[0], w_in_own, None, place, 0, 1, own_quarter=True)[0]
    u1 = u1[0]
    Wf["w_in"] = _move_wait("near", "w_in", flight_in, z)

    def carried(**moves):
        def arg(n, k, a):
            if k == "pass":
                return a[0] if a else (0, 1, 2)
            return span(n, *a) if a else None

        return _gather_moves([(n, Wf[n], [(k, arg(n, k, a)) for k, *a in ms]) for n, ms in moves.items()]), list(moves)

    def land(names, outs):
        Wf.update(zip(names, outs[-len(names):]))
        return outs[:-len(names)]

    side, names = carried(w_in=[("pass", (0, 1))], w_out=[("near", 0, 8)])
    land(names, _run_side("pass_w_in_near", side))
    flight_in = _move_start("far", "w_in", Wf["w_in"])
    z = _z_part(u1, flight_in[2], z, place, 1, 2)[0]
    Wf["w_in"] = _move_wait("far", "w_in", flight_in, z)
    side, names = carried(w_in=[("pass", (2,))], w_branch_a=[("near",)], w_branch_b=[("near",)])
    land(names, _run_side("pass_w_in_far", side))
    side, names = carried(w_branch_a=[("far",)], w_branch_b=[("far",)], w_out=[("near", 8, 16)])
    (z,) = land(names, _z_part(u1, Wf["w_in"], z, place, 3, 1, side=side))
    side, names = carried(w_branch_a=[("pass",)], w_branch_b=[("pass",)], w_up=[("near", 0, 10)])
    ya, o_hg, states = land(names, _hg_fwd(z, lb_logits, hg_norm_w, side=side))
    bias_win = _bias_window(rel_bias[0])
    side, names = carried(w_out=[("far",)], w_up=[("near", 10, 16), ("far", 0, 10)], w_down=[("near", 0, 3)])
    (yb,) = land(names, _at_fwd(z, bias_win, side=side))
    side, names = carried(w_out=[("pass",)], w_up=[("far", 10, 14)])
    (pa,) = land(names, _mm("branch_a", ya, Wf["w_branch_a"], "nn", [BF16], side=side))
    side, names = carried(w_up=[("far", 14, 16)], w_down=[("near", 3, 4)])
    (pb,) = land(names, _mm("branch_b", yb, Wf["w_branch_b"], "nn", [BF16], side=side))
    side, names = carried(w_down=[("near", 4, 8)])
    (merged,) = land(names, _merge(z, pa, pb, side=side))
    add = lambda acc, res: (acc + res,)
    side, names = carried(w_up=[("pass",)], w_down=[("near", 8, 12)])
    (h1,) = land(names, _mm("out_proj", merged, Wf["w_out"], "nn", [F32], extras=[x2], epilogue=add, side=side))
    side, names = carried(w_down=[("near", 12, 14)])
    (u2,) = land(names, _rms_fwd("norm_mlp", h1, norm_mlp_w, side=side))
    relu2 = lambda acc: (acc, jnp.square(jnp.maximum(acc, 0.0)))
    side, names = carried(w_down=[("near", 14, 16), ("far", 0, 14)])
    a_pre, act = land(names, _mm("mlp_up", u2, Wf["w_up"], "nn", [F32, BF16], epilogue=relu2, side=side))
    (Wf["w_down"],) = _run_side("far_w_down", _ici_far(["w_down"], [Wf["w_down"]], rows=[span("w_down", 14, 16)]))
    (Wf["w_down"],) = _run_side("pass_w_down", _d2d_gather(["w_down"], [Wf["w_down"]]))
    h2 = _mm("mlp_down", act, Wf["w_down"], "nn", [F32], extras=[h1], epilogue=add)
    loss_part, dh2, dh2b, d_nf = _loss_head(h2, tgt, nfw)

    drelu2 = lambda acc, a: (acc * (2.0 * jnp.maximum(a, 0.0)),)
    da = _mm("d_act", dh2b, Wf["w_down"], "nt", [BF16], extras=[a_pre], epilogue=drelu2)
    G = {}
    G["w_down"] = _mm("g_w_down", act, dh2b, "tn", [BF16])
    G["w_up"] = _mm("g_w_up", u2, da, "tn", [BF16])
    T_, S_, GOT = {}, {}, {}
    du2, T_["w_down"], T_["w_up"] = _mm("d_u2", da, Wf["w_up"], "nt", [F32],
                                        side=_sib_send(["w_down", "w_up"], [G["w_down"], G["w_up"]]))
    mlp2 = ["w_down", "w_up"]
    flight_mlp = _exchange_start("mlp", [_chip_sum(n, G[n], T_[n], place) for n in mlp2])
    dh1, dh1b, d_nmlp = _rms_bwd("norm_mlp_bwd", du2, h1, norm_mlp_w, dh2, side=_after(flight_mlp[-1]))
    dmerged = _mm("d_merged", dh1b, Wf["w_out"], "nt", [F32])
    G["w_out"] = _mm("g_w_out", merged, dh1b, "tn", [BF16])
    dpa, dpb, dz_ga, dz_gb = _dmerge(dmerged, z, pa, pb)
    dya = _mm("d_ya", dpa, Wf["w_branch_a"], "nt", [F32])
    dyb = _mm("d_yb", dpb, Wf["w_branch_b"], "nt", [F32])
    G["w_branch_a"] = _mm("g_w_a", ya, dpa, "tn", [BF16])
    G["w_branch_b"] = _mm("g_w_b", yb, dpb, "tn", [BF16])
    dz_q, dz_f, dz_i, dz_g, d_lbl, d_hgw, *sent = _hg_bwd(
        z, o_hg, dya, states, lb_logits, hg_norm_w, side=_sib_send(small3, [G[n] for n in small3]))
    flight_small = _exchange_start("small", [_chip_sum(n, G[n], t, place) for n, t in zip(small3, sent)])
    dz_aq, dz_ak, dz_av, dbias_win = _at_bwd(z, dyb, bias_win, side=_after(flight_small[-1]))
    dz = jnp.concatenate([dz_q, dz_f, dz_i, dz_g, dz_aq, dz_ak, dz_av, dz_ga, dz_gb], axis=1)
    g_send = _g_w_in_half(u1, dz, place, False)
    g_keep, T_["w_in"] = _g_w_in_half(u1, dz, place, True, side=_sib_send_half(g_send))
    for names, flight in ((mlp2, flight_mlp), (small3, flight_small)):
        sums, got = _exchange_wait("_".join(names), flight, g_keep)
        S_.update(zip(names, sums))
        GOT.update(zip(names, got))
    S_["w_in"] = _chip_sum("w_in", g_keep, T_["w_in"], place, kept_rows=True)
    early = [n for n in WEIGHTS if n != "w_in"]
    H_ = {n: _piece_sum(n, S_[n], GOT[n], place) for n in early}
    flight = _exchange_start("w_in", [S_["w_in"]])
    share_early = _sib_share([H_[n] for n in early])
    share_early.reads.append(flight[-1])
    du1, *shared = _mm("d_u1", dz, Wf["w_in"], "nt", [F32], side=share_early)
    O_ = dict(zip(early, shared))
    grad_x, _, d_nmix = _rms_bwd("norm_mix_bwd", du1, x2, norm_mix_w, dh1)
    d_rel = _bias_window_grad(dbias_win)
    big_out = {}
    for name in early:
        outs = _adam_quarter(name, *big[name], H_[name], O_[name], place)
        big_out[name] = tuple(a[None] for a in outs)
    (S_["w_in"],), (got_in,) = _exchange_wait("w_in", flight, outs[1])
    H_["w_in"] = _piece_sum("w_in", S_["w_in"], got_in, place)
    (O_["w_in"],) = _run_side("share_w_in", _sib_share([H_["w_in"]]))
    outs = _adam_quarter("w_in", *big["w_in"], H_["w_in"], O_["w_in"], place)
    big_out["w_in"] = tuple(a[None] for a in outs)

    smalls = [("lb_logits", lb_logits, m_lb_logits, v_lb_logits, d_lbl),
              ("hg_norm_w", hg_norm_w, m_hg_norm_w, v_hg_norm_w, d_hgw),
              ("rel_bias", rel_bias, m_rel_bias, v_rel_bias, d_rel),
              ("norm_mix_w", norm_mix_w, m_norm_mix_w, v_norm_mix_w, d_nmix),
              ("norm_mlp_w", norm_mlp_w, m_norm_mlp_w, v_norm_mlp_w, d_nmlp),
              ("norm_final_w", norm_final_w, m_norm_final_w, v_norm_final_w, d_nf)]
    one = jnp.zeros((1,), F32)
    like = [s[1] for s in smalls] + [one]
    packed = _small_allreduce_adam(_pack([s[4] for s in smalls] + [loss_part[0, :1]]), _pack(like),
                                   _pack([s[2] for s in smalls] + [one]), _pack([s[3] for s in smalls] + [one]), got_in)
    unpacked = [_unpack(p, like) for p in packed]
    small_out = {s[0]: vals for s, vals in zip(smalls, zip(*unpacked))}
    loss = unpacked[0][-1].reshape(())
    order = ["w_in", "lb_logits", "hg_norm_w", "rel_bias", "w_branch_a", "w_branch_b", "w_out", "norm_mix_w",
             "norm_mlp_w", "w_up", "w_down", "norm_final_w"]
    res = {**big_out, **small_out}
    return (loss, grad_x.reshape(x.shape), *[res[n][0] for n in order], *[res[n][1] for n in order],
            *[res[n][2] for n in order], *[res[n][3] for n in order])
```

```python
import functools

import jax
import jax.numpy as jnp
from jax import lax
from jax.experimental import pallas as pl
from jax.experimental.pallas import tpu as pltpu

F32 = jnp.float32
BF16 = jnp.bfloat16
HIGHEST = lax.Precision.HIGHEST

D_MODEL = 2048
SEQ = 2048
CHUNK = 64
HG_HEADS = 8
HG_D = 128
AT_HEADS = 16
AT_DH = 64
LEFT = 8
REL_CLIP = 256
D_FF = 8192
EPS = 1e-6
ADAM_LR = 0.001
ADAM_B1 = 0.9
ADAM_B2 = 0.999
ADAM_EPS = 1e-08
ADAM_WD = 0.01
ADAM_STEP = 10

LANE = 128
NEG = -1e30
EXP_CLAMP = 80.0
VMEM_LIMIT = 48 * 1024 * 1024
MM_TM, MM_TN, MM_TK = 1024, 1024, 2816
ROW_TILE = 256
QB = 2 * CHUNK


def _hgw():
    return HG_HEADS * HG_D


def _atw():
    return AT_HEADS * AT_DH


def _cparams(sem):
    return pltpu.CompilerParams(dimension_semantics=sem, vmem_limit_bytes=VMEM_LIMIT)


def _sigmoid(x):
    return jax.nn.sigmoid(x)


def _dot(a, b, dims, precision=None):
    return lax.dot_general(a, b, (dims, ((), ())), preferred_element_type=F32, precision=precision)


def _nn(a, b, precision=None):
    return _dot(a, b, ((1,), (0,)), precision)


def _nt(a, b, precision=None):
    return _dot(a, b, ((1,), (1,)), precision)


def _tn(a, b, precision=None):
    return _dot(a, b, ((0,), (0,)), precision)


class _Side:
    def __init__(self, build, nsem, reads=(), aliased=(), fresh=()):
        self.build, self.nsem = build, nsem
        self.reads, self.aliased, self.fresh = list(reads), list(aliased), list(fresh)

    def operands(self):
        return self.reads + self.aliased

    def in_specs(self):
        return [ANY] * len(self.operands())

    def out_specs(self):
        return [ANY] * (len(self.aliased) + len(self.fresh))

    def out_shape(self):
        return [jax.ShapeDtypeStruct(a.shape, a.dtype) for a in self.aliased] + self.fresh

    def aliases(self, n_in, n_out):
        return {n_in + len(self.reads) + t: n_out + t for t in range(len(self.aliased))}

    def scratch(self):
        return [pltpu.SemaphoreType.DMA((self.nsem,)), pltpu.SemaphoreType.DMA((self.nsem,))]

    def hooks(self, in_refs, out_refs, sems, first, last):
        nr, na = len(self.reads), len(self.aliased)
        args = (in_refs[:nr], out_refs[:na], out_refs[na:], *sems)

        @pl.when(first)
        def _():
            for cp in self.build(*args):
                cp.start()

        @pl.when(last)
        def _():
            for cp in self.build(*args):
                cp.wait()


def _after(*tokens):
    return _Side(lambda *args: [], 1, reads=tokens)


def _side_parts(side):
    if side is None:
        return [], [], [], [], lambda n_in, n_out: {}, []
    return side.operands(), side.in_specs(), side.out_specs(), side.out_shape(), side.aliases, side.scratch()


def _call_with_side(body, name, grid, in_specs, out_specs, out_shape, scratch, sem, operands, side, n_prefetch=0,
                    aliases=None, borrow=None):
    _, _, s_out, s_shape, _, s_scr = _side_parts(side)
    n_in, n_out = n_prefetch + len(in_specs), len(out_specs)
    borrow = borrow or {}
    s_ops, s_alias = [], {}
    if side is not None:
        keep = [t for t in range(len(side.aliased)) if t not in borrow]
        s_ops = side.reads + [side.aliased[t] for t in keep]
        s_alias = {n_in + len(side.reads) + pos: n_out + t for pos, t in enumerate(keep)}
        s_alias.update({n_prefetch + i: n_out + t for t, i in borrow.items()})
    s_in = [ANY] * len(s_ops)
    n_sin, n_sout = len(s_ops), len(s_out)

    def wrapped(*refs):
        a, b, c = n_in + n_sin, n_in + n_sin + n_out, n_in + n_sin + n_out + n_sout
        ids = [pl.program_id(d) for d in range(len(grid))]
        first = functools.reduce(lambda p, q: p & q, [i == 0 for i in ids])
        last = functools.reduce(lambda p, q: p & q, [i == g - 1 for i, g in zip(ids, grid)])
        side.hooks(refs[n_in:a], refs[b:c], refs[-2:], first, last)
        body(*refs[:n_in], *refs[a:b], *refs[c:-2])

    spec = dict(grid=grid, in_specs=in_specs + s_in, out_specs=out_specs + s_out, scratch_shapes=scratch + s_scr)
    if n_prefetch:
        spec = dict(grid_spec=pltpu.PrefetchScalarGridSpec(num_scalar_prefetch=n_prefetch, **spec))
    return pl.pallas_call(
        body if side is None else wrapped, name=name, out_shape=out_shape + s_shape,
        input_output_aliases={**s_alias, **{n_prefetch + i: o for i, o in (aliases or {}).items()}},
        compiler_params=_cparams(sem if side is None else ("arbitrary",) * len(grid)), **spec,
    )(*operands, *s_ops)


def _mm_tk(K):
    if K <= MM_TK:
        return K
    return max(t for t in range(LANE, MM_TK + 1, LANE) if K % t == 0)


def _mm(name, a, b, mode, out_dtypes, extras=(), epilogue=None, side=None):
    if mode == "nn":
        (M, K), (K2, N) = a.shape, b.shape
    elif mode == "nt":
        (M, K), (N, K2) = a.shape, b.shape
    else:
        (K, M), (K2, N) = a.shape, b.shape
    assert K == K2, (name, a.shape, b.shape)
    tm, tn, tk = min(MM_TM, M), min(MM_TN, N), _mm_tk(K)
    assert M % tm == 0 and N % tn == 0 and K % tk == 0, (name, M, N, K)
    ni, nj, nk = M // tm, N // tn, K // tk
    ne, no = len(extras), len(out_dtypes)
    if epilogue is None:
        epilogue = lambda acc: (acc,)
    s_ops, s_in, s_out, s_shape, s_alias, s_scr = _side_parts(side)
    n_in, n_sin, n_sout = 2 + ne, len(s_ops), len(s_out)

    def body(*refs):
        a_ref, b_ref = refs[:2]
        extra_refs = refs[2:n_in]
        out_refs = refs[n_in + n_sin:n_in + n_sin + no]
        rest = refs[n_in + n_sin + no + n_sout:]
        i, j, k = pl.program_id(0), pl.program_id(1), pl.program_id(2)
        if side is not None:
            side.hooks(refs[n_in:n_in + n_sin], refs[n_in + n_sin + no:n_in + n_sin + no + n_sout], rest[-2:],
                       (i == 0) & (j == 0) & (k == 0), (i == ni - 1) & (j == nj - 1) & (k == nk - 1))
        av, bv = a_ref[...].astype(BF16), b_ref[...].astype(BF16)
        prod = _nn(av, bv) if mode == "nn" else _nt(av, bv) if mode == "nt" else _tn(av, bv)

        def finish(acc):
            res = epilogue(acc, *[e[...] for e in extra_refs])
            for o_ref, r in zip(out_refs, res):
                o_ref[...] = r.astype(o_ref.dtype)

        if nk == 1:
            finish(prod)
        else:
            acc_ref = rest[0]

            @pl.when(k == 0)
            def _():
                acc_ref[...] = prod

            @pl.when((k > 0) & (k < nk - 1))
            def _():
                acc_ref[...] += prod

            @pl.when(k == nk - 1)
            def _():
                finish(acc_ref[...] + prod)

    if mode == "nn":
        a_spec = pl.BlockSpec((tm, tk), lambda i, j, k: (i, k))
        b_spec = pl.BlockSpec((tk, tn), lambda i, j, k: (k, j))
    elif mode == "nt":
        a_spec = pl.BlockSpec((tm, tk), lambda i, j, k: (i, k))
        b_spec = pl.BlockSpec((tn, tk), lambda i, j, k: (j, k))
    else:
        a_spec = pl.BlockSpec((tk, tm), lambda i, j, k: (k, i))
        b_spec = pl.BlockSpec((tk, tn), lambda i, j, k: (k, j))
    o_spec = pl.BlockSpec((tm, tn), lambda i, j, k: (i, j))
    sem = ("arbitrary",) * 3 if side is not None else ("parallel", "parallel", "arbitrary")
    outs = pl.pallas_call(
        body, name=name,
        grid=(ni, nj, nk),
        in_specs=[a_spec, b_spec] + [o_spec] * ne + s_in,
        out_specs=[o_spec] * no + s_out,
        out_shape=[jax.ShapeDtypeStruct((M, N), dt) for dt in out_dtypes] + s_shape,
        input_output_aliases=s_alias(n_in, no),
        scratch_shapes=([pltpu.VMEM((tm, tn), F32)] if nk > 1 else []) + s_scr,
        compiler_params=_cparams(sem),
    )(a, b, *extras, *s_ops)
    return outs[0] if len(outs) == 1 else outs


def _row_spec(tr, d):
    return pl.BlockSpec((tr, d), lambda i: (i, 0))


def _vec_spec(d):
    return pl.BlockSpec((1, d), lambda i: (0, 0))


def _rms_fwd(name, x, w, side=None):
    T, D = x.shape
    tr = min(ROW_TILE, T)

    def body(x_ref, w_ref, o_ref):
        xf = x_ref[...]
        r = lax.rsqrt(jnp.mean(xf * xf, axis=-1, keepdims=True) + EPS)
        o_ref[...] = (xf * r * w_ref[...]).astype(BF16)

    outs = _call_with_side(body, name, (T // tr,), [_row_spec(tr, D), _vec_spec(D)], [_row_spec(tr, D)],
                           [jax.ShapeDtypeStruct((T, D), BF16)], [], ("parallel",), (x, w), side)
    return outs[0] if side is None else outs


def _rms_bwd(name, dy, h, w, dres, side=None):
    T, D = h.shape
    tr = min(ROW_TILE, T)

    def body(dy_ref, h_ref, w_ref, dres_ref, dh_ref, dhb_ref, dw_ref):
        @pl.when(pl.program_id(0) == 0)
        def _():
            dw_ref[...] = jnp.zeros_like(dw_ref)

        hf, dyv = h_ref[...], dy_ref[...]
        r = lax.rsqrt(jnp.mean(hf * hf, axis=-1, keepdims=True) + EPS)
        xhat = hf * r
        dw_ref[...] += jnp.sum(dyv * xhat, axis=0, keepdims=True)
        dxh = dyv * w_ref[...]
        dh = dres_ref[...] + r * (dxh - xhat * jnp.mean(dxh * xhat, axis=-1, keepdims=True))
        dh_ref[...] = dh
        dhb_ref[...] = dh.astype(BF16)

    return _call_with_side(
        body, name, (T // tr,),
        [_row_spec(tr, D), _row_spec(tr, D), _vec_spec(D), _row_spec(tr, D)],
        [_row_spec(tr, D), _row_spec(tr, D), _vec_spec(D)],
        [jax.ShapeDtypeStruct((T, D), F32), jax.ShapeDtypeStruct((T, D), BF16), jax.ShapeDtypeStruct((1, D), F32)],
        [], ("arbitrary",), (dy, h, w, dres), side)


def _loss_head(h2, target, w):
    T, D = h2.shape
    tr = min(ROW_TILE, T)

    def body(h_ref, t_ref, w_ref, loss_ref, dh_ref, dhb_ref, dw_ref):
        @pl.when(pl.program_id(0) == 0)
        def _():
            dw_ref[...] = jnp.zeros_like(dw_ref)
            loss_ref[...] = jnp.zeros_like(loss_ref)

        hf, wv = h_ref[...], w_ref[...]
        r = lax.rsqrt(jnp.mean(hf * hf, axis=-1, keepdims=True) + EPS)
        xhat = hf * r
        diff = xhat * wv - t_ref[...]
        loss_ref[...] += 0.5 * jnp.sum(jnp.mean(diff * diff, axis=-1, keepdims=True))
        dyv = diff * (1.0 / D)
        dw_ref[...] += jnp.sum(dyv * xhat, axis=0, keepdims=True)
        dxh = dyv * wv
        dh = r * (dxh - xhat * jnp.mean(dxh * xhat, axis=-1, keepdims=True))
        dh_ref[...] = dh
        dhb_ref[...] = dh.astype(BF16)

    return pl.pallas_call(
        body, name="loss_head", grid=(T // tr,),
        in_specs=[_row_spec(tr, D), _row_spec(tr, D), _vec_spec(D)],
        out_specs=[_vec_spec(LANE), _row_spec(tr, D), _row_spec(tr, D), _vec_spec(D)],
        out_shape=[jax.ShapeDtypeStruct((1, LANE), F32), jax.ShapeDtypeStruct((T, D), F32),
                   jax.ShapeDtypeStruct((T, D), BF16), jax.ShapeDtypeStruct((1, D), F32)],
        compiler_params=_cparams(("arbitrary",)),
    )(h2, target, w)


def _gate_tiles(T, D):
    goff = 4 * _hgw() + 3 * _atw()
    tc = min(1024, D)
    assert goff % tc == 0 and D % tc == 0
    return min(ROW_TILE, T), tc, goff // tc, D // tc


def _merge(z, pa, pb, side=None):
    T, D = pa.shape
    tr, tc, g0, nd = _gate_tiles(T, D)

    def body(ga_ref, gb_ref, pa_ref, pb_ref, o_ref):
        o_ref[...] = (_sigmoid(ga_ref[...]) * pa_ref[...] + _sigmoid(gb_ref[...]) * pb_ref[...]).astype(BF16)

    t = pl.BlockSpec((tr, tc), lambda i, j: (i, j))
    outs = _call_with_side(
        body, "merge", (T // tr, nd),
        [pl.BlockSpec((tr, tc), lambda i, j: (i, g0 + j)), pl.BlockSpec((tr, tc), lambda i, j: (i, g0 + nd + j)), t, t],
        [t], [jax.ShapeDtypeStruct((T, D), BF16)], [], ("parallel", "parallel"), (z, z, pa, pb), side)
    return outs[0] if side is None else outs


def _dmerge(dm, z, pa, pb):
    T, D = pa.shape
    tr, tc, g0, nd = _gate_tiles(T, D)

    def body(dm_ref, ga_ref, gb_ref, pa_ref, pb_ref, dpa_ref, dpb_ref, dga_ref, dgb_ref):
        dmv = dm_ref[...]
        sa, sb = _sigmoid(ga_ref[...]), _sigmoid(gb_ref[...])
        dpa_ref[...] = (dmv * sa).astype(BF16)
        dpb_ref[...] = (dmv * sb).astype(BF16)
        dga_ref[...] = (dmv * pa_ref[...] * sa * (1.0 - sa)).astype(BF16)
        dgb_ref[...] = (dmv * pb_ref[...] * sb * (1.0 - sb)).astype(BF16)

    t = pl.BlockSpec((tr, tc), lambda i, j: (i, j))
    return pl.pallas_call(
        body, name="dmerge", grid=(T // tr, nd),
        in_specs=[t, pl.BlockSpec((tr, tc), lambda i, j: (i, g0 + j)),
                  pl.BlockSpec((tr, tc), lambda i, j: (i, g0 + nd + j)), t, t],
        out_specs=[t, t, t, t],
        out_shape=[jax.ShapeDtypeStruct((T, D), BF16)] * 4,
        compiler_params=_cparams(("parallel", "parallel")),
    )(dm, z, z, pa, pb)


def _hg_gates(xq, xf, lb):
    f = _sigmoid(xf)
    g = lb + (1.0 - lb) * f
    sq = _sigmoid(xq)
    return f, g, jnp.log(g), 1.0 - g, sq, xq * sq * (HG_D ** -0.5)


def _split2(x):
    hi = x.astype(BF16)
    return hi, (x - hi.astype(F32)).astype(BF16)


def _tri_sum(tri, x):
    hi, rest = x.astype(BF16), x - x.astype(BF16).astype(F32)
    mid, lo = _split2(rest)
    return _nn(tri, lo) + _nn(tri, mid) + _nn(tri, hi)


def _hg_decays(lg, tri_incl, rowi):
    b = _tri_sum(tri_incl, lg)
    b_last = jnp.sum(lg, axis=0, keepdims=True)
    b_mid = jnp.sum(jnp.where(rowi <= CHUNK // 2, lg, 0.0), axis=0, keepdims=True)
    return b, b_last, b_mid


HG_GROUP = 2


def _hg_in_specs(T):
    ng = HG_HEADS // HG_GROUP
    return [pl.BlockSpec((T, HG_GROUP * HG_D), lambda h, s=s: (0, s * ng + h)) for s in range(4)]


def _hg_fwd(z, lb_logits, hgw, side=None):
    T = z.shape[0]
    H, d, C, G = HG_HEADS, HG_D, CHUNK, HG_GROUP
    nc = T // C

    def body(hq_ref, hf_ref, hi_ref, hg_ref, lbl_ref, w_ref, ya_ref, o_ref, s_ref):
        lb_all = 1.0 / (1.0 + jnp.exp(lbl_ref[1:2, :] - lbl_ref[0:1, :]))
        wv = w_ref[...]
        row = lax.broadcasted_iota(jnp.int32, (C, C), 0)
        col = lax.broadcasted_iota(jnp.int32, (C, C), 1)
        tril = col <= row
        tri_incl = tril.astype(BF16)
        rowi = lax.broadcasted_iota(jnp.int32, (C, G * d), 0)
        lanes = [slice(hh * d, (hh + 1) * d) for hh in range(G)]
        per_head = lambda fn: jnp.concatenate([fn(hh, sl) for hh, sl in enumerate(lanes)], axis=1)
        wv_all = jnp.tile(wv, (1, G))

        def chunk(c, states):
            rows = pl.ds(pl.multiple_of(c * C, C), C)
            xq, xf, v, xg = hq_ref[rows, :], hf_ref[rows, :], hi_ref[rows, :], hg_ref[rows, :]
            _, _, lg, kk, _, q = _hg_gates(xq, xf, lb_all)
            b, b_last, b_mid = _hg_decays(lg, tri_incl, rowi)
            vb, qe = v.astype(BF16), (q * jnp.exp(b)).astype(BF16)
            qt = (q * jnp.exp(b - b_mid)).astype(BF16)
            kt = (kk * jnp.exp(jnp.minimum(b_mid - b, EXP_CLAMP))).astype(BF16)
            kd, e_last = (kk * jnp.exp(b_last - b)).astype(BF16), jnp.exp(b_last)
            for hh, st in enumerate(states):
                s_ref[hh, c] = st
            o = per_head(lambda hh, sl: _nt(qe[:, sl], states[hh].astype(BF16)))
            a = [jnp.where(tril, _nt(qt[:, sl], kt[:, sl]), 0.0).astype(BF16) for sl in lanes]
            o = o + per_head(lambda hh, sl: _nn(a[hh], vb[:, sl]))
            o_ref[rows, :] = o
            r = per_head(lambda hh, sl: jnp.broadcast_to(
                lax.rsqrt(jnp.mean(o[:, sl] * o[:, sl], axis=-1, keepdims=True) + EPS), (C, d)))
            ya_ref[rows, :] = (o * r * wv_all * (xg * _sigmoid(xg))).astype(BF16)
            return tuple(st * e_last[:, sl] + _tn(vb[:, sl], kd[:, sl]) for st, sl in zip(states, lanes))

        lax.fori_loop(0, nc, chunk, tuple(jnp.zeros((d, d), F32) for _ in range(G)))

    heads = pl.BlockSpec((T, G * d), lambda h: (0, h))
    return _call_with_side(
        body, "hg_fwd", (H // G,),
        _hg_in_specs(T) + [pl.BlockSpec((2, G * d), lambda h: (0, h)), pl.BlockSpec((1, d), lambda h: (0, 0))],
        [heads, heads, pl.BlockSpec((G, nc, d, d), lambda h: (h, 0, 0, 0))],
        [jax.ShapeDtypeStruct((T, H * d), BF16), jax.ShapeDtypeStruct((T, H * d), F32),
         jax.ShapeDtypeStruct((H, nc, d, d), F32)],
        [], ("parallel",), (z, z, z, z, lb_logits, hgw), side)


def _hg_bwd(z, o, dya, states, lb_logits, hgw, side=None):
    T = z.shape[0]
    H, d, C, G = HG_HEADS, HG_D, CHUNK, HG_GROUP
    nc = T // C
    scale = HG_D ** -0.5

    def body(hq_ref, hf_ref, hi_ref, hg_ref, o_ref, dy_ref, s_ref, lbl_ref, w_ref,
             dq_ref, df_ref, di_ref, dg_ref, dlbl_ref, dw_ref, acc_ref):
        lb_all = 1.0 / (1.0 + jnp.exp(lbl_ref[1:2, :] - lbl_ref[0:1, :]))
        wv = w_ref[...]
        row = lax.broadcasted_iota(jnp.int32, (C, C), 0)
        col = lax.broadcasted_iota(jnp.int32, (C, C), 1)
        tril = col <= row
        tri_incl = tril.astype(BF16)
        triu_incl = (col >= row).astype(BF16)
        rowi = lax.broadcasted_iota(jnp.int32, (C, G * d), 0)
        lanes = [slice(hh * d, (hh + 1) * d) for hh in range(G)]
        per_head = lambda fn: jnp.concatenate([fn(hh, sl) for hh, sl in enumerate(lanes)], axis=1)
        head_mean = lambda x: per_head(
            lambda hh, sl: jnp.broadcast_to(jnp.mean(x[:, sl], axis=-1, keepdims=True), (C, d)))
        wv_all = jnp.tile(wv, (1, G))
        lb = lb_all
        acc_ref[...] = jnp.zeros_like(acc_ref)

        @pl.when(pl.program_id(0) == 0)
        def _():
            dw_ref[...] = jnp.zeros_like(dw_ref)

        def chunk(i, carry):
            dsts, tail = carry
            c = nc - 1 - i
            rows = pl.ds(pl.multiple_of(c * C, C), C)
            xq, xf, v, xg = hq_ref[rows, :], hf_ref[rows, :], hi_ref[rows, :], hg_ref[rows, :]
            f, g, lg, kk, sq, q = _hg_gates(xq, xf, lb)
            b, b_last, b_mid = _hg_decays(lg, tri_incl, rowi)
            e_b, e_qm, e_km = jnp.exp(b), jnp.exp(b - b_mid), jnp.exp(jnp.minimum(b_mid - b, EXP_CLAMP))
            e_kl, e_last = jnp.exp(b_last - b), jnp.exp(b_last)
            ov, dy = o_ref[rows, :], dy_ref[rows, :]
            r = lax.rsqrt(head_mean(ov * ov) + EPS)
            xhat = ov * r
            sg = _sigmoid(xg)
            dxg = dy * xhat * wv_all * (sg * (1.0 + xg * (1.0 - sg)))
            dyn = dy * (xg * sg)
            acc_ref[0:1, :] += jnp.sum(dyn * xhat, axis=0, keepdims=True)
            dxh = dyn * wv_all
            dof = r * (dxh - xhat * head_mean(dxh * xhat))
            do, vb = dof.astype(BF16), v.astype(BF16)
            qe, kd, qt, kt = (q * e_b).astype(BF16), (kk * e_kl).astype(BF16), (q * e_qm).astype(BF16), (kk * e_km).astype(BF16)
            pm = [jnp.where(tril, _nt(do[:, sl], vb[:, sl]), 0.0).astype(BF16) for sl in lanes]
            am = [jnp.where(tril, _nt(qt[:, sl], kt[:, sl]), 0.0).astype(BF16) for sl in lanes]
            st = [_split2(s_ref[hh, c]) for hh in range(G)]
            ds = [_split2(x) for x in dsts]
            dq_state = per_head(lambda hh, sl: _nn(do[:, sl], st[hh][1]) + _nn(do[:, sl], st[hh][0]))
            dk_state = per_head(lambda hh, sl: _nn(vb[:, sl], ds[hh][1]) + _nn(vb[:, sl], ds[hh][0]))
            dq_intra = per_head(lambda hh, sl: _nn(pm[hh], kt[:, sl]))
            dk_intra = per_head(lambda hh, sl: _tn(pm[hh], qt[:, sl]))
            dv = per_head(lambda hh, sl: _tn(am[hh], do[:, sl]) + _nt(kd[:, sl], ds[hh][0]))
            new_dsts = tuple(x * e_last[:, sl] + _tn(do[:, sl], qe[:, sl]) for x, sl in zip(dsts, lanes))
            dq = dq_state * e_b + dq_intra * e_qm
            dk = dk_intra * e_km + dk_state * e_kl
            db = (qe.astype(F32) * dq_state + qt.astype(F32) * dq_intra
                  - kt.astype(F32) * dk_intra - kd.astype(F32) * dk_state)
            dlg = _tri_sum(triu_incl, db) + tail
            dgate = dlg / g - dk
            acc_ref[1:2, :] += jnp.sum(dgate * (1.0 - f), axis=0, keepdims=True)
            dq_ref[rows, :] = (dq * scale * (sq * (1.0 + xq * (1.0 - sq)))).astype(BF16)
            df_ref[rows, :] = (dgate * (1.0 - lb) * f * (1.0 - f)).astype(BF16)
            di_ref[rows, :] = dv.astype(BF16)
            dg_ref[rows, :] = dxg.astype(BF16)
            return new_dsts, tail + jnp.sum(db, axis=0, keepdims=True)

        lax.fori_loop(0, nc, chunk, (tuple(jnp.zeros((d, d), F32) for _ in range(G)), jnp.zeros((1, G * d), F32)))
        dw_ref[...] += functools.reduce(lambda p, q: p + q, [acc_ref[0:1, sl] for sl in lanes])
        dl0 = acc_ref[1:2, :] * lb_all * (1.0 - lb_all)
        dlbl_ref[0:1, :] = dl0
        dlbl_ref[1:2, :] = -dl0

    heads = pl.BlockSpec((T, G * d), lambda h: (0, h))
    logits = pl.BlockSpec((2, G * d), lambda h: (0, h))
    return _call_with_side(
        body, "hg_bwd", (H // G,),
        _hg_in_specs(T) + [heads, heads, pl.BlockSpec((G, nc, d, d), lambda h: (h, 0, 0, 0)), logits,
                           pl.BlockSpec((1, d), lambda h: (0, 0))],
        [heads, heads, heads, heads, logits, pl.BlockSpec((1, d), lambda h: (0, 0))],
        [jax.ShapeDtypeStruct((T, H * d), BF16)] * 4 + [jax.ShapeDtypeStruct((2, H * d), F32),
                                                        jax.ShapeDtypeStruct((1, d), F32)],
        [pltpu.VMEM((8, G * d), F32)], ("arbitrary",), (z, z, z, z, o, dya, states, lb_logits, hgw), side)


def _at_dims():
    pad = LEFT * CHUNK
    return pad, QB + pad, AT_HEADS * AT_DH // LANE, 4 * _hgw() // LANE


def _rel_of_period():
    pad, W, _, _ = _at_dims()
    n = jnp.arange(QB + W)
    return jnp.clip(pad - jnp.where(n < W, n, n - (QB + W)), -REL_CLIP, REL_CLIP) + REL_CLIP


def _bias_window(rel_bias):
    pad, W, _, _ = _at_dims()
    H, P = rel_bias.shape[0], QB + W
    per = rel_bias[:, _rel_of_period()]
    win = jnp.tile(per, (1, QB))[:, :QB * (P - 1)].reshape(H, QB, P - 1)[:, :, :W]
    t = jnp.arange(QB)[:, None]
    j = jnp.arange(W)[None, :]
    ok = (j // CHUNK >= t // CHUNK) & (j // CHUNK <= t // CHUNK + LEFT)
    return jnp.where(ok[None], win, NEG)


def _bias_window_grad(dbw):
    pad, W, _, _ = _at_dims()
    H, P = dbw.shape[0], QB + W
    flat = jnp.pad(dbw, ((0, 0), (0, 0), (0, P - 1 - W))).reshape(H, QB * (P - 1))
    per = jnp.pad(flat, ((0, 0), (0, QB))).reshape(H, QB, P).sum(axis=1)
    onehot = _rel_of_period()[:, None] == jnp.arange(2 * REL_CLIP + 1)[None, :]
    return jnp.dot(per, onehot.astype(F32), precision=HIGHEST)


def _at_stack(x):
    first = lax.broadcasted_iota(jnp.int32, x.shape, 1) < AT_DH
    return jnp.concatenate([jnp.where(first, x, 0.0), jnp.where(first, 0.0, x)], axis=0).astype(BF16)


def _at_unstack(x):
    first = lax.broadcasted_iota(jnp.int32, (QB, LANE), 1) < AT_DH
    return jnp.where(first, x[:QB], x[QB:])


def _at_softmax(qs, kw, bias_ref, qi):
    pad, W, _, _ = _at_dims()
    s = _nt(qs, kw) + bias_ref[...].reshape(2 * QB, W)
    valid = lax.broadcasted_iota(jnp.int32, (2 * QB, W), 1) + qi * QB >= pad
    s = jnp.where(valid, s, NEG)
    e = jnp.exp(s - jnp.max(s, axis=-1, keepdims=True))
    return e * (1.0 / jnp.sum(e, axis=-1, keepdims=True))


def _at_fwd(z, bias_win, side=None):
    T = z.shape[0]
    pad, W, HP, c0 = _at_dims()
    nq = T // QB

    def body(q_ref, k_ref, v_ref, bias_ref, o_ref, kpad, vpad):
        qi = pl.program_id(1)

        @pl.when(qi == 0)
        def _():
            kpad[0:pad, :] = jnp.zeros((pad, LANE), BF16)
            vpad[0:pad, :] = jnp.zeros((pad, LANE), BF16)
            kpad[pad:, :] = k_ref[...].astype(BF16)
            vpad[pad:, :] = v_ref[...].astype(BF16)

        win = pl.ds(pl.multiple_of(qi * QB, QB), W)
        kw, vw = kpad[win, :], vpad[win, :]
        p = _at_softmax(_at_stack(q_ref[...] * (AT_DH ** -0.5)), kw, bias_ref, qi)
        o_ref[...] = _at_unstack(_nn(p.astype(BF16), vw)).astype(BF16)

    full = lambda s: pl.BlockSpec((T, LANE), lambda hp, qi, s=s: (0, c0 + s * HP + hp))
    return _call_with_side(
        body, "at_fwd", (HP, nq),
        [pl.BlockSpec((QB, LANE), lambda hp, qi: (qi, c0 + hp)), full(1), full(2),
         pl.BlockSpec((2, QB, W), lambda hp, qi: (hp, 0, 0))],
        [pl.BlockSpec((QB, LANE), lambda hp, qi: (qi, hp))],
        [jax.ShapeDtypeStruct((T, HP * LANE), BF16)],
        [pltpu.VMEM((T + pad, LANE), BF16)] * 2, ("parallel", "arbitrary"), (z, z, z, bias_win), side)


def _at_bwd(z, dyb, bias_win, side=None):
    T = z.shape[0]
    pad, W, HP, c0 = _at_dims()
    nq = T // QB
    scale = AT_DH ** -0.5

    def body(q_ref, k_ref, v_ref, do_ref, bias_ref, dq_ref, dk_ref, dv_ref, dbias_ref, kpad, vpad, dkpad, dvpad):
        qi = pl.program_id(1)

        @pl.when(qi == 0)
        def _():
            kpad[0:pad, :] = jnp.zeros((pad, LANE), BF16)
            vpad[0:pad, :] = jnp.zeros((pad, LANE), BF16)
            kpad[pad:, :] = k_ref[...].astype(BF16)
            vpad[pad:, :] = v_ref[...].astype(BF16)
            dkpad[...] = jnp.zeros_like(dkpad)
            dvpad[...] = jnp.zeros_like(dvpad)
            dbias_ref[...] = jnp.zeros_like(dbias_ref)

        win = pl.ds(pl.multiple_of(qi * QB, QB), W)
        kw, vw = kpad[win, :], vpad[win, :]
        qs, dos = _at_stack(q_ref[...] * scale), _at_stack(do_ref[...])
        p = _at_softmax(qs, kw, bias_ref, qi)
        dp = _nt(dos, vw)
        ds = p * (dp - jnp.sum(p * dp, axis=-1, keepdims=True))
        dbias_ref[...] += ds.reshape(2, QB, W)
        dsb = ds.astype(BF16)
        dq_ref[...] = (_at_unstack(_nn(dsb, kw)) * scale).astype(BF16)
        lanes_win = pl.ds(pl.multiple_of(qi * QB, QB), W)
        dkpad[:, lanes_win] += _tn(qs, dsb)
        dvpad[:, lanes_win] += _tn(dos, p.astype(BF16))

        @pl.when(qi == nq - 1)
        def _():
            dk_ref[...] = dkpad[:, pad:].T.astype(BF16)
            dv_ref[...] = dvpad[:, pad:].T.astype(BF16)

    full = lambda s: pl.BlockSpec((T, LANE), lambda hp, qi, s=s: (0, c0 + s * HP + hp))
    blk = pl.BlockSpec((QB, LANE), lambda hp, qi: (qi, hp))
    col = pl.BlockSpec((T, LANE), lambda hp, qi: (0, hp))
    bw = pl.BlockSpec((2, QB, W), lambda hp, qi: (hp, 0, 0))
    return _call_with_side(
        body, "at_bwd", (HP, nq),
        [pl.BlockSpec((QB, LANE), lambda hp, qi: (qi, c0 + hp)), full(1), full(2), blk, bw],
        [blk, col, col, bw],
        [jax.ShapeDtypeStruct((T, HP * LANE), BF16)] * 3 + [jax.ShapeDtypeStruct(bias_win.shape, F32)],
        [pltpu.VMEM((T + pad, LANE), BF16)] * 2 + [pltpu.VMEM((LANE, T + pad), F32)] * 2,
        ("parallel", "arbitrary"), (z, z, z, dyb, bias_win), side)


def _piece_tiles(name, full_shape):
    pr, pc = _piece_shape(name, full_shape)
    tr = min(ROW_TILE, pr)
    assert pr % tr == 0
    nt = pr // tr
    if name in ROW_SHARDED:
        return tr, nt, lambda q, half, i: ((2 * q + half) * nt + i, 0)
    return tr, nt, lambda q, half, i: (half * nt + i, q)


def _cast_into_full(name, wq, place, also_alone=False):
    full = _full_shape(name, wq.shape)
    pc = wq.shape[1]
    tr, nt, at = _piece_tiles(name, full)

    def body(place_ref, w_ref, *o_refs):
        for o_ref in o_refs:
            o_ref[...] = w_ref[...].astype(BF16)

    quarter = pl.BlockSpec((tr, pc), lambda h, i, s: (h * nt + i, 0))
    outs = _call_with_side(
        body, "cast_" + name, (2, nt), [quarter],
        [pl.BlockSpec((tr, pc), lambda h, i, s: at(s[0], h, i))] + [quarter] * also_alone,
        [jax.ShapeDtypeStruct(full, BF16)] + [jax.ShapeDtypeStruct(wq.shape, BF16)] * also_alone,
        [], ("parallel", "parallel"), (place, wq), None, n_prefetch=1)
    return tuple(outs) if also_alone else outs[0]


def _g_w_in_half(u1, dz, place, own, side=None):
    T, K = u1.shape
    N = dz.shape[1]
    hk, tn = K // 2, min(MM_TN, N)
    half = (lambda s: s[1]) if own else (lambda s: 1 - s[1])

    def body(place_ref, a_ref, b_ref, o_ref):
        o_ref[...] = _tn(a_ref[...], b_ref[...]).astype(BF16)

    outs = _call_with_side(
        body, "g_w_in_keep" if own else "g_w_in_send", (N // tn,),
        [pl.BlockSpec((T, hk), lambda j, s: (0, half(s))), pl.BlockSpec((T, tn), lambda j, s: (0, j))],
        [pl.BlockSpec((hk, tn), lambda j, s: (0, j))], [jax.ShapeDtypeStruct((hk, N), BF16)],
        [], ("parallel",), (place, u1, dz), side, n_prefetch=1)
    return outs[0] if side is None else outs


def _chip_sum(name, grad, theirs, place, kept_rows=False):
    pr, pc = theirs.shape[1:]
    tr, nt, at = _piece_tiles(name, (2 * grad.shape[0], grad.shape[1]) if kept_rows else grad.shape)
    if kept_rows:
        at = lambda q, half, i: (i, q)

    def body(place_ref, g_ref, t_ref, o_ref):
        o_ref[...] = (g_ref[...].astype(F32) + t_ref[...].astype(F32)).astype(BF16)

    piece = pl.BlockSpec((None, tr, pc), lambda q, i, s: (q, i, 0))
    return pl.pallas_call(
        body, name="chip_sum_" + name,
        grid_spec=pltpu.PrefetchScalarGridSpec(
            num_scalar_prefetch=1, grid=(4, nt),
            in_specs=[pl.BlockSpec((tr, pc), lambda q, i, s: at(q, s[1], i)), piece], out_specs=piece),
        out_shape=jax.ShapeDtypeStruct(theirs.shape, BF16),
        compiler_params=_cparams(("parallel", "parallel")),
    )(place, grad, theirs)


def _piece_sum(name, chip_sums, got, place):
    pr, pc = chip_sums.shape[1:]
    tr = min(ROW_TILE, pr)

    def body(place_ref, own_ref, got_ref, o_ref):
        o_ref[...] = (own_ref[...].astype(F32) + got_ref[0].astype(F32) + got_ref[1].astype(F32)
                      + got_ref[2].astype(F32))

    return pl.pallas_call(
        body, name="piece_sum_" + name,
        grid_spec=pltpu.PrefetchScalarGridSpec(
            num_scalar_prefetch=1, grid=(pr // tr,),
            in_specs=[pl.BlockSpec((None, tr, pc), lambda i, s: (s[0], i, 0)),
                      pl.BlockSpec((3, tr, pc), lambda i, s: (0, i, 0))],
            out_specs=pl.BlockSpec((tr, pc), lambda i, s: (i, 0))),
        out_shape=jax.ShapeDtypeStruct((pr, pc), F32),
        compiler_params=_cparams(("parallel",)),
    )(place, chip_sums, got)


def _adam_quarter(name, w, m, v, g_mine, g_sib, place, side=None):
    pr, pc = g_mine.shape
    tr = min(ROW_TILE // 2, pr)
    nt = pr // tr

    def body(place_ref, w_ref, m_ref, v_ref, gm_ref, gs_ref, go_ref, d_ref, mo_ref, vo_ref):
        g = jnp.where(pl.program_id(0) == place_ref[1], gm_ref[...], gs_ref[...])
        delta, mn, vn = _adam_math(w_ref[...], g, m_ref[...], v_ref[...])
        go_ref[...] = g
        d_ref[...] = delta
        mo_ref[...] = mn
        vo_ref[...] = vn

    quarter = pl.BlockSpec((tr, pc), lambda h, i, s: (h * nt + i, 0))
    mine = pl.BlockSpec((tr, pc), lambda h, i, s: (jnp.where(h == s[1], i, 0), 0))
    sib = pl.BlockSpec((tr, pc), lambda h, i, s: (jnp.where(h == s[1], 0, i), 0))
    return _call_with_side(
        body, "adam_" + name, (2, nt), [quarter, quarter, quarter, mine, sib], [quarter] * 4,
        [jax.ShapeDtypeStruct(w.shape, F32)] * 4, [], ("parallel", "parallel"),
        (place, w, m, v, g_mine, g_sib), side, n_prefetch=1)


def _adam_math(w, g, m, v):
    m = ADAM_B1 * m + (1.0 - ADAM_B1) * g
    v = ADAM_B2 * v + (1.0 - ADAM_B2) * (g * g)
    m_hat = m / (1.0 - ADAM_B1 ** ADAM_STEP)
    v_hat = v / (1.0 - ADAM_B2 ** ADAM_STEP)
    return -ADAM_LR * (m_hat / (jnp.sqrt(v_hat) + ADAM_EPS) + ADAM_WD * w), m, v


WEIGHTS = ("w_in", "w_branch_a", "w_branch_b", "w_out", "w_up", "w_down")
ROW_SHARDED = ("w_out", "w_down")
ANY = pl.BlockSpec(memory_space=pl.ANY)
MESH = pl.DeviceIdType.MESH


def _place():
    x, y, c = lax.axis_index("x"), lax.axis_index("y"), lax.axis_index("c")
    chips = [(1 - x, y), (x, 1 - y), (1 - x, 1 - y)]
    return x, y, c, 2 * x + y, chips, [2 * cx + cy for cx, cy in chips]


def _piece(full_ref, name, q, half):
    K, N = full_ref.shape
    if name in ROW_SHARDED:
        rows = K // 8
        return full_ref.at[pl.ds(q * (2 * rows) + half * rows, rows), :]
    return full_ref.at[pl.ds(half * (K // 2), K // 2), pl.ds(q * (N // 4), N // 4)]


def _piece_shape(name, full_shape):
    K, N = full_shape
    return (K // 8, N) if name in ROW_SHARDED else (K // 2, N // 4)


def _full_shape(name, quarter_shape):
    Kq, Nq = quarter_shape
    return (4 * Kq, Nq) if name in ROW_SHARDED else (Kq, 4 * Nq)


def _remote(src, dst, send_sem, recv_sem, device):
    return pltpu.make_async_remote_copy(src_ref=src, dst_ref=dst, send_sem=send_sem, recv_sem=recv_sem,
                                        device_id=device, device_id_type=MESH)


def _z_part(u1, w_in, z_prev, place, k0, count, side=None, own_quarter=False):
    T, K = u1.shape
    nq = w_in.shape[1] if own_quarter else w_in.shape[1] // 4
    N = 4 * nq
    tn = nq // 2 if (nq // 2) % LANE == 0 else nq
    tm = min(MM_TM, T)
    per = nq // tn
    col = lambda g, j, s: (s[0] ^ (k0 + g)) * per + j
    ins = [pl.BlockSpec((tm, K), lambda g, i, j, s: (i, 0)),
           pl.BlockSpec((K, tn), (lambda g, i, j, s: (0, j)) if own_quarter else (lambda g, i, j, s: (0, col(g, j, s))))]
    operands = [place, u1, w_in]
    if z_prev is not None:
        ins.append(ANY)
        operands.append(z_prev)

    def body(place_ref, a_ref, b_ref, *rest):
        rest[-1][...] = _nn(a_ref[...], b_ref[...].astype(BF16))

    return _call_with_side(
        body, "z_part_%d" % k0, (count, T // tm, per), ins,
        [pl.BlockSpec((tm, tn), lambda g, i, j, s: (i, col(g, j, s)))], [jax.ShapeDtypeStruct((T, N), F32)],
        [], ("parallel",) * 3, tuple(operands), side, n_prefetch=1, aliases={} if z_prev is None else {2: 0},
        borrow={0: 1} if side is not None and side.aliased and side.aliased[0] is w_in else None)


def _rows(ref, span):
    return ref if span is None else ref.at[pl.ds(span[0], span[1]), :]


def _gather_moves(items):
    count = {"near": lambda arg: 2, "far": lambda arg: 1, "pass": len}

    def build(reads, aliased, fresh, send_sems, recv_sems, off=0):
        x, y, c, p, chips, chip_ids = _place()
        south = c == 0
        far_src = jnp.where(south, chip_ids[0], chip_ids[1])
        far_dst = (jnp.where(south, x, 1 - x), jnp.where(south, 1 - y, y), c)
        out = []

        def add(ref, device):
            k = off + len(out)
            out.append(_remote(ref, ref, send_sems.at[k], recv_sems.at[k], device))

        for (name, _, moves), ref in zip(items, aliased):
            for kind, arg in moves:
                if kind == "near":
                    for chip in chips[:2]:
                        add(_rows(_piece(ref, name, p, c), arg), (*chip, c))
                elif kind == "far":
                    add(_rows(_piece(ref, name, far_src, c), arg), far_dst)
                else:
                    for j in arg:
                        add(_piece(ref, name, chip_ids[j], c), (x, y, 1 - c))
        return out

    nsem = sum(count[kind](arg) for _, _, moves in items for kind, arg in moves)
    return _Side(build, nsem, aliased=[a for _, a, _ in items])


def _ici_far(names, fulls, rows=None):
    return _gather_moves([(n, a, [("far", r)]) for n, a, r in zip(names, fulls, rows or [None] * len(names))])


def _d2d_gather(names, fulls, which=(0, 1, 2)):
    return _gather_moves([(n, a, [("pass", which)]) for n, a in zip(names, fulls)])


def _sib_send(names, grads):
    def build(reads, aliased, fresh, send_sems, recv_sems, off=0):
        x, y, c, _, _, _ = _place()
        out = []
        for i, name in enumerate(names):
            for q in range(4):
                k = off + 4 * i + q
                out.append(_remote(_piece(reads[i], name, q, 1 - c), fresh[i].at[q], send_sems.at[k], recv_sems.at[k],
                                   (x, y, 1 - c)))
        return out

    shapes = [jax.ShapeDtypeStruct((4,) + _piece_shape(name, g.shape), BF16) for name, g in zip(names, grads)]
    return _Side(build, 4 * len(names), reads=grads, fresh=shapes)


def _sib_send_half(sent):
    K2, N = sent.shape

    def build(reads, aliased, fresh, send_sems, recv_sems, off=0):
        x, y, c, _, _, _ = _place()
        return [_remote(reads[0].at[:, pl.ds(q * (N // 4), N // 4)], fresh[0].at[q], send_sems.at[off + q],
                        recv_sems.at[off + q], (x, y, 1 - c)) for q in range(4)]

    return _Side(build, 4, reads=[sent], fresh=[jax.ShapeDtypeStruct((4, K2, N // 4), BF16)])


HBM = pl.BlockSpec(memory_space=pltpu.HBM)
SEM = pl.BlockSpec(memory_space=pltpu.SEMAPHORE)


def _exchange_copies(s_refs, land_refs, send_sems, recv_sems):
    _, _, c, _, chips, chip_ids = _place()
    return [_remote(s_ref.at[cid], land_ref.at[j], send_sems.at[3 * i + j], recv_sems.at[3 * i + j], (*chip, c))
            for i, (s_ref, land_ref) in enumerate(zip(s_refs, land_refs))
            for j, (chip, cid) in enumerate(zip(chips, chip_ids))]


def _exchange_start(name, chip_sums):
    n = len(chip_sums)

    def body(*refs):
        for cp in _exchange_copies(refs[:n], refs[n:2 * n], refs[2 * n], refs[2 * n + 1]):
            cp.start()
        refs[-1][...] = jnp.zeros_like(refs[-1])

    lands = [jax.ShapeDtypeStruct((3,) + s.shape[1:], s.dtype) for s in chip_sums]
    hbm = lambda a: pltpu.with_memory_space_constraint(a, pltpu.HBM)
    outs = pl.pallas_call(
        body, name="exchange_start_" + name,
        out_shape=(pltpu.SemaphoreType.DMA((3 * n,)), pltpu.SemaphoreType.DMA((3 * n,)),
                   *[pltpu.HBM(a.shape, a.dtype) for a in chip_sums + lands], jax.ShapeDtypeStruct((8, LANE), F32)),
        in_specs=(HBM,) * (2 * n), out_specs=(SEM, SEM) + (HBM,) * (2 * n) + (pl.BlockSpec(memory_space=pltpu.VMEM),),
        input_output_aliases={i: 2 + i for i in range(2 * n)},
        compiler_params=pltpu.CompilerParams(has_side_effects=pltpu.SideEffectType.DATAFLOW_SIDE_EFFECTING),
    )(*[hbm(s) for s in chip_sums], *[hbm(lax.empty(a.shape, a.dtype)) for a in lands])
    return outs[0], outs[1], list(outs[2:2 + n]), list(outs[2 + n:2 + 2 * n]), outs[-1]


def _exchange_wait(name, flight, after):
    send_sems, recv_sems, s_thru, land_thru, _ = flight
    n = len(s_thru)

    def body(*refs):
        for cp in _exchange_copies(refs[:n], refs[n:2 * n], refs[2 * n], refs[2 * n + 1]):
            cp.wait_send()
            cp.wait_recv()

    outs = pl.pallas_call(
        body, name="exchange_wait_" + name,
        out_shape=tuple(pltpu.HBM(a.shape, a.dtype) for a in s_thru + land_thru),
        in_specs=(HBM,) * (2 * n) + (SEM, SEM, ANY), out_specs=(HBM,) * (2 * n),
        input_output_aliases={i: i for i in range(2 * n)},
        compiler_params=pltpu.CompilerParams(has_side_effects=pltpu.SideEffectType.DATAFLOW_SIDE_EFFECTING),
    )(*s_thru, *land_thru, send_sems, recv_sems, after)
    return list(outs[:n]), list(outs[n:])


def _move_copies(kind, name, f_ref, send_sems, recv_sems):
    return _gather_moves([(name, None, [(kind, None)])]).build([], [f_ref], [], send_sems, recv_sems)


def _move_start(kind, name, full):
    def body(f_ref, send_sems, recv_sems, f_thru, token):
        for cp in _move_copies(kind, name, f_ref, send_sems, recv_sems):
            cp.start()
        token[...] = jnp.zeros_like(token)

    return pl.pallas_call(
        body, name=kind + "_start_" + name,
        out_shape=(pltpu.SemaphoreType.DMA((2,)), pltpu.SemaphoreType.DMA((2,)), pltpu.HBM(full.shape, full.dtype),
                   jax.ShapeDtypeStruct((8, LANE), F32)),
        in_specs=(HBM,), out_specs=(SEM, SEM, HBM, pl.BlockSpec(memory_space=pltpu.VMEM)),
        input_output_aliases={0: 2},
        compiler_params=pltpu.CompilerParams(has_side_effects=pltpu.SideEffectType.DATAFLOW_SIDE_EFFECTING),
    )(pltpu.with_memory_space_constraint(full, pltpu.HBM))


def _move_wait(kind, name, flight, after):
    send_sems, recv_sems, f_thru, _ = flight

    def body(f_ref, send_sems, recv_sems, after_ref, f_out):
        for cp in _move_copies(kind, name, f_ref, send_sems, recv_sems):
            cp.wait_send()
            cp.wait_recv()

    return pl.pallas_call(
        body, name=kind + "_wait_" + name, out_shape=pltpu.HBM(f_thru.shape, f_thru.dtype),
        in_specs=(HBM, SEM, SEM, ANY), out_specs=HBM, input_output_aliases={0: 0},
        compiler_params=pltpu.CompilerParams(has_side_effects=pltpu.SideEffectType.DATAFLOW_SIDE_EFFECTING),
    )(f_thru, send_sems, recv_sems, after)


def _sib_share(halves):
    def build(reads, aliased, fresh, send_sems, recv_sems, off=0):
        x, y, c, _, _, _ = _place()
        return [_remote(reads[i], fresh[i], send_sems.at[off + i], recv_sems.at[off + i], (x, y, 1 - c))
                for i in range(len(halves))]

    return _Side(build, len(halves), reads=halves, fresh=[jax.ShapeDtypeStruct(h.shape, F32) for h in halves])


def _run_side(name, side):
    nr, na = len(side.reads), len(side.aliased)

    def body(*refs):
        n_in, n_out = nr + na, na + len(side.fresh)
        outs = refs[n_in:n_in + n_out]
        copies = side.build(refs[:nr], outs[:na], outs[na:], *refs[-2:])
        for cp in copies:
            cp.start()
        for cp in copies:
            cp.wait()

    return pl.pallas_call(
        body, name=name, in_specs=side.in_specs(), out_specs=side.out_specs(), out_shape=side.out_shape(),
        input_output_aliases=side.aliases(0, 0), scratch_shapes=side.scratch(),
    )(*side.operands())


def _small_allreduce_adam(gpart, w, m, v, after):
    R = gpart.shape[0]

    def body(g_ref, w_ref, m_ref, v_ref, after_ref, go_ref, d_ref, mo_ref, vo_ref, buf, send_sems, recv_sems):
        x, y, c = lax.axis_index("x"), lax.axis_index("y"), lax.axis_index("c")
        me = 4 * x + 2 * y + c
        buf[me] = g_ref[...]
        copies = []
        for k in range(1, 8):
            fx, fy, fc = (k >> 2) & 1, (k >> 1) & 1, k & 1
            peer = (1 - x if fx else x, 1 - y if fy else y, 1 - c if fc else c)
            cp = _remote(g_ref, buf.at[me], send_sems.at[k - 1], recv_sems.at[k - 1], peer)
            cp.start()
            copies.append((cp, 4 * peer[0] + 2 * peer[1] + peer[2]))
        for k, (cp, pid) in enumerate(copies):
            _remote(g_ref, buf.at[pid], send_sems.at[k], recv_sems.at[k], (x, y, c)).wait_recv()
        for cp, _ in copies:
            cp.wait_send()
        g = buf[0]
        for d in range(1, 8):
            g = g + buf[d]
        delta, mn, vn = _adam_math(w_ref[...], g, m_ref[...], v_ref[...])
        go_ref[...] = g
        d_ref[...] = delta
        mo_ref[...] = mn
        vo_ref[...] = vn

    vm = pl.BlockSpec(memory_space=pltpu.VMEM)
    return pl.pallas_call(
        body, name="small_allreduce_adam",
        in_specs=[vm] * 4 + [ANY], out_specs=[vm] * 4,
        out_shape=[jax.ShapeDtypeStruct((R, LANE), F32)] * 4,
        scratch_shapes=[pltpu.VMEM((8, R, LANE), F32), pltpu.SemaphoreType.DMA((7,)), pltpu.SemaphoreType.DMA((7,))],
    )(gpart, w, m, v, after)


def _pack(arrs):
    flat = jnp.concatenate([a.reshape(-1).astype(F32) for a in arrs])
    rows = -(-flat.shape[0] // (8 * LANE)) * 8
    return jnp.pad(flat, (0, rows * LANE - flat.shape[0])).reshape(rows, LANE)


def _unpack(packed, like):
    flat, out, off = packed.reshape(-1), [], 0
    for a in like:
        out.append(flat[off:off + a.size].reshape(a.shape))
        off += a.size
    return out


def kernel(x, w_in, lb_logits, hg_norm_w, rel_bias, w_branch_a, w_branch_b, w_out, norm_mix_w, norm_mlp_w, w_up, w_down, norm_final_w, loss_target, m_w_in, m_lb_logits, m_hg_norm_w, m_rel_bias, m_w_branch_a, m_w_branch_b, m_w_out, m_norm_mix_w, m_norm_mlp_w, m_w_up, m_w_down, m_norm_final_w, v_w_in, v_lb_logits, v_hg_norm_w, v_rel_bias, v_w_branch_a, v_w_branch_b, v_w_out, v_norm_mix_w, v_norm_mlp_w, v_w_up, v_w_down, v_norm_final_w):
    T, D = x.shape[1], x.shape[2]
    x2, tgt = x.reshape(T, D), loss_target.reshape(T, D)
    big = dict(w_in=(w_in, m_w_in, v_w_in), w_branch_a=(w_branch_a, m_w_branch_a, v_w_branch_a),
               w_branch_b=(w_branch_b, m_w_branch_b, v_w_branch_b), w_out=(w_out, m_w_out, v_w_out),
               w_up=(w_up, m_w_up, v_w_up), w_down=(w_down, m_w_down, v_w_down))
    big = {k: tuple(a[0] for a in v) for k, v in big.items()}
    nfw = norm_final_w.reshape(1, D)

    place = jnp.stack([2 * lax.axis_index("x") + lax.axis_index("y"), lax.axis_index("c")]).astype(jnp.int32)
    small3 = ["w_branch_a", "w_branch_b", "w_out"]

    def span(name, lo, hi):
        pr = big[name][0].shape[0] // 2
        return (pr * lo // 16, pr * (hi - lo) // 16)

    w_in_full, w_in_own = _cast_into_full("w_in", big["w_in"][0], place, also_alone=True)
    flight_in = _move_start("near", "w_in", w_in_full)
    Wf = {name: _cast_into_full(name, big[name][0], place) for name in WEIGHTS if name != "w_in"}

    u1 = _rms_fwd("norm_mix", x2, norm_mix_w, side=_after(flight_in[-1]))
    z = _z_part(u1[0], w_in_own, None, place, 0, 1, own_quarter=True)[0]
    u1 = u1[0]
    Wf["w_in"] = _move_wait("near", "w_in", flight_in, z)

    def carried(**moves):
        def arg(n, k, a):
            if k == "pass":
                return a[0] if a else (0, 1, 2)
            return span(n, *a) if a else None

        return _gather_moves([(n, Wf[n], [(k, arg(n, k, a)) for k, *a in ms]) for n, ms in moves.items()]), list(moves)

    def land(names, outs):
        Wf.update(zip(names, outs[-len(names):]))
        return outs[:-len(names)]

    side, names = carried(w_in=[("pass", (0, 1))], w_out=[("near", 0, 8)])
    land(names, _run_side("pass_w_in_near", side))
    flight_in = _move_start("far", "w_in", Wf["w_in"])
    z = _z_part(u1, flight_in[2], z, place, 1, 2)[0]
    Wf["w_in"] = _move_wait("far", "w_in", flight_in, z)
    side, names = carried(w_in=[("pass", (2,))], w_branch_a=[("near",)], w_branch_b=[("near",)])
    land(names, _run_side("pass_w_in_far", side))
    side, names = carried(w_branch_a=[("far",)], w_branch_b=[("far",)], w_out=[("near", 8, 16)])
    (z,) = land(names, _z_part(u1, Wf["w_in"], z, place, 3, 1, side=side))
    side, names = carried(w_branch_a=[("pass",)], w_branch_b=[("pass",)], w_up=[("near", 0, 10)])
    ya, o_hg, states = land(names, _hg_fwd(z, lb_logits, hg_norm_w, side=side))
    bias_win = _bias_window(rel_bias[0])
    side, names = carried(w_out=[("far",)], w_up=[("near", 10, 16), ("far", 0, 10)], w_down=[("near", 0, 3)])
    (yb,) = land(names, _at_fwd(z, bias_win, side=side))
    side, names = carried(w_out=[("pass",)], w_up=[("far", 10, 14)])
    (pa,) = land(names, _mm("branch_a", ya, Wf["w_branch_a"], "nn", [BF16], side=side))
    side, names = carried(w_up=[("far", 14, 16)], w_down=[("near", 3, 4)])
    (pb,) = land(names, _mm("branch_b", yb, Wf["w_branch_b"], "nn", [BF16], side=side))
    side, names = carried(w_down=[("near", 4, 8)])
    (merged,) = land(names, _merge(z, pa, pb, side=side))
    add = lambda acc, res: (acc + res,)
    side, names = carried(w_up=[("pass",)], w_down=[("near", 8, 12)])
    (h1,) = land(names, _mm("out_proj", merged, Wf["w_out"], "nn", [F32], extras=[x2], epilogue=add, side=side))
    side, names = carried(w_down=[("near", 12, 14)])
    (u2,) = land(names, _rms_fwd("norm_mlp", h1, norm_mlp_w, side=side))
    relu2 = lambda acc: (acc, jnp.square(jnp.maximum(acc, 0.0)))
    side, names = carried(w_down=[("near", 14, 16), ("far", 0, 14)])
    a_pre, act = land(names, _mm("mlp_up", u2, Wf["w_up"], "nn", [F32, BF16], epilogue=relu2, side=side))
    (Wf["w_down"],) = _run_side("far_w_down", _ici_far(["w_down"], [Wf["w_down"]], rows=[span("w_down", 14, 16)]))
    (Wf["w_down"],) = _run_side("pass_w_down", _d2d_gather(["w_down"], [Wf["w_down"]]))
    h2 = _mm("mlp_down", act, Wf["w_down"], "nn", [F32], extras=[h1], epilogue=add)
    loss_part, dh2, dh2b, d_nf = _loss_head(h2, tgt, nfw)

    drelu2 = lambda acc, a: (acc * (2.0 * jnp.maximum(a, 0.0)),)
    da = _mm("d_act", dh2b, Wf["w_down"], "nt", [BF16], extras=[a_pre], epilogue=drelu2)
    G = {}
    G["w_down"] = _mm("g_w_down", act, dh2b, "tn", [BF16])
    G["w_up"] = _mm("g_w_up", u2, da, "tn", [BF16])
    T_, S_, GOT = {}, {}, {}
    du2, T_["w_down"], T_["w_up"] = _mm("d_u2", da, Wf["w_up"], "nt", [F32],
                                        side=_sib_send(["w_down", "w_up"], [G["w_down"], G["w_up"]]))
    mlp2 = ["w_down", "w_up"]
    flight_mlp = _exchange_start("mlp", [_chip_sum(n, G[n], T_[n], place) for n in mlp2])
    dh1, dh1b, d_nmlp = _rms_bwd("norm_mlp_bwd", du2, h1, norm_mlp_w, dh2, side=_after(flight_mlp[-1]))
    dmerged = _mm("d_merged", dh1b, Wf["w_out"], "nt", [F32])
    G["w_out"] = _mm("g_w_out", merged, dh1b, "tn", [BF16])
    dpa, dpb, dz_ga, dz_gb = _dmerge(dmerged, z, pa, pb)
    dya = _mm("d_ya", dpa, Wf["w_branch_a"], "nt", [F32])
    dyb = _mm("d_yb", dpb, Wf["w_branch_b"], "nt", [F32])
    G["w_branch_a"] = _mm("g_w_a", ya, dpa, "tn", [BF16])
    G["w_branch_b"] = _mm("g_w_b", yb, dpb, "tn", [BF16])
    dz_q, dz_f, dz_i, dz_g, d_lbl, d_hgw, *sent = _hg_bwd(
        z, o_hg, dya, states, lb_logits, hg_norm_w, side=_sib_send(small3, [G[n] for n in small3]))
    flight_small = _exchange_start("small", [_chip_sum(n, G[n], t, place) for n, t in zip(small3, sent)])
    dz_aq, dz_ak, dz_av, dbias_win = _at_bwd(z, dyb, bias_win, side=_after(flight_small[-1]))
    dz = jnp.concatenate([dz_q, dz_f, dz_i, dz_g, dz_aq, dz_ak, dz_av, dz_ga, dz_gb], axis=1)
    g_send = _g_w_in_half(u1, dz, place, False)
    g_keep, T_["w_in"] = _g_w_in_half(u1, dz, place, True, side=_sib_send_half(g_send))
    for names, flight in ((mlp2, flight_mlp), (small3, flight_small)):
        sums, got = _exchange_wait("_".join(names), flight, g_keep)
        S_.update(zip(names, sums))
        GOT.update(zip(names, got))
    S_["w_in"] = _chip_sum("w_in", g_keep, T_["w_in"], place, kept_rows=True)
    early = [n for n in WEIGHTS if n != "w_in"]
    H_ = {n: _piece_sum(n, S_[n], GOT[n], place) for n in early}
    flight = _exchange_start("w_in", [S_["w_in"]])
    share_early = _sib_share([H_[n] for n in early])
    share_early.reads.append(flight[-1])
    du1, *shared = _mm("d_u1", dz, Wf["w_in"], "nt", [F32], side=share_early)
    O_ = dict(zip(early, shared))
    grad_x, _, d_nmix = _rms_bwd("norm_mix_bwd", du1, x2, norm_mix_w, dh1)
    d_rel = _bias_window_grad(dbias_win)
    big_out = {}
    for name in early:
        outs = _adam_quarter(name, *big[name], H_[name], O_[name], place)
        big_out[name] = tuple(a[None] for a in outs)
    (S_["w_in"],), (got_in,) = _exchange_wait("w_in", flight, outs[1])
    H_["w_in"] = _piece_sum("w_in", S_["w_in"], got_in, place)
    (O_["w_in"],) = _run_side("share_w_in", _sib_share([H_["w_in"]]))
    outs = _adam_quarter("w_in", *big["w_in"], H_["w_in"], O_["w_in"], place)
    big_out["w_in"] = tuple(a[None] for a in outs)

    smalls = [("lb_logits", lb_logits, m_lb_logits, v_lb_logits, d_lbl),
              ("hg_norm_w", hg_norm_w, m_hg_norm_w, v_hg_norm_w, d_hgw),
              ("rel_bias", rel_bias, m_rel_bias, v_rel_bias, d_rel),
              ("norm_mix_w", norm_mix_w, m_norm_mix_w, v_norm_mix_w, d_nmix),
              ("norm_mlp_w", norm_mlp_w, m_norm_mlp_w, v_norm_mlp_w, d_nmlp),
              ("norm_final_w", norm_final_w, m_norm_final_w, v_norm_final_w, d_nf)]
    one = jnp.zeros((1,), F32)
    like = [s[1] for s in smalls] + [one]
    packed = _small_allreduce_adam(_pack([s[4] for s in smalls] + [loss_part[0, :1]]), _pack(like),
                                   _pack([s[2] for s in smalls] + [one]), _pack([s[3] for s in smalls] + [one]), got_in)
    unpacked = [_unpack(p, like) for p in packed]
    small_out = {s[0]: vals for s, vals in zip(smalls, zip(*unpacked))}
    loss = unpacked[0][-1].reshape(())
    order = ["w_in", "lb_logits", "hg_norm_w", "rel_bias", "w_branch_a", "w_branch_b", "w_out", "norm_mix_w",
             "norm_mlp_w", "w_up", "w_down", "norm_final_w"]
    res = {**big_out, **small_out}
    return (loss, grad_x.reshape(x.shape), *[res[n][0] for n in order], *[res[n][1] for n in order],
            *[res[n][2] for n in order], *[res[n][3] for n in order])
```

```python
import functools

import jax
import jax.numpy as jnp
from jax import lax
from jax.experimental import pallas as pl
from jax.experimental.pallas import tpu as pltpu

F32 = jnp.float32
BF16 = jnp.bfloat16
HIGHEST = lax.Precision.HIGHEST

D_MODEL = 2048
SEQ = 2048
CHUNK = 64
HG_HEADS = 8
HG_D = 128
AT_HEADS = 16
AT_DH = 64
LEFT = 8
REL_CLIP = 256
D_FF = 8192
EPS = 1e-6
ADAM_LR = 0.001
ADAM_B1 = 0.9
ADAM_B2 = 0.999
ADAM_EPS = 1e-08
ADAM_WD = 0.01
ADAM_STEP = 10

LANE = 128
NEG = -1e30
EXP_CLAMP = 80.0
VMEM_LIMIT = 48 * 1024 * 1024
MM_TM, MM_TN, MM_TK = 1024, 1024, 2816
ROW_TILE = 256
QB = 2 * CHUNK


def _hgw():
    return HG_HEADS * HG_D


def _atw():
    return AT_HEADS * AT_DH


def _cparams(sem):
    return pltpu.CompilerParams(dimension_semantics=sem, vmem_limit_bytes=VMEM_LIMIT)


def _sigmoid(x):
    return jax.nn.sigmoid(x)


def _dot(a, b, dims, precision=None):
    return lax.dot_general(a, b, (dims, ((), ())), preferred_element_type=F32, precision=precision)


def _nn(a, b, precision=None):
    return _dot(a, b, ((1,), (0,)), precision)


def _nt(a, b, precision=None):
    return _dot(a, b, ((1,), (1,)), precision)


def _tn(a, b, precision=None):
    return _dot(a, b, ((0,), (0,)), precision)


class _Side:
    def __init__(self, build, nsem, reads=(), aliased=(), fresh=()):
        self.build, self.nsem = build, nsem
        self.reads, self.aliased, self.fresh = list(reads), list(aliased), list(fresh)

    def operands(self):
        return self.reads + self.aliased

    def in_specs(self):
        return [ANY] * len(self.operands())

    def out_specs(self):
        return [ANY] * (len(self.aliased) + len(self.fresh))

    def out_shape(self):
        return [jax.ShapeDtypeStruct(a.shape, a.dtype) for a in self.aliased] + self.fresh

    def aliases(self, n_in, n_out):
        return {n_in + len(self.reads) + t: n_out + t for t in range(len(self.aliased))}

    def scratch(self):
        return [pltpu.SemaphoreType.DMA((self.nsem,)), pltpu.SemaphoreType.DMA((self.nsem,))]

    def hooks(self, in_refs, out_refs, sems, first, last):
        nr, na = len(self.reads), len(self.aliased)
        args = (in_refs[:nr], out_refs[:na], out_refs[na:], *sems)

        @pl.when(first)
        def _():
            for cp in self.build(*args):
                cp.start()

        @pl.when(last)
        def _():
            for cp in self.build(*args):
                cp.wait()


def _after(*tokens):
    return _Side(lambda *args: [], 1, reads=tokens)


def _side_parts(side):
    if side is None:
        return [], [], [], [], lambda n_in, n_out: {}, []
    return side.operands(), side.in_specs(), side.out_specs(), side.out_shape(), side.aliases, side.scratch()


def _call_with_side(body, name, grid, in_specs, out_specs, out_shape, scratch, sem, operands, side, n_prefetch=0,
                    aliases=None, borrow=None):
    _, _, s_out, s_shape, _, s_scr = _side_parts(side)
    n_in, n_out = n_prefetch + len(in_specs), len(out_specs)
    borrow = borrow or {}
    s_ops, s_alias = [], {}
    if side is not None:
        keep = [t for t in range(len(side.aliased)) if t not in borrow]
        s_ops = side.reads + [side.aliased[t] for t in keep]
        s_alias = {n_in + len(side.reads) + pos: n_out + t for pos, t in enumerate(keep)}
        s_alias.update({n_prefetch + i: n_out + t for t, i in borrow.items()})
    s_in = [ANY] * len(s_ops)
    n_sin, n_sout = len(s_ops), len(s_out)

    def wrapped(*refs):
        a, b, c = n_in + n_sin, n_in + n_sin + n_out, n_in + n_sin + n_out + n_sout
        ids = [pl.program_id(d) for d in range(len(grid))]
        first = functools.reduce(lambda p, q: p & q, [i == 0 for i in ids])
        last = functools.reduce(lambda p, q: p & q, [i == g - 1 for i, g in zip(ids, grid)])
        side.hooks(refs[n_in:a], refs[b:c], refs[-2:], first, last)
        body(*refs[:n_in], *refs[a:b], *refs[c:-2])

    spec = dict(grid=grid, in_specs=in_specs + s_in, out_specs=out_specs + s_out, scratch_shapes=scratch + s_scr)
    if n_prefetch:
        spec = dict(grid_spec=pltpu.PrefetchScalarGridSpec(num_scalar_prefetch=n_prefetch, **spec))
    return pl.pallas_call(
        body if side is None else wrapped, name=name, out_shape=out_shape + s_shape,
        input_output_aliases={**s_alias, **{n_prefetch + i: o for i, o in (aliases or {}).items()}},
        compiler_params=_cparams(sem if side is None else ("arbitrary",) * len(grid)), **spec,
    )(*operands, *s_ops)


def _mm_tk(K):
    if K <= MM_TK:
        return K
    return max(t for t in range(LANE, MM_TK + 1, LANE) if K % t == 0)


def _mm(name, a, b, mode, out_dtypes, extras=(), epilogue=None, side=None):
    if mode == "nn":
        (M, K), (K2, N) = a.shape, b.shape
    elif mode == "nt":
        (M, K), (N, K2) = a.shape, b.shape
    else:
        (K, M), (K2, N) = a.shape, b.shape
    assert K == K2, (name, a.shape, b.shape)
    tm, tn, tk = min(MM_TM, M), min(MM_TN, N), _mm_tk(K)
    assert M % tm == 0 and N % tn == 0 and K % tk == 0, (name, M, N, K)
    ni, nj, nk = M // tm, N // tn, K // tk
    ne, no = len(extras), len(out_dtypes)
    if epilogue is None:
        epilogue = lambda acc: (acc,)
    s_ops, s_in, s_out, s_shape, s_alias, s_scr = _side_parts(side)
    n_in, n_sin, n_sout = 2 + ne, len(s_ops), len(s_out)

    def body(*refs):
        a_ref, b_ref = refs[:2]
        extra_refs = refs[2:n_in]
        out_refs = refs[n_in + n_sin:n_in + n_sin + no]
        rest = refs[n_in + n_sin + no + n_sout:]
        i, j, k = pl.program_id(0), pl.program_id(1), pl.program_id(2)
        if side is not None:
            side.hooks(refs[n_in:n_in + n_sin], refs[n_in + n_sin + no:n_in + n_sin + no + n_sout], rest[-2:],
                       (i == 0) & (j == 0) & (k == 0), (i == ni - 1) & (j == nj - 1) & (k == nk - 1))
        av, bv = a_ref[...].astype(BF16), b_ref[...].astype(BF16)
        prod = _nn(av, bv) if mode == "nn" else _nt(av, bv) if mode == "nt" else _tn(av, bv)

        def finish(acc):
            res = epilogue(acc, *[e[...] for e in extra_refs])
            for o_ref, r in zip(out_refs, res):
                o_ref[...] = r.astype(o_ref.dtype)

        if nk == 1:
            finish(prod)
        else:
            acc_ref = rest[0]

            @pl.when(k == 0)
            def _():
                acc_ref[...] = prod

            @pl.when((k > 0) & (k < nk - 1))
            def _():
                acc_ref[...] += prod

            @pl.when(k == nk - 1)
            def _():
                finish(acc_ref[...] + prod)

    if mode == "nn":
        a_spec = pl.BlockSpec((tm, tk), lambda i, j, k: (i, k))
        b_spec = pl.BlockSpec((tk, tn), lambda i, j, k: (k, j))
    elif mode == "nt":
        a_spec = pl.BlockSpec((tm, tk), lambda i, j, k: (i, k))
        b_spec = pl.BlockSpec((tn, tk), lambda i, j, k: (j, k))
    else:
        a_spec = pl.BlockSpec((tk, tm), lambda i, j, k: (k, i))
        b_spec = pl.BlockSpec((tk, tn), lambda i, j, k: (k, j))
    o_spec = pl.BlockSpec((tm, tn), lambda i, j, k: (i, j))
    sem = ("arbitrary",) * 3 if side is not None else ("parallel", "parallel", "arbitrary")
    outs = pl.pallas_call(
        body, name=name,
        grid=(ni, nj, nk),
        in_specs=[a_spec, b_spec] + [o_spec] * ne + s_in,
        out_specs=[o_spec] * no + s_out,
        out_shape=[jax.ShapeDtypeStruct((M, N), dt) for dt in out_dtypes] + s_shape,
        input_output_aliases=s_alias(n_in, no),
        scratch_shapes=([pltpu.VMEM((tm, tn), F32)] if nk > 1 else []) + s_scr,
        compiler_params=_cparams(sem),
    )(a, b, *extras, *s_ops)
    return outs[0] if len(outs) == 1 else outs


def _row_spec(tr, d):
    return pl.BlockSpec((tr, d), lambda i: (i, 0))


def _vec_spec(d):
    return pl.BlockSpec((1, d), lambda i: (0, 0))


def _rms_fwd(name, x, w, side=None):
    T, D = x.shape
    tr = min(ROW_TILE, T)

    def body(x_ref, w_ref, o_ref):
        xf = x_ref[...]
        r = lax.rsqrt(jnp.mean(xf * xf, axis=-1, keepdims=True) + EPS)
        o_ref[...] = (xf * r * w_ref[...]).astype(BF16)

    outs = _call_with_side(body, name, (T // tr,), [_row_spec(tr, D), _vec_spec(D)], [_row_spec(tr, D)],
                           [jax.ShapeDtypeStruct((T, D), BF16)], [], ("parallel",), (x, w), side)
    return outs[0] if side is None else outs


def _rms_bwd(name, dy, h, w, dres, side=None):
    T, D = h.shape
    tr = min(ROW_TILE, T)

    def body(dy_ref, h_ref, w_ref, dres_ref, dh_ref, dhb_ref, dw_ref):
        @pl.when(pl.program_id(0) == 0)
        def _():
            dw_ref[...] = jnp.zeros_like(dw_ref)

        hf, dyv = h_ref[...], dy_ref[...]
        r = lax.rsqrt(jnp.mean(hf * hf, axis=-1, keepdims=True) + EPS)
        xhat = hf * r
        dw_ref[...] += jnp.sum(dyv * xhat, axis=0, keepdims=True)
        dxh = dyv * w_ref[...]
        dh = dres_ref[...] + r * (dxh - xhat * jnp.mean(dxh * xhat, axis=-1, keepdims=True))
        dh_ref[...] = dh
        dhb_ref[...] = dh.astype(BF16)

    return _call_with_side(
        body, name, (T // tr,),
        [_row_spec(tr, D), _row_spec(tr, D), _vec_spec(D), _row_spec(tr, D)],
        [_row_spec(tr, D), _row_spec(tr, D), _vec_spec(D)],
        [jax.ShapeDtypeStruct((T, D), F32), jax.ShapeDtypeStruct((T, D), BF16), jax.ShapeDtypeStruct((1, D), F32)],
        [], ("arbitrary",), (dy, h, w, dres), side)


def _loss_head(h2, target, w):
    T, D = h2.shape
    tr = min(ROW_TILE, T)

    def body(h_ref, t_ref, w_ref, loss_ref, dh_ref, dhb_ref, dw_ref):
        @pl.when(pl.program_id(0) == 0)
        def _():
            dw_ref[...] = jnp.zeros_like(dw_ref)
            loss_ref[...] = jnp.zeros_like(loss_ref)

        hf, wv = h_ref[...], w_ref[...]
        r = lax.rsqrt(jnp.mean(hf * hf, axis=-1, keepdims=True) + EPS)
        xhat = hf * r
        diff = xhat * wv - t_ref[...]
        loss_ref[...] += 0.5 * jnp.sum(jnp.mean(diff * diff, axis=-1, keepdims=True))
        dyv = diff * (1.0 / D)
        dw_ref[...] += jnp.sum(dyv * xhat, axis=0, keepdims=True)
        dxh = dyv * wv
        dh = r * (dxh - xhat * jnp.mean(dxh * xhat, axis=-1, keepdims=True))
        dh_ref[...] = dh
        dhb_ref[...] = dh.astype(BF16)

    return pl.pallas_call(
        body, name="loss_head", grid=(T // tr,),
        in_specs=[_row_spec(tr, D), _row_spec(tr, D), _vec_spec(D)],
        out_specs=[_vec_spec(LANE), _row_spec(tr, D), _row_spec(tr, D), _vec_spec(D)],
        out_shape=[jax.ShapeDtypeStruct((1, LANE), F32), jax.ShapeDtypeStruct((T, D), F32),
                   jax.ShapeDtypeStruct((T, D), BF16), jax.ShapeDtypeStruct((1, D), F32)],
        compiler_params=_cparams(("arbitrary",)),
    )(h2, target, w)


def _gate_tiles(T, D):
    goff = 4 * _hgw() + 3 * _atw()
    tc = min(1024, D)
    assert goff % tc == 0 and D % tc == 0
    return min(ROW_TILE, T), tc, goff // tc, D // tc


def _merge(z, pa, pb, side=None):
    T, D = pa.shape
    tr, tc, g0, nd = _gate_tiles(T, D)

    def body(ga_ref, gb_ref, pa_ref, pb_ref, o_ref):
        o_ref[...] = (_sigmoid(ga_ref[...]) * pa_ref[...] + _sigmoid(gb_ref[...]) * pb_ref[...]).astype(BF16)

    t = pl.BlockSpec((tr, tc), lambda i, j: (i, j))
    outs = _call_with_side(
        body, "merge", (T // tr, nd),
        [pl.BlockSpec((tr, tc), lambda i, j: (i, g0 + j)), pl.BlockSpec((tr, tc), lambda i, j: (i, g0 + nd + j)), t, t],
        [t], [jax.ShapeDtypeStruct((T, D), BF16)], [], ("parallel", "parallel"), (z, z, pa, pb), side)
    return outs[0] if side is None else outs


def _dmerge(dm, z, pa, pb):
    T, D = pa.shape
    tr, tc, g0, nd = _gate_tiles(T, D)

    def body(dm_ref, ga_ref, gb_ref, pa_ref, pb_ref, dpa_ref, dpb_ref, dga_ref, dgb_ref):
        dmv = dm_ref[...]
        sa, sb = _sigmoid(ga_ref[...]), _sigmoid(gb_ref[...])
        dpa_ref[...] = (dmv * sa).astype(BF16)
        dpb_ref[...] = (dmv * sb).astype(BF16)
        dga_ref[...] = (dmv * pa_ref[...] * sa * (1.0 - sa)).astype(BF16)
        dgb_ref[...] = (dmv * pb_ref[...] * sb * (1.0 - sb)).astype(BF16)

    t = pl.BlockSpec((tr, tc), lambda i, j: (i, j))
    return pl.pallas_call(
        body, name="dmerge", grid=(T // tr, nd),
        in_specs=[t, pl.BlockSpec((tr, tc), lambda i, j: (i, g0 + j)),
                  pl.BlockSpec((tr, tc), lambda i, j: (i, g0 + nd + j)), t, t],
        out_specs=[t, t, t, t],
        out_shape=[jax.ShapeDtypeStruct((T, D), BF16)] * 4,
        compiler_params=_cparams(("parallel", "parallel")),
    )(dm, z, z, pa, pb)


def _hg_gates(xq, xf, lb):
    f = _sigmoid(xf)
    g = lb + (1.0 - lb) * f
    sq = _sigmoid(xq)
    return f, g, jnp.log(g), 1.0 - g, sq, xq * sq * (HG_D ** -0.5)


def _split2(x):
    hi = x.astype(BF16)
    return hi, (x - hi.astype(F32)).astype(BF16)


def _tri_sum(tri, x):
    hi, rest = x.astype(BF16), x - x.astype(BF16).astype(F32)
    mid, lo = _split2(rest)
    return _nn(tri, lo) + _nn(tri, mid) + _nn(tri, hi)


def _hg_decays(lg, tri_incl, rowi):
    b = _tri_sum(tri_incl, lg)
    b_last = jnp.sum(lg, axis=0, keepdims=True)
    b_mid = jnp.sum(jnp.where(rowi <= CHUNK // 2, lg, 0.0), axis=0, keepdims=True)
    return b, b_last, b_mid


HG_GROUP = 2


def _hg_in_specs(T):
    ng = HG_HEADS // HG_GROUP
    return [pl.BlockSpec((T, HG_GROUP * HG_D), lambda h, s=s: (0, s * ng + h)) for s in range(4)]


def _hg_fwd(z, lb_logits, hgw, side=None):
    T = z.shape[0]
    H, d, C, G = HG_HEADS, HG_D, CHUNK, HG_GROUP
    nc = T // C

    def body(hq_ref, hf_ref, hi_ref, hg_ref, lbl_ref, w_ref, ya_ref, o_ref, s_ref):
        lb_all = 1.0 / (1.0 + jnp.exp(lbl_ref[1:2, :] - lbl_ref[0:1, :]))
        wv = w_ref[...]
        row = lax.broadcasted_iota(jnp.int32, (C, C), 0)
        col = lax.broadcasted_iota(jnp.int32, (C, C), 1)
        tril = col <= row
        tri_incl = tril.astype(BF16)
        rowi = lax.broadcasted_iota(jnp.int32, (C, G * d), 0)
        lanes = [slice(hh * d, (hh + 1) * d) for hh in range(G)]
        per_head = lambda fn: jnp.concatenate([fn(hh, sl) for hh, sl in enumerate(lanes)], axis=1)
        wv_all = jnp.tile(wv, (1, G))

        def chunk(c, states):
            rows = pl.ds(pl.multiple_of(c * C, C), C)
            xq, xf, v, xg = hq_ref[rows, :], hf_ref[rows, :], hi_ref[rows, :], hg_ref[rows, :]
            _, _, lg, kk, _, q = _hg_gates(xq, xf, lb_all)
            b, b_last, b_mid = _hg_decays(lg, tri_incl, rowi)
            vb, qe = v.astype(BF16), (q * jnp.exp(b)).astype(BF16)
            qt = (q * jnp.exp(b - b_mid)).astype(BF16)
            kt = (kk * jnp.exp(jnp.minimum(b_mid - b, EXP_CLAMP))).astype(BF16)
            kd, e_last = (kk * jnp.exp(b_last - b)).astype(BF16), jnp.exp(b_last)
            for hh, st in enumerate(states):
                s_ref[hh, c] = st
            o = per_head(lambda hh, sl: _nt(qe[:, sl], states[hh].astype(BF16)))
            a = [jnp.where(tril, _nt(qt[:, sl], kt[:, sl]), 0.0).astype(BF16) for sl in lanes]
            o = o + per_head(lambda hh, sl: _nn(a[hh], vb[:, sl]))
            o_ref[rows, :] = o
            r = per_head(lambda hh, sl: jnp.broadcast_to(
                lax.rsqrt(jnp.mean(o[:, sl] * o[:, sl], axis=-1, keepdims=True) + EPS), (C, d)))
            ya_ref[rows, :] = (o * r * wv_all * (xg * _sigmoid(xg))).astype(BF16)
            return tuple(st * e_last[:, sl] + _tn(vb[:, sl], kd[:, sl]) for st, sl in zip(states, lanes))

        lax.fori_loop(0, nc, chunk, tuple(jnp.zeros((d, d), F32) for _ in range(G)))

    heads = pl.BlockSpec((T, G * d), lambda h: (0, h))
    return _call_with_side(
        body, "hg_fwd", (H // G,),
        _hg_in_specs(T) + [pl.BlockSpec((2, G * d), lambda h: (0, h)), pl.BlockSpec((1, d), lambda h: (0, 0))],
        [heads, heads, pl.BlockSpec((G, nc, d, d), lambda h: (h, 0, 0, 0))],
        [jax.ShapeDtypeStruct((T, H * d), BF16), jax.ShapeDtypeStruct((T, H * d), F32),
         jax.ShapeDtypeStruct((H, nc, d, d), F32)],
        [], ("parallel",), (z, z, z, z, lb_logits, hgw), side)


def _hg_bwd(z, o, dya, states, lb_logits, hgw, side=None):
    T = z.shape[0]
    H, d, C, G = HG_HEADS, HG_D, CHUNK, HG_GROUP
    nc = T // C
    scale = HG_D ** -0.5

    def body(hq_ref, hf_ref, hi_ref, hg_ref, o_ref, dy_ref, s_ref, lbl_ref, w_ref,
             dq_ref, df_ref, di_ref, dg_ref, dlbl_ref, dw_ref, acc_ref):
        lb_all = 1.0 / (1.0 + jnp.exp(lbl_ref[1:2, :] - lbl_ref[0:1, :]))
        wv = w_ref[...]
        row = lax.broadcasted_iota(jnp.int32, (C, C), 0)
        col = lax.broadcasted_iota(jnp.int32, (C, C), 1)
        tril = col <= row
        tri_incl = tril.astype(BF16)
        triu_incl = (col >= row).astype(BF16)
        rowi = lax.broadcasted_iota(jnp.int32, (C, G * d), 0)
        lanes = [slice(hh * d, (hh + 1) * d) for hh in range(G)]
        per_head = lambda fn: jnp.concatenate([fn(hh, sl) for hh, sl in enumerate(lanes)], axis=1)
        head_mean = lambda x: per_head(
            lambda hh, sl: jnp.broadcast_to(jnp.mean(x[:, sl], axis=-1, keepdims=True), (C, d)))
        wv_all = jnp.tile(wv, (1, G))
        lb = lb_all
        acc_ref[...] = jnp.zeros_like(acc_ref)

        @pl.when(pl.program_id(0) == 0)
        def _():
            dw_ref[...] = jnp.zeros_like(dw_ref)

        def chunk(i, carry):
            dsts, tail = carry
            c = nc - 1 - i
            rows = pl.ds(pl.multiple_of(c * C, C), C)
            xq, xf, v, xg = hq_ref[rows, :], hf_ref[rows, :], hi_ref[rows, :], hg_ref[rows, :]
            f, g, lg, kk, sq, q = _hg_gates(xq, xf, lb)
            b, b_last, b_mid = _hg_decays(lg, tri_incl, rowi)
            e_b, e_qm, e_km = jnp.exp(b), jnp.exp(b - b_mid), jnp.exp(jnp.minimum(b_mid - b, EXP_CLAMP))
            e_kl, e_last = jnp.exp(b_last - b), jnp.exp(b_last)
            ov, dy = o_ref[rows, :], dy_ref[rows, :]
            r = lax.rsqrt(head_mean(ov * ov) + EPS)
            xhat = ov * r
            sg = _sigmoid(xg)
            dxg = dy * xhat * wv_all * (sg * (1.0 + xg * (1.0 - sg)))
            dyn = dy * (xg * sg)
            acc_ref[0:1, :] += jnp.sum(dyn * xhat, axis=0, keepdims=True)
            dxh = dyn * wv_all
            dof = r * (dxh - xhat * head_mean(dxh * xhat))
            do, vb = dof.astype(BF16), v.astype(BF16)
            qe, kd, qt, kt = (q * e_b).astype(BF16), (kk * e_kl).astype(BF16), (q * e_qm).astype(BF16), (kk * e_km).astype(BF16)
            pm = [jnp.where(tril, _nt(do[:, sl], vb[:, sl]), 0.0).astype(BF16) for sl in lanes]
            am = [jnp.where(tril, _nt(qt[:, sl], kt[:, sl]), 0.0).astype(BF16) for sl in lanes]
            st = [_split2(s_ref[hh, c]) for hh in range(G)]
            ds = [_split2(x) for x in dsts]
            dq_state = per_head(lambda hh, sl: _nn(do[:, sl], st[hh][1]) + _nn(do[:, sl], st[hh][0]))
            dk_state = per_head(lambda hh, sl: _nn(vb[:, sl], ds[hh][1]) + _nn(vb[:, sl], ds[hh][0]))
            dq_intra = per_head(lambda hh, sl: _nn(pm[hh], kt[:, sl]))
            dk_intra = per_head(lambda hh, sl: _tn(pm[hh], qt[:, sl]))
            dv = per_head(lambda hh, sl: _tn(am[hh], do[:, sl]) + _nt(kd[:, sl], ds[hh][0]))
            new_dsts = tuple(x * e_last[:, sl] + _tn(do[:, sl], qe[:, sl]) for x, sl in zip(dsts, lanes))
            dq = dq_state * e_b + dq_intra * e_qm
            dk = dk_intra * e_km + dk_state * e_kl
            db = (qe.astype(F32) * dq_state + qt.astype(F32) * dq_intra
                  - kt.astype(F32) * dk_intra - kd.astype(F32) * dk_state)
            dlg = _tri_sum(triu_incl, db) + tail
            dgate = dlg / g - dk
            acc_ref[1:2, :] += jnp.sum(dgate * (1.0 - f), axis=0, keepdims=True)
            dq_ref[rows, :] = (dq * scale * (sq * (1.0 + xq * (1.0 - sq)))).astype(BF16)
            df_ref[rows, :] = (dgate * (1.0 - lb) * f * (1.0 - f)).astype(BF16)
            di_ref[rows, :] = dv.astype(BF16)
            dg_ref[rows, :] = dxg.astype(BF16)
            return new_dsts, tail + jnp.sum(db, axis=0, keepdims=True)

        lax.fori_loop(0, nc, chunk, (tuple(jnp.zeros((d, d), F32) for _ in range(G)), jnp.zeros((1, G * d), F32)))
        dw_ref[...] += functools.reduce(lambda p, q: p + q, [acc_ref[0:1, sl] for sl in lanes])
        dl0 = acc_ref[1:2, :] * lb_all * (1.0 - lb_all)
        dlbl_ref[0:1, :] = dl0
        dlbl_ref[1:2, :] = -dl0

    heads = pl.BlockSpec((T, G * d), lambda h: (0, h))
    logits = pl.BlockSpec((2, G * d), lambda h: (0, h))
    return _call_with_side(
        body, "hg_bwd", (H // G,),
        _hg_in_specs(T) + [heads, heads, pl.BlockSpec((G, nc, d, d), lambda h: (h, 0, 0, 0)), logits,
                           pl.BlockSpec((1, d), lambda h: (0, 0))],
        [heads, heads, heads, heads, logits, pl.BlockSpec((1, d), lambda h: (0, 0))],
        [jax.ShapeDtypeStruct((T, H * d), BF16)] * 4 + [jax.ShapeDtypeStruct((2, H * d), F32),
                                                        jax.ShapeDtypeStruct((1, d), F32)],
        [pltpu.VMEM((8, G * d), F32)], ("arbitrary",), (z, z, z, z, o, dya, states, lb_logits, hgw), side)


def _at_dims():
    pad = LEFT * CHUNK
    return pad, QB + pad, AT_HEADS * AT_DH // LANE, 4 * _hgw() // LANE


def _rel_of_period():
    pad, W, _, _ = _at_dims()
    n = jnp.arange(QB + W)
    return jnp.clip(pad - jnp.where(n < W, n, n - (QB + W)), -REL_CLIP, REL_CLIP) + REL_CLIP


def _bias_window(rel_bias):
    pad, W, _, _ = _at_dims()
    H, P = rel_bias.shape[0], QB + W
    per = rel_bias[:, _rel_of_period()]
    win = jnp.tile(per, (1, QB))[:, :QB * (P - 1)].reshape(H, QB, P - 1)[:, :, :W]
    t = jnp.arange(QB)[:, None]
    j = jnp.arange(W)[None, :]
    ok = (j // CHUNK >= t // CHUNK) & (j // CHUNK <= t // CHUNK + LEFT)
    return jnp.where(ok[None], win, NEG)


def _bias_window_grad(dbw):
    pad, W, _, _ = _at_dims()
    H, P = dbw.shape[0], QB + W
    flat = jnp.pad(dbw, ((0, 0), (0, 0), (0, P - 1 - W))).reshape(H, QB * (P - 1))
    per = jnp.pad(flat, ((0, 0), (0, QB))).reshape(H, QB, P).sum(axis=1)
    onehot = _rel_of_period()[:, None] == jnp.arange(2 * REL_CLIP + 1)[None, :]
    return jnp.dot(per, onehot.astype(F32), precision=HIGHEST)


def _at_stack(x):
    first = lax.broadcasted_iota(jnp.int32, x.shape, 1) < AT_DH
    return jnp.concatenate([jnp.where(first, x, 0.0), jnp.where(first, 0.0, x)], axis=0).astype(BF16)


def _at_unstack(x):
    first = lax.broadcasted_iota(jnp.int32, (QB, LANE), 1) < AT_DH
    return jnp.where(first, x[:QB], x[QB:])


def _at_softmax(qs, kw, bias_ref, qi):
    pad, W, _, _ = _at_dims()
    s = _nt(qs, kw) + bias_ref[...].reshape(2 * QB, W)
    valid = lax.broadcasted_iota(jnp.int32, (2 * QB, W), 1) + qi * QB >= pad
    s = jnp.where(valid, s, NEG)
    e = jnp.exp(s - jnp.max(s, axis=-1, keepdims=True))
    return e * (1.0 / jnp.sum(e, axis=-1, keepdims=True))


def _at_fwd(z, bias_win, side=None):
    T = z.shape[0]
    pad, W, HP, c0 = _at_dims()
    nq = T // QB

    def body(q_ref, k_ref, v_ref, bias_ref, o_ref, kpad, vpad):
        qi = pl.program_id(1)

        @pl.when(qi == 0)
        def _():
            kpad[0:pad, :] = jnp.zeros((pad, LANE), BF16)
            vpad[0:pad, :] = jnp.zeros((pad, LANE), BF16)
            kpad[pad:, :] = k_ref[...].astype(BF16)
            vpad[pad:, :] = v_ref[...].astype(BF16)

        win = pl.ds(pl.multiple_of(qi * QB, QB), W)
        kw, vw = kpad[win, :], vpad[win, :]
        p = _at_softmax(_at_stack(q_ref[...] * (AT_DH ** -0.5)), kw, bias_ref, qi)
        o_ref[...] = _at_unstack(_nn(p.astype(BF16), vw)).astype(BF16)

    full = lambda s: pl.BlockSpec((T, LANE), lambda hp, qi, s=s: (0, c0 + s * HP + hp))
    return _call_with_side(
        body, "at_fwd", (HP, nq),
        [pl.BlockSpec((QB, LANE), lambda hp, qi: (qi, c0 + hp)), full(1), full(2),
         pl.BlockSpec((2, QB, W), lambda hp, qi: (hp, 0, 0))],
        [pl.BlockSpec((QB, LANE), lambda hp, qi: (qi, hp))],
        [jax.ShapeDtypeStruct((T, HP * LANE), BF16)],
        [pltpu.VMEM((T + pad, LANE), BF16)] * 2, ("parallel", "arbitrary"), (z, z, z, bias_win), side)


def _at_bwd(z, dyb, bias_win, side=None):
    T = z.shape[0]
    pad, W, HP, c0 = _at_dims()
    nq = T // QB
    scale = AT_DH ** -0.5

    def body(q_ref, k_ref, v_ref, do_ref, bias_ref, dq_ref, dk_ref, dv_ref, dbias_ref, kpad, vpad, dkpad, dvpad):
        qi = pl.program_id(1)

        @pl.when(qi == 0)
        def _():
            kpad[0:pad, :] = jnp.zeros((pad, LANE), BF16)
            vpad[0:pad, :] = jnp.zeros((pad, LANE), BF16)
            kpad[pad:, :] = k_ref[...].astype(BF16)
            vpad[pad:, :] = v_ref[...].astype(BF16)
            dkpad[...] = jnp.zeros_like(dkpad)
            dvpad[...] = jnp.zeros_like(dvpad)
            dbias_ref[...] = jnp.zeros_like(dbias_ref)

        win = pl.ds(pl.multiple_of(qi * QB, QB), W)
        kw, vw = kpad[win, :], vpad[win, :]
        qs, dos = _at_stack(q_ref[...] * scale), _at_stack(do_ref[...])
        p = _at_softmax(qs, kw, bias_ref, qi)
        dp = _nt(dos, vw)
        ds = p * (dp - jnp.sum(p * dp, axis=-1, keepdims=True))
        dbias_ref[...] += ds.reshape(2, QB, W)
        dsb = ds.astype(BF16)
        dq_ref[...] = (_at_unstack(_nn(dsb, kw)) * scale).astype(BF16)
        lanes_win = pl.ds(pl.multiple_of(qi * QB, QB), W)
        dkpad[:, lanes_win] += _tn(qs, dsb)
        dvpad[:, lanes_win] += _tn(dos, p.astype(BF16))

        @pl.when(qi == nq - 1)
        def _():
            dk_ref[...] = dkpad[:, pad:].T.astype(BF16)
            dv_ref[...] = dvpad[:, pad:].T.astype(BF16)

    full = lambda s: pl.BlockSpec((T, LANE), lambda hp, qi, s=s: (0, c0 + s * HP + hp))
    blk = pl.BlockSpec((QB, LANE), lambda hp, qi: (qi, hp))
    col = pl.BlockSpec((T, LANE), lambda hp, qi: (0, hp))
    bw = pl.BlockSpec((2, QB, W), lambda hp, qi: (hp, 0, 0))
    return _call_with_side(
        body, "at_bwd", (HP, nq),
        [pl.BlockSpec((QB, LANE), lambda hp, qi: (qi, c0 + hp)), full(1), full(2), blk, bw],
        [blk, col, col, bw],
        [jax.ShapeDtypeStruct((T, HP * LANE), BF16)] * 3 + [jax.ShapeDtypeStruct(bias_win.shape, F32)],
        [pltpu.VMEM((T + pad, LANE), BF16)] * 2 + [pltpu.VMEM((LANE, T + pad), F32)] * 2,
        ("parallel", "arbitrary"), (z, z, z, dyb, bias_win), side)


def _piece_tiles(name, full_shape):
    pr, pc = _piece_shape(name, full_shape)
    tr = min(ROW_TILE, pr)
    assert pr % tr == 0
    nt = pr // tr
    if name in ROW_SHARDED:
        return tr, nt, lambda q, half, i: ((2 * q + half) * nt + i, 0)
    return tr, nt, lambda q, half, i: (half * nt + i, q)


def _cast_into_full(name, wq, place, also_alone=False):
    full = _full_shape(name, wq.shape)
    pc = wq.shape[1]
    tr, nt, at = _piece_tiles(name, full)

    def body(place_ref, w_ref, *o_refs):
        for o_ref in o_refs:
            o_ref[...] = w_ref[...].astype(BF16)

    quarter = pl.BlockSpec((tr, pc), lambda h, i, s: (h * nt + i, 0))
    outs = _call_with_side(
        body, "cast_" + name, (2, nt), [quarter],
        [pl.BlockSpec((tr, pc), lambda h, i, s: at(s[0], h, i))] + [quarter] * also_alone,
        [jax.ShapeDtypeStruct(full, BF16)] + [jax.ShapeDtypeStruct(wq.shape, BF16)] * also_alone,
        [], ("parallel", "parallel"), (place, wq), None, n_prefetch=1)
    return tuple(outs) if also_alone else outs[0]


def _g_w_in_half(u1, dz, place, own, side=None):
    T, K = u1.shape
    N = dz.shape[1]
    hk, tn = K // 2, min(MM_TN, N)
    half = (lambda s: s[1]) if own else (lambda s: 1 - s[1])

    def body(place_ref, a_ref, b_ref, o_ref):
        o_ref[...] = _tn(a_ref[...], b_ref[...]).astype(BF16)

    outs = _call_with_side(
        body, "g_w_in_keep" if own else "g_w_in_send", (N // tn,),
        [pl.BlockSpec((T, hk), lambda j, s: (0, half(s))), pl.BlockSpec((T, tn), lambda j, s: (0, j))],
        [pl.BlockSpec((hk, tn), lambda j, s: (0, j))], [jax.ShapeDtypeStruct((hk, N), BF16)],
        [], ("parallel",), (place, u1, dz), side, n_prefetch=1)
    return outs[0] if side is None else outs


def _chip_sum(name, grad, theirs, place, kept_rows=False):
    pr, pc = theirs.shape[1:]
    tr, nt, at = _piece_tiles(name, (2 * grad.shape[0], grad.shape[1]) if kept_rows else grad.shape)
    if kept_rows:
        at = lambda q, half, i: (i, q)

    def body(place_ref, g_ref, t_ref, o_ref):
        o_ref[...] = (g_ref[...].astype(F32) + t_ref[...].astype(F32)).astype(BF16)

    piece = pl.BlockSpec((None, tr, pc), lambda q, i, s: (q, i, 0))
    return pl.pallas_call(
        body, name="chip_sum_" + name,
        grid_spec=pltpu.PrefetchScalarGridSpec(
            num_scalar_prefetch=1, grid=(4, nt),
            in_specs=[pl.BlockSpec((tr, pc), lambda q, i, s: at(q, s[1], i)), piece], out_specs=piece),
        out_shape=jax.ShapeDtypeStruct(theirs.shape, BF16),
        compiler_params=_cparams(("parallel", "parallel")),
    )(place, grad, theirs)


def _piece_sum(name, chip_sums, got, place):
    pr, pc = chip_sums.shape[1:]
    tr = min(ROW_TILE, pr)

    def body(place_ref, own_ref, got_ref, o_ref):
        o_ref[...] = (own_ref[...].astype(F32) + got_ref[0].astype(F32) + got_ref[1].astype(F32)
                      + got_ref[2].astype(F32))

    return pl.pallas_call(
        body, name="piece_sum_" + name,
        grid_spec=pltpu.PrefetchScalarGridSpec(
            num_scalar_prefetch=1, grid=(pr // tr,),
            in_specs=[pl.BlockSpec((None, tr, pc), lambda i, s: (s[0], i, 0)),
                      pl.BlockSpec((3, tr, pc), lambda i, s: (0, i, 0))],
            out_specs=pl.BlockSpec((tr, pc), lambda i, s: (i, 0))),
        out_shape=jax.ShapeDtypeStruct((pr, pc), F32),
        compiler_params=_cparams(("parallel",)),
    )(place, chip_sums, got)


def _adam_quarter(name, w, m, v, g_mine, g_sib, place, side=None):
    pr, pc = g_mine.shape
    tr = min(ROW_TILE // 2, pr)
    nt = pr // tr

    def body(place_ref, w_ref, m_ref, v_ref, gm_ref, gs_ref, go_ref, d_ref, mo_ref, vo_ref):
        g = jnp.where(pl.program_id(0) == place_ref[1], gm_ref[...], gs_ref[...])
        delta, mn, vn = _adam_math(w_ref[...], g, m_ref[...], v_ref[...])
        go_ref[...] = g
        d_ref[...] = delta
        mo_ref[...] = mn
        vo_ref[...] = vn

    quarter = pl.BlockSpec((tr, pc), lambda h, i, s: (h * nt + i, 0))
    mine = pl.BlockSpec((tr, pc), lambda h, i, s: (jnp.where(h == s[1], i, 0), 0))
    sib = pl.BlockSpec((tr, pc), lambda h, i, s: (jnp.where(h == s[1], 0, i), 0))
    return _call_with_side(
        body, "adam_" + name, (2, nt), [quarter, quarter, quarter, mine, sib], [quarter] * 4,
        [jax.ShapeDtypeStruct(w.shape, F32)] * 4, [], ("parallel", "parallel"),
        (place, w, m, v, g_mine, g_sib), side, n_prefetch=1)


def _adam_half(name, w, m, v, g, place, own, prev=None):
    pr, pc = g.shape
    tr = min(ROW_TILE // 2, pr)
    nt = pr // tr
    half = (lambda s: s[1]) if own else (lambda s: 1 - s[1])

    def body(place_ref, w_ref, m_ref, v_ref, g_ref, *rest):
        go_ref, d_ref, mo_ref, vo_ref = rest[-4:]
        gv = g_ref[...]
        delta, mn, vn = _adam_math(w_ref[...], gv, m_ref[...], v_ref[...])
        go_ref[...] = gv
        d_ref[...] = delta
        mo_ref[...] = mn
        vo_ref[...] = vn

    quarter = pl.BlockSpec((tr, pc), lambda i, s: (half(s) * nt + i, 0))
    n_prev = 0 if prev is None else 4
    return _call_with_side(
        body, "adam_%s_%s" % (name, "own" if own else "sibling"), (nt,),
        [quarter, quarter, quarter, pl.BlockSpec((tr, pc), lambda i, s: (i, 0))] + [ANY] * n_prev, [quarter] * 4,
        [jax.ShapeDtypeStruct(w.shape, F32)] * 4, [], ("parallel",),
        (place, w, m, v, g) + tuple(prev or ()), None, n_prefetch=1,
        aliases={4 + t: t for t in range(n_prev)})


def _adam_math(w, g, m, v):
    m = ADAM_B1 * m + (1.0 - ADAM_B1) * g
    v = ADAM_B2 * v + (1.0 - ADAM_B2) * (g * g)
    m_hat = m / (1.0 - ADAM_B1 ** ADAM_STEP)
    v_hat = v / (1.0 - ADAM_B2 ** ADAM_STEP)
    return -ADAM_LR * (m_hat / (jnp.sqrt(v_hat) + ADAM_EPS) + ADAM_WD * w), m, v


WEIGHTS = ("w_in", "w_branch_a", "w_branch_b", "w_out", "w_up", "w_down")
ROW_SHARDED = ("w_out", "w_down")
ANY = pl.BlockSpec(memory_space=pl.ANY)
MESH = pl.DeviceIdType.MESH


def _place():
    x, y, c = lax.axis_index("x"), lax.axis_index("y"), lax.axis_index("c")
    chips = [(1 - x, y), (x, 1 - y), (1 - x, 1 - y)]
    return x, y, c, 2 * x + y, chips, [2 * cx + cy for cx, cy in chips]


def _piece(full_ref, name, q, half):
    K, N = full_ref.shape
    if name in ROW_SHARDED:
        rows = K // 8
        return full_ref.at[pl.ds(q * (2 * rows) + half * rows, rows), :]
    return full_ref.at[pl.ds(half * (K // 2), K // 2), pl.ds(q * (N // 4), N // 4)]


def _piece_shape(name, full_shape):
    K, N = full_shape
    return (K // 8, N) if name in ROW_SHARDED else (K // 2, N // 4)


def _full_shape(name, quarter_shape):
    Kq, Nq = quarter_shape
    return (4 * Kq, Nq) if name in ROW_SHARDED else (Kq, 4 * Nq)


def _remote(src, dst, send_sem, recv_sem, device):
    return pltpu.make_async_remote_copy(src_ref=src, dst_ref=dst, send_sem=send_sem, recv_sem=recv_sem,
                                        device_id=device, device_id_type=MESH)


def _z_part(u1, w_in, z_prev, place, k0, count, side=None, own_quarter=False):
    T, K = u1.shape
    nq = w_in.shape[1] if own_quarter else w_in.shape[1] // 4
    N = 4 * nq
    tn = nq // 2 if (nq // 2) % LANE == 0 else nq
    tm = min(MM_TM, T)
    per = nq // tn
    col = lambda g, j, s: (s[0] ^ (k0 + g)) * per + j
    ins = [pl.BlockSpec((tm, K), lambda g, i, j, s: (i, 0)),
           pl.BlockSpec((K, tn), (lambda g, i, j, s: (0, j)) if own_quarter else (lambda g, i, j, s: (0, col(g, j, s))))]
    operands = [place, u1, w_in]
    if z_prev is not None:
        ins.append(ANY)
        operands.append(z_prev)

    def body(place_ref, a_ref, b_ref, *rest):
        rest[-1][...] = _nn(a_ref[...], b_ref[...].astype(BF16))

    return _call_with_side(
        body, "z_part_%d" % k0, (count, T // tm, per), ins,
        [pl.BlockSpec((tm, tn), lambda g, i, j, s: (i, col(g, j, s)))], [jax.ShapeDtypeStruct((T, N), F32)],
        [], ("parallel",) * 3, tuple(operands), side, n_prefetch=1, aliases={} if z_prev is None else {2: 0},
        borrow={0: 1} if side is not None and side.aliased and side.aliased[0] is w_in else None)


def _rows(ref, span):
    return ref if span is None else ref.at[pl.ds(span[0], span[1]), :]


def _gather_moves(items):
    count = {"near": lambda arg: 2, "far": lambda arg: 1, "pass": len}

    def build(reads, aliased, fresh, send_sems, recv_sems, off=0):
        x, y, c, p, chips, chip_ids = _place()
        south = c == 0
        far_src = jnp.where(south, chip_ids[0], chip_ids[1])
        far_dst = (jnp.where(south, x, 1 - x), jnp.where(south, 1 - y, y), c)
        out = []

        def add(ref, device):
            k = off + len(out)
            out.append(_remote(ref, ref, send_sems.at[k], recv_sems.at[k], device))

        for (name, _, moves), ref in zip(items, aliased):
            for kind, arg in moves:
                if kind == "near":
                    for chip in chips[:2]:
                        add(_rows(_piece(ref, name, p, c), arg), (*chip, c))
                elif kind == "far":
                    add(_rows(_piece(ref, name, far_src, c), arg), far_dst)
                else:
                    for j in arg:
                        add(_piece(ref, name, chip_ids[j], c), (x, y, 1 - c))
        return out

    nsem = sum(count[kind](arg) for _, _, moves in items for kind, arg in moves)
    return _Side(build, nsem, aliased=[a for _, a, _ in items])


def _ici_far(names, fulls, rows=None):
    return _gather_moves([(n, a, [("far", r)]) for n, a, r in zip(names, fulls, rows or [None] * len(names))])


def _d2d_gather(names, fulls, which=(0, 1, 2)):
    return _gather_moves([(n, a, [("pass", which)]) for n, a in zip(names, fulls)])


def _sib_send(names, grads):
    def build(reads, aliased, fresh, send_sems, recv_sems, off=0):
        x, y, c, _, _, _ = _place()
        out = []
        for i, name in enumerate(names):
            for q in range(4):
                k = off + 4 * i + q
                out.append(_remote(_piece(reads[i], name, q, 1 - c), fresh[i].at[q], send_sems.at[k], recv_sems.at[k],
                                   (x, y, 1 - c)))
        return out

    shapes = [jax.ShapeDtypeStruct((4,) + _piece_shape(name, g.shape), BF16) for name, g in zip(names, grads)]
    return _Side(build, 4 * len(names), reads=grads, fresh=shapes)


def _sib_send_half(sent):
    K2, N = sent.shape

    def build(reads, aliased, fresh, send_sems, recv_sems, off=0):
        x, y, c, _, _, _ = _place()
        return [_remote(reads[0].at[:, pl.ds(q * (N // 4), N // 4)], fresh[0].at[q], send_sems.at[off + q],
                        recv_sems.at[off + q], (x, y, 1 - c)) for q in range(4)]

    return _Side(build, 4, reads=[sent], fresh=[jax.ShapeDtypeStruct((4, K2, N // 4), BF16)])


HBM = pl.BlockSpec(memory_space=pltpu.HBM)
SEM = pl.BlockSpec(memory_space=pltpu.SEMAPHORE)


def _exchange_copies(s_refs, land_refs, send_sems, recv_sems):
    _, _, c, _, chips, chip_ids = _place()
    return [_remote(s_ref.at[cid], land_ref.at[j], send_sems.at[3 * i + j], recv_sems.at[3 * i + j], (*chip, c))
            for i, (s_ref, land_ref) in enumerate(zip(s_refs, land_refs))
            for j, (chip, cid) in enumerate(zip(chips, chip_ids))]


def _exchange_start(name, chip_sums):
    n = len(chip_sums)

    def body(*refs):
        for cp in _exchange_copies(refs[:n], refs[n:2 * n], refs[2 * n], refs[2 * n + 1]):
            cp.start()
        refs[-1][...] = jnp.zeros_like(refs[-1])

    lands = [jax.ShapeDtypeStruct((3,) + s.shape[1:], s.dtype) for s in chip_sums]
    hbm = lambda a: pltpu.with_memory_space_constraint(a, pltpu.HBM)
    outs = pl.pallas_call(
        body, name="exchange_start_" + name,
        out_shape=(pltpu.SemaphoreType.DMA((3 * n,)), pltpu.SemaphoreType.DMA((3 * n,)),
                   *[pltpu.HBM(a.shape, a.dtype) for a in chip_sums + lands], jax.ShapeDtypeStruct((8, LANE), F32)),
        in_specs=(HBM,) * (2 * n), out_specs=(SEM, SEM) + (HBM,) * (2 * n) + (pl.BlockSpec(memory_space=pltpu.VMEM),),
        input_output_aliases={i: 2 + i for i in range(2 * n)},
        compiler_params=pltpu.CompilerParams(has_side_effects=pltpu.SideEffectType.DATAFLOW_SIDE_EFFECTING),
    )(*[hbm(s) for s in chip_sums], *[hbm(lax.empty(a.shape, a.dtype)) for a in lands])
    return outs[0], outs[1], list(outs[2:2 + n]), list(outs[2 + n:2 + 2 * n]), outs[-1]


def _exchange_wait(name, flight, after):
    send_sems, recv_sems, s_thru, land_thru, _ = flight
    n = len(s_thru)

    def body(*refs):
        for cp in _exchange_copies(refs[:n], refs[n:2 * n], refs[2 * n], refs[2 * n + 1]):
            cp.wait_send()
            cp.wait_recv()

    outs = pl.pallas_call(
        body, name="exchange_wait_" + name,
        out_shape=tuple(pltpu.HBM(a.shape, a.dtype) for a in s_thru + land_thru),
        in_specs=(HBM,) * (2 * n) + (SEM, SEM, ANY), out_specs=(HBM,) * (2 * n),
        input_output_aliases={i: i for i in range(2 * n)},
        compiler_params=pltpu.CompilerParams(has_side_effects=pltpu.SideEffectType.DATAFLOW_SIDE_EFFECTING),
    )(*s_thru, *land_thru, send_sems, recv_sems, after)
    return list(outs[:n]), list(outs[n:])


def _move_copies(kind, name, f_ref, send_sems, recv_sems):
    return _gather_moves([(name, None, [(kind, None)])]).build([], [f_ref], [], send_sems, recv_sems)


def _move_start(kind, name, full):
    def body(f_ref, send_sems, recv_sems, f_thru, token):
        for cp in _move_copies(kind, name, f_ref, send_sems, recv_sems):
            cp.start()
        token[...] = jnp.zeros_like(token)

    return pl.pallas_call(
        body, name=kind + "_start_" + name,
        out_shape=(pltpu.SemaphoreType.DMA((2,)), pltpu.SemaphoreType.DMA((2,)), pltpu.HBM(full.shape, full.dtype),
                   jax.ShapeDtypeStruct((8, LANE), F32)),
        in_specs=(HBM,), out_specs=(SEM, SEM, HBM, pl.BlockSpec(memory_space=pltpu.VMEM)),
        input_output_aliases={0: 2},
        compiler_params=pltpu.CompilerParams(has_side_effects=pltpu.SideEffectType.DATAFLOW_SIDE_EFFECTING),
    )(pltpu.with_memory_space_constraint(full, pltpu.HBM))


def _move_wait(kind, name, flight, after):
    send_sems, recv_sems, f_thru, _ = flight

    def body(f_ref, send_sems, recv_sems, after_ref, f_out):
        for cp in _move_copies(kind, name, f_ref, send_sems, recv_sems):
            cp.wait_send()
            cp.wait_recv()

    return pl.pallas_call(
        body, name=kind + "_wait_" + name, out_shape=pltpu.HBM(f_thru.shape, f_thru.dtype),
        in_specs=(HBM, SEM, SEM, ANY), out_specs=HBM, input_output_aliases={0: 0},
        compiler_params=pltpu.CompilerParams(has_side_effects=pltpu.SideEffectType.DATAFLOW_SIDE_EFFECTING),
    )(f_thru, send_sems, recv_sems, after)


def _share_copy(h_ref, land_ref, send_sems, recv_sems):
    x, y, c, _, _, _ = _place()
    return _remote(h_ref, land_ref, send_sems.at[0], recv_sems.at[0], (x, y, 1 - c))


def _share_start(name, half):
    def body(h_ref, land_ref, send_sems, recv_sems, h_thru, land_thru, token):
        _share_copy(h_ref, land_ref, send_sems, recv_sems).start()
        token[...] = jnp.zeros_like(token)

    hbm = lambda a: pltpu.with_memory_space_constraint(a, pltpu.HBM)
    return pl.pallas_call(
        body, name="share_start_" + name,
        out_shape=(pltpu.SemaphoreType.DMA((1,)), pltpu.SemaphoreType.DMA((1,)), pltpu.HBM(half.shape, half.dtype),
                   pltpu.HBM(half.shape, half.dtype), jax.ShapeDtypeStruct((8, LANE), F32)),
        in_specs=(HBM, HBM), out_specs=(SEM, SEM, HBM, HBM, pl.BlockSpec(memory_space=pltpu.VMEM)),
        input_output_aliases={0: 2, 1: 3},
        compiler_params=pltpu.CompilerParams(has_side_effects=pltpu.SideEffectType.DATAFLOW_SIDE_EFFECTING),
    )(hbm(half), hbm(lax.empty(half.shape, half.dtype)))


def _share_wait(name, flight, after):
    send_sems, recv_sems, h_thru, land_thru, _ = flight

    def body(h_ref, land_ref, send_sems, recv_sems, after_ref, h_out, land_out):
        cp = _share_copy(h_ref, land_ref, send_sems, recv_sems)
        cp.wait_send()
        cp.wait_recv()

    return pl.pallas_call(
        body, name="share_wait_" + name,
        out_shape=(pltpu.HBM(h_thru.shape, h_thru.dtype), pltpu.HBM(land_thru.shape, land_thru.dtype)),
        in_specs=(HBM, HBM, SEM, SEM, ANY), out_specs=(HBM, HBM), input_output_aliases={0: 0, 1: 1},
        compiler_params=pltpu.CompilerParams(has_side_effects=pltpu.SideEffectType.DATAFLOW_SIDE_EFFECTING),
    )(h_thru, land_thru, send_sems, recv_sems, after)[1]


def _sib_share(halves):
    def build(reads, aliased, fresh, send_sems, recv_sems, off=0):
        x, y, c, _, _, _ = _place()
        return [_remote(reads[i], fresh[i], send_sems.at[off + i], recv_sems.at[off + i], (x, y, 1 - c))
                for i in range(len(halves))]

    return _Side(build, len(halves), reads=halves, fresh=[jax.ShapeDtypeStruct(h.shape, F32) for h in halves])


def _run_side(name, side):
    nr, na = len(side.reads), len(side.aliased)

    def body(*refs):
        n_in, n_out = nr + na, na + len(side.fresh)
        outs = refs[n_in:n_in + n_out]
        copies = side.build(refs[:nr], outs[:na], outs[na:], *refs[-2:])
        for cp in copies:
            cp.start()
        for cp in copies:
            cp.wait()

    return pl.pallas_call(
        body, name=name, in_specs=side.in_specs(), out_specs=side.out_specs(), out_shape=side.out_shape(),
        input_output_aliases=side.aliases(0, 0), scratch_shapes=side.scratch(),
    )(*side.operands())


def _small_allreduce_adam(gpart, w, m, v, after):
    R = gpart.shape[0]

    def body(g_ref, w_ref, m_ref, v_ref, after_ref, go_ref, d_ref, mo_ref, vo_ref, buf, send_sems, recv_sems):
        x, y, c = lax.axis_index("x"), lax.axis_index("y"), lax.axis_index("c")
        me = 4 * x + 2 * y + c
        buf[me] = g_ref[...]
        copies = []
        for k in range(1, 8):
            fx, fy, fc = (k >> 2) & 1, (k >> 1) & 1, k & 1
            peer = (1 - x if fx else x, 1 - y if fy else y, 1 - c if fc else c)
            cp = _remote(g_ref, buf.at[me], send_sems.at[k - 1], recv_sems.at[k - 1], peer)
            cp.start()
            copies.append((cp, 4 * peer[0] + 2 * peer[1] + peer[2]))
        for k, (cp, pid) in enumerate(copies):
            _remote(g_ref, buf.at[pid], send_sems.at[k], recv_sems.at[k], (x, y, c)).wait_recv()
        for cp, _ in copies:
            cp.wait_send()
        g = buf[0]
        for d in range(1, 8):
            g = g + buf[d]
        delta, mn, vn = _adam_math(w_ref[...], g, m_ref[...], v_ref[...])
        go_ref[...] = g
        d_ref[...] = delta
        mo_ref[...] = mn
        vo_ref[...] = vn

    vm = pl.BlockSpec(memory_space=pltpu.VMEM)
    return pl.pallas_call(
        body, name="small_allreduce_adam",
        in_specs=[vm] * 4 + [ANY], out_specs=[vm] * 4,
        out_shape=[jax.ShapeDtypeStruct((R, LANE), F32)] * 4,
        scratch_shapes=[pltpu.VMEM((8, R, LANE), F32), pltpu.SemaphoreType.DMA((7,)), pltpu.SemaphoreType.DMA((7,))],
    )(gpart, w, m, v, after)


def _pack(arrs):
    flat = jnp.concatenate([a.reshape(-1).astype(F32) for a in arrs])
    rows = -(-flat.shape[0] // (8 * LANE)) * 8
    return jnp.pad(flat, (0, rows * LANE - flat.shape[0])).reshape(rows, LANE)


def _unpack(packed, like):
    flat, out, off = packed.reshape(-1), [], 0
    for a in like:
        out.append(flat[off:off + a.size].reshape(a.shape))
        off += a.size
    return out


def kernel(x, w_in, lb_logits, hg_norm_w, rel_bias, w_branch_a, w_branch_b, w_out, norm_mix_w, norm_mlp_w, w_up, w_down, norm_final_w, loss_target, m_w_in, m_lb_logits, m_hg_norm_w, m_rel_bias, m_w_branch_a, m_w_branch_b, m_w_out, m_norm_mix_w, m_norm_mlp_w, m_w_up, m_w_down, m_norm_final_w, v_w_in, v_lb_logits, v_hg_norm_w, v_rel_bias, v_w_branch_a, v_w_branch_b, v_w_out, v_norm_mix_w, v_norm_mlp_w, v_w_up, v_w_down, v_norm_final_w):
    T, D = x.shape[1], x.shape[2]
    x2, tgt = x.reshape(T, D), loss_target.reshape(T, D)
    big = dict(w_in=(w_in, m_w_in, v_w_in), w_branch_a=(w_branch_a, m_w_branch_a, v_w_branch_a),
               w_branch_b=(w_branch_b, m_w_branch_b, v_w_branch_b), w_out=(w_out, m_w_out, v_w_out),
               w_up=(w_up, m_w_up, v_w_up), w_down=(w_down, m_w_down, v_w_down))
    big = {k: tuple(a[0] for a in v) for k, v in big.items()}
    nfw = norm_final_w.reshape(1, D)

    place = jnp.stack([2 * lax.axis_index("x") + lax.axis_index("y"), lax.axis_index("c")]).astype(jnp.int32)
    small3 = ["w_branch_a", "w_branch_b", "w_out"]

    def span(name, lo, hi):
        pr = big[name][0].shape[0] // 2
        return (pr * lo // 16, pr * (hi - lo) // 16)

    w_in_full, w_in_own = _cast_into_full("w_in", big["w_in"][0], place, also_alone=True)
    flight_in = _move_start("near", "w_in", w_in_full)
    Wf = {name: _cast_into_full(name, big[name][0], place) for name in WEIGHTS if name != "w_in"}

    u1 = _rms_fwd("norm_mix", x2, norm_mix_w, side=_after(flight_in[-1]))
    z = _z_part(u1[0], w_in_own, None, place, 0, 1, own_quarter=True)[0]
    u1 = u1[0]
    Wf["w_in"] = _move_wait("near", "w_in", flight_in, z)

    def carried(**moves):
        def arg(n, k, a):
            if k == "pass":
                return a[0] if a else (0, 1, 2)
            return span(n, *a) if a else None

        return _gather_moves([(n, Wf[n], [(k, arg(n, k, a)) for k, *a in ms]) for n, ms in moves.items()]), list(moves)

    def land(names, outs):
        Wf.update(zip(names, outs[-len(names):]))
        return outs[:-len(names)]

    side, names = carried(w_in=[("pass", (0, 1))], w_out=[("near", 0, 8)])
    land(names, _run_side("pass_w_in_near", side))
    flight_in = _move_start("far", "w_in", Wf["w_in"])
    z = _z_part(u1, flight_in[2], z, place, 1, 2)[0]
    Wf["w_in"] = _move_wait("far", "w_in", flight_in, z)
    side, names = carried(w_in=[("pass", (2,))], w_branch_a=[("near",)], w_branch_b=[("near",)])
    land(names, _run_side("pass_w_in_far", side))
    side, names = carried(w_branch_a=[("far",)], w_branch_b=[("far",)], w_out=[("near", 8, 16)])
    (z,) = land(names, _z_part(u1, Wf["w_in"], z, place, 3, 1, side=side))
    side, names = carried(w_branch_a=[("pass",)], w_branch_b=[("pass",)], w_up=[("near", 0, 10)])
    ya, o_hg, states = land(names, _hg_fwd(z, lb_logits, hg_norm_w, side=side))
    bias_win = _bias_window(rel_bias[0])
    side, names = carried(w_out=[("far",)], w_up=[("near", 10, 16), ("far", 0, 10)], w_down=[("near", 0, 3)])
    (yb,) = land(names, _at_fwd(z, bias_win, side=side))
    side, names = carried(w_out=[("pass",)], w_up=[("far", 10, 14)])
    (pa,) = land(names, _mm("branch_a", ya, Wf["w_branch_a"], "nn", [BF16], side=side))
    side, names = carried(w_up=[("far", 14, 16)], w_down=[("near", 3, 4)])
    (pb,) = land(names, _mm("branch_b", yb, Wf["w_branch_b"], "nn", [BF16], side=side))
    side, names = carried(w_down=[("near", 4, 8)])
    (merged,) = land(names, _merge(z, pa, pb, side=side))
    add = lambda acc, res: (acc + res,)
    side, names = carried(w_up=[("pass",)], w_down=[("near", 8, 12)])
    (h1,) = land(names, _mm("out_proj", merged, Wf["w_out"], "nn", [F32], extras=[x2], epilogue=add, side=side))
    side, names = carried(w_down=[("near", 12, 14)])
    (u2,) = land(names, _rms_fwd("norm_mlp", h1, norm_mlp_w, side=side))
    relu2 = lambda acc: (acc, jnp.square(jnp.maximum(acc, 0.0)))
    side, names = carried(w_down=[("near", 14, 16), ("far", 0, 14)])
    a_pre, act = land(names, _mm("mlp_up", u2, Wf["w_up"], "nn", [F32, BF16], epilogue=relu2, side=side))
    (Wf["w_down"],) = _run_side("far_w_down", _ici_far(["w_down"], [Wf["w_down"]], rows=[span("w_down", 14, 16)]))
    (Wf["w_down"],) = _run_side("pass_w_down", _d2d_gather(["w_down"], [Wf["w_down"]]))
    h2 = _mm("mlp_down", act, Wf["w_down"], "nn", [F32], extras=[h1], epilogue=add)
    loss_part, dh2, dh2b, d_nf = _loss_head(h2, tgt, nfw)

    drelu2 = lambda acc, a: (acc * (2.0 * jnp.maximum(a, 0.0)),)
    da = _mm("d_act", dh2b, Wf["w_down"], "nt", [BF16], extras=[a_pre], epilogue=drelu2)
    G = {}
    G["w_down"] = _mm("g_w_down", act, dh2b, "tn", [BF16])
    G["w_up"] = _mm("g_w_up", u2, da, "tn", [BF16])
    T_, S_, GOT = {}, {}, {}
    du2, T_["w_down"], T_["w_up"] = _mm("d_u2", da, Wf["w_up"], "nt", [F32],
                                        side=_sib_send(["w_down", "w_up"], [G["w_down"], G["w_up"]]))
    mlp2 = ["w_down", "w_up"]
    flight_mlp = _exchange_start("mlp", [_chip_sum(n, G[n], T_[n], place) for n in mlp2])
    dh1, dh1b, d_nmlp = _rms_bwd("norm_mlp_bwd", du2, h1, norm_mlp_w, dh2, side=_after(flight_mlp[-1]))
    dmerged = _mm("d_merged", dh1b, Wf["w_out"], "nt", [F32])
    G["w_out"] = _mm("g_w_out", merged, dh1b, "tn", [BF16])
    dpa, dpb, dz_ga, dz_gb = _dmerge(dmerged, z, pa, pb)
    dya = _mm("d_ya", dpa, Wf["w_branch_a"], "nt", [F32])
    dyb = _mm("d_yb", dpb, Wf["w_branch_b"], "nt", [F32])
    G["w_branch_a"] = _mm("g_w_a", ya, dpa, "tn", [BF16])
    G["w_branch_b"] = _mm("g_w_b", yb, dpb, "tn", [BF16])
    dz_q, dz_f, dz_i, dz_g, d_lbl, d_hgw, *sent = _hg_bwd(
        z, o_hg, dya, states, lb_logits, hg_norm_w, side=_sib_send(small3, [G[n] for n in small3]))
    flight_small = _exchange_start("small", [_chip_sum(n, G[n], t, place) for n, t in zip(small3, sent)])
    dz_aq, dz_ak, dz_av, dbias_win = _at_bwd(z, dyb, bias_win, side=_after(flight_small[-1]))
    dz = jnp.concatenate([dz_q, dz_f, dz_i, dz_g, dz_aq, dz_ak, dz_av, dz_ga, dz_gb], axis=1)
    g_send = _g_w_in_half(u1, dz, place, False)
    g_keep, T_["w_in"] = _g_w_in_half(u1, dz, place, True, side=_sib_send_half(g_send))
    for names, flight in ((mlp2, flight_mlp), (small3, flight_small)):
        sums, got = _exchange_wait("_".join(names), flight, g_keep)
        S_.update(zip(names, sums))
        GOT.update(zip(names, got))
    S_["w_in"] = _chip_sum("w_in", g_keep, T_["w_in"], place, kept_rows=True)
    early = [n for n in WEIGHTS if n != "w_in"]
    H_ = {n: _piece_sum(n, S_[n], GOT[n], place) for n in early}
    flight = _exchange_start("w_in", [S_["w_in"]])
    share_early = _sib_share([H_[n] for n in early])
    share_early.reads.append(flight[-1])
    du1, *shared = _mm("d_u1", dz, Wf["w_in"], "nt", [F32], side=share_early)
    O_ = dict(zip(early, shared))
    grad_x, _, d_nmix = _rms_bwd("norm_mix_bwd", du1, x2, norm_mix_w, dh1)
    d_rel = _bias_window_grad(dbias_win)
    big_out = {}
    for name in early:
        outs = _adam_quarter(name, *big[name], H_[name], O_[name], place)
        big_out[name] = tuple(a[None] for a in outs)
    (S_["w_in"],), (got_in,) = _exchange_wait("w_in", flight, outs[1])
    H_["w_in"] = _piece_sum("w_in", S_["w_in"], got_in, place)
    sharing = _share_start("w_in", H_["w_in"])
    first = _adam_half("w_in", *big["w_in"], sharing[2], place, True)
    outs = _adam_half("w_in", *big["w_in"], _share_wait("w_in", sharing, first[0]), place, False, prev=first)
    big_out["w_in"] = tuple(a[None] for a in outs)

    smalls = [("lb_logits", lb_logits, m_lb_logits, v_lb_logits, d_lbl),
              ("hg_norm_w", hg_norm_w, m_hg_norm_w, v_hg_norm_w, d_hgw),
              ("rel_bias", rel_bias, m_rel_bias, v_rel_bias, d_rel),
              ("norm_mix_w", norm_mix_w, m_norm_mix_w, v_norm_mix_w, d_nmix),
              ("norm_mlp_w", norm_mlp_w, m_norm_mlp_w, v_norm_mlp_w, d_nmlp),
              ("norm_final_w", norm_final_w, m_norm_final_w, v_norm_final_w, d_nf)]
    one = jnp.zeros((1,), F32)
    like = [s[1] for s in smalls] + [one]
    packed = _small_allreduce_adam(_pack([s[4] for s in smalls] + [loss_part[0, :1]]), _pack(like),
                                   _pack([s[2] for s in smalls] + [one]), _pack([s[3] for s in smalls] + [one]), got_in)
    unpacked = [_unpack(p, like) for p in packed]
    small_out = {s[0]: vals for s, vals in zip(smalls, zip(*unpacked))}
    loss = unpacked[0][-1].reshape(())
    order = ["w_in", "lb_logits", "hg_norm_w", "rel_bias", "w_branch_a", "w_branch_b", "w_out", "norm_mix_w",
             "norm_mlp_w", "w_up", "w_down", "norm_final_w"]
    res = {**big_out, **small_out}
    return (loss, grad_x.reshape(x.shape), *[res[n][0] for n in order], *[res[n][1] for n in order],
            *[res[n][2] for n in order], *[res[n][3] for n in order])
```

```python
import functools

import jax
import jax.numpy as jnp
from jax import lax
from jax.experimental import pallas as pl
from jax.experimental.pallas import tpu as pltpu

F32 = jnp.float32
BF16 = jnp.bfloat16
HIGHEST = lax.Precision.HIGHEST

D_MODEL = 2048
SEQ = 2048
CHUNK = 64
HG_HEADS = 8
HG_D = 128
AT_HEADS = 16
AT_DH = 64
LEFT = 8
REL_CLIP = 256
D_FF = 8192
EPS = 1e-6
ADAM_LR = 0.001
ADAM_B1 = 0.9
ADAM_B2 = 0.999
ADAM_EPS = 1e-08
ADAM_WD = 0.01
ADAM_STEP = 10

LANE = 128
NEG = -1e30
EXP_CLAMP = 80.0
VMEM_LIMIT = 48 * 1024 * 1024
MM_TM, MM_TN, MM_TK = 1024, 1024, 2816
ROW_TILE = 256
QB = 2 * CHUNK


def _hgw():
    return HG_HEADS * HG_D


def _atw():
    return AT_HEADS * AT_DH


def _cparams(sem):
    return pltpu.CompilerParams(dimension_semantics=sem, vmem_limit_bytes=VMEM_LIMIT)


def _sigmoid(x):
    return jax.nn.sigmoid(x)


def _dot(a, b, dims, precision=None):
    return lax.dot_general(a, b, (dims, ((), ())), preferred_element_type=F32, precision=precision)


def _nn(a, b, precision=None):
    return _dot(a, b, ((1,), (0,)), precision)


def _nt(a, b, precision=None):
    return _dot(a, b, ((1,), (1,)), precision)


def _tn(a, b, precision=None):
    return _dot(a, b, ((0,), (0,)), precision)


class _Side:
    def __init__(self, build, nsem, reads=(), aliased=(), fresh=()):
        self.build, self.nsem = build, nsem
        self.reads, self.aliased, self.fresh = list(reads), list(aliased), list(fresh)

    def operands(self):
        return self.reads + self.aliased

    def in_specs(self):
        return [ANY] * len(self.operands())

    def out_specs(self):
        return [ANY] * (len(self.aliased) + len(self.fresh))

    def out_shape(self):
        return [jax.ShapeDtypeStruct(a.shape, a.dtype) for a in self.aliased] + self.fresh

    def aliases(self, n_in, n_out):
        return {n_in + len(self.reads) + t: n_out + t for t in range(len(self.aliased))}

    def scratch(self):
        return [pltpu.SemaphoreType.DMA((self.nsem,)), pltpu.SemaphoreType.DMA((self.nsem,))]

    def hooks(self, in_refs, out_refs, sems, first, last):
        nr, na = len(self.reads), len(self.aliased)
        args = (in_refs[:nr], out_refs[:na], out_refs[na:], *sems)

        @pl.when(first)
        def _():
            for cp in self.build(*args):
                cp.start()

        @pl.when(last)
        def _():
            for cp in self.build(*args):
                cp.wait()


def _after(*tokens):
    return _Side(lambda *args: [], 1, reads=tokens)


def _side_parts(side):
    if side is None:
        return [], [], [], [], lambda n_in, n_out: {}, []
    return side.operands(), side.in_specs(), side.out_specs(), side.out_shape(), side.aliases, side.scratch()


def _call_with_side(body, name, grid, in_specs, out_specs, out_shape, scratch, sem, operands, side, n_prefetch=0,
                    aliases=None, borrow=None):
    _, _, s_out, s_shape, _, s_scr = _side_parts(side)
    n_in, n_out = n_prefetch + len(in_specs), len(out_specs)
    borrow = borrow or {}
    s_ops, s_alias = [], {}
    if side is not None:
        keep = [t for t in range(len(side.aliased)) if t not in borrow]
        s_ops = side.reads + [side.aliased[t] for t in keep]
        s_alias = {n_in + len(side.reads) + pos: n_out + t for pos, t in enumerate(keep)}
        s_alias.update({n_prefetch + i: n_out + t for t, i in borrow.items()})
    s_in = [ANY] * len(s_ops)
    n_sin, n_sout = len(s_ops), len(s_out)

    def wrapped(*refs):
        a, b, c = n_in + n_sin, n_in + n_sin + n_out, n_in + n_sin + n_out + n_sout
        ids = [pl.program_id(d) for d in range(len(grid))]
        first = functools.reduce(lambda p, q: p & q, [i == 0 for i in ids])
        last = functools.reduce(lambda p, q: p & q, [i == g - 1 for i, g in zip(ids, grid)])
        side.hooks(refs[n_in:a], refs[b:c], refs[-2:], first, last)
        body(*refs[:n_in], *refs[a:b], *refs[c:-2])

    spec = dict(grid=grid, in_specs=in_specs + s_in, out_specs=out_specs + s_out, scratch_shapes=scratch + s_scr)
    if n_prefetch:
        spec = dict(grid_spec=pltpu.PrefetchScalarGridSpec(num_scalar_prefetch=n_prefetch, **spec))
    return pl.pallas_call(
        body if side is None else wrapped, name=name, out_shape=out_shape + s_shape,
        input_output_aliases={**s_alias, **{n_prefetch + i: o for i, o in (aliases or {}).items()}},
        compiler_params=_cparams(sem if side is None else ("arbitrary",) * len(grid)), **spec,
    )(*operands, *s_ops)


def _mm_tk(K):
    if K <= MM_TK:
        return K
    return max(t for t in range(LANE, MM_TK + 1, LANE) if K % t == 0)


def _mm(name, a, b, mode, out_dtypes, extras=(), epilogue=None, side=None):
    if mode == "nn":
        (M, K), (K2, N) = a.shape, b.shape
    elif mode == "nt":
        (M, K), (N, K2) = a.shape, b.shape
    else:
        (K, M), (K2, N) = a.shape, b.shape
    assert K == K2, (name, a.shape, b.shape)
    tm, tn, tk = min(MM_TM, M), min(MM_TN, N), _mm_tk(K)
    assert M % tm == 0 and N % tn == 0 and K % tk == 0, (name, M, N, K)
    ni, nj, nk = M // tm, N // tn, K // tk
    ne, no = len(extras), len(out_dtypes)
    if epilogue is None:
        epilogue = lambda acc: (acc,)
    s_ops, s_in, s_out, s_shape, s_alias, s_scr = _side_parts(side)
    n_in, n_sin, n_sout = 2 + ne, len(s_ops), len(s_out)

    def body(*refs):
        a_ref, b_ref = refs[:2]
        extra_refs = refs[2:n_in]
        out_refs = refs[n_in + n_sin:n_in + n_sin + no]
        rest = refs[n_in + n_sin + no + n_sout:]
        i, j, k = pl.program_id(0), pl.program_id(1), pl.program_id(2)
        if side is not None:
            side.hooks(refs[n_in:n_in + n_sin], refs[n_in + n_sin + no:n_in + n_sin + no + n_sout], rest[-2:],
                       (i == 0) & (j == 0) & (k == 0), (i == ni - 1) & (j == nj - 1) & (k == nk - 1))
        av, bv = a_ref[...].astype(BF16), b_ref[...].astype(BF16)
        prod = _nn(av, bv) if mode == "nn" else _nt(av, bv) if mode == "nt" else _tn(av, bv)

        def finish(acc):
            res = epilogue(acc, *[e[...] for e in extra_refs])
            for o_ref, r in zip(out_refs, res):
                o_ref[...] = r.astype(o_ref.dtype)

        if nk == 1:
            finish(prod)
        else:
            acc_ref = rest[0]

            @pl.when(k == 0)
            def _():
                acc_ref[...] = prod

            @pl.when((k > 0) & (k < nk - 1))
            def _():
                acc_ref[...] += prod

            @pl.when(k == nk - 1)
            def _():
                finish(acc_ref[...] + prod)

    if mode == "nn":
        a_spec = pl.BlockSpec((tm, tk), lambda i, j, k: (i, k))
        b_spec = pl.BlockSpec((tk, tn), lambda i, j, k: (k, j))
    elif mode == "nt":
        a_spec = pl.BlockSpec((tm, tk), lambda i, j, k: (i, k))
        b_spec = pl.BlockSpec((tn, tk), lambda i, j, k: (j, k))
    else:
        a_spec = pl.BlockSpec((tk, tm), lambda i, j, k: (k, i))
        b_spec = pl.BlockSpec((tk, tn), lambda i, j, k: (k, j))
    o_spec = pl.BlockSpec((tm, tn), lambda i, j, k: (i, j))
    sem = ("arbitrary",) * 3 if side is not None else ("parallel", "parallel", "arbitrary")
    outs = pl.pallas_call(
        body, name=name,
        grid=(ni, nj, nk),
        in_specs=[a_spec, b_spec] + [o_spec] * ne + s_in,
        out_specs=[o_spec] * no + s_out,
        out_shape=[jax.ShapeDtypeStruct((M, N), dt) for dt in out_dtypes] + s_shape,
        input_output_aliases=s_alias(n_in, no),
        scratch_shapes=([pltpu.VMEM((tm, tn), F32)] if nk > 1 else []) + s_scr,
        compiler_params=_cparams(sem),
    )(a, b, *extras, *s_ops)
    return outs[0] if len(outs) == 1 else outs


def _row_spec(tr, d):
    return pl.BlockSpec((tr, d), lambda i: (i, 0))


def _vec_spec(d):
    return pl.BlockSpec((1, d), lambda i: (0, 0))


def _rms_fwd(name, x, w, side=None):
    T, D = x.shape
    tr = min(ROW_TILE, T)

    def body(x_ref, w_ref, o_ref):
        xf = x_ref[...]
        r = lax.rsqrt(jnp.mean(xf * xf, axis=-1, keepdims=True) + EPS)
        o_ref[...] = (xf * r * w_ref[...]).astype(BF16)

    outs = _call_with_side(body, name, (T // tr,), [_row_spec(tr, D), _vec_spec(D)], [_row_spec(tr, D)],
                           [jax.ShapeDtypeStruct((T, D), BF16)], [], ("parallel",), (x, w), side)
    return outs[0] if side is None else outs


def _rms_bwd(name, dy, h, w, dres, side=None):
    T, D = h.shape
    tr = min(ROW_TILE, T)

    def body(dy_ref, h_ref, w_ref, dres_ref, dh_ref, dhb_ref, dw_ref):
        @pl.when(pl.program_id(0) == 0)
        def _():
            dw_ref[...] = jnp.zeros_like(dw_ref)

        hf, dyv = h_ref[...], dy_ref[...]
        r = lax.rsqrt(jnp.mean(hf * hf, axis=-1, keepdims=True) + EPS)
        xhat = hf * r
        dw_ref[...] += jnp.sum(dyv * xhat, axis=0, keepdims=True)
        dxh = dyv * w_ref[...]
        dh = dres_ref[...] + r * (dxh - xhat * jnp.mean(dxh * xhat, axis=-1, keepdims=True))
        dh_ref[...] = dh
        dhb_ref[...] = dh.astype(BF16)

    return _call_with_side(
        body, name, (T // tr,),
        [_row_spec(tr, D), _row_spec(tr, D), _vec_spec(D), _row_spec(tr, D)],
        [_row_spec(tr, D), _row_spec(tr, D), _vec_spec(D)],
        [jax.ShapeDtypeStruct((T, D), F32), jax.ShapeDtypeStruct((T, D), BF16), jax.ShapeDtypeStruct((1, D), F32)],
        [], ("arbitrary",), (dy, h, w, dres), side)


def _loss_head(h2, target, w):
    T, D = h2.shape
    tr = min(ROW_TILE, T)

    def body(h_ref, t_ref, w_ref, loss_ref, dh_ref, dhb_ref, dw_ref):
        @pl.when(pl.program_id(0) == 0)
        def _():
            dw_ref[...] = jnp.zeros_like(dw_ref)
            loss_ref[...] = jnp.zeros_like(loss_ref)

        hf, wv = h_ref[...], w_ref[...]
        r = lax.rsqrt(jnp.mean(hf * hf, axis=-1, keepdims=True) + EPS)
        xhat = hf * r
        diff = xhat * wv - t_ref[...]
        loss_ref[...] += 0.5 * jnp.sum(jnp.mean(diff * diff, axis=-1, keepdims=True))
        dyv = diff * (1.0 / D)
        dw_ref[...] += jnp.sum(dyv * xhat, axis=0, keepdims=True)
        dxh = dyv * wv
        dh = r * (dxh - xhat * jnp.mean(dxh * xhat, axis=-1, keepdims=True))
        dh_ref[...] = dh
        dhb_ref[...] = dh.astype(BF16)

    return pl.pallas_call(
        body, name="loss_head", grid=(T // tr,),
        in_specs=[_row_spec(tr, D), _row_spec(tr, D), _vec_spec(D)],
        out_specs=[_vec_spec(LANE), _row_spec(tr, D), _row_spec(tr, D), _vec_spec(D)],
        out_shape=[jax.ShapeDtypeStruct((1, LANE), F32), jax.ShapeDtypeStruct((T, D), F32),
                   jax.ShapeDtypeStruct((T, D), BF16), jax.ShapeDtypeStruct((1, D), F32)],
        compiler_params=_cparams(("arbitrary",)),
    )(h2, target, w)


def _gate_tiles(T, D):
    goff = 4 * _hgw() + 3 * _atw()
    tc = min(1024, D)
    assert goff % tc == 0 and D % tc == 0
    return min(ROW_TILE, T), tc, goff // tc, D // tc


def _merge(z, pa, pb, side=None):
    T, D = pa.shape
    tr, tc, g0, nd = _gate_tiles(T, D)

    def body(ga_ref, gb_ref, pa_ref, pb_ref, o_ref):
        o_ref[...] = (_sigmoid(ga_ref[...]) * pa_ref[...] + _sigmoid(gb_ref[...]) * pb_ref[...]).astype(BF16)

    t = pl.BlockSpec((tr, tc), lambda i, j: (i, j))
    outs = _call_with_side(
        body, "merge", (T // tr, nd),
        [pl.BlockSpec((tr, tc), lambda i, j: (i, g0 + j)), pl.BlockSpec((tr, tc), lambda i, j: (i, g0 + nd + j)), t, t],
        [t], [jax.ShapeDtypeStruct((T, D), BF16)], [], ("parallel", "parallel"), (z, z, pa, pb), side)
    return outs[0] if side is None else outs


def _dmerge(dm, z, pa, pb):
    T, D = pa.shape
    tr, tc, g0, nd = _gate_tiles(T, D)

    def body(dm_ref, ga_ref, gb_ref, pa_ref, pb_ref, dpa_ref, dpb_ref, dga_ref, dgb_ref):
        dmv = dm_ref[...]
        sa, sb = _sigmoid(ga_ref[...]), _sigmoid(gb_ref[...])
        dpa_ref[...] = (dmv * sa).astype(BF16)
        dpb_ref[...] = (dmv * sb).astype(BF16)
        dga_ref[...] = (dmv * pa_ref[...] * sa * (1.0 - sa)).astype(BF16)
        dgb_ref[...] = (dmv * pb_ref[...] * sb * (1.0 - sb)).astype(BF16)

    t = pl.BlockSpec((tr, tc), lambda i, j: (i, j))
    return pl.pallas_call(
        body, name="dmerge", grid=(T // tr, nd),
        in_specs=[t, pl.BlockSpec((tr, tc), lambda i, j: (i, g0 + j)),
                  pl.BlockSpec((tr, tc), lambda i, j: (i, g0 + nd + j)), t, t],
        out_specs=[t, t, t, t],
        out_shape=[jax.ShapeDtypeStruct((T, D), BF16)] * 4,
        compiler_params=_cparams(("parallel", "parallel")),
    )(dm, z, z, pa, pb)


def _hg_gates(xq, xf, lb):
    f = _sigmoid(xf)
    g = lb + (1.0 - lb) * f
    sq = _sigmoid(xq)
    return f, g, jnp.log(g), 1.0 - g, sq, xq * sq * (HG_D ** -0.5)


def _split2(x):
    hi = x.astype(BF16)
    return hi, (x - hi.astype(F32)).astype(BF16)


def _tri_sum(tri, x):
    hi, rest = x.astype(BF16), x - x.astype(BF16).astype(F32)
    mid, lo = _split2(rest)
    return _nn(tri, lo) + _nn(tri, mid) + _nn(tri, hi)


def _hg_decays(lg, tri_incl, rowi):
    b = _tri_sum(tri_incl, lg)
    b_last = jnp.sum(lg, axis=0, keepdims=True)
    b_mid = jnp.sum(jnp.where(rowi <= CHUNK // 2, lg, 0.0), axis=0, keepdims=True)
    return b, b_last, b_mid


HG_GROUP = 2


def _hg_in_specs(T):
    ng = HG_HEADS // HG_GROUP
    return [pl.BlockSpec((T, HG_GROUP * HG_D), lambda h, s=s: (0, s * ng + h)) for s in range(4)]


def _hg_fwd(z, lb_logits, hgw, side=None):
    T = z.shape[0]
    H, d, C, G = HG_HEADS, HG_D, CHUNK, HG_GROUP
    nc = T // C

    def body(hq_ref, hf_ref, hi_ref, hg_ref, lbl_ref, w_ref, ya_ref, o_ref, s_ref):
        lb_all = 1.0 / (1.0 + jnp.exp(lbl_ref[1:2, :] - lbl_ref[0:1, :]))
        wv = w_ref[...]
        row = lax.broadcasted_iota(jnp.int32, (C, C), 0)
        col = lax.broadcasted_iota(jnp.int32, (C, C), 1)
        tril = col <= row
        tri_incl = tril.astype(BF16)
        rowi = lax.broadcasted_iota(jnp.int32, (C, G * d), 0)
        lanes = [slice(hh * d, (hh + 1) * d) for hh in range(G)]
        per_head = lambda fn: jnp.concatenate([fn(hh, sl) for hh, sl in enumerate(lanes)], axis=1)
        wv_all = jnp.tile(wv, (1, G))

        def chunk(c, states):
            rows = pl.ds(pl.multiple_of(c * C, C), C)
            xq, xf, v, xg = hq_ref[rows, :], hf_ref[rows, :], hi_ref[rows, :], hg_ref[rows, :]
            _, _, lg, kk, _, q = _hg_gates(xq, xf, lb_all)
            b, b_last, b_mid = _hg_decays(lg, tri_incl, rowi)
            vb, qe = v.astype(BF16), (q * jnp.exp(b)).astype(BF16)
            qt = (q * jnp.exp(b - b_mid)).astype(BF16)
            kt = (kk * jnp.exp(jnp.minimum(b_mid - b, EXP_CLAMP))).astype(BF16)
            kd, e_last = (kk * jnp.exp(b_last - b)).astype(BF16), jnp.exp(b_last)
            for hh, st in enumerate(states):
                s_ref[hh, c] = st
            o = per_head(lambda hh, sl: _nt(qe[:, sl], states[hh].astype(BF16)))
            a = [jnp.where(tril, _nt(qt[:, sl], kt[:, sl]), 0.0).astype(BF16) for sl in lanes]
            o = o + per_head(lambda hh, sl: _nn(a[hh], vb[:, sl]))
            o_ref[rows, :] = o
            r = per_head(lambda hh, sl: jnp.broadcast_to(
                lax.rsqrt(jnp.mean(o[:, sl] * o[:, sl], axis=-1, keepdims=True) + EPS), (C, d)))
            ya_ref[rows, :] = (o * r * wv_all * (xg * _sigmoid(xg))).astype(BF16)
            return tuple(st * e_last[:, sl] + _tn(vb[:, sl], kd[:, sl]) for st, sl in zip(states, lanes))

        lax.fori_loop(0, nc, chunk, tuple(jnp.zeros((d, d), F32) for _ in range(G)))

    heads = pl.BlockSpec((T, G * d), lambda h: (0, h))
    return _call_with_side(
        body, "hg_fwd", (H // G,),
        _hg_in_specs(T) + [pl.BlockSpec((2, G * d), lambda h: (0, h)), pl.BlockSpec((1, d), lambda h: (0, 0))],
        [heads, heads, pl.BlockSpec((G, nc, d, d), lambda h: (h, 0, 0, 0))],
        [jax.ShapeDtypeStruct((T, H * d), BF16), jax.ShapeDtypeStruct((T, H * d), F32),
         jax.ShapeDtypeStruct((H, nc, d, d), F32)],
        [], ("parallel",), (z, z, z, z, lb_logits, hgw), side)


def _hg_bwd(z, o, dya, states, lb_logits, hgw, side=None):
    T = z.shape[0]
    H, d, C, G = HG_HEADS, HG_D, CHUNK, HG_GROUP
    nc = T // C
    scale = HG_D ** -0.5

    def body(hq_ref, hf_ref, hi_ref, hg_ref, o_ref, dy_ref, s_ref, lbl_ref, w_ref,
             dq_ref, df_ref, di_ref, dg_ref, dlbl_ref, dw_ref, acc_ref):
        lb_all = 1.0 / (1.0 + jnp.exp(lbl_ref[1:2, :] - lbl_ref[0:1, :]))
        wv = w_ref[...]
        row = lax.broadcasted_iota(jnp.int32, (C, C), 0)
        col = lax.broadcasted_iota(jnp.int32, (C, C), 1)
        tril = col <= row
        tri_incl = tril.astype(BF16)
        triu_incl = (col >= row).astype(BF16)
        rowi = lax.broadcasted_iota(jnp.int32, (C, G * d), 0)
        lanes = [slice(hh * d, (hh + 1) * d) for hh in range(G)]
        per_head = lambda fn: jnp.concatenate([fn(hh, sl) for hh, sl in enumerate(lanes)], axis=1)
        head_mean = lambda x: per_head(
            lambda hh, sl: jnp.broadcast_to(jnp.mean(x[:, sl], axis=-1, keepdims=True), (C, d)))
        wv_all = jnp.tile(wv, (1, G))
        lb = lb_all
        acc_ref[...] = jnp.zeros_like(acc_ref)

        @pl.when(pl.program_id(0) == 0)
        def _():
            dw_ref[...] = jnp.zeros_like(dw_ref)

        def chunk(i, carry):
            dsts, tail = carry
            c = nc - 1 - i
            rows = pl.ds(pl.multiple_of(c * C, C), C)
            xq, xf, v, xg = hq_ref[rows, :], hf_ref[rows, :], hi_ref[rows, :], hg_ref[rows, :]
            f, g, lg, kk, sq, q = _hg_gates(xq, xf, lb)
            b, b_last, b_mid = _hg_decays(lg, tri_incl, rowi)
            e_b, e_qm, e_km = jnp.exp(b), jnp.exp(b - b_mid), jnp.exp(jnp.minimum(b_mid - b, EXP_CLAMP))
            e_kl, e_last = jnp.exp(b_last - b), jnp.exp(b_last)
            ov, dy = o_ref[rows, :], dy_ref[rows, :]
            r = lax.rsqrt(head_mean(ov * ov) + EPS)
            xhat = ov * r
            sg = _sigmoid(xg)
            dxg = dy * xhat * wv_all * (sg * (1.0 + xg * (1.0 - sg)))
            dyn = dy * (xg * sg)
            acc_ref[0:1, :] += jnp.sum(dyn * xhat, axis=0, keepdims=True)
            dxh = dyn * wv_all
            dof = r * (dxh - xhat * head_mean(dxh * xhat))
            do, vb = dof.astype(BF16), v.astype(BF16)
            qe, kd, qt, kt = (q * e_b).astype(BF16), (kk * e_kl).astype(BF16), (q * e_qm).astype(BF16), (kk * e_km).astype(BF16)
            pm = [jnp.where(tril, _nt(do[:, sl], vb[:, sl]), 0.0).astype(BF16) for sl in lanes]
            am = [jnp.where(tril, _nt(qt[:, sl], kt[:, sl]), 0.0).astype(BF16) for sl in lanes]
            st = [_split2(s_ref[hh, c]) for hh in range(G)]
            ds = [_split2(x) for x in dsts]
            dq_state = per_head(lambda hh, sl: _nn(do[:, sl], st[hh][1]) + _nn(do[:, sl], st[hh][0]))
            dk_state = per_head(lambda hh, sl: _nn(vb[:, sl], ds[hh][1]) + _nn(vb[:, sl], ds[hh][0]))
            dq_intra = per_head(lambda hh, sl: _nn(pm[hh], kt[:, sl]))
            dk_intra = per_head(lambda hh, sl: _tn(pm[hh], qt[:, sl]))
            dv = per_head(lambda hh, sl: _tn(am[hh], do[:, sl]) + _nt(kd[:, sl], ds[hh][0]))
            new_dsts = tuple(x * e_last[:, sl] + _tn(do[:, sl], qe[:, sl]) for x, sl in zip(dsts, lanes))
            dq = dq_state * e_b + dq_intra * e_qm
            dk = dk_intra * e_km + dk_state * e_kl
            db = (qe.astype(F32) * dq_state + qt.astype(F32) * dq_intra
                  - kt.astype(F32) * dk_intra - kd.astype(F32) * dk_state)
            dlg = _tri_sum(triu_incl, db) + tail
            dgate = dlg / g - dk
            acc_ref[1:2, :] += jnp.sum(dgate * (1.0 - f), axis=0, keepdims=True)
            dq_ref[rows, :] = (dq * scale * (sq * (1.0 + xq * (1.0 - sq)))).astype(BF16)
            df_ref[rows, :] = (dgate * (1.0 - lb) * f * (1.0 - f)).astype(BF16)
            di_ref[rows, :] = dv.astype(BF16)
            dg_ref[rows, :] = dxg.astype(BF16)
            return new_dsts, tail + jnp.sum(db, axis=0, keepdims=True)

        lax.fori_loop(0, nc, chunk, (tuple(jnp.zeros((d, d), F32) for _ in range(G)), jnp.zeros((1, G * d), F32)))
        dw_ref[...] += functools.reduce(lambda p, q: p + q, [acc_ref[0:1, sl] for sl in lanes])
        dl0 = acc_ref[1:2, :] * lb_all * (1.0 - lb_all)
        dlbl_ref[0:1, :] = dl0
        dlbl_ref[1:2, :] = -dl0

    heads = pl.BlockSpec((T, G * d), lambda h: (0, h))
    logits = pl.BlockSpec((2, G * d), lambda h: (0, h))
    return _call_with_side(
        body, "hg_bwd", (H // G,),
        _hg_in_specs(T) + [heads, heads, pl.BlockSpec((G, nc, d, d), lambda h: (h, 0, 0, 0)), logits,
                           pl.BlockSpec((1, d), lambda h: (0, 0))],
        [heads, heads, heads, heads, logits, pl.BlockSpec((1, d), lambda h: (0, 0))],
        [jax.ShapeDtypeStruct((T, H * d), BF16)] * 4 + [jax.ShapeDtypeStruct((2, H * d), F32),
                                                        jax.ShapeDtypeStruct((1, d), F32)],
        [pltpu.VMEM((8, G * d), F32)], ("arbitrary",), (z, z, z, z, o, dya, states, lb_logits, hgw), side)


def _at_dims():
    pad = LEFT * CHUNK
    return pad, QB + pad, AT_HEADS * AT_DH // LANE, 4 * _hgw() // LANE


def _rel_of_period():
    pad, W, _, _ = _at_dims()
    n = jnp.arange(QB + W)
    return jnp.clip(pad - jnp.where(n < W, n, n - (QB + W)), -REL_CLIP, REL_CLIP) + REL_CLIP


def _bias_window(rel_bias):
    pad, W, _, _ = _at_dims()
    H, P = rel_bias.shape[0], QB + W
    per = rel_bias[:, _rel_of_period()]
    win = jnp.tile(per, (1, QB))[:, :QB * (P - 1)].reshape(H, QB, P - 1)[:, :, :W]
    t = jnp.arange(QB)[:, None]
    j = jnp.arange(W)[None, :]
    ok = (j // CHUNK >= t // CHUNK) & (j // CHUNK <= t // CHUNK + LEFT)
    return jnp.where(ok[None], win, NEG)


def _bias_window_grad(dbw):
    pad, W, _, _ = _at_dims()
    H, P = dbw.shape[0], QB + W
    flat = jnp.pad(dbw, ((0, 0), (0, 0), (0, P - 1 - W))).reshape(H, QB * (P - 1))
    per = jnp.pad(flat, ((0, 0), (0, QB))).reshape(H, QB, P).sum(axis=1)
    onehot = _rel_of_period()[:, None] == jnp.arange(2 * REL_CLIP + 1)[None, :]
    return jnp.dot(per, onehot.astype(F32), precision=HIGHEST)


def _at_stack(x):
    first = lax.broadcasted_iota(jnp.int32, x.shape, 1) < AT_DH
    return jnp.concatenate([jnp.where(first, x, 0.0), jnp.where(first, 0.0, x)], axis=0).astype(BF16)


def _at_unstack(x):
    first = lax.broadcasted_iota(jnp.int32, (QB, LANE), 1) < AT_DH
    return jnp.where(first, x[:QB], x[QB:])


def _at_softmax(qs, kw, bias_ref, qi):
    pad, W, _, _ = _at_dims()
    s = _nt(qs, kw) + bias_ref[...].reshape(2 * QB, W)
    valid = lax.broadcasted_iota(jnp.int32, (2 * QB, W), 1) + qi * QB >= pad
    s = jnp.where(valid, s, NEG)
    e = jnp.exp(s - jnp.max(s, axis=-1, keepdims=True))
    return e * (1.0 / jnp.sum(e, axis=-1, keepdims=True))


def _at_fwd(z, bias_win, side=None):
    T = z.shape[0]
    pad, W, HP, c0 = _at_dims()
    nq = T // QB

    def body(q_ref, k_ref, v_ref, bias_ref, o_ref, kpad, vpad):
        qi = pl.program_id(1)

        @pl.when(qi == 0)
        def _():
            kpad[0:pad, :] = jnp.zeros((pad, LANE), BF16)
            vpad[0:pad, :] = jnp.zeros((pad, LANE), BF16)
            kpad[pad:, :] = k_ref[...].astype(BF16)
            vpad[pad:, :] = v_ref[...].astype(BF16)

        win = pl.ds(pl.multiple_of(qi * QB, QB), W)
        kw, vw = kpad[win, :], vpad[win, :]
        p = _at_softmax(_at_stack(q_ref[...] * (AT_DH ** -0.5)), kw, bias_ref, qi)
        o_ref[...] = _at_unstack(_nn(p.astype(BF16), vw)).astype(BF16)

    full = lambda s: pl.BlockSpec((T, LANE), lambda hp, qi, s=s: (0, c0 + s * HP + hp))
    return _call_with_side(
        body, "at_fwd", (HP, nq),
        [pl.BlockSpec((QB, LANE), lambda hp, qi: (qi, c0 + hp)), full(1), full(2),
         pl.BlockSpec((2, QB, W), lambda hp, qi: (hp, 0, 0))],
        [pl.BlockSpec((QB, LANE), lambda hp, qi: (qi, hp))],
        [jax.ShapeDtypeStruct((T, HP * LANE), BF16)],
        [pltpu.VMEM((T + pad, LANE), BF16)] * 2, ("parallel", "arbitrary"), (z, z, z, bias_win), side)


def _at_bwd(z, dyb, bias_win, side=None):
    T = z.shape[0]
    pad, W, HP, c0 = _at_dims()
    nq = T // QB
    scale = AT_DH ** -0.5

    def body(q_ref, k_ref, v_ref, do_ref, bias_ref, dq_ref, dk_ref, dv_ref, dbias_ref, kpad, vpad, dkpad, dvpad):
        qi = pl.program_id(1)

        @pl.when(qi == 0)
        def _():
            kpad[0:pad, :] = jnp.zeros((pad, LANE), BF16)
            vpad[0:pad, :] = jnp.zeros((pad, LANE), BF16)
            kpad[pad:, :] = k_ref[...].astype(BF16)
            vpad[pad:, :] = v_ref[...].astype(BF16)
            dkpad[...] = jnp.zeros_like(dkpad)
            dvpad[...] = jnp.zeros_like(dvpad)
            dbias_ref[...] = jnp.zeros_like(dbias_ref)

        win = pl.ds(pl.multiple_of(qi * QB, QB), W)
        kw, vw = kpad[win, :], vpad[win, :]
        qs, dos = _at_stack(q_ref[...] * scale), _at_stack(do_ref[...])
        p = _at_softmax(qs, kw, bias_ref, qi)
        dp = _nt(dos, vw)
        ds = p * (dp - jnp.sum(p * dp, axis=-1, keepdims=True))
        dbias_ref[...] += ds.reshape(2, QB, W)
        dsb = ds.astype(BF16)
        dq_ref[...] = (_at_unstack(_nn(dsb, kw)) * scale).astype(BF16)
        lanes_win = pl.ds(pl.multiple_of(qi * QB, QB), W)
        dkpad[:, lanes_win] += _tn(qs, dsb)
        dvpad[:, lanes_win] += _tn(dos, p.astype(BF16))

        @pl.when(qi == nq - 1)
        def _():
            dk_ref[...] = dkpad[:, pad:].T.astype(BF16)
            dv_ref[...] = dvpad[:, pad:].T.astype(BF16)

    full = lambda s: pl.BlockSpec((T, LANE), lambda hp, qi, s=s: (0, c0 + s * HP + hp))
    blk = pl.BlockSpec((QB, LANE), lambda hp, qi: (qi, hp))
    col = pl.BlockSpec((T, LANE), lambda hp, qi: (0, hp))
    bw = pl.BlockSpec((2, QB, W), lambda hp, qi: (hp, 0, 0))
    return _call_with_side(
        body, "at_bwd", (HP, nq),
        [pl.BlockSpec((QB, LANE), lambda hp, qi: (qi, c0 + hp)), full(1), full(2), blk, bw],
        [blk, col, col, bw],
        [jax.ShapeDtypeStruct((T, HP * LANE), BF16)] * 3 + [jax.ShapeDtypeStruct(bias_win.shape, F32)],
        [pltpu.VMEM((T + pad, LANE), BF16)] * 2 + [pltpu.VMEM((LANE, T + pad), F32)] * 2,
        ("parallel", "arbitrary"), (z, z, z, dyb, bias_win), side)


def _piece_tiles(name, full_shape):
    pr, pc = _piece_shape(name, full_shape)
    tr = min(ROW_TILE, pr)
    assert pr % tr == 0
    nt = pr // tr
    if name in ROW_SHARDED:
        return tr, nt, lambda q, half, i: ((2 * q + half) * nt + i, 0)
    return tr, nt, lambda q, half, i: (half * nt + i, q)


def _cast_into_full(name, wq, place, also_alone=False):
    full = _full_shape(name, wq.shape)
    pc = wq.shape[1]
    tr, nt, at = _piece_tiles(name, full)

    def body(place_ref, w_ref, *o_refs):
        for o_ref in o_refs:
            o_ref[...] = w_ref[...].astype(BF16)

    quarter = pl.BlockSpec((tr, pc), lambda h, i, s: (h * nt + i, 0))
    outs = _call_with_side(
        body, "cast_" + name, (2, nt), [quarter],
        [pl.BlockSpec((tr, pc), lambda h, i, s: at(s[0], h, i))] + [quarter] * also_alone,
        [jax.ShapeDtypeStruct(full, BF16)] + [jax.ShapeDtypeStruct(wq.shape, BF16)] * also_alone,
        [], ("parallel", "parallel"), (place, wq), None, n_prefetch=1)
    return tuple(outs) if also_alone else outs[0]


def _g_w_in_half(u1, dz, place, own, side=None):
    T, K = u1.shape
    N = dz.shape[1]
    hk, tn = K // 2, min(MM_TN, N)
    half = (lambda s: s[1]) if own else (lambda s: 1 - s[1])

    def body(place_ref, a_ref, b_ref, o_ref):
        o_ref[...] = _tn(a_ref[...], b_ref[...]).astype(BF16)

    outs = _call_with_side(
        body, "g_w_in_keep" if own else "g_w_in_send", (N // tn,),
        [pl.BlockSpec((T, hk), lambda j, s: (0, half(s))), pl.BlockSpec((T, tn), lambda j, s: (0, j))],
        [pl.BlockSpec((hk, tn), lambda j, s: (0, j))], [jax.ShapeDtypeStruct((hk, N), BF16)],
        [], ("parallel",), (place, u1, dz), side, n_prefetch=1)
    return outs[0] if side is None else outs


def _chip_sum(name, grad, theirs, place, kept_rows=False):
    pr, pc = theirs.shape[1:]
    tr, nt, at = _piece_tiles(name, (2 * grad.shape[0], grad.shape[1]) if kept_rows else grad.shape)
    if kept_rows:
        at = lambda q, half, i: (i, q)

    def body(place_ref, g_ref, t_ref, o_ref):
        o_ref[...] = (g_ref[...].astype(F32) + t_ref[...].astype(F32)).astype(BF16)

    piece = pl.BlockSpec((None, tr, pc), lambda q, i, s: (q, i, 0))
    return pl.pallas_call(
        body, name="chip_sum_" + name,
        grid_spec=pltpu.PrefetchScalarGridSpec(
            num_scalar_prefetch=1, grid=(4, nt),
            in_specs=[pl.BlockSpec((tr, pc), lambda q, i, s: at(q, s[1], i)), piece], out_specs=piece),
        out_shape=jax.ShapeDtypeStruct(theirs.shape, BF16),
        compiler_params=_cparams(("parallel", "parallel")),
    )(place, grad, theirs)


def _piece_sum(name, chip_sums, got, place, after=None):
    pr, pc = chip_sums.shape[1:]
    tr = min(ROW_TILE, pr)

    def body(place_ref, own_ref, got_ref, *rest):
        rest[-1][...] = (own_ref[...].astype(F32) + got_ref[0].astype(F32) + got_ref[1].astype(F32)
                         + got_ref[2].astype(F32))

    return pl.pallas_call(
        body, name="piece_sum_" + name,
        grid_spec=pltpu.PrefetchScalarGridSpec(
            num_scalar_prefetch=1, grid=(pr // tr,),
            in_specs=[pl.BlockSpec((None, tr, pc), lambda i, s: (s[0], i, 0)),
                      pl.BlockSpec((3, tr, pc), lambda i, s: (0, i, 0))] + [ANY] * (after is not None),
            out_specs=pl.BlockSpec((tr, pc), lambda i, s: (i, 0))),
        out_shape=jax.ShapeDtypeStruct((pr, pc), F32),
        compiler_params=_cparams(("parallel",)),
    )(place, chip_sums, got, *([] if after is None else [after]))


def _adam_quarter(name, w, m, v, g_mine, g_sib, place, side=None):
    pr, pc = g_mine.shape
    tr = min(ROW_TILE // 2, pr)
    nt = pr // tr

    def body(place_ref, w_ref, m_ref, v_ref, gm_ref, gs_ref, go_ref, d_ref, mo_ref, vo_ref):
        g = jnp.where(pl.program_id(0) == place_ref[1], gm_ref[...], gs_ref[...])
        delta, mn, vn = _adam_math(w_ref[...], g, m_ref[...], v_ref[...])
        go_ref[...] = g
        d_ref[...] = delta
        mo_ref[...] = mn
        vo_ref[...] = vn

    quarter = pl.BlockSpec((tr, pc), lambda h, i, s: (h * nt + i, 0))
    mine = pl.BlockSpec((tr, pc), lambda h, i, s: (jnp.where(h == s[1], i, 0), 0))
    sib = pl.BlockSpec((tr, pc), lambda h, i, s: (jnp.where(h == s[1], 0, i), 0))
    return _call_with_side(
        body, "adam_" + name, (2, nt), [quarter, quarter, quarter, mine, sib], [quarter] * 4,
        [jax.ShapeDtypeStruct(w.shape, F32)] * 4, [], ("parallel", "parallel"),
        (place, w, m, v, g_mine, g_sib), side, n_prefetch=1)


def _adam_half(name, w, m, v, g, place, own, prev=None):
    pr, pc = g.shape
    tr = min(ROW_TILE // 2, pr)
    nt = pr // tr
    half = (lambda s: s[1]) if own else (lambda s: 1 - s[1])

    def body(place_ref, w_ref, m_ref, v_ref, g_ref, *rest):
        go_ref, d_ref, mo_ref, vo_ref = rest[-4:]
        gv = g_ref[...]
        delta, mn, vn = _adam_math(w_ref[...], gv, m_ref[...], v_ref[...])
        go_ref[...] = gv
        d_ref[...] = delta
        mo_ref[...] = mn
        vo_ref[...] = vn

    quarter = pl.BlockSpec((tr, pc), lambda i, s: (half(s) * nt + i, 0))
    n_prev = 0 if prev is None else 4
    return _call_with_side(
        body, "adam_%s_%s" % (name, "own" if own else "sibling"), (nt,),
        [quarter, quarter, quarter, pl.BlockSpec((tr, pc), lambda i, s: (i, 0))] + [ANY] * n_prev, [quarter] * 4,
        [jax.ShapeDtypeStruct(w.shape, F32)] * 4, [], ("parallel",),
        (place, w, m, v, g) + tuple(prev or ()), None, n_prefetch=1,
        aliases={4 + t: t for t in range(n_prev)})


def _adam_math(w, g, m, v):
    m = ADAM_B1 * m + (1.0 - ADAM_B1) * g
    v = ADAM_B2 * v + (1.0 - ADAM_B2) * (g * g)
    m_hat = m / (1.0 - ADAM_B1 ** ADAM_STEP)
    v_hat = v / (1.0 - ADAM_B2 ** ADAM_STEP)
    return -ADAM_LR * (m_hat / (jnp.sqrt(v_hat) + ADAM_EPS) + ADAM_WD * w), m, v


WEIGHTS = ("w_in", "w_branch_a", "w_branch_b", "w_out", "w_up", "w_down")
ROW_SHARDED = ("w_out", "w_down")
ANY = pl.BlockSpec(memory_space=pl.ANY)
MESH = pl.DeviceIdType.MESH


def _place():
    x, y, c = lax.axis_index("x"), lax.axis_index("y"), lax.axis_index("c")
    chips = [(1 - x, y), (x, 1 - y), (1 - x, 1 - y)]
    return x, y, c, 2 * x + y, chips, [2 * cx + cy for cx, cy in chips]


def _piece(full_ref, name, q, half):
    K, N = full_ref.shape
    if name in ROW_SHARDED:
        rows = K // 8
        return full_ref.at[pl.ds(q * (2 * rows) + half * rows, rows), :]
    return full_ref.at[pl.ds(half * (K // 2), K // 2), pl.ds(q * (N // 4), N // 4)]


def _piece_shape(name, full_shape):
    K, N = full_shape
    return (K // 8, N) if name in ROW_SHARDED else (K // 2, N // 4)


def _full_shape(name, quarter_shape):
    Kq, Nq = quarter_shape
    return (4 * Kq, Nq) if name in ROW_SHARDED else (Kq, 4 * Nq)


def _remote(src, dst, send_sem, recv_sem, device):
    return pltpu.make_async_remote_copy(src_ref=src, dst_ref=dst, send_sem=send_sem, recv_sem=recv_sem,
                                        device_id=device, device_id_type=MESH)


def _z_part(u1, w_in, z_prev, place, k0, count, side=None, own_quarter=False):
    T, K = u1.shape
    nq = w_in.shape[1] if own_quarter else w_in.shape[1] // 4
    N = 4 * nq
    tn = nq // 2 if (nq // 2) % LANE == 0 else nq
    tm = min(MM_TM, T)
    per = nq // tn
    col = lambda g, j, s: (s[0] ^ (k0 + g)) * per + j
    ins = [pl.BlockSpec((tm, K), lambda g, i, j, s: (i, 0)),
           pl.BlockSpec((K, tn), (lambda g, i, j, s: (0, j)) if own_quarter else (lambda g, i, j, s: (0, col(g, j, s))))]
    operands = [place, u1, w_in]
    if z_prev is not None:
        ins.append(ANY)
        operands.append(z_prev)

    def body(place_ref, a_ref, b_ref, *rest):
        rest[-1][...] = _nn(a_ref[...], b_ref[...].astype(BF16))

    return _call_with_side(
        body, "z_part_%d" % k0, (count, T // tm, per), ins,
        [pl.BlockSpec((tm, tn), lambda g, i, j, s: (i, col(g, j, s)))], [jax.ShapeDtypeStruct((T, N), F32)],
        [], ("parallel",) * 3, tuple(operands), side, n_prefetch=1, aliases={} if z_prev is None else {2: 0},
        borrow={0: 1} if side is not None and side.aliased and side.aliased[0] is w_in else None)


def _rows(ref, span):
    return ref if span is None else ref.at[pl.ds(span[0], span[1]), :]


def _gather_moves(items):
    count = {"near": lambda arg: 2, "far": lambda arg: 1, "pass": len}

    def build(reads, aliased, fresh, send_sems, recv_sems, off=0):
        x, y, c, p, chips, chip_ids = _place()
        south = c == 0
        far_src = jnp.where(south, chip_ids[0], chip_ids[1])
        far_dst = (jnp.where(south, x, 1 - x), jnp.where(south, 1 - y, y), c)
        out = []

        def add(ref, device):
            k = off + len(out)
            out.append(_remote(ref, ref, send_sems.at[k], recv_sems.at[k], device))

        for (name, _, moves), ref in zip(items, aliased):
            for kind, arg in moves:
                if kind == "near":
                    for chip in chips[:2]:
                        add(_rows(_piece(ref, name, p, c), arg), (*chip, c))
                elif kind == "far":
                    add(_rows(_piece(ref, name, far_src, c), arg), far_dst)
                else:
                    for j in arg:
                        add(_piece(ref, name, chip_ids[j], c), (x, y, 1 - c))
        return out

    nsem = sum(count[kind](arg) for _, _, moves in items for kind, arg in moves)
    return _Side(build, nsem, aliased=[a for _, a, _ in items])


def _ici_far(names, fulls, rows=None):
    return _gather_moves([(n, a, [("far", r)]) for n, a, r in zip(names, fulls, rows or [None] * len(names))])


def _d2d_gather(names, fulls, which=(0, 1, 2)):
    return _gather_moves([(n, a, [("pass", which)]) for n, a in zip(names, fulls)])


def _sib_send(names, grads):
    def build(reads, aliased, fresh, send_sems, recv_sems, off=0):
        x, y, c, _, _, _ = _place()
        out = []
        for i, name in enumerate(names):
            for q in range(4):
                k = off + 4 * i + q
                out.append(_remote(_piece(reads[i], name, q, 1 - c), fresh[i].at[q], send_sems.at[k], recv_sems.at[k],
                                   (x, y, 1 - c)))
        return out

    shapes = [jax.ShapeDtypeStruct((4,) + _piece_shape(name, g.shape), BF16) for name, g in zip(names, grads)]
    return _Side(build, 4 * len(names), reads=grads, fresh=shapes)


def _sib_send_half(sent):
    K2, N = sent.shape

    def build(reads, aliased, fresh, send_sems, recv_sems, off=0):
        x, y, c, _, _, _ = _place()
        return [_remote(reads[0].at[:, pl.ds(q * (N // 4), N // 4)], fresh[0].at[q], send_sems.at[off + q],
                        recv_sems.at[off + q], (x, y, 1 - c)) for q in range(4)]

    return _Side(build, 4, reads=[sent], fresh=[jax.ShapeDtypeStruct((4, K2, N // 4), BF16)])


HBM = pl.BlockSpec(memory_space=pltpu.HBM)
SEM = pl.BlockSpec(memory_space=pltpu.SEMAPHORE)


def _exchange_copies(s_refs, land_refs, send_sems, recv_sems):
    _, _, c, _, chips, chip_ids = _place()
    return [_remote(s_ref.at[cid], land_ref.at[j], send_sems.at[3 * i + j], recv_sems.at[3 * i + j], (*chip, c))
            for i, (s_ref, land_ref) in enumerate(zip(s_refs, land_refs))
            for j, (chip, cid) in enumerate(zip(chips, chip_ids))]


def _exchange_start(name, chip_sums):
    n = len(chip_sums)

    def body(*refs):
        for cp in _exchange_copies(refs[:n], refs[n:2 * n], refs[2 * n], refs[2 * n + 1]):
            cp.start()
        refs[-1][...] = jnp.zeros_like(refs[-1])

    lands = [jax.ShapeDtypeStruct((3,) + s.shape[1:], s.dtype) for s in chip_sums]
    hbm = lambda a: pltpu.with_memory_space_constraint(a, pltpu.HBM)
    outs = pl.pallas_call(
        body, name="exchange_start_" + name,
        out_shape=(pltpu.SemaphoreType.DMA((3 * n,)), pltpu.SemaphoreType.DMA((3 * n,)),
                   *[pltpu.HBM(a.shape, a.dtype) for a in chip_sums + lands], jax.ShapeDtypeStruct((8, LANE), F32)),
        in_specs=(HBM,) * (2 * n), out_specs=(SEM, SEM) + (HBM,) * (2 * n) + (pl.BlockSpec(memory_space=pltpu.VMEM),),
        input_output_aliases={i: 2 + i for i in range(2 * n)},
        compiler_params=pltpu.CompilerParams(has_side_effects=pltpu.SideEffectType.DATAFLOW_SIDE_EFFECTING),
    )(*[hbm(s) for s in chip_sums], *[hbm(lax.empty(a.shape, a.dtype)) for a in lands])
    return outs[0], outs[1], list(outs[2:2 + n]), list(outs[2 + n:2 + 2 * n]), outs[-1]


def _exchange_wait(name, flight, after):
    send_sems, recv_sems, s_thru, land_thru, _ = flight
    n = len(s_thru)

    def body(*refs):
        for cp in _exchange_copies(refs[:n], refs[n:2 * n], refs[2 * n], refs[2 * n + 1]):
            cp.wait_send()
            cp.wait_recv()

    outs = pl.pallas_call(
        body, name="exchange_wait_" + name,
        out_shape=tuple(pltpu.HBM(a.shape, a.dtype) for a in s_thru + land_thru),
        in_specs=(HBM,) * (2 * n) + (SEM, SEM, ANY), out_specs=(HBM,) * (2 * n),
        input_output_aliases={i: i for i in range(2 * n)},
        compiler_params=pltpu.CompilerParams(has_side_effects=pltpu.SideEffectType.DATAFLOW_SIDE_EFFECTING),
    )(*s_thru, *land_thru, send_sems, recv_sems, after)
    return list(outs[:n]), list(outs[n:])


def _move_copies(kind, name, f_ref, send_sems, recv_sems):
    return _gather_moves([(name, None, [(kind, None)])]).build([], [f_ref], [], send_sems, recv_sems)


def _move_start(kind, name, full):
    def body(f_ref, send_sems, recv_sems, f_thru, token):
        for cp in _move_copies(kind, name, f_ref, send_sems, recv_sems):
            cp.start()
        token[...] = jnp.zeros_like(token)

    return pl.pallas_call(
        body, name=kind + "_start_" + name,
        out_shape=(pltpu.SemaphoreType.DMA((2,)), pltpu.SemaphoreType.DMA((2,)), pltpu.HBM(full.shape, full.dtype),
                   jax.ShapeDtypeStruct((8, LANE), F32)),
        in_specs=(HBM,), out_specs=(SEM, SEM, HBM, pl.BlockSpec(memory_space=pltpu.VMEM)),
        input_output_aliases={0: 2},
        compiler_params=pltpu.CompilerParams(has_side_effects=pltpu.SideEffectType.DATAFLOW_SIDE_EFFECTING),
    )(pltpu.with_memory_space_constraint(full, pltpu.HBM))


def _move_wait(kind, name, flight, after):
    send_sems, recv_sems, f_thru, _ = flight

    def body(f_ref, send_sems, recv_sems, after_ref, f_out):
        for cp in _move_copies(kind, name, f_ref, send_sems, recv_sems):
            cp.wait_send()
            cp.wait_recv()

    return pl.pallas_call(
        body, name=kind + "_wait_" + name, out_shape=pltpu.HBM(f_thru.shape, f_thru.dtype),
        in_specs=(HBM, SEM, SEM, ANY), out_specs=HBM, input_output_aliases={0: 0},
        compiler_params=pltpu.CompilerParams(has_side_effects=pltpu.SideEffectType.DATAFLOW_SIDE_EFFECTING),
    )(f_thru, send_sems, recv_sems, after)


def _share_copy(h_ref, land_ref, send_sems, recv_sems):
    x, y, c, _, _, _ = _place()
    return _remote(h_ref, land_ref, send_sems.at[0], recv_sems.at[0], (x, y, 1 - c))


def _share_start(name, half):
    def body(h_ref, land_ref, send_sems, recv_sems, h_thru, land_thru, token):
        _share_copy(h_ref, land_ref, send_sems, recv_sems).start()
        token[...] = jnp.zeros_like(token)

    hbm = lambda a: pltpu.with_memory_space_constraint(a, pltpu.HBM)
    return pl.pallas_call(
        body, name="share_start_" + name,
        out_shape=(pltpu.SemaphoreType.DMA((1,)), pltpu.SemaphoreType.DMA((1,)), pltpu.HBM(half.shape, half.dtype),
                   pltpu.HBM(half.shape, half.dtype), jax.ShapeDtypeStruct((8, LANE), F32)),
        in_specs=(HBM, HBM), out_specs=(SEM, SEM, HBM, HBM, pl.BlockSpec(memory_space=pltpu.VMEM)),
        input_output_aliases={0: 2, 1: 3},
        compiler_params=pltpu.CompilerParams(has_side_effects=pltpu.SideEffectType.DATAFLOW_SIDE_EFFECTING),
    )(hbm(half), hbm(lax.empty(half.shape, half.dtype)))


def _share_wait(name, flight, after):
    send_sems, recv_sems, h_thru, land_thru, _ = flight

    def body(h_ref, land_ref, send_sems, recv_sems, after_ref, h_out, land_out):
        cp = _share_copy(h_ref, land_ref, send_sems, recv_sems)
        cp.wait_send()
        cp.wait_recv()

    return pl.pallas_call(
        body, name="share_wait_" + name,
        out_shape=(pltpu.HBM(h_thru.shape, h_thru.dtype), pltpu.HBM(land_thru.shape, land_thru.dtype)),
        in_specs=(HBM, HBM, SEM, SEM, ANY), out_specs=(HBM, HBM), input_output_aliases={0: 0, 1: 1},
        compiler_params=pltpu.CompilerParams(has_side_effects=pltpu.SideEffectType.DATAFLOW_SIDE_EFFECTING),
    )(h_thru, land_thru, send_sems, recv_sems, after)[1]


def _sib_share(halves):
    def build(reads, aliased, fresh, send_sems, recv_sems, off=0):
        x, y, c, _, _, _ = _place()
        return [_remote(reads[i], fresh[i], send_sems.at[off + i], recv_sems.at[off + i], (x, y, 1 - c))
                for i in range(len(halves))]

    return _Side(build, len(halves), reads=halves, fresh=[jax.ShapeDtypeStruct(h.shape, F32) for h in halves])


def _run_side(name, side):
    nr, na = len(side.reads), len(side.aliased)

    def body(*refs):
        n_in, n_out = nr + na, na + len(side.fresh)
        outs = refs[n_in:n_in + n_out]
        copies = side.build(refs[:nr], outs[:na], outs[na:], *refs[-2:])
        for cp in copies:
            cp.start()
        for cp in copies:
            cp.wait()

    return pl.pallas_call(
        body, name=name, in_specs=side.in_specs(), out_specs=side.out_specs(), out_shape=side.out_shape(),
        input_output_aliases=side.aliases(0, 0), scratch_shapes=side.scratch(),
    )(*side.operands())


def _small_allreduce_adam(gpart, w, m, v, after):
    R = gpart.shape[0]

    def body(g_ref, w_ref, m_ref, v_ref, after_ref, go_ref, d_ref, mo_ref, vo_ref, buf, send_sems, recv_sems):
        x, y, c = lax.axis_index("x"), lax.axis_index("y"), lax.axis_index("c")
        me = 4 * x + 2 * y + c
        buf[me] = g_ref[...]
        copies = []
        for k in range(1, 8):
            fx, fy, fc = (k >> 2) & 1, (k >> 1) & 1, k & 1
            peer = (1 - x if fx else x, 1 - y if fy else y, 1 - c if fc else c)
            cp = _remote(g_ref, buf.at[me], send_sems.at[k - 1], recv_sems.at[k - 1], peer)
            cp.start()
            copies.append((cp, 4 * peer[0] + 2 * peer[1] + peer[2]))
        for k, (cp, pid) in enumerate(copies):
            _remote(g_ref, buf.at[pid], send_sems.at[k], recv_sems.at[k], (x, y, c)).wait_recv()
        for cp, _ in copies:
            cp.wait_send()
        g = buf[0]
        for d in range(1, 8):
            g = g + buf[d]
        delta, mn, vn = _adam_math(w_ref[...], g, m_ref[...], v_ref[...])
        go_ref[...] = g
        d_ref[...] = delta
        mo_ref[...] = mn
        vo_ref[...] = vn

    vm = pl.BlockSpec(memory_space=pltpu.VMEM)
    return pl.pallas_call(
        body, name="small_allreduce_adam",
        in_specs=[vm] * 4 + [ANY], out_specs=[vm] * 4,
        out_shape=[jax.ShapeDtypeStruct((R, LANE), F32)] * 4,
        scratch_shapes=[pltpu.VMEM((8, R, LANE), F32), pltpu.SemaphoreType.DMA((7,)), pltpu.SemaphoreType.DMA((7,))],
    )(gpart, w, m, v, after)


def _pack(arrs):
    flat = jnp.concatenate([a.reshape(-1).astype(F32) for a in arrs])
    rows = -(-flat.shape[0] // (8 * LANE)) * 8
    return jnp.pad(flat, (0, rows * LANE - flat.shape[0])).reshape(rows, LANE)


def _unpack(packed, like):
    flat, out, off = packed.reshape(-1), [], 0
    for a in like:
        out.append(flat[off:off + a.size].reshape(a.shape))
        off += a.size
    return out


def kernel(x, w_in, lb_logits, hg_norm_w, rel_bias, w_branch_a, w_branch_b, w_out, norm_mix_w, norm_mlp_w, w_up, w_down, norm_final_w, loss_target, m_w_in, m_lb_logits, m_hg_norm_w, m_rel_bias, m_w_branch_a, m_w_branch_b, m_w_out, m_norm_mix_w, m_norm_mlp_w, m_w_up, m_w_down, m_norm_final_w, v_w_in, v_lb_logits, v_hg_norm_w, v_rel_bias, v_w_branch_a, v_w_branch_b, v_w_out, v_norm_mix_w, v_norm_mlp_w, v_w_up, v_w_down, v_norm_final_w):
    T, D = x.shape[1], x.shape[2]
    x2, tgt = x.reshape(T, D), loss_target.reshape(T, D)
    big = dict(w_in=(w_in, m_w_in, v_w_in), w_branch_a=(w_branch_a, m_w_branch_a, v_w_branch_a),
               w_branch_b=(w_branch_b, m_w_branch_b, v_w_branch_b), w_out=(w_out, m_w_out, v_w_out),
               w_up=(w_up, m_w_up, v_w_up), w_down=(w_down, m_w_down, v_w_down))
    big = {k: tuple(a[0] for a in v) for k, v in big.items()}
    nfw = norm_final_w.reshape(1, D)

    place = jnp.stack([2 * lax.axis_index("x") + lax.axis_index("y"), lax.axis_index("c")]).astype(jnp.int32)
    small3 = ["w_branch_a", "w_branch_b", "w_out"]

    def span(name, lo, hi):
        pr = big[name][0].shape[0] // 2
        return (pr * lo // 16, pr * (hi - lo) // 16)

    w_in_full, w_in_own = _cast_into_full("w_in", big["w_in"][0], place, also_alone=True)
    flight_in = _move_start("near", "w_in", w_in_full)
    Wf = {name: _cast_into_full(name, big[name][0], place) for name in WEIGHTS if name != "w_in"}

    u1 = _rms_fwd("norm_mix", x2, norm_mix_w, side=_after(flight_in[-1]))
    z = _z_part(u1[0], w_in_own, None, place, 0, 1, own_quarter=True)[0]
    u1 = u1[0]
    Wf["w_in"] = _move_wait("near", "w_in", flight_in, z)

    def carried(**moves):
        def arg(n, k, a):
            if k == "pass":
                return a[0] if a else (0, 1, 2)
            return span(n, *a) if a else None

        return _gather_moves([(n, Wf[n], [(k, arg(n, k, a)) for k, *a in ms]) for n, ms in moves.items()]), list(moves)

    def land(names, outs):
        Wf.update(zip(names, outs[-len(names):]))
        return outs[:-len(names)]

    side, names = carried(w_in=[("pass", (0, 1))], w_out=[("near", 0, 8)])
    land(names, _run_side("pass_w_in_near", side))
    flight_in = _move_start("far", "w_in", Wf["w_in"])
    z = _z_part(u1, flight_in[2], z, place, 1, 2)[0]
    Wf["w_in"] = _move_wait("far", "w_in", flight_in, z)
    side, names = carried(w_in=[("pass", (2,))], w_branch_a=[("near",)], w_branch_b=[("near",)])
    land(names, _run_side("pass_w_in_far", side))
    side, names = carried(w_branch_a=[("far",)], w_branch_b=[("far",)], w_out=[("near", 8, 16)])
    (z,) = land(names, _z_part(u1, Wf["w_in"], z, place, 3, 1, side=side))
    side, names = carried(w_branch_a=[("pass",)], w_branch_b=[("pass",)], w_up=[("near", 0, 10)])
    ya, o_hg, states = land(names, _hg_fwd(z, lb_logits, hg_norm_w, side=side))
    bias_win = _bias_window(rel_bias[0])
    side, names = carried(w_out=[("far",)], w_up=[("near", 10, 16), ("far", 0, 10)], w_down=[("near", 0, 3)])
    (yb,) = land(names, _at_fwd(z, bias_win, side=side))
    side, names = carried(w_out=[("pass",)], w_up=[("far", 10, 14)])
    (pa,) = land(names, _mm("branch_a", ya, Wf["w_branch_a"], "nn", [BF16], side=side))
    side, names = carried(w_up=[("far", 14, 16)], w_down=[("near", 3, 4)])
    (pb,) = land(names, _mm("branch_b", yb, Wf["w_branch_b"], "nn", [BF16], side=side))
    side, names = carried(w_down=[("near", 4, 8)])
    (merged,) = land(names, _merge(z, pa, pb, side=side))
    add = lambda acc, res: (acc + res,)
    side, names = carried(w_up=[("pass",)], w_down=[("near", 8, 12)])
    (h1,) = land(names, _mm("out_proj", merged, Wf["w_out"], "nn", [F32], extras=[x2], epilogue=add, side=side))
    side, names = carried(w_down=[("near", 12, 14)])
    (u2,) = land(names, _rms_fwd("norm_mlp", h1, norm_mlp_w, side=side))
    relu2 = lambda acc: (acc, jnp.square(jnp.maximum(acc, 0.0)))
    side, names = carried(w_down=[("near", 14, 16), ("far", 0, 14)])
    a_pre, act = land(names, _mm("mlp_up", u2, Wf["w_up"], "nn", [F32, BF16], epilogue=relu2, side=side))
    (Wf["w_down"],) = _run_side("far_w_down", _ici_far(["w_down"], [Wf["w_down"]], rows=[span("w_down", 14, 16)]))
    (Wf["w_down"],) = _run_side("pass_w_down", _d2d_gather(["w_down"], [Wf["w_down"]]))
    h2 = _mm("mlp_down", act, Wf["w_down"], "nn", [F32], extras=[h1], epilogue=add)
    loss_part, dh2, dh2b, d_nf = _loss_head(h2, tgt, nfw)

    drelu2 = lambda acc, a: (acc * (2.0 * jnp.maximum(a, 0.0)),)
    da = _mm("d_act", dh2b, Wf["w_down"], "nt", [BF16], extras=[a_pre], epilogue=drelu2)
    G = {}
    G["w_down"] = _mm("g_w_down", act, dh2b, "tn", [BF16])
    G["w_up"] = _mm("g_w_up", u2, da, "tn", [BF16])
    T_, S_, GOT = {}, {}, {}
    du2, T_["w_down"], T_["w_up"] = _mm("d_u2", da, Wf["w_up"], "nt", [F32],
                                        side=_sib_send(["w_down", "w_up"], [G["w_down"], G["w_up"]]))
    mlp2 = ["w_down", "w_up"]
    flight_mlp = _exchange_start("mlp", [_chip_sum(n, G[n], T_[n], place) for n in mlp2])
    dh1, dh1b, d_nmlp = _rms_bwd("norm_mlp_bwd", du2, h1, norm_mlp_w, dh2, side=_after(flight_mlp[-1]))
    dmerged = _mm("d_merged", dh1b, Wf["w_out"], "nt", [F32])
    G["w_out"] = _mm("g_w_out", merged, dh1b, "tn", [BF16])
    dpa, dpb, dz_ga, dz_gb = _dmerge(dmerged, z, pa, pb)
    dya = _mm("d_ya", dpa, Wf["w_branch_a"], "nt", [F32])
    dyb = _mm("d_yb", dpb, Wf["w_branch_b"], "nt", [F32])
    G["w_branch_a"] = _mm("g_w_a", ya, dpa, "tn", [BF16])
    G["w_branch_b"] = _mm("g_w_b", yb, dpb, "tn", [BF16])
    dz_q, dz_f, dz_i, dz_g, d_lbl, d_hgw, *sent = _hg_bwd(
        z, o_hg, dya, states, lb_logits, hg_norm_w, side=_sib_send(small3, [G[n] for n in small3]))
    flight_small = _exchange_start("small", [_chip_sum(n, G[n], t, place) for n, t in zip(small3, sent)])
    dz_aq, dz_ak, dz_av, dbias_win = _at_bwd(z, dyb, bias_win, side=_after(flight_small[-1]))
    dz = jnp.concatenate([dz_q, dz_f, dz_i, dz_g, dz_aq, dz_ak, dz_av, dz_ga, dz_gb], axis=1)
    g_send = _g_w_in_half(u1, dz, place, False)
    g_keep, T_["w_in"] = _g_w_in_half(u1, dz, place, True, side=_sib_send_half(g_send))
    for names, flight in ((mlp2, flight_mlp), (small3, flight_small)):
        sums, got = _exchange_wait("_".join(names), flight, g_keep)
        S_.update(zip(names, sums))
        GOT.update(zip(names, got))
    S_["w_in"] = _chip_sum("w_in", g_keep, T_["w_in"], place, kept_rows=True)
    early = [n for n in WEIGHTS if n != "w_in"]
    flight = _exchange_start("w_in", [S_["w_in"]])
    H_ = {n: _piece_sum(n, S_[n], GOT[n], place, after=flight[-1]) for n in early}
    share_early = _sib_share([H_[n] for n in early])
    share_early.reads.append(flight[-1])
    du1, *shared = _mm("d_u1", dz, Wf["w_in"], "nt", [F32], side=share_early)
    O_ = dict(zip(early, shared))
    grad_x, _, d_nmix = _rms_bwd("norm_mix_bwd", du1, x2, norm_mix_w, dh1)
    d_rel = _bias_window_grad(dbias_win)
    big_out = {}
    for name in early:
        outs = _adam_quarter(name, *big[name], H_[name], O_[name], place)
        big_out[name] = tuple(a[None] for a in outs)
    (S_["w_in"],), (got_in,) = _exchange_wait("w_in", flight, outs[1])
    H_["w_in"] = _piece_sum("w_in", S_["w_in"], got_in, place)
    sharing = _share_start("w_in", H_["w_in"])
    first = _adam_half("w_in", *big["w_in"], sharing[2], place, True)
    outs = _adam_half("w_in", *big["w_in"], _share_wait("w_in", sharing, first[0]), place, False, prev=first)
    big_out["w_in"] = tuple(a[None] for a in outs)

    smalls = [("lb_logits", lb_logits, m_lb_logits, v_lb_logits, d_lbl),
              ("hg_norm_w", hg_norm_w, m_hg_norm_w, v_hg_norm_w, d_hgw),
              ("rel_bias", rel_bias, m_rel_bias, v_rel_bias, d_rel),
              ("norm_mix_w", norm_mix_w, m_norm_mix_w, v_norm_mix_w, d_nmix),
              ("norm_mlp_w", norm_mlp_w, m_norm_mlp_w, v_norm_mlp_w, d_nmlp),
              ("norm_final_w", norm_final_w, m_norm_final_w, v_norm_final_w, d_nf)]
    one = jnp.zeros((1,), F32)
    like = [s[1] for s in smalls] + [one]
    packed = _small_allreduce_adam(_pack([s[4] for s in smalls] + [loss_part[0, :1]]), _pack(like),
                                   _pack([s[2] for s in smalls] + [one]), _pack([s[3] for s in smalls] + [one]), got_in)
    unpacked = [_unpack(p, like) for p in packed]
    small_out = {s[0]: vals for s, vals in zip(smalls, zip(*unpacked))}
    loss = unpacked[0][-1].reshape(())
    order = ["w_in", "lb_logits", "hg_norm_w", "rel_bias", "w_branch_a", "w_branch_b", "w_out", "norm_mix_w",
             "norm_mlp_w", "w_up", "w_down", "norm_final_w"]
    res = {**big_out, **small_out}
    return (loss, grad_x.reshape(x.shape), *[res[n][0] for n in order], *[res[n][1] for n in order],
            *[res[n][2] for n in order], *[res[n][3] for n in order])
```

```python
import functools

import jax
import jax.numpy as jnp
from jax import lax
from jax.experimental import pallas as pl
from jax.experimental.pallas import tpu as pltpu

F32 = jnp.float32
BF16 = jnp.bfloat16
HIGHEST = lax.Precision.HIGHEST

D_MODEL = 2048
SEQ = 2048
CHUNK = 64
HG_HEADS = 8
HG_D = 128
AT_HEADS = 16
AT_DH = 64
LEFT = 8
REL_CLIP = 256
D_FF = 8192
EPS = 1e-6
ADAM_LR = 0.001
ADAM_B1 = 0.9
ADAM_B2 = 0.999
ADAM_EPS = 1e-08
ADAM_WD = 0.01
ADAM_STEP = 10

LANE = 128
NEG = -1e30
EXP_CLAMP = 80.0
VMEM_LIMIT = 48 * 1024 * 1024
MM_TM, MM_TN, MM_TK = 1024, 1024, 2816
ROW_TILE = 256
QB = 2 * CHUNK


def _hgw():
    return HG_HEADS * HG_D


def _atw():
    return AT_HEADS * AT_DH


def _cparams(sem):
    return pltpu.CompilerParams(dimension_semantics=sem, vmem_limit_bytes=VMEM_LIMIT)


def _sigmoid(x):
    return jax.nn.sigmoid(x)


def _dot(a, b, dims, precision=None):
    return lax.dot_general(a, b, (dims, ((), ())), preferred_element_type=F32, precision=precision)


def _nn(a, b, precision=None):
    return _dot(a, b, ((1,), (0,)), precision)


def _nt(a, b, precision=None):
    return _dot(a, b, ((1,), (1,)), precision)


def _tn(a, b, precision=None):
    return _dot(a, b, ((0,), (0,)), precision)


class _Side:
    def __init__(self, build, nsem, reads=(), aliased=(), fresh=()):
        self.build, self.nsem = build, nsem
        self.reads, self.aliased, self.fresh = list(reads), list(aliased), list(fresh)

    def operands(self):
        return self.reads + self.aliased

    def in_specs(self):
        return [ANY] * len(self.operands())

    def out_specs(self):
        return [ANY] * (len(self.aliased) + len(self.fresh))

    def out_shape(self):
        return [jax.ShapeDtypeStruct(a.shape, a.dtype) for a in self.aliased] + self.fresh

    def aliases(self, n_in, n_out):
        return {n_in + len(self.reads) + t: n_out + t for t in range(len(self.aliased))}

    def scratch(self):
        return [pltpu.SemaphoreType.DMA((self.nsem,)), pltpu.SemaphoreType.DMA((self.nsem,))]

    def hooks(self, in_refs, out_refs, sems, first, last):
        nr, na = len(self.reads), len(self.aliased)
        args = (in_refs[:nr], out_refs[:na], out_refs[na:], *sems)

        @pl.when(first)
        def _():
            for cp in self.build(*args):
                cp.start()

        @pl.when(last)
        def _():
            for cp in self.build(*args):
                cp.wait()


def _after(*tokens):
    return _Side(lambda *args: [], 1, reads=tokens)


def _side_parts(side):
    if side is None:
        return [], [], [], [], lambda n_in, n_out: {}, []
    return side.operands(), side.in_specs(), side.out_specs(), side.out_shape(), side.aliases, side.scratch()


def _call_with_side(body, name, grid, in_specs, out_specs, out_shape, scratch, sem, operands, side, n_prefetch=0,
                    aliases=None, borrow=None):
    _, _, s_out, s_shape, _, s_scr = _side_parts(side)
    n_in, n_out = n_prefetch + len(in_specs), len(out_specs)
    borrow = borrow or {}
    s_ops, s_alias = [], {}
    if side is not None:
        keep = [t for t in range(len(side.aliased)) if t not in borrow]
        s_ops = side.reads + [side.aliased[t] for t in keep]
        s_alias = {n_in + len(side.reads) + pos: n_out + t for pos, t in enumerate(keep)}
        s_alias.update({n_prefetch + i: n_out + t for t, i in borrow.items()})
    s_in = [ANY] * len(s_ops)
    n_sin, n_sout = len(s_ops), len(s_out)

    def wrapped(*refs):
        a, b, c = n_in + n_sin, n_in + n_sin + n_out, n_in + n_sin + n_out + n_sout
        ids = [pl.program_id(d) for d in range(len(grid))]
        first = functools.reduce(lambda p, q: p & q, [i == 0 for i in ids])
        last = functools.reduce(lambda p, q: p & q, [i == g - 1 for i, g in zip(ids, grid)])
        side.hooks(refs[n_in:a], refs[b:c], refs[-2:], first, last)
        body(*refs[:n_in], *refs[a:b], *refs[c:-2])

    spec = dict(grid=grid, in_specs=in_specs + s_in, out_specs=out_specs + s_out, scratch_shapes=scratch + s_scr)
    if n_prefetch:
        spec = dict(grid_spec=pltpu.PrefetchScalarGridSpec(num_scalar_prefetch=n_prefetch, **spec))
    return pl.pallas_call(
        body if side is None else wrapped, name=name, out_shape=out_shape + s_shape,
        input_output_aliases={**s_alias, **{n_prefetch + i: o for i, o in (aliases or {}).items()}},
        compiler_params=_cparams(sem if side is None else ("arbitrary",) * len(grid)), **spec,
    )(*operands, *s_ops)


def _mm_tk(K):
    if K <= MM_TK:
        return K
    return max(t for t in range(LANE, MM_TK + 1, LANE) if K % t == 0)


def _mm(name, a, b, mode, out_dtypes, extras=(), epilogue=None, side=None):
    if mode == "nn":
        (M, K), (K2, N) = a.shape, b.shape
    elif mode == "nt":
        (M, K), (N, K2) = a.shape, b.shape
    else:
        (K, M), (K2, N) = a.shape, b.shape
    assert K == K2, (name, a.shape, b.shape)
    tm, tn, tk = min(MM_TM, M), min(MM_TN, N), _mm_tk(K)
    assert M % tm == 0 and N % tn == 0 and K % tk == 0, (name, M, N, K)
    ni, nj, nk = M // tm, N // tn, K // tk
    ne, no = len(extras), len(out_dtypes)
    if epilogue is None:
        epilogue = lambda acc: (acc,)
    s_ops, s_in, s_out, s_shape, s_alias, s_scr = _side_parts(side)
    n_in, n_sin, n_sout = 2 + ne, len(s_ops), len(s_out)

    def body(*refs):
        a_ref, b_ref = refs[:2]
        extra_refs = refs[2:n_in]
        out_refs = refs[n_in + n_sin:n_in + n_sin + no]
        rest = refs[n_in + n_sin + no + n_sout:]
        i, j, k = pl.program_id(0), pl.program_id(1), pl.program_id(2)
        if side is not None:
            side.hooks(refs[n_in:n_in + n_sin], refs[n_in + n_sin + no:n_in + n_sin + no + n_sout], rest[-2:],
                       (i == 0) & (j == 0) & (k == 0), (i == ni - 1) & (j == nj - 1) & (k == nk - 1))
        av, bv = a_ref[...].astype(BF16), b_ref[...].astype(BF16)
        prod = _nn(av, bv) if mode == "nn" else _nt(av, bv) if mode == "nt" else _tn(av, bv)

        def finish(acc):
            res = epilogue(acc, *[e[...] for e in extra_refs])
            for o_ref, r in zip(out_refs, res):
                o_ref[...] = r.astype(o_ref.dtype)

        if nk == 1:
            finish(prod)
        else:
            acc_ref = rest[0]

            @pl.when(k == 0)
            def _():
                acc_ref[...] = prod

            @pl.when((k > 0) & (k < nk - 1))
            def _():
                acc_ref[...] += prod

            @pl.when(k == nk - 1)
            def _():
                finish(acc_ref[...] + prod)

    if mode == "nn":
        a_spec = pl.BlockSpec((tm, tk), lambda i, j, k: (i, k))
        b_spec = pl.BlockSpec((tk, tn), lambda i, j, k: (k, j))
    elif mode == "nt":
        a_spec = pl.BlockSpec((tm, tk), lambda i, j, k: (i, k))
        b_spec = pl.BlockSpec((tn, tk), lambda i, j, k: (j, k))
    else:
        a_spec = pl.BlockSpec((tk, tm), lambda i, j, k: (k, i))
        b_spec = pl.BlockSpec((tk, tn), lambda i, j, k: (k, j))
    o_spec = pl.BlockSpec((tm, tn), lambda i, j, k: (i, j))
    sem = ("arbitrary",) * 3 if side is not None else ("parallel", "parallel", "arbitrary")
    outs = pl.pallas_call(
        body, name=name,
        grid=(ni, nj, nk),
        in_specs=[a_spec, b_spec] + [o_spec] * ne + s_in,
        out_specs=[o_spec] * no + s_out,
        out_shape=[jax.ShapeDtypeStruct((M, N), dt) for dt in out_dtypes] + s_shape,
        input_output_aliases=s_alias(n_in, no),
        scratch_shapes=([pltpu.VMEM((tm, tn), F32)] if nk > 1 else []) + s_scr,
        compiler_params=_cparams(sem),
    )(a, b, *extras, *s_ops)
    return outs[0] if len(outs) == 1 else outs


def _row_spec(tr, d):
    return pl.BlockSpec((tr, d), lambda i: (i, 0))


def _vec_spec(d):
    return pl.BlockSpec((1, d), lambda i: (0, 0))


def _rms_fwd(name, x, w, side=None):
    T, D = x.shape
    tr = min(ROW_TILE, T)

    def body(x_ref, w_ref, o_ref):
        xf = x_ref[...]
        r = lax.rsqrt(jnp.mean(xf * xf, axis=-1, keepdims=True) + EPS)
        o_ref[...] = (xf * r * w_ref[...]).astype(BF16)

    outs = _call_with_side(body, name, (T // tr,), [_row_spec(tr, D), _vec_spec(D)], [_row_spec(tr, D)],
                           [jax.ShapeDtypeStruct((T, D), BF16)], [], ("parallel",), (x, w), side)
    return outs[0] if side is None else outs


def _rms_bwd(name, dy, h, w, dres, side=None):
    T, D = h.shape
    tr = min(ROW_TILE, T)

    def body(dy_ref, h_ref, w_ref, dres_ref, dh_ref, dhb_ref, dw_ref):
        @pl.when(pl.program_id(0) == 0)
        def _():
            dw_ref[...] = jnp.zeros_like(dw_ref)

        hf, dyv = h_ref[...], dy_ref[...]
        r = lax.rsqrt(jnp.mean(hf * hf, axis=-1, keepdims=True) + EPS)
        xhat = hf * r
        dw_ref[...] += jnp.sum(dyv * xhat, axis=0, keepdims=True)
        dxh = dyv * w_ref[...]
        dh = dres_ref[...] + r * (dxh - xhat * jnp.mean(dxh * xhat, axis=-1, keepdims=True))
        dh_ref[...] = dh
        dhb_ref[...] = dh.astype(BF16)

    return _call_with_side(
        body, name, (T // tr,),
        [_row_spec(tr, D), _row_spec(tr, D), _vec_spec(D), _row_spec(tr, D)],
        [_row_spec(tr, D), _row_spec(tr, D), _vec_spec(D)],
        [jax.ShapeDtypeStruct((T, D), F32), jax.ShapeDtypeStruct((T, D), BF16), jax.ShapeDtypeStruct((1, D), F32)],
        [], ("arbitrary",), (dy, h, w, dres), side)


def _loss_head(h2, target, w):
    T, D = h2.shape
    tr = min(ROW_TILE, T)

    def body(h_ref, t_ref, w_ref, loss_ref, dh_ref, dhb_ref, dw_ref):
        @pl.when(pl.program_id(0) == 0)
        def _():
            dw_ref[...] = jnp.zeros_like(dw_ref)
            loss_ref[...] = jnp.zeros_like(loss_ref)

        hf, wv = h_ref[...], w_ref[...]
        r = lax.rsqrt(jnp.mean(hf * hf, axis=-1, keepdims=True) + EPS)
        xhat = hf * r
        diff = xhat * wv - t_ref[...]
        loss_ref[...] += 0.5 * jnp.sum(jnp.mean(diff * diff, axis=-1, keepdims=True))
        dyv = diff * (1.0 / D)
        dw_ref[...] += jnp.sum(dyv * xhat, axis=0, keepdims=True)
        dxh = dyv * wv
        dh = r * (dxh - xhat * jnp.mean(dxh * xhat, axis=-1, keepdims=True))
        dh_ref[...] = dh
        dhb_ref[...] = dh.astype(BF16)

    return pl.pallas_call(
        body, name="loss_head", grid=(T // tr,),
        in_specs=[_row_spec(tr, D), _row_spec(tr, D), _vec_spec(D)],
        out_specs=[_vec_spec(LANE), _row_spec(tr, D), _row_spec(tr, D), _vec_spec(D)],
        out_shape=[jax.ShapeDtypeStruct((1, LANE), F32), jax.ShapeDtypeStruct((T, D), F32),
                   jax.ShapeDtypeStruct((T, D), BF16), jax.ShapeDtypeStruct((1, D), F32)],
        compiler_params=_cparams(("arbitrary",)),
    )(h2, target, w)


def _gate_tiles(T, D):
    goff = 4 * _hgw() + 3 * _atw()
    tc = min(1024, D)
    assert goff % tc == 0 and D % tc == 0
    return min(ROW_TILE, T), tc, goff // tc, D // tc


def _merge(z, pa, pb, side=None):
    T, D = pa.shape
    tr, tc, g0, nd = _gate_tiles(T, D)

    def body(ga_ref, gb_ref, pa_ref, pb_ref, o_ref):
        o_ref[...] = (_sigmoid(ga_ref[...]) * pa_ref[...] + _sigmoid(gb_ref[...]) * pb_ref[...]).astype(BF16)

    t = pl.BlockSpec((tr, tc), lambda i, j: (i, j))
    outs = _call_with_side(
        body, "merge", (T // tr, nd),
        [pl.BlockSpec((tr, tc), lambda i, j: (i, g0 + j)), pl.BlockSpec((tr, tc), lambda i, j: (i, g0 + nd + j)), t, t],
        [t], [jax.ShapeDtypeStruct((T, D), BF16)], [], ("parallel", "parallel"), (z, z, pa, pb), side)
    return outs[0] if side is None else outs


def _dmerge(dm, z, pa, pb):
    T, D = pa.shape
    tr, tc, g0, nd = _gate_tiles(T, D)

    def body(dm_ref, ga_ref, gb_ref, pa_ref, pb_ref, dpa_ref, dpb_ref, dga_ref, dgb_ref):
        dmv = dm_ref[...]
        sa, sb = _sigmoid(ga_ref[...]), _sigmoid(gb_ref[...])
        dpa_ref[...] = (dmv * sa).astype(BF16)
        dpb_ref[...] = (dmv * sb).astype(BF16)
        dga_ref[...] = (dmv * pa_ref[...] * sa * (1.0 - sa)).astype(BF16)
        dgb_ref[...] = (dmv * pb_ref[...] * sb * (1.0 - sb)).astype(BF16)

    t = pl.BlockSpec((tr, tc), lambda i, j: (i, j))
    return pl.pallas_call(
        body, name="dmerge", grid=(T // tr, nd),
        in_specs=[t, pl.BlockSpec((tr, tc), lambda i, j: (i, g0 + j)),
                  pl.BlockSpec((tr, tc), lambda i, j: (i, g0 + nd + j)), t, t],
        out_specs=[t, t, t, t],
        out_shape=[jax.ShapeDtypeStruct((T, D), BF16)] * 4,
        compiler_params=_cparams(("parallel", "parallel")),
    )(dm, z, z, pa, pb)


def _hg_gates(xq, xf, lb):
    f = _sigmoid(xf)
    g = lb + (1.0 - lb) * f
    sq = _sigmoid(xq)
    return f, g, jnp.log(g), 1.0 - g, sq, xq * sq * (HG_D ** -0.5)


def _split2(x):
    hi = x.astype(BF16)
    return hi, (x - hi.astype(F32)).astype(BF16)


def _tri_sum(tri, x):
    hi, rest = x.astype(BF16), x - x.astype(BF16).astype(F32)
    mid, lo = _split2(rest)
    return _nn(tri, lo) + _nn(tri, mid) + _nn(tri, hi)


def _hg_decays(lg, tri_incl, rowi):
    b = _tri_sum(tri_incl, lg)
    b_last = jnp.sum(lg, axis=0, keepdims=True)
    b_mid = jnp.sum(jnp.where(rowi <= CHUNK // 2, lg, 0.0), axis=0, keepdims=True)
    return b, b_last, b_mid


HG_GROUP = 2


def _hg_in_specs(T):
    ng = HG_HEADS // HG_GROUP
    return [pl.BlockSpec((T, HG_GROUP * HG_D), lambda h, s=s: (0, s * ng + h)) for s in range(4)]


def _hg_fwd(z, lb_logits, hgw, side=None):
    T = z.shape[0]
    H, d, C, G = HG_HEADS, HG_D, CHUNK, HG_GROUP
    nc = T // C

    def body(hq_ref, hf_ref, hi_ref, hg_ref, lbl_ref, w_ref, ya_ref, o_ref, s_ref):
        lb_all = 1.0 / (1.0 + jnp.exp(lbl_ref[1:2, :] - lbl_ref[0:1, :]))
        wv = w_ref[...]
        row = lax.broadcasted_iota(jnp.int32, (C, C), 0)
        col = lax.broadcasted_iota(jnp.int32, (C, C), 1)
        tril = col <= row
        tri_incl = tril.astype(BF16)
        rowi = lax.broadcasted_iota(jnp.int32, (C, G * d), 0)
        lanes = [slice(hh * d, (hh + 1) * d) for hh in range(G)]
        per_head = lambda fn: jnp.concatenate([fn(hh, sl) for hh, sl in enumerate(lanes)], axis=1)
        wv_all = jnp.tile(wv, (1, G))

        def chunk(c, states):
            rows = pl.ds(pl.multiple_of(c * C, C), C)
            xq, xf, v, xg = hq_ref[rows, :], hf_ref[rows, :], hi_ref[rows, :], hg_ref[rows, :]
            _, _, lg, kk, _, q = _hg_gates(xq, xf, lb_all)
            b, b_last, b_mid = _hg_decays(lg, tri_incl, rowi)
            vb, qe = v.astype(BF16), (q * jnp.exp(b)).astype(BF16)
            qt = (q * jnp.exp(b - b_mid)).astype(BF16)
            kt = (kk * jnp.exp(jnp.minimum(b_mid - b, EXP_CLAMP))).astype(BF16)
            kd, e_last = (kk * jnp.exp(b_last - b)).astype(BF16), jnp.exp(b_last)
            for hh, st in enumerate(states):
                s_ref[hh, c] = st
            o = per_head(lambda hh, sl: _nt(qe[:, sl], states[hh].astype(BF16)))
            a = [jnp.where(tril, _nt(qt[:, sl], kt[:, sl]), 0.0).astype(BF16) for sl in lanes]
            o = o + per_head(lambda hh, sl: _nn(a[hh], vb[:, sl]))
            o_ref[rows, :] = o
            r = per_head(lambda hh, sl: jnp.broadcast_to(
                lax.rsqrt(jnp.mean(o[:, sl] * o[:, sl], axis=-1, keepdims=True) + EPS), (C, d)))
            ya_ref[rows, :] = (o * r * wv_all * (xg * _sigmoid(xg))).astype(BF16)
            return tuple(st * e_last[:, sl] + _tn(vb[:, sl], kd[:, sl]) for st, sl in zip(states, lanes))

        lax.fori_loop(0, nc, chunk, tuple(jnp.zeros((d, d), F32) for _ in range(G)))

    heads = pl.BlockSpec((T, G * d), lambda h: (0, h))
    return _call_with_side(
        body, "hg_fwd", (H // G,),
        _hg_in_specs(T) + [pl.BlockSpec((2, G * d), lambda h: (0, h)), pl.BlockSpec((1, d), lambda h: (0, 0))],
        [heads, heads, pl.BlockSpec((G, nc, d, d), lambda h: (h, 0, 0, 0))],
        [jax.ShapeDtypeStruct((T, H * d), BF16), jax.ShapeDtypeStruct((T, H * d), F32),
         jax.ShapeDtypeStruct((H, nc, d, d), F32)],
        [], ("parallel",), (z, z, z, z, lb_logits, hgw), side)


def _hg_bwd(z, o, dya, states, lb_logits, hgw, side=None):
    T = z.shape[0]
    H, d, C, G = HG_HEADS, HG_D, CHUNK, HG_GROUP
    nc = T // C
    scale = HG_D ** -0.5

    def body(hq_ref, hf_ref, hi_ref, hg_ref, o_ref, dy_ref, s_ref, lbl_ref, w_ref,
             dq_ref, df_ref, di_ref, dg_ref, dlbl_ref, dw_ref, acc_ref):
        lb_all = 1.0 / (1.0 + jnp.exp(lbl_ref[1:2, :] - lbl_ref[0:1, :]))
        wv = w_ref[...]
        row = lax.broadcasted_iota(jnp.int32, (C, C), 0)
        col = lax.broadcasted_iota(jnp.int32, (C, C), 1)
        tril = col <= row
        tri_incl = tril.astype(BF16)
        triu_incl = (col >= row).astype(BF16)
        rowi = lax.broadcasted_iota(jnp.int32, (C, G * d), 0)
        lanes = [slice(hh * d, (hh + 1) * d) for hh in range(G)]
        per_head = lambda fn: jnp.concatenate([fn(hh, sl) for hh, sl in enumerate(lanes)], axis=1)
        head_mean = lambda x: per_head(
            lambda hh, sl: jnp.broadcast_to(jnp.mean(x[:, sl], axis=-1, keepdims=True), (C, d)))
        wv_all = jnp.tile(wv, (1, G))
        lb = lb_all
        acc_ref[...] = jnp.zeros_like(acc_ref)

        @pl.when(pl.program_id(0) == 0)
        def _():
            dw_ref[...] = jnp.zeros_like(dw_ref)

        def chunk(i, carry):
            dsts, tail = carry
            c = nc - 1 - i
            rows = pl.ds(pl.multiple_of(c * C, C), C)
            xq, xf, v, xg = hq_ref[rows, :], hf_ref[rows, :], hi_ref[rows, :], hg_ref[rows, :]
            f, g, lg, kk, sq, q = _hg_gates(xq, xf, lb)
            b, b_last, b_mid = _hg_decays(lg, tri_incl, rowi)
            e_b, e_qm, e_km = jnp.exp(b), jnp.exp(b - b_mid), jnp.exp(jnp.minimum(b_mid - b, EXP_CLAMP))
            e_kl, e_last = jnp.exp(b_last - b), jnp.exp(b_last)
            ov, dy = o_ref[rows, :], dy_ref[rows, :]
            r = lax.rsqrt(head_mean(ov * ov) + EPS)
            xhat = ov * r
            sg = _sigmoid(xg)
            dxg = dy * xhat * wv_all * (sg * (1.0 + xg * (1.0 - sg)))
            dyn = dy * (xg * sg)
            acc_ref[0:1, :] += jnp.sum(dyn * xhat, axis=0, keepdims=True)
            dxh = dyn * wv_all
            dof = r * (dxh - xhat * head_mean(dxh * xhat))
            do, vb = dof.astype(BF16), v.astype(BF16)
            qe, kd, qt, kt = (q * e_b).astype(BF16), (kk * e_kl).astype(BF16), (q * e_qm).astype(BF16), (kk * e_km).astype(BF16)
            pm = [jnp.where(tril, _nt(do[:, sl], vb[:, sl]), 0.0).astype(BF16) for sl in lanes]
            am = [jnp.where(tril, _nt(qt[:, sl], kt[:, sl]), 0.0).astype(BF16) for sl in lanes]
            st = [_split2(s_ref[hh, c]) for hh in range(G)]
            ds = [_split2(x) for x in dsts]
            dq_state = per_head(lambda hh, sl: _nn(do[:, sl], st[hh][1]) + _nn(do[:, sl], st[hh][0]))
            dk_state = per_head(lambda hh, sl: _nn(vb[:, sl], ds[hh][1]) + _nn(vb[:, sl], ds[hh][0]))
            dq_intra = per_head(lambda hh, sl: _nn(pm[hh], kt[:, sl]))
            dk_intra = per_head(lambda hh, sl: _tn(pm[hh], qt[:, sl]))
            dv = per_head(lambda hh, sl: _tn(am[hh], do[:, sl]) + _nt(kd[:, sl], ds[hh][0]))
            new_dsts = tuple(x * e_last[:, sl] + _tn(do[:, sl], qe[:, sl]) for x, sl in zip(dsts, lanes))
            dq = dq_state * e_b + dq_intra * e_qm
            dk = dk_intra * e_km + dk_state * e_kl
            db = (qe.astype(F32) * dq_state + qt.astype(F32) * dq_intra
                  - kt.astype(F32) * dk_intra - kd.astype(F32) * dk_state)
            dlg = _tri_sum(triu_incl, db) + tail
            dgate = dlg / g - dk
            acc_ref[1:2, :] += jnp.sum(dgate * (1.0 - f), axis=0, keepdims=True)
            dq_ref[rows, :] = (dq * scale * (sq * (1.0 + xq * (1.0 - sq)))).astype(BF16)
            df_ref[rows, :] = (dgate * (1.0 - lb) * f * (1.0 - f)).astype(BF16)
            di_ref[rows, :] = dv.astype(BF16)
            dg_ref[rows, :] = dxg.astype(BF16)
            return new_dsts, tail + jnp.sum(db, axis=0, keepdims=True)

        lax.fori_loop(0, nc, chunk, (tuple(jnp.zeros((d, d), F32) for _ in range(G)), jnp.zeros((1, G * d), F32)))
        dw_ref[...] += functools.reduce(lambda p, q: p + q, [acc_ref[0:1, sl] for sl in lanes])
        dl0 = acc_ref[1:2, :] * lb_all * (1.0 - lb_all)
        dlbl_ref[0:1, :] = dl0
        dlbl_ref[1:2, :] = -dl0

    heads = pl.BlockSpec((T, G * d), lambda h: (0, h))
    logits = pl.BlockSpec((2, G * d), lambda h: (0, h))
    return _call_with_side(
        body, "hg_bwd", (H // G,),
        _hg_in_specs(T) + [heads, heads, pl.BlockSpec((G, nc, d, d), lambda h: (h, 0, 0, 0)), logits,
                           pl.BlockSpec((1, d), lambda h: (0, 0))],
        [heads, heads, heads, heads, logits, pl.BlockSpec((1, d), lambda h: (0, 0))],
        [jax.ShapeDtypeStruct((T, H * d), BF16)] * 4 + [jax.ShapeDtypeStruct((2, H * d), F32),
                                                        jax.ShapeDtypeStruct((1, d), F32)],
        [pltpu.VMEM((8, G * d), F32)], ("arbitrary",), (z, z, z, z, o, dya, states, lb_logits, hgw), side)


def _at_dims():
    pad = LEFT * CHUNK
    return pad, QB + pad, AT_HEADS * AT_DH // LANE, 4 * _hgw() // LANE


def _rel_of_period():
    pad, W, _, _ = _at_dims()
    n = jnp.arange(QB + W)
    return jnp.clip(pad - jnp.where(n < W, n, n - (QB + W)), -REL_CLIP, REL_CLIP) + REL_CLIP


def _bias_window(rel_bias):
    pad, W, _, _ = _at_dims()
    H, P = rel_bias.shape[0], QB + W
    per = rel_bias[:, _rel_of_period()]
    win = jnp.tile(per, (1, QB))[:, :QB * (P - 1)].reshape(H, QB, P - 1)[:, :, :W]
    t = jnp.arange(QB)[:, None]
    j = jnp.arange(W)[None, :]
    ok = (j // CHUNK >= t // CHUNK) & (j // CHUNK <= t // CHUNK + LEFT)
    return jnp.where(ok[None], win, NEG)


def _bias_window_grad(dbw):
    pad, W, _, _ = _at_dims()
    H, P = dbw.shape[0], QB + W
    flat = jnp.pad(dbw, ((0, 0), (0, 0), (0, P - 1 - W))).reshape(H, QB * (P - 1))
    per = jnp.pad(flat, ((0, 0), (0, QB))).reshape(H, QB, P).sum(axis=1)
    onehot = _rel_of_period()[:, None] == jnp.arange(2 * REL_CLIP + 1)[None, :]
    return jnp.dot(per, onehot.astype(F32), precision=HIGHEST)


def _at_stack(x):
    first = lax.broadcasted_iota(jnp.int32, x.shape, 1) < AT_DH
    return jnp.concatenate([jnp.where(first, x, 0.0), jnp.where(first, 0.0, x)], axis=0).astype(BF16)


def _at_unstack(x):
    first = lax.broadcasted_iota(jnp.int32, (QB, LANE), 1) < AT_DH
    return jnp.where(first, x[:QB], x[QB:])


def _at_softmax(qs, kw, bias_ref, qi):
    pad, W, _, _ = _at_dims()
    s = _nt(qs, kw) + bias_ref[...].reshape(2 * QB, W)
    valid = lax.broadcasted_iota(jnp.int32, (2 * QB, W), 1) + qi * QB >= pad
    s = jnp.where(valid, s, NEG)
    e = jnp.exp(s - jnp.max(s, axis=-1, keepdims=True))
    return e * (1.0 / jnp.sum(e, axis=-1, keepdims=True))


def _at_fwd(z, bias_win, side=None):
    T = z.shape[0]
    pad, W, HP, c0 = _at_dims()
    nq = T // QB

    G = 2
    assert HP % G == 0 and c0 % G == 0

    def body(q_ref, k_ref, v_ref, bias_ref, o_ref, kpad, vpad):
        qi = pl.program_id(1)

        @pl.when(qi == 0)
        def _():
            kpad[0:pad, :] = jnp.zeros((pad, G * LANE), BF16)
            vpad[0:pad, :] = jnp.zeros((pad, G * LANE), BF16)
            kpad[pad:, :] = k_ref[...].astype(BF16)
            vpad[pad:, :] = v_ref[...].astype(BF16)

        win = pl.ds(pl.multiple_of(qi * QB, QB), W)
        for g in range(G):
            lanes = slice(g * LANE, (g + 1) * LANE)
            p = _at_softmax(_at_stack(q_ref[:, lanes] * (AT_DH ** -0.5)), kpad[win, lanes],
                            bias_ref.at[pl.ds(2 * g, 2)], qi)
            o_ref[:, lanes] = _at_unstack(_nn(p.astype(BF16), vpad[win, lanes])).astype(BF16)

    full = lambda s: pl.BlockSpec((T, G * LANE), lambda hp, qi, s=s: (0, (c0 + s * HP) // G + hp))
    return _call_with_side(
        body, "at_fwd", (HP // G, nq),
        [pl.BlockSpec((QB, G * LANE), lambda hp, qi: (qi, c0 // G + hp)), full(1), full(2),
         pl.BlockSpec((2 * G, QB, W), lambda hp, qi: (hp, 0, 0))],
        [pl.BlockSpec((QB, G * LANE), lambda hp, qi: (qi, hp))],
        [jax.ShapeDtypeStruct((T, HP * LANE), BF16)],
        [pltpu.VMEM((T + pad, G * LANE), BF16)] * 2, ("parallel", "arbitrary"), (z, z, z, bias_win), side)


def _at_bwd(z, dyb, bias_win, side=None):
    T = z.shape[0]
    pad, W, HP, c0 = _at_dims()
    nq = T // QB
    scale = AT_DH ** -0.5

    def body(q_ref, k_ref, v_ref, do_ref, bias_ref, dq_ref, dk_ref, dv_ref, dbias_ref, kpad, vpad, dkpad, dvpad):
        qi = pl.program_id(1)

        @pl.when(qi == 0)
        def _():
            kpad[0:pad, :] = jnp.zeros((pad, LANE), BF16)
            vpad[0:pad, :] = jnp.zeros((pad, LANE), BF16)
            kpad[pad:, :] = k_ref[...].astype(BF16)
            vpad[pad:, :] = v_ref[...].astype(BF16)
            dkpad[...] = jnp.zeros_like(dkpad)
            dvpad[...] = jnp.zeros_like(dvpad)
            dbias_ref[...] = jnp.zeros_like(dbias_ref)

        win = pl.ds(pl.multiple_of(qi * QB, QB), W)
        kw, vw = kpad[win, :], vpad[win, :]
        qs, dos = _at_stack(q_ref[...] * scale), _at_stack(do_ref[...])
        p = _at_softmax(qs, kw, bias_ref, qi)
        dp = _nt(dos, vw)
        ds = p * (dp - jnp.sum(p * dp, axis=-1, keepdims=True))
        dbias_ref[...] += ds.reshape(2, QB, W)
        dsb = ds.astype(BF16)
        dq_ref[...] = (_at_unstack(_nn(dsb, kw)) * scale).astype(BF16)
        lanes_win = pl.ds(pl.multiple_of(qi * QB, QB), W)
        dkpad[:, lanes_win] += _tn(qs, dsb)
        dvpad[:, lanes_win] += _tn(dos, p.astype(BF16))

        @pl.when(qi == nq - 1)
        def _():
            dk_ref[...] = dkpad[:, pad:].T.astype(BF16)
            dv_ref[...] = dvpad[:, pad:].T.astype(BF16)

    full = lambda s: pl.BlockSpec((T, LANE), lambda hp, qi, s=s: (0, c0 + s * HP + hp))
    blk = pl.BlockSpec((QB, LANE), lambda hp, qi: (qi, hp))
    col = pl.BlockSpec((T, LANE), lambda hp, qi: (0, hp))
    bw = pl.BlockSpec((2, QB, W), lambda hp, qi: (hp, 0, 0))
    return _call_with_side(
        body, "at_bwd", (HP, nq),
        [pl.BlockSpec((QB, LANE), lambda hp, qi: (qi, c0 + hp)), full(1), full(2), blk, bw],
        [blk, col, col, bw],
        [jax.ShapeDtypeStruct((T, HP * LANE), BF16)] * 3 + [jax.ShapeDtypeStruct(bias_win.shape, F32)],
        [pltpu.VMEM((T + pad, LANE), BF16)] * 2 + [pltpu.VMEM((LANE, T + pad), F32)] * 2,
        ("parallel", "arbitrary"), (z, z, z, dyb, bias_win), side)


def _piece_tiles(name, full_shape):
    pr, pc = _piece_shape(name, full_shape)
    tr = min(ROW_TILE, pr)
    assert pr % tr == 0
    nt = pr // tr
    if name in ROW_SHARDED:
        return tr, nt, lambda q, half, i: ((2 * q + half) * nt + i, 0)
    return tr, nt, lambda q, half, i: (half * nt + i, q)


def _cast_into_full(name, wq, place, also_alone=False):
    full = _full_shape(name, wq.shape)
    pc = wq.shape[1]
    tr, nt, at = _piece_tiles(name, full)

    def body(place_ref, w_ref, *o_refs):
        for o_ref in o_refs:
            o_ref[...] = w_ref[...].astype(BF16)

    quarter = pl.BlockSpec((tr, pc), lambda h, i, s: (h * nt + i, 0))
    outs = _call_with_side(
        body, "cast_" + name, (2, nt), [quarter],
        [pl.BlockSpec((tr, pc), lambda h, i, s: at(s[0], h, i))] + [quarter] * also_alone,
        [jax.ShapeDtypeStruct(full, BF16)] + [jax.ShapeDtypeStruct(wq.shape, BF16)] * also_alone,
        [], ("parallel", "parallel"), (place, wq), None, n_prefetch=1)
    return tuple(outs) if also_alone else outs[0]


def _g_w_in_half(u1, dz, place, own, side=None):
    T, K = u1.shape
    N = dz.shape[1]
    hk, tn = K // 2, min(MM_TN, N)
    half = (lambda s: s[1]) if own else (lambda s: 1 - s[1])

    def body(place_ref, a_ref, b_ref, o_ref):
        o_ref[...] = _tn(a_ref[...], b_ref[...]).astype(BF16)

    outs = _call_with_side(
        body, "g_w_in_keep" if own else "g_w_in_send", (N // tn,),
        [pl.BlockSpec((T, hk), lambda j, s: (0, half(s))), pl.BlockSpec((T, tn), lambda j, s: (0, j))],
        [pl.BlockSpec((hk, tn), lambda j, s: (0, j))], [jax.ShapeDtypeStruct((hk, N), BF16)],
        [], ("parallel",), (place, u1, dz), side, n_prefetch=1)
    return outs[0] if side is None else outs


def _chip_sum(name, grad, theirs, place, kept_rows=False):
    pr, pc = theirs.shape[1:]
    tr, nt, at = _piece_tiles(name, (2 * grad.shape[0], grad.shape[1]) if kept_rows else grad.shape)
    if kept_rows:
        at = lambda q, half, i: (i, q)

    def body(place_ref, g_ref, t_ref, o_ref):
        o_ref[...] = (g_ref[...].astype(F32) + t_ref[...].astype(F32)).astype(BF16)

    piece = pl.BlockSpec((None, tr, pc), lambda q, i, s: (q, i, 0))
    return pl.pallas_call(
        body, name="chip_sum_" + name,
        grid_spec=pltpu.PrefetchScalarGridSpec(
            num_scalar_prefetch=1, grid=(4, nt),
            in_specs=[pl.BlockSpec((tr, pc), lambda q, i, s: at(q, s[1], i)), piece], out_specs=piece),
        out_shape=jax.ShapeDtypeStruct(theirs.shape, BF16),
        compiler_params=_cparams(("parallel", "parallel")),
    )(place, grad, theirs)


def _piece_sum(name, chip_sums, got, place):
    pr, pc = chip_sums.shape[1:]
    tr = min(ROW_TILE, pr)

    def body(place_ref, own_ref, got_ref, o_ref):
        o_ref[...] = (own_ref[...].astype(F32) + got_ref[0].astype(F32) + got_ref[1].astype(F32)
                      + got_ref[2].astype(F32))

    return pl.pallas_call(
        body, name="piece_sum_" + name,
        grid_spec=pltpu.PrefetchScalarGridSpec(
            num_scalar_prefetch=1, grid=(pr // tr,),
            in_specs=[pl.BlockSpec((None, tr, pc), lambda i, s: (s[0], i, 0)),
                      pl.BlockSpec((3, tr, pc), lambda i, s: (0, i, 0))],
            out_specs=pl.BlockSpec((tr, pc), lambda i, s: (i, 0))),
        out_shape=jax.ShapeDtypeStruct((pr, pc), F32),
        compiler_params=_cparams(("parallel",)),
    )(place, chip_sums, got)


def _adam_quarter(name, w, m, v, g_mine, g_sib, place, side=None):
    pr, pc = g_mine.shape
    tr = min(ROW_TILE // 2, pr)
    nt = pr // tr

    def body(place_ref, w_ref, m_ref, v_ref, gm_ref, gs_ref, go_ref, d_ref, mo_ref, vo_ref):
        g = jnp.where(pl.program_id(0) == place_ref[1], gm_ref[...], gs_ref[...])
        delta, mn, vn = _adam_math(w_ref[...], g, m_ref[...], v_ref[...])
        go_ref[...] = g
        d_ref[...] = delta
        mo_ref[...] = mn
        vo_ref[...] = vn

    quarter = pl.BlockSpec((tr, pc), lambda h, i, s: (h * nt + i, 0))
    mine = pl.BlockSpec((tr, pc), lambda h, i, s: (jnp.where(h == s[1], i, 0), 0))
    sib = pl.BlockSpec((tr, pc), lambda h, i, s: (jnp.where(h == s[1], 0, i), 0))
    return _call_with_side(
        body, "adam_" + name, (2, nt), [quarter, quarter, quarter, mine, sib], [quarter] * 4,
        [jax.ShapeDtypeStruct(w.shape, F32)] * 4, [], ("parallel", "parallel"),
        (place, w, m, v, g_mine, g_sib), side, n_prefetch=1)


def _adam_half(name, w, m, v, g, place, own, prev=None):
    pr, pc = g.shape
    tr = min(ROW_TILE // 2, pr)
    nt = pr // tr
    half = (lambda s: s[1]) if own else (lambda s: 1 - s[1])

    def body(place_ref, w_ref, m_ref, v_ref, g_ref, *rest):
        go_ref, d_ref, mo_ref, vo_ref = rest[-4:]
        gv = g_ref[...]
        delta, mn, vn = _adam_math(w_ref[...], gv, m_ref[...], v_ref[...])
        go_ref[...] = gv
        d_ref[...] = delta
        mo_ref[...] = mn
        vo_ref[...] = vn

    quarter = pl.BlockSpec((tr, pc), lambda i, s: (half(s) * nt + i, 0))
    n_prev = 0 if prev is None else 4
    return _call_with_side(
        body, "adam_%s_%s" % (name, "own" if own else "sibling"), (nt,),
        [quarter, quarter, quarter, pl.BlockSpec((tr, pc), lambda i, s: (i, 0))] + [ANY] * n_prev, [quarter] * 4,
        [jax.ShapeDtypeStruct(w.shape, F32)] * 4, [], ("parallel",),
        (place, w, m, v, g) + tuple(prev or ()), None, n_prefetch=1,
        aliases={4 + t: t for t in range(n_prev)})


def _adam_math(w, g, m, v):
    m = ADAM_B1 * m + (1.0 - ADAM_B1) * g
    v = ADAM_B2 * v + (1.0 - ADAM_B2) * (g * g)
    m_hat = m / (1.0 - ADAM_B1 ** ADAM_STEP)
    v_hat = v / (1.0 - ADAM_B2 ** ADAM_STEP)
    return -ADAM_LR * (m_hat / (jnp.sqrt(v_hat) + ADAM_EPS) + ADAM_WD * w), m, v


WEIGHTS = ("w_in", "w_branch_a", "w_branch_b", "w_out", "w_up", "w_down")
ROW_SHARDED = ("w_out", "w_down")
ANY = pl.BlockSpec(memory_space=pl.ANY)
MESH = pl.DeviceIdType.MESH


def _place():
    x, y, c = lax.axis_index("x"), lax.axis_index("y"), lax.axis_index("c")
    chips = [(1 - x, y), (x, 1 - y), (1 - x, 1 - y)]
    return x, y, c, 2 * x + y, chips, [2 * cx + cy for cx, cy in chips]


def _piece(full_ref, name, q, half):
    K, N = full_ref.shape
    if name in ROW_SHARDED:
        rows = K // 8
        return full_ref.at[pl.ds(q * (2 * rows) + half * rows, rows), :]
    return full_ref.at[pl.ds(half * (K // 2), K // 2), pl.ds(q * (N // 4), N // 4)]


def _piece_shape(name, full_shape):
    K, N = full_shape
    return (K // 8, N) if name in ROW_SHARDED else (K // 2, N // 4)


def _full_shape(name, quarter_shape):
    Kq, Nq = quarter_shape
    return (4 * Kq, Nq) if name in ROW_SHARDED else (Kq, 4 * Nq)


def _remote(src, dst, send_sem, recv_sem, device):
    return pltpu.make_async_remote_copy(src_ref=src, dst_ref=dst, send_sem=send_sem, recv_sem=recv_sem,
                                        device_id=device, device_id_type=MESH)


def _z_part(u1, w_in, z_prev, place, k0, count, side=None, own_quarter=False):
    T, K = u1.shape
    nq = w_in.shape[1] if own_quarter else w_in.shape[1] // 4
    N = 4 * nq
    tn = nq // 2 if (nq // 2) % LANE == 0 else nq
    tm = min(MM_TM, T)
    per = nq // tn
    col = lambda g, j, s: (s[0] ^ (k0 + g)) * per + j
    ins = [pl.BlockSpec((tm, K), lambda g, i, j, s: (i, 0)),
           pl.BlockSpec((K, tn), (lambda g, i, j, s: (0, j)) if own_quarter else (lambda g, i, j, s: (0, col(g, j, s))))]
    operands = [place, u1, w_in]
    if z_prev is not None:
        ins.append(ANY)
        operands.append(z_prev)

    def body(place_ref, a_ref, b_ref, *rest):
        rest[-1][...] = _nn(a_ref[...], b_ref[...].astype(BF16))

    return _call_with_side(
        body, "z_part_%d" % k0, (count, T // tm, per), ins,
        [pl.BlockSpec((tm, tn), lambda g, i, j, s: (i, col(g, j, s)))], [jax.ShapeDtypeStruct((T, N), F32)],
        [], ("parallel",) * 3, tuple(operands), side, n_prefetch=1, aliases={} if z_prev is None else {2: 0},
        borrow={0: 1} if side is not None and side.aliased and side.aliased[0] is w_in else None)


def _rows(ref, span):
    return ref if span is None else ref.at[pl.ds(span[0], span[1]), :]


def _gather_moves(items):
    count = {"near": lambda arg: 2, "far": lambda arg: 1, "pass": len}

    def build(reads, aliased, fresh, send_sems, recv_sems, off=0):
        x, y, c, p, chips, chip_ids = _place()
        south = c == 0
        far_src = jnp.where(south, chip_ids[0], chip_ids[1])
        far_dst = (jnp.where(south, x, 1 - x), jnp.where(south, 1 - y, y), c)
        out = []

        def add(ref, device):
            k = off + len(out)
            out.append(_remote(ref, ref, send_sems.at[k], recv_sems.at[k], device))

        for (name, _, moves), ref in zip(items, aliased):
            for kind, arg in moves:
                if kind == "near":
                    for chip in chips[:2]:
                        add(_rows(_piece(ref, name, p, c), arg), (*chip, c))
                elif kind == "far":
                    add(_rows(_piece(ref, name, far_src, c), arg), far_dst)
                else:
                    for j in arg:
                        add(_piece(ref, name, chip_ids[j], c), (x, y, 1 - c))
        return out

    nsem = sum(count[kind](arg) for _, _, moves in items for kind, arg in moves)
    return _Side(build, nsem, aliased=[a for _, a, _ in items])


def _ici_far(names, fulls, rows=None):
    return _gather_moves([(n, a, [("far", r)]) for n, a, r in zip(names, fulls, rows or [None] * len(names))])


def _d2d_gather(names, fulls, which=(0, 1, 2)):
    return _gather_moves([(n, a, [("pass", which)]) for n, a in zip(names, fulls)])


def _sib_send(names, grads):
    def build(reads, aliased, fresh, send_sems, recv_sems, off=0):
        x, y, c, _, _, _ = _place()
        out = []
        for i, name in enumerate(names):
            for q in range(4):
                k = off + 4 * i + q
                out.append(_remote(_piece(reads[i], name, q, 1 - c), fresh[i].at[q], send_sems.at[k], recv_sems.at[k],
                                   (x, y, 1 - c)))
        return out

    shapes = [jax.ShapeDtypeStruct((4,) + _piece_shape(name, g.shape), BF16) for name, g in zip(names, grads)]
    return _Side(build, 4 * len(names), reads=grads, fresh=shapes)


def _sib_send_half(sent):
    K2, N = sent.shape

    def build(reads, aliased, fresh, send_sems, recv_sems, off=0):
        x, y, c, _, _, _ = _place()
        return [_remote(reads[0].at[:, pl.ds(q * (N // 4), N // 4)], fresh[0].at[q], send_sems.at[off + q],
                        recv_sems.at[off + q], (x, y, 1 - c)) for q in range(4)]

    return _Side(build, 4, reads=[sent], fresh=[jax.ShapeDtypeStruct((4, K2, N // 4), BF16)])


HBM = pl.BlockSpec(memory_space=pltpu.HBM)
SEM = pl.BlockSpec(memory_space=pltpu.SEMAPHORE)


def _exchange_copies(s_refs, land_refs, send_sems, recv_sems):
    _, _, c, _, chips, chip_ids = _place()
    return [_remote(s_ref.at[cid], land_ref.at[j], send_sems.at[3 * i + j], recv_sems.at[3 * i + j], (*chip, c))
            for i, (s_ref, land_ref) in enumerate(zip(s_refs, land_refs))
            for j, (chip, cid) in enumerate(zip(chips, chip_ids))]


def _exchange_start(name, chip_sums):
    n = len(chip_sums)

    def body(*refs):
        for cp in _exchange_copies(refs[:n], refs[n:2 * n], refs[2 * n], refs[2 * n + 1]):
            cp.start()
        refs[-1][...] = jnp.zeros_like(refs[-1])

    lands = [jax.ShapeDtypeStruct((3,) + s.shape[1:], s.dtype) for s in chip_sums]
    hbm = lambda a: pltpu.with_memory_space_constraint(a, pltpu.HBM)
    outs = pl.pallas_call(
        body, name="exchange_start_" + name,
        out_shape=(pltpu.SemaphoreType.DMA((3 * n,)), pltpu.SemaphoreType.DMA((3 * n,)),
                   *[pltpu.HBM(a.shape, a.dtype) for a in chip_sums + lands], jax.ShapeDtypeStruct((8, LANE), F32)),
        in_specs=(HBM,) * (2 * n), out_specs=(SEM, SEM) + (HBM,) * (2 * n) + (pl.BlockSpec(memory_space=pltpu.VMEM),),
        input_output_aliases={i: 2 + i for i in range(2 * n)},
        compiler_params=pltpu.CompilerParams(has_side_effects=pltpu.SideEffectType.DATAFLOW_SIDE_EFFECTING),
    )(*[hbm(s) for s in chip_sums], *[hbm(lax.empty(a.shape, a.dtype)) for a in lands])
    return outs[0], outs[1], list(outs[2:2 + n]), list(outs[2 + n:2 + 2 * n]), outs[-1]


def _exchange_wait(name, flight, after):
    send_sems, recv_sems, s_thru, land_thru, _ = flight
    n = len(s_thru)

    def body(*refs):
        for cp in _exchange_copies(refs[:n], refs[n:2 * n], refs[2 * n], refs[2 * n + 1]):
            cp.wait_send()
            cp.wait_recv()

    outs = pl.pallas_call(
        body, name="exchange_wait_" + name,
        out_shape=tuple(pltpu.HBM(a.shape, a.dtype) for a in s_thru + land_thru),
        in_specs=(HBM,) * (2 * n) + (SEM, SEM, ANY), out_specs=(HBM,) * (2 * n),
        input_output_aliases={i: i for i in range(2 * n)},
        compiler_params=pltpu.CompilerParams(has_side_effects=pltpu.SideEffectType.DATAFLOW_SIDE_EFFECTING),
    )(*s_thru, *land_thru, send_sems, recv_sems, after)
    return list(outs[:n]), list(outs[n:])


def _move_copies(kind, name, f_ref, send_sems, recv_sems):
    return _gather_moves([(name, None, [(kind, None)])]).build([], [f_ref], [], send_sems, recv_sems)


def _move_start(kind, name, full):
    def body(f_ref, send_sems, recv_sems, f_thru, token):
        for cp in _move_copies(kind, name, f_ref, send_sems, recv_sems):
            cp.start()
        token[...] = jnp.zeros_like(token)

    return pl.pallas_call(
        body, name=kind + "_start_" + name,
        out_shape=(pltpu.SemaphoreType.DMA((2,)), pltpu.SemaphoreType.DMA((2,)), pltpu.HBM(full.shape, full.dtype),
                   jax.ShapeDtypeStruct((8, LANE), F32)),
        in_specs=(HBM,), out_specs=(SEM, SEM, HBM, pl.BlockSpec(memory_space=pltpu.VMEM)),
        input_output_aliases={0: 2},
        compiler_params=pltpu.CompilerParams(has_side_effects=pltpu.SideEffectType.DATAFLOW_SIDE_EFFECTING),
    )(pltpu.with_memory_space_constraint(full, pltpu.HBM))


def _move_wait(kind, name, flight, after):
    send_sems, recv_sems, f_thru, _ = flight

    def body(f_ref, send_sems, recv_sems, after_ref, f_out):
        for cp in _move_copies(kind, name, f_ref, send_sems, recv_sems):
            cp.wait_send()
            cp.wait_recv()

    return pl.pallas_call(
        body, name=kind + "_wait_" + name, out_shape=pltpu.HBM(f_thru.shape, f_thru.dtype),
        in_specs=(HBM, SEM, SEM, ANY), out_specs=HBM, input_output_aliases={0: 0},
        compiler_params=pltpu.CompilerParams(has_side_effects=pltpu.SideEffectType.DATAFLOW_SIDE_EFFECTING),
    )(f_thru, send_sems, recv_sems, after)


def _share_copy(h_ref, land_ref, send_sems, recv_sems):
    x, y, c, _, _, _ = _place()
    return _remote(h_ref, land_ref, send_sems.at[0], recv_sems.at[0], (x, y, 1 - c))


def _share_start(name, half):
    def body(h_ref, land_ref, send_sems, recv_sems, h_thru, land_thru, token):
        _share_copy(h_ref, land_ref, send_sems, recv_sems).start()
        token[...] = jnp.zeros_like(token)

    hbm = lambda a: pltpu.with_memory_space_constraint(a, pltpu.HBM)
    return pl.pallas_call(
        body, name="share_start_" + name,
        out_shape=(pltpu.SemaphoreType.DMA((1,)), pltpu.SemaphoreType.DMA((1,)), pltpu.HBM(half.shape, half.dtype),
                   pltpu.HBM(half.shape, half.dtype), jax.ShapeDtypeStruct((8, LANE), F32)),
        in_specs=(HBM, HBM), out_specs=(SEM, SEM, HBM, HBM, pl.BlockSpec(memory_space=pltpu.VMEM)),
        input_output_aliases={0: 2, 1: 3},
        compiler_params=pltpu.CompilerParams(has_side_effects=pltpu.SideEffectType.DATAFLOW_SIDE_EFFECTING),
    )(hbm(half), hbm(lax.empty(half.shape, half.dtype)))


def _share_wait(name, flight, after):
    send_sems, recv_sems, h_thru, land_thru, _ = flight

    def body(h_ref, land_ref, send_sems, recv_sems, after_ref, h_out, land_out):
        cp = _share_copy(h_ref, land_ref, send_sems, recv_sems)
        cp.wait_send()
        cp.wait_recv()

    return pl.pallas_call(
        body, name="share_wait_" + name,
        out_shape=(pltpu.HBM(h_thru.shape, h_thru.dtype), pltpu.HBM(land_thru.shape, land_thru.dtype)),
        in_specs=(HBM, HBM, SEM, SEM, ANY), out_specs=(HBM, HBM), input_output_aliases={0: 0, 1: 1},
        compiler_params=pltpu.CompilerParams(has_side_effects=pltpu.SideEffectType.DATAFLOW_SIDE_EFFECTING),
    )(h_thru, land_thru, send_sems, recv_sems, after)[1]


def _sib_share(halves):
    def build(reads, aliased, fresh, send_sems, recv_sems, off=0):
        x, y, c, _, _, _ = _place()
        return [_remote(reads[i], fresh[i], send_sems.at[off + i], recv_sems.at[off + i], (x, y, 1 - c))
                for i in range(len(halves))]

    return _Side(build, len(halves), reads=halves, fresh=[jax.ShapeDtypeStruct(h.shape, F32) for h in halves])


def _run_side(name, side):
    nr, na = len(side.reads), len(side.aliased)

    def body(*refs):
        n_in, n_out = nr + na, na + len(side.fresh)
        outs = refs[n_in:n_in + n_out]
        copies = side.build(refs[:nr], outs[:na], outs[na:], *refs[-2:])
        for cp in copies:
            cp.start()
        for cp in copies:
            cp.wait()

    return pl.pallas_call(
        body, name=name, in_specs=side.in_specs(), out_specs=side.out_specs(), out_shape=side.out_shape(),
        input_output_aliases=side.aliases(0, 0), scratch_shapes=side.scratch(),
    )(*side.operands())


def _small_allreduce_adam(gpart, w, m, v, after):
    R = gpart.shape[0]

    def body(g_ref, w_ref, m_ref, v_ref, after_ref, go_ref, d_ref, mo_ref, vo_ref, buf, send_sems, recv_sems):
        x, y, c = lax.axis_index("x"), lax.axis_index("y"), lax.axis_index("c")
        me = 4 * x + 2 * y + c
        buf[me] = g_ref[...]
        copies = []
        for k in range(1, 8):
            fx, fy, fc = (k >> 2) & 1, (k >> 1) & 1, k & 1
            peer = (1 - x if fx else x, 1 - y if fy else y, 1 - c if fc else c)
            cp = _remote(g_ref, buf.at[me], send_sems.at[k - 1], recv_sems.at[k - 1], peer)
            cp.start()
            copies.append((cp, 4 * peer[0] + 2 * peer[1] + peer[2]))
        for k, (cp, pid) in enumerate(copies):
            _remote(g_ref, buf.at[pid], send_sems.at[k], recv_sems.at[k], (x, y, c)).wait_recv()
        for cp, _ in copies:
            cp.wait_send()
        g = buf[0]
        for d in range(1, 8):
            g = g + buf[d]
        delta, mn, vn = _adam_math(w_ref[...], g, m_ref[...], v_ref[...])
        go_ref[...] = g
        d_ref[...] = delta
        mo_ref[...] = mn
        vo_ref[...] = vn

    vm = pl.BlockSpec(memory_space=pltpu.VMEM)
    return pl.pallas_call(
        body, name="small_allreduce_adam",
        in_specs=[vm] * 4 + [ANY], out_specs=[vm] * 4,
        out_shape=[jax.ShapeDtypeStruct((R, LANE), F32)] * 4,
        scratch_shapes=[pltpu.VMEM((8, R, LANE), F32), pltpu.SemaphoreType.DMA((7,)), pltpu.SemaphoreType.DMA((7,))],
    )(gpart, w, m, v, after)


def _pack(arrs):
    flat = jnp.concatenate([a.reshape(-1).astype(F32) for a in arrs])
    rows = -(-flat.shape[0] // (8 * LANE)) * 8
    return jnp.pad(flat, (0, rows * LANE - flat.shape[0])).reshape(rows, LANE)


def _unpack(packed, like):
    flat, out, off = packed.reshape(-1), [], 0
    for a in like:
        out.append(flat[off:off + a.size].reshape(a.shape))
        off += a.size
    return out


def kernel(x, w_in, lb_logits, hg_norm_w, rel_bias, w_branch_a, w_branch_b, w_out, norm_mix_w, norm_mlp_w, w_up, w_down, norm_final_w, loss_target, m_w_in, m_lb_logits, m_hg_norm_w, m_rel_bias, m_w_branch_a, m_w_branch_b, m_w_out, m_norm_mix_w, m_norm_mlp_w, m_w_up, m_w_down, m_norm_final_w, v_w_in, v_lb_logits, v_hg_norm_w, v_rel_bias, v_w_branch_a, v_w_branch_b, v_w_out, v_norm_mix_w, v_norm_mlp_w, v_w_up, v_w_down, v_norm_final_w):
    T, D = x.shape[1], x.shape[2]
    x2, tgt = x.reshape(T, D), loss_target.reshape(T, D)
    big = dict(w_in=(w_in, m_w_in, v_w_in), w_branch_a=(w_branch_a, m_w_branch_a, v_w_branch_a),
               w_branch_b=(w_branch_b, m_w_branch_b, v_w_branch_b), w_out=(w_out, m_w_out, v_w_out),
               w_up=(w_up, m_w_up, v_w_up), w_down=(w_down, m_w_down, v_w_down))
    big = {k: tuple(a[0] for a in v) for k, v in big.items()}
    nfw = norm_final_w.reshape(1, D)

    place = jnp.stack([2 * lax.axis_index("x") + lax.axis_index("y"), lax.axis_index("c")]).astype(jnp.int32)
    small3 = ["w_branch_a", "w_branch_b", "w_out"]

    def span(name, lo, hi):
        pr = big[name][0].shape[0] // 2
        return (pr * lo // 16, pr * (hi - lo) // 16)

    w_in_full, w_in_own = _cast_into_full("w_in", big["w_in"][0], place, also_alone=True)
    flight_in = _move_start("near", "w_in", w_in_full)
    Wf = {name: _cast_into_full(name, big[name][0], place) for name in WEIGHTS if name != "w_in"}

    u1 = _rms_fwd("norm_mix", x2, norm_mix_w, side=_after(flight_in[-1]))
    z = _z_part(u1[0], w_in_own, None, place, 0, 1, own_quarter=True)[0]
    u1 = u1[0]
    Wf["w_in"] = _move_wait("near", "w_in", flight_in, z)

    def carried(**moves):
        def arg(n, k, a):
            if k == "pass":
                return a[0] if a else (0, 1, 2)
            return span(n, *a) if a else None

        return _gather_moves([(n, Wf[n], [(k, arg(n, k, a)) for k, *a in ms]) for n, ms in moves.items()]), list(moves)

    def land(names, outs):
        Wf.update(zip(names, outs[-len(names):]))
        return outs[:-len(names)]

    side, names = carried(w_in=[("pass", (0, 1))], w_out=[("near", 0, 8)])
    land(names, _run_side("pass_w_in_near", side))
    flight_in = _move_start("far", "w_in", Wf["w_in"])
    z = _z_part(u1, flight_in[2], z, place, 1, 2)[0]
    Wf["w_in"] = _move_wait("far", "w_in", flight_in, z)
    side, names = carried(w_in=[("pass", (2,))], w_branch_a=[("near",)], w_branch_b=[("near",)])
    land(names, _run_side("pass_w_in_far", side))
    side, names = carried(w_branch_a=[("far",)], w_branch_b=[("far",)], w_out=[("near", 8, 16)])
    (z,) = land(names, _z_part(u1, Wf["w_in"], z, place, 3, 1, side=side))
    side, names = carried(w_branch_a=[("pass",)], w_branch_b=[("pass",)], w_up=[("near", 0, 10)])
    ya, o_hg, states = land(names, _hg_fwd(z, lb_logits, hg_norm_w, side=side))
    bias_win = _bias_window(rel_bias[0])
    side, names = carried(w_out=[("far",)], w_up=[("near", 10, 16), ("far", 0, 10)], w_down=[("near", 0, 3)])
    (yb,) = land(names, _at_fwd(z, bias_win, side=side))
    side, names = carried(w_out=[("pass",)], w_up=[("far", 10, 14)])
    (pa,) = land(names, _mm("branch_a", ya, Wf["w_branch_a"], "nn", [BF16], side=side))
    side, names = carried(w_up=[("far", 14, 16)], w_down=[("near", 3, 4)])
    (pb,) = land(names, _mm("branch_b", yb, Wf["w_branch_b"], "nn", [BF16], side=side))
    side, names = carried(w_down=[("near", 4, 8)])
    (merged,) = land(names, _merge(z, pa, pb, side=side))
    add = lambda acc, res: (acc + res,)
    side, names = carried(w_up=[("pass",)], w_down=[("near", 8, 12)])
    (h1,) = land(names, _mm("out_proj", merged, Wf["w_out"], "nn", [F32], extras=[x2], epilogue=add, side=side))
    side, names = carried(w_down=[("near", 12, 14)])
    (u2,) = land(names, _rms_fwd("norm_mlp", h1, norm_mlp_w, side=side))
    relu2 = lambda acc: (acc, jnp.square(jnp.maximum(acc, 0.0)))
    side, names = carried(w_down=[("near", 14, 16), ("far", 0, 14)])
    a_pre, act = land(names, _mm("mlp_up", u2, Wf["w_up"], "nn", [F32, BF16], epilogue=relu2, side=side))
    (Wf["w_down"],) = _run_side("far_w_down", _ici_far(["w_down"], [Wf["w_down"]], rows=[span("w_down", 14, 16)]))
    (Wf["w_down"],) = _run_side("pass_w_down", _d2d_gather(["w_down"], [Wf["w_down"]]))
    h2 = _mm("mlp_down", act, Wf["w_down"], "nn", [F32], extras=[h1], epilogue=add)
    loss_part, dh2, dh2b, d_nf = _loss_head(h2, tgt, nfw)

    drelu2 = lambda acc, a: (acc * (2.0 * jnp.maximum(a, 0.0)),)
    da = _mm("d_act", dh2b, Wf["w_down"], "nt", [BF16], extras=[a_pre], epilogue=drelu2)
    G = {}
    G["w_down"] = _mm("g_w_down", act, dh2b, "tn", [BF16])
    G["w_up"] = _mm("g_w_up", u2, da, "tn", [BF16])
    T_, S_, GOT = {}, {}, {}
    du2, T_["w_down"], T_["w_up"] = _mm("d_u2", da, Wf["w_up"], "nt", [F32],
                                        side=_sib_send(["w_down", "w_up"], [G["w_down"], G["w_up"]]))
    mlp2 = ["w_down", "w_up"]
    flight_mlp = _exchange_start("mlp", [_chip_sum(n, G[n], T_[n], place) for n in mlp2])
    dh1, dh1b, d_nmlp = _rms_bwd("norm_mlp_bwd", du2, h1, norm_mlp_w, dh2, side=_after(flight_mlp[-1]))
    dmerged = _mm("d_merged", dh1b, Wf["w_out"], "nt", [F32])
    G["w_out"] = _mm("g_w_out", merged, dh1b, "tn", [BF16])
    dpa, dpb, dz_ga, dz_gb = _dmerge(dmerged, z, pa, pb)
    dya = _mm("d_ya", dpa, Wf["w_branch_a"], "nt", [F32])
    dyb = _mm("d_yb", dpb, Wf["w_branch_b"], "nt", [F32])
    G["w_branch_a"] = _mm("g_w_a", ya, dpa, "tn", [BF16])
    G["w_branch_b"] = _mm("g_w_b", yb, dpb, "tn", [BF16])
    dz_q, dz_f, dz_i, dz_g, d_lbl, d_hgw, *sent = _hg_bwd(
        z, o_hg, dya, states, lb_logits, hg_norm_w, side=_sib_send(small3, [G[n] for n in small3]))
    flight_small = _exchange_start("small", [_chip_sum(n, G[n], t, place) for n, t in zip(small3, sent)])
    dz_aq, dz_ak, dz_av, dbias_win = _at_bwd(z, dyb, bias_win, side=_after(flight_small[-1]))
    dz = jnp.concatenate([dz_q, dz_f, dz_i, dz_g, dz_aq, dz_ak, dz_av, dz_ga, dz_gb], axis=1)
    g_send = _g_w_in_half(u1, dz, place, False)
    g_keep, T_["w_in"] = _g_w_in_half(u1, dz, place, True, side=_sib_send_half(g_send))
    for names, flight in ((mlp2, flight_mlp), (small3, flight_small)):
        sums, got = _exchange_wait("_".join(names), flight, g_keep)
        S_.update(zip(names, sums))
        GOT.update(zip(names, got))
    S_["w_in"] = _chip_sum("w_in", g_keep, T_["w_in"], place, kept_rows=True)
    early = [n for n in WEIGHTS if n != "w_in"]
    H_ = {n: _piece_sum(n, S_[n], GOT[n], place) for n in early}
    flight = _exchange_start("w_in", [S_["w_in"]])
    share_early = _sib_share([H_[n] for n in early])
    share_early.reads.append(flight[-1])
    du1, *shared = _mm("d_u1", dz, Wf["w_in"], "nt", [F32], side=share_early)
    O_ = dict(zip(early, shared))
    grad_x, _, d_nmix = _rms_bwd("norm_mix_bwd", du1, x2, norm_mix_w, dh1)
    d_rel = _bias_window_grad(dbias_win)
    big_out = {}
    for name in early:
        outs = _adam_quarter(name, *big[name], H_[name], O_[name], place)
        big_out[name] = tuple(a[None] for a in outs)
    (S_["w_in"],), (got_in,) = _exchange_wait("w_in", flight, outs[1])
    H_["w_in"] = _piece_sum("w_in", S_["w_in"], got_in, place)
    sharing = _share_start("w_in", H_["w_in"])
    first = _adam_half("w_in", *big["w_in"], sharing[2], place, True)
    outs = _adam_half("w_in", *big["w_in"], _share_wait("w_in", sharing, first[0]), place, False, prev=first)
    big_out["w_in"] = tuple(a[None] for a in outs)

    smalls = [("lb_logits", lb_logits, m_lb_logits, v_lb_logits, d_lbl),
              ("hg_norm_w", hg_norm_w, m_hg_norm_w, v_hg_norm_w, d_hgw),
              ("rel_bias", rel_bias, m_rel_bias, v_rel_bias, d_rel),
              ("norm_mix_w", norm_mix_w, m_norm_mix_w, v_norm_mix_w, d_nmix),
              ("norm_mlp_w", norm_mlp_w, m_norm_mlp_w, v_norm_mlp_w, d_nmlp),
              ("norm_final_w", norm_final_w, m_norm_final_w, v_norm_final_w, d_nf)]
    one = jnp.zeros((1,), F32)
    like = [s[1] for s in smalls] + [one]
    packed = _small_allreduce_adam(_pack([s[4] for s in smalls] + [loss_part[0, :1]]), _pack(like),
                                   _pack([s[2] for s in smalls] + [one]), _pack([s[3] for s in smalls] + [one]), got_in)
    unpacked = [_unpack(p, like) for p in packed]
    small_out = {s[0]: vals for s, vals in zip(smalls, zip(*unpacked))}
    loss = unpacked[0][-1].reshape(())
    order = ["w_in", "lb_logits", "hg_norm_w", "rel_bias", "w_branch_a", "w_branch_b", "w_out", "norm_mix_w",
             "norm_mlp_w", "w_up", "w_down", "norm_final_w"]
    res = {**big_out, **small_out}
    return (loss, grad_x.reshape(x.shape), *[res[n][0] for n in order], *[res[n][1] for n in order],
            *[res[n][2] for n in order], *[res[n][3] for n in order])
```
